```python
import math
import jax
import jax.numpy as jnp
from jax import lax
import numpy as np

D_MODEL = 1024
BATCH = 2
SEQ = 8192
DEPTH = 2

GRID_W = 64
CTX_LEN = 256

DA_HEADS = 4
DA_DIM = 64
DA_VDIM = 2 * DA_DIM
DA_WIDTH = DA_HEADS * DA_VDIM
QUERY_BLOCK = 128
ROPE_BASE = 10000.0
POOL_WINDOWS = (2, 4, 8, 16)
POOL_GROUP = 64
POOL_WIDTH = len(POOL_WINDOWS) * POOL_GROUP
RET_HEADS = 4
RET_DK = 64
RET_DV = 64
RET_WIDTH = RET_HEADS * RET_DV
RET_CHUNK = 128
MIX_WIDTH = DA_WIDTH + POOL_WIDTH + RET_WIDTH
IN_SIZES = (DA_HEADS * 2 * DA_DIM, DA_HEADS * 2 * DA_DIM, DA_WIDTH,
            POOL_WIDTH,
            RET_HEADS * RET_DK, RET_HEADS * RET_DK, RET_WIDTH, RET_WIDTH)
IN_WIDTH = sum(IN_SIZES)
IN_OFFSETS = tuple(int(v) for v in np.cumsum(IN_SIZES)[:-1])
N_EXPERTS = 256
TOP_K = 8
N_GROUPS = 8
TOPK_GROUPS = 4
EXPERT_HIDDEN = 256
SHARED_HIDDEN = 256
ROUTED_SCALE = 2.5
MOE_BLOCK = 128
DN_ALPHA = (2.0 * DEPTH) ** 0.25
DN_BETA = (8.0 * DEPTH) ** -0.25
LN_EPS = 1e-6
RMS_EPS = 1e-5

kernel_name = 'hybrid_diffattn_pool_retention_moe_dit'


def layer_norm(x, gain=None, bias=None):
    xf = x.astype(jnp.float32)
    mu = jnp.mean(xf, axis=-1, keepdims=True)
    var = jnp.mean(jnp.square(xf - mu), axis=-1, keepdims=True)
    y = (xf - mu) * lax.rsqrt(var + LN_EPS)
    if gain is not None:
        y = y * gain.astype(jnp.float32) + bias.astype(jnp.float32)
    return y.astype(x.dtype)


def rms_norm(x):
    xf = x.astype(jnp.float32)
    return (xf * lax.rsqrt(jnp.mean(xf * xf, axis=-1, keepdims=True) + RMS_EPS)).astype(x.dtype)


def modulate(x, shift, scale):
    return layer_norm(x) * (1.0 + scale) + shift


def swiglu(x, w_gate_up, w_down):
    g, u = jnp.split(x @ w_gate_up, 2, axis=-1)
    return (jax.nn.silu(g) * u) @ w_down


def axial_rope_tables(n):
    rows = n // GRID_W
    row = jnp.repeat(jnp.arange(rows, dtype=jnp.float32), GRID_W)
    col = jnp.tile(jnp.arange(GRID_W, dtype=jnp.float32), rows)
    nf = DA_DIM // 4
    freqs = ROPE_BASE ** (-jnp.arange(nf, dtype=jnp.float32) / nf)
    ang = jnp.stack([row[:, None] * freqs, col[:, None] * freqs], axis=1)
    return jnp.cos(ang), jnp.sin(ang)


def apply_axial_rope(t, cos, sin):
    xa = t.reshape(t.shape[:-1] + (2, 2, DA_DIM // 4))
    x1, x2 = xa[..., 0, :], xa[..., 1, :]
    cs = cos[None, :, None, None].astype(t.dtype)
    sn = sin[None, :, None, None].astype(t.dtype)
    return jnp.stack([x1 * cs - x2 * sn, x1 * sn + x2 * cs], axis=-2).reshape(t.shape)


def diff_softmax_attend(q, keys, vals, lam):
    s = jnp.einsum('bqhmd,bkhmd->bhmqk', q, keys).astype(jnp.float32) * (DA_DIM ** -0.5)
    p = jax.nn.softmax(s, axis=-1)
    a = (p[:, :, 0] - lam * p[:, :, 1]).astype(vals.dtype)
    return jnp.einsum('bhqk,bkhe->bqhe', a, vals)


def pool_mixer(u, pool_w, pool_scale):
    bsz, n, _ = u.shape
    ug = u.reshape(bsz, n, len(POOL_WINDOWS), POOL_GROUP)
    cs = jnp.concatenate([jnp.zeros((bsz, 1) + ug.shape[2:], jnp.float32),
                          jnp.cumsum(ug.astype(jnp.float32), axis=1)], axis=1)
    t = jnp.arange(n)
    means = []
    for gi, w in enumerate(POOL_WINDOWS):
        lo = jnp.clip(t - w // 2, 0, n)
        hi = jnp.clip(t + w - w // 2, 0, n)
        means.append((cs[:, hi, gi] - cs[:, lo, gi]) / (hi - lo).astype(jnp.float32)[None, :, None])
    mean = jnp.stack(means, axis=2).astype(u.dtype)
    y = jnp.einsum('bngc,gce->bnge', mean - ug, pool_w)
    return y.reshape(bsz, n, POOL_WIDTH) * pool_scale


def retention_chunked(q, k, v, log_gamma, state0):
    bsz, n, nh, _ = q.shape
    nc = n // RET_CHUNK
    to_chunks = lambda t: t.reshape(bsz, nc, RET_CHUNK, nh, t.shape[-1]).transpose(1, 0, 3, 2, 4)
    idx = jnp.arange(RET_CHUNK, dtype=jnp.float32)
    diff = idx[:, None] - idx[None, :]
    dmat = jnp.exp(jnp.where(diff >= 0, diff * log_gamma[:, None, None], -jnp.inf)).astype(q.dtype)
    q_decay = jnp.exp((idx + 1.0)[None, :] * log_gamma[:, None]).astype(q.dtype)[..., None]
    k_decay = jnp.exp((RET_CHUNK - 1.0 - idx)[None, :] * log_gamma[:, None]).astype(q.dtype)[..., None]
    chunk_decay = jnp.exp(RET_CHUNK * log_gamma).astype(q.dtype)[:, None, None]

    def step(state, inp):
        qc, kc, vc = inp
        intra = jnp.einsum('bhid,bhjd->bhij', qc, kc) * dmat
        o = jnp.einsum('bhij,bhjv->bhiv', intra, vc) + jnp.einsum('bhid,bhdv->bhiv', qc * q_decay, state)
        state = state * chunk_decay + jnp.einsum('bhjd,bhjv->bhdv', kc * k_decay, vc)
        return state, o

    state, o = lax.scan(step, state0, (to_chunks(q), to_chunks(k), to_chunks(v)))
    return o.transpose(1, 0, 3, 2, 4).reshape(bsz, n, nh, v.shape[-1]), state


def retention_state(k, v, log_gamma):
    n = k.shape[1]
    w = jnp.exp((n - 1.0 - jnp.arange(n, dtype=jnp.float32))[None, :] * log_gamma[:, None]).astype(k.dtype)
    return jnp.einsum('bnhd,hn,bnhv->bhdv', k, w, v)


def token_mixer(h, hc, w_in, w_out, lam_vec, lambda_init, pool_w, pool_scale, log_decay, cos, sin, need_ctx):
    bsz, n, _ = h.shape
    m = hc.shape[1]
    dq, dk, dv, pu, rq, rk, rv, rg = jnp.split(h @ w_in, IN_OFFSETS, axis=-1)
    dqc, dkc, dvc, puc, rqc, rkc, rvc, rgc = jnp.split(hc @ w_in, IN_OFFSETS, axis=-1)

    qk_heads = lambda t: t.reshape(t.shape[:2] + (DA_HEADS, 2, DA_DIM))
    lam_f = lam_vec.astype(jnp.float32)
    lam = jnp.exp(jnp.sum(lam_f[0] * lam_f[1])) - jnp.exp(jnp.sum(lam_f[2] * lam_f[3])) + lambda_init
    q = apply_axial_rope(qk_heads(dq), cos, sin)
    k = apply_axial_rope(qk_heads(dk), cos, sin)
    kc = qk_heads(dkc)
    vc = dvc.reshape(bsz, m, DA_HEADS, DA_VDIM)
    keys = jnp.concatenate([kc, k], axis=1)
    vals = jnp.concatenate([vc, dv.reshape(bsz, n, DA_HEADS, DA_VDIM)], axis=1)
    q_blocks = q.reshape(bsz, n // QUERY_BLOCK, QUERY_BLOCK, DA_HEADS, 2, DA_DIM).swapaxes(0, 1)
    da = lax.map(lambda qb: diff_softmax_attend(qb, keys, vals, lam), q_blocks)
    da = da.swapaxes(0, 1).reshape(bsz, n, DA_HEADS, DA_VDIM)
    da_out = lambda o: (rms_norm(o) * (1.0 - lambda_init)).reshape(o.shape[:2] + (DA_WIDTH,))

    ret_heads = lambda t, d: t.reshape(t.shape[:2] + (RET_HEADS, d))
    qr = ret_heads(rq, RET_DK)
    kr = ret_heads(rk, RET_DK) * RET_DK ** -0.5
    vr = ret_heads(rv, RET_DV)
    krc = ret_heads(rkc, RET_DK) * RET_DK ** -0.5
    vrc = ret_heads(rvc, RET_DV)
    lg_f = -jnp.exp(log_decay[0].astype(jnp.float32))
    lg_b = -jnp.exp(log_decay[1].astype(jnp.float32))
    flip = lambda t: jnp.flip(t, axis=1)
    ret_out = lambda o, g: layer_norm(o).reshape(o.shape[:2] + (RET_WIDTH,)) * jax.nn.silu(g)
    if need_ctx:
        zero = jnp.zeros((bsz, RET_HEADS, RET_DK, RET_DV), h.dtype)
        qrc = ret_heads(rqc, RET_DK)
        oc_f, s_f = retention_chunked(qrc, krc, vrc, lg_f, zero)
        oc_b, s_b = retention_chunked(flip(qrc), flip(krc), flip(vrc), lg_b, zero)
    else:
        s_f = retention_state(krc, vrc, lg_f)
        s_b = retention_state(flip(krc), flip(vrc), lg_b)
    o_f, _ = retention_chunked(qr, kr, vr, lg_f, s_f)
    o_b, _ = retention_chunked(flip(qr), flip(kr), flip(vr), lg_b, s_b)

    y = jnp.concatenate([da_out(da), pool_mixer(pu, pool_w, pool_scale),
                         ret_out(o_f + flip(o_b), rg)], axis=-1) @ w_out
    if not need_ctx:
        return y, None
    dac = diff_softmax_attend(qk_heads(dqc), kc, vc, lam)
    yc = jnp.concatenate([da_out(dac), pool_mixer(puc, pool_w, pool_scale),
                          ret_out(oc_f + flip(oc_b), rgc)], axis=-1) @ w_out
    return y, yc


def routed_moe(tokens, w_router, router_bias, w_gu, w_dn, ws_gu, ws_dn):
    n_tok, d = tokens.shape
    scores = jax.nn.sigmoid((tokens @ w_router).astype(jnp.float32))
    grouped = (scores + router_bias.astype(jnp.float32)).reshape(n_tok, N_GROUPS, N_EXPERTS // N_GROUPS)
    group_score = jnp.sum(lax.top_k(grouped, 2)[0], axis=-1)
    top_groups = lax.top_k(group_score, TOPK_GROUPS)[1]
    group_ok = jnp.any(top_groups[:, :, None] == jnp.arange(N_GROUPS), axis=1)
    masked = jnp.where(group_ok[:, :, None], grouped, -jnp.inf).reshape(n_tok, N_EXPERTS)
    expert_idx = lax.top_k(masked, TOP_K)[1]
    gate = jnp.take_along_axis(scores, expert_idx, axis=1)
    gate = gate / jnp.sum(gate, axis=-1, keepdims=True) * ROUTED_SCALE

    n_asg = n_tok * TOP_K
    flat_e = expert_idx.reshape(-1)
    order = jnp.argsort(flat_e)
    sorted_e = flat_e[order]
    counts = jnp.bincount(flat_e, length=N_EXPERTS)
    padded = (counts + MOE_BLOCK - 1) // MOE_BLOCK * MOE_BLOCK
    pad_end = jnp.cumsum(padded)
    dest = (pad_end - padded)[sorted_e] + jnp.arange(n_asg) - (jnp.cumsum(counts) - counts)[sorted_e]
    n_blocks = -(-n_asg // MOE_BLOCK) + N_EXPERTS
    n_rows = n_blocks * MOE_BLOCK
    row_tok = jnp.full((n_rows,), n_tok, dtype=jnp.int32).at[dest].set((order // TOP_K).astype(jnp.int32))
    row_gate = jnp.zeros((n_rows,), tokens.dtype).at[dest].set(gate.reshape(-1)[order].astype(tokens.dtype))
    block_expert = jnp.minimum(jnp.searchsorted(pad_end, jnp.arange(n_blocks) * MOE_BLOCK, side='right'),
                               N_EXPERTS - 1)
    tokens_pad = jnp.concatenate([tokens, jnp.zeros((1, d), tokens.dtype)], axis=0)

    def expert_block(args):
        rows, wts, e = args
        return swiglu(tokens_pad[rows], w_gu[e], w_dn[e]) * wts[:, None]

    y_rows = lax.map(expert_block, (row_tok.reshape(n_blocks, MOE_BLOCK),
                                    row_gate.reshape(n_blocks, MOE_BLOCK), block_expert))
    routed = jax.ops.segment_sum(y_rows.reshape(n_rows, d), row_tok, num_segments=n_tok + 1)[:n_tok]
    return routed + swiglu(tokens, ws_gu, ws_dn)


def setup_inputs(seed: int = 0) -> dict:
    key = jax.random.key(seed)
    ks = jax.random.split(key, 20)
    nrm = jax.random.normal
    D = D_MODEL
    base_decay = jnp.log(-jnp.log1p(-(2.0 ** (-5.0 - jnp.arange(RET_HEADS, dtype=jnp.float32)))))
    return {
        'x': nrm(ks[0], (BATCH, SEQ, D), jnp.float32),
        'c': nrm(ks[1], (BATCH, D), jnp.float32),
        'ctx': nrm(ks[2], (BATCH, CTX_LEN, D), jnp.float32),
        'c_ctx': nrm(ks[3], (D,), jnp.float32),
        'w_mod': nrm(ks[4], (DEPTH, D, 6 * D), jnp.float32) * (0.5 * D ** -0.5),
        'b_mod': 0.02 * nrm(ks[5], (DEPTH, 6 * D), jnp.float32),
        'w_in': nrm(ks[6], (DEPTH, D, IN_WIDTH), jnp.float32) * D ** -0.5,
        'w_out': nrm(ks[7], (DEPTH, MIX_WIDTH, D), jnp.float32) * (MIX_WIDTH ** -0.5 * DN_BETA),
        'diff_lambda': 0.1 * nrm(ks[8], (DEPTH, 4, DA_DIM), jnp.float32),
        'pool_w': nrm(ks[9], (DEPTH, len(POOL_WINDOWS), POOL_GROUP, POOL_GROUP), jnp.float32) * POOL_GROUP ** -0.5,
        'pool_scale': 1.0 + 0.1 * nrm(ks[10], (DEPTH, POOL_WIDTH), jnp.float32),
        'ret_log_decay': base_decay + 0.1 * nrm(ks[11], (DEPTH, 2, RET_HEADS), jnp.float32),
        'ln_g': 1.0 + 0.1 * nrm(ks[12], (DEPTH, 2, D), jnp.float32),
        'ln_b': 0.02 * nrm(ks[13], (DEPTH, 2, D), jnp.float32),
        'w_router': nrm(ks[14], (DEPTH, D, N_EXPERTS), jnp.float32) * D ** -0.5,
        'router_bias': 0.01 * nrm(ks[15], (DEPTH, N_EXPERTS), jnp.float32),
        'w_expert_gate_up': nrm(ks[16], (DEPTH, N_EXPERTS, D, 2 * EXPERT_HIDDEN), jnp.float32) * D ** -0.5,
        'w_expert_down': nrm(ks[17], (DEPTH, N_EXPERTS, EXPERT_HIDDEN, D), jnp.float32) * (EXPERT_HIDDEN ** -0.5 * DN_BETA),
        'w_shared_gate_up': nrm(ks[18], (DEPTH, D, 2 * SHARED_HIDDEN), jnp.float32) * D ** -0.5,
        'w_shared_down': nrm(ks[19], (DEPTH, SHARED_HIDDEN, D), jnp.float32) * (SHARED_HIDDEN ** -0.5 * DN_BETA),
    }


def reference(x, c, ctx, c_ctx, w_mod, b_mod, w_in, w_out, diff_lambda, pool_w, pool_scale, ret_log_decay,
              ln_g, ln_b, w_router, router_bias, w_expert_gate_up, w_expert_down, w_shared_gate_up, w_shared_down):
    bsz, n, d = x.shape
    cos, sin = axial_rope_tables(n)
    silu_c = jax.nn.silu(c)
    silu_cc = jax.nn.silu(c_ctx)
    xc = ctx
    for l in range(DEPTH):
        need_ctx = l < DEPTH - 1
        mod = jnp.split((silu_c @ w_mod[l] + b_mod[l])[:, None, :], 6, axis=-1)
        modc = jnp.split((silu_cc @ w_mod[l] + b_mod[l])[None, None, :], 6, axis=-1)
        lambda_init = 0.8 - 0.6 * math.exp(-0.3 * l)

        y, yc = token_mixer(modulate(x, mod[0], mod[1]), modulate(xc, modc[0], modc[1]), w_in[l], w_out[l],
                            diff_lambda[l], lambda_init, pool_w[l], pool_scale[l], ret_log_decay[l],
                            cos, sin, need_ctx)
        x = layer_norm(DN_ALPHA * x + mod[2] * y, ln_g[l, 0], ln_b[l, 0])
        if need_ctx:
            xc = layer_norm(DN_ALPHA * xc + modc[2] * yc, ln_g[l, 0], ln_b[l, 0])

        tok = modulate(x, mod[3], mod[4]).reshape(-1, d)
        if need_ctx:
            tok = jnp.concatenate([tok, modulate(xc, modc[3], modc[4]).reshape(-1, d)], axis=0)
        f = routed_moe(tok, w_router[l], router_bias[l], w_expert_gate_up[l], w_expert_down[l],
                       w_shared_gate_up[l], w_shared_down[l])
        x = layer_norm(DN_ALPHA * x + mod[5] * f[:bsz * n].reshape(bsz, n, d), ln_g[l, 1], ln_b[l, 1])
        if need_ctx:
            xc = layer_norm(DN_ALPHA * xc + modc[5] * f[bsz * n:].reshape(xc.shape), ln_g[l, 1], ln_b[l, 1])
    return x
```

```python
import functools
import math

import jax
import jax.numpy as jnp
from jax import lax
from jax.experimental import pallas as pl
from jax.experimental.pallas import tpu as pltpu

F32 = jnp.float32
BF16 = jnp.bfloat16
HIGHEST = lax.Precision.HIGHEST

D_MODEL = 1024
CTX_LEN = 256
GRID_W = 64
DA_HEADS = 4
DA_DIM = 64
DA_VDIM = 2 * DA_DIM
DA_WIDTH = DA_HEADS * DA_VDIM
ROPE_BASE = 10000.0
POOL_WINDOWS = (2, 4, 8, 16)
POOL_GROUP = 64
POOL_WIDTH = len(POOL_WINDOWS) * POOL_GROUP
POOL_HALO = 8
RET_HEADS = 4
RET_DK = 64
RET_WIDTH = RET_HEADS * RET_DK
RET_CHUNK = 128
QK_WIDTH = 2 * DA_HEADS * 2 * DA_DIM
IN_WIDTH = QK_WIDTH + DA_WIDTH + POOL_WIDTH + 4 * RET_WIDTH
N_EXPERTS = 256
TOP_K = 8
N_GROUPS = 8
GROUP_SIZE = N_EXPERTS // N_GROUPS
TOPK_GROUPS = 4
EXPERT_HIDDEN = 256
ROUTED_SCALE = 2.5
LN_EPS = 1e-6
RMS_EPS = 1e-5

LANES = 128
ROW_TILE = 256
ATTN_Q_TILE = 256
ATTN_K_CHUNK = 512
EXPERT_BLOCK = 256
PACK_W = D_MODEL // 2
PACK_S = PACK_W // LANES
ROW_S = D_MODEL // LANES
VMEM_LIMIT = 56 * 1024 * 1024


def _cparams(*sem):
    return pltpu.CompilerParams(dimension_semantics=sem, vmem_limit_bytes=VMEM_LIMIT)


def _sigmoid(x):
    return 1.0 / (1.0 + jnp.exp(-x))


def _layer_norm_rows(x):
    mu = jnp.mean(x, axis=-1, keepdims=True)
    xc = x - mu
    var = jnp.mean(xc * xc, axis=-1, keepdims=True)
    return xc * lax.rsqrt(var + LN_EPS)


def _mod_row(i, tiles_per_batch):
    return jnp.where(i % tiles_per_batch == tiles_per_batch - 1, 2, i // tiles_per_batch)


def _mod_kernel(c_ref, w_ref, b_ref, o_ref):
    c = c_ref[...]
    s = c * _sigmoid(c)
    o_ref[...] = jnp.dot(s, w_ref[...], precision=HIGHEST, preferred_element_type=F32) + b_ref[...]


def _mod_call(cvec, w_mod, b_mod):
    depth, d, n = w_mod.shape
    tn = 1536
    return pl.pallas_call(
        _mod_kernel,
        grid=(depth, n // tn),
        in_specs=[
            pl.BlockSpec((8, d), lambda l, j: (0, 0)),
            pl.BlockSpec((None, d, tn), lambda l, j: (l, 0, j)),
            pl.BlockSpec((None, 1, tn), lambda l, j: (l, 0, j)),
        ],
        out_specs=pl.BlockSpec((None, 8, tn), lambda l, j: (l, 0, j)),
        out_shape=jax.ShapeDtypeStruct((depth, 8, n), F32),
        compiler_params=_cparams("arbitrary", "arbitrary"),
        name="mod",
    )(cvec, w_mod, b_mod.reshape(depth, 1, n))


def _inproj_kernel(x_ref, mod_ref, w_ref, ct_ref, st_ref, qk_ref, v_ref, u_ref, r_ref, g_ref):
    d = D_MODEL
    xn = _layer_norm_rows(x_ref[...])
    h = (xn * (1.0 + mod_ref[:, d:2 * d]) + mod_ref[:, 0:d]).astype(BF16)

    a = jnp.dot(h, w_ref[:, 0:QK_WIDTH], preferred_element_type=F32)
    lane = lax.broadcasted_iota(jnp.int32, (a.shape[0], LANES), 1)
    first_half = (lane % 32) < 16
    ct = ct_ref[...]
    st = st_ref[...]
    for s in range(QK_WIDTH // LANES):
        blk = a[:, s * LANES:(s + 1) * LANES]
        partner = jnp.where(first_half, pltpu.roll(blk, LANES - 16, 1), pltpu.roll(blk, 16, 1))
        rot = blk * ct + partner * st
        if s < QK_WIDTH // LANES // 2:
            rot = rot * (DA_DIM ** -0.5)
        qk_ref[:, s * LANES:(s + 1) * LANES] = rot.astype(BF16)

    o = QK_WIDTH
    v_ref[...] = jnp.dot(h, w_ref[:, o:o + DA_WIDTH], preferred_element_type=F32).astype(BF16)
    o += DA_WIDTH
    u_ref[...] = jnp.dot(h, w_ref[:, o:o + POOL_WIDTH], preferred_element_type=F32)
    o += POOL_WIDTH
    r = jnp.dot(h, w_ref[:, o:o + 3 * RET_WIDTH], preferred_element_type=F32)
    r_ref[:, 0:RET_WIDTH] = r[:, 0:RET_WIDTH].astype(BF16)
    r_ref[:, RET_WIDTH:2 * RET_WIDTH] = (r[:, RET_WIDTH:2 * RET_WIDTH] * (RET_DK ** -0.5)).astype(BF16)
    r_ref[:, 2 * RET_WIDTH:] = r[:, 2 * RET_WIDTH:].astype(BF16)
    o += 3 * RET_WIDTH
    g_ref[...] = jnp.dot(h, w_ref[:, o:o + RET_WIDTH], preferred_element_type=F32)


def _inproj_call(x, mod3, w_in_bf, rope_c, rope_s, tiles_per_batch):
    r, d = x.shape
    t = ROW_TILE
    nt = r // t
    row = lambda i: (i, 0)
    return pl.pallas_call(
        _inproj_kernel,
        grid=(nt,),
        in_specs=[
            pl.BlockSpec((t, d), row),
            pl.BlockSpec((None, 1, 6 * d), lambda i: (_mod_row(i, tiles_per_batch), 0, 0)),
            pl.BlockSpec((d, IN_WIDTH), lambda i: (0, 0)),
            pl.BlockSpec((t, LANES), lambda i: (i % tiles_per_batch, 0)),
            pl.BlockSpec((t, LANES), lambda i: (i % tiles_per_batch, 0)),
        ],
        out_specs=[
            pl.BlockSpec((t, QK_WIDTH), row),
            pl.BlockSpec((t, DA_WIDTH), row),
            pl.BlockSpec((t, POOL_WIDTH), row),
            pl.BlockSpec((t, 3 * RET_WIDTH), row),
            pl.BlockSpec((t, RET_WIDTH), row),
        ],
        out_shape=[
            jax.ShapeDtypeStruct((r, QK_WIDTH), BF16),
            jax.ShapeDtypeStruct((r, DA_WIDTH), BF16),
            jax.ShapeDtypeStruct((r, POOL_WIDTH), F32),
            jax.ShapeDtypeStruct((r, 3 * RET_WIDTH), BF16),
            jax.ShapeDtypeStruct((r, RET_WIDTH), F32),
        ],
        compiler_params=_cparams("arbitrary"),
        name="inproj",
    )(x, mod3, w_in_bf, rope_c, rope_s)


def _attn_kernel(lam_ref, q_ref, k_ref, v_ref, o_ref, *, k_chunk, seq, lambda_init):
    q = q_ref[...]
    lane = lax.broadcasted_iota(jnp.int32, q.shape, 1)
    zero = jnp.zeros_like(q)
    q_maps = (jnp.where(lane < DA_DIM, q, zero), jnp.where(lane >= DA_DIM, q, zero))
    mq = q.shape[0]
    is_ctx_tile = pl.program_id(2) == pl.num_programs(2) - 1
    n_latent_chunks = jnp.where(is_ctx_tile, 0, seq // k_chunk)

    def step(off, size, carry):
        k = k_ref[pl.ds(off, size), :]
        v = v_ref[pl.ds(off, size), :]
        new = []
        for qm, (m, l, acc) in zip(q_maps, carry):
            s = lax.dot_general(qm, k, (((1,), (1,)), ((), ())), preferred_element_type=F32)
            m_new = jnp.maximum(m, jnp.max(s, axis=-1, keepdims=True))
            alpha = jnp.exp(m - m_new)
            p = jnp.exp(s - m_new)
            l_new = alpha * l + jnp.sum(p, axis=-1, keepdims=True)
            acc_new = alpha * acc + jnp.dot(p.astype(BF16), v, preferred_element_type=F32)
            new.append((m_new, l_new, acc_new))
        return tuple(new)

    def body(c, carry):
        return step(pl.multiple_of(c * k_chunk, k_chunk), k_chunk, carry)

    init = tuple((jnp.full((mq, 1), -jnp.inf, F32), jnp.zeros((mq, 1), F32), jnp.zeros((mq, DA_VDIM), F32))
                 for _ in range(2))
    carry = lax.fori_loop(0, n_latent_chunks, body, init)
    (_, l0, a0), (_, l1, a1) = step(seq, CTX_LEN, carry)

    lv = lam_ref[...]
    lam = (jnp.exp(jnp.sum(lv[0:1] * lv[1:2], axis=-1, keepdims=True))
           - jnp.exp(jnp.sum(lv[2:3] * lv[3:4], axis=-1, keepdims=True)) + lambda_init)
    o = a0 / l0 - lam * (a1 / l1)
    o = o * lax.rsqrt(jnp.mean(o * o, axis=-1, keepdims=True) + RMS_EPS) * (1.0 - lambda_init)
    o_ref[...] = o.astype(BF16)


def _attn_call(lam_vec, qk, vda, *, batch, rows_per_batch, seq, lambda_init):
    tq = ATTN_Q_TILE
    assert seq % ATTN_K_CHUNK == 0 and rows_per_batch - seq == CTX_LEN == tq
    nq = rows_per_batch // tq
    kern = functools.partial(_attn_kernel, k_chunk=ATTN_K_CHUNK, seq=seq, lambda_init=lambda_init)
    return pl.pallas_call(
        kern,
        grid=(batch, DA_HEADS, nq),
        in_specs=[
            pl.BlockSpec((4, DA_DIM), lambda b, h, i: (0, 0)),
            pl.BlockSpec((tq, DA_VDIM), lambda b, h, i: (b * nq + i, h)),
            pl.BlockSpec((rows_per_batch, DA_VDIM), lambda b, h, i: (b, DA_HEADS + h)),
            pl.BlockSpec((rows_per_batch, DA_VDIM), lambda b, h, i: (b, h)),
        ],
        out_specs=pl.BlockSpec((tq, DA_VDIM), lambda b, h, i: (b * nq + i, h)),
        out_shape=jax.ShapeDtypeStruct((qk.shape[0], DA_WIDTH), BF16),
        compiler_params=_cparams("arbitrary", "arbitrary", "arbitrary"),
        name="diff_attn",
    )(lam_vec, qk, qk, vda)


def _ret_kernel(ld_ref, f_ref, b_ref, of_ref, ob_ref, dm_ref, qd_ref, kd_ref, cd_ref, st_ref):
    c = pl.program_id(1)
    ch = RET_CHUNK
    w = RET_WIDTH
    lane_head = lax.broadcasted_iota(jnp.int32, (1, w), 1) // RET_DK

    @pl.when(c == 0)
    def _():
        st_ref[...] = jnp.zeros_like(st_ref)
        ri = lax.broadcasted_iota(jnp.int32, (ch, ch), 0)
        ci = lax.broadcasted_iota(jnp.int32, (ch, ch), 1)
        rowf = lax.broadcasted_iota(jnp.int32, (ch, w), 0).astype(F32)
        for d in range(2):
            lg_lane = jnp.zeros((1, w), F32)
            for hh in range(RET_HEADS):
                lg = -jnp.exp(jnp.full((1, 1), ld_ref[d, hh], F32))
                lg_lane = jnp.where(lane_head == hh, lg, lg_lane)
                dist = ((ri - ci) if d == 0 else (ci - ri)).astype(F32)
                dm_ref[d, hh] = jnp.where(dist >= 0, jnp.exp(dist * lg), 0.0)
            if d == 0:
                qd_ref[d] = jnp.exp((rowf + 1.0) * lg_lane)
                kd_ref[d] = jnp.exp((ch - 1.0 - rowf) * lg_lane)
            else:
                qd_ref[d] = jnp.exp((ch - rowf) * lg_lane)
                kd_ref[d] = jnp.exp(rowf * lg_lane)
            cd_ref[d] = jnp.exp(float(ch) * lg_lane)

    rblk = lax.broadcasted_iota(jnp.int32, (w, w), 0) // RET_DK
    cblk = lax.broadcasted_iota(jnp.int32, (w, w), 1) // RET_DK
    for d, (src, dst) in enumerate(((f_ref, of_ref), (b_ref, ob_ref))):
        q = src[:, 0:w]
        k = src[:, w:2 * w]
        v = src[:, 2 * w:3 * w]
        st = st_ref[d]
        o = jnp.dot((q.astype(F32) * qd_ref[d]).astype(BF16), st.astype(BF16), preferred_element_type=F32)
        for hh in range(RET_HEADS):
            in_head = lane_head == hh
            qm = jnp.where(in_head, q, jnp.zeros_like(q))
            s = lax.dot_general(qm, k, (((1,), (1,)), ((), ())), preferred_element_type=F32)
            intra = (s * dm_ref[d, hh]).astype(BF16)
            o = o + jnp.where(in_head, jnp.dot(intra, v, preferred_element_type=F32), 0.0)
        dst[...] = o
        kk_t = (k.astype(F32) * kd_ref[d]).T.astype(BF16)
        upd = jnp.dot(kk_t, v, preferred_element_type=F32)
        st_ref[d] = jnp.where(rblk == cblk, st * cd_ref[d] + upd, 0.0)


def _ret_call(log_decay, rqkv, *, batch, rows_per_batch, seq):
    ch = RET_CHUNK
    nc = rows_per_batch // ch
    n_lat = seq // ch
    n_ctx = nc - n_lat

    def fwd(b, c):
        return (b * nc + jnp.where(c < n_ctx, n_lat + c, c - n_ctx), 0)

    def bwd(b, c):
        return (b * nc + nc - 1 - c, 0)

    w = RET_WIDTH
    return pl.pallas_call(
        _ret_kernel,
        grid=(batch, nc),
        in_specs=[
            pl.BlockSpec(memory_space=pltpu.SMEM),
            pl.BlockSpec((ch, 3 * w), fwd),
            pl.BlockSpec((ch, 3 * w), bwd),
        ],
        out_specs=[pl.BlockSpec((ch, w), fwd), pl.BlockSpec((ch, w), bwd)],
        out_shape=[jax.ShapeDtypeStruct((rqkv.shape[0], w), F32)] * 2,
        scratch_shapes=[
            pltpu.VMEM((2, RET_HEADS, ch, ch), F32),
            pltpu.VMEM((2, ch, w), F32),
            pltpu.VMEM((2, ch, w), F32),
            pltpu.VMEM((2, 1, w), F32),
            pltpu.VMEM((2, w, w), F32),
        ],
        compiler_params=_cparams("arbitrary", "arbitrary"),
        name="retention",
    )(log_decay, rqkv, rqkv)


def _mixout_kernel(x_ref, da_ref, u_ref, up_ref, un_ref, of_ref, ob_ref, rg_ref, mod_ref, wo_ref, pw_ref,
                   ps_ref, lng_ref, lnb_ref, o_ref, *, tiles_per_batch, seq, alpha):
    d = D_MODEL
    t = x_ref.shape[0]
    i = pl.program_id(0)
    j = i % tiles_per_batch
    is_ctx = j == tiles_per_batch - 1
    stream_len = jnp.where(is_ctx, CTX_LEN, seq)
    p0 = jnp.where(is_ctx, 0, j * t)

    u = u_ref[...]
    prev = jnp.where(p0 > 0, up_ref[...], 0.0)
    nxt = jnp.where(p0 + t < stream_len, un_ref[...], 0.0)
    ext = jnp.concatenate([prev, u, nxt], axis=0)
    n = t + 2 * POOL_HALO
    a2 = ext + pltpu.roll(ext, 1, 0)
    a4 = pltpu.roll(a2, 1, 0) + pltpu.roll(a2, n - 1, 0)
    a8 = pltpu.roll(a4, 2, 0) + pltpu.roll(a4, n - 2, 0)
    a16 = pltpu.roll(a8, 4, 0) + pltpu.roll(a8, n - 4, 0)
    pos = p0 + lax.broadcasted_iota(jnp.int32, (t, POOL_WIDTH), 0)
    group = lax.broadcasted_iota(jnp.int32, (1, POOL_WIDTH), 1) // POOL_GROUP
    mean = jnp.zeros((t, POOL_WIDTH), F32)
    for gi, (wnd, asum) in enumerate(zip(POOL_WINDOWS, (a2, a4, a8, a16))):
        cnt = jnp.minimum(pos + wnd // 2, stream_len) - jnp.maximum(pos - wnd // 2, 0)
        mean = jnp.where(group == gi, asum[POOL_HALO:POOL_HALO + t] / cnt.astype(F32), mean)
    pool = jnp.dot((mean - u).astype(BF16), pw_ref[...], preferred_element_type=F32) * ps_ref[...]

    o = of_ref[...] + ob_ref[...]
    head = lax.broadcasted_iota(jnp.int32, (1, RET_WIDTH), 1) // RET_DK

    def head_mean(val):
        out = jnp.zeros_like(val)
        for hh in range(RET_HEADS):
            m = jnp.sum(jnp.where(head == hh, val, 0.0), axis=-1, keepdims=True) * (1.0 / RET_DK)
            out = jnp.where(head == hh, m, out)
        return out

    oc = o - head_mean(o)
    rn = oc * lax.rsqrt(head_mean(oc * oc) + LN_EPS)
    g = rg_ref[...]
    ret = rn * (g * _sigmoid(g))

    y = jnp.dot(da_ref[...], wo_ref[0:DA_WIDTH, :], preferred_element_type=F32)
    y = y + jnp.dot(pool.astype(BF16), wo_ref[DA_WIDTH:DA_WIDTH + POOL_WIDTH, :], preferred_element_type=F32)
    y = y + jnp.dot(ret.astype(BF16), wo_ref[DA_WIDTH + POOL_WIDTH:, :], preferred_element_type=F32)
    z = alpha * x_ref[...] + mod_ref[:, 2 * d:3 * d] * y
    o_ref[...] = _layer_norm_rows(z) * lng_ref[...] + lnb_ref[...]


def _mixout_call(x, da, u, o_f, o_b, rg, mod3, w_out_bf, pool_bd, pool_scale, ln_g, ln_b, *, tiles_per_batch, seq,
                 alpha):
    r, d = x.shape
    t = ROW_TILE
    nt = r // t
    hb = t // POOL_HALO
    n_halo_blocks = r // POOL_HALO
    row = lambda i: (i, 0)
    const = lambda i: (0, 0)
    kern = functools.partial(_mixout_kernel, tiles_per_batch=tiles_per_batch, seq=seq, alpha=alpha)
    return pl.pallas_call(
        kern,
        grid=(nt,),
        in_specs=[
            pl.BlockSpec((t, d), row),
            pl.BlockSpec((t, DA_WIDTH), row),
            pl.BlockSpec((t, POOL_WIDTH), row),
            pl.BlockSpec((POOL_HALO, POOL_WIDTH), lambda i: (jnp.maximum(i * hb - 1, 0), 0)),
            pl.BlockSpec((POOL_HALO, POOL_WIDTH), lambda i: (jnp.minimum((i + 1) * hb, n_halo_blocks - 1), 0)),
            pl.BlockSpec((t, RET_WIDTH), row),
            pl.BlockSpec((t, RET_WIDTH), row),
            pl.BlockSpec((t, RET_WIDTH), row),
            pl.BlockSpec((None, 1, 6 * d), lambda i: (_mod_row(i, tiles_per_batch), 0, 0)),
            pl.BlockSpec((d, d), const),
            pl.BlockSpec((POOL_WIDTH, POOL_WIDTH), const),
            pl.BlockSpec((1, POOL_WIDTH), const),
            pl.BlockSpec((1, d), const),
            pl.BlockSpec((1, d), const),
        ],
        out_specs=pl.BlockSpec((t, d), row),
        out_shape=jax.ShapeDtypeStruct((r, d), F32),
        compiler_params=_cparams("arbitrary"),
        name="mixer_out",
    )(x, da, u, u, u, o_f, o_b, rg, mod3, w_out_bf, pool_bd, pool_scale, ln_g, ln_b)


def _router_kernel(x_ref, mod_ref, wrh_ref, wrl_ref, bias_ref, wsgu_ref, wsdn_ref,
                   tokp_ref, idx_ref, gate_ref, rank_ref, cnt_ref, fsh_ref, carry_ref):
    d = D_MODEL
    t = x_ref.shape[0]
    ne = N_EXPERTS
    neg = -jnp.inf

    @pl.when(pl.program_id(0) == 0)
    def _():
        carry_ref[...] = jnp.zeros_like(carry_ref)

    tok = _layer_norm_rows(x_ref[...]) * (1.0 + mod_ref[:, 4 * d:5 * d]) + mod_ref[:, 3 * d:4 * d]
    tok_hi = tok.astype(BF16)
    tok_lo = (tok - tok_hi.astype(F32)).astype(BF16)

    bits = pltpu.bitcast(tok_hi.astype(F32), jnp.uint32)
    tokp_ref[...] = (lax.shift_right_logical(bits[:, 0:PACK_W], jnp.uint32(16))
                     | (bits[:, PACK_W:] & jnp.uint32(0xFFFF0000)))

    hs = jnp.dot(tok_hi, wsgu_ref[...], preferred_element_type=F32)
    gs, us = hs[:, 0:EXPERT_HIDDEN], hs[:, EXPERT_HIDDEN:]
    fsh_ref[...] = jnp.dot((gs * _sigmoid(gs) * us).astype(BF16), wsdn_ref[...], preferred_element_type=F32)

    nt_dims = (((1,), (1,)), ((), ()))
    logits = (lax.dot_general(wrh_ref[...], tok_hi, nt_dims, preferred_element_type=F32)
              + lax.dot_general(wrh_ref[...], tok_lo, nt_dims, preferred_element_type=F32)
              + lax.dot_general(wrl_ref[...], tok_hi, nt_dims, preferred_element_type=F32))
    scores = _sigmoid(logits)
    biased = scores + bias_ref[...]

    gidx = lax.broadcasted_iota(jnp.int32, (GROUP_SIZE, t), 0)
    blocks, gscores = [], []
    for g in range(N_GROUPS):
        blk = biased[g * GROUP_SIZE:(g + 1) * GROUP_SIZE, :]
        m1 = jnp.max(blk, axis=0, keepdims=True)
        first = jnp.min(jnp.where(blk == m1, gidx, GROUP_SIZE), axis=0, keepdims=True)
        m2 = jnp.max(jnp.where(gidx == first, neg, blk), axis=0, keepdims=True)
        blocks.append(blk)
        gscores.append(m1 + m2)

    keep = [jnp.zeros((1, t), F32) for _ in range(N_GROUPS)]
    for _ in range(TOPK_GROUPS):
        m = gscores[0]
        for gs_ in gscores[1:]:
            m = jnp.maximum(m, gs_)
        found = jnp.zeros((1, t), F32)
        for g in range(N_GROUPS):
            hit = jnp.where(gscores[g] == m, 1.0 - found, 0.0)
            found = found + hit
            keep[g] = keep[g] + hit
            gscores[g] = jnp.where(hit > 0.0, neg, gscores[g])
    masked = jnp.concatenate([jnp.where(keep[g] > 0.0, blocks[g], neg) for g in range(N_GROUPS)], axis=0)

    ei = lax.broadcasted_iota(jnp.int32, (ne, t), 0)
    cur = masked
    onehot = jnp.zeros((ne, t), F32)
    idxs, gates = [], []
    for _ in range(TOP_K):
        m = jnp.max(cur, axis=0, keepdims=True)
        ii = jnp.min(jnp.where(cur == m, ei, ne), axis=0, keepdims=True)
        sel = ei == ii
        idxs.append(ii)
        gates.append(jnp.sum(jnp.where(sel, scores, 0.0), axis=0, keepdims=True))
        onehot = jnp.where(sel, 1.0, onehot)
        cur = jnp.where(sel, neg, cur)
    gsum = gates[0]
    for gk in gates[1:]:
        gsum = gsum + gk
    for k in range(TOP_K):
        idx_ref[k:k + 1, :] = idxs[k]
        gate_ref[k:k + 1, :] = gates[k] / gsum * ROUTED_SCALE

    ti = lax.broadcasted_iota(jnp.int32, (t, t), 0)
    tj = lax.broadcasted_iota(jnp.int32, (t, t), 1)
    before = jnp.where(ti < tj, 1.0, 0.0).astype(BF16)
    prefix = jnp.dot(onehot.astype(BF16), before, preferred_element_type=F32) + carry_ref[:, 0:1]
    for k in range(TOP_K):
        rank_k = jnp.sum(jnp.where(ei == idxs[k], prefix, 0.0), axis=0, keepdims=True)
        rank_ref[k:k + 1, :] = rank_k.astype(jnp.int32)
    carry_ref[...] = carry_ref[...] + jnp.sum(onehot, axis=1, keepdims=True)
    cnt_ref[...] = carry_ref[...].astype(jnp.int32)


def _router_call(x, mod3, wr_hi, wr_lo, bias_col, ws_gu_bf, ws_dn_bf, *, tiles_per_batch):
    r, d = x.shape
    t = ROW_TILE
    nt = r // t
    row = lambda i: (i, 0)
    col = lambda i: (0, i)
    const = lambda i: (0, 0)
    return pl.pallas_call(
        _router_kernel,
        grid=(nt,),
        in_specs=[
            pl.BlockSpec((t, d), row),
            pl.BlockSpec((None, 1, 6 * d), lambda i: (_mod_row(i, tiles_per_batch), 0, 0)),
            pl.BlockSpec((N_EXPERTS, d), const),
            pl.BlockSpec((N_EXPERTS, d), const),
            pl.BlockSpec((N_EXPERTS, 1), const),
            pl.BlockSpec((d, 2 * EXPERT_HIDDEN), const),
            pl.BlockSpec((EXPERT_HIDDEN, d), const),
        ],
        out_specs=[
            pl.BlockSpec((t, PACK_W), row),
            pl.BlockSpec((TOP_K, t), col),
            pl.BlockSpec((TOP_K, t), col),
            pl.BlockSpec((TOP_K, t), col),
            pl.BlockSpec((N_EXPERTS, LANES), const),
            pl.BlockSpec((t, d), row),
        ],
        out_shape=[
            jax.ShapeDtypeStruct((r, PACK_W), jnp.uint32),
            jax.ShapeDtypeStruct((TOP_K, r), jnp.int32),
            jax.ShapeDtypeStruct((TOP_K, r), F32),
            jax.ShapeDtypeStruct((TOP_K, r), jnp.int32),
            jax.ShapeDtypeStruct((N_EXPERTS, LANES), jnp.int32),
            jax.ShapeDtypeStruct((r, d), F32),
        ],
        scratch_shapes=[pltpu.VMEM((N_EXPERTS, LANES), F32)],
        compiler_params=_cparams("arbitrary"),
        name="router",
    )(x, mod3, wr_hi, wr_lo, bias_col, ws_gu_bf, ws_dn_bf)


def _dest_kernel(idx_ref, rank_ref, offs_ref, dest_ref):
    t = idx_ref.shape[1]
    ei = lax.broadcasted_iota(jnp.int32, (N_EXPERTS, t), 0)
    offs = offs_ref[...].astype(F32)
    for k in range(TOP_K):
        start = jnp.sum(jnp.where(ei == idx_ref[k:k + 1, :], offs, 0.0), axis=0, keepdims=True)
        dest_ref[k:k + 1, :] = start.astype(jnp.int32) + rank_ref[k:k + 1, :]


def _dest_call(idx, rank, offs_col):
    r = idx.shape[1]
    t = ROW_TILE
    col = lambda i: (0, i)
    return pl.pallas_call(
        _dest_kernel,
        grid=(r // t,),
        in_specs=[pl.BlockSpec((TOP_K, t), col), pl.BlockSpec((TOP_K, t), col),
                  pl.BlockSpec((N_EXPERTS, 1), lambda i: (0, 0))],
        out_specs=pl.BlockSpec((TOP_K, t), col),
        out_shape=jax.ShapeDtypeStruct((TOP_K, r), jnp.int32),
        compiler_params=_cparams("arbitrary"),
        name="moe_dest",
    )(idx, rank, offs_col)


def _dispatch_kernel(dest_ref, tokp_ref, xs_in_ref, xs_ref, sem):
    del xs_in_ref
    t = dest_ref.shape[1]
    base = pl.program_id(0) * t

    def row_copy(tt, k):
        return pltpu.make_async_copy(tokp_ref.at[base + tt], xs_ref.at[dest_ref[k, tt]], sem)

    def issue(tt, carry):
        for k in range(TOP_K):
            row_copy(tt, k).start()
        return carry

    lax.fori_loop(0, t, issue, 0)

    def drain(tt, carry):
        for k in range(TOP_K):
            row_copy(tt, k).wait()
        return carry

    lax.fori_loop(0, t, drain, 0)


def _dispatch_call(dest, tokp3, xs_zero):
    r = dest.shape[1]
    t = ROW_TILE
    return pl.pallas_call(
        _dispatch_kernel,
        grid=(r // t,),
        in_specs=[
            pl.BlockSpec((TOP_K, t), lambda i: (0, i), memory_space=pltpu.SMEM),
            pl.BlockSpec(memory_space=pl.ANY),
            pl.BlockSpec(memory_space=pl.ANY),
        ],
        out_specs=pl.BlockSpec(memory_space=pl.ANY),
        out_shape=jax.ShapeDtypeStruct(xs_zero.shape, xs_zero.dtype),
        scratch_shapes=[pltpu.SemaphoreType.DMA(())],
        input_output_aliases={2: 0},
        compiler_params=_cparams("arbitrary"),
        name="moe_dispatch",
    )(dest, tokp3, xs_zero)


def _expert_kernel(be_ref, nb_ref, xs_ref, wgu_ref, wdn_ref, ys_ref, wgu_bf, wdn_bf):
    j = pl.program_id(0)

    @pl.when(j < nb_ref[0])
    def _():
        changed = jnp.logical_or(j == 0, be_ref[j] != be_ref[jnp.maximum(j - 1, 0)])

        @pl.when(changed)
        def _():
            wgu_bf[...] = wgu_ref[...].astype(BF16)
            wdn_bf[...] = wdn_ref[...].astype(BF16)

        def unpack(word):
            lo = pltpu.bitcast(lax.shift_left(word, jnp.uint32(16)), F32)
            hi = pltpu.bitcast(word & jnp.uint32(0xFFFF0000), F32)
            return lo.astype(BF16), hi.astype(BF16)

        parts = [unpack(xs_ref[:, s, :]) for s in range(PACK_S)]
        x_lo = jnp.concatenate([p[0] for p in parts], axis=1)
        x_hi = jnp.concatenate([p[1] for p in parts], axis=1)
        h = (jnp.dot(x_lo, wgu_bf[0:PACK_W, :], preferred_element_type=F32)
             + jnp.dot(x_hi, wgu_bf[PACK_W:, :], preferred_element_type=F32))
        g, u = h[:, 0:EXPERT_HIDDEN], h[:, EXPERT_HIDDEN:]
        y = jnp.dot((g * _sigmoid(g) * u).astype(BF16), wdn_bf[...], preferred_element_type=F32)
        for s in range(ROW_S):
            ys_ref[:, s, :] = y[:, s * LANES:(s + 1) * LANES]


def _expert_call(block_expert, n_blocks_used, xs, w_gu, w_dn):
    n_rows = xs.shape[0]
    bm = EXPERT_BLOCK
    d = D_MODEL
    grid_spec = pltpu.PrefetchScalarGridSpec(
        num_scalar_prefetch=2,
        grid=(n_rows // bm,),
        in_specs=[
            pl.BlockSpec((bm, PACK_S, LANES), lambda j, be, nb: (j, 0, 0)),
            pl.BlockSpec((None, d, 2 * EXPERT_HIDDEN), lambda j, be, nb: (be[j], 0, 0)),
            pl.BlockSpec((None, EXPERT_HIDDEN, d), lambda j, be, nb: (be[j], 0, 0)),
        ],
        out_specs=pl.BlockSpec((bm, ROW_S, LANES), lambda j, be, nb: (j, 0, 0)),
        scratch_shapes=[pltpu.VMEM((d, 2 * EXPERT_HIDDEN), BF16), pltpu.VMEM((EXPERT_HIDDEN, d), BF16)],
    )
    return pl.pallas_call(
        _expert_kernel,
        grid_spec=grid_spec,
        out_shape=jax.ShapeDtypeStruct((n_rows, ROW_S, LANES), F32),
        compiler_params=_cparams("arbitrary"),
        name="moe_experts",
    )(block_expert, n_blocks_used, xs, w_gu, w_dn)


def _combine_kernel(dest_ref, ys_ref, x_ref, fsh_ref, gate_ref, mod_ref, lng_ref, lnb_ref, o_ref, buf, sem, *,
                    alpha):
    d = D_MODEL
    t = x_ref.shape[0]

    def row_copy(tt, k):
        return pltpu.make_async_copy(ys_ref.at[dest_ref[k, tt]], buf.at[k * t + tt], sem)

    def issue(tt, carry):
        for k in range(TOP_K):
            row_copy(tt, k).start()
        return carry

    lax.fori_loop(0, t, issue, 0)

    def drain(tt, carry):
        for k in range(TOP_K):
            row_copy(tt, k).wait()
        return carry

    lax.fori_loop(0, t, drain, 0)

    gate_rows = gate_ref[...]
    pad = jnp.zeros((LANES - TOP_K, t), F32)
    gate_cols = jnp.concatenate([gate_rows, pad], axis=0).T
    x = x_ref[...]
    fsh = fsh_ref[...]
    gmod = mod_ref[:, 5 * d:6 * d]
    zs = []
    for s in range(ROW_S):
        sl = slice(s * LANES, (s + 1) * LANES)
        f = fsh[:, sl]
        for k in range(TOP_K):
            f = f + gate_cols[:, k:k + 1] * buf[pl.ds(k * t, t), s, :]
        zs.append(alpha * x[:, sl] + gmod[:, sl] * f)
    z = jnp.concatenate(zs, axis=1)
    o_ref[...] = _layer_norm_rows(z) * lng_ref[...] + lnb_ref[...]


def _combine_call(dest, ys, x, fsh, gate, mod3, ln_g, ln_b, *, tiles_per_batch, alpha):
    r, d = x.shape
    t = ROW_TILE
    row = lambda i: (i, 0)
    const = lambda i: (0, 0)
    kern = functools.partial(_combine_kernel, alpha=alpha)
    return pl.pallas_call(
        kern,
        grid=(r // t,),
        in_specs=[
            pl.BlockSpec((TOP_K, t), lambda i: (0, i), memory_space=pltpu.SMEM),
            pl.BlockSpec(memory_space=pl.ANY),
            pl.BlockSpec((t, d), row),
            pl.BlockSpec((t, d), row),
            pl.BlockSpec((TOP_K, t), lambda i: (0, i)),
            pl.BlockSpec((None, 1, 6 * d), lambda i: (_mod_row(i, tiles_per_batch), 0, 0)),
            pl.BlockSpec((1, d), const),
            pl.BlockSpec((1, d), const),
        ],
        out_specs=pl.BlockSpec((t, d), row),
        out_shape=jax.ShapeDtypeStruct((r, d), F32),
        scratch_shapes=[pltpu.VMEM((TOP_K * t, ROW_S, LANES), F32), pltpu.SemaphoreType.DMA(())],
        compiler_params=_cparams("arbitrary"),
        name="moe_combine",
    )(dest, ys, x, fsh, gate, mod3, ln_g, ln_b)


def _rope_tables(seq):
    rows = seq // GRID_W
    row = jnp.repeat(jnp.arange(rows, dtype=F32), GRID_W)
    col = jnp.tile(jnp.arange(GRID_W, dtype=F32), rows)
    nf = DA_DIM // 4
    freqs = ROPE_BASE ** (-jnp.arange(nf, dtype=F32) / nf)
    cr, sr = jnp.cos(row[:, None] * freqs), jnp.sin(row[:, None] * freqs)
    cc, sc = jnp.cos(col[:, None] * freqs), jnp.sin(col[:, None] * freqs)
    c64 = jnp.concatenate([cr, cr, cc, cc], axis=1)
    s64 = jnp.concatenate([-sr, sr, -sc, sc], axis=1)
    c = jnp.concatenate([jnp.tile(c64, (1, 2)), jnp.ones((CTX_LEN, LANES), F32)], axis=0)
    s = jnp.concatenate([jnp.tile(s64, (1, 2)), jnp.zeros((CTX_LEN, LANES), F32)], axis=0)
    return c, s


def kernel(x, c, ctx, c_ctx, w_mod, b_mod, w_in, w_out, diff_lambda, pool_w, pool_scale, ret_log_decay, ln_g, ln_b,
           w_router, router_bias, w_expert_gate_up, w_expert_down, w_shared_gate_up, w_shared_down):
    batch, seq, d = x.shape
    depth = w_mod.shape[0]
    assert d == D_MODEL and ctx.shape[1] == CTX_LEN == ROW_TILE and batch == 2
    assert seq % ROW_TILE == 0 and seq % GRID_W == 0 and w_in.shape[-1] == IN_WIDTH
    rows_per_batch = seq + CTX_LEN
    tiles_per_batch = rows_per_batch // ROW_TILE
    r = batch * rows_per_batch
    alpha = (2.0 * depth) ** 0.25

    xa = jnp.concatenate([x, ctx], axis=1).reshape(r, d)
    cvec = jnp.zeros((8, d), F32).at[0:batch].set(c).at[batch].set(c_ctx)
    mod_all = _mod_call(cvec, w_mod, b_mod)
    rope_c, rope_s = _rope_tables(seq)

    n_sorted = r * TOP_K + N_EXPERTS * EXPERT_BLOCK
    n_blocks = n_sorted // EXPERT_BLOCK

    for l in range(depth):
        lambda_init = 0.8 - 0.6 * math.exp(-0.3 * l)
        mod3 = mod_all[l].reshape(8, 1, 6 * d)
        lng = ln_g[l].reshape(2, 1, d)
        lnb = ln_b[l].reshape(2, 1, d)

        qk, vda, u, rqkv, rg = _inproj_call(xa, mod3, w_in[l].astype(BF16), rope_c, rope_s, tiles_per_batch)
        da = _attn_call(diff_lambda[l], qk, vda, batch=batch, rows_per_batch=rows_per_batch, seq=seq,
                        lambda_init=lambda_init)
        o_f, o_b = _ret_call(ret_log_decay[l], rqkv, batch=batch, rows_per_batch=rows_per_batch, seq=seq)
        pool_bd = jnp.zeros((POOL_WIDTH, POOL_WIDTH), F32)
        for gi in range(len(POOL_WINDOWS)):
            sl = slice(gi * POOL_GROUP, (gi + 1) * POOL_GROUP)
            pool_bd = pool_bd.at[sl, sl].set(pool_w[l, gi])
        xa = _mixout_call(xa, da, u, o_f, o_b, rg, mod3, w_out[l].astype(BF16), pool_bd.astype(BF16),
                          pool_scale[l].reshape(1, POOL_WIDTH), lng[0], lnb[0],
                          tiles_per_batch=tiles_per_batch, seq=seq, alpha=alpha)

        wr_t = w_router[l].T
        wr_hi = wr_t.astype(BF16)
        wr_lo = (wr_t - wr_hi.astype(F32)).astype(BF16)
        tokp, idx, gate, rank, cnt, fsh = _router_call(
            xa, mod3, wr_hi, wr_lo, router_bias[l].reshape(N_EXPERTS, 1),
            w_shared_gate_up[l].astype(BF16), w_shared_down[l].astype(BF16), tiles_per_batch=tiles_per_batch)
        counts = cnt[:, 0]
        padded = (counts + EXPERT_BLOCK - 1) // EXPERT_BLOCK * EXPERT_BLOCK
        pad_end = jnp.cumsum(padded)
        offs = pad_end - padded
        block_expert = jnp.minimum(
            jnp.searchsorted(pad_end, jnp.arange(n_blocks, dtype=jnp.int32) * EXPERT_BLOCK, side='right'),
            N_EXPERTS - 1).astype(jnp.int32)
        n_used = (pad_end[-1:] // EXPERT_BLOCK).astype(jnp.int32)
        dest = _dest_call(idx, rank, offs.reshape(N_EXPERTS, 1).astype(jnp.int32))
        xs = _dispatch_call(dest, tokp.reshape(r, PACK_S, LANES),
                            jnp.zeros((n_sorted, PACK_S, LANES), jnp.uint32))
        ys = _expert_call(block_expert, n_used, xs, w_expert_gate_up[l], w_expert_down[l])
        xa = _combine_call(dest, ys, xa, fsh, gate, mod3, lng[1], lnb[1],
                           tiles_per_batch=tiles_per_batch, alpha=alpha)

    return xa.reshape(batch, rows_per_batch, d)[:, :seq]
```

```python
import functools
import math

import jax
import jax.numpy as jnp
from jax import lax
from jax.experimental import pallas as pl
from jax.experimental.pallas import tpu as pltpu

F32 = jnp.float32
BF16 = jnp.bfloat16
HIGHEST = lax.Precision.HIGHEST

D_MODEL = 1024
CTX_LEN = 256
GRID_W = 64
DA_HEADS = 4
DA_DIM = 64
DA_VDIM = 2 * DA_DIM
DA_WIDTH = DA_HEADS * DA_VDIM
ROPE_BASE = 10000.0
POOL_WINDOWS = (2, 4, 8, 16)
POOL_GROUP = 64
POOL_WIDTH = len(POOL_WINDOWS) * POOL_GROUP
POOL_HALO = 8
RET_HEADS = 4
RET_DK = 64
RET_WIDTH = RET_HEADS * RET_DK
RET_CHUNK = 128
QK_WIDTH = 2 * DA_HEADS * 2 * DA_DIM
IN_WIDTH = QK_WIDTH + DA_WIDTH + POOL_WIDTH + 4 * RET_WIDTH
N_EXPERTS = 256
TOP_K = 8
N_GROUPS = 8
GROUP_SIZE = N_EXPERTS // N_GROUPS
TOPK_GROUPS = 4
EXPERT_HIDDEN = 256
ROUTED_SCALE = 2.5
LN_EPS = 1e-6
RMS_EPS = 1e-5

LANES = 128
ROW_TILE = 256
ATTN_Q_TILE = 256
ATTN_K_CHUNK = 256
ATTN_UNROLL = 4
EXPERT_BLOCK = 256
PACK_W = D_MODEL // 2
PACK_S = PACK_W // LANES
ROW_S = D_MODEL // LANES
VMEM_LIMIT = 56 * 1024 * 1024


def _cparams(*sem):
    return pltpu.CompilerParams(dimension_semantics=sem, vmem_limit_bytes=VMEM_LIMIT)


def _sigmoid(x):
    return 1.0 / (1.0 + jnp.exp(-x))


def _layer_norm_rows(x):
    mu = jnp.mean(x, axis=-1, keepdims=True)
    xc = x - mu
    var = jnp.mean(xc * xc, axis=-1, keepdims=True)
    return xc * lax.rsqrt(var + LN_EPS)


def _mod_row(i, tiles_per_batch):
    return jnp.where(i % tiles_per_batch == tiles_per_batch - 1, 2, i // tiles_per_batch)


def _mod_kernel(c_ref, w_ref, b_ref, o_ref):
    c = c_ref[...]
    s = c * _sigmoid(c)
    o_ref[...] = jnp.dot(s, w_ref[...], precision=HIGHEST, preferred_element_type=F32) + b_ref[...]


def _mod_call(cvec, w_mod, b_mod):
    depth, d, n = w_mod.shape
    tn = 1536
    return pl.pallas_call(
        _mod_kernel,
        grid=(depth, n // tn),
        in_specs=[
            pl.BlockSpec((8, d), lambda l, j: (0, 0)),
            pl.BlockSpec((None, d, tn), lambda l, j: (l, 0, j)),
            pl.BlockSpec((None, 1, tn), lambda l, j: (l, 0, j)),
        ],
        out_specs=pl.BlockSpec((None, 8, tn), lambda l, j: (l, 0, j)),
        out_shape=jax.ShapeDtypeStruct((depth, 8, n), F32),
        compiler_params=_cparams("arbitrary", "arbitrary"),
        name="mod",
    )(cvec, w_mod, b_mod.reshape(depth, 1, n))


def _inproj_kernel(x_ref, mod_ref, w_ref, wvt_ref, ct_ref, st_ref, qk_ref, vt_ref, u_ref, r_ref, g_ref):
    d = D_MODEL
    xn = _layer_norm_rows(x_ref[...])
    h = (xn * (1.0 + mod_ref[:, d:2 * d]) + mod_ref[:, 0:d]).astype(BF16)

    a = jnp.dot(h, w_ref[:, 0:QK_WIDTH], preferred_element_type=F32)
    lane = lax.broadcasted_iota(jnp.int32, (a.shape[0], LANES), 1)
    first_half = (lane % 32) < 16
    ct = ct_ref[...]
    st = st_ref[...]
    for s in range(QK_WIDTH // LANES):
        blk = a[:, s * LANES:(s + 1) * LANES]
        partner = jnp.where(first_half, pltpu.roll(blk, LANES - 16, 1), pltpu.roll(blk, 16, 1))
        rot = blk * ct + partner * st
        if s < QK_WIDTH // LANES // 2:
            rot = rot * (DA_DIM ** -0.5 * math.log2(math.e))
        qk_ref[:, s * LANES:(s + 1) * LANES] = rot.astype(BF16)

    vt_ref[...] = lax.dot_general(wvt_ref[...], h, (((1,), (1,)), ((), ())),
                                  preferred_element_type=F32).astype(BF16)
    o = QK_WIDTH + DA_WIDTH
    u_ref[...] = jnp.dot(h, w_ref[:, o:o + POOL_WIDTH], preferred_element_type=F32)
    o += POOL_WIDTH
    r = jnp.dot(h, w_ref[:, o:o + 3 * RET_WIDTH], preferred_element_type=F32)
    r_ref[:, 0:RET_WIDTH] = r[:, 0:RET_WIDTH].astype(BF16)
    r_ref[:, RET_WIDTH:2 * RET_WIDTH] = (r[:, RET_WIDTH:2 * RET_WIDTH] * (RET_DK ** -0.5)).astype(BF16)
    r_ref[:, 2 * RET_WIDTH:] = r[:, 2 * RET_WIDTH:].astype(BF16)
    o += 3 * RET_WIDTH
    g_ref[...] = jnp.dot(h, w_ref[:, o:o + RET_WIDTH], preferred_element_type=F32)


def _inproj_call(x, mod3, w_in_bf, w_vt_bf, rope_c, rope_s, tiles_per_batch):
    r, d = x.shape
    t = ROW_TILE
    nt = r // t
    row = lambda i: (i, 0)
    return pl.pallas_call(
        _inproj_kernel,
        grid=(nt,),
        in_specs=[
            pl.BlockSpec((t, d), row),
            pl.BlockSpec((None, 1, 6 * d), lambda i: (_mod_row(i, tiles_per_batch), 0, 0)),
            pl.BlockSpec((d, IN_WIDTH), lambda i: (0, 0)),
            pl.BlockSpec((DA_WIDTH, d), lambda i: (0, 0)),
            pl.BlockSpec((t, LANES), lambda i: (i % tiles_per_batch, 0)),
            pl.BlockSpec((t, LANES), lambda i: (i % tiles_per_batch, 0)),
        ],
        out_specs=[
            pl.BlockSpec((t, QK_WIDTH), row),
            pl.BlockSpec((DA_WIDTH, t), lambda i: (0, i)),
            pl.BlockSpec((t, POOL_WIDTH), row),
            pl.BlockSpec((t, 3 * RET_WIDTH), row),
            pl.BlockSpec((t, RET_WIDTH), row),
        ],
        out_shape=[
            jax.ShapeDtypeStruct((r, QK_WIDTH), BF16),
            jax.ShapeDtypeStruct((DA_WIDTH, r), BF16),
            jax.ShapeDtypeStruct((r, POOL_WIDTH), F32),
            jax.ShapeDtypeStruct((r, 3 * RET_WIDTH), BF16),
            jax.ShapeDtypeStruct((r, RET_WIDTH), F32),
        ],
        compiler_params=_cparams("arbitrary"),
        name="inproj",
    )(x, mod3, w_in_bf, w_vt_bf, rope_c, rope_s)


def _attn_kernel(lam_ref, q_ref, k_ref, vt_ref, o_ref, *, k_chunk, seq, lambda_init):
    q = q_ref[...]
    mq = q.shape[0]
    lane = lax.broadcasted_iota(jnp.int32, q.shape, 1)
    zero = jnp.zeros_like(q)
    q2 = jnp.concatenate([jnp.where(lane < DA_DIM, q, zero), jnp.where(lane >= DA_DIM, q, zero)], axis=0)
    qt = q2.astype(F32).T.astype(BF16)

    def scores(c):
        off = pl.multiple_of(c * k_chunk, k_chunk)
        return jnp.dot(k_ref[pl.ds(off, k_chunk), :], qt, preferred_element_type=F32)

    def update(c, s, m, l, acc):
        off = pl.multiple_of(c * k_chunk, k_chunk)
        vt = vt_ref[:, pl.ds(off, k_chunk)]
        m_new = jnp.maximum(m, jnp.max(s, axis=0, keepdims=True))
        alpha = jnp.exp2(m - m_new)
        p = jnp.exp2(s - m_new)
        l_new = alpha * l + jnp.sum(p, axis=0, keepdims=True)
        acc_new = alpha * acc + jnp.dot(vt, p.astype(BF16), preferred_element_type=F32)
        return m_new, l_new, acc_new

    n_chunks = (seq + CTX_LEN) // k_chunk
    n_latent_iters = (n_chunks - 1) // ATTN_UNROLL
    is_ctx_tile = pl.program_id(2) == pl.num_programs(2) - 1
    c0 = jnp.where(is_ctx_tile, n_chunks - 1, 0)

    def body(it, carry):
        s, m, l, acc = carry
        for u in range(ATTN_UNROLL):
            c = it * ATTN_UNROLL + u
            s_next = scores(c + 1)
            m, l, acc = update(c, s, m, l, acc)
            s = s_next
        return s, m, l, acc

    init = (scores(c0), jnp.full((1, 2 * mq), -jnp.inf, F32), jnp.zeros((1, 2 * mq), F32),
            jnp.zeros((DA_VDIM, 2 * mq), F32))
    s_last, m, l, acc = lax.fori_loop(0, jnp.where(is_ctx_tile, 0, n_latent_iters), body, init)
    _, l, acc = update(n_chunks - 1, s_last, m, l, acc)
    l0, l1 = l[:, 0:mq], l[:, mq:]
    a0, a1 = acc[:, 0:mq], acc[:, mq:]

    lv = lam_ref[...]
    lam = (jnp.exp(jnp.sum(lv[0:1] * lv[1:2], axis=-1, keepdims=True))
           - jnp.exp(jnp.sum(lv[2:3] * lv[3:4], axis=-1, keepdims=True)) + lambda_init)
    o = a0 / l0 - lam * (a1 / l1)
    o = o * lax.rsqrt(jnp.mean(o * o, axis=0, keepdims=True) + RMS_EPS) * (1.0 - lambda_init)
    o_ref[...] = o.T.astype(BF16)


def _attn_call(lam_vec, qk, vda, *, batch, rows_per_batch, seq, lambda_init):
    tq = ATTN_Q_TILE
    assert seq % (ATTN_K_CHUNK * ATTN_UNROLL) == 0 and rows_per_batch - seq == CTX_LEN == tq == ATTN_K_CHUNK
    nq = rows_per_batch // tq
    kern = functools.partial(_attn_kernel, k_chunk=ATTN_K_CHUNK, seq=seq, lambda_init=lambda_init)
    return pl.pallas_call(
        kern,
        grid=(batch, DA_HEADS, nq),
        in_specs=[
            pl.BlockSpec((4, DA_DIM), lambda b, h, i: (0, 0)),
            pl.BlockSpec((tq, DA_VDIM), lambda b, h, i: (b * nq + i, h)),
            pl.BlockSpec((rows_per_batch, DA_VDIM), lambda b, h, i: (b, DA_HEADS + h)),
            pl.BlockSpec((DA_VDIM, rows_per_batch), lambda b, h, i: (h, b)),
        ],
        out_specs=pl.BlockSpec((tq, DA_VDIM), lambda b, h, i: (b * nq + i, h)),
        out_shape=jax.ShapeDtypeStruct((qk.shape[0], DA_WIDTH), BF16),
        compiler_params=_cparams("arbitrary", "arbitrary", "arbitrary"),
        name="diff_attn",
    )(lam_vec, qk, qk, vda)


def _ret_kernel(ld_ref, f_ref, b_ref, of_ref, ob_ref, dm_ref, qd_ref, kd_ref, cd_ref, st_ref):
    c = pl.program_id(1)
    ch = RET_CHUNK
    w = RET_WIDTH
    lane_head = lax.broadcasted_iota(jnp.int32, (1, w), 1) // RET_DK

    @pl.when(c == 0)
    def _():
        st_ref[...] = jnp.zeros_like(st_ref)
        ri = lax.broadcasted_iota(jnp.int32, (ch, ch), 0)
        ci = lax.broadcasted_iota(jnp.int32, (ch, ch), 1)
        rowf = lax.broadcasted_iota(jnp.int32, (ch, w), 0).astype(F32)
        for d in range(2):
            lg_lane = jnp.zeros((1, w), F32)
            for hh in range(RET_HEADS):
                lg = -jnp.exp(jnp.full((1, 1), ld_ref[d, hh], F32))
                lg_lane = jnp.where(lane_head == hh, lg, lg_lane)
                dist = ((ri - ci) if d == 0 else (ci - ri)).astype(F32)
                dm_ref[d, hh] = jnp.where(dist >= 0, jnp.exp(dist * lg), 0.0)
            if d == 0:
                qd_ref[d] = jnp.exp((rowf + 1.0) * lg_lane)
                kd_ref[d] = jnp.exp((ch - 1.0 - rowf) * lg_lane)
            else:
                qd_ref[d] = jnp.exp((ch - rowf) * lg_lane)
                kd_ref[d] = jnp.exp(rowf * lg_lane)
            cd_ref[d] = jnp.exp(float(ch) * lg_lane)

    rblk = lax.broadcasted_iota(jnp.int32, (w, w), 0) // RET_DK
    cblk = lax.broadcasted_iota(jnp.int32, (w, w), 1) // RET_DK
    for d, (src, dst) in enumerate(((f_ref, of_ref), (b_ref, ob_ref))):
        q = src[:, 0:w]
        k = src[:, w:2 * w]
        v = src[:, 2 * w:3 * w]
        st = st_ref[d]
        o = jnp.dot((q.astype(F32) * qd_ref[d]).astype(BF16), st.astype(BF16), preferred_element_type=F32)
        for hh in range(RET_HEADS):
            in_head = lane_head == hh
            qm = jnp.where(in_head, q, jnp.zeros_like(q))
            s = lax.dot_general(qm, k, (((1,), (1,)), ((), ())), preferred_element_type=F32)
            intra = (s * dm_ref[d, hh]).astype(BF16)
            o = o + jnp.where(in_head, jnp.dot(intra, v, preferred_element_type=F32), 0.0)
        dst[...] = o
        kk_t = (k.astype(F32) * kd_ref[d]).T.astype(BF16)
        upd = jnp.dot(kk_t, v, preferred_element_type=F32)
        st_ref[d] = jnp.where(rblk == cblk, st * cd_ref[d] + upd, 0.0)


def _ret_call(log_decay, rqkv, *, batch, rows_per_batch, seq):
    ch = RET_CHUNK
    nc = rows_per_batch // ch
    n_lat = seq // ch
    n_ctx = nc - n_lat

    def fwd(b, c):
        return (b * nc + jnp.where(c < n_ctx, n_lat + c, c - n_ctx), 0)

    def bwd(b, c):
        return (b * nc + nc - 1 - c, 0)

    w = RET_WIDTH
    return pl.pallas_call(
        _ret_kernel,
        grid=(batch, nc),
        in_specs=[
            pl.BlockSpec(memory_space=pltpu.SMEM),
            pl.BlockSpec((ch, 3 * w), fwd),
            pl.BlockSpec((ch, 3 * w), bwd),
        ],
        out_specs=[pl.BlockSpec((ch, w), fwd), pl.BlockSpec((ch, w), bwd)],
        out_shape=[jax.ShapeDtypeStruct((rqkv.shape[0], w), F32)] * 2,
        scratch_shapes=[
            pltpu.VMEM((2, RET_HEADS, ch, ch), F32),
            pltpu.VMEM((2, ch, w), F32),
            pltpu.VMEM((2, ch, w), F32),
            pltpu.VMEM((2, 1, w), F32),
            pltpu.VMEM((2, w, w), F32),
        ],
        compiler_params=_cparams("arbitrary", "arbitrary"),
        name="retention",
    )(log_decay, rqkv, rqkv)


def _mixout_kernel(x_ref, da_ref, u_ref, up_ref, un_ref, of_ref, ob_ref, rg_ref, mod_ref, wo_ref, pw_ref,
                   ps_ref, lng_ref, lnb_ref, o_ref, *, tiles_per_batch, seq, alpha):
    d = D_MODEL
    t = x_ref.shape[0]
    i = pl.program_id(0)
    j = i % tiles_per_batch
    is_ctx = j == tiles_per_batch - 1
    stream_len = jnp.where(is_ctx, CTX_LEN, seq)
    p0 = jnp.where(is_ctx, 0, j * t)

    u = u_ref[...]
    prev = jnp.where(p0 > 0, up_ref[...], 0.0)
    nxt = jnp.where(p0 + t < stream_len, un_ref[...], 0.0)
    ext = jnp.concatenate([prev, u, nxt], axis=0)
    n = t + 2 * POOL_HALO
    a2 = ext + pltpu.roll(ext, 1, 0)
    a4 = pltpu.roll(a2, 1, 0) + pltpu.roll(a2, n - 1, 0)
    a8 = pltpu.roll(a4, 2, 0) + pltpu.roll(a4, n - 2, 0)
    a16 = pltpu.roll(a8, 4, 0) + pltpu.roll(a8, n - 4, 0)
    pos = p0 + lax.broadcasted_iota(jnp.int32, (t, POOL_WIDTH), 0)
    group = lax.broadcasted_iota(jnp.int32, (1, POOL_WIDTH), 1) // POOL_GROUP
    mean = jnp.zeros((t, POOL_WIDTH), F32)
    for gi, (wnd, asum) in enumerate(zip(POOL_WINDOWS, (a2, a4, a8, a16))):
        cnt = jnp.minimum(pos + wnd // 2, stream_len) - jnp.maximum(pos - wnd // 2, 0)
        mean = jnp.where(group == gi, asum[POOL_HALO:POOL_HALO + t] / cnt.astype(F32), mean)
    pool = jnp.dot((mean - u).astype(BF16), pw_ref[...], preferred_element_type=F32) * ps_ref[...]

    o = of_ref[...] + ob_ref[...]
    head = lax.broadcasted_iota(jnp.int32, (1, RET_WIDTH), 1) // RET_DK

    def head_mean(val):
        out = jnp.zeros_like(val)
        for hh in range(RET_HEADS):
            m = jnp.sum(jnp.where(head == hh, val, 0.0), axis=-1, keepdims=True) * (1.0 / RET_DK)
            out = jnp.where(head == hh, m, out)
        return out

    oc = o - head_mean(o)
    rn = oc * lax.rsqrt(head_mean(oc * oc) + LN_EPS)
    g = rg_ref[...]
    ret = rn * (g * _sigmoid(g))

    y = jnp.dot(da_ref[...], wo_ref[0:DA_WIDTH, :], preferred_element_type=F32)
    y = y + jnp.dot(pool.astype(BF16), wo_ref[DA_WIDTH:DA_WIDTH + POOL_WIDTH, :], preferred_element_type=F32)
    y = y + jnp.dot(ret.astype(BF16), wo_ref[DA_WIDTH + POOL_WIDTH:, :], preferred_element_type=F32)
    z = alpha * x_ref[...] + mod_ref[:, 2 * d:3 * d] * y
    o_ref[...] = _layer_norm_rows(z) * lng_ref[...] + lnb_ref[...]


def _mixout_call(x, da, u, o_f, o_b, rg, mod3, w_out_bf, pool_bd, pool_scale, ln_g, ln_b, *, tiles_per_batch, seq,
                 alpha):
    r, d = x.shape
    t = ROW_TILE
    nt = r // t
    hb = t // POOL_HALO
    n_halo_blocks = r // POOL_HALO
    row = lambda i: (i, 0)
    const = lambda i: (0, 0)
    kern = functools.partial(_mixout_kernel, tiles_per_batch=tiles_per_batch, seq=seq, alpha=alpha)
    return pl.pallas_call(
        kern,
        grid=(nt,),
        in_specs=[
            pl.BlockSpec((t, d), row),
            pl.BlockSpec((t, DA_WIDTH), row),
            pl.BlockSpec((t, POOL_WIDTH), row),
            pl.BlockSpec((POOL_HALO, POOL_WIDTH), lambda i: (jnp.maximum(i * hb - 1, 0), 0)),
            pl.BlockSpec((POOL_HALO, POOL_WIDTH), lambda i: (jnp.minimum((i + 1) * hb, n_halo_blocks - 1), 0)),
            pl.BlockSpec((t, RET_WIDTH), row),
            pl.BlockSpec((t, RET_WIDTH), row),
            pl.BlockSpec((t, RET_WIDTH), row),
            pl.BlockSpec((None, 1, 6 * d), lambda i: (_mod_row(i, tiles_per_batch), 0, 0)),
            pl.BlockSpec((d, d), const),
            pl.BlockSpec((POOL_WIDTH, POOL_WIDTH), const),
            pl.BlockSpec((1, POOL_WIDTH), const),
            pl.BlockSpec((1, d), const),
            pl.BlockSpec((1, d), const),
        ],
        out_specs=pl.BlockSpec((t, d), row),
        out_shape=jax.ShapeDtypeStruct((r, d), F32),
        compiler_params=_cparams("arbitrary"),
        name="mixer_out",
    )(x, da, u, u, u, o_f, o_b, rg, mod3, w_out_bf, pool_bd, pool_scale, ln_g, ln_b)


def _router_kernel(x_ref, mod_ref, wrh_ref, wrl_ref, bias_ref, wsgu_ref, wsdn_ref,
                   tokp_ref, idx_ref, gate_ref, rank_ref, cnt_ref, fsh_ref, carry_ref):
    d = D_MODEL
    t = x_ref.shape[0]
    ne = N_EXPERTS
    neg = -jnp.inf

    @pl.when(pl.program_id(0) == 0)
    def _():
        carry_ref[...] = jnp.zeros_like(carry_ref)

    tok = _layer_norm_rows(x_ref[...]) * (1.0 + mod_ref[:, 4 * d:5 * d]) + mod_ref[:, 3 * d:4 * d]
    tok_hi = tok.astype(BF16)
    tok_lo = (tok - tok_hi.astype(F32)).astype(BF16)

    bits = pltpu.bitcast(tok_hi.astype(F32), jnp.uint32)
    tokp_ref[...] = (lax.shift_right_logical(bits[:, 0:PACK_W], jnp.uint32(16))
                     | (bits[:, PACK_W:] & jnp.uint32(0xFFFF0000)))

    hs = jnp.dot(tok_hi, wsgu_ref[...], preferred_element_type=F32)
    gs, us = hs[:, 0:EXPERT_HIDDEN], hs[:, EXPERT_HIDDEN:]
    fsh_ref[...] = jnp.dot((gs * _sigmoid(gs) * us).astype(BF16), wsdn_ref[...], preferred_element_type=F32)

    nt_dims = (((1,), (1,)), ((), ()))
    logits = (lax.dot_general(wrh_ref[...], tok_hi, nt_dims, preferred_element_type=F32)
              + lax.dot_general(wrh_ref[...], tok_lo, nt_dims, preferred_element_type=F32)
              + lax.dot_general(wrl_ref[...], tok_hi, nt_dims, preferred_element_type=F32))
    scores = _sigmoid(logits)
    biased = scores + bias_ref[...]

    gidx = lax.broadcasted_iota(jnp.int32, (GROUP_SIZE, t), 0)
    blocks, gscores = [], []
    for g in range(N_GROUPS):
        blk = biased[g * GROUP_SIZE:(g + 1) * GROUP_SIZE, :]
        m1 = jnp.max(blk, axis=0, keepdims=True)
        first = jnp.min(jnp.where(blk == m1, gidx, GROUP_SIZE), axis=0, keepdims=True)
        m2 = jnp.max(jnp.where(gidx == first, neg, blk), axis=0, keepdims=True)
        blocks.append(blk)
        gscores.append(m1 + m2)

    keep = [jnp.zeros((1, t), F32) for _ in range(N_GROUPS)]
    for _ in range(TOPK_GROUPS):
        m = gscores[0]
        for gs_ in gscores[1:]:
            m = jnp.maximum(m, gs_)
        found = jnp.zeros((1, t), F32)
        for g in range(N_GROUPS):
            hit = jnp.where(gscores[g] == m, 1.0 - found, 0.0)
            found = found + hit
            keep[g] = keep[g] + hit
            gscores[g] = jnp.where(hit > 0.0, neg, gscores[g])
    masked = jnp.concatenate([jnp.where(keep[g] > 0.0, blocks[g], neg) for g in range(N_GROUPS)], axis=0)

    ei = lax.broadcasted_iota(jnp.int32, (ne, t), 0)
    cur = masked
    onehot = jnp.zeros((ne, t), F32)
    idxs, gates = [], []
    for _ in range(TOP_K):
        m = jnp.max(cur, axis=0, keepdims=True)
        ii = jnp.min(jnp.where(cur == m, ei, ne), axis=0, keepdims=True)
        sel = ei == ii
        idxs.append(ii)
        gates.append(jnp.sum(jnp.where(sel, scores, 0.0), axis=0, keepdims=True))
        onehot = jnp.where(sel, 1.0, onehot)
        cur = jnp.where(sel, neg, cur)
    gsum = gates[0]
    for gk in gates[1:]:
        gsum = gsum + gk
    for k in range(TOP_K):
        idx_ref[k:k + 1, :] = idxs[k]
        gate_ref[k:k + 1, :] = gates[k] / gsum * ROUTED_SCALE

    ti = lax.broadcasted_iota(jnp.int32, (t, t), 0)
    tj = lax.broadcasted_iota(jnp.int32, (t, t), 1)
    before = jnp.where(ti < tj, 1.0, 0.0).astype(BF16)
    prefix = jnp.dot(onehot.astype(BF16), before, preferred_element_type=F32) + carry_ref[:, 0:1]
    for k in range(TOP_K):
        rank_k = jnp.sum(jnp.where(ei == idxs[k], prefix, 0.0), axis=0, keepdims=True)
        rank_ref[k:k + 1, :] = rank_k.astype(jnp.int32)
    carry_ref[...] = carry_ref[...] + jnp.sum(onehot, axis=1, keepdims=True)
    cnt_ref[...] = carry_ref[...].astype(jnp.int32)


def _router_call(x, mod3, wr_hi, wr_lo, bias_col, ws_gu_bf, ws_dn_bf, *, tiles_per_batch):
    r, d = x.shape
    t = ROW_TILE
    nt = r // t
    row = lambda i: (i, 0)
    col = lambda i: (0, i)
    const = lambda i: (0, 0)
    return pl.pallas_call(
        _router_kernel,
        grid=(nt,),
        in_specs=[
            pl.BlockSpec((t, d), row),
            pl.BlockSpec((None, 1, 6 * d), lambda i: (_mod_row(i, tiles_per_batch), 0, 0)),
            pl.BlockSpec((N_EXPERTS, d), const),
            pl.BlockSpec((N_EXPERTS, d), const),
            pl.BlockSpec((N_EXPERTS, 1), const),
            pl.BlockSpec((d, 2 * EXPERT_HIDDEN), const),
            pl.BlockSpec((EXPERT_HIDDEN, d), const),
        ],
        out_specs=[
            pl.BlockSpec((t, PACK_W), row),
            pl.BlockSpec((TOP_K, t), col),
            pl.BlockSpec((TOP_K, t), col),
            pl.BlockSpec((TOP_K, t), col),
            pl.BlockSpec((N_EXPERTS, LANES), const),
            pl.BlockSpec((t, d), row),
        ],
        out_shape=[
            jax.ShapeDtypeStruct((r, PACK_W), jnp.uint32),
            jax.ShapeDtypeStruct((TOP_K, r), jnp.int32),
            jax.ShapeDtypeStruct((TOP_K, r), F32),
            jax.ShapeDtypeStruct((TOP_K, r), jnp.int32),
            jax.ShapeDtypeStruct((N_EXPERTS, LANES), jnp.int32),
            jax.ShapeDtypeStruct((r, d), F32),
        ],
        scratch_shapes=[pltpu.VMEM((N_EXPERTS, LANES), F32)],
        compiler_params=_cparams("arbitrary"),
        name="router",
    )(x, mod3, wr_hi, wr_lo, bias_col, ws_gu_bf, ws_dn_bf)


def _dest_kernel(idx_ref, rank_ref, offs_ref, dest_ref):
    t = idx_ref.shape[1]
    ei = lax.broadcasted_iota(jnp.int32, (N_EXPERTS, t), 0)
    offs = offs_ref[...].astype(F32)
    for k in range(TOP_K):
        start = jnp.sum(jnp.where(ei == idx_ref[k:k + 1, :], offs, 0.0), axis=0, keepdims=True)
        dest_ref[k:k + 1, :] = start.astype(jnp.int32) + rank_ref[k:k + 1, :]


def _dest_call(idx, rank, offs_col):
    r = idx.shape[1]
    t = ROW_TILE
    col = lambda i: (0, i)
    return pl.pallas_call(
        _dest_kernel,
        grid=(r // t,),
        in_specs=[pl.BlockSpec((TOP_K, t), col), pl.BlockSpec((TOP_K, t), col),
                  pl.BlockSpec((N_EXPERTS, 1), lambda i: (0, 0))],
        out_specs=pl.BlockSpec((TOP_K, t), col),
        out_shape=jax.ShapeDtypeStruct((TOP_K, r), jnp.int32),
        compiler_params=_cparams("arbitrary"),
        name="moe_dest",
    )(idx, rank, offs_col)


def _dispatch_kernel(dest_ref, tokp_ref, xs_in_ref, xs_ref, sem):
    del xs_in_ref
    t = dest_ref.shape[1]

    def row_copy(tt, k):
        return pltpu.make_async_copy(tokp_ref.at[tt], xs_ref.at[dest_ref[k, tt]], sem)

    def issue(tt, carry):
        for k in range(TOP_K):
            row_copy(tt, k).start()
        return carry

    lax.fori_loop(0, t, issue, 0)

    def drain(tt, carry):
        for k in range(TOP_K):
            row_copy(tt, k).wait()
        return carry

    lax.fori_loop(0, t, drain, 0)


def _dispatch_call(dest, tokp3, xs_zero):
    r = dest.shape[1]
    t = ROW_TILE
    return pl.pallas_call(
        _dispatch_kernel,
        grid=(r // t,),
        in_specs=[
            pl.BlockSpec((TOP_K, t), lambda i: (0, i), memory_space=pltpu.SMEM),
            pl.BlockSpec((t, PACK_S, LANES), lambda i: (i, 0, 0)),
            pl.BlockSpec(memory_space=pl.ANY),
        ],
        out_specs=pl.BlockSpec(memory_space=pl.ANY),
        out_shape=jax.ShapeDtypeStruct(xs_zero.shape, xs_zero.dtype),
        scratch_shapes=[pltpu.SemaphoreType.DMA(())],
        input_output_aliases={2: 0},
        compiler_params=_cparams("arbitrary"),
        name="moe_dispatch",
    )(dest, tokp3, xs_zero)


def _expert_kernel(be_ref, nb_ref, xs_ref, wgu_ref, wdn_ref, ys_ref, wgu_bf, wdn_bf):
    j = pl.program_id(0)

    @pl.when(j < nb_ref[0])
    def _():
        changed = jnp.logical_or(j == 0, be_ref[j] != be_ref[jnp.maximum(j - 1, 0)])

        @pl.when(changed)
        def _():
            wgu_bf[...] = wgu_ref[...].astype(BF16)
            wdn_bf[...] = wdn_ref[...].astype(BF16)

        def unpack(word):
            lo = pltpu.bitcast(lax.shift_left(word, jnp.uint32(16)), F32)
            hi = pltpu.bitcast(word & jnp.uint32(0xFFFF0000), F32)
            return lo.astype(BF16), hi.astype(BF16)

        parts = [unpack(xs_ref[:, s, :]) for s in range(PACK_S)]
        x_lo = jnp.concatenate([p[0] for p in parts], axis=1)
        x_hi = jnp.concatenate([p[1] for p in parts], axis=1)
        h = (jnp.dot(x_lo, wgu_bf[0:PACK_W, :], preferred_element_type=F32)
             + jnp.dot(x_hi, wgu_bf[PACK_W:, :], preferred_element_type=F32))
        g, u = h[:, 0:EXPERT_HIDDEN], h[:, EXPERT_HIDDEN:]
        y = jnp.dot((g * _sigmoid(g) * u).astype(BF16), wdn_bf[...], preferred_element_type=F32)
        for s in range(ROW_S):
            ys_ref[:, s, :] = y[:, s * LANES:(s + 1) * LANES]


def _expert_call(block_expert, n_blocks_used, xs, w_gu, w_dn, layer):
    n_rows = xs.shape[0]
    bm = EXPERT_BLOCK
    d = D_MODEL
    grid_spec = pltpu.PrefetchScalarGridSpec(
        num_scalar_prefetch=2,
        grid=(n_rows // bm,),
        in_specs=[
            pl.BlockSpec((bm, PACK_S, LANES), lambda j, be, nb: (j, 0, 0)),
            pl.BlockSpec((None, None, d, 2 * EXPERT_HIDDEN), lambda j, be, nb: (layer, be[j], 0, 0)),
            pl.BlockSpec((None, None, EXPERT_HIDDEN, d), lambda j, be, nb: (layer, be[j], 0, 0)),
        ],
        out_specs=pl.BlockSpec((bm, ROW_S, LANES), lambda j, be, nb: (j, 0, 0)),
        scratch_shapes=[pltpu.VMEM((d, 2 * EXPERT_HIDDEN), BF16), pltpu.VMEM((EXPERT_HIDDEN, d), BF16)],
    )
    return pl.pallas_call(
        _expert_kernel,
        grid_spec=grid_spec,
        out_shape=jax.ShapeDtypeStruct((n_rows, ROW_S, LANES), F32),
        compiler_params=_cparams("arbitrary"),
        name="moe_experts",
    )(block_expert, n_blocks_used, xs, w_gu, w_dn)


def _combine_kernel(dest_ref, ys_ref, x_ref, fsh_ref, gate_ref, mod_ref, lng_ref, lnb_ref, o_ref, buf, sem, *,
                    alpha):
    d = D_MODEL
    t = x_ref.shape[0]

    def row_copy(tt, k):
        return pltpu.make_async_copy(ys_ref.at[dest_ref[k, tt]], buf.at[k * t + tt], sem)

    def issue(tt, carry):
        for k in range(TOP_K):
            row_copy(tt, k).start()
        return carry

    lax.fori_loop(0, t, issue, 0)

    def drain(tt, carry):
        for k in range(TOP_K):
            row_copy(tt, k).wait()
        return carry

    lax.fori_loop(0, t, drain, 0)

    gate_rows = gate_ref[...]
    pad = jnp.zeros((LANES - TOP_K, t), F32)
    gate_cols = jnp.concatenate([gate_rows, pad], axis=0).T
    x = x_ref[...]
    fsh = fsh_ref[...]
    gmod = mod_ref[:, 5 * d:6 * d]
    zs = []
    for s in range(ROW_S):
        sl = slice(s * LANES, (s + 1) * LANES)
        f = fsh[:, sl]
        for k in range(TOP_K):
            f = f + gate_cols[:, k:k + 1] * buf[pl.ds(k * t, t), s, :]
        zs.append(alpha * x[:, sl] + gmod[:, sl] * f)
    z = jnp.concatenate(zs, axis=1)
    o_ref[...] = _layer_norm_rows(z) * lng_ref[...] + lnb_ref[...]


def _combine_call(dest, ys, x, fsh, gate, mod3, ln_g, ln_b, *, tiles_per_batch, alpha):
    r, d = x.shape
    t = ROW_TILE
    row = lambda i: (i, 0)
    const = lambda i: (0, 0)
    kern = functools.partial(_combine_kernel, alpha=alpha)
    return pl.pallas_call(
        kern,
        grid=(r // t,),
        in_specs=[
            pl.BlockSpec((TOP_K, t), lambda i: (0, i), memory_space=pltpu.SMEM),
            pl.BlockSpec(memory_space=pl.ANY),
            pl.BlockSpec((t, d), row),
            pl.BlockSpec((t, d), row),
            pl.BlockSpec((TOP_K, t), lambda i: (0, i)),
            pl.BlockSpec((None, 1, 6 * d), lambda i: (_mod_row(i, tiles_per_batch), 0, 0)),
            pl.BlockSpec((1, d), const),
            pl.BlockSpec((1, d), const),
        ],
        out_specs=pl.BlockSpec((t, d), row),
        out_shape=jax.ShapeDtypeStruct((r, d), F32),
        scratch_shapes=[pltpu.VMEM((TOP_K * t, ROW_S, LANES), F32), pltpu.SemaphoreType.DMA(())],
        compiler_params=_cparams("arbitrary"),
        name="moe_combine",
    )(dest, ys, x, fsh, gate, mod3, ln_g, ln_b)


def _rope_tables(seq):
    rows = seq // GRID_W
    row = jnp.repeat(jnp.arange(rows, dtype=F32), GRID_W)
    col = jnp.tile(jnp.arange(GRID_W, dtype=F32), rows)
    nf = DA_DIM // 4
    freqs = ROPE_BASE ** (-jnp.arange(nf, dtype=F32) / nf)
    cr, sr = jnp.cos(row[:, None] * freqs), jnp.sin(row[:, None] * freqs)
    cc, sc = jnp.cos(col[:, None] * freqs), jnp.sin(col[:, None] * freqs)
    c64 = jnp.concatenate([cr, cr, cc, cc], axis=1)
    s64 = jnp.concatenate([-sr, sr, -sc, sc], axis=1)
    c = jnp.concatenate([jnp.tile(c64, (1, 2)), jnp.ones((CTX_LEN, LANES), F32)], axis=0)
    s = jnp.concatenate([jnp.tile(s64, (1, 2)), jnp.zeros((CTX_LEN, LANES), F32)], axis=0)
    return c, s


def kernel(x, c, ctx, c_ctx, w_mod, b_mod, w_in, w_out, diff_lambda, pool_w, pool_scale, ret_log_decay, ln_g, ln_b,
           w_router, router_bias, w_expert_gate_up, w_expert_down, w_shared_gate_up, w_shared_down):
    batch, seq, d = x.shape
    depth = w_mod.shape[0]
    assert d == D_MODEL and ctx.shape[1] == CTX_LEN == ROW_TILE and batch == 2
    assert seq % ROW_TILE == 0 and seq % GRID_W == 0 and w_in.shape[-1] == IN_WIDTH
    rows_per_batch = seq + CTX_LEN
    tiles_per_batch = rows_per_batch // ROW_TILE
    r = batch * rows_per_batch
    alpha = (2.0 * depth) ** 0.25

    xa = jnp.concatenate([x, ctx], axis=1).reshape(r, d)
    cvec = jnp.zeros((8, d), F32).at[0:batch].set(c).at[batch].set(c_ctx)
    mod_all = _mod_call(cvec, w_mod, b_mod)
    rope_c, rope_s = _rope_tables(seq)

    n_sorted = r * TOP_K + N_EXPERTS * EXPERT_BLOCK
    n_blocks = n_sorted // EXPERT_BLOCK

    for l in range(depth):
        lambda_init = 0.8 - 0.6 * math.exp(-0.3 * l)
        mod3 = mod_all[l].reshape(8, 1, 6 * d)
        lng = ln_g[l].reshape(2, 1, d)
        lnb = ln_b[l].reshape(2, 1, d)

        w_in_bf = w_in[l].astype(BF16)
        w_vt_bf = w_in_bf[:, QK_WIDTH:QK_WIDTH + DA_WIDTH].T
        qk, vda, u, rqkv, rg = _inproj_call(xa, mod3, w_in_bf, w_vt_bf, rope_c, rope_s, tiles_per_batch)
        da = _attn_call(diff_lambda[l], qk, vda, batch=batch, rows_per_batch=rows_per_batch, seq=seq,
                        lambda_init=lambda_init)
        o_f, o_b = _ret_call(ret_log_decay[l], rqkv, batch=batch, rows_per_batch=rows_per_batch, seq=seq)
        pool_bd = jnp.zeros((POOL_WIDTH, POOL_WIDTH), F32)
        for gi in range(len(POOL_WINDOWS)):
            sl = slice(gi * POOL_GROUP, (gi + 1) * POOL_GROUP)
            pool_bd = pool_bd.at[sl, sl].set(pool_w[l, gi])
        xa = _mixout_call(xa, da, u, o_f, o_b, rg, mod3, w_out[l].astype(BF16), pool_bd.astype(BF16),
                          pool_scale[l].reshape(1, POOL_WIDTH), lng[0], lnb[0],
                          tiles_per_batch=tiles_per_batch, seq=seq, alpha=alpha)

        wr_t = w_router[l].T
        wr_hi = wr_t.astype(BF16)
        wr_lo = (wr_t - wr_hi.astype(F32)).astype(BF16)
        tokp, idx, gate, rank, cnt, fsh = _router_call(
            xa, mod3, wr_hi, wr_lo, router_bias[l].reshape(N_EXPERTS, 1),
            w_shared_gate_up[l].astype(BF16), w_shared_down[l].astype(BF16), tiles_per_batch=tiles_per_batch)
        counts = cnt[:, 0]
        padded = (counts + EXPERT_BLOCK - 1) // EXPERT_BLOCK * EXPERT_BLOCK
        pad_end = jnp.cumsum(padded)
        offs = pad_end - padded
        block_expert = jnp.minimum(
            jnp.searchsorted(pad_end, jnp.arange(n_blocks, dtype=jnp.int32) * EXPERT_BLOCK, side='right'),
            N_EXPERTS - 1).astype(jnp.int32)
        n_used = (pad_end[-1:] // EXPERT_BLOCK).astype(jnp.int32)
        dest = _dest_call(idx, rank, offs.reshape(N_EXPERTS, 1).astype(jnp.int32))
        xs = _dispatch_call(dest, tokp.reshape(r, PACK_S, LANES),
                            jnp.zeros((n_sorted, PACK_S, LANES), jnp.uint32))
        ys = _expert_call(block_expert, n_used, xs, w_expert_gate_up, w_expert_down, l)
        xa = _combine_call(dest, ys, xa, fsh, gate, mod3, lng[1], lnb[1],
                           tiles_per_batch=tiles_per_batch, alpha=alpha)

    return xa.reshape(batch, rows_per_batch, d)[:, :seq]
```

```python
import functools
import math

import jax
import jax.numpy as jnp
from jax import lax
from jax.experimental import pallas as pl
from jax.experimental.pallas import tpu as pltpu

F32 = jnp.float32
BF16 = jnp.bfloat16
HIGHEST = lax.Precision.HIGHEST

D_MODEL = 1024
CTX_LEN = 256
GRID_W = 64
DA_HEADS = 4
DA_DIM = 64
DA_VDIM = 2 * DA_DIM
DA_WIDTH = DA_HEADS * DA_VDIM
ROPE_BASE = 10000.0
POOL_WINDOWS = (2, 4, 8, 16)
POOL_GROUP = 64
POOL_WIDTH = len(POOL_WINDOWS) * POOL_GROUP
POOL_HALO = 8
RET_HEADS = 4
RET_DK = 64
RET_WIDTH = RET_HEADS * RET_DK
RET_CHUNK = 128
QK_WIDTH = 2 * DA_HEADS * 2 * DA_DIM
IN_WIDTH = QK_WIDTH + DA_WIDTH + POOL_WIDTH + 4 * RET_WIDTH
N_EXPERTS = 256
TOP_K = 8
N_GROUPS = 8
GROUP_SIZE = N_EXPERTS // N_GROUPS
TOPK_GROUPS = 4
EXPERT_HIDDEN = 256
ROUTED_SCALE = 2.5
LN_EPS = 1e-6
RMS_EPS = 1e-5

LANES = 128
ROW_TILE = 256
ATTN_Q_TILE = 256
ATTN_K_CHUNK = 256
ATTN_UNROLL = 4
EXPERT_BLOCK = 256
PACK_W = D_MODEL // 2
PACK_S = PACK_W // LANES
ROW_S = D_MODEL // LANES
VMEM_LIMIT = 56 * 1024 * 1024


def _cparams(*sem):
    return pltpu.CompilerParams(dimension_semantics=sem, vmem_limit_bytes=VMEM_LIMIT)


def _sigmoid(x):
    return 1.0 / (1.0 + jnp.exp(-x))


def _layer_norm_rows(x):
    mu = jnp.mean(x, axis=-1, keepdims=True)
    xc = x - mu
    var = jnp.mean(xc * xc, axis=-1, keepdims=True)
    return xc * lax.rsqrt(var + LN_EPS)


def _mod_row(i, tiles_per_batch):
    return jnp.where(i % tiles_per_batch == tiles_per_batch - 1, 2, i // tiles_per_batch)


def _mod_kernel(c_ref, w_ref, b_ref, o_ref):
    c = c_ref[...]
    s = c * _sigmoid(c)
    o_ref[...] = jnp.dot(s, w_ref[...], precision=HIGHEST, preferred_element_type=F32) + b_ref[...]


def _mod_call(cvec, w_mod, b_mod):
    depth, d, n = w_mod.shape
    tn = 1536
    return pl.pallas_call(
        _mod_kernel,
        grid=(depth, n // tn),
        in_specs=[
            pl.BlockSpec((8, d), lambda l, j: (0, 0)),
            pl.BlockSpec((None, d, tn), lambda l, j: (l, 0, j)),
            pl.BlockSpec((None, 1, tn), lambda l, j: (l, 0, j)),
        ],
        out_specs=pl.BlockSpec((None, 8, tn), lambda l, j: (l, 0, j)),
        out_shape=jax.ShapeDtypeStruct((depth, 8, n), F32),
        compiler_params=_cparams("arbitrary", "arbitrary"),
        name="mod",
    )(cvec, w_mod, b_mod.reshape(depth, 1, n))


def _inproj_kernel(x_ref, mod_ref, w_ref, wvt_ref, ct_ref, st_ref, qk_ref, vt_ref, u_ref, r_ref, g_ref):
    d = D_MODEL
    xn = _layer_norm_rows(x_ref[...])
    h = (xn * (1.0 + mod_ref[:, d:2 * d]) + mod_ref[:, 0:d]).astype(BF16)

    a = jnp.dot(h, w_ref[:, 0:QK_WIDTH], preferred_element_type=F32)
    lane = lax.broadcasted_iota(jnp.int32, (a.shape[0], LANES), 1)
    first_half = (lane % 32) < 16
    ct = ct_ref[...]
    st = st_ref[...]
    for s in range(QK_WIDTH // LANES):
        blk = a[:, s * LANES:(s + 1) * LANES]
        partner = jnp.where(first_half, pltpu.roll(blk, LANES - 16, 1), pltpu.roll(blk, 16, 1))
        rot = blk * ct + partner * st
        if s < QK_WIDTH // LANES // 2:
            rot = rot * (DA_DIM ** -0.5 * math.log2(math.e))
        qk_ref[:, s * LANES:(s + 1) * LANES] = rot.astype(BF16)

    vt_ref[...] = lax.dot_general(wvt_ref[...], h, (((1,), (1,)), ((), ())),
                                  preferred_element_type=F32).astype(BF16)
    o = QK_WIDTH + DA_WIDTH
    u_ref[...] = jnp.dot(h, w_ref[:, o:o + POOL_WIDTH], preferred_element_type=F32)
    o += POOL_WIDTH
    r = jnp.dot(h, w_ref[:, o:o + 3 * RET_WIDTH], preferred_element_type=F32)
    r_ref[:, 0:RET_WIDTH] = r[:, 0:RET_WIDTH].astype(BF16)
    r_ref[:, RET_WIDTH:2 * RET_WIDTH] = (r[:, RET_WIDTH:2 * RET_WIDTH] * (RET_DK ** -0.5)).astype(BF16)
    r_ref[:, 2 * RET_WIDTH:] = r[:, 2 * RET_WIDTH:].astype(BF16)
    o += 3 * RET_WIDTH
    g_ref[...] = jnp.dot(h, w_ref[:, o:o + RET_WIDTH], preferred_element_type=F32)


def _inproj_call(x, mod3, w_in_bf, w_vt_bf, rope_c, rope_s, tiles_per_batch):
    r, d = x.shape
    t = ROW_TILE
    nt = r // t
    row = lambda i: (i, 0)
    return pl.pallas_call(
        _inproj_kernel,
        grid=(nt,),
        in_specs=[
            pl.BlockSpec((t, d), row),
            pl.BlockSpec((None, 1, 6 * d), lambda i: (_mod_row(i, tiles_per_batch), 0, 0)),
            pl.BlockSpec((d, IN_WIDTH), lambda i: (0, 0)),
            pl.BlockSpec((DA_WIDTH, d), lambda i: (0, 0)),
            pl.BlockSpec((t, LANES), lambda i: (i % tiles_per_batch, 0)),
            pl.BlockSpec((t, LANES), lambda i: (i % tiles_per_batch, 0)),
        ],
        out_specs=[
            pl.BlockSpec((t, QK_WIDTH), row),
            pl.BlockSpec((DA_WIDTH, t), lambda i: (0, i)),
            pl.BlockSpec((t, POOL_WIDTH), row),
            pl.BlockSpec((t, 3 * RET_WIDTH), row),
            pl.BlockSpec((t, RET_WIDTH), row),
        ],
        out_shape=[
            jax.ShapeDtypeStruct((r, QK_WIDTH), BF16),
            jax.ShapeDtypeStruct((DA_WIDTH, r), BF16),
            jax.ShapeDtypeStruct((r, POOL_WIDTH), F32),
            jax.ShapeDtypeStruct((r, 3 * RET_WIDTH), BF16),
            jax.ShapeDtypeStruct((r, RET_WIDTH), F32),
        ],
        compiler_params=_cparams("arbitrary"),
        name="inproj",
    )(x, mod3, w_in_bf, w_vt_bf, rope_c, rope_s)


def _attn_kernel(lam_ref, q_ref, k_ref, vt_ref, o_ref, *, k_chunk, seq, lambda_init):
    q = q_ref[...]
    mq = q.shape[0]
    lane = lax.broadcasted_iota(jnp.int32, q.shape, 1)
    zero = jnp.zeros_like(q)
    q2 = jnp.concatenate([jnp.where(lane < DA_DIM, q, zero), jnp.where(lane >= DA_DIM, q, zero)], axis=0)
    qt = q2.astype(F32).T.astype(BF16)

    def scores(c):
        off = pl.multiple_of(c * k_chunk, k_chunk)
        return jnp.dot(k_ref[pl.ds(off, k_chunk), :], qt, preferred_element_type=F32)

    def update(c, s, m, l, acc):
        off = pl.multiple_of(c * k_chunk, k_chunk)
        vt = vt_ref[:, pl.ds(off, k_chunk)]
        m_new = jnp.maximum(m, jnp.max(s, axis=0, keepdims=True))
        alpha = jnp.exp2(m - m_new)
        p = jnp.exp2(s - m_new)
        l_new = alpha * l + jnp.sum(p, axis=0, keepdims=True)
        acc_new = alpha * acc + jnp.dot(vt, p.astype(BF16), preferred_element_type=F32)
        return m_new, l_new, acc_new

    n_chunks = (seq + CTX_LEN) // k_chunk
    n_latent_iters = (n_chunks - 1) // ATTN_UNROLL
    is_ctx_tile = pl.program_id(2) == pl.num_programs(2) - 1
    c0 = jnp.where(is_ctx_tile, n_chunks - 1, 0)

    def body(it, carry):
        s, m, l, acc = carry
        for u in range(ATTN_UNROLL):
            c = it * ATTN_UNROLL + u
            s_next = scores(c + 1)
            m, l, acc = update(c, s, m, l, acc)
            s = s_next
        return s, m, l, acc

    init = (scores(c0), jnp.full((1, 2 * mq), -jnp.inf, F32), jnp.zeros((1, 2 * mq), F32),
            jnp.zeros((DA_VDIM, 2 * mq), F32))
    s_last, m, l, acc = lax.fori_loop(0, jnp.where(is_ctx_tile, 0, n_latent_iters), body, init)
    _, l, acc = update(n_chunks - 1, s_last, m, l, acc)
    l0, l1 = l[:, 0:mq], l[:, mq:]
    a0, a1 = acc[:, 0:mq], acc[:, mq:]

    lv = lam_ref[...]
    lam = (jnp.exp(jnp.sum(lv[0:1] * lv[1:2], axis=-1, keepdims=True))
           - jnp.exp(jnp.sum(lv[2:3] * lv[3:4], axis=-1, keepdims=True)) + lambda_init)
    o = a0 / l0 - lam * (a1 / l1)
    o = o * lax.rsqrt(jnp.mean(o * o, axis=0, keepdims=True) + RMS_EPS) * (1.0 - lambda_init)
    o_ref[...] = o.T.astype(BF16)


def _attn_call(lam_vec, qk, vda, *, batch, rows_per_batch, seq, lambda_init):
    tq = ATTN_Q_TILE
    assert seq % (ATTN_K_CHUNK * ATTN_UNROLL) == 0 and rows_per_batch - seq == CTX_LEN == tq == ATTN_K_CHUNK
    nq = rows_per_batch // tq
    kern = functools.partial(_attn_kernel, k_chunk=ATTN_K_CHUNK, seq=seq, lambda_init=lambda_init)
    return pl.pallas_call(
        kern,
        grid=(batch, DA_HEADS, nq),
        in_specs=[
            pl.BlockSpec((4, DA_DIM), lambda b, h, i: (0, 0)),
            pl.BlockSpec((tq, DA_VDIM), lambda b, h, i: (b * nq + i, h)),
            pl.BlockSpec((rows_per_batch, DA_VDIM), lambda b, h, i: (b, DA_HEADS + h)),
            pl.BlockSpec((DA_VDIM, rows_per_batch), lambda b, h, i: (h, b)),
        ],
        out_specs=pl.BlockSpec((tq, DA_VDIM), lambda b, h, i: (b * nq + i, h)),
        out_shape=jax.ShapeDtypeStruct((qk.shape[0], DA_WIDTH), BF16),
        compiler_params=_cparams("arbitrary", "arbitrary", "arbitrary"),
        name="diff_attn",
    )(lam_vec, qk, qk, vda)


def _ret_kernel(ld_ref, f_ref, b_ref, of_ref, ob_ref, dm_ref, qd_ref, kd_ref, cd_ref, st_ref):
    c = pl.program_id(1)
    ch = RET_CHUNK
    w = RET_WIDTH
    lane_head = lax.broadcasted_iota(jnp.int32, (1, w), 1) // RET_DK

    @pl.when(c == 0)
    def _():
        st_ref[...] = jnp.zeros_like(st_ref)
        ri = lax.broadcasted_iota(jnp.int32, (ch, ch), 0)
        ci = lax.broadcasted_iota(jnp.int32, (ch, ch), 1)
        rowf = lax.broadcasted_iota(jnp.int32, (ch, w), 0).astype(F32)
        for d in range(2):
            lg_lane = jnp.zeros((1, w), F32)
            for hh in range(RET_HEADS):
                lg = -jnp.exp(jnp.full((1, 1), ld_ref[d, hh], F32))
                lg_lane = jnp.where(lane_head == hh, lg, lg_lane)
                dist = ((ri - ci) if d == 0 else (ci - ri)).astype(F32)
                dm_ref[d, hh] = jnp.where(dist >= 0, jnp.exp(dist * lg), 0.0)
            if d == 0:
                qd_ref[d] = jnp.exp((rowf + 1.0) * lg_lane)
                kd_ref[d] = jnp.exp((ch - 1.0 - rowf) * lg_lane)
            else:
                qd_ref[d] = jnp.exp((ch - rowf) * lg_lane)
                kd_ref[d] = jnp.exp(rowf * lg_lane)
            cd_ref[d] = jnp.exp(float(ch) * lg_lane)

    rblk = lax.broadcasted_iota(jnp.int32, (w, w), 0) // RET_DK
    cblk = lax.broadcasted_iota(jnp.int32, (w, w), 1) // RET_DK
    for d, (src, dst) in enumerate(((f_ref, of_ref), (b_ref, ob_ref))):
        q = src[:, 0:w]
        k = src[:, w:2 * w]
        v = src[:, 2 * w:3 * w]
        st = st_ref[d]
        o = jnp.dot((q.astype(F32) * qd_ref[d]).astype(BF16), st.astype(BF16), preferred_element_type=F32)
        for hh in range(RET_HEADS):
            in_head = lane_head == hh
            qm = jnp.where(in_head, q, jnp.zeros_like(q))
            s = lax.dot_general(qm, k, (((1,), (1,)), ((), ())), preferred_element_type=F32)
            intra = (s * dm_ref[d, hh]).astype(BF16)
            o = o + jnp.where(in_head, jnp.dot(intra, v, preferred_element_type=F32), 0.0)
        dst[...] = o
        kk_t = (k.astype(F32) * kd_ref[d]).T.astype(BF16)
        upd = jnp.dot(kk_t, v, preferred_element_type=F32)
        st_ref[d] = jnp.where(rblk == cblk, st * cd_ref[d] + upd, 0.0)


def _ret_call(log_decay, rqkv, *, batch, rows_per_batch, seq):
    ch = RET_CHUNK
    nc = rows_per_batch // ch
    n_lat = seq // ch
    n_ctx = nc - n_lat

    def fwd(b, c):
        return (b * nc + jnp.where(c < n_ctx, n_lat + c, c - n_ctx), 0)

    def bwd(b, c):
        return (b * nc + nc - 1 - c, 0)

    w = RET_WIDTH
    return pl.pallas_call(
        _ret_kernel,
        grid=(batch, nc),
        in_specs=[
            pl.BlockSpec(memory_space=pltpu.SMEM),
            pl.BlockSpec((ch, 3 * w), fwd),
            pl.BlockSpec((ch, 3 * w), bwd),
        ],
        out_specs=[pl.BlockSpec((ch, w), fwd), pl.BlockSpec((ch, w), bwd)],
        out_shape=[jax.ShapeDtypeStruct((rqkv.shape[0], w), F32)] * 2,
        scratch_shapes=[
            pltpu.VMEM((2, RET_HEADS, ch, ch), F32),
            pltpu.VMEM((2, ch, w), F32),
            pltpu.VMEM((2, ch, w), F32),
            pltpu.VMEM((2, 1, w), F32),
            pltpu.VMEM((2, w, w), F32),
        ],
        compiler_params=_cparams("arbitrary", "arbitrary"),
        name="retention",
    )(log_decay, rqkv, rqkv)


def _mixout_kernel(x_ref, da_ref, u_ref, up_ref, un_ref, of_ref, ob_ref, rg_ref, mod_ref, wo_ref, pw_ref,
                   ps_ref, lng_ref, lnb_ref, o_ref, *, tiles_per_batch, seq, alpha):
    d = D_MODEL
    t = x_ref.shape[0]
    i = pl.program_id(0)
    j = i % tiles_per_batch
    is_ctx = j == tiles_per_batch - 1
    stream_len = jnp.where(is_ctx, CTX_LEN, seq)
    p0 = jnp.where(is_ctx, 0, j * t)

    u = u_ref[...]
    prev = jnp.where(p0 > 0, up_ref[...], 0.0)
    nxt = jnp.where(p0 + t < stream_len, un_ref[...], 0.0)
    ext = jnp.concatenate([prev, u, nxt], axis=0)
    n = t + 2 * POOL_HALO
    a2 = ext + pltpu.roll(ext, 1, 0)
    a4 = pltpu.roll(a2, 1, 0) + pltpu.roll(a2, n - 1, 0)
    a8 = pltpu.roll(a4, 2, 0) + pltpu.roll(a4, n - 2, 0)
    a16 = pltpu.roll(a8, 4, 0) + pltpu.roll(a8, n - 4, 0)
    pos = p0 + lax.broadcasted_iota(jnp.int32, (t, POOL_WIDTH), 0)
    group = lax.broadcasted_iota(jnp.int32, (1, POOL_WIDTH), 1) // POOL_GROUP
    mean = jnp.zeros((t, POOL_WIDTH), F32)
    for gi, (wnd, asum) in enumerate(zip(POOL_WINDOWS, (a2, a4, a8, a16))):
        cnt = jnp.minimum(pos + wnd // 2, stream_len) - jnp.maximum(pos - wnd // 2, 0)
        mean = jnp.where(group == gi, asum[POOL_HALO:POOL_HALO + t] / cnt.astype(F32), mean)
    pool = jnp.dot((mean - u).astype(BF16), pw_ref[...], preferred_element_type=F32) * ps_ref[...]

    o = of_ref[...] + ob_ref[...]
    head = lax.broadcasted_iota(jnp.int32, (1, RET_WIDTH), 1) // RET_DK

    def head_mean(val):
        out = jnp.zeros_like(val)
        for hh in range(RET_HEADS):
            m = jnp.sum(jnp.where(head == hh, val, 0.0), axis=-1, keepdims=True) * (1.0 / RET_DK)
            out = jnp.where(head == hh, m, out)
        return out

    oc = o - head_mean(o)
    rn = oc * lax.rsqrt(head_mean(oc * oc) + LN_EPS)
    g = rg_ref[...]
    ret = rn * (g * _sigmoid(g))

    y = jnp.dot(da_ref[...], wo_ref[0:DA_WIDTH, :], preferred_element_type=F32)
    y = y + jnp.dot(pool.astype(BF16), wo_ref[DA_WIDTH:DA_WIDTH + POOL_WIDTH, :], preferred_element_type=F32)
    y = y + jnp.dot(ret.astype(BF16), wo_ref[DA_WIDTH + POOL_WIDTH:, :], preferred_element_type=F32)
    z = alpha * x_ref[...] + mod_ref[:, 2 * d:3 * d] * y
    o_ref[...] = _layer_norm_rows(z) * lng_ref[...] + lnb_ref[...]


def _mixout_call(x, da, u, o_f, o_b, rg, mod3, w_out_bf, pool_bd, pool_scale, ln_g, ln_b, *, tiles_per_batch, seq,
                 alpha):
    r, d = x.shape
    t = ROW_TILE
    nt = r // t
    hb = t // POOL_HALO
    n_halo_blocks = r // POOL_HALO
    row = lambda i: (i, 0)
    const = lambda i: (0, 0)
    kern = functools.partial(_mixout_kernel, tiles_per_batch=tiles_per_batch, seq=seq, alpha=alpha)
    return pl.pallas_call(
        kern,
        grid=(nt,),
        in_specs=[
            pl.BlockSpec((t, d), row),
            pl.BlockSpec((t, DA_WIDTH), row),
            pl.BlockSpec((t, POOL_WIDTH), row),
            pl.BlockSpec((POOL_HALO, POOL_WIDTH), lambda i: (jnp.maximum(i * hb - 1, 0), 0)),
            pl.BlockSpec((POOL_HALO, POOL_WIDTH), lambda i: (jnp.minimum((i + 1) * hb, n_halo_blocks - 1), 0)),
            pl.BlockSpec((t, RET_WIDTH), row),
            pl.BlockSpec((t, RET_WIDTH), row),
            pl.BlockSpec((t, RET_WIDTH), row),
            pl.BlockSpec((None, 1, 6 * d), lambda i: (_mod_row(i, tiles_per_batch), 0, 0)),
            pl.BlockSpec((d, d), const),
            pl.BlockSpec((POOL_WIDTH, POOL_WIDTH), const),
            pl.BlockSpec((1, POOL_WIDTH), const),
            pl.BlockSpec((1, d), const),
            pl.BlockSpec((1, d), const),
        ],
        out_specs=pl.BlockSpec((t, d), row),
        out_shape=jax.ShapeDtypeStruct((r, d), F32),
        compiler_params=_cparams("arbitrary"),
        name="mixer_out",
    )(x, da, u, u, u, o_f, o_b, rg, mod3, w_out_bf, pool_bd, pool_scale, ln_g, ln_b)


def _router_kernel(x_ref, mod_ref, wrh_ref, wrl_ref, bias_ref, wsgu_ref, wsdn_ref,
                   tokp_ref, idx_ref, gate_ref, rank_ref, cnt_ref, fsh_ref, carry_ref):
    d = D_MODEL
    t = x_ref.shape[0]
    ne = N_EXPERTS
    neg = -jnp.inf

    @pl.when(pl.program_id(0) == 0)
    def _():
        carry_ref[...] = jnp.zeros_like(carry_ref)

    tok = _layer_norm_rows(x_ref[...]) * (1.0 + mod_ref[:, 4 * d:5 * d]) + mod_ref[:, 3 * d:4 * d]
    tok_hi = tok.astype(BF16)
    tok_lo = (tok - tok_hi.astype(F32)).astype(BF16)

    bits = pltpu.bitcast(tok_hi.astype(F32), jnp.uint32)
    tokp_ref[...] = (lax.shift_right_logical(bits[:, 0:PACK_W], jnp.uint32(16))
                     | (bits[:, PACK_W:] & jnp.uint32(0xFFFF0000)))

    hs = jnp.dot(tok_hi, wsgu_ref[...], preferred_element_type=F32)
    gs, us = hs[:, 0:EXPERT_HIDDEN], hs[:, EXPERT_HIDDEN:]
    fsh_ref[...] = jnp.dot((gs * _sigmoid(gs) * us).astype(BF16), wsdn_ref[...], preferred_element_type=F32)

    nt_dims = (((1,), (1,)), ((), ()))
    logits = (lax.dot_general(wrh_ref[...], tok_hi, nt_dims, preferred_element_type=F32)
              + lax.dot_general(wrh_ref[...], tok_lo, nt_dims, preferred_element_type=F32)
              + lax.dot_general(wrl_ref[...], tok_hi, nt_dims, preferred_element_type=F32))
    scores = _sigmoid(logits)
    biased = scores + bias_ref[...]

    gidx = lax.broadcasted_iota(jnp.int32, (GROUP_SIZE, t), 0)
    blocks, gscores = [], []
    for g in range(N_GROUPS):
        blk = biased[g * GROUP_SIZE:(g + 1) * GROUP_SIZE, :]
        m1 = jnp.max(blk, axis=0, keepdims=True)
        first = jnp.min(jnp.where(blk == m1, gidx, GROUP_SIZE), axis=0, keepdims=True)
        m2 = jnp.max(jnp.where(gidx == first, neg, blk), axis=0, keepdims=True)
        blocks.append(blk)
        gscores.append(m1 + m2)

    keep = [jnp.zeros((1, t), F32) for _ in range(N_GROUPS)]
    for _ in range(TOPK_GROUPS):
        m = gscores[0]
        for gs_ in gscores[1:]:
            m = jnp.maximum(m, gs_)
        found = jnp.zeros((1, t), F32)
        for g in range(N_GROUPS):
            hit = jnp.where(gscores[g] == m, 1.0 - found, 0.0)
            found = found + hit
            keep[g] = keep[g] + hit
            gscores[g] = jnp.where(hit > 0.0, neg, gscores[g])
    masked = jnp.concatenate([jnp.where(keep[g] > 0.0, blocks[g], neg) for g in range(N_GROUPS)], axis=0)

    ei = lax.broadcasted_iota(jnp.int32, (ne, t), 0)
    cur = masked
    onehot = jnp.zeros((ne, t), F32)
    idxs, gates = [], []
    for _ in range(TOP_K):
        m = jnp.max(cur, axis=0, keepdims=True)
        ii = jnp.min(jnp.where(cur == m, ei, ne), axis=0, keepdims=True)
        sel = ei == ii
        idxs.append(ii)
        gates.append(jnp.sum(jnp.where(sel, scores, 0.0), axis=0, keepdims=True))
        onehot = jnp.where(sel, 1.0, onehot)
        cur = jnp.where(sel, neg, cur)
    gsum = gates[0]
    for gk in gates[1:]:
        gsum = gsum + gk
    for k in range(TOP_K):
        idx_ref[k:k + 1, :] = idxs[k]
        gate_ref[k:k + 1, :] = gates[k] / gsum * ROUTED_SCALE

    ti = lax.broadcasted_iota(jnp.int32, (t, t), 0)
    tj = lax.broadcasted_iota(jnp.int32, (t, t), 1)
    before = jnp.where(ti < tj, 1.0, 0.0).astype(BF16)
    prefix = jnp.dot(onehot.astype(BF16), before, preferred_element_type=F32) + carry_ref[:, 0:1]
    for k in range(TOP_K):
        rank_k = jnp.sum(jnp.where(ei == idxs[k], prefix, 0.0), axis=0, keepdims=True)
        rank_ref[k:k + 1, :] = rank_k.astype(jnp.int32)
    carry_ref[...] = carry_ref[...] + jnp.sum(onehot, axis=1, keepdims=True)
    cnt_ref[...] = carry_ref[...].astype(jnp.int32)


def _router_call(x, mod3, wr_hi, wr_lo, bias_col, ws_gu_bf, ws_dn_bf, *, tiles_per_batch):
    r, d = x.shape
    t = ROW_TILE
    nt = r // t
    row = lambda i: (i, 0)
    col = lambda i: (0, i)
    const = lambda i: (0, 0)
    return pl.pallas_call(
        _router_kernel,
        grid=(nt,),
        in_specs=[
            pl.BlockSpec((t, d), row),
            pl.BlockSpec((None, 1, 6 * d), lambda i: (_mod_row(i, tiles_per_batch), 0, 0)),
            pl.BlockSpec((N_EXPERTS, d), const),
            pl.BlockSpec((N_EXPERTS, d), const),
            pl.BlockSpec((N_EXPERTS, 1), const),
            pl.BlockSpec((d, 2 * EXPERT_HIDDEN), const),
            pl.BlockSpec((EXPERT_HIDDEN, d), const),
        ],
        out_specs=[
            pl.BlockSpec((t, PACK_W), row),
            pl.BlockSpec((TOP_K, t), col),
            pl.BlockSpec((TOP_K, t), col),
            pl.BlockSpec((TOP_K, t), col),
            pl.BlockSpec((N_EXPERTS, LANES), const),
            pl.BlockSpec((t, d), row),
        ],
        out_shape=[
            jax.ShapeDtypeStruct((r, PACK_W), jnp.uint32),
            jax.ShapeDtypeStruct((TOP_K, r), jnp.int32),
            jax.ShapeDtypeStruct((TOP_K, r), F32),
            jax.ShapeDtypeStruct((TOP_K, r), jnp.int32),
            jax.ShapeDtypeStruct((N_EXPERTS, LANES), jnp.int32),
            jax.ShapeDtypeStruct((r, d), F32),
        ],
        scratch_shapes=[pltpu.VMEM((N_EXPERTS, LANES), F32)],
        compiler_params=_cparams("arbitrary"),
        name="router",
    )(x, mod3, wr_hi, wr_lo, bias_col, ws_gu_bf, ws_dn_bf)


def _dest_kernel(idx_ref, rank_ref, offs_ref, dest_ref):
    t = idx_ref.shape[1]
    ei = lax.broadcasted_iota(jnp.int32, (N_EXPERTS, t), 0)
    offs = offs_ref[...].astype(F32)
    for k in range(TOP_K):
        start = jnp.sum(jnp.where(ei == idx_ref[k:k + 1, :], offs, 0.0), axis=0, keepdims=True)
        dest_ref[k:k + 1, :] = start.astype(jnp.int32) + rank_ref[k:k + 1, :]


def _dest_call(idx, rank, offs_col):
    r = idx.shape[1]
    t = ROW_TILE
    col = lambda i: (0, i)
    return pl.pallas_call(
        _dest_kernel,
        grid=(r // t,),
        in_specs=[pl.BlockSpec((TOP_K, t), col), pl.BlockSpec((TOP_K, t), col),
                  pl.BlockSpec((N_EXPERTS, 1), lambda i: (0, 0))],
        out_specs=pl.BlockSpec((TOP_K, t), col),
        out_shape=jax.ShapeDtypeStruct((TOP_K, r), jnp.int32),
        compiler_params=_cparams("arbitrary"),
        name="moe_dest",
    )(idx, rank, offs_col)


def _dispatch_kernel(dest_ref, tokp_ref, xs_in_ref, xs_ref, sem):
    del xs_in_ref
    t = dest_ref.shape[1]

    def issue(tt, carry):
        for k in range(TOP_K):
            pltpu.make_async_copy(tokp_ref.at[pl.ds(tt, 1), :], xs_ref.at[pl.ds(dest_ref[k, tt], 1), :], sem).start()
        return carry

    lax.fori_loop(0, t, issue, 0)
    all_rows = xs_ref.at[pl.ds(0, TOP_K * t), :]
    pltpu.make_async_copy(all_rows, all_rows, sem).wait()


def _dispatch_call(dest, tokp, xs_zero):
    r = dest.shape[1]
    t = ROW_TILE
    return pl.pallas_call(
        _dispatch_kernel,
        grid=(r // t,),
        in_specs=[
            pl.BlockSpec((TOP_K, t), lambda i: (0, i), memory_space=pltpu.SMEM),
            pl.BlockSpec((t, PACK_W), lambda i: (i, 0)),
            pl.BlockSpec(memory_space=pl.ANY),
        ],
        out_specs=pl.BlockSpec(memory_space=pl.ANY),
        out_shape=jax.ShapeDtypeStruct(xs_zero.shape, xs_zero.dtype),
        scratch_shapes=[pltpu.SemaphoreType.DMA(())],
        input_output_aliases={2: 0},
        compiler_params=_cparams("arbitrary"),
        name="moe_dispatch",
    )(dest, tokp, xs_zero)


def _expert_kernel(be_ref, nb_ref, xs_ref, wgu_ref, wdn_ref, ys_ref, wgu_bf, wdn_bf):
    j = pl.program_id(0)

    @pl.when(j < nb_ref[0])
    def _():
        changed = jnp.logical_or(j == 0, be_ref[j] != be_ref[jnp.maximum(j - 1, 0)])

        @pl.when(changed)
        def _():
            wgu_bf[...] = wgu_ref[...].astype(BF16)
            wdn_bf[...] = wdn_ref[...].astype(BF16)

        word = xs_ref[...]
        x_lo = pltpu.bitcast(lax.shift_left(word, jnp.uint32(16)), F32).astype(BF16)
        x_hi = pltpu.bitcast(word & jnp.uint32(0xFFFF0000), F32).astype(BF16)
        h = (jnp.dot(x_lo, wgu_bf[0:PACK_W, :], preferred_element_type=F32)
             + jnp.dot(x_hi, wgu_bf[PACK_W:, :], preferred_element_type=F32))
        g, u = h[:, 0:EXPERT_HIDDEN], h[:, EXPERT_HIDDEN:]
        ys_ref[...] = jnp.dot((g * _sigmoid(g) * u).astype(BF16), wdn_bf[...], preferred_element_type=F32)


def _expert_call(block_expert, n_blocks_used, xs, w_gu, w_dn, layer):
    n_rows = xs.shape[0]
    bm = EXPERT_BLOCK
    d = D_MODEL
    grid_spec = pltpu.PrefetchScalarGridSpec(
        num_scalar_prefetch=2,
        grid=(n_rows // bm,),
        in_specs=[
            pl.BlockSpec((bm, PACK_W), lambda j, be, nb: (j, 0)),
            pl.BlockSpec((None, None, d, 2 * EXPERT_HIDDEN), lambda j, be, nb: (layer, be[j], 0, 0)),
            pl.BlockSpec((None, None, EXPERT_HIDDEN, d), lambda j, be, nb: (layer, be[j], 0, 0)),
        ],
        out_specs=pl.BlockSpec((bm, d), lambda j, be, nb: (j, 0)),
        scratch_shapes=[pltpu.VMEM((d, 2 * EXPERT_HIDDEN), BF16), pltpu.VMEM((EXPERT_HIDDEN, d), BF16)],
    )
    return pl.pallas_call(
        _expert_kernel,
        grid_spec=grid_spec,
        out_shape=jax.ShapeDtypeStruct((n_rows, d), F32),
        compiler_params=_cparams("arbitrary"),
        name="moe_experts",
    )(block_expert, n_blocks_used, xs, w_gu, w_dn)


def _combine_kernel(dest_ref, ys_ref, x_ref, fsh_ref, gate_ref, mod_ref, lng_ref, lnb_ref, o_ref, buf, sem, *,
                    alpha):
    d = D_MODEL
    t = x_ref.shape[0]

    def issue(tt, carry):
        for k in range(TOP_K):
            pltpu.make_async_copy(ys_ref.at[pl.ds(dest_ref[k, tt], 1), :], buf.at[pl.ds(k * t + tt, 1), :],
                                  sem).start()
        return carry

    lax.fori_loop(0, t, issue, 0)
    pltpu.make_async_copy(ys_ref.at[pl.ds(0, TOP_K * t), :], buf, sem).wait()

    gate_rows = gate_ref[...]
    pad = jnp.zeros((LANES - TOP_K, t), F32)
    gate_cols = jnp.concatenate([gate_rows, pad], axis=0).T
    f = fsh_ref[...]
    for k in range(TOP_K):
        f = f + gate_cols[:, k:k + 1] * buf[pl.ds(k * t, t), :]
    z = alpha * x_ref[...] + mod_ref[:, 5 * d:6 * d] * f
    o_ref[...] = _layer_norm_rows(z) * lng_ref[...] + lnb_ref[...]


def _combine_call(dest, ys, x, fsh, gate, mod3, ln_g, ln_b, *, tiles_per_batch, alpha):
    r, d = x.shape
    t = ROW_TILE
    row = lambda i: (i, 0)
    const = lambda i: (0, 0)
    kern = functools.partial(_combine_kernel, alpha=alpha)
    return pl.pallas_call(
        kern,
        grid=(r // t,),
        in_specs=[
            pl.BlockSpec((TOP_K, t), lambda i: (0, i), memory_space=pltpu.SMEM),
            pl.BlockSpec(memory_space=pl.ANY),
            pl.BlockSpec((t, d), row),
            pl.BlockSpec((t, d), row),
            pl.BlockSpec((TOP_K, t), lambda i: (0, i)),
            pl.BlockSpec((None, 1, 6 * d), lambda i: (_mod_row(i, tiles_per_batch), 0, 0)),
            pl.BlockSpec((1, d), const),
            pl.BlockSpec((1, d), const),
        ],
        out_specs=pl.BlockSpec((t, d), row),
        out_shape=jax.ShapeDtypeStruct((r, d), F32),
        scratch_shapes=[pltpu.VMEM((TOP_K * t, d), F32), pltpu.SemaphoreType.DMA(())],
        compiler_params=_cparams("arbitrary"),
        name="moe_combine",
    )(dest, ys, x, fsh, gate, mod3, ln_g, ln_b)


def _rope_tables(seq):
    rows = seq // GRID_W
    row = jnp.repeat(jnp.arange(rows, dtype=F32), GRID_W)
    col = jnp.tile(jnp.arange(GRID_W, dtype=F32), rows)
    nf = DA_DIM // 4
    freqs = ROPE_BASE ** (-jnp.arange(nf, dtype=F32) / nf)
    cr, sr = jnp.cos(row[:, None] * freqs), jnp.sin(row[:, None] * freqs)
    cc, sc = jnp.cos(col[:, None] * freqs), jnp.sin(col[:, None] * freqs)
    c64 = jnp.concatenate([cr, cr, cc, cc], axis=1)
    s64 = jnp.concatenate([-sr, sr, -sc, sc], axis=1)
    c = jnp.concatenate([jnp.tile(c64, (1, 2)), jnp.ones((CTX_LEN, LANES), F32)], axis=0)
    s = jnp.concatenate([jnp.tile(s64, (1, 2)), jnp.zeros((CTX_LEN, LANES), F32)], axis=0)
    return c, s


def kernel(x, c, ctx, c_ctx, w_mod, b_mod, w_in, w_out, diff_lambda, pool_w, pool_scale, ret_log_decay, ln_g, ln_b,
           w_router, router_bias, w_expert_gate_up, w_expert_down, w_shared_gate_up, w_shared_down):
    batch, seq, d = x.shape
    depth = w_mod.shape[0]
    assert d == D_MODEL and ctx.shape[1] == CTX_LEN == ROW_TILE and batch == 2
    assert seq % ROW_TILE == 0 and seq % GRID_W == 0 and w_in.shape[-1] == IN_WIDTH
    rows_per_batch = seq + CTX_LEN
    tiles_per_batch = rows_per_batch // ROW_TILE
    r = batch * rows_per_batch
    alpha = (2.0 * depth) ** 0.25

    xa = jnp.concatenate([x, ctx], axis=1).reshape(r, d)
    cvec = jnp.zeros((8, d), F32).at[0:batch].set(c).at[batch].set(c_ctx)
    mod_all = _mod_call(cvec, w_mod, b_mod)
    rope_c, rope_s = _rope_tables(seq)

    n_sorted = r * TOP_K + N_EXPERTS * EXPERT_BLOCK
    n_blocks = n_sorted // EXPERT_BLOCK

    for l in range(depth):
        lambda_init = 0.8 - 0.6 * math.exp(-0.3 * l)
        mod3 = mod_all[l].reshape(8, 1, 6 * d)
        lng = ln_g[l].reshape(2, 1, d)
        lnb = ln_b[l].reshape(2, 1, d)

        w_in_bf = w_in[l].astype(BF16)
        w_vt_bf = w_in_bf[:, QK_WIDTH:QK_WIDTH + DA_WIDTH].T
        qk, vda, u, rqkv, rg = _inproj_call(xa, mod3, w_in_bf, w_vt_bf, rope_c, rope_s, tiles_per_batch)
        da = _attn_call(diff_lambda[l], qk, vda, batch=batch, rows_per_batch=rows_per_batch, seq=seq,
                        lambda_init=lambda_init)
        o_f, o_b = _ret_call(ret_log_decay[l], rqkv, batch=batch, rows_per_batch=rows_per_batch, seq=seq)
        pool_bd = jnp.zeros((POOL_WIDTH, POOL_WIDTH), F32)
        for gi in range(len(POOL_WINDOWS)):
            sl = slice(gi * POOL_GROUP, (gi + 1) * POOL_GROUP)
            pool_bd = pool_bd.at[sl, sl].set(pool_w[l, gi])
        xa = _mixout_call(xa, da, u, o_f, o_b, rg, mod3, w_out[l].astype(BF16), pool_bd.astype(BF16),
                          pool_scale[l].reshape(1, POOL_WIDTH), lng[0], lnb[0],
                          tiles_per_batch=tiles_per_batch, seq=seq, alpha=alpha)

        wr_t = w_router[l].T
        wr_hi = wr_t.astype(BF16)
        wr_lo = (wr_t - wr_hi.astype(F32)).astype(BF16)
        tokp, idx, gate, rank, cnt, fsh = _router_call(
            xa, mod3, wr_hi, wr_lo, router_bias[l].reshape(N_EXPERTS, 1),
            w_shared_gate_up[l].astype(BF16), w_shared_down[l].astype(BF16), tiles_per_batch=tiles_per_batch)
        counts = cnt[:, 0]
        padded = (counts + EXPERT_BLOCK - 1) // EXPERT_BLOCK * EXPERT_BLOCK
        pad_end = jnp.cumsum(padded)
        offs = pad_end - padded
        block_expert = jnp.minimum(
            jnp.searchsorted(pad_end, jnp.arange(n_blocks, dtype=jnp.int32) * EXPERT_BLOCK, side='right'),
            N_EXPERTS - 1).astype(jnp.int32)
        n_used = (pad_end[-1:] // EXPERT_BLOCK).astype(jnp.int32)
        dest = _dest_call(idx, rank, offs.reshape(N_EXPERTS, 1).astype(jnp.int32))
        xs = _dispatch_call(dest, tokp, jnp.zeros((n_sorted, PACK_W), jnp.uint32))
        ys = _expert_call(block_expert, n_used, xs, w_expert_gate_up, w_expert_down, l)
        xa = _combine_call(dest, ys, xa, fsh, gate, mod3, lng[1], lnb[1],
                           tiles_per_batch=tiles_per_batch, alpha=alpha)

    return xa.reshape(batch, rows_per_batch, d)[:, :seq]
```

```python
import functools
import math

import jax
import jax.numpy as jnp
from jax import lax
from jax.experimental import pallas as pl
from jax.experimental.pallas import tpu as pltpu

F32 = jnp.float32
BF16 = jnp.bfloat16
HIGHEST = lax.Precision.HIGHEST

D_MODEL = 1024
CTX_LEN = 256
GRID_W = 64
DA_HEADS = 4
DA_DIM = 64
DA_VDIM = 2 * DA_DIM
DA_WIDTH = DA_HEADS * DA_VDIM
ROPE_BASE = 10000.0
POOL_WINDOWS = (2, 4, 8, 16)
POOL_GROUP = 64
POOL_WIDTH = len(POOL_WINDOWS) * POOL_GROUP
POOL_HALO = 8
RET_HEADS = 4
RET_DK = 64
RET_WIDTH = RET_HEADS * RET_DK
RET_CHUNK = 128
QK_WIDTH = 2 * DA_HEADS * 2 * DA_DIM
IN_WIDTH = QK_WIDTH + DA_WIDTH + POOL_WIDTH + 4 * RET_WIDTH
N_EXPERTS = 256
TOP_K = 8
N_GROUPS = 8
GROUP_SIZE = N_EXPERTS // N_GROUPS
TOPK_GROUPS = 4
EXPERT_HIDDEN = 256
ROUTED_SCALE = 2.5
LN_EPS = 1e-6
RMS_EPS = 1e-5

LANES = 128
ROW_TILE = 256
ATTN_Q_TILE = 256
ATTN_K_CHUNK = 256
ATTN_UNROLL = 8
EXPERT_BLOCK = 256
PACK_W = D_MODEL // 2
PACK_S = PACK_W // LANES
ROW_S = D_MODEL // LANES
VMEM_LIMIT = 56 * 1024 * 1024


def _cparams(*sem):
    return pltpu.CompilerParams(dimension_semantics=sem, vmem_limit_bytes=VMEM_LIMIT)


def _sigmoid(x):
    return 1.0 / (1.0 + jnp.exp(-x))


def _layer_norm_rows(x):
    mu = jnp.mean(x, axis=-1, keepdims=True)
    xc = x - mu
    var = jnp.mean(xc * xc, axis=-1, keepdims=True)
    return xc * lax.rsqrt(var + LN_EPS)


def _mod_row(i, tiles_per_batch):
    return jnp.where(i % tiles_per_batch == tiles_per_batch - 1, 2, i // tiles_per_batch)


def _mod_kernel(c_ref, w_ref, b_ref, o_ref):
    c = c_ref[...]
    s = c * _sigmoid(c)
    o_ref[...] = jnp.dot(s, w_ref[...], precision=HIGHEST, preferred_element_type=F32) + b_ref[...]


def _mod_call(cvec, w_mod, b_mod):
    depth, d, n = w_mod.shape
    tn = 1536
    return pl.pallas_call(
        _mod_kernel,
        grid=(depth, n // tn),
        in_specs=[
            pl.BlockSpec((8, d), lambda l, j: (0, 0)),
            pl.BlockSpec((None, d, tn), lambda l, j: (l, 0, j)),
            pl.BlockSpec((None, 1, tn), lambda l, j: (l, 0, j)),
        ],
        out_specs=pl.BlockSpec((None, 8, tn), lambda l, j: (l, 0, j)),
        out_shape=jax.ShapeDtypeStruct((depth, 8, n), F32),
        compiler_params=_cparams("arbitrary", "arbitrary"),
        name="mod",
    )(cvec, w_mod, b_mod.reshape(depth, 1, n))


def _inproj_kernel(x_ref, mod_ref, w_ref, wvt_ref, ct_ref, st_ref, qk_ref, vt_ref, u_ref, r_ref, g_ref):
    d = D_MODEL
    xn = _layer_norm_rows(x_ref[...])
    h = (xn * (1.0 + mod_ref[:, d:2 * d]) + mod_ref[:, 0:d]).astype(BF16)

    a = jnp.dot(h, w_ref[:, 0:QK_WIDTH], preferred_element_type=F32)
    lane = lax.broadcasted_iota(jnp.int32, (a.shape[0], LANES), 1)
    first_half = (lane % 32) < 16
    ct = ct_ref[...]
    st = st_ref[...]
    for s in range(QK_WIDTH // LANES):
        blk = a[:, s * LANES:(s + 1) * LANES]
        partner = jnp.where(first_half, pltpu.roll(blk, LANES - 16, 1), pltpu.roll(blk, 16, 1))
        rot = blk * ct + partner * st
        if s < QK_WIDTH // LANES // 2:
            rot = rot * (DA_DIM ** -0.5 * math.log2(math.e))
        qk_ref[:, s * LANES:(s + 1) * LANES] = rot.astype(BF16)

    vt_ref[...] = lax.dot_general(wvt_ref[...], h, (((1,), (1,)), ((), ())),
                                  preferred_element_type=F32).astype(BF16)
    o = QK_WIDTH + DA_WIDTH
    u_ref[...] = jnp.dot(h, w_ref[:, o:o + POOL_WIDTH], preferred_element_type=F32)
    o += POOL_WIDTH
    r = jnp.dot(h, w_ref[:, o:o + 3 * RET_WIDTH], preferred_element_type=F32)
    r_ref[:, 0:RET_WIDTH] = r[:, 0:RET_WIDTH].astype(BF16)
    r_ref[:, RET_WIDTH:2 * RET_WIDTH] = (r[:, RET_WIDTH:2 * RET_WIDTH] * (RET_DK ** -0.5)).astype(BF16)
    r_ref[:, 2 * RET_WIDTH:] = r[:, 2 * RET_WIDTH:].astype(BF16)
    o += 3 * RET_WIDTH
    g_ref[...] = jnp.dot(h, w_ref[:, o:o + RET_WIDTH], preferred_element_type=F32)


def _inproj_call(x, mod3, w_in_bf, w_vt_bf, rope_c, rope_s, tiles_per_batch):
    r, d = x.shape
    t = ROW_TILE
    nt = r // t
    row = lambda i: (i, 0)
    return pl.pallas_call(
        _inproj_kernel,
        grid=(nt,),
        in_specs=[
            pl.BlockSpec((t, d), row),
            pl.BlockSpec((None, 1, 6 * d), lambda i: (_mod_row(i, tiles_per_batch), 0, 0)),
            pl.BlockSpec((d, IN_WIDTH), lambda i: (0, 0)),
            pl.BlockSpec((DA_WIDTH, d), lambda i: (0, 0)),
            pl.BlockSpec((t, LANES), lambda i: (i % tiles_per_batch, 0)),
            pl.BlockSpec((t, LANES), lambda i: (i % tiles_per_batch, 0)),
        ],
        out_specs=[
            pl.BlockSpec((t, QK_WIDTH), row),
            pl.BlockSpec((DA_WIDTH, t), lambda i: (0, i)),
            pl.BlockSpec((t, POOL_WIDTH), row),
            pl.BlockSpec((t, 3 * RET_WIDTH), row),
            pl.BlockSpec((t, RET_WIDTH), row),
        ],
        out_shape=[
            jax.ShapeDtypeStruct((r, QK_WIDTH), BF16),
            jax.ShapeDtypeStruct((DA_WIDTH, r), BF16),
            jax.ShapeDtypeStruct((r, POOL_WIDTH), F32),
            jax.ShapeDtypeStruct((r, 3 * RET_WIDTH), BF16),
            jax.ShapeDtypeStruct((r, RET_WIDTH), F32),
        ],
        compiler_params=_cparams("arbitrary"),
        name="inproj",
    )(x, mod3, w_in_bf, w_vt_bf, rope_c, rope_s)


def _attn_kernel(lam_ref, q_ref, k_ref, vt_ref, o_ref, *, k_chunk, seq, lambda_init):
    q = q_ref[...]
    mq = q.shape[0]
    lane = lax.broadcasted_iota(jnp.int32, q.shape, 1)
    zero = jnp.zeros_like(q)
    q2 = jnp.concatenate([jnp.where(lane < DA_DIM, q, zero), jnp.where(lane >= DA_DIM, q, zero)], axis=0)
    qt = q2.astype(F32).T.astype(BF16)

    def scores(c):
        off = pl.multiple_of(c * k_chunk, k_chunk)
        return jnp.dot(k_ref[pl.ds(off, k_chunk), :], qt, preferred_element_type=F32)

    ones_rows = jnp.where(lax.broadcasted_iota(jnp.int32, (16, k_chunk), 0) == 0, 1.0, 0.0).astype(BF16)

    def update(c, s, m, acc):
        off = pl.multiple_of(c * k_chunk, k_chunk)
        vt = jnp.concatenate([vt_ref[:, pl.ds(off, k_chunk)], ones_rows], axis=0)
        m_new = jnp.maximum(m, jnp.max(s, axis=0, keepdims=True))
        alpha = jnp.exp2(m - m_new)
        p = jnp.exp2((s - m_new).astype(BF16))
        acc_new = alpha * acc + jnp.dot(vt, p, preferred_element_type=F32)
        return m_new, acc_new

    n_chunks = (seq + CTX_LEN) // k_chunk
    n_latent_iters = (n_chunks - 1) // ATTN_UNROLL
    is_ctx_tile = pl.program_id(2) == pl.num_programs(2) - 1
    c0 = jnp.where(is_ctx_tile, n_chunks - 1, 0)

    def body(it, carry):
        s, m, acc = carry
        for u in range(ATTN_UNROLL):
            c = it * ATTN_UNROLL + u
            s_next = scores(c + 1)
            m, acc = update(c, s, m, acc)
            s = s_next
        return s, m, acc

    init = (scores(c0), jnp.full((1, 2 * mq), -jnp.inf, F32), jnp.zeros((DA_VDIM + 16, 2 * mq), F32))
    s_last, m, acc = lax.fori_loop(0, jnp.where(is_ctx_tile, 0, n_latent_iters), body, init)
    _, acc = update(n_chunks - 1, s_last, m, acc)
    l0, l1 = acc[DA_VDIM:DA_VDIM + 1, 0:mq], acc[DA_VDIM:DA_VDIM + 1, mq:]
    a0, a1 = acc[0:DA_VDIM, 0:mq], acc[0:DA_VDIM, mq:]

    lv = lam_ref[...]
    lam = (jnp.exp(jnp.sum(lv[0:1] * lv[1:2], axis=-1, keepdims=True))
           - jnp.exp(jnp.sum(lv[2:3] * lv[3:4], axis=-1, keepdims=True)) + lambda_init)
    o = a0 / l0 - lam * (a1 / l1)
    o = o * lax.rsqrt(jnp.mean(o * o, axis=0, keepdims=True) + RMS_EPS) * (1.0 - lambda_init)
    o_ref[...] = o.T.astype(BF16)


def _attn_call(lam_vec, qk, vda, *, batch, rows_per_batch, seq, lambda_init):
    tq = ATTN_Q_TILE
    assert seq % (ATTN_K_CHUNK * ATTN_UNROLL) == 0 and rows_per_batch - seq == CTX_LEN == tq == ATTN_K_CHUNK
    nq = rows_per_batch // tq
    kern = functools.partial(_attn_kernel, k_chunk=ATTN_K_CHUNK, seq=seq, lambda_init=lambda_init)
    return pl.pallas_call(
        kern,
        grid=(batch, DA_HEADS, nq),
        in_specs=[
            pl.BlockSpec((4, DA_DIM), lambda b, h, i: (0, 0)),
            pl.BlockSpec((tq, DA_VDIM), lambda b, h, i: (b * nq + i, h)),
            pl.BlockSpec((rows_per_batch, DA_VDIM), lambda b, h, i: (b, DA_HEADS + h)),
            pl.BlockSpec((DA_VDIM, rows_per_batch), lambda b, h, i: (h, b)),
        ],
        out_specs=pl.BlockSpec((tq, DA_VDIM), lambda b, h, i: (b * nq + i, h)),
        out_shape=jax.ShapeDtypeStruct((qk.shape[0], DA_WIDTH), BF16),
        compiler_params=_cparams("arbitrary", "arbitrary", "arbitrary"),
        name="diff_attn",
    )(lam_vec, qk, qk, vda)


def _ret_kernel(ld_ref, f_ref, b_ref, of_ref, ob_ref, dm_ref, qd_ref, kd_ref, cd_ref, st_ref):
    c = pl.program_id(1)
    ch = RET_CHUNK
    w = RET_WIDTH
    lane_head = lax.broadcasted_iota(jnp.int32, (1, w), 1) // RET_DK

    @pl.when(c == 0)
    def _():
        st_ref[...] = jnp.zeros_like(st_ref)
        ri = lax.broadcasted_iota(jnp.int32, (ch, ch), 0)
        ci = lax.broadcasted_iota(jnp.int32, (ch, ch), 1)
        rowf = lax.broadcasted_iota(jnp.int32, (ch, w), 0).astype(F32)
        for d in range(2):
            lg_lane = jnp.zeros((1, w), F32)
            for hh in range(RET_HEADS):
                lg = -jnp.exp(jnp.full((1, 1), ld_ref[d, hh], F32))
                lg_lane = jnp.where(lane_head == hh, lg, lg_lane)
                dist = ((ri - ci) if d == 0 else (ci - ri)).astype(F32)
                dm_ref[d, hh] = jnp.where(dist >= 0, jnp.exp(dist * lg), 0.0)
            if d == 0:
                qd_ref[d] = jnp.exp((rowf + 1.0) * lg_lane)
                kd_ref[d] = jnp.exp((ch - 1.0 - rowf) * lg_lane)
            else:
                qd_ref[d] = jnp.exp((ch - rowf) * lg_lane)
                kd_ref[d] = jnp.exp(rowf * lg_lane)
            cd_ref[d] = jnp.exp(float(ch) * lg_lane)

    rblk = lax.broadcasted_iota(jnp.int32, (w, w), 0) // RET_DK
    cblk = lax.broadcasted_iota(jnp.int32, (w, w), 1) // RET_DK
    for d, (src, dst) in enumerate(((f_ref, of_ref), (b_ref, ob_ref))):
        q = src[:, 0:w]
        k = src[:, w:2 * w]
        v = src[:, 2 * w:3 * w]
        st = st_ref[d]
        o = jnp.dot((q.astype(F32) * qd_ref[d]).astype(BF16), st.astype(BF16), preferred_element_type=F32)
        for hh in range(RET_HEADS):
            in_head = lane_head == hh
            qm = jnp.where(in_head, q, jnp.zeros_like(q))
            s = lax.dot_general(qm, k, (((1,), (1,)), ((), ())), preferred_element_type=F32)
            intra = (s * dm_ref[d, hh]).astype(BF16)
            o = o + jnp.where(in_head, jnp.dot(intra, v, preferred_element_type=F32), 0.0)
        dst[...] = o
        kk_t = (k.astype(F32) * kd_ref[d]).T.astype(BF16)
        upd = jnp.dot(kk_t, v, preferred_element_type=F32)
        st_ref[d] = jnp.where(rblk == cblk, st * cd_ref[d] + upd, 0.0)


def _ret_call(log_decay, rqkv, *, batch, rows_per_batch, seq):
    ch = RET_CHUNK
    nc = rows_per_batch // ch
    n_lat = seq // ch
    n_ctx = nc - n_lat

    def fwd(b, c):
        return (b * nc + jnp.where(c < n_ctx, n_lat + c, c - n_ctx), 0)

    def bwd(b, c):
        return (b * nc + nc - 1 - c, 0)

    w = RET_WIDTH
    return pl.pallas_call(
        _ret_kernel,
        grid=(batch, nc),
        in_specs=[
            pl.BlockSpec(memory_space=pltpu.SMEM),
            pl.BlockSpec((ch, 3 * w), fwd),
            pl.BlockSpec((ch, 3 * w), bwd),
        ],
        out_specs=[pl.BlockSpec((ch, w), fwd), pl.BlockSpec((ch, w), bwd)],
        out_shape=[jax.ShapeDtypeStruct((rqkv.shape[0], w), F32)] * 2,
        scratch_shapes=[
            pltpu.VMEM((2, RET_HEADS, ch, ch), F32),
            pltpu.VMEM((2, ch, w), F32),
            pltpu.VMEM((2, ch, w), F32),
            pltpu.VMEM((2, 1, w), F32),
            pltpu.VMEM((2, w, w), F32),
        ],
        compiler_params=_cparams("arbitrary", "arbitrary"),
        name="retention",
    )(log_decay, rqkv, rqkv)


def _mixout_kernel(x_ref, da_ref, u_ref, up_ref, un_ref, of_ref, ob_ref, rg_ref, mod_ref, wo_ref, pw_ref,
                   ps_ref, lng_ref, lnb_ref, o_ref, *, tiles_per_batch, seq, alpha):
    d = D_MODEL
    t = x_ref.shape[0]
    i = pl.program_id(0)
    j = i % tiles_per_batch
    is_ctx = j == tiles_per_batch - 1
    stream_len = jnp.where(is_ctx, CTX_LEN, seq)
    p0 = jnp.where(is_ctx, 0, j * t)

    u = u_ref[...]
    prev = jnp.where(p0 > 0, up_ref[...], 0.0)
    nxt = jnp.where(p0 + t < stream_len, un_ref[...], 0.0)
    ext = jnp.concatenate([prev, u, nxt], axis=0)
    n = t + 2 * POOL_HALO
    a2 = ext + pltpu.roll(ext, 1, 0)
    a4 = pltpu.roll(a2, 1, 0) + pltpu.roll(a2, n - 1, 0)
    a8 = pltpu.roll(a4, 2, 0) + pltpu.roll(a4, n - 2, 0)
    a16 = pltpu.roll(a8, 4, 0) + pltpu.roll(a8, n - 4, 0)
    pos = p0 + lax.broadcasted_iota(jnp.int32, (t, POOL_WIDTH), 0)
    group = lax.broadcasted_iota(jnp.int32, (1, POOL_WIDTH), 1) // POOL_GROUP
    mean = jnp.zeros((t, POOL_WIDTH), F32)
    for gi, (wnd, asum) in enumerate(zip(POOL_WINDOWS, (a2, a4, a8, a16))):
        cnt = jnp.minimum(pos + wnd // 2, stream_len) - jnp.maximum(pos - wnd // 2, 0)
        mean = jnp.where(group == gi, asum[POOL_HALO:POOL_HALO + t] / cnt.astype(F32), mean)
    pool = jnp.dot((mean - u).astype(BF16), pw_ref[...], preferred_element_type=F32) * ps_ref[...]

    o = of_ref[...] + ob_ref[...]
    head = lax.broadcasted_iota(jnp.int32, (1, RET_WIDTH), 1) // RET_DK

    def head_mean(val):
        out = jnp.zeros_like(val)
        for hh in range(RET_HEADS):
            m = jnp.sum(jnp.where(head == hh, val, 0.0), axis=-1, keepdims=True) * (1.0 / RET_DK)
            out = jnp.where(head == hh, m, out)
        return out

    oc = o - head_mean(o)
    rn = oc * lax.rsqrt(head_mean(oc * oc) + LN_EPS)
    g = rg_ref[...]
    ret = rn * (g * _sigmoid(g))

    y = jnp.dot(da_ref[...], wo_ref[0:DA_WIDTH, :], preferred_element_type=F32)
    y = y + jnp.dot(pool.astype(BF16), wo_ref[DA_WIDTH:DA_WIDTH + POOL_WIDTH, :], preferred_element_type=F32)
    y = y + jnp.dot(ret.astype(BF16), wo_ref[DA_WIDTH + POOL_WIDTH:, :], preferred_element_type=F32)
    z = alpha * x_ref[...] + mod_ref[:, 2 * d:3 * d] * y
    o_ref[...] = _layer_norm_rows(z) * lng_ref[...] + lnb_ref[...]


def _mixout_call(x, da, u, o_f, o_b, rg, mod3, w_out_bf, pool_bd, pool_scale, ln_g, ln_b, *, tiles_per_batch, seq,
                 alpha):
    r, d = x.shape
    t = ROW_TILE
    nt = r // t
    hb = t // POOL_HALO
    n_halo_blocks = r // POOL_HALO
    row = lambda i: (i, 0)
    const = lambda i: (0, 0)
    kern = functools.partial(_mixout_kernel, tiles_per_batch=tiles_per_batch, seq=seq, alpha=alpha)
    return pl.pallas_call(
        kern,
        grid=(nt,),
        in_specs=[
            pl.BlockSpec((t, d), row),
            pl.BlockSpec((t, DA_WIDTH), row),
            pl.BlockSpec((t, POOL_WIDTH), row),
            pl.BlockSpec((POOL_HALO, POOL_WIDTH), lambda i: (jnp.maximum(i * hb - 1, 0), 0)),
            pl.BlockSpec((POOL_HALO, POOL_WIDTH), lambda i: (jnp.minimum((i + 1) * hb, n_halo_blocks - 1), 0)),
            pl.BlockSpec((t, RET_WIDTH), row),
            pl.BlockSpec((t, RET_WIDTH), row),
            pl.BlockSpec((t, RET_WIDTH), row),
            pl.BlockSpec((None, 1, 6 * d), lambda i: (_mod_row(i, tiles_per_batch), 0, 0)),
            pl.BlockSpec((d, d), const),
            pl.BlockSpec((POOL_WIDTH, POOL_WIDTH), const),
            pl.BlockSpec((1, POOL_WIDTH), const),
            pl.BlockSpec((1, d), const),
            pl.BlockSpec((1, d), const),
        ],
        out_specs=pl.BlockSpec((t, d), row),
        out_shape=jax.ShapeDtypeStruct((r, d), F32),
        compiler_params=_cparams("arbitrary"),
        name="mixer_out",
    )(x, da, u, u, u, o_f, o_b, rg, mod3, w_out_bf, pool_bd, pool_scale, ln_g, ln_b)


def _router_kernel(x_ref, mod_ref, wrh_ref, wrl_ref, bias_ref, wsgu_ref, wsdn_ref,
                   tokp_ref, idx_ref, gate_ref, rank_ref, cnt_ref, fsh_ref, carry_ref):
    d = D_MODEL
    t = x_ref.shape[0]
    ne = N_EXPERTS
    neg = -jnp.inf

    @pl.when(pl.program_id(0) == 0)
    def _():
        carry_ref[...] = jnp.zeros_like(carry_ref)

    tok = _layer_norm_rows(x_ref[...]) * (1.0 + mod_ref[:, 4 * d:5 * d]) + mod_ref[:, 3 * d:4 * d]
    tok_hi = tok.astype(BF16)
    tok_lo = (tok - tok_hi.astype(F32)).astype(BF16)

    bits = pltpu.bitcast(tok_hi.astype(F32), jnp.uint32)
    tokp_ref[...] = (lax.shift_right_logical(bits[:, 0:PACK_W], jnp.uint32(16))
                     | (bits[:, PACK_W:] & jnp.uint32(0xFFFF0000)))

    hs = jnp.dot(tok_hi, wsgu_ref[...], preferred_element_type=F32)
    gs, us = hs[:, 0:EXPERT_HIDDEN], hs[:, EXPERT_HIDDEN:]
    fsh_ref[...] = jnp.dot((gs * _sigmoid(gs) * us).astype(BF16), wsdn_ref[...], preferred_element_type=F32)

    nt_dims = (((1,), (1,)), ((), ()))
    logits = (lax.dot_general(wrh_ref[...], tok_hi, nt_dims, preferred_element_type=F32)
              + lax.dot_general(wrh_ref[...], tok_lo, nt_dims, preferred_element_type=F32)
              + lax.dot_general(wrl_ref[...], tok_hi, nt_dims, preferred_element_type=F32))
    scores = _sigmoid(logits)
    biased = scores + bias_ref[...]

    gidx = lax.broadcasted_iota(jnp.int32, (GROUP_SIZE, t), 0)
    blocks, gscores = [], []
    for g in range(N_GROUPS):
        blk = biased[g * GROUP_SIZE:(g + 1) * GROUP_SIZE, :]
        m1 = jnp.max(blk, axis=0, keepdims=True)
        first = jnp.min(jnp.where(blk == m1, gidx, GROUP_SIZE), axis=0, keepdims=True)
        m2 = jnp.max(jnp.where(gidx == first, neg, blk), axis=0, keepdims=True)
        blocks.append(blk)
        gscores.append(m1 + m2)

    keep = [jnp.zeros((1, t), F32) for _ in range(N_GROUPS)]
    for _ in range(TOPK_GROUPS):
        m = gscores[0]
        for gs_ in gscores[1:]:
            m = jnp.maximum(m, gs_)
        found = jnp.zeros((1, t), F32)
        for g in range(N_GROUPS):
            hit = jnp.where(gscores[g] == m, 1.0 - found, 0.0)
            found = found + hit
            keep[g] = keep[g] + hit
            gscores[g] = jnp.where(hit > 0.0, neg, gscores[g])
    masked = jnp.concatenate([jnp.where(keep[g] > 0.0, blocks[g], neg) for g in range(N_GROUPS)], axis=0)

    ei = lax.broadcasted_iota(jnp.int32, (ne, t), 0)
    cur = masked
    onehot = jnp.zeros((ne, t), F32)
    idxs, gates = [], []
    for _ in range(TOP_K):
        m = jnp.max(cur, axis=0, keepdims=True)
        ii = jnp.min(jnp.where(cur == m, ei, ne), axis=0, keepdims=True)
        sel = ei == ii
        idxs.append(ii)
        gates.append(jnp.sum(jnp.where(sel, scores, 0.0), axis=0, keepdims=True))
        onehot = jnp.where(sel, 1.0, onehot)
        cur = jnp.where(sel, neg, cur)
    gsum = gates[0]
    for gk in gates[1:]:
        gsum = gsum + gk
    for k in range(TOP_K):
        idx_ref[k:k + 1, :] = idxs[k]
        gate_ref[k:k + 1, :] = gates[k] / gsum * ROUTED_SCALE

    ti = lax.broadcasted_iota(jnp.int32, (t, t), 0)
    tj = lax.broadcasted_iota(jnp.int32, (t, t), 1)
    before = jnp.where(ti < tj, 1.0, 0.0).astype(BF16)
    prefix = jnp.dot(onehot.astype(BF16), before, preferred_element_type=F32) + carry_ref[:, 0:1]
    for k in range(TOP_K):
        rank_k = jnp.sum(jnp.where(ei == idxs[k], prefix, 0.0), axis=0, keepdims=True)
        rank_ref[k:k + 1, :] = rank_k.astype(jnp.int32)
    carry_ref[...] = carry_ref[...] + jnp.sum(onehot, axis=1, keepdims=True)
    cnt_ref[...] = carry_ref[...].astype(jnp.int32)


def _router_call(x, mod3, wr_hi, wr_lo, bias_col, ws_gu_bf, ws_dn_bf, *, tiles_per_batch):
    r, d = x.shape
    t = ROW_TILE
    nt = r // t
    row = lambda i: (i, 0)
    col = lambda i: (0, i)
    const = lambda i: (0, 0)
    return pl.pallas_call(
        _router_kernel,
        grid=(nt,),
        in_specs=[
            pl.BlockSpec((t, d), row),
            pl.BlockSpec((None, 1, 6 * d), lambda i: (_mod_row(i, tiles_per_batch), 0, 0)),
            pl.BlockSpec((N_EXPERTS, d), const),
            pl.BlockSpec((N_EXPERTS, d), const),
            pl.BlockSpec((N_EXPERTS, 1), const),
            pl.BlockSpec((d, 2 * EXPERT_HIDDEN), const),
            pl.BlockSpec((EXPERT_HIDDEN, d), const),
        ],
        out_specs=[
            pl.BlockSpec((t, PACK_W), row),
            pl.BlockSpec((TOP_K, t), col),
            pl.BlockSpec((TOP_K, t), col),
            pl.BlockSpec((TOP_K, t), col),
            pl.BlockSpec((N_EXPERTS, LANES), const),
            pl.BlockSpec((t, d), row),
        ],
        out_shape=[
            jax.ShapeDtypeStruct((r, PACK_W), jnp.uint32),
            jax.ShapeDtypeStruct((TOP_K, r), jnp.int32),
            jax.ShapeDtypeStruct((TOP_K, r), F32),
            jax.ShapeDtypeStruct((TOP_K, r), jnp.int32),
            jax.ShapeDtypeStruct((N_EXPERTS, LANES), jnp.int32),
            jax.ShapeDtypeStruct((r, d), F32),
        ],
        scratch_shapes=[pltpu.VMEM((N_EXPERTS, LANES), F32)],
        compiler_params=_cparams("arbitrary"),
        name="router",
    )(x, mod3, wr_hi, wr_lo, bias_col, ws_gu_bf, ws_dn_bf)


def _dest_kernel(idx_ref, rank_ref, offs_ref, dest_ref):
    t = idx_ref.shape[1]
    ei = lax.broadcasted_iota(jnp.int32, (N_EXPERTS, t), 0)
    offs = offs_ref[...].astype(F32)
    for k in range(TOP_K):
        start = jnp.sum(jnp.where(ei == idx_ref[k:k + 1, :], offs, 0.0), axis=0, keepdims=True)
        dest_ref[k:k + 1, :] = start.astype(jnp.int32) + rank_ref[k:k + 1, :]


def _dest_call(idx, rank, offs_col):
    r = idx.shape[1]
    t = ROW_TILE
    col = lambda i: (0, i)
    return pl.pallas_call(
        _dest_kernel,
        grid=(r // t,),
        in_specs=[pl.BlockSpec((TOP_K, t), col), pl.BlockSpec((TOP_K, t), col),
                  pl.BlockSpec((N_EXPERTS, 1), lambda i: (0, 0))],
        out_specs=pl.BlockSpec((TOP_K, t), col),
        out_shape=jax.ShapeDtypeStruct((TOP_K, r), jnp.int32),
        compiler_params=_cparams("arbitrary"),
        name="moe_dest",
    )(idx, rank, offs_col)


def _dispatch_kernel(dest_ref, tokp_ref, xs_in_ref, xs_ref, sem):
    del xs_in_ref
    t = dest_ref.shape[1]

    def issue(tt, carry):
        for k in range(TOP_K):
            pltpu.make_async_copy(tokp_ref.at[pl.ds(tt, 1), :], xs_ref.at[pl.ds(dest_ref[k, tt], 1), :], sem).start()
        return carry

    lax.fori_loop(0, t, issue, 0)
    all_rows = xs_ref.at[pl.ds(0, TOP_K * t), :]
    pltpu.make_async_copy(all_rows, all_rows, sem).wait()


def _dispatch_call(dest, tokp, xs_zero):
    r = dest.shape[1]
    t = ROW_TILE
    return pl.pallas_call(
        _dispatch_kernel,
        grid=(r // t,),
        in_specs=[
            pl.BlockSpec((TOP_K, t), lambda i: (0, i), memory_space=pltpu.SMEM),
            pl.BlockSpec((t, PACK_W), lambda i: (i, 0)),
            pl.BlockSpec(memory_space=pl.ANY),
        ],
        out_specs=pl.BlockSpec(memory_space=pl.ANY),
        out_shape=jax.ShapeDtypeStruct(xs_zero.shape, xs_zero.dtype),
        scratch_shapes=[pltpu.SemaphoreType.DMA(())],
        input_output_aliases={2: 0},
        compiler_params=_cparams("arbitrary"),
        name="moe_dispatch",
    )(dest, tokp, xs_zero)


def _expert_kernel(be_ref, nb_ref, xs_ref, wgu_ref, wdn_ref, ys_ref, wgu_bf, wdn_bf):
    j = pl.program_id(0)

    @pl.when(j < nb_ref[0])
    def _():
        changed = jnp.logical_or(j == 0, be_ref[j] != be_ref[jnp.maximum(j - 1, 0)])

        @pl.when(changed)
        def _():
            wgu_bf[...] = wgu_ref[...].astype(BF16)
            wdn_bf[...] = wdn_ref[...].astype(BF16)

        word = xs_ref[...]
        x_lo = pltpu.bitcast(lax.shift_left(word, jnp.uint32(16)), F32).astype(BF16)
        x_hi = pltpu.bitcast(word & jnp.uint32(0xFFFF0000), F32).astype(BF16)
        h = (jnp.dot(x_lo, wgu_bf[0:PACK_W, :], preferred_element_type=F32)
             + jnp.dot(x_hi, wgu_bf[PACK_W:, :], preferred_element_type=F32))
        g, u = h[:, 0:EXPERT_HIDDEN], h[:, EXPERT_HIDDEN:]
        ys_ref[...] = jnp.dot((g * _sigmoid(g) * u).astype(BF16), wdn_bf[...], preferred_element_type=F32)


def _expert_call(block_expert, n_blocks_used, xs, w_gu, w_dn, layer):
    n_rows = xs.shape[0]
    bm = EXPERT_BLOCK
    d = D_MODEL
    grid_spec = pltpu.PrefetchScalarGridSpec(
        num_scalar_prefetch=2,
        grid=(n_rows // bm,),
        in_specs=[
            pl.BlockSpec((bm, PACK_W), lambda j, be, nb: (j, 0)),
            pl.BlockSpec((None, None, d, 2 * EXPERT_HIDDEN), lambda j, be, nb: (layer, be[j], 0, 0)),
            pl.BlockSpec((None, None, EXPERT_HIDDEN, d), lambda j, be, nb: (layer, be[j], 0, 0)),
        ],
        out_specs=pl.BlockSpec((bm, d), lambda j, be, nb: (j, 0)),
        scratch_shapes=[pltpu.VMEM((d, 2 * EXPERT_HIDDEN), BF16), pltpu.VMEM((EXPERT_HIDDEN, d), BF16)],
    )
    return pl.pallas_call(
        _expert_kernel,
        grid_spec=grid_spec,
        out_shape=jax.ShapeDtypeStruct((n_rows, d), F32),
        compiler_params=_cparams("arbitrary"),
        name="moe_experts",
    )(block_expert, n_blocks_used, xs, w_gu, w_dn)


def _combine_kernel(dest_ref, ys_ref, x_ref, fsh_ref, gate_ref, mod_ref, lng_ref, lnb_ref, o_ref, buf, sem, *,
                    alpha):
    d = D_MODEL
    t = x_ref.shape[0]

    def issue(tt, carry):
        for k in range(TOP_K):
            pltpu.make_async_copy(ys_ref.at[pl.ds(dest_ref[k, tt], 1), :], buf.at[pl.ds(k * t + tt, 1), :],
                                  sem).start()
        return carry

    lax.fori_loop(0, t, issue, 0)
    pltpu.make_async_copy(ys_ref.at[pl.ds(0, TOP_K * t), :], buf, sem).wait()

    gate_rows = gate_ref[...]
    pad = jnp.zeros((LANES - TOP_K, t), F32)
    gate_cols = jnp.concatenate([gate_rows, pad], axis=0).T
    f = fsh_ref[...]
    for k in range(TOP_K):
        f = f + gate_cols[:, k:k + 1] * buf[pl.ds(k * t, t), :]
    z = alpha * x_ref[...] + mod_ref[:, 5 * d:6 * d] * f
    o_ref[...] = _layer_norm_rows(z) * lng_ref[...] + lnb_ref[...]


def _combine_call(dest, ys, x, fsh, gate, mod3, ln_g, ln_b, *, tiles_per_batch, alpha):
    r, d = x.shape
    t = ROW_TILE
    row = lambda i: (i, 0)
    const = lambda i: (0, 0)
    kern = functools.partial(_combine_kernel, alpha=alpha)
    return pl.pallas_call(
        kern,
        grid=(r // t,),
        in_specs=[
            pl.BlockSpec((TOP_K, t), lambda i: (0, i), memory_space=pltpu.SMEM),
            pl.BlockSpec(memory_space=pl.ANY),
            pl.BlockSpec((t, d), row),
            pl.BlockSpec((t, d), row),
            pl.BlockSpec((TOP_K, t), lambda i: (0, i)),
            pl.BlockSpec((None, 1, 6 * d), lambda i: (_mod_row(i, tiles_per_batch), 0, 0)),
            pl.BlockSpec((1, d), const),
            pl.BlockSpec((1, d), const),
        ],
        out_specs=pl.BlockSpec((t, d), row),
        out_shape=jax.ShapeDtypeStruct((r, d), F32),
        scratch_shapes=[pltpu.VMEM((TOP_K * t, d), F32), pltpu.SemaphoreType.DMA(())],
        compiler_params=_cparams("arbitrary"),
        name="moe_combine",
    )(dest, ys, x, fsh, gate, mod3, ln_g, ln_b)


def _rope_tables(seq):
    rows = seq // GRID_W
    row = jnp.repeat(jnp.arange(rows, dtype=F32), GRID_W)
    col = jnp.tile(jnp.arange(GRID_W, dtype=F32), rows)
    nf = DA_DIM // 4
    freqs = ROPE_BASE ** (-jnp.arange(nf, dtype=F32) / nf)
    cr, sr = jnp.cos(row[:, None] * freqs), jnp.sin(row[:, None] * freqs)
    cc, sc = jnp.cos(col[:, None] * freqs), jnp.sin(col[:, None] * freqs)
    c64 = jnp.concatenate([cr, cr, cc, cc], axis=1)
    s64 = jnp.concatenate([-sr, sr, -sc, sc], axis=1)
    c = jnp.concatenate([jnp.tile(c64, (1, 2)), jnp.ones((CTX_LEN, LANES), F32)], axis=0)
    s = jnp.concatenate([jnp.tile(s64, (1, 2)), jnp.zeros((CTX_LEN, LANES), F32)], axis=0)
    return c, s


def kernel(x, c, ctx, c_ctx, w_mod, b_mod, w_in, w_out, diff_lambda, pool_w, pool_scale, ret_log_decay, ln_g, ln_b,
           w_router, router_bias, w_expert_gate_up, w_expert_down, w_shared_gate_up, w_shared_down):
    batch, seq, d = x.shape
    depth = w_mod.shape[0]
    assert d == D_MODEL and ctx.shape[1] == CTX_LEN == ROW_TILE and batch == 2
    assert seq % ROW_TILE == 0 and seq % GRID_W == 0 and w_in.shape[-1] == IN_WIDTH
    rows_per_batch = seq + CTX_LEN
    tiles_per_batch = rows_per_batch // ROW_TILE
    r = batch * rows_per_batch
    alpha = (2.0 * depth) ** 0.25

    xa = jnp.concatenate([x, ctx], axis=1).reshape(r, d)
    cvec = jnp.zeros((8, d), F32).at[0:batch].set(c).at[batch].set(c_ctx)
    mod_all = _mod_call(cvec, w_mod, b_mod)
    rope_c, rope_s = _rope_tables(seq)

    n_sorted = r * TOP_K + N_EXPERTS * EXPERT_BLOCK
    n_blocks = n_sorted // EXPERT_BLOCK

    for l in range(depth):
        lambda_init = 0.8 - 0.6 * math.exp(-0.3 * l)
        mod3 = mod_all[l].reshape(8, 1, 6 * d)
        lng = ln_g[l].reshape(2, 1, d)
        lnb = ln_b[l].reshape(2, 1, d)

        w_in_bf = w_in[l].astype(BF16)
        w_vt_bf = w_in_bf[:, QK_WIDTH:QK_WIDTH + DA_WIDTH].T
        qk, vda, u, rqkv, rg = _inproj_call(xa, mod3, w_in_bf, w_vt_bf, rope_c, rope_s, tiles_per_batch)
        da = _attn_call(diff_lambda[l], qk, vda, batch=batch, rows_per_batch=rows_per_batch, seq=seq,
                        lambda_init=lambda_init)
        o_f, o_b = _ret_call(ret_log_decay[l], rqkv, batch=batch, rows_per_batch=rows_per_batch, seq=seq)
        pool_bd = jnp.zeros((POOL_WIDTH, POOL_WIDTH), F32)
        for gi in range(len(POOL_WINDOWS)):
            sl = slice(gi * POOL_GROUP, (gi + 1) * POOL_GROUP)
            pool_bd = pool_bd.at[sl, sl].set(pool_w[l, gi])
        xa = _mixout_call(xa, da, u, o_f, o_b, rg, mod3, w_out[l].astype(BF16), pool_bd.astype(BF16),
                          pool_scale[l].reshape(1, POOL_WIDTH), lng[0], lnb[0],
                          tiles_per_batch=tiles_per_batch, seq=seq, alpha=alpha)

        wr_t = w_router[l].T
        wr_hi = wr_t.astype(BF16)
        wr_lo = (wr_t - wr_hi.astype(F32)).astype(BF16)
        tokp, idx, gate, rank, cnt, fsh = _router_call(
            xa, mod3, wr_hi, wr_lo, router_bias[l].reshape(N_EXPERTS, 1),
            w_shared_gate_up[l].astype(BF16), w_shared_down[l].astype(BF16), tiles_per_batch=tiles_per_batch)
        counts = cnt[:, 0]
        padded = (counts + EXPERT_BLOCK - 1) // EXPERT_BLOCK * EXPERT_BLOCK
        pad_end = jnp.cumsum(padded)
        offs = pad_end - padded
        block_expert = jnp.minimum(
            jnp.searchsorted(pad_end, jnp.arange(n_blocks, dtype=jnp.int32) * EXPERT_BLOCK, side='right'),
            N_EXPERTS - 1).astype(jnp.int32)
        n_used = (pad_end[-1:] // EXPERT_BLOCK).astype(jnp.int32)
        dest = _dest_call(idx, rank, offs.reshape(N_EXPERTS, 1).astype(jnp.int32))
        xs = _dispatch_call(dest, tokp, jnp.zeros((n_sorted, PACK_W), jnp.uint32))
        ys = _expert_call(block_expert, n_used, xs, w_expert_gate_up, w_expert_down, l)
        xa = _combine_call(dest, ys, xa, fsh, gate, mod3, lng[1], lnb[1],
                           tiles_per_batch=tiles_per_batch, alpha=alpha)

    return xa.reshape(batch, rows_per_batch, d)[:, :seq]
```

```python
import functools
import math

import jax
import jax.numpy as jnp
from jax import lax
from jax.experimental import pallas as pl
from jax.experimental.pallas import tpu as pltpu

F32 = jnp.float32
BF16 = jnp.bfloat16
HIGHEST = lax.Precision.HIGHEST

D_MODEL = 1024
CTX_LEN = 256
GRID_W = 64
DA_HEADS = 4
DA_DIM = 64
DA_VDIM = 2 * DA_DIM
DA_WIDTH = DA_HEADS * DA_VDIM
ROPE_BASE = 10000.0
POOL_WINDOWS = (2, 4, 8, 16)
POOL_GROUP = 64
POOL_WIDTH = len(POOL_WINDOWS) * POOL_GROUP
POOL_HALO = 8
RET_HEADS = 4
RET_DK = 64
RET_WIDTH = RET_HEADS * RET_DK
RET_CHUNK = 128
QK_WIDTH = 2 * DA_HEADS * 2 * DA_DIM
IN_WIDTH = QK_WIDTH + DA_WIDTH + POOL_WIDTH + 4 * RET_WIDTH
N_EXPERTS = 256
TOP_K = 8
N_GROUPS = 8
GROUP_SIZE = N_EXPERTS // N_GROUPS
TOPK_GROUPS = 4
EXPERT_HIDDEN = 256
ROUTED_SCALE = 2.5
LN_EPS = 1e-6
RMS_EPS = 1e-5

LANES = 128
ROW_TILE = 256
ATTN_Q_TILE = 256
ATTN_K_CHUNK = 256
ATTN_UNROLL = 16
EXPERT_BLOCK = 256
PACK_W = D_MODEL // 2
PACK_S = PACK_W // LANES
ROW_S = D_MODEL // LANES
VMEM_LIMIT = 56 * 1024 * 1024


def _cparams(*sem):
    return pltpu.CompilerParams(dimension_semantics=sem, vmem_limit_bytes=VMEM_LIMIT)


def _sigmoid(x):
    return 1.0 / (1.0 + jnp.exp(-x))


def _layer_norm_rows(x):
    mu = jnp.mean(x, axis=-1, keepdims=True)
    xc = x - mu
    var = jnp.mean(xc * xc, axis=-1, keepdims=True)
    return xc * lax.rsqrt(var + LN_EPS)


def _mod_row(i, tiles_per_batch):
    return jnp.where(i % tiles_per_batch == tiles_per_batch - 1, 2, i // tiles_per_batch)


def _mod_kernel(c_ref, w_ref, b_ref, o_ref):
    c = c_ref[...]
    s = c * _sigmoid(c)
    o_ref[...] = jnp.dot(s, w_ref[...], precision=HIGHEST, preferred_element_type=F32) + b_ref[...]


def _mod_call(cvec, w_mod, b_mod):
    depth, d, n = w_mod.shape
    tn = 1536
    return pl.pallas_call(
        _mod_kernel,
        grid=(depth, n // tn),
        in_specs=[
            pl.BlockSpec((8, d), lambda l, j: (0, 0)),
            pl.BlockSpec((None, d, tn), lambda l, j: (l, 0, j)),
            pl.BlockSpec((None, 1, tn), lambda l, j: (l, 0, j)),
        ],
        out_specs=pl.BlockSpec((None, 8, tn), lambda l, j: (l, 0, j)),
        out_shape=jax.ShapeDtypeStruct((depth, 8, n), F32),
        compiler_params=_cparams("arbitrary", "arbitrary"),
        name="mod",
    )(cvec, w_mod, b_mod.reshape(depth, 1, n))


def _inproj_kernel(x_ref, mod_ref, w_ref, wvt_ref, ct_ref, st_ref, qk_ref, vt_ref, u_ref, r_ref, g_ref):
    d = D_MODEL
    xn = _layer_norm_rows(x_ref[...])
    h = (xn * (1.0 + mod_ref[:, d:2 * d]) + mod_ref[:, 0:d]).astype(BF16)

    a = jnp.dot(h, w_ref[:, 0:QK_WIDTH], preferred_element_type=F32)
    lane = lax.broadcasted_iota(jnp.int32, (a.shape[0], LANES), 1)
    first_half = (lane % 32) < 16
    ct = ct_ref[...]
    st = st_ref[...]
    for s in range(QK_WIDTH // LANES):
        blk = a[:, s * LANES:(s + 1) * LANES]
        partner = jnp.where(first_half, pltpu.roll(blk, LANES - 16, 1), pltpu.roll(blk, 16, 1))
        rot = blk * ct + partner * st
        if s < QK_WIDTH // LANES // 2:
            rot = rot * (DA_DIM ** -0.5 * math.log2(math.e))
        qk_ref[:, s * LANES:(s + 1) * LANES] = rot.astype(BF16)

    vt_ref[...] = lax.dot_general(wvt_ref[...], h, (((1,), (1,)), ((), ())),
                                  preferred_element_type=F32).astype(BF16)
    o = QK_WIDTH + DA_WIDTH
    u_ref[...] = jnp.dot(h, w_ref[:, o:o + POOL_WIDTH], preferred_element_type=F32)
    o += POOL_WIDTH
    r = jnp.dot(h, w_ref[:, o:o + 3 * RET_WIDTH], preferred_element_type=F32)
    r_ref[:, 0:RET_WIDTH] = r[:, 0:RET_WIDTH].astype(BF16)
    r_ref[:, RET_WIDTH:2 * RET_WIDTH] = (r[:, RET_WIDTH:2 * RET_WIDTH] * (RET_DK ** -0.5)).astype(BF16)
    r_ref[:, 2 * RET_WIDTH:] = r[:, 2 * RET_WIDTH:].astype(BF16)
    o += 3 * RET_WIDTH
    g_ref[...] = jnp.dot(h, w_ref[:, o:o + RET_WIDTH], preferred_element_type=F32)


def _inproj_call(x, mod3, w_in_bf, w_vt_bf, rope_c, rope_s, tiles_per_batch):
    r, d = x.shape
    t = ROW_TILE
    nt = r // t
    row = lambda i: (i, 0)
    return pl.pallas_call(
        _inproj_kernel,
        grid=(nt,),
        in_specs=[
            pl.BlockSpec((t, d), row),
            pl.BlockSpec((None, 1, 6 * d), lambda i: (_mod_row(i, tiles_per_batch), 0, 0)),
            pl.BlockSpec((d, IN_WIDTH), lambda i: (0, 0)),
            pl.BlockSpec((DA_WIDTH, d), lambda i: (0, 0)),
            pl.BlockSpec((t, LANES), lambda i: (i % tiles_per_batch, 0)),
            pl.BlockSpec((t, LANES), lambda i: (i % tiles_per_batch, 0)),
        ],
        out_specs=[
            pl.BlockSpec((t, QK_WIDTH), row),
            pl.BlockSpec((DA_WIDTH, t), lambda i: (0, i)),
            pl.BlockSpec((t, POOL_WIDTH), row),
            pl.BlockSpec((t, 3 * RET_WIDTH), row),
            pl.BlockSpec((t, RET_WIDTH), row),
        ],
        out_shape=[
            jax.ShapeDtypeStruct((r, QK_WIDTH), BF16),
            jax.ShapeDtypeStruct((DA_WIDTH, r), BF16),
            jax.ShapeDtypeStruct((r, POOL_WIDTH), F32),
            jax.ShapeDtypeStruct((r, 3 * RET_WIDTH), BF16),
            jax.ShapeDtypeStruct((r, RET_WIDTH), F32),
        ],
        compiler_params=_cparams("arbitrary"),
        name="inproj",
    )(x, mod3, w_in_bf, w_vt_bf, rope_c, rope_s)


def _attn_kernel(lam_ref, q_ref, k_ref, vt_ref, o_ref, s_ref, *, k_chunk, seq, lambda_init):
    q = q_ref[...]
    mq = q.shape[0]
    lane = lax.broadcasted_iota(jnp.int32, q.shape, 1)
    zero = jnp.zeros_like(q)
    q2 = jnp.concatenate([jnp.where(lane < DA_DIM, q, zero), jnp.where(lane >= DA_DIM, q, zero)], axis=0)
    qt = q2.astype(F32).T.astype(BF16)

    n_chunks = (seq + CTX_LEN) // k_chunk
    last = n_chunks - 1
    is_ctx_tile = pl.program_id(2) == pl.num_programs(2) - 1
    n_iters = jnp.where(is_ctx_tile, 0, last // ATTN_UNROLL)

    def score_chunk(c, m):
        off = pl.multiple_of(c * k_chunk, k_chunk)
        s = jnp.dot(k_ref[pl.ds(off, k_chunk), :], qt, preferred_element_type=F32)
        s_ref[c] = s
        return jnp.maximum(m, jnp.max(s, axis=0, keepdims=True))

    def pass1(it, m):
        for u in range(ATTN_UNROLL):
            m = score_chunk(it * ATTN_UNROLL + u, m)
        return m

    m = lax.fori_loop(0, n_iters, pass1, jnp.full((1, 2 * mq), -jnp.inf, F32))
    m = score_chunk(last, m)

    ones_rows = jnp.where(lax.broadcasted_iota(jnp.int32, (16, k_chunk), 0) == 0, 1.0, 0.0).astype(BF16)

    def value_chunk(c, acc):
        off = pl.multiple_of(c * k_chunk, k_chunk)
        vt = jnp.concatenate([vt_ref[:, pl.ds(off, k_chunk)], ones_rows], axis=0)
        p = jnp.exp2((s_ref[c] - m).astype(BF16))
        return acc + jnp.dot(vt, p, preferred_element_type=F32)

    def pass2(it, acc):
        for u in range(ATTN_UNROLL):
            acc = value_chunk(it * ATTN_UNROLL + u, acc)
        return acc

    acc = lax.fori_loop(0, n_iters, pass2, jnp.zeros((DA_VDIM + 16, 2 * mq), F32))
    acc = value_chunk(last, acc)
    l0, l1 = acc[DA_VDIM:DA_VDIM + 1, 0:mq], acc[DA_VDIM:DA_VDIM + 1, mq:]
    a0, a1 = acc[0:DA_VDIM, 0:mq], acc[0:DA_VDIM, mq:]

    lv = lam_ref[...]
    lam = (jnp.exp(jnp.sum(lv[0:1] * lv[1:2], axis=-1, keepdims=True))
           - jnp.exp(jnp.sum(lv[2:3] * lv[3:4], axis=-1, keepdims=True)) + lambda_init)
    o = a0 / l0 - lam * (a1 / l1)
    o = o * lax.rsqrt(jnp.mean(o * o, axis=0, keepdims=True) + RMS_EPS) * (1.0 - lambda_init)
    o_ref[...] = o.T.astype(BF16)


def _attn_call(lam_vec, qk, vda, *, batch, rows_per_batch, seq, lambda_init):
    tq = ATTN_Q_TILE
    assert seq % (ATTN_K_CHUNK * ATTN_UNROLL) == 0 and rows_per_batch - seq == CTX_LEN == tq == ATTN_K_CHUNK
    nq = rows_per_batch // tq
    kern = functools.partial(_attn_kernel, k_chunk=ATTN_K_CHUNK, seq=seq, lambda_init=lambda_init)
    return pl.pallas_call(
        kern,
        grid=(batch, DA_HEADS, nq),
        in_specs=[
            pl.BlockSpec((4, DA_DIM), lambda b, h, i: (0, 0)),
            pl.BlockSpec((tq, DA_VDIM), lambda b, h, i: (b * nq + i, h)),
            pl.BlockSpec((rows_per_batch, DA_VDIM), lambda b, h, i: (b, DA_HEADS + h)),
            pl.BlockSpec((DA_VDIM, rows_per_batch), lambda b, h, i: (h, b)),
        ],
        out_specs=pl.BlockSpec((tq, DA_VDIM), lambda b, h, i: (b * nq + i, h)),
        out_shape=jax.ShapeDtypeStruct((qk.shape[0], DA_WIDTH), BF16),
        scratch_shapes=[pltpu.VMEM((rows_per_batch // ATTN_K_CHUNK, ATTN_K_CHUNK, 2 * tq), F32)],
        compiler_params=_cparams("arbitrary", "arbitrary", "arbitrary"),
        name="diff_attn",
    )(lam_vec, qk, qk, vda)


def _ret_kernel(ld_ref, f_ref, b_ref, of_ref, ob_ref, dm_ref, qd_ref, kd_ref, cd_ref, st_ref):
    c = pl.program_id(1)
    ch = RET_CHUNK
    w = RET_WIDTH
    lane_head = lax.broadcasted_iota(jnp.int32, (1, w), 1) // RET_DK

    @pl.when(c == 0)
    def _():
        st_ref[...] = jnp.zeros_like(st_ref)
        ri = lax.broadcasted_iota(jnp.int32, (ch, ch), 0)
        ci = lax.broadcasted_iota(jnp.int32, (ch, ch), 1)
        rowf = lax.broadcasted_iota(jnp.int32, (ch, w), 0).astype(F32)
        for d in range(2):
            lg_lane = jnp.zeros((1, w), F32)
            for hh in range(RET_HEADS):
                lg = -jnp.exp(jnp.full((1, 1), ld_ref[d, hh], F32))
                lg_lane = jnp.where(lane_head == hh, lg, lg_lane)
                dist = ((ri - ci) if d == 0 else (ci - ri)).astype(F32)
                dm_ref[d, hh] = jnp.where(dist >= 0, jnp.exp(dist * lg), 0.0)
            if d == 0:
                qd_ref[d] = jnp.exp((rowf + 1.0) * lg_lane)
                kd_ref[d] = jnp.exp((ch - 1.0 - rowf) * lg_lane)
            else:
                qd_ref[d] = jnp.exp((ch - rowf) * lg_lane)
                kd_ref[d] = jnp.exp(rowf * lg_lane)
            cd_ref[d] = jnp.exp(float(ch) * lg_lane)

    rblk = lax.broadcasted_iota(jnp.int32, (w, w), 0) // RET_DK
    cblk = lax.broadcasted_iota(jnp.int32, (w, w), 1) // RET_DK
    for d, (src, dst) in enumerate(((f_ref, of_ref), (b_ref, ob_ref))):
        q = src[:, 0:w]
        k = src[:, w:2 * w]
        v = src[:, 2 * w:3 * w]
        st = st_ref[d]
        o = jnp.dot((q.astype(F32) * qd_ref[d]).astype(BF16), st.astype(BF16), preferred_element_type=F32)
        for hh in range(RET_HEADS):
            in_head = lane_head == hh
            qm = jnp.where(in_head, q, jnp.zeros_like(q))
            s = lax.dot_general(qm, k, (((1,), (1,)), ((), ())), preferred_element_type=F32)
            intra = (s * dm_ref[d, hh]).astype(BF16)
            o = o + jnp.where(in_head, jnp.dot(intra, v, preferred_element_type=F32), 0.0)
        dst[...] = o
        kk_t = (k.astype(F32) * kd_ref[d]).T.astype(BF16)
        upd = jnp.dot(kk_t, v, preferred_element_type=F32)
        st_ref[d] = jnp.where(rblk == cblk, st * cd_ref[d] + upd, 0.0)


def _ret_call(log_decay, rqkv, *, batch, rows_per_batch, seq):
    ch = RET_CHUNK
    nc = rows_per_batch // ch
    n_lat = seq // ch
    n_ctx = nc - n_lat

    def fwd(b, c):
        return (b * nc + jnp.where(c < n_ctx, n_lat + c, c - n_ctx), 0)

    def bwd(b, c):
        return (b * nc + nc - 1 - c, 0)

    w = RET_WIDTH
    return pl.pallas_call(
        _ret_kernel,
        grid=(batch, nc),
        in_specs=[
            pl.BlockSpec(memory_space=pltpu.SMEM),
            pl.BlockSpec((ch, 3 * w), fwd),
            pl.BlockSpec((ch, 3 * w), bwd),
        ],
        out_specs=[pl.BlockSpec((ch, w), fwd), pl.BlockSpec((ch, w), bwd)],
        out_shape=[jax.ShapeDtypeStruct((rqkv.shape[0], w), F32)] * 2,
        scratch_shapes=[
            pltpu.VMEM((2, RET_HEADS, ch, ch), F32),
            pltpu.VMEM((2, ch, w), F32),
            pltpu.VMEM((2, ch, w), F32),
            pltpu.VMEM((2, 1, w), F32),
            pltpu.VMEM((2, w, w), F32),
        ],
        compiler_params=_cparams("arbitrary", "arbitrary"),
        name="retention",
    )(log_decay, rqkv, rqkv)


def _mixout_kernel(x_ref, da_ref, u_ref, up_ref, un_ref, of_ref, ob_ref, rg_ref, mod_ref, wo_ref, pw_ref,
                   ps_ref, lng_ref, lnb_ref, o_ref, *, tiles_per_batch, seq, alpha):
    d = D_MODEL
    t = x_ref.shape[0]
    i = pl.program_id(0)
    j = i % tiles_per_batch
    is_ctx = j == tiles_per_batch - 1
    stream_len = jnp.where(is_ctx, CTX_LEN, seq)
    p0 = jnp.where(is_ctx, 0, j * t)

    u = u_ref[...]
    prev = jnp.where(p0 > 0, up_ref[...], 0.0)
    nxt = jnp.where(p0 + t < stream_len, un_ref[...], 0.0)
    ext = jnp.concatenate([prev, u, nxt], axis=0)
    n = t + 2 * POOL_HALO
    a2 = ext + pltpu.roll(ext, 1, 0)
    a4 = pltpu.roll(a2, 1, 0) + pltpu.roll(a2, n - 1, 0)
    a8 = pltpu.roll(a4, 2, 0) + pltpu.roll(a4, n - 2, 0)
    a16 = pltpu.roll(a8, 4, 0) + pltpu.roll(a8, n - 4, 0)
    pos = p0 + lax.broadcasted_iota(jnp.int32, (t, POOL_WIDTH), 0)
    group = lax.broadcasted_iota(jnp.int32, (1, POOL_WIDTH), 1) // POOL_GROUP
    mean = jnp.zeros((t, POOL_WIDTH), F32)
    for gi, (wnd, asum) in enumerate(zip(POOL_WINDOWS, (a2, a4, a8, a16))):
        cnt = jnp.minimum(pos + wnd // 2, stream_len) - jnp.maximum(pos - wnd // 2, 0)
        mean = jnp.where(group == gi, asum[POOL_HALO:POOL_HALO + t] / cnt.astype(F32), mean)
    pool = jnp.dot((mean - u).astype(BF16), pw_ref[...], preferred_element_type=F32) * ps_ref[...]

    o = of_ref[...] + ob_ref[...]
    head = lax.broadcasted_iota(jnp.int32, (1, RET_WIDTH), 1) // RET_DK

    def head_mean(val):
        out = jnp.zeros_like(val)
        for hh in range(RET_HEADS):
            m = jnp.sum(jnp.where(head == hh, val, 0.0), axis=-1, keepdims=True) * (1.0 / RET_DK)
            out = jnp.where(head == hh, m, out)
        return out

    oc = o - head_mean(o)
    rn = oc * lax.rsqrt(head_mean(oc * oc) + LN_EPS)
    g = rg_ref[...]
    ret = rn * (g * _sigmoid(g))

    y = jnp.dot(da_ref[...], wo_ref[0:DA_WIDTH, :], preferred_element_type=F32)
    y = y + jnp.dot(pool.astype(BF16), wo_ref[DA_WIDTH:DA_WIDTH + POOL_WIDTH, :], preferred_element_type=F32)
    y = y + jnp.dot(ret.astype(BF16), wo_ref[DA_WIDTH + POOL_WIDTH:, :], preferred_element_type=F32)
    z = alpha * x_ref[...] + mod_ref[:, 2 * d:3 * d] * y
    o_ref[...] = _layer_norm_rows(z) * lng_ref[...] + lnb_ref[...]


def _mixout_call(x, da, u, o_f, o_b, rg, mod3, w_out_bf, pool_bd, pool_scale, ln_g, ln_b, *, tiles_per_batch, seq,
                 alpha):
    r, d = x.shape
    t = ROW_TILE
    nt = r // t
    hb = t // POOL_HALO
    n_halo_blocks = r // POOL_HALO
    row = lambda i: (i, 0)
    const = lambda i: (0, 0)
    kern = functools.partial(_mixout_kernel, tiles_per_batch=tiles_per_batch, seq=seq, alpha=alpha)
    return pl.pallas_call(
        kern,
        grid=(nt,),
        in_specs=[
            pl.BlockSpec((t, d), row),
            pl.BlockSpec((t, DA_WIDTH), row),
            pl.BlockSpec((t, POOL_WIDTH), row),
            pl.BlockSpec((POOL_HALO, POOL_WIDTH), lambda i: (jnp.maximum(i * hb - 1, 0), 0)),
            pl.BlockSpec((POOL_HALO, POOL_WIDTH), lambda i: (jnp.minimum((i + 1) * hb, n_halo_blocks - 1), 0)),
            pl.BlockSpec((t, RET_WIDTH), row),
            pl.BlockSpec((t, RET_WIDTH), row),
            pl.BlockSpec((t, RET_WIDTH), row),
            pl.BlockSpec((None, 1, 6 * d), lambda i: (_mod_row(i, tiles_per_batch), 0, 0)),
            pl.BlockSpec((d, d), const),
            pl.BlockSpec((POOL_WIDTH, POOL_WIDTH), const),
            pl.BlockSpec((1, POOL_WIDTH), const),
            pl.BlockSpec((1, d), const),
            pl.BlockSpec((1, d), const),
        ],
        out_specs=pl.BlockSpec((t, d), row),
        out_shape=jax.ShapeDtypeStruct((r, d), F32),
        compiler_params=_cparams("arbitrary"),
        name="mixer_out",
    )(x, da, u, u, u, o_f, o_b, rg, mod3, w_out_bf, pool_bd, pool_scale, ln_g, ln_b)


def _router_kernel(x_ref, mod_ref, wrh_ref, wrl_ref, bias_ref, wsgu_ref, wsdn_ref,
                   tokp_ref, idx_ref, gate_ref, rank_ref, cnt_ref, fsh_ref, carry_ref):
    d = D_MODEL
    t = x_ref.shape[0]
    ne = N_EXPERTS
    neg = -jnp.inf

    @pl.when(pl.program_id(0) == 0)
    def _():
        carry_ref[...] = jnp.zeros_like(carry_ref)

    tok = _layer_norm_rows(x_ref[...]) * (1.0 + mod_ref[:, 4 * d:5 * d]) + mod_ref[:, 3 * d:4 * d]
    tok_hi = tok.astype(BF16)
    tok_lo = (tok - tok_hi.astype(F32)).astype(BF16)

    bits = pltpu.bitcast(tok_hi.astype(F32), jnp.uint32)
    tokp_ref[...] = (lax.shift_right_logical(bits[:, 0:PACK_W], jnp.uint32(16))
                     | (bits[:, PACK_W:] & jnp.uint32(0xFFFF0000)))

    hs = jnp.dot(tok_hi, wsgu_ref[...], preferred_element_type=F32)
    gs, us = hs[:, 0:EXPERT_HIDDEN], hs[:, EXPERT_HIDDEN:]
    fsh_ref[...] = jnp.dot((gs * _sigmoid(gs) * us).astype(BF16), wsdn_ref[...], preferred_element_type=F32)

    nt_dims = (((1,), (1,)), ((), ()))
    logits = (lax.dot_general(wrh_ref[...], tok_hi, nt_dims, preferred_element_type=F32)
              + lax.dot_general(wrh_ref[...], tok_lo, nt_dims, preferred_element_type=F32)
              + lax.dot_general(wrl_ref[...], tok_hi, nt_dims, preferred_element_type=F32))
    scores = _sigmoid(logits)
    biased = scores + bias_ref[...]

    gidx = lax.broadcasted_iota(jnp.int32, (GROUP_SIZE, t), 0)
    blocks, gscores = [], []
    for g in range(N_GROUPS):
        blk = biased[g * GROUP_SIZE:(g + 1) * GROUP_SIZE, :]
        m1 = jnp.max(blk, axis=0, keepdims=True)
        first = jnp.min(jnp.where(blk == m1, gidx, GROUP_SIZE), axis=0, keepdims=True)
        m2 = jnp.max(jnp.where(gidx == first, neg, blk), axis=0, keepdims=True)
        blocks.append(blk)
        gscores.append(m1 + m2)

    keep = [jnp.zeros((1, t), F32) for _ in range(N_GROUPS)]
    for _ in range(TOPK_GROUPS):
        m = gscores[0]
        for gs_ in gscores[1:]:
            m = jnp.maximum(m, gs_)
        found = jnp.zeros((1, t), F32)
        for g in range(N_GROUPS):
            hit = jnp.where(gscores[g] == m, 1.0 - found, 0.0)
            found = found + hit
            keep[g] = keep[g] + hit
            gscores[g] = jnp.where(hit > 0.0, neg, gscores[g])
    masked = jnp.concatenate([jnp.where(keep[g] > 0.0, blocks[g], neg) for g in range(N_GROUPS)], axis=0)

    ei = lax.broadcasted_iota(jnp.int32, (ne, t), 0)
    cur = masked
    onehot = jnp.zeros((ne, t), F32)
    idxs, gates = [], []
    for _ in range(TOP_K):
        m = jnp.max(cur, axis=0, keepdims=True)
        ii = jnp.min(jnp.where(cur == m, ei, ne), axis=0, keepdims=True)
        sel = ei == ii
        idxs.append(ii)
        gates.append(jnp.sum(jnp.where(sel, scores, 0.0), axis=0, keepdims=True))
        onehot = jnp.where(sel, 1.0, onehot)
        cur = jnp.where(sel, neg, cur)
    gsum = gates[0]
    for gk in gates[1:]:
        gsum = gsum + gk
    for k in range(TOP_K):
        idx_ref[k:k + 1, :] = idxs[k]
        gate_ref[k:k + 1, :] = gates[k] / gsum * ROUTED_SCALE

    ti = lax.broadcasted_iota(jnp.int32, (t, t), 0)
    tj = lax.broadcasted_iota(jnp.int32, (t, t), 1)
    before = jnp.where(ti < tj, 1.0, 0.0).astype(BF16)
    prefix = jnp.dot(onehot.astype(BF16), before, preferred_element_type=F32) + carry_ref[:, 0:1]
    for k in range(TOP_K):
        rank_k = jnp.sum(jnp.where(ei == idxs[k], prefix, 0.0), axis=0, keepdims=True)
        rank_ref[k:k + 1, :] = rank_k.astype(jnp.int32)
    carry_ref[...] = carry_ref[...] + jnp.sum(onehot, axis=1, keepdims=True)
    cnt_ref[...] = carry_ref[...].astype(jnp.int32)


def _router_call(x, mod3, wr_hi, wr_lo, bias_col, ws_gu_bf, ws_dn_bf, *, tiles_per_batch):
    r, d = x.shape
    t = ROW_TILE
    nt = r // t
    row = lambda i: (i, 0)
    col = lambda i: (0, i)
    const = lambda i: (0, 0)
    return pl.pallas_call(
        _router_kernel,
        grid=(nt,),
        in_specs=[
            pl.BlockSpec((t, d), row),
            pl.BlockSpec((None, 1, 6 * d), lambda i: (_mod_row(i, tiles_per_batch), 0, 0)),
            pl.BlockSpec((N_EXPERTS, d), const),
            pl.BlockSpec((N_EXPERTS, d), const),
            pl.BlockSpec((N_EXPERTS, 1), const),
            pl.BlockSpec((d, 2 * EXPERT_HIDDEN), const),
            pl.BlockSpec((EXPERT_HIDDEN, d), const),
        ],
        out_specs=[
            pl.BlockSpec((t, PACK_W), row),
            pl.BlockSpec((TOP_K, t), col),
            pl.BlockSpec((TOP_K, t), col),
            pl.BlockSpec((TOP_K, t), col),
            pl.BlockSpec((N_EXPERTS, LANES), const),
            pl.BlockSpec((t, d), row),
        ],
        out_shape=[
            jax.ShapeDtypeStruct((r, PACK_W), jnp.uint32),
            jax.ShapeDtypeStruct((TOP_K, r), jnp.int32),
            jax.ShapeDtypeStruct((TOP_K, r), F32),
            jax.ShapeDtypeStruct((TOP_K, r), jnp.int32),
            jax.ShapeDtypeStruct((N_EXPERTS, LANES), jnp.int32),
            jax.ShapeDtypeStruct((r, d), F32),
        ],
        scratch_shapes=[pltpu.VMEM((N_EXPERTS, LANES), F32)],
        compiler_params=_cparams("arbitrary"),
        name="router",
    )(x, mod3, wr_hi, wr_lo, bias_col, ws_gu_bf, ws_dn_bf)


def _dest_kernel(idx_ref, rank_ref, offs_ref, dest_ref):
    t = idx_ref.shape[1]
    ei = lax.broadcasted_iota(jnp.int32, (N_EXPERTS, t), 0)
    offs = offs_ref[...].astype(F32)
    for k in range(TOP_K):
        start = jnp.sum(jnp.where(ei == idx_ref[k:k + 1, :], offs, 0.0), axis=0, keepdims=True)
        dest_ref[k:k + 1, :] = start.astype(jnp.int32) + rank_ref[k:k + 1, :]


def _dest_call(idx, rank, offs_col):
    r = idx.shape[1]
    t = ROW_TILE
    col = lambda i: (0, i)
    return pl.pallas_call(
        _dest_kernel,
        grid=(r // t,),
        in_specs=[pl.BlockSpec((TOP_K, t), col), pl.BlockSpec((TOP_K, t), col),
                  pl.BlockSpec((N_EXPERTS, 1), lambda i: (0, 0))],
        out_specs=pl.BlockSpec((TOP_K, t), col),
        out_shape=jax.ShapeDtypeStruct((TOP_K, r), jnp.int32),
        compiler_params=_cparams("arbitrary"),
        name="moe_dest",
    )(idx, rank, offs_col)


def _dispatch_kernel(dest_ref, tokp_ref, xs_in_ref, xs_ref, sem):
    del xs_in_ref
    t = dest_ref.shape[1]

    def issue(tt, carry):
        for k in range(TOP_K):
            pltpu.make_async_copy(tokp_ref.at[pl.ds(tt, 1), :], xs_ref.at[pl.ds(dest_ref[k, tt], 1), :], sem).start()
        return carry

    lax.fori_loop(0, t, issue, 0)
    all_rows = xs_ref.at[pl.ds(0, TOP_K * t), :]
    pltpu.make_async_copy(all_rows, all_rows, sem).wait()


def _dispatch_call(dest, tokp, xs_zero):
    r = dest.shape[1]
    t = ROW_TILE
    return pl.pallas_call(
        _dispatch_kernel,
        grid=(r // t,),
        in_specs=[
            pl.BlockSpec((TOP_K, t), lambda i: (0, i), memory_space=pltpu.SMEM),
            pl.BlockSpec((t, PACK_W), lambda i: (i, 0)),
            pl.BlockSpec(memory_space=pl.ANY),
        ],
        out_specs=pl.BlockSpec(memory_space=pl.ANY),
        out_shape=jax.ShapeDtypeStruct(xs_zero.shape, xs_zero.dtype),
        scratch_shapes=[pltpu.SemaphoreType.DMA(())],
        input_output_aliases={2: 0},
        compiler_params=_cparams("arbitrary"),
        name="moe_dispatch",
    )(dest, tokp, xs_zero)


def _expert_kernel(be_ref, nb_ref, xs_ref, wgu_ref, wdn_ref, ys_ref, wgu_bf, wdn_bf):
    j = pl.program_id(0)

    @pl.when(j < nb_ref[0])
    def _():
        changed = jnp.logical_or(j == 0, be_ref[j] != be_ref[jnp.maximum(j - 1, 0)])

        @pl.when(changed)
        def _():
            wgu_bf[...] = wgu_ref[...].astype(BF16)
            wdn_bf[...] = wdn_ref[...].astype(BF16)

        word = xs_ref[...]
        x_lo = pltpu.bitcast(lax.shift_left(word, jnp.uint32(16)), F32).astype(BF16)
        x_hi = pltpu.bitcast(word & jnp.uint32(0xFFFF0000), F32).astype(BF16)
        h = (jnp.dot(x_lo, wgu_bf[0:PACK_W, :], preferred_element_type=F32)
             + jnp.dot(x_hi, wgu_bf[PACK_W:, :], preferred_element_type=F32))
        g, u = h[:, 0:EXPERT_HIDDEN], h[:, EXPERT_HIDDEN:]
        ys_ref[...] = jnp.dot((g * _sigmoid(g) * u).astype(BF16), wdn_bf[...], preferred_element_type=F32)


def _expert_call(block_expert, n_blocks_used, xs, w_gu, w_dn, layer):
    n_rows = xs.shape[0]
    bm = EXPERT_BLOCK
    d = D_MODEL
    grid_spec = pltpu.PrefetchScalarGridSpec(
        num_scalar_prefetch=2,
        grid=(n_rows // bm,),
        in_specs=[
            pl.BlockSpec((bm, PACK_W), lambda j, be, nb: (j, 0)),
            pl.BlockSpec((None, None, d, 2 * EXPERT_HIDDEN), lambda j, be, nb: (layer, be[j], 0, 0)),
            pl.BlockSpec((None, None, EXPERT_HIDDEN, d), lambda j, be, nb: (layer, be[j], 0, 0)),
        ],
        out_specs=pl.BlockSpec((bm, d), lambda j, be, nb: (j, 0)),
        scratch_shapes=[pltpu.VMEM((d, 2 * EXPERT_HIDDEN), BF16), pltpu.VMEM((EXPERT_HIDDEN, d), BF16)],
    )
    return pl.pallas_call(
        _expert_kernel,
        grid_spec=grid_spec,
        out_shape=jax.ShapeDtypeStruct((n_rows, d), F32),
        compiler_params=_cparams("arbitrary"),
        name="moe_experts",
    )(block_expert, n_blocks_used, xs, w_gu, w_dn)


def _combine_kernel(dest_ref, ys_ref, x_ref, fsh_ref, gate_ref, mod_ref, lng_ref, lnb_ref, o_ref, buf, sem, *,
                    alpha):
    d = D_MODEL
    t = x_ref.shape[0]

    def issue(tt, carry):
        for k in range(TOP_K):
            pltpu.make_async_copy(ys_ref.at[pl.ds(dest_ref[k, tt], 1), :], buf.at[pl.ds(k * t + tt, 1), :],
                                  sem).start()
        return carry

    lax.fori_loop(0, t, issue, 0)
    pltpu.make_async_copy(ys_ref.at[pl.ds(0, TOP_K * t), :], buf, sem).wait()

    gate_rows = gate_ref[...]
    pad = jnp.zeros((LANES - TOP_K, t), F32)
    gate_cols = jnp.concatenate([gate_rows, pad], axis=0).T
    f = fsh_ref[...]
    for k in range(TOP_K):
        f = f + gate_cols[:, k:k + 1] * buf[pl.ds(k * t, t), :]
    z = alpha * x_ref[...] + mod_ref[:, 5 * d:6 * d] * f
    o_ref[...] = _layer_norm_rows(z) * lng_ref[...] + lnb_ref[...]


def _combine_call(dest, ys, x, fsh, gate, mod3, ln_g, ln_b, *, tiles_per_batch, alpha):
    r, d = x.shape
    t = ROW_TILE
    row = lambda i: (i, 0)
    const = lambda i: (0, 0)
    kern = functools.partial(_combine_kernel, alpha=alpha)
    return pl.pallas_call(
        kern,
        grid=(r // t,),
        in_specs=[
            pl.BlockSpec((TOP_K, t), lambda i: (0, i), memory_space=pltpu.SMEM),
            pl.BlockSpec(memory_space=pl.ANY),
            pl.BlockSpec((t, d), row),
            pl.BlockSpec((t, d), row),
            pl.BlockSpec((TOP_K, t), lambda i: (0, i)),
            pl.BlockSpec((None, 1, 6 * d), lambda i: (_mod_row(i, tiles_per_batch), 0, 0)),
            pl.BlockSpec((1, d), const),
            pl.BlockSpec((1, d), const),
        ],
        out_specs=pl.BlockSpec((t, d), row),
        out_shape=jax.ShapeDtypeStruct((r, d), F32),
        scratch_shapes=[pltpu.VMEM((TOP_K * t, d), F32), pltpu.SemaphoreType.DMA(())],
        compiler_params=_cparams("arbitrary"),
        name="moe_combine",
    )(dest, ys, x, fsh, gate, mod3, ln_g, ln_b)


def _rope_tables(seq):
    rows = seq // GRID_W
    row = jnp.repeat(jnp.arange(rows, dtype=F32), GRID_W)
    col = jnp.tile(jnp.arange(GRID_W, dtype=F32), rows)
    nf = DA_DIM // 4
    freqs = ROPE_BASE ** (-jnp.arange(nf, dtype=F32) / nf)
    cr, sr = jnp.cos(row[:, None] * freqs), jnp.sin(row[:, None] * freqs)
    cc, sc = jnp.cos(col[:, None] * freqs), jnp.sin(col[:, None] * freqs)
    c64 = jnp.concatenate([cr, cr, cc, cc], axis=1)
    s64 = jnp.concatenate([-sr, sr, -sc, sc], axis=1)
    c = jnp.concatenate([jnp.tile(c64, (1, 2)), jnp.ones((CTX_LEN, LANES), F32)], axis=0)
    s = jnp.concatenate([jnp.tile(s64, (1, 2)), jnp.zeros((CTX_LEN, LANES), F32)], axis=0)
    return c, s


def kernel(x, c, ctx, c_ctx, w_mod, b_mod, w_in, w_out, diff_lambda, pool_w, pool_scale, ret_log_decay, ln_g, ln_b,
           w_router, router_bias, w_expert_gate_up, w_expert_down, w_shared_gate_up, w_shared_down):
    batch, seq, d = x.shape
    depth = w_mod.shape[0]
    assert d == D_MODEL and ctx.shape[1] == CTX_LEN == ROW_TILE and batch == 2
    assert seq % ROW_TILE == 0 and seq % GRID_W == 0 and w_in.shape[-1] == IN_WIDTH
    rows_per_batch = seq + CTX_LEN
    tiles_per_batch = rows_per_batch // ROW_TILE
    r = batch * rows_per_batch
    alpha = (2.0 * depth) ** 0.25

    xa = jnp.concatenate([x, ctx], axis=1).reshape(r, d)
    cvec = jnp.zeros((8, d), F32).at[0:batch].set(c).at[batch].set(c_ctx)
    mod_all = _mod_call(cvec, w_mod, b_mod)
    rope_c, rope_s = _rope_tables(seq)

    n_sorted = r * TOP_K + N_EXPERTS * EXPERT_BLOCK
    n_blocks = n_sorted // EXPERT_BLOCK

    for l in range(depth):
        lambda_init = 0.8 - 0.6 * math.exp(-0.3 * l)
        mod3 = mod_all[l].reshape(8, 1, 6 * d)
        lng = ln_g[l].reshape(2, 1, d)
        lnb = ln_b[l].reshape(2, 1, d)

        w_in_bf = w_in[l].astype(BF16)
        w_vt_bf = w_in_bf[:, QK_WIDTH:QK_WIDTH + DA_WIDTH].T
        qk, vda, u, rqkv, rg = _inproj_call(xa, mod3, w_in_bf, w_vt_bf, rope_c, rope_s, tiles_per_batch)
        da = _attn_call(diff_lambda[l], qk, vda, batch=batch, rows_per_batch=rows_per_batch, seq=seq,
                        lambda_init=lambda_init)
        o_f, o_b = _ret_call(ret_log_decay[l], rqkv, batch=batch, rows_per_batch=rows_per_batch, seq=seq)
        pool_bd = jnp.zeros((POOL_WIDTH, POOL_WIDTH), F32)
        for gi in range(len(POOL_WINDOWS)):
            sl = slice(gi * POOL_GROUP, (gi + 1) * POOL_GROUP)
            pool_bd = pool_bd.at[sl, sl].set(pool_w[l, gi])
        xa = _mixout_call(xa, da, u, o_f, o_b, rg, mod3, w_out[l].astype(BF16), pool_bd.astype(BF16),
                          pool_scale[l].reshape(1, POOL_WIDTH), lng[0], lnb[0],
                          tiles_per_batch=tiles_per_batch, seq=seq, alpha=alpha)

        wr_t = w_router[l].T
        wr_hi = wr_t.astype(BF16)
        wr_lo = (wr_t - wr_hi.astype(F32)).astype(BF16)
        tokp, idx, gate, rank, cnt, fsh = _router_call(
            xa, mod3, wr_hi, wr_lo, router_bias[l].reshape(N_EXPERTS, 1),
            w_shared_gate_up[l].astype(BF16), w_shared_down[l].astype(BF16), tiles_per_batch=tiles_per_batch)
        counts = cnt[:, 0]
        padded = (counts + EXPERT_BLOCK - 1) // EXPERT_BLOCK * EXPERT_BLOCK
        pad_end = jnp.cumsum(padded)
        offs = pad_end - padded
        block_expert = jnp.minimum(
            jnp.searchsorted(pad_end, jnp.arange(n_blocks, dtype=jnp.int32) * EXPERT_BLOCK, side='right'),
            N_EXPERTS - 1).astype(jnp.int32)
        n_used = (pad_end[-1:] // EXPERT_BLOCK).astype(jnp.int32)
        dest = _dest_call(idx, rank, offs.reshape(N_EXPERTS, 1).astype(jnp.int32))
        xs = _dispatch_call(dest, tokp, jnp.zeros((n_sorted, PACK_W), jnp.uint32))
        ys = _expert_call(block_expert, n_used, xs, w_expert_gate_up, w_expert_down, l)
        xa = _combine_call(dest, ys, xa, fsh, gate, mod3, lng[1], lnb[1],
                           tiles_per_batch=tiles_per_batch, alpha=alpha)

    return xa.reshape(batch, rows_per_batch, d)[:, :seq]
```

```python
import functools
import math

import jax
import jax.numpy as jnp
from jax import lax
from jax.experimental import pallas as pl
from jax.experimental.pallas import tpu as pltpu

F32 = jnp.float32
BF16 = jnp.bfloat16
HIGHEST = lax.Precision.HIGHEST

D_MODEL = 1024
CTX_LEN = 256
GRID_W = 64
DA_HEADS = 4
DA_DIM = 64
DA_VDIM = 2 * DA_DIM
DA_WIDTH = DA_HEADS * DA_VDIM
ROPE_BASE = 10000.0
POOL_WINDOWS = (2, 4, 8, 16)
POOL_GROUP = 64
POOL_WIDTH = len(POOL_WINDOWS) * POOL_GROUP
POOL_HALO = 8
RET_HEADS = 4
RET_DK = 64
RET_WIDTH = RET_HEADS * RET_DK
RET_CHUNK = 128
QK_WIDTH = 2 * DA_HEADS * 2 * DA_DIM
IN_WIDTH = QK_WIDTH + DA_WIDTH + POOL_WIDTH + 4 * RET_WIDTH
N_EXPERTS = 256
TOP_K = 8
N_GROUPS = 8
GROUP_SIZE = N_EXPERTS // N_GROUPS
TOPK_GROUPS = 4
EXPERT_HIDDEN = 256
ROUTED_SCALE = 2.5
LN_EPS = 1e-6
RMS_EPS = 1e-5

LANES = 128
ROW_TILE = 256
ATTN_Q_TILE = 256
ATTN_K_CHUNK = 256
ATTN_UNROLL = 16
EXPERT_BLOCK = 256
PACK_W = D_MODEL // 2
PACK_S = PACK_W // LANES
ROW_S = D_MODEL // LANES
VMEM_LIMIT = 56 * 1024 * 1024


def _cparams(*sem):
    return pltpu.CompilerParams(dimension_semantics=sem, vmem_limit_bytes=VMEM_LIMIT)


def _sigmoid(x):
    return 1.0 / (1.0 + jnp.exp(-x))


def _layer_norm_rows(x):
    mu = jnp.mean(x, axis=-1, keepdims=True)
    xc = x - mu
    var = jnp.mean(xc * xc, axis=-1, keepdims=True)
    return xc * lax.rsqrt(var + LN_EPS)


def _mod_row(i, tiles_per_batch):
    return jnp.where(i % tiles_per_batch == tiles_per_batch - 1, 2, i // tiles_per_batch)


def _mod_kernel(c_ref, w_ref, b_ref, o_ref):
    c = c_ref[...]
    s = c * _sigmoid(c)
    o_ref[...] = jnp.dot(s, w_ref[...], precision=HIGHEST, preferred_element_type=F32) + b_ref[...]


def _mod_call(cvec, w_mod, b_mod):
    depth, d, n = w_mod.shape
    tn = 1536
    return pl.pallas_call(
        _mod_kernel,
        grid=(depth, n // tn),
        in_specs=[
            pl.BlockSpec((8, d), lambda l, j: (0, 0)),
            pl.BlockSpec((None, d, tn), lambda l, j: (l, 0, j)),
            pl.BlockSpec((None, 1, tn), lambda l, j: (l, 0, j)),
        ],
        out_specs=pl.BlockSpec((None, 8, tn), lambda l, j: (l, 0, j)),
        out_shape=jax.ShapeDtypeStruct((depth, 8, n), F32),
        compiler_params=_cparams("arbitrary", "arbitrary"),
        name="mod",
    )(cvec, w_mod, b_mod.reshape(depth, 1, n))


def _inproj_kernel(x_ref, mod_ref, w_ref, wvt_ref, ct_ref, st_ref, qk_ref, vt_ref, u_ref, r_ref, g_ref):
    d = D_MODEL
    xn = _layer_norm_rows(x_ref[...])
    h = (xn * (1.0 + mod_ref[:, d:2 * d]) + mod_ref[:, 0:d]).astype(BF16)

    a = jnp.dot(h, w_ref[:, 0:QK_WIDTH], preferred_element_type=F32)
    lane = lax.broadcasted_iota(jnp.int32, (a.shape[0], LANES), 1)
    first_half = (lane % 32) < 16
    ct = ct_ref[...]
    st = st_ref[...]
    for s in range(QK_WIDTH // LANES):
        blk = a[:, s * LANES:(s + 1) * LANES]
        partner = jnp.where(first_half, pltpu.roll(blk, LANES - 16, 1), pltpu.roll(blk, 16, 1))
        rot = blk * ct + partner * st
        if s < QK_WIDTH // LANES // 2:
            rot = rot * (DA_DIM ** -0.5 * math.log2(math.e))
        qk_ref[:, s * LANES:(s + 1) * LANES] = rot.astype(BF16)

    vt_ref[...] = lax.dot_general(wvt_ref[...], h, (((1,), (1,)), ((), ())),
                                  preferred_element_type=F32).astype(BF16)
    o = QK_WIDTH + DA_WIDTH
    u_ref[...] = jnp.dot(h, w_ref[:, o:o + POOL_WIDTH], preferred_element_type=F32)
    o += POOL_WIDTH
    r = jnp.dot(h, w_ref[:, o:o + 3 * RET_WIDTH], preferred_element_type=F32)
    r_ref[:, 0:RET_WIDTH] = r[:, 0:RET_WIDTH].astype(BF16)
    r_ref[:, RET_WIDTH:2 * RET_WIDTH] = (r[:, RET_WIDTH:2 * RET_WIDTH] * (RET_DK ** -0.5)).astype(BF16)
    r_ref[:, 2 * RET_WIDTH:] = r[:, 2 * RET_WIDTH:].astype(BF16)
    o += 3 * RET_WIDTH
    g_ref[...] = jnp.dot(h, w_ref[:, o:o + RET_WIDTH], preferred_element_type=F32)


def _inproj_call(x, mod3, w_in_bf, w_vt_bf, rope_c, rope_s, tiles_per_batch):
    r, d = x.shape
    t = ROW_TILE
    nt = r // t
    row = lambda i: (i, 0)
    return pl.pallas_call(
        _inproj_kernel,
        grid=(nt,),
        in_specs=[
            pl.BlockSpec((t, d), row),
            pl.BlockSpec((None, 1, 6 * d), lambda i: (_mod_row(i, tiles_per_batch), 0, 0)),
            pl.BlockSpec((d, IN_WIDTH), lambda i: (0, 0)),
            pl.BlockSpec((DA_WIDTH, d), lambda i: (0, 0)),
            pl.BlockSpec((t, LANES), lambda i: (i % tiles_per_batch, 0)),
            pl.BlockSpec((t, LANES), lambda i: (i % tiles_per_batch, 0)),
        ],
        out_specs=[
            pl.BlockSpec((t, QK_WIDTH), row),
            pl.BlockSpec((DA_WIDTH, t), lambda i: (0, i)),
            pl.BlockSpec((t, POOL_WIDTH), row),
            pl.BlockSpec((t, 3 * RET_WIDTH), row),
            pl.BlockSpec((t, RET_WIDTH), row),
        ],
        out_shape=[
            jax.ShapeDtypeStruct((r, QK_WIDTH), BF16),
            jax.ShapeDtypeStruct((DA_WIDTH, r), BF16),
            jax.ShapeDtypeStruct((r, POOL_WIDTH), F32),
            jax.ShapeDtypeStruct((r, 3 * RET_WIDTH), BF16),
            jax.ShapeDtypeStruct((r, RET_WIDTH), F32),
        ],
        compiler_params=_cparams("arbitrary"),
        name="inproj",
    )(x, mod3, w_in_bf, w_vt_bf, rope_c, rope_s)


def _attn_kernel(lam_ref, q_ref, k_ref, vt_ref, o_ref, s_ref, *, k_chunk, seq, lambda_init):
    q = q_ref[...]
    mq = q.shape[0]
    lane = lax.broadcasted_iota(jnp.int32, q.shape, 1)
    zero = jnp.zeros_like(q)
    q2 = jnp.concatenate([jnp.where(lane < DA_DIM, q, zero), jnp.where(lane >= DA_DIM, q, zero)], axis=0)
    qt = q2.astype(F32).T.astype(BF16)

    n_chunks = (seq + CTX_LEN) // k_chunk
    last = n_chunks - 1
    is_ctx_tile = pl.program_id(2) == pl.num_programs(2) - 1
    n_iters = jnp.where(is_ctx_tile, 0, last // ATTN_UNROLL)

    def score_chunk(c, m):
        off = pl.multiple_of(c * k_chunk, k_chunk)
        s = jnp.dot(k_ref[pl.ds(off, k_chunk), :], qt, preferred_element_type=F32)
        s_ref[c] = s
        return jnp.maximum(m, jnp.max(s, axis=0, keepdims=True))

    def pass1(it, m):
        for u in range(ATTN_UNROLL):
            m = score_chunk(it * ATTN_UNROLL + u, m)
        return m

    m = lax.fori_loop(0, n_iters, pass1, jnp.full((1, 2 * mq), -jnp.inf, F32))
    m = score_chunk(last, m)

    ones_rows = jnp.where(lax.broadcasted_iota(jnp.int32, (16, k_chunk), 0) == 0, 1.0, 0.0).astype(BF16)

    def value_chunk(c, acc):
        off = pl.multiple_of(c * k_chunk, k_chunk)
        vt = jnp.concatenate([vt_ref[:, pl.ds(off, k_chunk)], ones_rows], axis=0)
        p = jnp.exp2((s_ref[c] - m).astype(BF16))
        return acc + jnp.dot(vt, p, preferred_element_type=F32)

    def pass2(it, acc):
        for u in range(ATTN_UNROLL):
            acc = value_chunk(it * ATTN_UNROLL + u, acc)
        return acc

    acc = lax.fori_loop(0, n_iters, pass2, jnp.zeros((DA_VDIM + 16, 2 * mq), F32))
    acc = value_chunk(last, acc)
    l0, l1 = acc[DA_VDIM:DA_VDIM + 1, 0:mq], acc[DA_VDIM:DA_VDIM + 1, mq:]
    a0, a1 = acc[0:DA_VDIM, 0:mq], acc[0:DA_VDIM, mq:]

    lv = lam_ref[...]
    lam = (jnp.exp(jnp.sum(lv[0:1] * lv[1:2], axis=-1, keepdims=True))
           - jnp.exp(jnp.sum(lv[2:3] * lv[3:4], axis=-1, keepdims=True)) + lambda_init)
    o = a0 / l0 - lam * (a1 / l1)
    o = o * lax.rsqrt(jnp.mean(o * o, axis=0, keepdims=True) + RMS_EPS) * (1.0 - lambda_init)
    o_ref[...] = o.T.astype(BF16)


def _attn_call(lam_vec, qk, vda, *, batch, rows_per_batch, seq, lambda_init):
    tq = ATTN_Q_TILE
    assert seq % (ATTN_K_CHUNK * ATTN_UNROLL) == 0 and rows_per_batch - seq == CTX_LEN == tq == ATTN_K_CHUNK
    nq = rows_per_batch // tq
    kern = functools.partial(_attn_kernel, k_chunk=ATTN_K_CHUNK, seq=seq, lambda_init=lambda_init)
    return pl.pallas_call(
        kern,
        grid=(batch, DA_HEADS, nq),
        in_specs=[
            pl.BlockSpec((4, DA_DIM), lambda b, h, i: (0, 0)),
            pl.BlockSpec((tq, DA_VDIM), lambda b, h, i: (b * nq + i, h)),
            pl.BlockSpec((rows_per_batch, DA_VDIM), lambda b, h, i: (b, DA_HEADS + h)),
            pl.BlockSpec((DA_VDIM, rows_per_batch), lambda b, h, i: (h, b)),
        ],
        out_specs=pl.BlockSpec((tq, DA_VDIM), lambda b, h, i: (b * nq + i, h)),
        out_shape=jax.ShapeDtypeStruct((qk.shape[0], DA_WIDTH), BF16),
        scratch_shapes=[pltpu.VMEM((rows_per_batch // ATTN_K_CHUNK, ATTN_K_CHUNK, 2 * tq), F32)],
        compiler_params=_cparams("arbitrary", "arbitrary", "arbitrary"),
        name="diff_attn",
    )(lam_vec, qk, qk, vda)


def _ret_kernel(ld_ref, f_ref, b_ref, of_ref, ob_ref, dm_ref, qd_ref, kd_ref, cd_ref, st_ref):
    c = pl.program_id(1)
    ch = RET_CHUNK
    w = RET_WIDTH
    lane_head = lax.broadcasted_iota(jnp.int32, (1, w), 1) // RET_DK

    @pl.when(c == 0)
    def _():
        st_ref[...] = jnp.zeros_like(st_ref)
        ri = lax.broadcasted_iota(jnp.int32, (ch, ch), 0)
        ci = lax.broadcasted_iota(jnp.int32, (ch, ch), 1)
        rowf = lax.broadcasted_iota(jnp.int32, (ch, w), 0).astype(F32)
        for d in range(2):
            lg_lane = jnp.zeros((1, w), F32)
            for hh in range(RET_HEADS):
                lg = -jnp.exp(jnp.full((1, 1), ld_ref[d, hh], F32))
                lg_lane = jnp.where(lane_head == hh, lg, lg_lane)
                dist = ((ri - ci) if d == 0 else (ci - ri)).astype(F32)
                dm_ref[d, hh] = jnp.where(dist >= 0, jnp.exp(dist * lg), 0.0)
            if d == 0:
                qd_ref[d] = jnp.exp((rowf + 1.0) * lg_lane)
                kd_ref[d] = jnp.exp((ch - 1.0 - rowf) * lg_lane)
            else:
                qd_ref[d] = jnp.exp((ch - rowf) * lg_lane)
                kd_ref[d] = jnp.exp(rowf * lg_lane)
            cd_ref[d] = jnp.exp(float(ch) * lg_lane)

    rblk = lax.broadcasted_iota(jnp.int32, (w, w), 0) // RET_DK
    cblk = lax.broadcasted_iota(jnp.int32, (w, w), 1) // RET_DK
    for d, (src, dst) in enumerate(((f_ref, of_ref), (b_ref, ob_ref))):
        q = src[:, 0:w]
        k = src[:, w:2 * w]
        v = src[:, 2 * w:3 * w]
        st = st_ref[d]
        o = jnp.dot((q.astype(F32) * qd_ref[d]).astype(BF16), st.astype(BF16), preferred_element_type=F32)
        for hh in range(RET_HEADS):
            in_head = lane_head == hh
            qm = jnp.where(in_head, q, jnp.zeros_like(q))
            s = lax.dot_general(qm, k, (((1,), (1,)), ((), ())), preferred_element_type=F32)
            intra = (s * dm_ref[d, hh]).astype(BF16)
            o = o + jnp.where(in_head, jnp.dot(intra, v, preferred_element_type=F32), 0.0)
        dst[...] = o
        kk_t = (k.astype(F32) * kd_ref[d]).T.astype(BF16)
        upd = jnp.dot(kk_t, v, preferred_element_type=F32)
        st_ref[d] = jnp.where(rblk == cblk, st * cd_ref[d] + upd, 0.0)


def _ret_call(log_decay, rqkv, *, batch, rows_per_batch, seq):
    ch = RET_CHUNK
    nc = rows_per_batch // ch
    n_lat = seq // ch
    n_ctx = nc - n_lat

    def fwd(b, c):
        return (b * nc + jnp.where(c < n_ctx, n_lat + c, c - n_ctx), 0)

    def bwd(b, c):
        return (b * nc + nc - 1 - c, 0)

    w = RET_WIDTH
    return pl.pallas_call(
        _ret_kernel,
        grid=(batch, nc),
        in_specs=[
            pl.BlockSpec(memory_space=pltpu.SMEM),
            pl.BlockSpec((ch, 3 * w), fwd),
            pl.BlockSpec((ch, 3 * w), bwd),
        ],
        out_specs=[pl.BlockSpec((ch, w), fwd), pl.BlockSpec((ch, w), bwd)],
        out_shape=[jax.ShapeDtypeStruct((rqkv.shape[0], w), F32)] * 2,
        scratch_shapes=[
            pltpu.VMEM((2, RET_HEADS, ch, ch), F32),
            pltpu.VMEM((2, ch, w), F32),
            pltpu.VMEM((2, ch, w), F32),
            pltpu.VMEM((2, 1, w), F32),
            pltpu.VMEM((2, w, w), F32),
        ],
        compiler_params=_cparams("arbitrary", "arbitrary"),
        name="retention",
    )(log_decay, rqkv, rqkv)


def _mixout_kernel(x_ref, da_ref, u_ref, up_ref, un_ref, of_ref, ob_ref, rg_ref, mod_ref, wo_ref, pw_ref,
                   ps_ref, lng_ref, lnb_ref, o_ref, *, tiles_per_batch, seq, alpha):
    d = D_MODEL
    t = x_ref.shape[0]
    i = pl.program_id(0)
    j = i % tiles_per_batch
    is_ctx = j == tiles_per_batch - 1
    stream_len = jnp.where(is_ctx, CTX_LEN, seq)
    p0 = jnp.where(is_ctx, 0, j * t)

    u = u_ref[...]
    prev = jnp.where(p0 > 0, up_ref[...], 0.0)
    nxt = jnp.where(p0 + t < stream_len, un_ref[...], 0.0)
    ext = jnp.concatenate([prev, u, nxt], axis=0)
    n = t + 2 * POOL_HALO
    a2 = ext + pltpu.roll(ext, 1, 0)
    a4 = pltpu.roll(a2, 1, 0) + pltpu.roll(a2, n - 1, 0)
    a8 = pltpu.roll(a4, 2, 0) + pltpu.roll(a4, n - 2, 0)
    a16 = pltpu.roll(a8, 4, 0) + pltpu.roll(a8, n - 4, 0)
    pos = p0 + lax.broadcasted_iota(jnp.int32, (t, POOL_WIDTH), 0)
    group = lax.broadcasted_iota(jnp.int32, (1, POOL_WIDTH), 1) // POOL_GROUP
    mean = jnp.zeros((t, POOL_WIDTH), F32)
    for gi, (wnd, asum) in enumerate(zip(POOL_WINDOWS, (a2, a4, a8, a16))):
        cnt = jnp.minimum(pos + wnd // 2, stream_len) - jnp.maximum(pos - wnd // 2, 0)
        mean = jnp.where(group == gi, asum[POOL_HALO:POOL_HALO + t] / cnt.astype(F32), mean)
    pool = jnp.dot((mean - u).astype(BF16), pw_ref[...], preferred_element_type=F32) * ps_ref[...]

    o = of_ref[...] + ob_ref[...]
    head = lax.broadcasted_iota(jnp.int32, (1, RET_WIDTH), 1) // RET_DK

    def head_mean(val):
        out = jnp.zeros_like(val)
        for hh in range(RET_HEADS):
            m = jnp.sum(jnp.where(head == hh, val, 0.0), axis=-1, keepdims=True) * (1.0 / RET_DK)
            out = jnp.where(head == hh, m, out)
        return out

    oc = o - head_mean(o)
    rn = oc * lax.rsqrt(head_mean(oc * oc) + LN_EPS)
    g = rg_ref[...]
    ret = rn * (g * _sigmoid(g))

    y = jnp.dot(da_ref[...], wo_ref[0:DA_WIDTH, :], preferred_element_type=F32)
    y = y + jnp.dot(pool.astype(BF16), wo_ref[DA_WIDTH:DA_WIDTH + POOL_WIDTH, :], preferred_element_type=F32)
    y = y + jnp.dot(ret.astype(BF16), wo_ref[DA_WIDTH + POOL_WIDTH:, :], preferred_element_type=F32)
    z = alpha * x_ref[...] + mod_ref[:, 2 * d:3 * d] * y
    o_ref[...] = _layer_norm_rows(z) * lng_ref[...] + lnb_ref[...]


def _mixout_call(x, da, u, o_f, o_b, rg, mod3, w_out_bf, pool_bd, pool_scale, ln_g, ln_b, *, tiles_per_batch, seq,
                 alpha):
    r, d = x.shape
    t = ROW_TILE
    nt = r // t
    hb = t // POOL_HALO
    n_halo_blocks = r // POOL_HALO
    row = lambda i: (i, 0)
    const = lambda i: (0, 0)
    kern = functools.partial(_mixout_kernel, tiles_per_batch=tiles_per_batch, seq=seq, alpha=alpha)
    return pl.pallas_call(
        kern,
        grid=(nt,),
        in_specs=[
            pl.BlockSpec((t, d), row),
            pl.BlockSpec((t, DA_WIDTH), row),
            pl.BlockSpec((t, POOL_WIDTH), row),
            pl.BlockSpec((POOL_HALO, POOL_WIDTH), lambda i: (jnp.maximum(i * hb - 1, 0), 0)),
            pl.BlockSpec((POOL_HALO, POOL_WIDTH), lambda i: (jnp.minimum((i + 1) * hb, n_halo_blocks - 1), 0)),
            pl.BlockSpec((t, RET_WIDTH), row),
            pl.BlockSpec((t, RET_WIDTH), row),
            pl.BlockSpec((t, RET_WIDTH), row),
            pl.BlockSpec((None, 1, 6 * d), lambda i: (_mod_row(i, tiles_per_batch), 0, 0)),
            pl.BlockSpec((d, d), const),
            pl.BlockSpec((POOL_WIDTH, POOL_WIDTH), const),
            pl.BlockSpec((1, POOL_WIDTH), const),
            pl.BlockSpec((1, d), const),
            pl.BlockSpec((1, d), const),
        ],
        out_specs=pl.BlockSpec((t, d), row),
        out_shape=jax.ShapeDtypeStruct((r, d), F32),
        compiler_params=_cparams("arbitrary"),
        name="mixer_out",
    )(x, da, u, u, u, o_f, o_b, rg, mod3, w_out_bf, pool_bd, pool_scale, ln_g, ln_b)


def _router_kernel(x_ref, mod_ref, wrh_ref, wrl_ref, bias_ref, wsgu_ref, wsdn_ref,
                   tokp_ref, idx_ref, gate_ref, rank_ref, cnt_ref, fsh_ref, carry_ref):
    d = D_MODEL
    t = x_ref.shape[0]
    ne = N_EXPERTS
    neg = -jnp.inf

    @pl.when(pl.program_id(0) == 0)
    def _():
        carry_ref[...] = jnp.zeros_like(carry_ref)

    tok = _layer_norm_rows(x_ref[...]) * (1.0 + mod_ref[:, 4 * d:5 * d]) + mod_ref[:, 3 * d:4 * d]
    tok_hi = tok.astype(BF16)
    tok_lo = (tok - tok_hi.astype(F32)).astype(BF16)

    bits = pltpu.bitcast(tok_hi.astype(F32), jnp.uint32)
    tokp_ref[...] = (lax.shift_right_logical(bits[:, 0:PACK_W], jnp.uint32(16))
                     | (bits[:, PACK_W:] & jnp.uint32(0xFFFF0000)))

    hs = jnp.dot(tok_hi, wsgu_ref[...], preferred_element_type=F32)
    gs, us = hs[:, 0:EXPERT_HIDDEN], hs[:, EXPERT_HIDDEN:]
    fsh_ref[...] = jnp.dot((gs * _sigmoid(gs) * us).astype(BF16), wsdn_ref[...], preferred_element_type=F32)

    nt_dims = (((1,), (1,)), ((), ()))
    logits = (lax.dot_general(wrh_ref[...], tok_hi, nt_dims, preferred_element_type=F32)
              + lax.dot_general(wrh_ref[...], tok_lo, nt_dims, preferred_element_type=F32)
              + lax.dot_general(wrl_ref[...], tok_hi, nt_dims, preferred_element_type=F32))
    scores = _sigmoid(logits)
    biased = scores + bias_ref[...]

    gidx = lax.broadcasted_iota(jnp.int32, (GROUP_SIZE, t), 0)
    blocks, gscores = [], []
    for g in range(N_GROUPS):
        blk = biased[g * GROUP_SIZE:(g + 1) * GROUP_SIZE, :]
        m1 = jnp.max(blk, axis=0, keepdims=True)
        first = jnp.min(jnp.where(blk == m1, gidx, GROUP_SIZE), axis=0, keepdims=True)
        m2 = jnp.max(jnp.where(gidx == first, neg, blk), axis=0, keepdims=True)
        blocks.append(blk)
        gscores.append(m1 + m2)

    keep = [jnp.zeros((1, t), F32) for _ in range(N_GROUPS)]
    for _ in range(TOPK_GROUPS):
        m = gscores[0]
        for gs_ in gscores[1:]:
            m = jnp.maximum(m, gs_)
        found = jnp.zeros((1, t), F32)
        for g in range(N_GROUPS):
            hit = jnp.where(gscores[g] == m, 1.0 - found, 0.0)
            found = found + hit
            keep[g] = keep[g] + hit
            gscores[g] = jnp.where(hit > 0.0, neg, gscores[g])
    masked = jnp.concatenate([jnp.where(keep[g] > 0.0, blocks[g], neg) for g in range(N_GROUPS)], axis=0)

    ei = lax.broadcasted_iota(jnp.int32, (ne, t), 0)
    cur = masked
    onehot = jnp.zeros((ne, t), F32)
    idxs, gates = [], []
    for _ in range(TOP_K):
        m = jnp.max(cur, axis=0, keepdims=True)
        ii = jnp.min(jnp.where(cur == m, ei, ne), axis=0, keepdims=True)
        sel = ei == ii
        idxs.append(ii)
        gates.append(jnp.sum(jnp.where(sel, scores, 0.0), axis=0, keepdims=True))
        onehot = jnp.where(sel, 1.0, onehot)
        cur = jnp.where(sel, neg, cur)
    gsum = gates[0]
    for gk in gates[1:]:
        gsum = gsum + gk
    for k in range(TOP_K):
        idx_ref[k:k + 1, :] = idxs[k]
        gate_ref[k:k + 1, :] = gates[k] / gsum * ROUTED_SCALE

    ti = lax.broadcasted_iota(jnp.int32, (t, t), 0)
    tj = lax.broadcasted_iota(jnp.int32, (t, t), 1)
    before = jnp.where(ti < tj, 1.0, 0.0).astype(BF16)
    prefix = jnp.dot(onehot.astype(BF16), before, preferred_element_type=F32) + carry_ref[:, 0:1]
    for k in range(TOP_K):
        rank_k = jnp.sum(jnp.where(ei == idxs[k], prefix, 0.0), axis=0, keepdims=True)
        rank_ref[k:k + 1, :] = rank_k.astype(jnp.int32)
    carry_ref[...] = carry_ref[...] + jnp.sum(onehot, axis=1, keepdims=True)
    cnt_ref[...] = carry_ref[...].astype(jnp.int32)


def _router_call(x, mod3, wr_hi, wr_lo, bias_col, ws_gu_bf, ws_dn_bf, *, tiles_per_batch):
    r, d = x.shape
    t = ROW_TILE
    nt = r // t
    row = lambda i: (i, 0)
    col = lambda i: (0, i)
    const = lambda i: (0, 0)
    return pl.pallas_call(
        _router_kernel,
        grid=(nt,),
        in_specs=[
            pl.BlockSpec((t, d), row),
            pl.BlockSpec((None, 1, 6 * d), lambda i: (_mod_row(i, tiles_per_batch), 0, 0)),
            pl.BlockSpec((N_EXPERTS, d), const),
            pl.BlockSpec((N_EXPERTS, d), const),
            pl.BlockSpec((N_EXPERTS, 1), const),
            pl.BlockSpec((d, 2 * EXPERT_HIDDEN), const),
            pl.BlockSpec((EXPERT_HIDDEN, d), const),
        ],
        out_specs=[
            pl.BlockSpec((t, PACK_W), row),
            pl.BlockSpec((TOP_K, t), col),
            pl.BlockSpec((TOP_K, t), col),
            pl.BlockSpec((TOP_K, t), col),
            pl.BlockSpec((N_EXPERTS, LANES), const),
            pl.BlockSpec((t, d), row),
        ],
        out_shape=[
            jax.ShapeDtypeStruct((r, PACK_W), jnp.uint32),
            jax.ShapeDtypeStruct((TOP_K, r), jnp.int32),
            jax.ShapeDtypeStruct((TOP_K, r), F32),
            jax.ShapeDtypeStruct((TOP_K, r), jnp.int32),
            jax.ShapeDtypeStruct((N_EXPERTS, LANES), jnp.int32),
            jax.ShapeDtypeStruct((r, d), F32),
        ],
        scratch_shapes=[pltpu.VMEM((N_EXPERTS, LANES), F32)],
        compiler_params=_cparams("arbitrary"),
        name="router",
    )(x, mod3, wr_hi, wr_lo, bias_col, ws_gu_bf, ws_dn_bf)


def _dest_kernel(idx_ref, rank_ref, offs_ref, dest_ref):
    t = idx_ref.shape[1]
    ei = lax.broadcasted_iota(jnp.int32, (N_EXPERTS, t), 0)
    offs = offs_ref[...].astype(F32)
    for k in range(TOP_K):
        start = jnp.sum(jnp.where(ei == idx_ref[k:k + 1, :], offs, 0.0), axis=0, keepdims=True)
        dest_ref[k:k + 1, :] = start.astype(jnp.int32) + rank_ref[k:k + 1, :]


def _dest_call(idx, rank, offs_col):
    r = idx.shape[1]
    t = ROW_TILE
    col = lambda i: (0, i)
    return pl.pallas_call(
        _dest_kernel,
        grid=(r // t,),
        in_specs=[pl.BlockSpec((TOP_K, t), col), pl.BlockSpec((TOP_K, t), col),
                  pl.BlockSpec((N_EXPERTS, 1), lambda i: (0, 0))],
        out_specs=pl.BlockSpec((TOP_K, t), col),
        out_shape=jax.ShapeDtypeStruct((TOP_K, r), jnp.int32),
        compiler_params=_cparams("arbitrary"),
        name="moe_dest",
    )(idx, rank, offs_col)


def _dispatch_kernel(dest_ref, tokp_ref, xs_in_ref, xs_ref, sem):
    del xs_in_ref
    t = dest_ref.shape[1]

    def issue(tt, carry):
        for k in range(TOP_K):
            pltpu.make_async_copy(tokp_ref.at[pl.ds(tt, 1), :], xs_ref.at[pl.ds(dest_ref[k, tt], 1), :], sem).start()
        return carry

    lax.fori_loop(0, t, issue, 0)
    all_rows = xs_ref.at[pl.ds(0, TOP_K * t), :]
    pltpu.make_async_copy(all_rows, all_rows, sem).wait()


def _dispatch_call(dest, tokp, xs_zero):
    r = dest.shape[1]
    t = ROW_TILE
    return pl.pallas_call(
        _dispatch_kernel,
        grid=(r // t,),
        in_specs=[
            pl.BlockSpec((TOP_K, t), lambda i: (0, i), memory_space=pltpu.SMEM),
            pl.BlockSpec((t, PACK_W), lambda i: (i, 0)),
            pl.BlockSpec(memory_space=pl.ANY),
        ],
        out_specs=pl.BlockSpec(memory_space=pl.ANY),
        out_shape=jax.ShapeDtypeStruct(xs_zero.shape, xs_zero.dtype),
        scratch_shapes=[pltpu.SemaphoreType.DMA(())],
        input_output_aliases={2: 0},
        compiler_params=_cparams("arbitrary"),
        name="moe_dispatch",
    )(dest, tokp, xs_zero)


def _expert_kernel(be_ref, nb_ref, nxt_ref, xs_ref, wgu_hbm, wdn_hbm, ys_ref, wgu_f32, wdn_f32, wgu_bf, wdn_bf,
                   sems, *, layer):
    j = pl.program_id(0)

    def weight_copies(e):
        return (pltpu.make_async_copy(wgu_hbm.at[layer, e], wgu_f32, sems.at[0]),
                pltpu.make_async_copy(wdn_hbm.at[layer, e], wdn_f32, sems.at[1]))

    @pl.when(j < nb_ref[0])
    def _():
        changed = jnp.logical_or(j == 0, be_ref[j] != be_ref[jnp.maximum(j - 1, 0)])

        @pl.when(j == 0)
        def _():
            for cp in weight_copies(be_ref[0]):
                cp.start()

        @pl.when(changed)
        def _():
            for cp in weight_copies(be_ref[j]):
                cp.wait()
            wgu_bf[...] = wgu_f32[...].astype(BF16)
            wdn_bf[...] = wdn_f32[...].astype(BF16)

            @pl.when(nxt_ref[j] >= 0)
            def _():
                for cp in weight_copies(nxt_ref[j]):
                    cp.start()

        word = xs_ref[...]
        x_lo = pltpu.bitcast(lax.shift_left(word, jnp.uint32(16)), F32).astype(BF16)
        x_hi = pltpu.bitcast(word & jnp.uint32(0xFFFF0000), F32).astype(BF16)
        h = (jnp.dot(x_lo, wgu_bf[0:PACK_W, :], preferred_element_type=F32)
             + jnp.dot(x_hi, wgu_bf[PACK_W:, :], preferred_element_type=F32))
        g, u = h[:, 0:EXPERT_HIDDEN], h[:, EXPERT_HIDDEN:]
        ys_ref[...] = jnp.dot((g * _sigmoid(g) * u).astype(BF16), wdn_bf[...], preferred_element_type=F32)


def _expert_call(block_expert, n_blocks_used, next_expert, xs, w_gu, w_dn, layer):
    n_rows = xs.shape[0]
    bm = EXPERT_BLOCK
    d = D_MODEL
    grid_spec = pltpu.PrefetchScalarGridSpec(
        num_scalar_prefetch=3,
        grid=(n_rows // bm,),
        in_specs=[
            pl.BlockSpec((bm, PACK_W), lambda j, be, nb, nxt: (j, 0)),
            pl.BlockSpec(memory_space=pl.ANY),
            pl.BlockSpec(memory_space=pl.ANY),
        ],
        out_specs=pl.BlockSpec((bm, d), lambda j, be, nb, nxt: (j, 0)),
        scratch_shapes=[
            pltpu.VMEM((d, 2 * EXPERT_HIDDEN), F32),
            pltpu.VMEM((EXPERT_HIDDEN, d), F32),
            pltpu.VMEM((d, 2 * EXPERT_HIDDEN), BF16),
            pltpu.VMEM((EXPERT_HIDDEN, d), BF16),
            pltpu.SemaphoreType.DMA((2,)),
        ],
    )
    return pl.pallas_call(
        functools.partial(_expert_kernel, layer=layer),
        grid_spec=grid_spec,
        out_shape=jax.ShapeDtypeStruct((n_rows, d), F32),
        compiler_params=_cparams("arbitrary"),
        name="moe_experts",
    )(block_expert, n_blocks_used, next_expert, xs, w_gu, w_dn)


def _combine_kernel(dest_ref, ys_ref, x_ref, fsh_ref, gate_ref, mod_ref, lng_ref, lnb_ref, o_ref, buf, sem, *,
                    alpha):
    d = D_MODEL
    t = x_ref.shape[0]

    def issue(tt, carry):
        for k in range(TOP_K):
            pltpu.make_async_copy(ys_ref.at[pl.ds(dest_ref[k, tt], 1), :], buf.at[pl.ds(k * t + tt, 1), :],
                                  sem).start()
        return carry

    lax.fori_loop(0, t, issue, 0)
    pltpu.make_async_copy(ys_ref.at[pl.ds(0, TOP_K * t), :], buf, sem).wait()

    gate_rows = gate_ref[...]
    pad = jnp.zeros((LANES - TOP_K, t), F32)
    gate_cols = jnp.concatenate([gate_rows, pad], axis=0).T
    f = fsh_ref[...]
    for k in range(TOP_K):
        f = f + gate_cols[:, k:k + 1] * buf[pl.ds(k * t, t), :]
    z = alpha * x_ref[...] + mod_ref[:, 5 * d:6 * d] * f
    o_ref[...] = _layer_norm_rows(z) * lng_ref[...] + lnb_ref[...]


def _combine_call(dest, ys, x, fsh, gate, mod3, ln_g, ln_b, *, tiles_per_batch, alpha):
    r, d = x.shape
    t = ROW_TILE
    row = lambda i: (i, 0)
    const = lambda i: (0, 0)
    kern = functools.partial(_combine_kernel, alpha=alpha)
    return pl.pallas_call(
        kern,
        grid=(r // t,),
        in_specs=[
            pl.BlockSpec((TOP_K, t), lambda i: (0, i), memory_space=pltpu.SMEM),
            pl.BlockSpec(memory_space=pl.ANY),
            pl.BlockSpec((t, d), row),
            pl.BlockSpec((t, d), row),
            pl.BlockSpec((TOP_K, t), lambda i: (0, i)),
            pl.BlockSpec((None, 1, 6 * d), lambda i: (_mod_row(i, tiles_per_batch), 0, 0)),
            pl.BlockSpec((1, d), const),
            pl.BlockSpec((1, d), const),
        ],
        out_specs=pl.BlockSpec((t, d), row),
        out_shape=jax.ShapeDtypeStruct((r, d), F32),
        scratch_shapes=[pltpu.VMEM((TOP_K * t, d), F32), pltpu.SemaphoreType.DMA(())],
        compiler_params=_cparams("arbitrary"),
        name="moe_combine",
    )(dest, ys, x, fsh, gate, mod3, ln_g, ln_b)


def _rope_tables(seq):
    rows = seq // GRID_W
    row = jnp.repeat(jnp.arange(rows, dtype=F32), GRID_W)
    col = jnp.tile(jnp.arange(GRID_W, dtype=F32), rows)
    nf = DA_DIM // 4
    freqs = ROPE_BASE ** (-jnp.arange(nf, dtype=F32) / nf)
    cr, sr = jnp.cos(row[:, None] * freqs), jnp.sin(row[:, None] * freqs)
    cc, sc = jnp.cos(col[:, None] * freqs), jnp.sin(col[:, None] * freqs)
    c64 = jnp.concatenate([cr, cr, cc, cc], axis=1)
    s64 = jnp.concatenate([-sr, sr, -sc, sc], axis=1)
    c = jnp.concatenate([jnp.tile(c64, (1, 2)), jnp.ones((CTX_LEN, LANES), F32)], axis=0)
    s = jnp.concatenate([jnp.tile(s64, (1, 2)), jnp.zeros((CTX_LEN, LANES), F32)], axis=0)
    return c, s


def kernel(x, c, ctx, c_ctx, w_mod, b_mod, w_in, w_out, diff_lambda, pool_w, pool_scale, ret_log_decay, ln_g, ln_b,
           w_router, router_bias, w_expert_gate_up, w_expert_down, w_shared_gate_up, w_shared_down):
    batch, seq, d = x.shape
    depth = w_mod.shape[0]
    assert d == D_MODEL and ctx.shape[1] == CTX_LEN == ROW_TILE and batch == 2
    assert seq % ROW_TILE == 0 and seq % GRID_W == 0 and w_in.shape[-1] == IN_WIDTH
    rows_per_batch = seq + CTX_LEN
    tiles_per_batch = rows_per_batch // ROW_TILE
    r = batch * rows_per_batch
    alpha = (2.0 * depth) ** 0.25

    xa = jnp.concatenate([x, ctx], axis=1).reshape(r, d)
    cvec = jnp.zeros((8, d), F32).at[0:batch].set(c).at[batch].set(c_ctx)
    mod_all = _mod_call(cvec, w_mod, b_mod)
    rope_c, rope_s = _rope_tables(seq)

    n_sorted = r * TOP_K + N_EXPERTS * EXPERT_BLOCK
    n_blocks = n_sorted // EXPERT_BLOCK

    for l in range(depth):
        lambda_init = 0.8 - 0.6 * math.exp(-0.3 * l)
        mod3 = mod_all[l].reshape(8, 1, 6 * d)
        lng = ln_g[l].reshape(2, 1, d)
        lnb = ln_b[l].reshape(2, 1, d)

        w_in_bf = w_in[l].astype(BF16)
        w_vt_bf = w_in_bf[:, QK_WIDTH:QK_WIDTH + DA_WIDTH].T
        qk, vda, u, rqkv, rg = _inproj_call(xa, mod3, w_in_bf, w_vt_bf, rope_c, rope_s, tiles_per_batch)
        da = _attn_call(diff_lambda[l], qk, vda, batch=batch, rows_per_batch=rows_per_batch, seq=seq,
                        lambda_init=lambda_init)
        o_f, o_b = _ret_call(ret_log_decay[l], rqkv, batch=batch, rows_per_batch=rows_per_batch, seq=seq)
        pool_bd = jnp.zeros((POOL_WIDTH, POOL_WIDTH), F32)
        for gi in range(len(POOL_WINDOWS)):
            sl = slice(gi * POOL_GROUP, (gi + 1) * POOL_GROUP)
            pool_bd = pool_bd.at[sl, sl].set(pool_w[l, gi])
        xa = _mixout_call(xa, da, u, o_f, o_b, rg, mod3, w_out[l].astype(BF16), pool_bd.astype(BF16),
                          pool_scale[l].reshape(1, POOL_WIDTH), lng[0], lnb[0],
                          tiles_per_batch=tiles_per_batch, seq=seq, alpha=alpha)

        wr_t = w_router[l].T
        wr_hi = wr_t.astype(BF16)
        wr_lo = (wr_t - wr_hi.astype(F32)).astype(BF16)
        tokp, idx, gate, rank, cnt, fsh = _router_call(
            xa, mod3, wr_hi, wr_lo, router_bias[l].reshape(N_EXPERTS, 1),
            w_shared_gate_up[l].astype(BF16), w_shared_down[l].astype(BF16), tiles_per_batch=tiles_per_batch)
        counts = cnt[:, 0]
        padded = (counts + EXPERT_BLOCK - 1) // EXPERT_BLOCK * EXPERT_BLOCK
        pad_end = jnp.cumsum(padded)
        offs = pad_end - padded
        block_expert = jnp.minimum(
            jnp.searchsorted(pad_end, jnp.arange(n_blocks, dtype=jnp.int32) * EXPERT_BLOCK, side='right'),
            N_EXPERTS - 1).astype(jnp.int32)
        n_used = (pad_end[-1:] // EXPERT_BLOCK).astype(jnp.int32)
        blk = jnp.arange(n_blocks, dtype=jnp.int32)
        be_used = jnp.where(blk < n_used, block_expert, N_EXPERTS)
        nxt_blk = jnp.searchsorted(be_used, block_expert, side='right').astype(jnp.int32)
        next_expert = jnp.where(nxt_blk < n_used, block_expert[jnp.minimum(nxt_blk, n_blocks - 1)], -1)
        dest = _dest_call(idx, rank, offs.reshape(N_EXPERTS, 1).astype(jnp.int32))
        xs = _dispatch_call(dest, tokp, jnp.zeros((n_sorted, PACK_W), jnp.uint32))
        ys = _expert_call(block_expert, n_used, next_expert.astype(jnp.int32), xs, w_expert_gate_up,
                          w_expert_down, l)
        xa = _combine_call(dest, ys, xa, fsh, gate, mod3, lng[1], lnb[1],
                           tiles_per_batch=tiles_per_batch, alpha=alpha)

    return xa.reshape(batch, rows_per_batch, d)[:, :seq]
```

```python
import functools
import math

import jax
import jax.numpy as jnp
from jax import lax
from jax.experimental import pallas as pl
from jax.experimental.pallas import tpu as pltpu

F32 = jnp.float32
BF16 = jnp.bfloat16
HIGHEST = lax.Precision.HIGHEST

D_MODEL = 1024
CTX_LEN = 256
GRID_W = 64
DA_HEADS = 4
DA_DIM = 64
DA_VDIM = 2 * DA_DIM
DA_WIDTH = DA_HEADS * DA_VDIM
ROPE_BASE = 10000.0
POOL_WINDOWS = (2, 4, 8, 16)
POOL_GROUP = 64
POOL_WIDTH = len(POOL_WINDOWS) * POOL_GROUP
POOL_HALO = 8
RET_HEADS = 4
RET_DK = 64
RET_WIDTH = RET_HEADS * RET_DK
RET_CHUNK = 128
QK_WIDTH = 2 * DA_HEADS * 2 * DA_DIM
IN_WIDTH = QK_WIDTH + DA_WIDTH + POOL_WIDTH + 4 * RET_WIDTH
N_EXPERTS = 256
TOP_K = 8
N_GROUPS = 8
GROUP_SIZE = N_EXPERTS // N_GROUPS
TOPK_GROUPS = 4
EXPERT_HIDDEN = 256
ROUTED_SCALE = 2.5
LN_EPS = 1e-6
RMS_EPS = 1e-5

LANES = 128
ROW_TILE = 256
ATTN_Q_TILE = 256
ATTN_K_CHUNK = 256
ATTN_UNROLL = 16
EXPERT_BLOCK = 256
PACK_W = D_MODEL // 2
PACK_S = PACK_W // LANES
ROW_S = D_MODEL // LANES
VMEM_LIMIT = 56 * 1024 * 1024


def _cparams(*sem):
    return pltpu.CompilerParams(dimension_semantics=sem, vmem_limit_bytes=VMEM_LIMIT)


def _sigmoid(x):
    return 1.0 / (1.0 + jnp.exp(-x))


def _layer_norm_rows(x):
    mu = jnp.mean(x, axis=-1, keepdims=True)
    xc = x - mu
    var = jnp.mean(xc * xc, axis=-1, keepdims=True)
    return xc * lax.rsqrt(var + LN_EPS)


def _pack_bf16_pairs(x):
    half = x.shape[1] // 2
    bits = pltpu.bitcast(x.astype(BF16).astype(F32), jnp.uint32)
    return lax.shift_right_logical(bits[:, 0:half], jnp.uint32(16)) | (bits[:, half:] & jnp.uint32(0xFFFF0000))


def _unpack_bf16_pairs(word):
    lo = pltpu.bitcast(lax.shift_left(word, jnp.uint32(16)), F32)
    hi = pltpu.bitcast(word & jnp.uint32(0xFFFF0000), F32)
    return lo, hi


def _mod_row(i, tiles_per_batch):
    return jnp.where(i % tiles_per_batch == tiles_per_batch - 1, 2, i // tiles_per_batch)


def _mod_kernel(c_ref, w_ref, b_ref, o_ref):
    c = c_ref[...]
    s = c * _sigmoid(c)
    o_ref[...] = jnp.dot(s, w_ref[...], precision=HIGHEST, preferred_element_type=F32) + b_ref[...]


def _mod_call(cvec, w_mod, b_mod):
    depth, d, n = w_mod.shape
    tn = 1536
    return pl.pallas_call(
        _mod_kernel,
        grid=(depth, n // tn),
        in_specs=[
            pl.BlockSpec((8, d), lambda l, j: (0, 0)),
            pl.BlockSpec((None, d, tn), lambda l, j: (l, 0, j)),
            pl.BlockSpec((None, 1, tn), lambda l, j: (l, 0, j)),
        ],
        out_specs=pl.BlockSpec((None, 8, tn), lambda l, j: (l, 0, j)),
        out_shape=jax.ShapeDtypeStruct((depth, 8, n), F32),
        compiler_params=_cparams("arbitrary", "arbitrary"),
        name="mod",
    )(cvec, w_mod, b_mod.reshape(depth, 1, n))


def _inproj_kernel(x_ref, mod_ref, w_ref, wvt_ref, ct_ref, st_ref, qk_ref, vt_ref, u_ref, r_ref, g_ref):
    d = D_MODEL
    xn = _layer_norm_rows(x_ref[...])
    h = (xn * (1.0 + mod_ref[:, d:2 * d]) + mod_ref[:, 0:d]).astype(BF16)

    a = jnp.dot(h, w_ref[:, 0:QK_WIDTH], preferred_element_type=F32)
    lane = lax.broadcasted_iota(jnp.int32, (a.shape[0], LANES), 1)
    first_half = (lane % 32) < 16
    ct = ct_ref[...]
    st = st_ref[...]
    for s in range(QK_WIDTH // LANES):
        blk = a[:, s * LANES:(s + 1) * LANES]
        partner = jnp.where(first_half, pltpu.roll(blk, LANES - 16, 1), pltpu.roll(blk, 16, 1))
        rot = blk * ct + partner * st
        if s < QK_WIDTH // LANES // 2:
            rot = rot * (DA_DIM ** -0.5 * math.log2(math.e))
        qk_ref[:, s * LANES:(s + 1) * LANES] = rot.astype(BF16)

    vt_ref[...] = lax.dot_general(wvt_ref[...], h, (((1,), (1,)), ((), ())),
                                  preferred_element_type=F32).astype(BF16)
    o = QK_WIDTH + DA_WIDTH
    u_ref[...] = jnp.dot(h, w_ref[:, o:o + POOL_WIDTH], preferred_element_type=F32)
    o += POOL_WIDTH
    r = jnp.dot(h, w_ref[:, o:o + 3 * RET_WIDTH], preferred_element_type=F32)
    r_ref[:, 0:RET_WIDTH] = r[:, 0:RET_WIDTH].astype(BF16)
    r_ref[:, RET_WIDTH:2 * RET_WIDTH] = (r[:, RET_WIDTH:2 * RET_WIDTH] * (RET_DK ** -0.5)).astype(BF16)
    r_ref[:, 2 * RET_WIDTH:] = r[:, 2 * RET_WIDTH:].astype(BF16)
    o += 3 * RET_WIDTH
    g_ref[...] = jnp.dot(h, w_ref[:, o:o + RET_WIDTH], preferred_element_type=F32)


def _inproj_call(x, mod3, w_in_bf, w_vt_bf, rope_c, rope_s, tiles_per_batch):
    r, d = x.shape
    t = ROW_TILE
    nt = r // t
    row = lambda i: (i, 0)
    return pl.pallas_call(
        _inproj_kernel,
        grid=(nt,),
        in_specs=[
            pl.BlockSpec((t, d), row),
            pl.BlockSpec((None, 1, 6 * d), lambda i: (_mod_row(i, tiles_per_batch), 0, 0)),
            pl.BlockSpec((d, IN_WIDTH), lambda i: (0, 0)),
            pl.BlockSpec((DA_WIDTH, d), lambda i: (0, 0)),
            pl.BlockSpec((t, LANES), lambda i: (i % tiles_per_batch, 0)),
            pl.BlockSpec((t, LANES), lambda i: (i % tiles_per_batch, 0)),
        ],
        out_specs=[
            pl.BlockSpec((t, QK_WIDTH), row),
            pl.BlockSpec((DA_WIDTH, t), lambda i: (0, i)),
            pl.BlockSpec((t, POOL_WIDTH), row),
            pl.BlockSpec((t, 3 * RET_WIDTH), row),
            pl.BlockSpec((t, RET_WIDTH), row),
        ],
        out_shape=[
            jax.ShapeDtypeStruct((r, QK_WIDTH), BF16),
            jax.ShapeDtypeStruct((DA_WIDTH, r), BF16),
            jax.ShapeDtypeStruct((r, POOL_WIDTH), F32),
            jax.ShapeDtypeStruct((r, 3 * RET_WIDTH), BF16),
            jax.ShapeDtypeStruct((r, RET_WIDTH), F32),
        ],
        compiler_params=_cparams("arbitrary"),
        name="inproj",
    )(x, mod3, w_in_bf, w_vt_bf, rope_c, rope_s)


def _attn_kernel(lam_ref, q_ref, k_ref, vt_ref, o_ref, s_ref, *, k_chunk, seq, lambda_init):
    q = q_ref[...]
    mq = q.shape[0]
    lane = lax.broadcasted_iota(jnp.int32, q.shape, 1)
    zero = jnp.zeros_like(q)
    q2 = jnp.concatenate([jnp.where(lane < DA_DIM, q, zero), jnp.where(lane >= DA_DIM, q, zero)], axis=0)
    qt = q2.astype(F32).T.astype(BF16)

    n_chunks = (seq + CTX_LEN) // k_chunk
    last = n_chunks - 1
    is_ctx_tile = pl.program_id(2) == pl.num_programs(2) - 1
    n_iters = jnp.where(is_ctx_tile, 0, last // ATTN_UNROLL)

    def score_chunk(c, m):
        off = pl.multiple_of(c * k_chunk, k_chunk)
        s = jnp.dot(k_ref[pl.ds(off, k_chunk), :], qt, preferred_element_type=F32)
        s_ref[c] = s
        return jnp.maximum(m, jnp.max(s, axis=0, keepdims=True))

    def pass1(it, m):
        for u in range(ATTN_UNROLL):
            m = score_chunk(it * ATTN_UNROLL + u, m)
        return m

    m = lax.fori_loop(0, n_iters, pass1, jnp.full((1, 2 * mq), -jnp.inf, F32))
    m = score_chunk(last, m)

    ones_rows = jnp.where(lax.broadcasted_iota(jnp.int32, (16, k_chunk), 0) == 0, 1.0, 0.0).astype(BF16)

    def value_chunk(c, acc):
        off = pl.multiple_of(c * k_chunk, k_chunk)
        vt = jnp.concatenate([vt_ref[:, pl.ds(off, k_chunk)], ones_rows], axis=0)
        p = jnp.exp2((s_ref[c] - m).astype(BF16))
        return acc + jnp.dot(vt, p, preferred_element_type=F32)

    def pass2(it, acc):
        for u in range(ATTN_UNROLL):
            acc = value_chunk(it * ATTN_UNROLL + u, acc)
        return acc

    acc = lax.fori_loop(0, n_iters, pass2, jnp.zeros((DA_VDIM + 16, 2 * mq), F32))
    acc = value_chunk(last, acc)
    l0, l1 = acc[DA_VDIM:DA_VDIM + 1, 0:mq], acc[DA_VDIM:DA_VDIM + 1, mq:]
    a0, a1 = acc[0:DA_VDIM, 0:mq], acc[0:DA_VDIM, mq:]

    lv = lam_ref[...]
    lam = (jnp.exp(jnp.sum(lv[0:1] * lv[1:2], axis=-1, keepdims=True))
           - jnp.exp(jnp.sum(lv[2:3] * lv[3:4], axis=-1, keepdims=True)) + lambda_init)
    o = a0 / l0 - lam * (a1 / l1)
    o = o * lax.rsqrt(jnp.mean(o * o, axis=0, keepdims=True) + RMS_EPS) * (1.0 - lambda_init)
    o_ref[...] = o.T.astype(BF16)


def _attn_call(lam_vec, qk, vda, *, batch, rows_per_batch, seq, lambda_init):
    tq = ATTN_Q_TILE
    assert seq % (ATTN_K_CHUNK * ATTN_UNROLL) == 0 and rows_per_batch - seq == CTX_LEN == tq == ATTN_K_CHUNK
    nq = rows_per_batch // tq
    kern = functools.partial(_attn_kernel, k_chunk=ATTN_K_CHUNK, seq=seq, lambda_init=lambda_init)
    return pl.pallas_call(
        kern,
        grid=(batch, DA_HEADS, nq),
        in_specs=[
            pl.BlockSpec((4, DA_DIM), lambda b, h, i: (0, 0)),
            pl.BlockSpec((tq, DA_VDIM), lambda b, h, i: (b * nq + i, h)),
            pl.BlockSpec((rows_per_batch, DA_VDIM), lambda b, h, i: (b, DA_HEADS + h)),
            pl.BlockSpec((DA_VDIM, rows_per_batch), lambda b, h, i: (h, b)),
        ],
        out_specs=pl.BlockSpec((tq, DA_VDIM), lambda b, h, i: (b * nq + i, h)),
        out_shape=jax.ShapeDtypeStruct((qk.shape[0], DA_WIDTH), BF16),
        scratch_shapes=[pltpu.VMEM((rows_per_batch // ATTN_K_CHUNK, ATTN_K_CHUNK, 2 * tq), F32)],
        compiler_params=_cparams("arbitrary", "arbitrary", "arbitrary"),
        name="diff_attn",
    )(lam_vec, qk, qk, vda)


def _ret_kernel(ld_ref, f_ref, b_ref, of_ref, ob_ref, dm_ref, qd_ref, kd_ref, cd_ref, st_ref):
    c = pl.program_id(1)
    ch = RET_CHUNK
    w = RET_WIDTH
    lane_head = lax.broadcasted_iota(jnp.int32, (1, w), 1) // RET_DK

    @pl.when(c == 0)
    def _():
        st_ref[...] = jnp.zeros_like(st_ref)
        ri = lax.broadcasted_iota(jnp.int32, (ch, ch), 0)
        ci = lax.broadcasted_iota(jnp.int32, (ch, ch), 1)
        rowf = lax.broadcasted_iota(jnp.int32, (ch, w), 0).astype(F32)
        for d in range(2):
            lg_lane = jnp.zeros((1, w), F32)
            for hh in range(RET_HEADS):
                lg = -jnp.exp(jnp.full((1, 1), ld_ref[d, hh], F32))
                lg_lane = jnp.where(lane_head == hh, lg, lg_lane)
                dist = ((ri - ci) if d == 0 else (ci - ri)).astype(F32)
                dm_ref[d, hh] = jnp.where(dist >= 0, jnp.exp(dist * lg), 0.0)
            if d == 0:
                qd_ref[d] = jnp.exp((rowf + 1.0) * lg_lane)
                kd_ref[d] = jnp.exp((ch - 1.0 - rowf) * lg_lane)
            else:
                qd_ref[d] = jnp.exp((ch - rowf) * lg_lane)
                kd_ref[d] = jnp.exp(rowf * lg_lane)
            cd_ref[d] = jnp.exp(float(ch) * lg_lane)

    rblk = lax.broadcasted_iota(jnp.int32, (w, w), 0) // RET_DK
    cblk = lax.broadcasted_iota(jnp.int32, (w, w), 1) // RET_DK
    for d, (src, dst) in enumerate(((f_ref, of_ref), (b_ref, ob_ref))):
        q = src[:, 0:w]
        k = src[:, w:2 * w]
        v = src[:, 2 * w:3 * w]
        st = st_ref[d]
        o = jnp.dot((q.astype(F32) * qd_ref[d]).astype(BF16), st.astype(BF16), preferred_element_type=F32)
        for hh in range(RET_HEADS):
            in_head = lane_head == hh
            qm = jnp.where(in_head, q, jnp.zeros_like(q))
            s = lax.dot_general(qm, k, (((1,), (1,)), ((), ())), preferred_element_type=F32)
            intra = (s * dm_ref[d, hh]).astype(BF16)
            o = o + jnp.where(in_head, jnp.dot(intra, v, preferred_element_type=F32), 0.0)
        dst[...] = o
        kk_t = (k.astype(F32) * kd_ref[d]).T.astype(BF16)
        upd = jnp.dot(kk_t, v, preferred_element_type=F32)
        st_ref[d] = jnp.where(rblk == cblk, st * cd_ref[d] + upd, 0.0)


def _ret_call(log_decay, rqkv, *, batch, rows_per_batch, seq):
    ch = RET_CHUNK
    nc = rows_per_batch // ch
    n_lat = seq // ch
    n_ctx = nc - n_lat

    def fwd(b, c):
        return (b * nc + jnp.where(c < n_ctx, n_lat + c, c - n_ctx), 0)

    def bwd(b, c):
        return (b * nc + nc - 1 - c, 0)

    w = RET_WIDTH
    return pl.pallas_call(
        _ret_kernel,
        grid=(batch, nc),
        in_specs=[
            pl.BlockSpec(memory_space=pltpu.SMEM),
            pl.BlockSpec((ch, 3 * w), fwd),
            pl.BlockSpec((ch, 3 * w), bwd),
        ],
        out_specs=[pl.BlockSpec((ch, w), fwd), pl.BlockSpec((ch, w), bwd)],
        out_shape=[jax.ShapeDtypeStruct((rqkv.shape[0], w), F32)] * 2,
        scratch_shapes=[
            pltpu.VMEM((2, RET_HEADS, ch, ch), F32),
            pltpu.VMEM((2, ch, w), F32),
            pltpu.VMEM((2, ch, w), F32),
            pltpu.VMEM((2, 1, w), F32),
            pltpu.VMEM((2, w, w), F32),
        ],
        compiler_params=_cparams("arbitrary", "arbitrary"),
        name="retention",
    )(log_decay, rqkv, rqkv)


def _mixout_kernel(x_ref, da_ref, u_ref, up_ref, un_ref, of_ref, ob_ref, rg_ref, mod_ref, wo_ref, pw_ref,
                   ps_ref, lng_ref, lnb_ref, o_ref, *, tiles_per_batch, seq, alpha):
    d = D_MODEL
    t = x_ref.shape[0]
    i = pl.program_id(0)
    j = i % tiles_per_batch
    is_ctx = j == tiles_per_batch - 1
    stream_len = jnp.where(is_ctx, CTX_LEN, seq)
    p0 = jnp.where(is_ctx, 0, j * t)

    u = u_ref[...]
    prev = jnp.where(p0 > 0, up_ref[...], 0.0)
    nxt = jnp.where(p0 + t < stream_len, un_ref[...], 0.0)
    ext = jnp.concatenate([prev, u, nxt], axis=0)
    n = t + 2 * POOL_HALO
    a2 = ext + pltpu.roll(ext, 1, 0)
    a4 = pltpu.roll(a2, 1, 0) + pltpu.roll(a2, n - 1, 0)
    a8 = pltpu.roll(a4, 2, 0) + pltpu.roll(a4, n - 2, 0)
    a16 = pltpu.roll(a8, 4, 0) + pltpu.roll(a8, n - 4, 0)
    pos = p0 + lax.broadcasted_iota(jnp.int32, (t, POOL_WIDTH), 0)
    group = lax.broadcasted_iota(jnp.int32, (1, POOL_WIDTH), 1) // POOL_GROUP
    mean = jnp.zeros((t, POOL_WIDTH), F32)
    for gi, (wnd, asum) in enumerate(zip(POOL_WINDOWS, (a2, a4, a8, a16))):
        cnt = jnp.minimum(pos + wnd // 2, stream_len) - jnp.maximum(pos - wnd // 2, 0)
        mean = jnp.where(group == gi, asum[POOL_HALO:POOL_HALO + t] / cnt.astype(F32), mean)
    pool = jnp.dot((mean - u).astype(BF16), pw_ref[...], preferred_element_type=F32) * ps_ref[...]

    o = of_ref[...] + ob_ref[...]
    head = lax.broadcasted_iota(jnp.int32, (1, RET_WIDTH), 1) // RET_DK

    def head_mean(val):
        out = jnp.zeros_like(val)
        for hh in range(RET_HEADS):
            m = jnp.sum(jnp.where(head == hh, val, 0.0), axis=-1, keepdims=True) * (1.0 / RET_DK)
            out = jnp.where(head == hh, m, out)
        return out

    oc = o - head_mean(o)
    rn = oc * lax.rsqrt(head_mean(oc * oc) + LN_EPS)
    g = rg_ref[...]
    ret = rn * (g * _sigmoid(g))

    y = jnp.dot(da_ref[...], wo_ref[0:DA_WIDTH, :], preferred_element_type=F32)
    y = y + jnp.dot(pool.astype(BF16), wo_ref[DA_WIDTH:DA_WIDTH + POOL_WIDTH, :], preferred_element_type=F32)
    y = y + jnp.dot(ret.astype(BF16), wo_ref[DA_WIDTH + POOL_WIDTH:, :], preferred_element_type=F32)
    z = alpha * x_ref[...] + mod_ref[:, 2 * d:3 * d] * y
    o_ref[...] = _layer_norm_rows(z) * lng_ref[...] + lnb_ref[...]


def _mixout_call(x, da, u, o_f, o_b, rg, mod3, w_out_bf, pool_bd, pool_scale, ln_g, ln_b, *, tiles_per_batch, seq,
                 alpha):
    r, d = x.shape
    t = ROW_TILE
    nt = r // t
    hb = t // POOL_HALO
    n_halo_blocks = r // POOL_HALO
    row = lambda i: (i, 0)
    const = lambda i: (0, 0)
    kern = functools.partial(_mixout_kernel, tiles_per_batch=tiles_per_batch, seq=seq, alpha=alpha)
    return pl.pallas_call(
        kern,
        grid=(nt,),
        in_specs=[
            pl.BlockSpec((t, d), row),
            pl.BlockSpec((t, DA_WIDTH), row),
            pl.BlockSpec((t, POOL_WIDTH), row),
            pl.BlockSpec((POOL_HALO, POOL_WIDTH), lambda i: (jnp.maximum(i * hb - 1, 0), 0)),
            pl.BlockSpec((POOL_HALO, POOL_WIDTH), lambda i: (jnp.minimum((i + 1) * hb, n_halo_blocks - 1), 0)),
            pl.BlockSpec((t, RET_WIDTH), row),
            pl.BlockSpec((t, RET_WIDTH), row),
            pl.BlockSpec((t, RET_WIDTH), row),
            pl.BlockSpec((None, 1, 6 * d), lambda i: (_mod_row(i, tiles_per_batch), 0, 0)),
            pl.BlockSpec((d, d), const),
            pl.BlockSpec((POOL_WIDTH, POOL_WIDTH), const),
            pl.BlockSpec((1, POOL_WIDTH), const),
            pl.BlockSpec((1, d), const),
            pl.BlockSpec((1, d), const),
        ],
        out_specs=pl.BlockSpec((t, d), row),
        out_shape=jax.ShapeDtypeStruct((r, d), F32),
        compiler_params=_cparams("arbitrary"),
        name="mixer_out",
    )(x, da, u, u, u, o_f, o_b, rg, mod3, w_out_bf, pool_bd, pool_scale, ln_g, ln_b)


def _router_kernel(x_ref, mod_ref, wrh_ref, wrl_ref, bias_ref, wsgu_ref, wsdn_ref,
                   tokp_ref, idx_ref, gate_ref, rank_ref, cnt_ref, fsh_ref, carry_ref):
    d = D_MODEL
    t = x_ref.shape[0]
    ne = N_EXPERTS
    neg = -jnp.inf

    @pl.when(pl.program_id(0) == 0)
    def _():
        carry_ref[...] = jnp.zeros_like(carry_ref)

    tok = _layer_norm_rows(x_ref[...]) * (1.0 + mod_ref[:, 4 * d:5 * d]) + mod_ref[:, 3 * d:4 * d]
    tok_hi = tok.astype(BF16)
    tok_lo = (tok - tok_hi.astype(F32)).astype(BF16)

    tokp_ref[...] = _pack_bf16_pairs(tok)

    hs = jnp.dot(tok_hi, wsgu_ref[...], preferred_element_type=F32)
    gs, us = hs[:, 0:EXPERT_HIDDEN], hs[:, EXPERT_HIDDEN:]
    fsh_ref[...] = jnp.dot((gs * _sigmoid(gs) * us).astype(BF16), wsdn_ref[...], preferred_element_type=F32)

    nt_dims = (((1,), (1,)), ((), ()))
    logits = (lax.dot_general(wrh_ref[...], tok_hi, nt_dims, preferred_element_type=F32)
              + lax.dot_general(wrh_ref[...], tok_lo, nt_dims, preferred_element_type=F32)
              + lax.dot_general(wrl_ref[...], tok_hi, nt_dims, preferred_element_type=F32))
    scores = _sigmoid(logits)
    biased = scores + bias_ref[...]

    gidx = lax.broadcasted_iota(jnp.int32, (GROUP_SIZE, t), 0)
    blocks, gscores = [], []
    for g in range(N_GROUPS):
        blk = biased[g * GROUP_SIZE:(g + 1) * GROUP_SIZE, :]
        m1 = jnp.max(blk, axis=0, keepdims=True)
        first = jnp.min(jnp.where(blk == m1, gidx, GROUP_SIZE), axis=0, keepdims=True)
        m2 = jnp.max(jnp.where(gidx == first, neg, blk), axis=0, keepdims=True)
        blocks.append(blk)
        gscores.append(m1 + m2)

    keep = [jnp.zeros((1, t), F32) for _ in range(N_GROUPS)]
    for _ in range(TOPK_GROUPS):
        m = gscores[0]
        for gs_ in gscores[1:]:
            m = jnp.maximum(m, gs_)
        found = jnp.zeros((1, t), F32)
        for g in range(N_GROUPS):
            hit = jnp.where(gscores[g] == m, 1.0 - found, 0.0)
            found = found + hit
            keep[g] = keep[g] + hit
            gscores[g] = jnp.where(hit > 0.0, neg, gscores[g])
    masked = jnp.concatenate([jnp.where(keep[g] > 0.0, blocks[g], neg) for g in range(N_GROUPS)], axis=0)

    ei = lax.broadcasted_iota(jnp.int32, (ne, t), 0)
    cur = masked
    onehot = jnp.zeros((ne, t), F32)
    idxs, gates = [], []
    for _ in range(TOP_K):
        m = jnp.max(cur, axis=0, keepdims=True)
        ii = jnp.min(jnp.where(cur == m, ei, ne), axis=0, keepdims=True)
        sel = ei == ii
        idxs.append(ii)
        gates.append(jnp.sum(jnp.where(sel, scores, 0.0), axis=0, keepdims=True))
        onehot = jnp.where(sel, 1.0, onehot)
        cur = jnp.where(sel, neg, cur)
    gsum = gates[0]
    for gk in gates[1:]:
        gsum = gsum + gk
    for k in range(TOP_K):
        idx_ref[k:k + 1, :] = idxs[k]
        gate_ref[k:k + 1, :] = gates[k] / gsum * ROUTED_SCALE

    ti = lax.broadcasted_iota(jnp.int32, (t, t), 0)
    tj = lax.broadcasted_iota(jnp.int32, (t, t), 1)
    before = jnp.where(ti < tj, 1.0, 0.0).astype(BF16)
    prefix = jnp.dot(onehot.astype(BF16), before, preferred_element_type=F32) + carry_ref[:, 0:1]
    for k in range(TOP_K):
        rank_k = jnp.sum(jnp.where(ei == idxs[k], prefix, 0.0), axis=0, keepdims=True)
        rank_ref[k:k + 1, :] = rank_k.astype(jnp.int32)
    carry_ref[...] = carry_ref[...] + jnp.sum(onehot, axis=1, keepdims=True)
    cnt_ref[...] = carry_ref[...].astype(jnp.int32)


def _router_call(x, mod3, wr_hi, wr_lo, bias_col, ws_gu_bf, ws_dn_bf, *, tiles_per_batch):
    r, d = x.shape
    t = ROW_TILE
    nt = r // t
    row = lambda i: (i, 0)
    col = lambda i: (0, i)
    const = lambda i: (0, 0)
    return pl.pallas_call(
        _router_kernel,
        grid=(nt,),
        in_specs=[
            pl.BlockSpec((t, d), row),
            pl.BlockSpec((None, 1, 6 * d), lambda i: (_mod_row(i, tiles_per_batch), 0, 0)),
            pl.BlockSpec((N_EXPERTS, d), const),
            pl.BlockSpec((N_EXPERTS, d), const),
            pl.BlockSpec((N_EXPERTS, 1), const),
            pl.BlockSpec((d, 2 * EXPERT_HIDDEN), const),
            pl.BlockSpec((EXPERT_HIDDEN, d), const),
        ],
        out_specs=[
            pl.BlockSpec((t, PACK_W), row),
            pl.BlockSpec((TOP_K, t), col),
            pl.BlockSpec((TOP_K, t), col),
            pl.BlockSpec((TOP_K, t), col),
            pl.BlockSpec((N_EXPERTS, LANES), const),
            pl.BlockSpec((t, d), row),
        ],
        out_shape=[
            jax.ShapeDtypeStruct((r, PACK_W), jnp.uint32),
            jax.ShapeDtypeStruct((TOP_K, r), jnp.int32),
            jax.ShapeDtypeStruct((TOP_K, r), F32),
            jax.ShapeDtypeStruct((TOP_K, r), jnp.int32),
            jax.ShapeDtypeStruct((N_EXPERTS, LANES), jnp.int32),
            jax.ShapeDtypeStruct((r, d), F32),
        ],
        scratch_shapes=[pltpu.VMEM((N_EXPERTS, LANES), F32)],
        compiler_params=_cparams("arbitrary"),
        name="router",
    )(x, mod3, wr_hi, wr_lo, bias_col, ws_gu_bf, ws_dn_bf)


def _dest_kernel(idx_ref, rank_ref, offs_ref, dest_ref):
    t = idx_ref.shape[1]
    ei = lax.broadcasted_iota(jnp.int32, (N_EXPERTS, t), 0)
    offs = offs_ref[...].astype(F32)
    for k in range(TOP_K):
        start = jnp.sum(jnp.where(ei == idx_ref[k:k + 1, :], offs, 0.0), axis=0, keepdims=True)
        dest_ref[k:k + 1, :] = start.astype(jnp.int32) + rank_ref[k:k + 1, :]


def _dest_call(idx, rank, offs_col):
    r = idx.shape[1]
    t = ROW_TILE
    col = lambda i: (0, i)
    return pl.pallas_call(
        _dest_kernel,
        grid=(r // t,),
        in_specs=[pl.BlockSpec((TOP_K, t), col), pl.BlockSpec((TOP_K, t), col),
                  pl.BlockSpec((N_EXPERTS, 1), lambda i: (0, 0))],
        out_specs=pl.BlockSpec((TOP_K, t), col),
        out_shape=jax.ShapeDtypeStruct((TOP_K, r), jnp.int32),
        compiler_params=_cparams("arbitrary"),
        name="moe_dest",
    )(idx, rank, offs_col)


def _dispatch_kernel(last_row_ref, n_used_ref, dest_ref, tokp_ref, xs_ref, zeros_ref, sem, zero_sem):
    t = dest_ref.shape[1]

    @pl.when(pl.program_id(0) == 0)
    def _():
        zeros_ref[...] = jnp.zeros_like(zeros_ref)

        def zero_copy(o):
            row0 = pl.multiple_of(last_row_ref[o], EXPERT_BLOCK)
            return pltpu.make_async_copy(zeros_ref, xs_ref.at[pl.ds(row0, EXPERT_BLOCK), :], zero_sem)

        def start(o, carry):
            zero_copy(o).start()
            return carry

        def wait(o, carry):
            zero_copy(o).wait()
            return carry

        lax.fori_loop(0, n_used_ref[0], start, 0)
        lax.fori_loop(0, n_used_ref[0], wait, 0)

    def issue(tt, carry):
        for k in range(TOP_K):
            pltpu.make_async_copy(tokp_ref.at[pl.ds(tt, 1), :], xs_ref.at[pl.ds(dest_ref[k, tt], 1), :],
                                  sem).start(priority=k % 2)
        return carry

    lax.fori_loop(0, t, issue, 0, unroll=2)
    all_rows = xs_ref.at[pl.ds(0, TOP_K * t), :]
    pltpu.make_async_copy(all_rows, all_rows, sem).wait()


def _dispatch_call(used_last_row, n_used_experts, dest, tokp, n_sorted):
    r = dest.shape[1]
    t = ROW_TILE
    grid_spec = pltpu.PrefetchScalarGridSpec(
        num_scalar_prefetch=2,
        grid=(r // t,),
        in_specs=[
            pl.BlockSpec((TOP_K, t), lambda i, lr, nu: (0, i), memory_space=pltpu.SMEM),
            pl.BlockSpec((t, PACK_W), lambda i, lr, nu: (i, 0)),
        ],
        out_specs=pl.BlockSpec(memory_space=pl.ANY),
        scratch_shapes=[pltpu.VMEM((EXPERT_BLOCK, PACK_W), jnp.uint32), pltpu.SemaphoreType.DMA(()),
                        pltpu.SemaphoreType.DMA(())],
    )
    return pl.pallas_call(
        _dispatch_kernel,
        grid_spec=grid_spec,
        out_shape=jax.ShapeDtypeStruct((n_sorted, PACK_W), jnp.uint32),
        compiler_params=_cparams("arbitrary"),
        name="moe_dispatch",
    )(used_last_row, n_used_experts, dest, tokp)


def _expert_kernel(be_ref, nb_ref, ord_ref, ue_ref, nue_ref, xs_ref, wgu_hbm, wdn_hbm, ys_ref, wgu_f32, wdn_f32,
                   wgu_bf, wdn_bf, sems, *, layer):
    j = pl.program_id(0)

    def weight_copies(o):
        slot = o % 2
        e = ue_ref[o]
        return (pltpu.make_async_copy(wgu_hbm.at[layer, e], wgu_f32.at[slot], sems.at[0, slot]),
                pltpu.make_async_copy(wdn_hbm.at[layer, e], wdn_f32.at[slot], sems.at[1, slot]))

    def start_weights(o):
        @pl.when(o < nue_ref[0])
        def _():
            for cp in weight_copies(o):
                cp.start()

    @pl.when(j < nb_ref[0])
    def _():
        o = ord_ref[j]
        changed = jnp.logical_or(j == 0, be_ref[j] != be_ref[jnp.maximum(j - 1, 0)])

        @pl.when(j == 0)
        def _():
            start_weights(0)
            start_weights(1)

        @pl.when(changed)
        def _():
            for cp in weight_copies(o):
                cp.wait()
            slot = o % 2
            wgu_bf[...] = wgu_f32[slot].astype(BF16)
            wdn_bf[...] = wdn_f32[slot].astype(BF16)
            start_weights(o + 2)

        x_lo, x_hi = _unpack_bf16_pairs(xs_ref[...])
        h = (jnp.dot(x_lo.astype(BF16), wgu_bf[0:PACK_W, :], preferred_element_type=F32)
             + jnp.dot(x_hi.astype(BF16), wgu_bf[PACK_W:, :], preferred_element_type=F32))
        g, u = h[:, 0:EXPERT_HIDDEN], h[:, EXPERT_HIDDEN:]
        y = jnp.dot((g * _sigmoid(g) * u).astype(BF16), wdn_bf[...], preferred_element_type=F32)
        ys_ref[...] = _pack_bf16_pairs(y)


def _expert_call(block_expert, n_blocks_used, block_ordinal, used_expert, n_used_experts, xs, w_gu, w_dn, layer):
    n_rows = xs.shape[0]
    bm = EXPERT_BLOCK
    d = D_MODEL
    used_block = lambda j, be, nb, od, ue, nue: (jnp.minimum(j, nb[0] - 1), 0)
    grid_spec = pltpu.PrefetchScalarGridSpec(
        num_scalar_prefetch=5,
        grid=(n_rows // bm,),
        in_specs=[
            pl.BlockSpec((bm, PACK_W), used_block),
            pl.BlockSpec(memory_space=pl.ANY),
            pl.BlockSpec(memory_space=pl.ANY),
        ],
        out_specs=pl.BlockSpec((bm, PACK_W), used_block),
        scratch_shapes=[
            pltpu.VMEM((2, d, 2 * EXPERT_HIDDEN), F32),
            pltpu.VMEM((2, EXPERT_HIDDEN, d), F32),
            pltpu.VMEM((d, 2 * EXPERT_HIDDEN), BF16),
            pltpu.VMEM((EXPERT_HIDDEN, d), BF16),
            pltpu.SemaphoreType.DMA((2, 2)),
        ],
    )
    return pl.pallas_call(
        functools.partial(_expert_kernel, layer=layer),
        grid_spec=grid_spec,
        out_shape=jax.ShapeDtypeStruct((n_rows, PACK_W), jnp.uint32),
        compiler_params=_cparams("arbitrary"),
        name="moe_experts",
    )(block_expert, n_blocks_used, block_ordinal, used_expert, n_used_experts, xs, w_gu, w_dn)


def _combine_kernel(dest_ref, ys_ref, x_ref, fsh_ref, gate_ref, mod_ref, lng_ref, lnb_ref, o_ref, buf, sem, *,
                    alpha):
    d = D_MODEL
    t = x_ref.shape[0]

    def issue(tt, carry):
        for k in range(TOP_K):
            pltpu.make_async_copy(ys_ref.at[pl.ds(dest_ref[k, tt], 1), :], buf.at[pl.ds(k * t + tt, 1), :],
                                  sem).start(priority=k % 2)
        return carry

    lax.fori_loop(0, t, issue, 0, unroll=2)
    pltpu.make_async_copy(ys_ref.at[pl.ds(0, TOP_K * t), :], buf, sem).wait()

    gate_rows = gate_ref[...]
    pad = jnp.zeros((LANES - TOP_K, t), F32)
    gate_cols = jnp.concatenate([gate_rows, pad], axis=0).T
    f_lo = fsh_ref[:, 0:PACK_W]
    f_hi = fsh_ref[:, PACK_W:]
    for k in range(TOP_K):
        y_lo, y_hi = _unpack_bf16_pairs(buf[pl.ds(k * t, t), :])
        f_lo = f_lo + gate_cols[:, k:k + 1] * y_lo
        f_hi = f_hi + gate_cols[:, k:k + 1] * y_hi
    f = jnp.concatenate([f_lo, f_hi], axis=1)
    z = alpha * x_ref[...] + mod_ref[:, 5 * d:6 * d] * f
    o_ref[...] = _layer_norm_rows(z) * lng_ref[...] + lnb_ref[...]


def _combine_call(dest, ys, x, fsh, gate, mod3, ln_g, ln_b, *, tiles_per_batch, alpha, drop_context):
    r, d = x.shape
    t = ROW_TILE
    if drop_context:
        per_batch = tiles_per_batch - 1
        src = lambda i: (i // per_batch) * tiles_per_batch + i % per_batch
        n_tiles = (r // t) // tiles_per_batch * per_batch
    else:
        src = lambda i: i
        n_tiles = r // t
    row = lambda i: (src(i), 0)
    col = lambda i: (0, src(i))
    const = lambda i: (0, 0)
    kern = functools.partial(_combine_kernel, alpha=alpha)
    return pl.pallas_call(
        kern,
        grid=(n_tiles,),
        in_specs=[
            pl.BlockSpec((TOP_K, t), col, memory_space=pltpu.SMEM),
            pl.BlockSpec(memory_space=pl.ANY),
            pl.BlockSpec((t, d), row),
            pl.BlockSpec((t, d), row),
            pl.BlockSpec((TOP_K, t), col),
            pl.BlockSpec((None, 1, 6 * d), lambda i: (_mod_row(src(i), tiles_per_batch), 0, 0)),
            pl.BlockSpec((1, d), const),
            pl.BlockSpec((1, d), const),
        ],
        out_specs=pl.BlockSpec((t, d), lambda i: (i, 0)),
        out_shape=jax.ShapeDtypeStruct((n_tiles * t, d), F32),
        scratch_shapes=[pltpu.VMEM((TOP_K * t, PACK_W), jnp.uint32), pltpu.SemaphoreType.DMA(())],
        compiler_params=_cparams("arbitrary"),
        name="moe_combine",
    )(dest, ys, x, fsh, gate, mod3, ln_g, ln_b)


def _rope_tables(seq):
    rows = seq // GRID_W
    row = jnp.repeat(jnp.arange(rows, dtype=F32), GRID_W)
    col = jnp.tile(jnp.arange(GRID_W, dtype=F32), rows)
    nf = DA_DIM // 4
    freqs = ROPE_BASE ** (-jnp.arange(nf, dtype=F32) / nf)
    cr, sr = jnp.cos(row[:, None] * freqs), jnp.sin(row[:, None] * freqs)
    cc, sc = jnp.cos(col[:, None] * freqs), jnp.sin(col[:, None] * freqs)
    c64 = jnp.concatenate([cr, cr, cc, cc], axis=1)
    s64 = jnp.concatenate([-sr, sr, -sc, sc], axis=1)
    c = jnp.concatenate([jnp.tile(c64, (1, 2)), jnp.ones((CTX_LEN, LANES), F32)], axis=0)
    s = jnp.concatenate([jnp.tile(s64, (1, 2)), jnp.zeros((CTX_LEN, LANES), F32)], axis=0)
    return c, s


def kernel(x, c, ctx, c_ctx, w_mod, b_mod, w_in, w_out, diff_lambda, pool_w, pool_scale, ret_log_decay, ln_g, ln_b,
           w_router, router_bias, w_expert_gate_up, w_expert_down, w_shared_gate_up, w_shared_down):
    batch, seq, d = x.shape
    depth = w_mod.shape[0]
    assert d == D_MODEL and ctx.shape[1] == CTX_LEN == ROW_TILE and batch == 2
    assert seq % ROW_TILE == 0 and seq % GRID_W == 0 and w_in.shape[-1] == IN_WIDTH
    rows_per_batch = seq + CTX_LEN
    tiles_per_batch = rows_per_batch // ROW_TILE
    r = batch * rows_per_batch
    alpha = (2.0 * depth) ** 0.25

    xa = jnp.concatenate([x, ctx], axis=1).reshape(r, d)
    cvec = jnp.zeros((8, d), F32).at[0:batch].set(c).at[batch].set(c_ctx)
    mod_all = _mod_call(cvec, w_mod, b_mod)
    rope_c, rope_s = _rope_tables(seq)

    n_sorted = r * TOP_K + N_EXPERTS * EXPERT_BLOCK
    n_blocks = n_sorted // EXPERT_BLOCK

    for l in range(depth):
        lambda_init = 0.8 - 0.6 * math.exp(-0.3 * l)
        mod3 = mod_all[l].reshape(8, 1, 6 * d)
        lng = ln_g[l].reshape(2, 1, d)
        lnb = ln_b[l].reshape(2, 1, d)

        w_in_bf = w_in[l].astype(BF16)
        w_vt_bf = w_in_bf[:, QK_WIDTH:QK_WIDTH + DA_WIDTH].T
        qk, vda, u, rqkv, rg = _inproj_call(xa, mod3, w_in_bf, w_vt_bf, rope_c, rope_s, tiles_per_batch)
        da = _attn_call(diff_lambda[l], qk, vda, batch=batch, rows_per_batch=rows_per_batch, seq=seq,
                        lambda_init=lambda_init)
        o_f, o_b = _ret_call(ret_log_decay[l], rqkv, batch=batch, rows_per_batch=rows_per_batch, seq=seq)
        pool_bd = jnp.zeros((POOL_WIDTH, POOL_WIDTH), F32)
        for gi in range(len(POOL_WINDOWS)):
            sl = slice(gi * POOL_GROUP, (gi + 1) * POOL_GROUP)
            pool_bd = pool_bd.at[sl, sl].set(pool_w[l, gi])
        xa = _mixout_call(xa, da, u, o_f, o_b, rg, mod3, w_out[l].astype(BF16), pool_bd.astype(BF16),
                          pool_scale[l].reshape(1, POOL_WIDTH), lng[0], lnb[0],
                          tiles_per_batch=tiles_per_batch, seq=seq, alpha=alpha)

        wr_t = w_router[l].T
        wr_hi = wr_t.astype(BF16)
        wr_lo = (wr_t - wr_hi.astype(F32)).astype(BF16)
        tokp, idx, gate, rank, cnt, fsh = _router_call(
            xa, mod3, wr_hi, wr_lo, router_bias[l].reshape(N_EXPERTS, 1),
            w_shared_gate_up[l].astype(BF16), w_shared_down[l].astype(BF16), tiles_per_batch=tiles_per_batch)
        counts = cnt[:, 0]
        padded = (counts + EXPERT_BLOCK - 1) // EXPERT_BLOCK * EXPERT_BLOCK
        pad_end = jnp.cumsum(padded)
        offs = pad_end - padded
        expert_ids = jnp.arange(N_EXPERTS, dtype=jnp.int32)
        blk_row = jnp.arange(n_blocks, dtype=jnp.int32) * EXPERT_BLOCK
        block_expert = jnp.minimum(jnp.sum(pad_end[None, :] <= blk_row[:, None], axis=1), N_EXPERTS - 1)
        n_used = pad_end[-1:] // EXPERT_BLOCK
        used = counts > 0
        ordinal = jnp.cumsum(used) - 1
        hit = used[None, :] & (ordinal[None, :] == expert_ids[:, None])
        used_expert = jnp.sum(jnp.where(hit, expert_ids[None, :], 0), axis=1)
        used_last_row = jnp.sum(jnp.where(hit, (pad_end - EXPERT_BLOCK)[None, :], 0), axis=1)
        n_used_experts = jnp.sum(used)[None]
        block_ordinal = ordinal[block_expert]
        i32 = lambda a: a.astype(jnp.int32)

        dest = _dest_call(idx, rank, i32(offs).reshape(N_EXPERTS, 1))
        xs = _dispatch_call(i32(used_last_row), i32(n_used_experts), dest, tokp, n_sorted)
        ys = _expert_call(i32(block_expert), i32(n_used), i32(block_ordinal), i32(used_expert),
                          i32(n_used_experts), xs, w_expert_gate_up, w_expert_down, l)
        xa = _combine_call(dest, ys, xa, fsh, gate, mod3, lng[1], lnb[1], tiles_per_batch=tiles_per_batch,
                           alpha=alpha, drop_context=(l == depth - 1))

    return xa.reshape(batch, seq, d)
```

```python
import functools
import math

import jax
import jax.numpy as jnp
from jax import lax
from jax.experimental import pallas as pl
from jax.experimental.pallas import tpu as pltpu
from jax.experimental.pallas import tpu_sc as plsc

F32 = jnp.float32
BF16 = jnp.bfloat16
HIGHEST = lax.Precision.HIGHEST

D_MODEL = 1024
CTX_LEN = 256
GRID_W = 64
DA_HEADS = 4
DA_DIM = 64
DA_VDIM = 2 * DA_DIM
DA_WIDTH = DA_HEADS * DA_VDIM
ROPE_BASE = 10000.0
POOL_WINDOWS = (2, 4, 8, 16)
POOL_GROUP = 64
POOL_WIDTH = len(POOL_WINDOWS) * POOL_GROUP
POOL_HALO = 8
RET_HEADS = 4
RET_DK = 64
RET_WIDTH = RET_HEADS * RET_DK
RET_CHUNK = 128
QK_WIDTH = 2 * DA_HEADS * 2 * DA_DIM
IN_WIDTH = QK_WIDTH + DA_WIDTH + POOL_WIDTH + 4 * RET_WIDTH
N_EXPERTS = 256
TOP_K = 8
N_GROUPS = 8
GROUP_SIZE = N_EXPERTS // N_GROUPS
TOPK_GROUPS = 4
EXPERT_HIDDEN = 256
ROUTED_SCALE = 2.5
LN_EPS = 1e-6
RMS_EPS = 1e-5

LANES = 128
ROW_TILE = 256
ATTN_Q_TILE = 256
ATTN_K_CHUNK = 256
ATTN_UNROLL = 16
SC_NUM_CORES = 2
SC_NUM_SUBCORES = 16
SC_GATHER_WINDOW = 128
EXPERT_BLOCK = 256
PACK_W = D_MODEL // 2
PACK_S = PACK_W // LANES
ROW_S = D_MODEL // LANES
VMEM_LIMIT = 56 * 1024 * 1024


def _cparams(*sem):
    return pltpu.CompilerParams(dimension_semantics=sem, vmem_limit_bytes=VMEM_LIMIT)


def _sigmoid(x):
    return 1.0 / (1.0 + jnp.exp(-x))


def _layer_norm_rows(x):
    mu = jnp.mean(x, axis=-1, keepdims=True)
    xc = x - mu
    var = jnp.mean(xc * xc, axis=-1, keepdims=True)
    return xc * lax.rsqrt(var + LN_EPS)


def _pack_bf16_pairs(x):
    half = x.shape[1] // 2
    bits = pltpu.bitcast(x.astype(BF16).astype(F32), jnp.uint32)
    return lax.shift_right_logical(bits[:, 0:half], jnp.uint32(16)) | (bits[:, half:] & jnp.uint32(0xFFFF0000))


def _unpack_bf16_pairs(word):
    lo = pltpu.bitcast(lax.shift_left(word, jnp.uint32(16)), F32)
    hi = pltpu.bitcast(word & jnp.uint32(0xFFFF0000), F32)
    return lo, hi


def _mod_row(i, tiles_per_batch):
    return jnp.where(i % tiles_per_batch == tiles_per_batch - 1, 2, i // tiles_per_batch)


def _mod_kernel(c_ref, w_ref, b_ref, o_ref):
    c = c_ref[...]
    s = c * _sigmoid(c)
    o_ref[...] = jnp.dot(s, w_ref[...], precision=HIGHEST, preferred_element_type=F32) + b_ref[...]


def _mod_call(cvec, w_mod, b_mod):
    depth, d, n = w_mod.shape
    tn = 1536
    return pl.pallas_call(
        _mod_kernel,
        grid=(depth, n // tn),
        in_specs=[
            pl.BlockSpec((8, d), lambda l, j: (0, 0)),
            pl.BlockSpec((None, d, tn), lambda l, j: (l, 0, j)),
            pl.BlockSpec((None, 1, tn), lambda l, j: (l, 0, j)),
        ],
        out_specs=pl.BlockSpec((None, 8, tn), lambda l, j: (l, 0, j)),
        out_shape=jax.ShapeDtypeStruct((depth, 8, n), F32),
        compiler_params=_cparams("arbitrary", "arbitrary"),
        name="mod",
    )(cvec, w_mod, b_mod.reshape(depth, 1, n))


def _inproj_kernel(x_ref, mod_ref, w_ref, wvt_ref, ct_ref, st_ref, qk_ref, vt_ref, u_ref, r_ref, g_ref):
    d = D_MODEL
    xn = _layer_norm_rows(x_ref[...])
    h = (xn * (1.0 + mod_ref[:, d:2 * d]) + mod_ref[:, 0:d]).astype(BF16)

    a = jnp.dot(h, w_ref[:, 0:QK_WIDTH], preferred_element_type=F32)
    lane = lax.broadcasted_iota(jnp.int32, (a.shape[0], LANES), 1)
    first_half = (lane % 32) < 16
    ct = ct_ref[...]
    st = st_ref[...]
    for s in range(QK_WIDTH // LANES):
        blk = a[:, s * LANES:(s + 1) * LANES]
        partner = jnp.where(first_half, pltpu.roll(blk, LANES - 16, 1), pltpu.roll(blk, 16, 1))
        rot = blk * ct + partner * st
        if s < QK_WIDTH // LANES // 2:
            rot = rot * (DA_DIM ** -0.5 * math.log2(math.e))
        qk_ref[:, s * LANES:(s + 1) * LANES] = rot.astype(BF16)

    vt_ref[...] = lax.dot_general(wvt_ref[...], h, (((1,), (1,)), ((), ())),
                                  preferred_element_type=F32).astype(BF16)
    o = QK_WIDTH + DA_WIDTH
    u_ref[...] = jnp.dot(h, w_ref[:, o:o + POOL_WIDTH], preferred_element_type=F32)
    o += POOL_WIDTH
    r = jnp.dot(h, w_ref[:, o:o + 3 * RET_WIDTH], preferred_element_type=F32)
    r_ref[:, 0:RET_WIDTH] = r[:, 0:RET_WIDTH].astype(BF16)
    r_ref[:, RET_WIDTH:2 * RET_WIDTH] = (r[:, RET_WIDTH:2 * RET_WIDTH] * (RET_DK ** -0.5)).astype(BF16)
    r_ref[:, 2 * RET_WIDTH:] = r[:, 2 * RET_WIDTH:].astype(BF16)
    o += 3 * RET_WIDTH
    g_ref[...] = jnp.dot(h, w_ref[:, o:o + RET_WIDTH], preferred_element_type=F32)


def _inproj_call(x, mod3, w_in_bf, w_vt_bf, rope_c, rope_s, tiles_per_batch):
    r, d = x.shape
    t = ROW_TILE
    nt = r // t
    row = lambda i: (i, 0)
    return pl.pallas_call(
        _inproj_kernel,
        grid=(nt,),
        in_specs=[
            pl.BlockSpec((t, d), row),
            pl.BlockSpec((None, 1, 6 * d), lambda i: (_mod_row(i, tiles_per_batch), 0, 0)),
            pl.BlockSpec((d, IN_WIDTH), lambda i: (0, 0)),
            pl.BlockSpec((DA_WIDTH, d), lambda i: (0, 0)),
            pl.BlockSpec((t, LANES), lambda i: (i % tiles_per_batch, 0)),
            pl.BlockSpec((t, LANES), lambda i: (i % tiles_per_batch, 0)),
        ],
        out_specs=[
            pl.BlockSpec((t, QK_WIDTH), row),
            pl.BlockSpec((DA_WIDTH, t), lambda i: (0, i)),
            pl.BlockSpec((t, POOL_WIDTH), row),
            pl.BlockSpec((t, 3 * RET_WIDTH), row),
            pl.BlockSpec((t, RET_WIDTH), row),
        ],
        out_shape=[
            jax.ShapeDtypeStruct((r, QK_WIDTH), BF16),
            jax.ShapeDtypeStruct((DA_WIDTH, r), BF16),
            jax.ShapeDtypeStruct((r, POOL_WIDTH), F32),
            jax.ShapeDtypeStruct((r, 3 * RET_WIDTH), BF16),
            jax.ShapeDtypeStruct((r, RET_WIDTH), F32),
        ],
        compiler_params=_cparams("arbitrary"),
        name="inproj",
    )(x, mod3, w_in_bf, w_vt_bf, rope_c, rope_s)


def _attn_kernel(lam_ref, q_ref, k_ref, vt_ref, o_ref, s_ref, *, k_chunk, seq, lambda_init):
    q = q_ref[...]
    mq = q.shape[0]
    lane = lax.broadcasted_iota(jnp.int32, q.shape, 1)
    zero = jnp.zeros_like(q)
    q2 = jnp.concatenate([jnp.where(lane < DA_DIM, q, zero), jnp.where(lane >= DA_DIM, q, zero)], axis=0)
    qt = q2.astype(F32).T.astype(BF16)

    n_chunks = (seq + CTX_LEN) // k_chunk
    last = n_chunks - 1
    is_ctx_tile = pl.program_id(2) == pl.num_programs(2) - 1
    n_iters = jnp.where(is_ctx_tile, 0, last // ATTN_UNROLL)

    def score_chunk(c, m):
        off = pl.multiple_of(c * k_chunk, k_chunk)
        s = jnp.dot(k_ref[pl.ds(off, k_chunk), :], qt, preferred_element_type=F32)
        s_ref[c] = s
        return jnp.maximum(m, jnp.max(s, axis=0, keepdims=True))

    def pass1(it, m):
        for u in range(ATTN_UNROLL):
            m = score_chunk(it * ATTN_UNROLL + u, m)
        return m

    m = lax.fori_loop(0, n_iters, pass1, jnp.full((1, 2 * mq), -jnp.inf, F32))
    m = score_chunk(last, m)

    ones_rows = jnp.where(lax.broadcasted_iota(jnp.int32, (16, k_chunk), 0) == 0, 1.0, 0.0).astype(BF16)

    def value_chunk(c, acc):
        off = pl.multiple_of(c * k_chunk, k_chunk)
        vt = jnp.concatenate([vt_ref[:, pl.ds(off, k_chunk)], ones_rows], axis=0)
        p = jnp.exp2((s_ref[c] - m).astype(BF16))
        return acc + jnp.dot(vt, p, preferred_element_type=F32)

    def pass2(it, acc):
        for u in range(ATTN_UNROLL):
            acc = value_chunk(it * ATTN_UNROLL + u, acc)
        return acc

    acc = lax.fori_loop(0, n_iters, pass2, jnp.zeros((DA_VDIM + 16, 2 * mq), F32))
    acc = value_chunk(last, acc)
    l0, l1 = acc[DA_VDIM:DA_VDIM + 1, 0:mq], acc[DA_VDIM:DA_VDIM + 1, mq:]
    a0, a1 = acc[0:DA_VDIM, 0:mq], acc[0:DA_VDIM, mq:]

    lv = lam_ref[...]
    lam = (jnp.exp(jnp.sum(lv[0:1] * lv[1:2], axis=-1, keepdims=True))
           - jnp.exp(jnp.sum(lv[2:3] * lv[3:4], axis=-1, keepdims=True)) + lambda_init)
    o = a0 / l0 - lam * (a1 / l1)
    o = o * lax.rsqrt(jnp.mean(o * o, axis=0, keepdims=True) + RMS_EPS) * (1.0 - lambda_init)
    o_ref[...] = o.T.astype(BF16)


def _attn_call(lam_vec, qk, vda, *, batch, rows_per_batch, seq, lambda_init):
    tq = ATTN_Q_TILE
    assert seq % (ATTN_K_CHUNK * ATTN_UNROLL) == 0 and rows_per_batch - seq == CTX_LEN == tq == ATTN_K_CHUNK
    nq = rows_per_batch // tq
    kern = functools.partial(_attn_kernel, k_chunk=ATTN_K_CHUNK, seq=seq, lambda_init=lambda_init)
    return pl.pallas_call(
        kern,
        grid=(batch, DA_HEADS, nq),
        in_specs=[
            pl.BlockSpec((4, DA_DIM), lambda b, h, i: (0, 0)),
            pl.BlockSpec((tq, DA_VDIM), lambda b, h, i: (b * nq + i, h)),
            pl.BlockSpec((rows_per_batch, DA_VDIM), lambda b, h, i: (b, DA_HEADS + h)),
            pl.BlockSpec((DA_VDIM, rows_per_batch), lambda b, h, i: (h, b)),
        ],
        out_specs=pl.BlockSpec((tq, DA_VDIM), lambda b, h, i: (b * nq + i, h)),
        out_shape=jax.ShapeDtypeStruct((qk.shape[0], DA_WIDTH), BF16),
        scratch_shapes=[pltpu.VMEM((rows_per_batch // ATTN_K_CHUNK, ATTN_K_CHUNK, 2 * tq), F32)],
        compiler_params=_cparams("arbitrary", "arbitrary", "arbitrary"),
        name="diff_attn",
    )(lam_vec, qk, qk, vda)


def _ret_kernel(ld_ref, f_ref, b_ref, of_ref, ob_ref, dm_ref, qd_ref, kd_ref, cd_ref, st_ref):
    c = pl.program_id(1)
    ch = RET_CHUNK
    w = RET_WIDTH
    lane_head = lax.broadcasted_iota(jnp.int32, (1, w), 1) // RET_DK

    @pl.when(c == 0)
    def _():
        st_ref[...] = jnp.zeros_like(st_ref)
        ri = lax.broadcasted_iota(jnp.int32, (ch, ch), 0)
        ci = lax.broadcasted_iota(jnp.int32, (ch, ch), 1)
        rowf = lax.broadcasted_iota(jnp.int32, (ch, w), 0).astype(F32)
        for d in range(2):
            lg_lane = jnp.zeros((1, w), F32)
            for hh in range(RET_HEADS):
                lg = -jnp.exp(jnp.full((1, 1), ld_ref[d, hh], F32))
                lg_lane = jnp.where(lane_head == hh, lg, lg_lane)
                dist = ((ri - ci) if d == 0 else (ci - ri)).astype(F32)
                dm_ref[d, hh] = jnp.where(dist >= 0, jnp.exp(dist * lg), 0.0)
            if d == 0:
                qd_ref[d] = jnp.exp((rowf + 1.0) * lg_lane)
                kd_ref[d] = jnp.exp((ch - 1.0 - rowf) * lg_lane)
            else:
                qd_ref[d] = jnp.exp((ch - rowf) * lg_lane)
                kd_ref[d] = jnp.exp(rowf * lg_lane)
            cd_ref[d] = jnp.exp(float(ch) * lg_lane)

    rblk = lax.broadcasted_iota(jnp.int32, (w, w), 0) // RET_DK
    cblk = lax.broadcasted_iota(jnp.int32, (w, w), 1) // RET_DK
    for d, (src, dst) in enumerate(((f_ref, of_ref), (b_ref, ob_ref))):
        q = src[:, 0:w]
        k = src[:, w:2 * w]
        v = src[:, 2 * w:3 * w]
        st = st_ref[d]
        o = jnp.dot((q.astype(F32) * qd_ref[d]).astype(BF16), st.astype(BF16), preferred_element_type=F32)
        for hh in range(RET_HEADS):
            in_head = lane_head == hh
            qm = jnp.where(in_head, q, jnp.zeros_like(q))
            s = lax.dot_general(qm, k, (((1,), (1,)), ((), ())), preferred_element_type=F32)
            intra = (s * dm_ref[d, hh]).astype(BF16)
            o = o + jnp.where(in_head, jnp.dot(intra, v, preferred_element_type=F32), 0.0)
        dst[...] = o
        kk_t = (k.astype(F32) * kd_ref[d]).T.astype(BF16)
        upd = jnp.dot(kk_t, v, preferred_element_type=F32)
        st_ref[d] = jnp.where(rblk == cblk, st * cd_ref[d] + upd, 0.0)


def _ret_call(log_decay, rqkv, *, batch, rows_per_batch, seq):
    ch = RET_CHUNK
    nc = rows_per_batch // ch
    n_lat = seq // ch
    n_ctx = nc - n_lat

    def fwd(b, c):
        return (b * nc + jnp.where(c < n_ctx, n_lat + c, c - n_ctx), 0)

    def bwd(b, c):
        return (b * nc + nc - 1 - c, 0)

    w = RET_WIDTH
    return pl.pallas_call(
        _ret_kernel,
        grid=(batch, nc),
        in_specs=[
            pl.BlockSpec(memory_space=pltpu.SMEM),
            pl.BlockSpec((ch, 3 * w), fwd),
            pl.BlockSpec((ch, 3 * w), bwd),
        ],
        out_specs=[pl.BlockSpec((ch, w), fwd), pl.BlockSpec((ch, w), bwd)],
        out_shape=[jax.ShapeDtypeStruct((rqkv.shape[0], w), F32)] * 2,
        scratch_shapes=[
            pltpu.VMEM((2, RET_HEADS, ch, ch), F32),
            pltpu.VMEM((2, ch, w), F32),
            pltpu.VMEM((2, ch, w), F32),
            pltpu.VMEM((2, 1, w), F32),
            pltpu.VMEM((2, w, w), F32),
        ],
        compiler_params=_cparams("arbitrary", "arbitrary"),
        name="retention",
    )(log_decay, rqkv, rqkv)


def _mixout_kernel(x_ref, da_ref, u_ref, up_ref, un_ref, of_ref, ob_ref, rg_ref, mod_ref, wo_ref, pw_ref,
                   ps_ref, lng_ref, lnb_ref, o_ref, *, tiles_per_batch, seq, alpha):
    d = D_MODEL
    t = x_ref.shape[0]
    i = pl.program_id(0)
    j = i % tiles_per_batch
    is_ctx = j == tiles_per_batch - 1
    stream_len = jnp.where(is_ctx, CTX_LEN, seq)
    p0 = jnp.where(is_ctx, 0, j * t)

    u = u_ref[...]
    prev = jnp.where(p0 > 0, up_ref[...], 0.0)
    nxt = jnp.where(p0 + t < stream_len, un_ref[...], 0.0)
    ext = jnp.concatenate([prev, u, nxt], axis=0)
    n = t + 2 * POOL_HALO
    a2 = ext + pltpu.roll(ext, 1, 0)
    a4 = pltpu.roll(a2, 1, 0) + pltpu.roll(a2, n - 1, 0)
    a8 = pltpu.roll(a4, 2, 0) + pltpu.roll(a4, n - 2, 0)
    a16 = pltpu.roll(a8, 4, 0) + pltpu.roll(a8, n - 4, 0)
    pos = p0 + lax.broadcasted_iota(jnp.int32, (t, POOL_WIDTH), 0)
    group = lax.broadcasted_iota(jnp.int32, (1, POOL_WIDTH), 1) // POOL_GROUP
    mean = jnp.zeros((t, POOL_WIDTH), F32)
    for gi, (wnd, asum) in enumerate(zip(POOL_WINDOWS, (a2, a4, a8, a16))):
        cnt = jnp.minimum(pos + wnd // 2, stream_len) - jnp.maximum(pos - wnd // 2, 0)
        mean = jnp.where(group == gi, asum[POOL_HALO:POOL_HALO + t] / cnt.astype(F32), mean)
    pool = jnp.dot((mean - u).astype(BF16), pw_ref[...], preferred_element_type=F32) * ps_ref[...]

    o = of_ref[...] + ob_ref[...]
    head = lax.broadcasted_iota(jnp.int32, (1, RET_WIDTH), 1) // RET_DK

    def head_mean(val):
        out = jnp.zeros_like(val)
        for hh in range(RET_HEADS):
            m = jnp.sum(jnp.where(head == hh, val, 0.0), axis=-1, keepdims=True) * (1.0 / RET_DK)
            out = jnp.where(head == hh, m, out)
        return out

    oc = o - head_mean(o)
    rn = oc * lax.rsqrt(head_mean(oc * oc) + LN_EPS)
    g = rg_ref[...]
    ret = rn * (g * _sigmoid(g))

    y = jnp.dot(da_ref[...], wo_ref[0:DA_WIDTH, :], preferred_element_type=F32)
    y = y + jnp.dot(pool.astype(BF16), wo_ref[DA_WIDTH:DA_WIDTH + POOL_WIDTH, :], preferred_element_type=F32)
    y = y + jnp.dot(ret.astype(BF16), wo_ref[DA_WIDTH + POOL_WIDTH:, :], preferred_element_type=F32)
    z = alpha * x_ref[...] + mod_ref[:, 2 * d:3 * d] * y
    o_ref[...] = _layer_norm_rows(z) * lng_ref[...] + lnb_ref[...]


def _mixout_call(x, da, u, o_f, o_b, rg, mod3, w_out_bf, pool_bd, pool_scale, ln_g, ln_b, *, tiles_per_batch, seq,
                 alpha):
    r, d = x.shape
    t = ROW_TILE
    nt = r // t
    hb = t // POOL_HALO
    n_halo_blocks = r // POOL_HALO
    row = lambda i: (i, 0)
    const = lambda i: (0, 0)
    kern = functools.partial(_mixout_kernel, tiles_per_batch=tiles_per_batch, seq=seq, alpha=alpha)
    return pl.pallas_call(
        kern,
        grid=(nt,),
        in_specs=[
            pl.BlockSpec((t, d), row),
            pl.BlockSpec((t, DA_WIDTH), row),
            pl.BlockSpec((t, POOL_WIDTH), row),
            pl.BlockSpec((POOL_HALO, POOL_WIDTH), lambda i: (jnp.maximum(i * hb - 1, 0), 0)),
            pl.BlockSpec((POOL_HALO, POOL_WIDTH), lambda i: (jnp.minimum((i + 1) * hb, n_halo_blocks - 1), 0)),
            pl.BlockSpec((t, RET_WIDTH), row),
            pl.BlockSpec((t, RET_WIDTH), row),
            pl.BlockSpec((t, RET_WIDTH), row),
            pl.BlockSpec((None, 1, 6 * d), lambda i: (_mod_row(i, tiles_per_batch), 0, 0)),
            pl.BlockSpec((d, d), const),
            pl.BlockSpec((POOL_WIDTH, POOL_WIDTH), const),
            pl.BlockSpec((1, POOL_WIDTH), const),
            pl.BlockSpec((1, d), const),
            pl.BlockSpec((1, d), const),
        ],
        out_specs=pl.BlockSpec((t, d), row),
        out_shape=jax.ShapeDtypeStruct((r, d), F32),
        compiler_params=_cparams("arbitrary"),
        name="mixer_out",
    )(x, da, u, u, u, o_f, o_b, rg, mod3, w_out_bf, pool_bd, pool_scale, ln_g, ln_b)


def _router_kernel(x_ref, mod_ref, wrh_ref, wrl_ref, bias_ref, wsgu_ref, wsdn_ref,
                   tokp_ref, idx_ref, gate_ref, rank_ref, cnt_ref, fsh_ref, carry_ref):
    d = D_MODEL
    t = x_ref.shape[0]
    ne = N_EXPERTS
    neg = -jnp.inf

    @pl.when(pl.program_id(0) == 0)
    def _():
        carry_ref[...] = jnp.zeros_like(carry_ref)

    tok = _layer_norm_rows(x_ref[...]) * (1.0 + mod_ref[:, 4 * d:5 * d]) + mod_ref[:, 3 * d:4 * d]
    tok_hi = tok.astype(BF16)
    tok_lo = (tok - tok_hi.astype(F32)).astype(BF16)

    tokp_ref[...] = _pack_bf16_pairs(tok)

    hs = jnp.dot(tok_hi, wsgu_ref[...], preferred_element_type=F32)
    gs, us = hs[:, 0:EXPERT_HIDDEN], hs[:, EXPERT_HIDDEN:]
    fsh_ref[...] = jnp.dot((gs * _sigmoid(gs) * us).astype(BF16), wsdn_ref[...], preferred_element_type=F32)

    nt_dims = (((1,), (1,)), ((), ()))
    logits = (lax.dot_general(wrh_ref[...], tok_hi, nt_dims, preferred_element_type=F32)
              + lax.dot_general(wrh_ref[...], tok_lo, nt_dims, preferred_element_type=F32)
              + lax.dot_general(wrl_ref[...], tok_hi, nt_dims, preferred_element_type=F32))
    scores = _sigmoid(logits)
    biased = scores + bias_ref[...]

    gidx = lax.broadcasted_iota(jnp.int32, (GROUP_SIZE, t), 0)
    blocks, gscores = [], []
    for g in range(N_GROUPS):
        blk = biased[g * GROUP_SIZE:(g + 1) * GROUP_SIZE, :]
        m1 = jnp.max(blk, axis=0, keepdims=True)
        first = jnp.min(jnp.where(blk == m1, gidx, GROUP_SIZE), axis=0, keepdims=True)
        m2 = jnp.max(jnp.where(gidx == first, neg, blk), axis=0, keepdims=True)
        blocks.append(blk)
        gscores.append(m1 + m2)

    keep = [jnp.zeros((1, t), F32) for _ in range(N_GROUPS)]
    for _ in range(TOPK_GROUPS):
        m = gscores[0]
        for gs_ in gscores[1:]:
            m = jnp.maximum(m, gs_)
        found = jnp.zeros((1, t), F32)
        for g in range(N_GROUPS):
            hit = jnp.where(gscores[g] == m, 1.0 - found, 0.0)
            found = found + hit
            keep[g] = keep[g] + hit
            gscores[g] = jnp.where(hit > 0.0, neg, gscores[g])
    masked = jnp.concatenate([jnp.where(keep[g] > 0.0, blocks[g], neg) for g in range(N_GROUPS)], axis=0)

    ei = lax.broadcasted_iota(jnp.int32, (ne, t), 0)
    cur = masked
    onehot = jnp.zeros((ne, t), F32)
    idxs, gates = [], []
    for _ in range(TOP_K):
        m = jnp.max(cur, axis=0, keepdims=True)
        ii = jnp.min(jnp.where(cur == m, ei, ne), axis=0, keepdims=True)
        sel = ei == ii
        idxs.append(ii)
        gates.append(jnp.sum(jnp.where(sel, scores, 0.0), axis=0, keepdims=True))
        onehot = jnp.where(sel, 1.0, onehot)
        cur = jnp.where(sel, neg, cur)
    gsum = gates[0]
    for gk in gates[1:]:
        gsum = gsum + gk
    for k in range(TOP_K):
        idx_ref[k:k + 1, :] = idxs[k]
        gate_ref[k:k + 1, :] = gates[k] / gsum * ROUTED_SCALE

    ti = lax.broadcasted_iota(jnp.int32, (t, t), 0)
    tj = lax.broadcasted_iota(jnp.int32, (t, t), 1)
    before = jnp.where(ti < tj, 1.0, 0.0).astype(BF16)
    prefix = jnp.dot(onehot.astype(BF16), before, preferred_element_type=F32) + carry_ref[:, 0:1]
    for k in range(TOP_K):
        rank_k = jnp.sum(jnp.where(ei == idxs[k], prefix, 0.0), axis=0, keepdims=True)
        rank_ref[k:k + 1, :] = rank_k.astype(jnp.int32)
    carry_ref[...] = carry_ref[...] + jnp.sum(onehot, axis=1, keepdims=True)
    cnt_ref[...] = carry_ref[...].astype(jnp.int32)


def _router_call(x, mod3, wr_hi, wr_lo, bias_col, ws_gu_bf, ws_dn_bf, *, tiles_per_batch):
    r, d = x.shape
    t = ROW_TILE
    nt = r // t
    row = lambda i: (i, 0)
    col = lambda i: (0, i)
    const = lambda i: (0, 0)
    return pl.pallas_call(
        _router_kernel,
        grid=(nt,),
        in_specs=[
            pl.BlockSpec((t, d), row),
            pl.BlockSpec((None, 1, 6 * d), lambda i: (_mod_row(i, tiles_per_batch), 0, 0)),
            pl.BlockSpec((N_EXPERTS, d), const),
            pl.BlockSpec((N_EXPERTS, d), const),
            pl.BlockSpec((N_EXPERTS, 1), const),
            pl.BlockSpec((d, 2 * EXPERT_HIDDEN), const),
            pl.BlockSpec((EXPERT_HIDDEN, d), const),
        ],
        out_specs=[
            pl.BlockSpec((t, PACK_W), row),
            pl.BlockSpec((TOP_K, t), col),
            pl.BlockSpec((TOP_K, t), col),
            pl.BlockSpec((TOP_K, t), col),
            pl.BlockSpec((N_EXPERTS, LANES), const),
            pl.BlockSpec((t, d), row),
        ],
        out_shape=[
            jax.ShapeDtypeStruct((r, PACK_W), jnp.uint32),
            jax.ShapeDtypeStruct((TOP_K, r), jnp.int32),
            jax.ShapeDtypeStruct((TOP_K, r), F32),
            jax.ShapeDtypeStruct((TOP_K, r), jnp.int32),
            jax.ShapeDtypeStruct((N_EXPERTS, LANES), jnp.int32),
            jax.ShapeDtypeStruct((r, d), F32),
        ],
        scratch_shapes=[pltpu.VMEM((N_EXPERTS, LANES), F32)],
        compiler_params=_cparams("arbitrary"),
        name="router",
    )(x, mod3, wr_hi, wr_lo, bias_col, ws_gu_bf, ws_dn_bf)


def _dest_kernel(idx_ref, rank_ref, offs_ref, dest_ref):
    t = idx_ref.shape[1]
    ei = lax.broadcasted_iota(jnp.int32, (N_EXPERTS, t), 0)
    offs = offs_ref[...].astype(F32)
    for k in range(TOP_K):
        start = jnp.sum(jnp.where(ei == idx_ref[k:k + 1, :], offs, 0.0), axis=0, keepdims=True)
        dest_ref[k:k + 1, :] = start.astype(jnp.int32) + rank_ref[k:k + 1, :]


def _dest_call(idx, rank, offs_col):
    r = idx.shape[1]
    t = ROW_TILE
    col = lambda i: (0, i)
    return pl.pallas_call(
        _dest_kernel,
        grid=(r // t,),
        in_specs=[pl.BlockSpec((TOP_K, t), col), pl.BlockSpec((TOP_K, t), col),
                  pl.BlockSpec((N_EXPERTS, 1), lambda i: (0, 0))],
        out_specs=pl.BlockSpec((TOP_K, t), col),
        out_shape=jax.ShapeDtypeStruct((TOP_K, r), jnp.int32),
        compiler_params=_cparams("arbitrary"),
        name="moe_dest",
    )(idx, rank, offs_col)


def _dispatch_kernel(last_row_ref, n_used_ref, dest_ref, tokp_ref, xs_ref, zeros_ref, sem, zero_sem):
    t = dest_ref.shape[1]

    @pl.when(pl.program_id(0) == 0)
    def _():
        zeros_ref[...] = jnp.zeros_like(zeros_ref)

        def zero_copy(o):
            row0 = pl.multiple_of(last_row_ref[o], EXPERT_BLOCK)
            return pltpu.make_async_copy(zeros_ref, xs_ref.at[pl.ds(row0, EXPERT_BLOCK), :], zero_sem)

        def start(o, carry):
            zero_copy(o).start()
            return carry

        def wait(o, carry):
            zero_copy(o).wait()
            return carry

        lax.fori_loop(0, n_used_ref[0], start, 0)
        lax.fori_loop(0, n_used_ref[0], wait, 0)

    def issue(tt, carry):
        for k in range(TOP_K):
            pltpu.make_async_copy(tokp_ref.at[pl.ds(tt, 1), :], xs_ref.at[pl.ds(dest_ref[k, tt], 1), :],
                                  sem).start(priority=k % 2)
        return carry

    lax.fori_loop(0, t, issue, 0, unroll=2)
    all_rows = xs_ref.at[pl.ds(0, TOP_K * t), :]
    pltpu.make_async_copy(all_rows, all_rows, sem).wait()


def _dispatch_call(used_last_row, n_used_experts, dest, tokp, n_sorted):
    r = dest.shape[1]
    t = ROW_TILE
    grid_spec = pltpu.PrefetchScalarGridSpec(
        num_scalar_prefetch=2,
        grid=(r // t,),
        in_specs=[
            pl.BlockSpec((TOP_K, t), lambda i, lr, nu: (0, i), memory_space=pltpu.SMEM),
            pl.BlockSpec((t, PACK_W), lambda i, lr, nu: (i, 0)),
        ],
        out_specs=pl.BlockSpec(memory_space=pl.ANY),
        scratch_shapes=[pltpu.VMEM((EXPERT_BLOCK, PACK_W), jnp.uint32), pltpu.SemaphoreType.DMA(()),
                        pltpu.SemaphoreType.DMA(())],
    )
    return pl.pallas_call(
        _dispatch_kernel,
        grid_spec=grid_spec,
        out_shape=jax.ShapeDtypeStruct((n_sorted, PACK_W), jnp.uint32),
        compiler_params=_cparams("arbitrary"),
        name="moe_dispatch",
    )(used_last_row, n_used_experts, dest, tokp)


def _expert_kernel(be_ref, nb_ref, ord_ref, ue_ref, nue_ref, xs_ref, wgu_hbm, wdn_hbm, ys_ref, wgu_f32, wdn_f32,
                   wgu_bf, wdn_bf, sems, *, layer):
    j = pl.program_id(0)

    def weight_copies(o):
        slot = o % 2
        e = ue_ref[o]
        return (pltpu.make_async_copy(wgu_hbm.at[layer, e], wgu_f32.at[slot], sems.at[0, slot]),
                pltpu.make_async_copy(wdn_hbm.at[layer, e], wdn_f32.at[slot], sems.at[1, slot]))

    def start_weights(o):
        @pl.when(o < nue_ref[0])
        def _():
            for cp in weight_copies(o):
                cp.start()

    @pl.when(j < nb_ref[0])
    def _():
        o = ord_ref[j]
        changed = jnp.logical_or(j == 0, be_ref[j] != be_ref[jnp.maximum(j - 1, 0)])

        @pl.when(j == 0)
        def _():
            start_weights(0)
            start_weights(1)

        @pl.when(changed)
        def _():
            for cp in weight_copies(o):
                cp.wait()
            slot = o % 2
            wgu_bf[...] = wgu_f32[slot].astype(BF16)
            wdn_bf[...] = wdn_f32[slot].astype(BF16)
            start_weights(o + 2)

        x_lo, x_hi = _unpack_bf16_pairs(xs_ref[...])
        h = (jnp.dot(x_lo.astype(BF16), wgu_bf[0:PACK_W, :], preferred_element_type=F32)
             + jnp.dot(x_hi.astype(BF16), wgu_bf[PACK_W:, :], preferred_element_type=F32))
        g, u = h[:, 0:EXPERT_HIDDEN], h[:, EXPERT_HIDDEN:]
        y = jnp.dot((g * _sigmoid(g) * u).astype(BF16), wdn_bf[...], preferred_element_type=F32)
        ys_ref[...] = _pack_bf16_pairs(y)


def _expert_call(block_expert, n_blocks_used, block_ordinal, used_expert, n_used_experts, xs, w_gu, w_dn, layer):
    n_rows = xs.shape[0]
    bm = EXPERT_BLOCK
    d = D_MODEL
    used_block = lambda j, be, nb, od, ue, nue: (jnp.minimum(j, nb[0] - 1), 0)
    grid_spec = pltpu.PrefetchScalarGridSpec(
        num_scalar_prefetch=5,
        grid=(n_rows // bm,),
        in_specs=[
            pl.BlockSpec((bm, PACK_W), used_block),
            pl.BlockSpec(memory_space=pl.ANY),
            pl.BlockSpec(memory_space=pl.ANY),
        ],
        out_specs=pl.BlockSpec((bm, PACK_W), used_block),
        scratch_shapes=[
            pltpu.VMEM((2, d, 2 * EXPERT_HIDDEN), F32),
            pltpu.VMEM((2, EXPERT_HIDDEN, d), F32),
            pltpu.VMEM((d, 2 * EXPERT_HIDDEN), BF16),
            pltpu.VMEM((EXPERT_HIDDEN, d), BF16),
            pltpu.SemaphoreType.DMA((2, 2)),
        ],
    )
    return pl.pallas_call(
        functools.partial(_expert_kernel, layer=layer),
        grid_spec=grid_spec,
        out_shape=jax.ShapeDtypeStruct((n_rows, PACK_W), jnp.uint32),
        compiler_params=_cparams("arbitrary"),
        name="moe_experts",
    )(block_expert, n_blocks_used, block_ordinal, used_expert, n_used_experts, xs, w_gu, w_dn)


def _sc_gather_rows(table, indices):
    n = indices.shape[0]
    width = table.shape[1]
    workers = SC_NUM_CORES * SC_NUM_SUBCORES
    assert n % (SC_GATHER_WINDOW * workers) == 0
    per_worker = n // workers
    mesh = plsc.VectorSubcoreMesh(core_axis_name="core", subcore_axis_name="subcore", num_cores=SC_NUM_CORES,
                                  num_subcores=SC_NUM_SUBCORES)

    @functools.partial(
        pl.kernel, out_type=jax.ShapeDtypeStruct((n, width), table.dtype), mesh=mesh,
        scratch_types=[pltpu.VMEM((SC_GATHER_WINDOW,), jnp.int32),
                       pltpu.VMEM((SC_GATHER_WINDOW, width), table.dtype),
                       pltpu.SemaphoreType.DMA],
        name="moe_sc_gather")
    def gather_kernel(table_hbm, idx_hbm, out_hbm, idx_vmem, rows_vmem, sem):
        worker = lax.axis_index("subcore") * SC_NUM_CORES + lax.axis_index("core")
        base = worker * per_worker

        @pl.loop(0, per_worker // SC_GATHER_WINDOW)
        def _(w):
            off = base + w * SC_GATHER_WINDOW
            pltpu.sync_copy(idx_hbm.at[pl.ds(off, SC_GATHER_WINDOW)], idx_vmem)
            pltpu.async_copy(table_hbm.at[idx_vmem], rows_vmem, sem).wait()
            pltpu.sync_copy(rows_vmem, out_hbm.at[pl.ds(off, SC_GATHER_WINDOW)])

    return gather_kernel(table, indices)


def _combine_kernel(*refs, alpha):
    y_refs = refs[:TOP_K]
    x_ref, fsh_ref, gate_ref, mod_ref, lng_ref, lnb_ref, o_ref = refs[TOP_K:]
    d = D_MODEL
    t = x_ref.shape[0]
    gate_rows = gate_ref[...]
    pad = jnp.zeros((LANES - TOP_K, t), F32)
    gate_cols = jnp.concatenate([gate_rows, pad], axis=0).T
    f_lo = fsh_ref[:, 0:PACK_W]
    f_hi = fsh_ref[:, PACK_W:]
    for k in range(TOP_K):
        y_lo, y_hi = _unpack_bf16_pairs(pltpu.bitcast(y_refs[k][...], jnp.uint32))
        f_lo = f_lo + gate_cols[:, k:k + 1] * y_lo
        f_hi = f_hi + gate_cols[:, k:k + 1] * y_hi
    f = jnp.concatenate([f_lo, f_hi], axis=1)
    z = alpha * x_ref[...] + mod_ref[:, 5 * d:6 * d] * f
    o_ref[...] = _layer_norm_rows(z) * lng_ref[...] + lnb_ref[...]


def _combine_call(y_tok, x, fsh, gate, mod3, ln_g, ln_b, *, tiles_per_batch, alpha, drop_context):
    r, d = x.shape
    t = ROW_TILE
    nt = r // t
    if drop_context:
        per_batch = tiles_per_batch - 1
        src = lambda i: (i // per_batch) * tiles_per_batch + i % per_batch
        n_tiles = nt // tiles_per_batch * per_batch
    else:
        src = lambda i: i
        n_tiles = nt
    row = lambda i: (src(i), 0)
    col = lambda i: (0, src(i))
    const = lambda i: (0, 0)
    kern = functools.partial(_combine_kernel, alpha=alpha)
    y_specs = [pl.BlockSpec((t, PACK_W), functools.partial(lambda k, i: (k * nt + src(i), 0), k))
               for k in range(TOP_K)]
    return pl.pallas_call(
        kern,
        grid=(n_tiles,),
        in_specs=y_specs + [
            pl.BlockSpec((t, d), row),
            pl.BlockSpec((t, d), row),
            pl.BlockSpec((TOP_K, t), col),
            pl.BlockSpec((None, 1, 6 * d), lambda i: (_mod_row(src(i), tiles_per_batch), 0, 0)),
            pl.BlockSpec((1, d), const),
            pl.BlockSpec((1, d), const),
        ],
        out_specs=pl.BlockSpec((t, d), lambda i: (i, 0)),
        out_shape=jax.ShapeDtypeStruct((n_tiles * t, d), F32),
        compiler_params=_cparams("arbitrary"),
        name="moe_combine",
    )(*([y_tok] * TOP_K), x, fsh, gate, mod3, ln_g, ln_b)


def _rope_tables(seq):
    rows = seq // GRID_W
    row = jnp.repeat(jnp.arange(rows, dtype=F32), GRID_W)
    col = jnp.tile(jnp.arange(GRID_W, dtype=F32), rows)
    nf = DA_DIM // 4
    freqs = ROPE_BASE ** (-jnp.arange(nf, dtype=F32) / nf)
    cr, sr = jnp.cos(row[:, None] * freqs), jnp.sin(row[:, None] * freqs)
    cc, sc = jnp.cos(col[:, None] * freqs), jnp.sin(col[:, None] * freqs)
    c64 = jnp.concatenate([cr, cr, cc, cc], axis=1)
    s64 = jnp.concatenate([-sr, sr, -sc, sc], axis=1)
    c = jnp.concatenate([jnp.tile(c64, (1, 2)), jnp.ones((CTX_LEN, LANES), F32)], axis=0)
    s = jnp.concatenate([jnp.tile(s64, (1, 2)), jnp.zeros((CTX_LEN, LANES), F32)], axis=0)
    return c, s


def kernel(x, c, ctx, c_ctx, w_mod, b_mod, w_in, w_out, diff_lambda, pool_w, pool_scale, ret_log_decay, ln_g, ln_b,
           w_router, router_bias, w_expert_gate_up, w_expert_down, w_shared_gate_up, w_shared_down):
    batch, seq, d = x.shape
    depth = w_mod.shape[0]
    assert d == D_MODEL and ctx.shape[1] == CTX_LEN == ROW_TILE and batch == 2
    assert seq % ROW_TILE == 0 and seq % GRID_W == 0 and w_in.shape[-1] == IN_WIDTH
    rows_per_batch = seq + CTX_LEN
    tiles_per_batch = rows_per_batch // ROW_TILE
    r = batch * rows_per_batch
    alpha = (2.0 * depth) ** 0.25

    xa = jnp.concatenate([x, ctx], axis=1).reshape(r, d)
    cvec = jnp.zeros((8, d), F32).at[0:batch].set(c).at[batch].set(c_ctx)
    mod_all = _mod_call(cvec, w_mod, b_mod)
    rope_c, rope_s = _rope_tables(seq)

    n_sorted = r * TOP_K + N_EXPERTS * EXPERT_BLOCK
    n_blocks = n_sorted // EXPERT_BLOCK

    for l in range(depth):
        lambda_init = 0.8 - 0.6 * math.exp(-0.3 * l)
        mod3 = mod_all[l].reshape(8, 1, 6 * d)
        lng = ln_g[l].reshape(2, 1, d)
        lnb = ln_b[l].reshape(2, 1, d)

        w_in_bf = w_in[l].astype(BF16)
        w_vt_bf = w_in_bf[:, QK_WIDTH:QK_WIDTH + DA_WIDTH].T
        qk, vda, u, rqkv, rg = _inproj_call(xa, mod3, w_in_bf, w_vt_bf, rope_c, rope_s, tiles_per_batch)
        da = _attn_call(diff_lambda[l], qk, vda, batch=batch, rows_per_batch=rows_per_batch, seq=seq,
                        lambda_init=lambda_init)
        o_f, o_b = _ret_call(ret_log_decay[l], rqkv, batch=batch, rows_per_batch=rows_per_batch, seq=seq)
        pool_bd = jnp.zeros((POOL_WIDTH, POOL_WIDTH), F32)
        for gi in range(len(POOL_WINDOWS)):
            sl = slice(gi * POOL_GROUP, (gi + 1) * POOL_GROUP)
            pool_bd = pool_bd.at[sl, sl].set(pool_w[l, gi])
        xa = _mixout_call(xa, da, u, o_f, o_b, rg, mod3, w_out[l].astype(BF16), pool_bd.astype(BF16),
                          pool_scale[l].reshape(1, POOL_WIDTH), lng[0], lnb[0],
                          tiles_per_batch=tiles_per_batch, seq=seq, alpha=alpha)

        wr_t = w_router[l].T
        wr_hi = wr_t.astype(BF16)
        wr_lo = (wr_t - wr_hi.astype(F32)).astype(BF16)
        tokp, idx, gate, rank, cnt, fsh = _router_call(
            xa, mod3, wr_hi, wr_lo, router_bias[l].reshape(N_EXPERTS, 1),
            w_shared_gate_up[l].astype(BF16), w_shared_down[l].astype(BF16), tiles_per_batch=tiles_per_batch)
        counts = cnt[:, 0]
        padded = (counts + EXPERT_BLOCK - 1) // EXPERT_BLOCK * EXPERT_BLOCK
        pad_end = jnp.cumsum(padded)
        offs = pad_end - padded
        expert_ids = jnp.arange(N_EXPERTS, dtype=jnp.int32)
        blk_row = jnp.arange(n_blocks, dtype=jnp.int32) * EXPERT_BLOCK
        block_expert = jnp.minimum(jnp.sum(pad_end[None, :] <= blk_row[:, None], axis=1), N_EXPERTS - 1)
        n_used = pad_end[-1:] // EXPERT_BLOCK
        used = counts > 0
        ordinal = jnp.cumsum(used) - 1
        hit = used[None, :] & (ordinal[None, :] == expert_ids[:, None])
        used_expert = jnp.sum(jnp.where(hit, expert_ids[None, :], 0), axis=1)
        used_last_row = jnp.sum(jnp.where(hit, (pad_end - EXPERT_BLOCK)[None, :], 0), axis=1)
        n_used_experts = jnp.sum(used)[None]
        block_ordinal = ordinal[block_expert]
        i32 = lambda a: a.astype(jnp.int32)

        dest = _dest_call(idx, rank, i32(offs).reshape(N_EXPERTS, 1))
        xs = _dispatch_call(i32(used_last_row), i32(n_used_experts), dest, tokp, n_sorted)
        ys = _expert_call(i32(block_expert), i32(n_used), i32(block_ordinal), i32(used_expert),
                          i32(n_used_experts), xs, w_expert_gate_up, w_expert_down, l)
        y_tok = _sc_gather_rows(lax.bitcast_convert_type(ys, jnp.int32), dest.reshape(TOP_K * r))
        xa = _combine_call(y_tok, xa, fsh, gate, mod3, lng[1], lnb[1], tiles_per_batch=tiles_per_batch,
                           alpha=alpha, drop_context=(l == depth - 1))

    return xa.reshape(batch, seq, d)
```

```python
import functools
import math

import jax
import jax.numpy as jnp
from jax import lax
from jax.experimental import pallas as pl
from jax.experimental.pallas import tpu as pltpu
from jax.experimental.pallas import tpu_sc as plsc

F32 = jnp.float32
BF16 = jnp.bfloat16
HIGHEST = lax.Precision.HIGHEST

D_MODEL = 1024
CTX_LEN = 256
GRID_W = 64
DA_HEADS = 4
DA_DIM = 64
DA_VDIM = 2 * DA_DIM
DA_WIDTH = DA_HEADS * DA_VDIM
ROPE_BASE = 10000.0
POOL_WINDOWS = (2, 4, 8, 16)
POOL_GROUP = 64
POOL_WIDTH = len(POOL_WINDOWS) * POOL_GROUP
POOL_HALO = 8
RET_HEADS = 4
RET_DK = 64
RET_WIDTH = RET_HEADS * RET_DK
RET_CHUNK = 128
QK_WIDTH = 2 * DA_HEADS * 2 * DA_DIM
IN_WIDTH = QK_WIDTH + DA_WIDTH + POOL_WIDTH + 4 * RET_WIDTH
N_EXPERTS = 256
TOP_K = 8
N_GROUPS = 8
GROUP_SIZE = N_EXPERTS // N_GROUPS
TOPK_GROUPS = 4
EXPERT_HIDDEN = 256
ROUTED_SCALE = 2.5
LN_EPS = 1e-6
RMS_EPS = 1e-5

LANES = 128
ROW_TILE = 256
ATTN_Q_TILE = 256
ATTN_K_CHUNK = 256
ATTN_UNROLL = 16
SC_NUM_CORES = 2
SC_NUM_SUBCORES = 16
SC_GATHER_WINDOW = 128
EXPERT_BLOCK = 256
SC_SPARE_ROWS = N_EXPERTS * EXPERT_BLOCK
PACK_W = D_MODEL // 2
VMEM_LIMIT = 56 * 1024 * 1024


def _cparams(*sem):
    return pltpu.CompilerParams(dimension_semantics=sem, vmem_limit_bytes=VMEM_LIMIT)


def _sigmoid(x):
    return 1.0 / (1.0 + jnp.exp(-x))


def _layer_norm_rows(x):
    mu = jnp.mean(x, axis=-1, keepdims=True)
    xc = x - mu
    var = jnp.mean(xc * xc, axis=-1, keepdims=True)
    return xc * lax.rsqrt(var + LN_EPS)


def _pack_bf16_pairs(x):
    half = x.shape[1] // 2
    bits = pltpu.bitcast(x.astype(BF16).astype(F32), jnp.uint32)
    word = lax.shift_right_logical(bits[:, 0:half], jnp.uint32(16)) | (bits[:, half:] & jnp.uint32(0xFFFF0000))
    return pltpu.bitcast(word, jnp.int32)


def _unpack_bf16_pairs(packed):
    word = pltpu.bitcast(packed, jnp.uint32)
    lo = pltpu.bitcast(lax.shift_left(word, jnp.uint32(16)), F32)
    hi = pltpu.bitcast(word & jnp.uint32(0xFFFF0000), F32)
    return lo, hi


def _mod_row(i, tiles_per_batch):
    return jnp.where(i % tiles_per_batch == tiles_per_batch - 1, 2, i // tiles_per_batch)


def _mod_kernel(c_ref, w_ref, b_ref, o_ref):
    c = c_ref[...]
    s = c * _sigmoid(c)
    o_ref[...] = jnp.dot(s, w_ref[...], precision=HIGHEST, preferred_element_type=F32) + b_ref[...]


def _mod_call(cvec, w_mod, b_mod):
    depth, d, n = w_mod.shape
    tn = 1536
    return pl.pallas_call(
        _mod_kernel,
        grid=(depth, n // tn),
        in_specs=[
            pl.BlockSpec((8, d), lambda l, j: (0, 0)),
            pl.BlockSpec((None, d, tn), lambda l, j: (l, 0, j)),
            pl.BlockSpec((None, 1, tn), lambda l, j: (l, 0, j)),
        ],
        out_specs=pl.BlockSpec((None, 8, tn), lambda l, j: (l, 0, j)),
        out_shape=jax.ShapeDtypeStruct((depth, 8, n), F32),
        compiler_params=_cparams("arbitrary", "arbitrary"),
        name="mod",
    )(cvec, w_mod, b_mod.reshape(depth, 1, n))


def _inproj_kernel(x_ref, mod_ref, w_ref, wvt_ref, ct_ref, st_ref, qk_ref, vt_ref, u_ref, r_ref, g_ref):
    d = D_MODEL
    xn = _layer_norm_rows(x_ref[...])
    h = (xn * (1.0 + mod_ref[:, d:2 * d]) + mod_ref[:, 0:d]).astype(BF16)

    a = jnp.dot(h, w_ref[:, 0:QK_WIDTH], preferred_element_type=F32)
    lane = lax.broadcasted_iota(jnp.int32, (a.shape[0], LANES), 1)
    first_half = (lane % 32) < 16
    ct = ct_ref[...]
    st = st_ref[...]
    for s in range(QK_WIDTH // LANES):
        blk = a[:, s * LANES:(s + 1) * LANES]
        partner = jnp.where(first_half, pltpu.roll(blk, LANES - 16, 1), pltpu.roll(blk, 16, 1))
        rot = blk * ct + partner * st
        if s < QK_WIDTH // LANES // 2:
            rot = rot * (DA_DIM ** -0.5 * math.log2(math.e))
        qk_ref[:, s * LANES:(s + 1) * LANES] = rot.astype(BF16)

    vt_ref[...] = lax.dot_general(wvt_ref[...], h, (((1,), (1,)), ((), ())),
                                  preferred_element_type=F32).astype(BF16)
    o = QK_WIDTH + DA_WIDTH
    u_ref[...] = jnp.dot(h, w_ref[:, o:o + POOL_WIDTH], preferred_element_type=F32)
    o += POOL_WIDTH
    r = jnp.dot(h, w_ref[:, o:o + 3 * RET_WIDTH], preferred_element_type=F32)
    r_ref[:, 0:RET_WIDTH] = r[:, 0:RET_WIDTH].astype(BF16)
    r_ref[:, RET_WIDTH:2 * RET_WIDTH] = (r[:, RET_WIDTH:2 * RET_WIDTH] * (RET_DK ** -0.5)).astype(BF16)
    r_ref[:, 2 * RET_WIDTH:] = r[:, 2 * RET_WIDTH:].astype(BF16)
    o += 3 * RET_WIDTH
    g_ref[...] = jnp.dot(h, w_ref[:, o:o + RET_WIDTH], preferred_element_type=F32)


def _inproj_call(x, mod3, w_in_bf, w_vt_bf, rope_c, rope_s, tiles_per_batch):
    r, d = x.shape
    t = ROW_TILE
    nt = r // t
    row = lambda i: (i, 0)
    return pl.pallas_call(
        _inproj_kernel,
        grid=(nt,),
        in_specs=[
            pl.BlockSpec((t, d), row),
            pl.BlockSpec((None, 1, 6 * d), lambda i: (_mod_row(i, tiles_per_batch), 0, 0)),
            pl.BlockSpec((d, IN_WIDTH), lambda i: (0, 0)),
            pl.BlockSpec((DA_WIDTH, d), lambda i: (0, 0)),
            pl.BlockSpec((t, LANES), lambda i: (i % tiles_per_batch, 0)),
            pl.BlockSpec((t, LANES), lambda i: (i % tiles_per_batch, 0)),
        ],
        out_specs=[
            pl.BlockSpec((t, QK_WIDTH), row),
            pl.BlockSpec((DA_WIDTH, t), lambda i: (0, i)),
            pl.BlockSpec((t, POOL_WIDTH), row),
            pl.BlockSpec((t, 3 * RET_WIDTH), row),
            pl.BlockSpec((t, RET_WIDTH), row),
        ],
        out_shape=[
            jax.ShapeDtypeStruct((r, QK_WIDTH), BF16),
            jax.ShapeDtypeStruct((DA_WIDTH, r), BF16),
            jax.ShapeDtypeStruct((r, POOL_WIDTH), F32),
            jax.ShapeDtypeStruct((r, 3 * RET_WIDTH), BF16),
            jax.ShapeDtypeStruct((r, RET_WIDTH), F32),
        ],
        compiler_params=_cparams("arbitrary"),
        name="inproj",
    )(x, mod3, w_in_bf, w_vt_bf, rope_c, rope_s)


def _attn_kernel(lam_ref, q_ref, k_ref, vt_ref, o_ref, s_ref, *, k_chunk, seq, lambda_init):
    q = q_ref[...]
    mq = q.shape[0]
    lane = lax.broadcasted_iota(jnp.int32, q.shape, 1)
    zero = jnp.zeros_like(q)
    q2 = jnp.concatenate([jnp.where(lane < DA_DIM, q, zero), jnp.where(lane >= DA_DIM, q, zero)], axis=0)
    qt = q2.astype(F32).T.astype(BF16)

    n_chunks = (seq + CTX_LEN) // k_chunk
    last = n_chunks - 1
    is_ctx_tile = pl.program_id(2) == pl.num_programs(2) - 1
    n_iters = jnp.where(is_ctx_tile, 0, last // ATTN_UNROLL)

    def score_chunk(c, m):
        off = pl.multiple_of(c * k_chunk, k_chunk)
        s = jnp.dot(k_ref[pl.ds(off, k_chunk), :], qt, preferred_element_type=F32)
        s_ref[c] = s
        return jnp.maximum(m, jnp.max(s, axis=0, keepdims=True))

    def pass1(it, m):
        for u in range(ATTN_UNROLL):
            m = score_chunk(it * ATTN_UNROLL + u, m)
        return m

    m = lax.fori_loop(0, n_iters, pass1, jnp.full((1, 2 * mq), -jnp.inf, F32))
    m = score_chunk(last, m)

    ones_rows = jnp.where(lax.broadcasted_iota(jnp.int32, (16, k_chunk), 0) == 0, 1.0, 0.0).astype(BF16)

    def value_chunk(c, acc):
        off = pl.multiple_of(c * k_chunk, k_chunk)
        vt = jnp.concatenate([vt_ref[:, pl.ds(off, k_chunk)], ones_rows], axis=0)
        p = jnp.exp2((s_ref[c] - m).astype(BF16))
        return acc + jnp.dot(vt, p, preferred_element_type=F32)

    def pass2(it, acc):
        for u in range(ATTN_UNROLL):
            acc = value_chunk(it * ATTN_UNROLL + u, acc)
        return acc

    acc = lax.fori_loop(0, n_iters, pass2, jnp.zeros((DA_VDIM + 16, 2 * mq), F32))
    acc = value_chunk(last, acc)
    l0, l1 = acc[DA_VDIM:DA_VDIM + 1, 0:mq], acc[DA_VDIM:DA_VDIM + 1, mq:]
    a0, a1 = acc[0:DA_VDIM, 0:mq], acc[0:DA_VDIM, mq:]

    lv = lam_ref[...]
    lam = (jnp.exp(jnp.sum(lv[0:1] * lv[1:2], axis=-1, keepdims=True))
           - jnp.exp(jnp.sum(lv[2:3] * lv[3:4], axis=-1, keepdims=True)) + lambda_init)
    o = a0 / l0 - lam * (a1 / l1)
    o = o * lax.rsqrt(jnp.mean(o * o, axis=0, keepdims=True) + RMS_EPS) * (1.0 - lambda_init)
    o_ref[...] = o.T.astype(BF16)


def _attn_call(lam_vec, qk, vda, *, batch, rows_per_batch, seq, lambda_init):
    tq = ATTN_Q_TILE
    assert seq % (ATTN_K_CHUNK * ATTN_UNROLL) == 0 and rows_per_batch - seq == CTX_LEN == tq == ATTN_K_CHUNK
    nq = rows_per_batch // tq
    kern = functools.partial(_attn_kernel, k_chunk=ATTN_K_CHUNK, seq=seq, lambda_init=lambda_init)
    return pl.pallas_call(
        kern,
        grid=(batch, DA_HEADS, nq),
        in_specs=[
            pl.BlockSpec((4, DA_DIM), lambda b, h, i: (0, 0)),
            pl.BlockSpec((tq, DA_VDIM), lambda b, h, i: (b * nq + i, h)),
            pl.BlockSpec((rows_per_batch, DA_VDIM), lambda b, h, i: (b, DA_HEADS + h)),
            pl.BlockSpec((DA_VDIM, rows_per_batch), lambda b, h, i: (h, b)),
        ],
        out_specs=pl.BlockSpec((tq, DA_VDIM), lambda b, h, i: (b * nq + i, h)),
        out_shape=jax.ShapeDtypeStruct((qk.shape[0], DA_WIDTH), BF16),
        scratch_shapes=[pltpu.VMEM((rows_per_batch // ATTN_K_CHUNK, ATTN_K_CHUNK, 2 * tq), F32)],
        compiler_params=_cparams("arbitrary", "arbitrary", "arbitrary"),
        name="diff_attn",
    )(lam_vec, qk, qk, vda)


def _ret_kernel(ld_ref, f_ref, b_ref, of_ref, ob_ref, dm_ref, qd_ref, kd_ref, cd_ref, st_ref):
    c = pl.program_id(1)
    ch = RET_CHUNK
    w = RET_WIDTH
    lane_head = lax.broadcasted_iota(jnp.int32, (1, w), 1) // RET_DK

    @pl.when(c == 0)
    def _():
        st_ref[...] = jnp.zeros_like(st_ref)
        ri = lax.broadcasted_iota(jnp.int32, (ch, ch), 0)
        ci = lax.broadcasted_iota(jnp.int32, (ch, ch), 1)
        rowf = lax.broadcasted_iota(jnp.int32, (ch, w), 0).astype(F32)
        for d in range(2):
            lg_lane = jnp.zeros((1, w), F32)
            for hh in range(RET_HEADS):
                lg = -jnp.exp(jnp.full((1, 1), ld_ref[d, hh], F32))
                lg_lane = jnp.where(lane_head == hh, lg, lg_lane)
                dist = ((ri - ci) if d == 0 else (ci - ri)).astype(F32)
                dm_ref[d, hh] = jnp.where(dist >= 0, jnp.exp(dist * lg), 0.0)
            if d == 0:
                qd_ref[d] = jnp.exp((rowf + 1.0) * lg_lane)
                kd_ref[d] = jnp.exp((ch - 1.0 - rowf) * lg_lane)
            else:
                qd_ref[d] = jnp.exp((ch - rowf) * lg_lane)
                kd_ref[d] = jnp.exp(rowf * lg_lane)
            cd_ref[d] = jnp.exp(float(ch) * lg_lane)

    rblk = lax.broadcasted_iota(jnp.int32, (w, w), 0) // RET_DK
    cblk = lax.broadcasted_iota(jnp.int32, (w, w), 1) // RET_DK
    for d, (src, dst) in enumerate(((f_ref, of_ref), (b_ref, ob_ref))):
        q = src[:, 0:w]
        k = src[:, w:2 * w]
        v = src[:, 2 * w:3 * w]
        st = st_ref[d]
        o = jnp.dot((q.astype(F32) * qd_ref[d]).astype(BF16), st.astype(BF16), preferred_element_type=F32)
        for hh in range(RET_HEADS):
            in_head = lane_head == hh
            qm = jnp.where(in_head, q, jnp.zeros_like(q))
            s = lax.dot_general(qm, k, (((1,), (1,)), ((), ())), preferred_element_type=F32)
            intra = (s * dm_ref[d, hh]).astype(BF16)
            o = o + jnp.where(in_head, jnp.dot(intra, v, preferred_element_type=F32), 0.0)
        dst[...] = o
        kk_t = (k.astype(F32) * kd_ref[d]).T.astype(BF16)
        upd = jnp.dot(kk_t, v, preferred_element_type=F32)
        st_ref[d] = jnp.where(rblk == cblk, st * cd_ref[d] + upd, 0.0)


def _ret_call(log_decay, rqkv, *, batch, rows_per_batch, seq):
    ch = RET_CHUNK
    nc = rows_per_batch // ch
    n_lat = seq // ch
    n_ctx = nc - n_lat

    def fwd(b, c):
        return (b * nc + jnp.where(c < n_ctx, n_lat + c, c - n_ctx), 0)

    def bwd(b, c):
        return (b * nc + nc - 1 - c, 0)

    w = RET_WIDTH
    return pl.pallas_call(
        _ret_kernel,
        grid=(batch, nc),
        in_specs=[
            pl.BlockSpec(memory_space=pltpu.SMEM),
            pl.BlockSpec((ch, 3 * w), fwd),
            pl.BlockSpec((ch, 3 * w), bwd),
        ],
        out_specs=[pl.BlockSpec((ch, w), fwd), pl.BlockSpec((ch, w), bwd)],
        out_shape=[jax.ShapeDtypeStruct((rqkv.shape[0], w), F32)] * 2,
        scratch_shapes=[
            pltpu.VMEM((2, RET_HEADS, ch, ch), F32),
            pltpu.VMEM((2, ch, w), F32),
            pltpu.VMEM((2, ch, w), F32),
            pltpu.VMEM((2, 1, w), F32),
            pltpu.VMEM((2, w, w), F32),
        ],
        compiler_params=_cparams("arbitrary", "arbitrary"),
        name="retention",
    )(log_decay, rqkv, rqkv)


def _mixout_kernel(x_ref, da_ref, u_ref, up_ref, un_ref, of_ref, ob_ref, rg_ref, mod_ref, wo_ref, pw_ref,
                   ps_ref, lng_ref, lnb_ref, o_ref, *, tiles_per_batch, seq, alpha):
    d = D_MODEL
    t = x_ref.shape[0]
    i = pl.program_id(0)
    j = i % tiles_per_batch
    is_ctx = j == tiles_per_batch - 1
    stream_len = jnp.where(is_ctx, CTX_LEN, seq)
    p0 = jnp.where(is_ctx, 0, j * t)

    u = u_ref[...]
    prev = jnp.where(p0 > 0, up_ref[...], 0.0)
    nxt = jnp.where(p0 + t < stream_len, un_ref[...], 0.0)
    ext = jnp.concatenate([prev, u, nxt], axis=0)
    n = t + 2 * POOL_HALO
    a2 = ext + pltpu.roll(ext, 1, 0)
    a4 = pltpu.roll(a2, 1, 0) + pltpu.roll(a2, n - 1, 0)
    a8 = pltpu.roll(a4, 2, 0) + pltpu.roll(a4, n - 2, 0)
    a16 = pltpu.roll(a8, 4, 0) + pltpu.roll(a8, n - 4, 0)
    pos = p0 + lax.broadcasted_iota(jnp.int32, (t, POOL_WIDTH), 0)
    group = lax.broadcasted_iota(jnp.int32, (1, POOL_WIDTH), 1) // POOL_GROUP
    mean = jnp.zeros((t, POOL_WIDTH), F32)
    for gi, (wnd, asum) in enumerate(zip(POOL_WINDOWS, (a2, a4, a8, a16))):
        cnt = jnp.minimum(pos + wnd // 2, stream_len) - jnp.maximum(pos - wnd // 2, 0)
        mean = jnp.where(group == gi, asum[POOL_HALO:POOL_HALO + t] / cnt.astype(F32), mean)
    pool = jnp.dot((mean - u).astype(BF16), pw_ref[...], preferred_element_type=F32) * ps_ref[...]

    o = of_ref[...] + ob_ref[...]
    head = lax.broadcasted_iota(jnp.int32, (1, RET_WIDTH), 1) // RET_DK

    def head_mean(val):
        out = jnp.zeros_like(val)
        for hh in range(RET_HEADS):
            m = jnp.sum(jnp.where(head == hh, val, 0.0), axis=-1, keepdims=True) * (1.0 / RET_DK)
            out = jnp.where(head == hh, m, out)
        return out

    oc = o - head_mean(o)
    rn = oc * lax.rsqrt(head_mean(oc * oc) + LN_EPS)
    g = rg_ref[...]
    ret = rn * (g * _sigmoid(g))

    y = jnp.dot(da_ref[...], wo_ref[0:DA_WIDTH, :], preferred_element_type=F32)
    y = y + jnp.dot(pool.astype(BF16), wo_ref[DA_WIDTH:DA_WIDTH + POOL_WIDTH, :], preferred_element_type=F32)
    y = y + jnp.dot(ret.astype(BF16), wo_ref[DA_WIDTH + POOL_WIDTH:, :], preferred_element_type=F32)
    z = alpha * x_ref[...] + mod_ref[:, 2 * d:3 * d] * y
    o_ref[...] = _layer_norm_rows(z) * lng_ref[...] + lnb_ref[...]


def _mixout_call(x, da, u, o_f, o_b, rg, mod3, w_out_bf, pool_bd, pool_scale, ln_g, ln_b, *, tiles_per_batch, seq,
                 alpha):
    r, d = x.shape
    t = ROW_TILE
    nt = r // t
    hb = t // POOL_HALO
    n_halo_blocks = r // POOL_HALO
    row = lambda i: (i, 0)
    const = lambda i: (0, 0)
    kern = functools.partial(_mixout_kernel, tiles_per_batch=tiles_per_batch, seq=seq, alpha=alpha)
    return pl.pallas_call(
        kern,
        grid=(nt,),
        in_specs=[
            pl.BlockSpec((t, d), row),
            pl.BlockSpec((t, DA_WIDTH), row),
            pl.BlockSpec((t, POOL_WIDTH), row),
            pl.BlockSpec((POOL_HALO, POOL_WIDTH), lambda i: (jnp.maximum(i * hb - 1, 0), 0)),
            pl.BlockSpec((POOL_HALO, POOL_WIDTH), lambda i: (jnp.minimum((i + 1) * hb, n_halo_blocks - 1), 0)),
            pl.BlockSpec((t, RET_WIDTH), row),
            pl.BlockSpec((t, RET_WIDTH), row),
            pl.BlockSpec((t, RET_WIDTH), row),
            pl.BlockSpec((None, 1, 6 * d), lambda i: (_mod_row(i, tiles_per_batch), 0, 0)),
            pl.BlockSpec((d, d), const),
            pl.BlockSpec((POOL_WIDTH, POOL_WIDTH), const),
            pl.BlockSpec((1, POOL_WIDTH), const),
            pl.BlockSpec((1, d), const),
            pl.BlockSpec((1, d), const),
        ],
        out_specs=pl.BlockSpec((t, d), row),
        out_shape=jax.ShapeDtypeStruct((r, d), F32),
        compiler_params=_cparams("arbitrary"),
        name="mixer_out",
    )(x, da, u, u, u, o_f, o_b, rg, mod3, w_out_bf, pool_bd, pool_scale, ln_g, ln_b)


def _router_kernel(x_ref, mod_ref, wrh_ref, wrl_ref, bias_ref, wsgu_ref, wsdn_ref,
                   tokp_ref, idx_ref, gate_ref, rank_ref, cnt_ref, fsh_ref, carry_ref):
    d = D_MODEL
    t = x_ref.shape[0]
    ne = N_EXPERTS
    neg = -jnp.inf

    @pl.when(pl.program_id(0) == 0)
    def _():
        carry_ref[...] = jnp.zeros_like(carry_ref)

    tok = _layer_norm_rows(x_ref[...]) * (1.0 + mod_ref[:, 4 * d:5 * d]) + mod_ref[:, 3 * d:4 * d]
    tok_hi = tok.astype(BF16)
    tok_lo = (tok - tok_hi.astype(F32)).astype(BF16)

    tokp_ref[...] = _pack_bf16_pairs(tok)

    hs = jnp.dot(tok_hi, wsgu_ref[...], preferred_element_type=F32)
    gs, us = hs[:, 0:EXPERT_HIDDEN], hs[:, EXPERT_HIDDEN:]
    fsh_ref[...] = jnp.dot((gs * _sigmoid(gs) * us).astype(BF16), wsdn_ref[...], preferred_element_type=F32)

    nt_dims = (((1,), (1,)), ((), ()))
    logits = (lax.dot_general(wrh_ref[...], tok_hi, nt_dims, preferred_element_type=F32)
              + lax.dot_general(wrh_ref[...], tok_lo, nt_dims, preferred_element_type=F32)
              + lax.dot_general(wrl_ref[...], tok_hi, nt_dims, preferred_element_type=F32))
    scores = _sigmoid(logits)
    biased = scores + bias_ref[...]

    gidx = lax.broadcasted_iota(jnp.int32, (GROUP_SIZE, t), 0)
    blocks, gscores = [], []
    for g in range(N_GROUPS):
        blk = biased[g * GROUP_SIZE:(g + 1) * GROUP_SIZE, :]
        m1 = jnp.max(blk, axis=0, keepdims=True)
        first = jnp.min(jnp.where(blk == m1, gidx, GROUP_SIZE), axis=0, keepdims=True)
        m2 = jnp.max(jnp.where(gidx == first, neg, blk), axis=0, keepdims=True)
        blocks.append(blk)
        gscores.append(m1 + m2)

    keep = [jnp.zeros((1, t), F32) for _ in range(N_GROUPS)]
    for _ in range(TOPK_GROUPS):
        m = gscores[0]
        for gs_ in gscores[1:]:
            m = jnp.maximum(m, gs_)
        found = jnp.zeros((1, t), F32)
        for g in range(N_GROUPS):
            hit = jnp.where(gscores[g] == m, 1.0 - found, 0.0)
            found = found + hit
            keep[g] = keep[g] + hit
            gscores[g] = jnp.where(hit > 0.0, neg, gscores[g])
    masked = jnp.concatenate([jnp.where(keep[g] > 0.0, blocks[g], neg) for g in range(N_GROUPS)], axis=0)

    ei = lax.broadcasted_iota(jnp.int32, (ne, t), 0)
    cur = masked
    onehot = jnp.zeros((ne, t), F32)
    idxs, gates = [], []
    for _ in range(TOP_K):
        m = jnp.max(cur, axis=0, keepdims=True)
        ii = jnp.min(jnp.where(cur == m, ei, ne), axis=0, keepdims=True)
        sel = ei == ii
        idxs.append(ii)
        gates.append(jnp.sum(jnp.where(sel, scores, 0.0), axis=0, keepdims=True))
        onehot = jnp.where(sel, 1.0, onehot)
        cur = jnp.where(sel, neg, cur)
    gsum = gates[0]
    for gk in gates[1:]:
        gsum = gsum + gk
    for k in range(TOP_K):
        idx_ref[k:k + 1, :] = idxs[k]
        gate_ref[k:k + 1, :] = gates[k] / gsum * ROUTED_SCALE

    ti = lax.broadcasted_iota(jnp.int32, (t, t), 0)
    tj = lax.broadcasted_iota(jnp.int32, (t, t), 1)
    before = jnp.where(ti < tj, 1.0, 0.0).astype(BF16)
    prefix = jnp.dot(onehot.astype(BF16), before, preferred_element_type=F32) + carry_ref[:, 0:1]
    for k in range(TOP_K):
        rank_k = jnp.sum(jnp.where(ei == idxs[k], prefix, 0.0), axis=0, keepdims=True)
        rank_ref[k:k + 1, :] = rank_k.astype(jnp.int32)
    carry_ref[...] = carry_ref[...] + jnp.sum(onehot, axis=1, keepdims=True)
    cnt_ref[...] = carry_ref[...].astype(jnp.int32)


def _router_call(x, mod3, wr_hi, wr_lo, bias_col, ws_gu_bf, ws_dn_bf, *, tiles_per_batch):
    r, d = x.shape
    t = ROW_TILE
    nt = r // t
    row = lambda i: (i, 0)
    col = lambda i: (0, i)
    const = lambda i: (0, 0)
    return pl.pallas_call(
        _router_kernel,
        grid=(nt,),
        in_specs=[
            pl.BlockSpec((t, d), row),
            pl.BlockSpec((None, 1, 6 * d), lambda i: (_mod_row(i, tiles_per_batch), 0, 0)),
            pl.BlockSpec((N_EXPERTS, d), const),
            pl.BlockSpec((N_EXPERTS, d), const),
            pl.BlockSpec((N_EXPERTS, 1), const),
            pl.BlockSpec((d, 2 * EXPERT_HIDDEN), const),
            pl.BlockSpec((EXPERT_HIDDEN, d), const),
        ],
        out_specs=[
            pl.BlockSpec((t, PACK_W), row),
            pl.BlockSpec((TOP_K, t), col),
            pl.BlockSpec((TOP_K, t), col),
            pl.BlockSpec((TOP_K, t), col),
            pl.BlockSpec((N_EXPERTS, LANES), const),
            pl.BlockSpec((t, d), row),
        ],
        out_shape=[
            jax.ShapeDtypeStruct((r, PACK_W), jnp.int32),
            jax.ShapeDtypeStruct((TOP_K, r), jnp.int32),
            jax.ShapeDtypeStruct((TOP_K, r), F32),
            jax.ShapeDtypeStruct((TOP_K, r), jnp.int32),
            jax.ShapeDtypeStruct((N_EXPERTS, LANES), jnp.int32),
            jax.ShapeDtypeStruct((r, d), F32),
        ],
        scratch_shapes=[pltpu.VMEM((N_EXPERTS, LANES), F32)],
        compiler_params=_cparams("arbitrary"),
        name="router",
    )(x, mod3, wr_hi, wr_lo, bias_col, ws_gu_bf, ws_dn_bf)


def _dest_kernel(idx_ref, rank_ref, offs_ref, dest_ref):
    t = idx_ref.shape[1]
    ei = lax.broadcasted_iota(jnp.int32, (N_EXPERTS, t), 0)
    offs = offs_ref[...].astype(F32)
    for k in range(TOP_K):
        start = jnp.sum(jnp.where(ei == idx_ref[k:k + 1, :], offs, 0.0), axis=0, keepdims=True)
        dest_ref[k:k + 1, :] = start.astype(jnp.int32) + rank_ref[k:k + 1, :]


def _dest_call(idx, rank, offs_col):
    r = idx.shape[1]
    t = ROW_TILE
    col = lambda i: (0, i)
    return pl.pallas_call(
        _dest_kernel,
        grid=(r // t,),
        in_specs=[pl.BlockSpec((TOP_K, t), col), pl.BlockSpec((TOP_K, t), col),
                  pl.BlockSpec((N_EXPERTS, 1), lambda i: (0, 0))],
        out_specs=pl.BlockSpec((TOP_K, t), col),
        out_shape=jax.ShapeDtypeStruct((TOP_K, r), jnp.int32),
        compiler_params=_cparams("arbitrary"),
        name="moe_dest",
    )(idx, rank, offs_col)


def _sc_dispatch(tokp, dest_flat, pad_rows, n_sorted):
    r, width = tokp.shape
    win = SC_GATHER_WINDOW
    workers = SC_NUM_CORES * SC_NUM_SUBCORES
    token_windows = r // win
    n_items = dest_flat.shape[0] // win
    n_pad_windows = pad_rows.shape[0] // win
    assert r % win == 0 and n_items % workers == 0 and n_pad_windows % workers == 0
    items_per_worker = n_items // workers
    pads_per_worker = n_pad_windows // workers
    mesh = plsc.VectorSubcoreMesh(core_axis_name="core", subcore_axis_name="subcore", num_cores=SC_NUM_CORES,
                                  num_subcores=SC_NUM_SUBCORES)
    zero_rows = jnp.zeros((win, width), tokp.dtype)

    @functools.partial(
        pl.kernel, out_type=jax.ShapeDtypeStruct((n_sorted + SC_SPARE_ROWS, width), tokp.dtype), mesh=mesh,
        scratch_types=[pltpu.VMEM((win,), jnp.int32), pltpu.VMEM((win, width), tokp.dtype),
                       pltpu.SemaphoreType.DMA],
        name="moe_sc_dispatch")
    def dispatch_kernel(tok_hbm, dest_hbm, pad_hbm, zero_hbm, xs_hbm, idx_vmem, rows_vmem, sem):
        worker = lax.axis_index("subcore") * SC_NUM_CORES + lax.axis_index("core")

        @pl.loop(0, items_per_worker)
        def _(j):
            item = worker * items_per_worker + j
            tok0 = lax.rem(item, token_windows) * win
            pltpu.sync_copy(dest_hbm.at[pl.ds(item * win, win)], idx_vmem)
            pltpu.sync_copy(tok_hbm.at[pl.ds(tok0, win)], rows_vmem)
            pltpu.async_copy(rows_vmem, xs_hbm.at[idx_vmem], sem).wait()

        pltpu.sync_copy(zero_hbm, rows_vmem)

        @pl.loop(0, pads_per_worker)
        def _(j):
            off = (worker * pads_per_worker + j) * win
            pltpu.sync_copy(pad_hbm.at[pl.ds(off, win)], idx_vmem)
            pltpu.async_copy(rows_vmem, xs_hbm.at[idx_vmem], sem).wait()

    return dispatch_kernel(tokp, dest_flat, pad_rows, zero_rows)


def _expert_kernel(be_ref, nb_ref, ord_ref, ue_ref, nue_ref, xs_ref, wgu_hbm, wdn_hbm, ys_ref, wgu_f32, wdn_f32,
                   wgu_bf, wdn_bf, sems, *, layer):
    j = pl.program_id(0)

    def weight_copies(o):
        slot = o % 2
        e = ue_ref[o]
        return (pltpu.make_async_copy(wgu_hbm.at[layer, e], wgu_f32.at[slot], sems.at[0, slot]),
                pltpu.make_async_copy(wdn_hbm.at[layer, e], wdn_f32.at[slot], sems.at[1, slot]))

    def start_weights(o):
        @pl.when(o < nue_ref[0])
        def _():
            for cp in weight_copies(o):
                cp.start()

    @pl.when(j < nb_ref[0])
    def _():
        o = ord_ref[j]
        changed = jnp.logical_or(j == 0, be_ref[j] != be_ref[jnp.maximum(j - 1, 0)])

        @pl.when(j == 0)
        def _():
            start_weights(0)
            start_weights(1)

        @pl.when(changed)
        def _():
            for cp in weight_copies(o):
                cp.wait()
            slot = o % 2
            wgu_bf[...] = wgu_f32[slot].astype(BF16)
            wdn_bf[...] = wdn_f32[slot].astype(BF16)
            start_weights(o + 2)

        x_lo, x_hi = _unpack_bf16_pairs(xs_ref[...])
        h = (jnp.dot(x_lo.astype(BF16), wgu_bf[0:PACK_W, :], preferred_element_type=F32)
             + jnp.dot(x_hi.astype(BF16), wgu_bf[PACK_W:, :], preferred_element_type=F32))
        g, u = h[:, 0:EXPERT_HIDDEN], h[:, EXPERT_HIDDEN:]
        y = jnp.dot((g * _sigmoid(g) * u).astype(BF16), wdn_bf[...], preferred_element_type=F32)
        ys_ref[...] = _pack_bf16_pairs(y)


def _expert_call(block_expert, n_blocks_used, block_ordinal, used_expert, n_used_experts, xs, w_gu, w_dn, layer):
    n_rows = block_expert.shape[0] * EXPERT_BLOCK
    bm = EXPERT_BLOCK
    d = D_MODEL
    used_block = lambda j, be, nb, od, ue, nue: (jnp.minimum(j, nb[0] - 1), 0)
    grid_spec = pltpu.PrefetchScalarGridSpec(
        num_scalar_prefetch=5,
        grid=(n_rows // bm,),
        in_specs=[
            pl.BlockSpec((bm, PACK_W), used_block),
            pl.BlockSpec(memory_space=pl.ANY),
            pl.BlockSpec(memory_space=pl.ANY),
        ],
        out_specs=pl.BlockSpec((bm, PACK_W), used_block),
        scratch_shapes=[
            pltpu.VMEM((2, d, 2 * EXPERT_HIDDEN), F32),
            pltpu.VMEM((2, EXPERT_HIDDEN, d), F32),
            pltpu.VMEM((d, 2 * EXPERT_HIDDEN), BF16),
            pltpu.VMEM((EXPERT_HIDDEN, d), BF16),
            pltpu.SemaphoreType.DMA((2, 2)),
        ],
    )
    return pl.pallas_call(
        functools.partial(_expert_kernel, layer=layer),
        grid_spec=grid_spec,
        out_shape=jax.ShapeDtypeStruct((n_rows, PACK_W), jnp.int32),
        compiler_params=_cparams("arbitrary"),
        name="moe_experts",
    )(block_expert, n_blocks_used, block_ordinal, used_expert, n_used_experts, xs, w_gu, w_dn)


def _sc_gather_rows(table, indices):
    n = indices.shape[0]
    width = table.shape[1]
    workers = SC_NUM_CORES * SC_NUM_SUBCORES
    assert n % (SC_GATHER_WINDOW * workers) == 0
    per_worker = n // workers
    mesh = plsc.VectorSubcoreMesh(core_axis_name="core", subcore_axis_name="subcore", num_cores=SC_NUM_CORES,
                                  num_subcores=SC_NUM_SUBCORES)

    @functools.partial(
        pl.kernel, out_type=jax.ShapeDtypeStruct((n, width), table.dtype), mesh=mesh,
        scratch_types=[pltpu.VMEM((SC_GATHER_WINDOW,), jnp.int32),
                       pltpu.VMEM((SC_GATHER_WINDOW, width), table.dtype),
                       pltpu.SemaphoreType.DMA],
        name="moe_sc_gather")
    def gather_kernel(table_hbm, idx_hbm, out_hbm, idx_vmem, rows_vmem, sem):
        worker = lax.axis_index("subcore") * SC_NUM_CORES + lax.axis_index("core")
        base = worker * per_worker

        @pl.loop(0, per_worker // SC_GATHER_WINDOW)
        def _(w):
            off = base + w * SC_GATHER_WINDOW
            pltpu.sync_copy(idx_hbm.at[pl.ds(off, SC_GATHER_WINDOW)], idx_vmem)
            pltpu.async_copy(table_hbm.at[idx_vmem], rows_vmem, sem).wait()
            pltpu.sync_copy(rows_vmem, out_hbm.at[pl.ds(off, SC_GATHER_WINDOW)])

    return gather_kernel(table, indices)


def _combine_kernel(*refs, alpha):
    y_refs = refs[:TOP_K]
    x_ref, fsh_ref, gate_ref, mod_ref, lng_ref, lnb_ref, o_ref = refs[TOP_K:]
    d = D_MODEL
    t = x_ref.shape[0]
    gate_rows = gate_ref[...]
    pad = jnp.zeros((LANES - TOP_K, t), F32)
    gate_cols = jnp.concatenate([gate_rows, pad], axis=0).T
    f_lo = fsh_ref[:, 0:PACK_W]
    f_hi = fsh_ref[:, PACK_W:]
    for k in range(TOP_K):
        y_lo, y_hi = _unpack_bf16_pairs(y_refs[k][...])
        f_lo = f_lo + gate_cols[:, k:k + 1] * y_lo
        f_hi = f_hi + gate_cols[:, k:k + 1] * y_hi
    f = jnp.concatenate([f_lo, f_hi], axis=1)
    z = alpha * x_ref[...] + mod_ref[:, 5 * d:6 * d] * f
    o_ref[...] = _layer_norm_rows(z) * lng_ref[...] + lnb_ref[...]


def _combine_call(y_tok, x, fsh, gate, mod3, ln_g, ln_b, *, tiles_per_batch, alpha, drop_context):
    r, d = x.shape
    t = ROW_TILE
    nt = r // t
    if drop_context:
        per_batch = tiles_per_batch - 1
        src = lambda i: (i // per_batch) * tiles_per_batch + i % per_batch
        n_tiles = nt // tiles_per_batch * per_batch
    else:
        src = lambda i: i
        n_tiles = nt
    row = lambda i: (src(i), 0)
    col = lambda i: (0, src(i))
    const = lambda i: (0, 0)
    kern = functools.partial(_combine_kernel, alpha=alpha)
    y_specs = [pl.BlockSpec((t, PACK_W), functools.partial(lambda k, i: (k * nt + src(i), 0), k))
               for k in range(TOP_K)]
    return pl.pallas_call(
        kern,
        grid=(n_tiles,),
        in_specs=y_specs + [
            pl.BlockSpec((t, d), row),
            pl.BlockSpec((t, d), row),
            pl.BlockSpec((TOP_K, t), col),
            pl.BlockSpec((None, 1, 6 * d), lambda i: (_mod_row(src(i), tiles_per_batch), 0, 0)),
            pl.BlockSpec((1, d), const),
            pl.BlockSpec((1, d), const),
        ],
        out_specs=pl.BlockSpec((t, d), lambda i: (i, 0)),
        out_shape=jax.ShapeDtypeStruct((n_tiles * t, d), F32),
        compiler_params=_cparams("arbitrary"),
        name="moe_combine",
    )(*([y_tok] * TOP_K), x, fsh, gate, mod3, ln_g, ln_b)


def _rope_tables(seq):
    rows = seq // GRID_W
    row = jnp.repeat(jnp.arange(rows, dtype=F32), GRID_W)
    col = jnp.tile(jnp.arange(GRID_W, dtype=F32), rows)
    nf = DA_DIM // 4
    freqs = ROPE_BASE ** (-jnp.arange(nf, dtype=F32) / nf)
    cr, sr = jnp.cos(row[:, None] * freqs), jnp.sin(row[:, None] * freqs)
    cc, sc = jnp.cos(col[:, None] * freqs), jnp.sin(col[:, None] * freqs)
    c64 = jnp.concatenate([cr, cr, cc, cc], axis=1)
    s64 = jnp.concatenate([-sr, sr, -sc, sc], axis=1)
    c = jnp.concatenate([jnp.tile(c64, (1, 2)), jnp.ones((CTX_LEN, LANES), F32)], axis=0)
    s = jnp.concatenate([jnp.tile(s64, (1, 2)), jnp.zeros((CTX_LEN, LANES), F32)], axis=0)
    return c, s


def kernel(x, c, ctx, c_ctx, w_mod, b_mod, w_in, w_out, diff_lambda, pool_w, pool_scale, ret_log_decay, ln_g, ln_b,
           w_router, router_bias, w_expert_gate_up, w_expert_down, w_shared_gate_up, w_shared_down):
    batch, seq, d = x.shape
    depth = w_mod.shape[0]
    assert d == D_MODEL and ctx.shape[1] == CTX_LEN == ROW_TILE and batch == 2
    assert seq % ROW_TILE == 0 and seq % GRID_W == 0 and w_in.shape[-1] == IN_WIDTH
    rows_per_batch = seq + CTX_LEN
    tiles_per_batch = rows_per_batch // ROW_TILE
    r = batch * rows_per_batch
    alpha = (2.0 * depth) ** 0.25

    xa = jnp.concatenate([x, ctx], axis=1).reshape(r, d)
    cvec = jnp.zeros((8, d), F32).at[0:batch].set(c).at[batch].set(c_ctx)
    mod_all = _mod_call(cvec, w_mod, b_mod)
    rope_c, rope_s = _rope_tables(seq)

    n_sorted = r * TOP_K + N_EXPERTS * EXPERT_BLOCK
    n_blocks = n_sorted // EXPERT_BLOCK

    for l in range(depth):
        lambda_init = 0.8 - 0.6 * math.exp(-0.3 * l)
        mod3 = mod_all[l].reshape(8, 1, 6 * d)
        lng = ln_g[l].reshape(2, 1, d)
        lnb = ln_b[l].reshape(2, 1, d)

        w_in_bf = w_in[l].astype(BF16)
        w_vt_bf = w_in_bf[:, QK_WIDTH:QK_WIDTH + DA_WIDTH].T
        qk, vda, u, rqkv, rg = _inproj_call(xa, mod3, w_in_bf, w_vt_bf, rope_c, rope_s, tiles_per_batch)
        da = _attn_call(diff_lambda[l], qk, vda, batch=batch, rows_per_batch=rows_per_batch, seq=seq,
                        lambda_init=lambda_init)
        o_f, o_b = _ret_call(ret_log_decay[l], rqkv, batch=batch, rows_per_batch=rows_per_batch, seq=seq)
        pool_bd = jnp.zeros((POOL_WIDTH, POOL_WIDTH), F32)
        for gi in range(len(POOL_WINDOWS)):
            sl = slice(gi * POOL_GROUP, (gi + 1) * POOL_GROUP)
            pool_bd = pool_bd.at[sl, sl].set(pool_w[l, gi])
        xa = _mixout_call(xa, da, u, o_f, o_b, rg, mod3, w_out[l].astype(BF16), pool_bd.astype(BF16),
                          pool_scale[l].reshape(1, POOL_WIDTH), lng[0], lnb[0],
                          tiles_per_batch=tiles_per_batch, seq=seq, alpha=alpha)

        wr_t = w_router[l].T
        wr_hi = wr_t.astype(BF16)
        wr_lo = (wr_t - wr_hi.astype(F32)).astype(BF16)
        tokp, idx, gate, rank, cnt, fsh = _router_call(
            xa, mod3, wr_hi, wr_lo, router_bias[l].reshape(N_EXPERTS, 1),
            w_shared_gate_up[l].astype(BF16), w_shared_down[l].astype(BF16), tiles_per_batch=tiles_per_batch)
        counts = cnt[:, 0]
        padded = (counts + EXPERT_BLOCK - 1) // EXPERT_BLOCK * EXPERT_BLOCK
        pad_end = jnp.cumsum(padded)
        offs = pad_end - padded
        expert_ids = jnp.arange(N_EXPERTS, dtype=jnp.int32)
        blk_row = jnp.arange(n_blocks, dtype=jnp.int32) * EXPERT_BLOCK
        block_expert = jnp.minimum(jnp.sum(pad_end[None, :] <= blk_row[:, None], axis=1), N_EXPERTS - 1)
        n_used = pad_end[-1:] // EXPERT_BLOCK
        used = counts > 0
        ordinal = jnp.cumsum(used) - 1
        hit = used[None, :] & (ordinal[None, :] == expert_ids[:, None])
        used_expert = jnp.sum(jnp.where(hit, expert_ids[None, :], 0), axis=1)
        n_used_experts = jnp.sum(used)[None]
        block_ordinal = ordinal[block_expert]
        slot = jnp.arange(EXPERT_BLOCK, dtype=jnp.int32)[None, :]
        first_pad = (padded - EXPERT_BLOCK)[:, None] + slot
        is_pad = (first_pad >= counts[:, None]) & (padded[:, None] > 0)
        spare = n_sorted + jnp.arange(N_EXPERTS * EXPERT_BLOCK, dtype=jnp.int32).reshape(N_EXPERTS, EXPERT_BLOCK)
        pad_rows = jnp.where(is_pad, offs[:, None] + first_pad, spare).reshape(N_EXPERTS * EXPERT_BLOCK)
        i32 = lambda a: a.astype(jnp.int32)

        dest = _dest_call(idx, rank, i32(offs).reshape(N_EXPERTS, 1))
        dest_flat = dest.reshape(TOP_K * r)
        xs = _sc_dispatch(tokp, dest_flat, i32(pad_rows), n_sorted)
        ys = _expert_call(i32(block_expert), i32(n_used), i32(block_ordinal), i32(used_expert),
                          i32(n_used_experts), xs, w_expert_gate_up, w_expert_down, l)
        y_tok = _sc_gather_rows(ys, dest_flat)
        xa = _combine_call(y_tok, xa, fsh, gate, mod3, lng[1], lnb[1], tiles_per_batch=tiles_per_batch,
                           alpha=alpha, drop_context=(l == depth - 1))

    return xa.reshape(batch, seq, d)
```

```python
import functools
import math

import jax
import jax.numpy as jnp
from jax import lax
from jax.experimental import pallas as pl
from jax.experimental.pallas import tpu as pltpu
from jax.experimental.pallas import tpu_sc as plsc

F32 = jnp.float32
BF16 = jnp.bfloat16
HIGHEST = lax.Precision.HIGHEST

D_MODEL = 1024
CTX_LEN = 256
GRID_W = 64
DA_HEADS = 4
DA_DIM = 64
DA_VDIM = 2 * DA_DIM
DA_WIDTH = DA_HEADS * DA_VDIM
ROPE_BASE = 10000.0
POOL_WINDOWS = (2, 4, 8, 16)
POOL_GROUP = 64
POOL_WIDTH = len(POOL_WINDOWS) * POOL_GROUP
POOL_HALO = 8
RET_HEADS = 4
RET_DK = 64
RET_WIDTH = RET_HEADS * RET_DK
RET_CHUNK = 128
QK_WIDTH = 2 * DA_HEADS * 2 * DA_DIM
IN_WIDTH = QK_WIDTH + DA_WIDTH + POOL_WIDTH + 4 * RET_WIDTH
N_EXPERTS = 256
TOP_K = 8
N_GROUPS = 8
GROUP_SIZE = N_EXPERTS // N_GROUPS
TOPK_GROUPS = 4
EXPERT_HIDDEN = 256
ROUTED_SCALE = 2.5
LN_EPS = 1e-6
RMS_EPS = 1e-5

LANES = 128
ROW_TILE = 256
ATTN_Q_TILE = 256
ATTN_K_CHUNK = 256
ATTN_UNROLL = 16
SC_NUM_CORES = 2
SC_NUM_SUBCORES = 16
SC_GATHER_WINDOW = 128
EXPERT_BLOCK = 256
SC_SPARE_ROWS = N_EXPERTS * EXPERT_BLOCK
PACK_W = D_MODEL // 2
VMEM_LIMIT = 56 * 1024 * 1024


def _cparams(*sem):
    return pltpu.CompilerParams(dimension_semantics=sem, vmem_limit_bytes=VMEM_LIMIT)


def _sigmoid(x):
    return 1.0 / (1.0 + jnp.exp(-x))


def _layer_norm_rows(x):
    mu = jnp.mean(x, axis=-1, keepdims=True)
    xc = x - mu
    var = jnp.mean(xc * xc, axis=-1, keepdims=True)
    return xc * lax.rsqrt(var + LN_EPS)


def _pack_bf16_pairs(x):
    half = x.shape[1] // 2
    bits = pltpu.bitcast(x.astype(BF16).astype(F32), jnp.uint32)
    word = lax.shift_right_logical(bits[:, 0:half], jnp.uint32(16)) | (bits[:, half:] & jnp.uint32(0xFFFF0000))
    return pltpu.bitcast(word, jnp.int32)


def _unpack_bf16_pairs(packed):
    word = pltpu.bitcast(packed, jnp.uint32)
    lo = pltpu.bitcast(lax.shift_left(word, jnp.uint32(16)), F32)
    hi = pltpu.bitcast(word & jnp.uint32(0xFFFF0000), F32)
    return lo, hi


def _mod_row(i, tiles_per_batch):
    return jnp.where(i % tiles_per_batch == tiles_per_batch - 1, 2, i // tiles_per_batch)


def _mod_kernel(c_ref, w_ref, b_ref, o_ref):
    c = c_ref[...]
    s = c * _sigmoid(c)
    o_ref[...] = jnp.dot(s, w_ref[...], precision=HIGHEST, preferred_element_type=F32) + b_ref[...]


def _mod_call(cvec, w_mod, b_mod):
    depth, d, n = w_mod.shape
    tn = 1536
    return pl.pallas_call(
        _mod_kernel,
        grid=(depth, n // tn),
        in_specs=[
            pl.BlockSpec((8, d), lambda l, j: (0, 0)),
            pl.BlockSpec((None, d, tn), lambda l, j: (l, 0, j)),
            pl.BlockSpec((None, 1, tn), lambda l, j: (l, 0, j)),
        ],
        out_specs=pl.BlockSpec((None, 8, tn), lambda l, j: (l, 0, j)),
        out_shape=jax.ShapeDtypeStruct((depth, 8, n), F32),
        compiler_params=_cparams("arbitrary", "arbitrary"),
        name="mod",
    )(cvec, w_mod, b_mod.reshape(depth, 1, n))


def _inproj_kernel(x_ref, mod_ref, w_ref, wvt_ref, ct_ref, st_ref, qk_ref, vt_ref, u_ref, r_ref, g_ref):
    d = D_MODEL
    xn = _layer_norm_rows(x_ref[...])
    h = (xn * (1.0 + mod_ref[:, d:2 * d]) + mod_ref[:, 0:d]).astype(BF16)

    a = jnp.dot(h, w_ref[:, 0:QK_WIDTH], preferred_element_type=F32)
    lane = lax.broadcasted_iota(jnp.int32, (a.shape[0], LANES), 1)
    first_half = (lane % 32) < 16
    ct = ct_ref[...]
    st = st_ref[...]
    for s in range(QK_WIDTH // LANES):
        blk = a[:, s * LANES:(s + 1) * LANES]
        partner = jnp.where(first_half, pltpu.roll(blk, LANES - 16, 1), pltpu.roll(blk, 16, 1))
        rot = blk * ct + partner * st
        if s < QK_WIDTH // LANES // 2:
            rot = rot * (DA_DIM ** -0.5 * math.log2(math.e))
        qk_ref[:, s * LANES:(s + 1) * LANES] = rot.astype(BF16)

    vt_ref[...] = lax.dot_general(wvt_ref[...], h, (((1,), (1,)), ((), ())),
                                  preferred_element_type=F32).astype(BF16)
    o = QK_WIDTH + DA_WIDTH
    u_ref[...] = jnp.dot(h, w_ref[:, o:o + POOL_WIDTH], preferred_element_type=F32)
    o += POOL_WIDTH
    r = jnp.dot(h, w_ref[:, o:o + 3 * RET_WIDTH], preferred_element_type=F32)
    r_ref[:, 0:RET_WIDTH] = r[:, 0:RET_WIDTH].astype(BF16)
    r_ref[:, RET_WIDTH:2 * RET_WIDTH] = (r[:, RET_WIDTH:2 * RET_WIDTH] * (RET_DK ** -0.5)).astype(BF16)
    r_ref[:, 2 * RET_WIDTH:] = r[:, 2 * RET_WIDTH:].astype(BF16)
    o += 3 * RET_WIDTH
    g_ref[...] = jnp.dot(h, w_ref[:, o:o + RET_WIDTH], preferred_element_type=F32)


def _inproj_call(x, mod3, w_in_bf, w_vt_bf, rope_c, rope_s, tiles_per_batch):
    r, d = x.shape
    t = ROW_TILE
    nt = r // t
    row = lambda i: (i, 0)
    return pl.pallas_call(
        _inproj_kernel,
        grid=(nt,),
        in_specs=[
            pl.BlockSpec((t, d), row),
            pl.BlockSpec((None, 1, 6 * d), lambda i: (_mod_row(i, tiles_per_batch), 0, 0)),
            pl.BlockSpec((d, IN_WIDTH), lambda i: (0, 0)),
            pl.BlockSpec((DA_WIDTH, d), lambda i: (0, 0)),
            pl.BlockSpec((t, LANES), lambda i: (i % tiles_per_batch, 0)),
            pl.BlockSpec((t, LANES), lambda i: (i % tiles_per_batch, 0)),
        ],
        out_specs=[
            pl.BlockSpec((t, QK_WIDTH), row),
            pl.BlockSpec((DA_WIDTH, t), lambda i: (0, i)),
            pl.BlockSpec((t, POOL_WIDTH), row),
            pl.BlockSpec((t, 3 * RET_WIDTH), row),
            pl.BlockSpec((t, RET_WIDTH), row),
        ],
        out_shape=[
            jax.ShapeDtypeStruct((r, QK_WIDTH), BF16),
            jax.ShapeDtypeStruct((DA_WIDTH, r), BF16),
            jax.ShapeDtypeStruct((r, POOL_WIDTH), F32),
            jax.ShapeDtypeStruct((r, 3 * RET_WIDTH), BF16),
            jax.ShapeDtypeStruct((r, RET_WIDTH), F32),
        ],
        compiler_params=_cparams("arbitrary"),
        name="inproj",
    )(x, mod3, w_in_bf, w_vt_bf, rope_c, rope_s)


def _attn_kernel(lam_ref, q_ref, k_ref, vt_ref, o_ref, s_ref, *, k_chunk, seq, lambda_init):
    q = q_ref[...]
    mq = q.shape[0]
    lane = lax.broadcasted_iota(jnp.int32, q.shape, 1)
    zero = jnp.zeros_like(q)
    q2 = jnp.concatenate([jnp.where(lane < DA_DIM, q, zero), jnp.where(lane >= DA_DIM, q, zero)], axis=0)
    qt = q2.astype(F32).T.astype(BF16)

    n_chunks = (seq + CTX_LEN) // k_chunk
    last = n_chunks - 1
    is_ctx_tile = pl.program_id(2) == pl.num_programs(2) - 1
    n_iters = jnp.where(is_ctx_tile, 0, last // ATTN_UNROLL)

    def score_chunk(c, m):
        off = pl.multiple_of(c * k_chunk, k_chunk)
        s = jnp.dot(k_ref[pl.ds(off, k_chunk), :], qt, preferred_element_type=F32)
        s_ref[c] = s
        return jnp.maximum(m, jnp.max(s, axis=0, keepdims=True))

    def pass1(it, m):
        for u in range(ATTN_UNROLL):
            m = score_chunk(it * ATTN_UNROLL + u, m)
        return m

    m = lax.fori_loop(0, n_iters, pass1, jnp.full((1, 2 * mq), -jnp.inf, F32))
    m = score_chunk(last, m)

    ones_rows = jnp.where(lax.broadcasted_iota(jnp.int32, (16, k_chunk), 0) == 0, 1.0, 0.0).astype(BF16)

    def value_chunk(c, acc):
        off = pl.multiple_of(c * k_chunk, k_chunk)
        vt = jnp.concatenate([vt_ref[:, pl.ds(off, k_chunk)], ones_rows], axis=0)
        p = jnp.exp2((s_ref[c] - m).astype(BF16))
        return acc + jnp.dot(vt, p, preferred_element_type=F32)

    def pass2(it, acc):
        for u in range(ATTN_UNROLL):
            acc = value_chunk(it * ATTN_UNROLL + u, acc)
        return acc

    acc = lax.fori_loop(0, n_iters, pass2, jnp.zeros((DA_VDIM + 16, 2 * mq), F32))
    acc = value_chunk(last, acc)
    l0, l1 = acc[DA_VDIM:DA_VDIM + 1, 0:mq], acc[DA_VDIM:DA_VDIM + 1, mq:]
    a0, a1 = acc[0:DA_VDIM, 0:mq], acc[0:DA_VDIM, mq:]

    lv = lam_ref[...]
    lam = (jnp.exp(jnp.sum(lv[0:1] * lv[1:2], axis=-1, keepdims=True))
           - jnp.exp(jnp.sum(lv[2:3] * lv[3:4], axis=-1, keepdims=True)) + lambda_init)
    o = a0 / l0 - lam * (a1 / l1)
    o = o * lax.rsqrt(jnp.mean(o * o, axis=0, keepdims=True) + RMS_EPS) * (1.0 - lambda_init)
    o_ref[...] = o.T.astype(BF16)


def _attn_call(lam_vec, qk, vda, *, batch, rows_per_batch, seq, lambda_init):
    tq = ATTN_Q_TILE
    assert seq % (ATTN_K_CHUNK * ATTN_UNROLL) == 0 and rows_per_batch - seq == CTX_LEN == tq == ATTN_K_CHUNK
    nq = rows_per_batch // tq
    kern = functools.partial(_attn_kernel, k_chunk=ATTN_K_CHUNK, seq=seq, lambda_init=lambda_init)
    return pl.pallas_call(
        kern,
        grid=(batch, DA_HEADS, nq),
        in_specs=[
            pl.BlockSpec((4, DA_DIM), lambda b, h, i: (0, 0)),
            pl.BlockSpec((tq, DA_VDIM), lambda b, h, i: (b * nq + i, h)),
            pl.BlockSpec((rows_per_batch, DA_VDIM), lambda b, h, i: (b, DA_HEADS + h)),
            pl.BlockSpec((DA_VDIM, rows_per_batch), lambda b, h, i: (h, b)),
        ],
        out_specs=pl.BlockSpec((tq, DA_VDIM), lambda b, h, i: (b * nq + i, h)),
        out_shape=jax.ShapeDtypeStruct((qk.shape[0], DA_WIDTH), BF16),
        scratch_shapes=[pltpu.VMEM((rows_per_batch // ATTN_K_CHUNK, ATTN_K_CHUNK, 2 * tq), F32)],
        compiler_params=_cparams("arbitrary", "arbitrary", "arbitrary"),
        name="diff_attn",
    )(lam_vec, qk, qk, vda)


def _ret_kernel(ld_ref, f_ref, b_ref, of_ref, ob_ref, dm_ref, qd_ref, kd_ref, cd_ref, st_ref):
    c = pl.program_id(1)
    ch = RET_CHUNK
    w = RET_WIDTH
    lane_head = lax.broadcasted_iota(jnp.int32, (1, w), 1) // RET_DK

    @pl.when(c == 0)
    def _():
        st_ref[...] = jnp.zeros_like(st_ref)
        ri = lax.broadcasted_iota(jnp.int32, (ch, ch), 0)
        ci = lax.broadcasted_iota(jnp.int32, (ch, ch), 1)
        rowf = lax.broadcasted_iota(jnp.int32, (ch, w), 0).astype(F32)
        for d in range(2):
            lg_lane = jnp.zeros((1, w), F32)
            for hh in range(RET_HEADS):
                lg = -jnp.exp(jnp.full((1, 1), ld_ref[d, hh], F32))
                lg_lane = jnp.where(lane_head == hh, lg, lg_lane)
                dist = ((ri - ci) if d == 0 else (ci - ri)).astype(F32)
                dm_ref[d, hh] = jnp.where(dist >= 0, jnp.exp(dist * lg), 0.0)
            if d == 0:
                qd_ref[d] = jnp.exp((rowf + 1.0) * lg_lane)
                kd_ref[d] = jnp.exp((ch - 1.0 - rowf) * lg_lane)
            else:
                qd_ref[d] = jnp.exp((ch - rowf) * lg_lane)
                kd_ref[d] = jnp.exp(rowf * lg_lane)
            cd_ref[d] = jnp.exp(float(ch) * lg_lane)

    rblk = lax.broadcasted_iota(jnp.int32, (w, w), 0) // RET_DK
    cblk = lax.broadcasted_iota(jnp.int32, (w, w), 1) // RET_DK
    for d, (src, dst) in enumerate(((f_ref, of_ref), (b_ref, ob_ref))):
        q = src[:, 0:w]
        k = src[:, w:2 * w]
        v = src[:, 2 * w:3 * w]
        st = st_ref[d]
        o = jnp.dot((q.astype(F32) * qd_ref[d]).astype(BF16), st.astype(BF16), preferred_element_type=F32)
        for hh in range(RET_HEADS):
            in_head = lane_head == hh
            qm = jnp.where(in_head, q, jnp.zeros_like(q))
            s = lax.dot_general(qm, k, (((1,), (1,)), ((), ())), preferred_element_type=F32)
            intra = (s * dm_ref[d, hh]).astype(BF16)
            o = o + jnp.where(in_head, jnp.dot(intra, v, preferred_element_type=F32), 0.0)
        dst[...] = o
        kk_t = (k.astype(F32) * kd_ref[d]).T.astype(BF16)
        upd = jnp.dot(kk_t, v, preferred_element_type=F32)
        st_ref[d] = jnp.where(rblk == cblk, st * cd_ref[d] + upd, 0.0)


def _ret_call(log_decay, rqkv, *, batch, rows_per_batch, seq):
    ch = RET_CHUNK
    nc = rows_per_batch // ch
    n_lat = seq // ch
    n_ctx = nc - n_lat

    def fwd(b, c):
        return (b * nc + jnp.where(c < n_ctx, n_lat + c, c - n_ctx), 0)

    def bwd(b, c):
        return (b * nc + nc - 1 - c, 0)

    w = RET_WIDTH
    return pl.pallas_call(
        _ret_kernel,
        grid=(batch, nc),
        in_specs=[
            pl.BlockSpec(memory_space=pltpu.SMEM),
            pl.BlockSpec((ch, 3 * w), fwd),
            pl.BlockSpec((ch, 3 * w), bwd),
        ],
        out_specs=[pl.BlockSpec((ch, w), fwd), pl.BlockSpec((ch, w), bwd)],
        out_shape=[jax.ShapeDtypeStruct((rqkv.shape[0], w), F32)] * 2,
        scratch_shapes=[
            pltpu.VMEM((2, RET_HEADS, ch, ch), F32),
            pltpu.VMEM((2, ch, w), F32),
            pltpu.VMEM((2, ch, w), F32),
            pltpu.VMEM((2, 1, w), F32),
            pltpu.VMEM((2, w, w), F32),
        ],
        compiler_params=_cparams("arbitrary", "arbitrary"),
        name="retention",
    )(log_decay, rqkv, rqkv)


def _mixout_kernel(x_ref, da_ref, u_ref, up_ref, un_ref, of_ref, ob_ref, rg_ref, mod_ref, wo_ref, pw_ref,
                   ps_ref, lng_ref, lnb_ref, o_ref, *, tiles_per_batch, seq, alpha):
    d = D_MODEL
    t = x_ref.shape[0]
    i = pl.program_id(0)
    j = i % tiles_per_batch
    is_ctx = j == tiles_per_batch - 1
    stream_len = jnp.where(is_ctx, CTX_LEN, seq)
    p0 = jnp.where(is_ctx, 0, j * t)

    u = u_ref[...]
    prev = jnp.where(p0 > 0, up_ref[...], 0.0)
    nxt = jnp.where(p0 + t < stream_len, un_ref[...], 0.0)
    ext = jnp.concatenate([prev, u, nxt], axis=0)
    n = t + 2 * POOL_HALO
    a2 = ext + pltpu.roll(ext, 1, 0)
    a4 = pltpu.roll(a2, 1, 0) + pltpu.roll(a2, n - 1, 0)
    a8 = pltpu.roll(a4, 2, 0) + pltpu.roll(a4, n - 2, 0)
    a16 = pltpu.roll(a8, 4, 0) + pltpu.roll(a8, n - 4, 0)
    pos = p0 + lax.broadcasted_iota(jnp.int32, (t, POOL_WIDTH), 0)
    group = lax.broadcasted_iota(jnp.int32, (1, POOL_WIDTH), 1) // POOL_GROUP
    mean = jnp.zeros((t, POOL_WIDTH), F32)
    for gi, (wnd, asum) in enumerate(zip(POOL_WINDOWS, (a2, a4, a8, a16))):
        cnt = jnp.minimum(pos + wnd // 2, stream_len) - jnp.maximum(pos - wnd // 2, 0)
        mean = jnp.where(group == gi, asum[POOL_HALO:POOL_HALO + t] / cnt.astype(F32), mean)
    pool = jnp.dot((mean - u).astype(BF16), pw_ref[...], preferred_element_type=F32) * ps_ref[...]

    o = of_ref[...] + ob_ref[...]
    head = lax.broadcasted_iota(jnp.int32, (1, RET_WIDTH), 1) // RET_DK

    def head_mean(val):
        out = jnp.zeros_like(val)
        for hh in range(RET_HEADS):
            m = jnp.sum(jnp.where(head == hh, val, 0.0), axis=-1, keepdims=True) * (1.0 / RET_DK)
            out = jnp.where(head == hh, m, out)
        return out

    oc = o - head_mean(o)
    rn = oc * lax.rsqrt(head_mean(oc * oc) + LN_EPS)
    g = rg_ref[...]
    ret = rn * (g * _sigmoid(g))

    y = jnp.dot(da_ref[...], wo_ref[0:DA_WIDTH, :], preferred_element_type=F32)
    y = y + jnp.dot(pool.astype(BF16), wo_ref[DA_WIDTH:DA_WIDTH + POOL_WIDTH, :], preferred_element_type=F32)
    y = y + jnp.dot(ret.astype(BF16), wo_ref[DA_WIDTH + POOL_WIDTH:, :], preferred_element_type=F32)
    z = alpha * x_ref[...] + mod_ref[:, 2 * d:3 * d] * y
    o_ref[...] = _layer_norm_rows(z) * lng_ref[...] + lnb_ref[...]


def _mixout_call(x, da, u, o_f, o_b, rg, mod3, w_out_bf, pool_bd, pool_scale, ln_g, ln_b, *, tiles_per_batch, seq,
                 alpha):
    r, d = x.shape
    t = ROW_TILE
    nt = r // t
    hb = t // POOL_HALO
    n_halo_blocks = r // POOL_HALO
    row = lambda i: (i, 0)
    const = lambda i: (0, 0)
    kern = functools.partial(_mixout_kernel, tiles_per_batch=tiles_per_batch, seq=seq, alpha=alpha)
    return pl.pallas_call(
        kern,
        grid=(nt,),
        in_specs=[
            pl.BlockSpec((t, d), row),
            pl.BlockSpec((t, DA_WIDTH), row),
            pl.BlockSpec((t, POOL_WIDTH), row),
            pl.BlockSpec((POOL_HALO, POOL_WIDTH), lambda i: (jnp.maximum(i * hb - 1, 0), 0)),
            pl.BlockSpec((POOL_HALO, POOL_WIDTH), lambda i: (jnp.minimum((i + 1) * hb, n_halo_blocks - 1), 0)),
            pl.BlockSpec((t, RET_WIDTH), row),
            pl.BlockSpec((t, RET_WIDTH), row),
            pl.BlockSpec((t, RET_WIDTH), row),
            pl.BlockSpec((None, 1, 6 * d), lambda i: (_mod_row(i, tiles_per_batch), 0, 0)),
            pl.BlockSpec((d, d), const),
            pl.BlockSpec((POOL_WIDTH, POOL_WIDTH), const),
            pl.BlockSpec((1, POOL_WIDTH), const),
            pl.BlockSpec((1, d), const),
            pl.BlockSpec((1, d), const),
        ],
        out_specs=pl.BlockSpec((t, d), row),
        out_shape=jax.ShapeDtypeStruct((r, d), F32),
        compiler_params=_cparams("arbitrary"),
        name="mixer_out",
    )(x, da, u, u, u, o_f, o_b, rg, mod3, w_out_bf, pool_bd, pool_scale, ln_g, ln_b)


def _router_kernel(x_ref, mod_ref, wrh_ref, wrl_ref, bias_ref, wsgu_ref, wsdn_ref,
                   tokp_ref, idx_ref, gate_ref, rank_ref, cnt_ref, fsh_ref, carry_ref):
    d = D_MODEL
    t = x_ref.shape[0]
    ne = N_EXPERTS
    neg = -jnp.inf

    @pl.when(pl.program_id(0) == 0)
    def _():
        carry_ref[...] = jnp.zeros_like(carry_ref)

    tok = _layer_norm_rows(x_ref[...]) * (1.0 + mod_ref[:, 4 * d:5 * d]) + mod_ref[:, 3 * d:4 * d]
    tok_hi = tok.astype(BF16)
    tok_lo = (tok - tok_hi.astype(F32)).astype(BF16)

    tokp_ref[...] = _pack_bf16_pairs(tok)

    hs = jnp.dot(tok_hi, wsgu_ref[...], preferred_element_type=F32)
    gs, us = hs[:, 0:EXPERT_HIDDEN], hs[:, EXPERT_HIDDEN:]
    fsh_ref[...] = jnp.dot((gs * _sigmoid(gs) * us).astype(BF16), wsdn_ref[...], preferred_element_type=F32)

    nt_dims = (((1,), (1,)), ((), ()))
    logits = (lax.dot_general(wrh_ref[...], tok_hi, nt_dims, preferred_element_type=F32)
              + lax.dot_general(wrh_ref[...], tok_lo, nt_dims, preferred_element_type=F32)
              + lax.dot_general(wrl_ref[...], tok_hi, nt_dims, preferred_element_type=F32))
    scores = _sigmoid(logits)
    biased = scores + bias_ref[...]

    gidx = lax.broadcasted_iota(jnp.int32, (GROUP_SIZE, t), 0)
    blocks, gscores = [], []
    for g in range(N_GROUPS):
        blk = biased[g * GROUP_SIZE:(g + 1) * GROUP_SIZE, :]
        m1 = jnp.max(blk, axis=0, keepdims=True)
        first = jnp.min(jnp.where(blk == m1, gidx, GROUP_SIZE), axis=0, keepdims=True)
        m2 = jnp.max(jnp.where(gidx == first, neg, blk), axis=0, keepdims=True)
        blocks.append(blk)
        gscores.append(m1 + m2)

    keep = [jnp.zeros((1, t), F32) for _ in range(N_GROUPS)]
    for _ in range(TOPK_GROUPS):
        m = gscores[0]
        for gs_ in gscores[1:]:
            m = jnp.maximum(m, gs_)
        found = jnp.zeros((1, t), F32)
        for g in range(N_GROUPS):
            hit = jnp.where(gscores[g] == m, 1.0 - found, 0.0)
            found = found + hit
            keep[g] = keep[g] + hit
            gscores[g] = jnp.where(hit > 0.0, neg, gscores[g])
    masked = jnp.concatenate([jnp.where(keep[g] > 0.0, blocks[g], neg) for g in range(N_GROUPS)], axis=0)

    ei = lax.broadcasted_iota(jnp.int32, (ne, t), 0)
    cur = masked
    onehot = jnp.zeros((ne, t), F32)
    idxs, gates = [], []
    for _ in range(TOP_K):
        m = jnp.max(cur, axis=0, keepdims=True)
        ii = jnp.min(jnp.where(cur == m, ei, ne), axis=0, keepdims=True)
        sel = ei == ii
        idxs.append(ii)
        gates.append(jnp.sum(jnp.where(sel, scores, 0.0), axis=0, keepdims=True))
        onehot = jnp.where(sel, 1.0, onehot)
        cur = jnp.where(sel, neg, cur)
    gsum = gates[0]
    for gk in gates[1:]:
        gsum = gsum + gk
    for k in range(TOP_K):
        idx_ref[k:k + 1, :] = idxs[k]
        gate_ref[k:k + 1, :] = gates[k] / gsum * ROUTED_SCALE

    ti = lax.broadcasted_iota(jnp.int32, (t, t), 0)
    tj = lax.broadcasted_iota(jnp.int32, (t, t), 1)
    before = jnp.where(ti < tj, 1.0, 0.0).astype(BF16)
    prefix = jnp.dot(onehot.astype(BF16), before, preferred_element_type=F32) + carry_ref[:, 0:1]
    for k in range(TOP_K):
        rank_k = jnp.sum(jnp.where(ei == idxs[k], prefix, 0.0), axis=0, keepdims=True)
        rank_ref[k:k + 1, :] = rank_k.astype(jnp.int32)
    carry_ref[...] = carry_ref[...] + jnp.sum(onehot, axis=1, keepdims=True)
    cnt_ref[...] = carry_ref[...].astype(jnp.int32)


def _router_call(x, mod3, wr_hi, wr_lo, bias_col, ws_gu_bf, ws_dn_bf, *, tiles_per_batch):
    r, d = x.shape
    t = ROW_TILE
    nt = r // t
    row = lambda i: (i, 0)
    col = lambda i: (0, i)
    const = lambda i: (0, 0)
    return pl.pallas_call(
        _router_kernel,
        grid=(nt,),
        in_specs=[
            pl.BlockSpec((t, d), row),
            pl.BlockSpec((None, 1, 6 * d), lambda i: (_mod_row(i, tiles_per_batch), 0, 0)),
            pl.BlockSpec((N_EXPERTS, d), const),
            pl.BlockSpec((N_EXPERTS, d), const),
            pl.BlockSpec((N_EXPERTS, 1), const),
            pl.BlockSpec((d, 2 * EXPERT_HIDDEN), const),
            pl.BlockSpec((EXPERT_HIDDEN, d), const),
        ],
        out_specs=[
            pl.BlockSpec((t, PACK_W), row),
            pl.BlockSpec((TOP_K, t), col),
            pl.BlockSpec((TOP_K, t), col),
            pl.BlockSpec((TOP_K, t), col),
            pl.BlockSpec((N_EXPERTS, LANES), const),
            pl.BlockSpec((t, d), row),
        ],
        out_shape=[
            jax.ShapeDtypeStruct((r, PACK_W), jnp.int32),
            jax.ShapeDtypeStruct((TOP_K, r), jnp.int32),
            jax.ShapeDtypeStruct((TOP_K, r), F32),
            jax.ShapeDtypeStruct((TOP_K, r), jnp.int32),
            jax.ShapeDtypeStruct((N_EXPERTS, LANES), jnp.int32),
            jax.ShapeDtypeStruct((r, d), F32),
        ],
        scratch_shapes=[pltpu.VMEM((N_EXPERTS, LANES), F32)],
        compiler_params=_cparams("arbitrary"),
        name="router",
    )(x, mod3, wr_hi, wr_lo, bias_col, ws_gu_bf, ws_dn_bf)


def _dest_kernel(idx_ref, rank_ref, offs_ref, dest_ref):
    t = idx_ref.shape[1]
    ei = lax.broadcasted_iota(jnp.int32, (N_EXPERTS, t), 0)
    offs = offs_ref[...].astype(F32)
    for k in range(TOP_K):
        start = jnp.sum(jnp.where(ei == idx_ref[k:k + 1, :], offs, 0.0), axis=0, keepdims=True)
        dest_ref[k:k + 1, :] = start.astype(jnp.int32) + rank_ref[k:k + 1, :]


def _dest_call(idx, rank, offs_col):
    r = idx.shape[1]
    t = ROW_TILE
    col = lambda i: (0, i)
    return pl.pallas_call(
        _dest_kernel,
        grid=(r // t,),
        in_specs=[pl.BlockSpec((TOP_K, t), col), pl.BlockSpec((TOP_K, t), col),
                  pl.BlockSpec((N_EXPERTS, 1), lambda i: (0, 0))],
        out_specs=pl.BlockSpec((TOP_K, t), col),
        out_shape=jax.ShapeDtypeStruct((TOP_K, r), jnp.int32),
        compiler_params=_cparams("arbitrary"),
        name="moe_dest",
    )(idx, rank, offs_col)


def _sc_dispatch(tokp, dest_flat, pad_rows, n_sorted):
    r, width = tokp.shape
    win = SC_GATHER_WINDOW
    workers = SC_NUM_CORES * SC_NUM_SUBCORES
    token_windows = r // win
    n_items = dest_flat.shape[0] // win
    n_pad_windows = pad_rows.shape[0] // win
    assert r % win == 0 and n_items % workers == 0 and n_pad_windows % workers == 0
    items_per_worker = n_items // workers
    pads_per_worker = n_pad_windows // workers
    mesh = plsc.VectorSubcoreMesh(core_axis_name="core", subcore_axis_name="subcore", num_cores=SC_NUM_CORES,
                                  num_subcores=SC_NUM_SUBCORES)
    zero_rows = jnp.zeros((win, width), tokp.dtype)

    @functools.partial(
        pl.kernel, out_type=jax.ShapeDtypeStruct((n_sorted + SC_SPARE_ROWS, width), tokp.dtype), mesh=mesh,
        scratch_types=[pltpu.VMEM((win,), jnp.int32), pltpu.VMEM((win, width), tokp.dtype),
                       pltpu.SemaphoreType.DMA],
        name="moe_sc_dispatch")
    def dispatch_kernel(tok_hbm, dest_hbm, pad_hbm, zero_hbm, xs_hbm, idx_vmem, rows_vmem, sem):
        worker = lax.axis_index("subcore") * SC_NUM_CORES + lax.axis_index("core")

        @pl.loop(0, items_per_worker)
        def _(j):
            item = worker * items_per_worker + j
            tok0 = lax.rem(item, token_windows) * win
            pltpu.sync_copy(dest_hbm.at[pl.ds(item * win, win)], idx_vmem)
            pltpu.sync_copy(tok_hbm.at[pl.ds(tok0, win)], rows_vmem)
            pltpu.async_copy(rows_vmem, xs_hbm.at[idx_vmem], sem).wait()

        pltpu.sync_copy(zero_hbm, rows_vmem)

        @pl.loop(0, pads_per_worker)
        def _(j):
            off = (worker * pads_per_worker + j) * win
            pltpu.sync_copy(pad_hbm.at[pl.ds(off, win)], idx_vmem)
            pltpu.async_copy(rows_vmem, xs_hbm.at[idx_vmem], sem).wait()

    return dispatch_kernel(tokp, dest_flat, pad_rows, zero_rows)


def _expert_kernel(be_ref, nb_ref, ord_ref, ue_ref, nue_ref, xs_ref, wgu_hbm, wdn_hbm, ys_ref, wgu_f32, wdn_f32,
                   wgu_bf, wdn_bf, act_ref, sems, *, layer):
    j = pl.program_id(0)

    def weight_copies(o):
        slot = o % 2
        e = ue_ref[o]
        return (pltpu.make_async_copy(wgu_hbm.at[layer, e], wgu_f32.at[slot], sems.at[0, slot]),
                pltpu.make_async_copy(wdn_hbm.at[layer, e], wdn_f32.at[slot], sems.at[1, slot]))

    def start_weights(o):
        @pl.when(o < nue_ref[0])
        def _():
            for cp in weight_copies(o):
                cp.start()

    nb = nb_ref[0]

    @pl.when(j < nb)
    def _():
        o = ord_ref[j]
        changed = jnp.logical_or(j == 0, be_ref[j] != be_ref[jnp.maximum(j - 1, 0)])

        @pl.when(j == 0)
        def _():
            start_weights(0)
            start_weights(1)

        @pl.when(changed)
        def _():
            for cp in weight_copies(o):
                cp.wait()
            slot = o % 2
            wgu_bf[...] = wgu_f32[slot].astype(BF16)
            wdn_bf[slot] = wdn_f32[slot].astype(BF16)
            start_weights(o + 2)

    def up_stage():
        x_lo, x_hi = _unpack_bf16_pairs(xs_ref[...])
        h = (jnp.dot(x_lo.astype(BF16), wgu_bf[0:PACK_W, :], preferred_element_type=F32)
             + jnp.dot(x_hi.astype(BF16), wgu_bf[PACK_W:, :], preferred_element_type=F32))
        g, u = h[:, 0:EXPERT_HIDDEN], h[:, EXPERT_HIDDEN:]
        act_ref[j % 2] = (g * _sigmoid(g) * u).astype(BF16)

    def down_stage():
        prev = jnp.maximum(j - 1, 0)
        y = jnp.dot(act_ref[prev % 2], wdn_bf[ord_ref[prev] % 2], preferred_element_type=F32)
        ys_ref[...] = _pack_bf16_pairs(y)

    @pl.when(j == 0)
    def _():
        up_stage()

    @pl.when(jnp.logical_and(j >= 1, j < nb))
    def _():
        down_stage()
        up_stage()

    @pl.when(j == nb)
    def _():
        down_stage()


def _expert_call(block_expert, n_blocks_used, block_ordinal, used_expert, n_used_experts, xs, w_gu, w_dn, layer):
    n_rows = block_expert.shape[0] * EXPERT_BLOCK
    bm = EXPERT_BLOCK
    d = D_MODEL
    in_block = lambda j, be, nb, od, ue, nue: (jnp.minimum(j, nb[0] - 1), 0)
    out_block = lambda j, be, nb, od, ue, nue: (jnp.clip(j - 1, 0, nb[0] - 1), 0)
    grid_spec = pltpu.PrefetchScalarGridSpec(
        num_scalar_prefetch=5,
        grid=(n_rows // bm + 1,),
        in_specs=[
            pl.BlockSpec((bm, PACK_W), in_block),
            pl.BlockSpec(memory_space=pl.ANY),
            pl.BlockSpec(memory_space=pl.ANY),
        ],
        out_specs=pl.BlockSpec((bm, PACK_W), out_block),
        scratch_shapes=[
            pltpu.VMEM((2, d, 2 * EXPERT_HIDDEN), F32),
            pltpu.VMEM((2, EXPERT_HIDDEN, d), F32),
            pltpu.VMEM((d, 2 * EXPERT_HIDDEN), BF16),
            pltpu.VMEM((2, EXPERT_HIDDEN, d), BF16),
            pltpu.VMEM((2, bm, EXPERT_HIDDEN), BF16),
            pltpu.SemaphoreType.DMA((2, 2)),
        ],
    )
    return pl.pallas_call(
        functools.partial(_expert_kernel, layer=layer),
        grid_spec=grid_spec,
        out_shape=jax.ShapeDtypeStruct((n_rows, PACK_W), jnp.int32),
        compiler_params=_cparams("arbitrary"),
        name="moe_experts",
    )(block_expert, n_blocks_used, block_ordinal, used_expert, n_used_experts, xs, w_gu, w_dn)


def _sc_gather_rows(table, indices):
    n = indices.shape[0]
    width = table.shape[1]
    workers = SC_NUM_CORES * SC_NUM_SUBCORES
    assert n % (SC_GATHER_WINDOW * workers) == 0
    per_worker = n // workers
    mesh = plsc.VectorSubcoreMesh(core_axis_name="core", subcore_axis_name="subcore", num_cores=SC_NUM_CORES,
                                  num_subcores=SC_NUM_SUBCORES)

    @functools.partial(
        pl.kernel, out_type=jax.ShapeDtypeStruct((n, width), table.dtype), mesh=mesh,
        scratch_types=[pltpu.VMEM((SC_GATHER_WINDOW,), jnp.int32),
                       pltpu.VMEM((SC_GATHER_WINDOW, width), table.dtype),
                       pltpu.SemaphoreType.DMA],
        name="moe_sc_gather")
    def gather_kernel(table_hbm, idx_hbm, out_hbm, idx_vmem, rows_vmem, sem):
        worker = lax.axis_index("subcore") * SC_NUM_CORES + lax.axis_index("core")
        base = worker * per_worker

        @pl.loop(0, per_worker // SC_GATHER_WINDOW)
        def _(w):
            off = base + w * SC_GATHER_WINDOW
            pltpu.sync_copy(idx_hbm.at[pl.ds(off, SC_GATHER_WINDOW)], idx_vmem)
            pltpu.async_copy(table_hbm.at[idx_vmem], rows_vmem, sem).wait()
            pltpu.sync_copy(rows_vmem, out_hbm.at[pl.ds(off, SC_GATHER_WINDOW)])

    return gather_kernel(table, indices)


def _combine_kernel(*refs, alpha):
    y_refs = refs[:TOP_K]
    x_ref, fsh_ref, gate_ref, mod_ref, lng_ref, lnb_ref, o_ref = refs[TOP_K:]
    d = D_MODEL
    t = x_ref.shape[0]
    gate_rows = gate_ref[...]
    pad = jnp.zeros((LANES - TOP_K, t), F32)
    gate_cols = jnp.concatenate([gate_rows, pad], axis=0).T
    f_lo = fsh_ref[:, 0:PACK_W]
    f_hi = fsh_ref[:, PACK_W:]
    for k in range(TOP_K):
        y_lo, y_hi = _unpack_bf16_pairs(y_refs[k][...])
        f_lo = f_lo + gate_cols[:, k:k + 1] * y_lo
        f_hi = f_hi + gate_cols[:, k:k + 1] * y_hi
    f = jnp.concatenate([f_lo, f_hi], axis=1)
    z = alpha * x_ref[...] + mod_ref[:, 5 * d:6 * d] * f
    o_ref[...] = _layer_norm_rows(z) * lng_ref[...] + lnb_ref[...]


def _combine_call(y_tok, x, fsh, gate, mod3, ln_g, ln_b, *, tiles_per_batch, alpha, drop_context):
    r, d = x.shape
    t = ROW_TILE
    nt = r // t
    if drop_context:
        per_batch = tiles_per_batch - 1
        src = lambda i: (i // per_batch) * tiles_per_batch + i % per_batch
        n_tiles = nt // tiles_per_batch * per_batch
    else:
        src = lambda i: i
        n_tiles = nt
    row = lambda i: (src(i), 0)
    col = lambda i: (0, src(i))
    const = lambda i: (0, 0)
    kern = functools.partial(_combine_kernel, alpha=alpha)
    y_specs = [pl.BlockSpec((t, PACK_W), functools.partial(lambda k, i: (k * nt + src(i), 0), k))
               for k in range(TOP_K)]
    return pl.pallas_call(
        kern,
        grid=(n_tiles,),
        in_specs=y_specs + [
            pl.BlockSpec((t, d), row),
            pl.BlockSpec((t, d), row),
            pl.BlockSpec((TOP_K, t), col),
            pl.BlockSpec((None, 1, 6 * d), lambda i: (_mod_row(src(i), tiles_per_batch), 0, 0)),
            pl.BlockSpec((1, d), const),
            pl.BlockSpec((1, d), const),
        ],
        out_specs=pl.BlockSpec((t, d), lambda i: (i, 0)),
        out_shape=jax.ShapeDtypeStruct((n_tiles * t, d), F32),
        compiler_params=_cparams("arbitrary"),
        name="moe_combine",
    )(*([y_tok] * TOP_K), x, fsh, gate, mod3, ln_g, ln_b)


def _rope_tables(seq):
    rows = seq // GRID_W
    row = jnp.repeat(jnp.arange(rows, dtype=F32), GRID_W)
    col = jnp.tile(jnp.arange(GRID_W, dtype=F32), rows)
    nf = DA_DIM // 4
    freqs = ROPE_BASE ** (-jnp.arange(nf, dtype=F32) / nf)
    cr, sr = jnp.cos(row[:, None] * freqs), jnp.sin(row[:, None] * freqs)
    cc, sc = jnp.cos(col[:, None] * freqs), jnp.sin(col[:, None] * freqs)
    c64 = jnp.concatenate([cr, cr, cc, cc], axis=1)
    s64 = jnp.concatenate([-sr, sr, -sc, sc], axis=1)
    c = jnp.concatenate([jnp.tile(c64, (1, 2)), jnp.ones((CTX_LEN, LANES), F32)], axis=0)
    s = jnp.concatenate([jnp.tile(s64, (1, 2)), jnp.zeros((CTX_LEN, LANES), F32)], axis=0)
    return c, s


def kernel(x, c, ctx, c_ctx, w_mod, b_mod, w_in, w_out, diff_lambda, pool_w, pool_scale, ret_log_decay, ln_g, ln_b,
           w_router, router_bias, w_expert_gate_up, w_expert_down, w_shared_gate_up, w_shared_down):
    batch, seq, d = x.shape
    depth = w_mod.shape[0]
    assert d == D_MODEL and ctx.shape[1] == CTX_LEN == ROW_TILE and batch == 2
    assert seq % ROW_TILE == 0 and seq % GRID_W == 0 and w_in.shape[-1] == IN_WIDTH
    rows_per_batch = seq + CTX_LEN
    tiles_per_batch = rows_per_batch // ROW_TILE
    r = batch * rows_per_batch
    alpha = (2.0 * depth) ** 0.25

    xa = jnp.concatenate([x, ctx], axis=1).reshape(r, d)
    cvec = jnp.zeros((8, d), F32).at[0:batch].set(c).at[batch].set(c_ctx)
    mod_all = _mod_call(cvec, w_mod, b_mod)
    rope_c, rope_s = _rope_tables(seq)

    n_sorted = r * TOP_K + N_EXPERTS * EXPERT_BLOCK
    n_blocks = n_sorted // EXPERT_BLOCK

    for l in range(depth):
        lambda_init = 0.8 - 0.6 * math.exp(-0.3 * l)
        mod3 = mod_all[l].reshape(8, 1, 6 * d)
        lng = ln_g[l].reshape(2, 1, d)
        lnb = ln_b[l].reshape(2, 1, d)

        w_in_bf = w_in[l].astype(BF16)
        w_vt_bf = w_in_bf[:, QK_WIDTH:QK_WIDTH + DA_WIDTH].T
        qk, vda, u, rqkv, rg = _inproj_call(xa, mod3, w_in_bf, w_vt_bf, rope_c, rope_s, tiles_per_batch)
        da = _attn_call(diff_lambda[l], qk, vda, batch=batch, rows_per_batch=rows_per_batch, seq=seq,
                        lambda_init=lambda_init)
        o_f, o_b = _ret_call(ret_log_decay[l], rqkv, batch=batch, rows_per_batch=rows_per_batch, seq=seq)
        pool_bd = jnp.zeros((POOL_WIDTH, POOL_WIDTH), F32)
        for gi in range(len(POOL_WINDOWS)):
            sl = slice(gi * POOL_GROUP, (gi + 1) * POOL_GROUP)
            pool_bd = pool_bd.at[sl, sl].set(pool_w[l, gi])
        xa = _mixout_call(xa, da, u, o_f, o_b, rg, mod3, w_out[l].astype(BF16), pool_bd.astype(BF16),
                          pool_scale[l].reshape(1, POOL_WIDTH), lng[0], lnb[0],
                          tiles_per_batch=tiles_per_batch, seq=seq, alpha=alpha)

        wr_t = w_router[l].T
        wr_hi = wr_t.astype(BF16)
        wr_lo = (wr_t - wr_hi.astype(F32)).astype(BF16)
        tokp, idx, gate, rank, cnt, fsh = _router_call(
            xa, mod3, wr_hi, wr_lo, router_bias[l].reshape(N_EXPERTS, 1),
            w_shared_gate_up[l].astype(BF16), w_shared_down[l].astype(BF16), tiles_per_batch=tiles_per_batch)
        counts = cnt[:, 0]
        padded = (counts + EXPERT_BLOCK - 1) // EXPERT_BLOCK * EXPERT_BLOCK
        pad_end = jnp.cumsum(padded)
        offs = pad_end - padded
        expert_ids = jnp.arange(N_EXPERTS, dtype=jnp.int32)
        blk_row = jnp.arange(n_blocks, dtype=jnp.int32) * EXPERT_BLOCK
        block_expert = jnp.minimum(jnp.sum(pad_end[None, :] <= blk_row[:, None], axis=1), N_EXPERTS - 1)
        n_used = pad_end[-1:] // EXPERT_BLOCK
        used = counts > 0
        ordinal = jnp.cumsum(used) - 1
        hit = used[None, :] & (ordinal[None, :] == expert_ids[:, None])
        used_expert = jnp.sum(jnp.where(hit, expert_ids[None, :], 0), axis=1)
        n_used_experts = jnp.sum(used)[None]
        block_ordinal = ordinal[block_expert]
        slot = jnp.arange(EXPERT_BLOCK, dtype=jnp.int32)[None, :]
        first_pad = (padded - EXPERT_BLOCK)[:, None] + slot
        is_pad = (first_pad >= counts[:, None]) & (padded[:, None] > 0)
        spare = n_sorted + jnp.arange(N_EXPERTS * EXPERT_BLOCK, dtype=jnp.int32).reshape(N_EXPERTS, EXPERT_BLOCK)
        pad_rows = jnp.where(is_pad, offs[:, None] + first_pad, spare).reshape(N_EXPERTS * EXPERT_BLOCK)
        i32 = lambda a: a.astype(jnp.int32)

        dest = _dest_call(idx, rank, i32(offs).reshape(N_EXPERTS, 1))
        dest_flat = dest.reshape(TOP_K * r)
        xs = _sc_dispatch(tokp, dest_flat, i32(pad_rows), n_sorted)
        ys = _expert_call(i32(block_expert), i32(n_used), i32(block_ordinal), i32(used_expert),
                          i32(n_used_experts), xs, w_expert_gate_up, w_expert_down, l)
        y_tok = _sc_gather_rows(ys, dest_flat)
        xa = _combine_call(y_tok, xa, fsh, gate, mod3, lng[1], lnb[1], tiles_per_batch=tiles_per_batch,
                           alpha=alpha, drop_context=(l == depth - 1))

    return xa.reshape(batch, seq, d)
```

```python
import functools
import math

import jax
import jax.numpy as jnp
from jax import lax
from jax.experimental import pallas as pl
from jax.experimental.pallas import tpu as pltpu
from jax.experimental.pallas import tpu_sc as plsc

F32 = jnp.float32
BF16 = jnp.bfloat16
HIGHEST = lax.Precision.HIGHEST

D_MODEL = 1024
CTX_LEN = 256
GRID_W = 64
DA_HEADS = 4
DA_DIM = 64
DA_VDIM = 2 * DA_DIM
DA_WIDTH = DA_HEADS * DA_VDIM
ROPE_BASE = 10000.0
POOL_WINDOWS = (2, 4, 8, 16)
POOL_GROUP = 64
POOL_WIDTH = len(POOL_WINDOWS) * POOL_GROUP
POOL_HALO = 8
RET_HEADS = 4
RET_DK = 64
RET_WIDTH = RET_HEADS * RET_DK
RET_CHUNK = 128
QK_WIDTH = 2 * DA_HEADS * 2 * DA_DIM
IN_WIDTH = QK_WIDTH + DA_WIDTH + POOL_WIDTH + 4 * RET_WIDTH
N_EXPERTS = 256
TOP_K = 8
N_GROUPS = 8
GROUP_SIZE = N_EXPERTS // N_GROUPS
TOPK_GROUPS = 4
EXPERT_HIDDEN = 256
ROUTED_SCALE = 2.5
LN_EPS = 1e-6
RMS_EPS = 1e-5

LANES = 128
ROW_TILE = 256
DEST_STEPS = 4
ATTN_Q_TILE = 256
ATTN_K_CHUNK = 256
ATTN_UNROLL = 16
SC_NUM_CORES = 2
SC_NUM_SUBCORES = 16
SC_GATHER_WINDOW = 128
EXPERT_BLOCK = 256
SC_SPARE_ROWS = N_EXPERTS * EXPERT_BLOCK
PACK_W = D_MODEL // 2
VMEM_LIMIT = 56 * 1024 * 1024


def _cparams(*sem):
    return pltpu.CompilerParams(dimension_semantics=sem, vmem_limit_bytes=VMEM_LIMIT)


def _sigmoid(x):
    return 1.0 / (1.0 + jnp.exp(-x))


def _layer_norm_rows(x):
    mu = jnp.mean(x, axis=-1, keepdims=True)
    xc = x - mu
    var = jnp.mean(xc * xc, axis=-1, keepdims=True)
    return xc * lax.rsqrt(var + LN_EPS)


def _pack_bf16_pairs(x):
    half = x.shape[1] // 2
    bits = pltpu.bitcast(x.astype(BF16).astype(F32), jnp.uint32)
    word = lax.shift_right_logical(bits[:, 0:half], jnp.uint32(16)) | (bits[:, half:] & jnp.uint32(0xFFFF0000))
    return pltpu.bitcast(word, jnp.int32)


def _unpack_bf16_pairs(packed):
    word = pltpu.bitcast(packed, jnp.uint32)
    lo = pltpu.bitcast(lax.shift_left(word, jnp.uint32(16)), F32)
    hi = pltpu.bitcast(word & jnp.uint32(0xFFFF0000), F32)
    return lo, hi


def _mod_row(i, tiles_per_batch):
    return jnp.where(i % tiles_per_batch == tiles_per_batch - 1, 2, i // tiles_per_batch)


def _mod_kernel(c_ref, w_ref, b_ref, o_ref):
    c = c_ref[...]
    s = c * _sigmoid(c)
    o_ref[...] = jnp.dot(s, w_ref[...], precision=HIGHEST, preferred_element_type=F32) + b_ref[...]


def _mod_call(cvec, w_mod, b_mod):
    depth, d, n = w_mod.shape
    tn = 1536
    return pl.pallas_call(
        _mod_kernel,
        grid=(depth, n // tn),
        in_specs=[
            pl.BlockSpec((8, d), lambda l, j: (0, 0)),
            pl.BlockSpec((None, d, tn), lambda l, j: (l, 0, j)),
            pl.BlockSpec((None, 1, tn), lambda l, j: (l, 0, j)),
        ],
        out_specs=pl.BlockSpec((None, 8, tn), lambda l, j: (l, 0, j)),
        out_shape=jax.ShapeDtypeStruct((depth, 8, n), F32),
        compiler_params=_cparams("arbitrary", "arbitrary"),
        name="mod",
    )(cvec, w_mod, b_mod.reshape(depth, 1, n))


def _inproj_kernel(x_ref, mod_ref, w_ref, wvt_ref, ct_ref, st_ref, qk_ref, vt_ref, u_ref, r_ref, g_ref):
    d = D_MODEL
    xn = _layer_norm_rows(x_ref[...])
    h = (xn * (1.0 + mod_ref[:, d:2 * d]) + mod_ref[:, 0:d]).astype(BF16)

    a = jnp.dot(h, w_ref[:, 0:QK_WIDTH], preferred_element_type=F32)
    lane = lax.broadcasted_iota(jnp.int32, (a.shape[0], LANES), 1)
    first_half = (lane % 32) < 16
    ct = ct_ref[...]
    st = st_ref[...]
    for s in range(QK_WIDTH // LANES):
        blk = a[:, s * LANES:(s + 1) * LANES]
        partner = jnp.where(first_half, pltpu.roll(blk, LANES - 16, 1), pltpu.roll(blk, 16, 1))
        rot = blk * ct + partner * st
        if s < QK_WIDTH // LANES // 2:
            rot = rot * (DA_DIM ** -0.5 * math.log2(math.e))
        qk_ref[:, s * LANES:(s + 1) * LANES] = rot.astype(BF16)

    vt_ref[...] = lax.dot_general(wvt_ref[...], h, (((1,), (1,)), ((), ())),
                                  preferred_element_type=F32).astype(BF16)
    o = QK_WIDTH + DA_WIDTH
    u_ref[...] = jnp.dot(h, w_ref[:, o:o + POOL_WIDTH], preferred_element_type=F32)
    o += POOL_WIDTH
    r = jnp.dot(h, w_ref[:, o:o + 3 * RET_WIDTH], preferred_element_type=F32)
    r_ref[:, 0:RET_WIDTH] = r[:, 0:RET_WIDTH].astype(BF16)
    r_ref[:, RET_WIDTH:2 * RET_WIDTH] = (r[:, RET_WIDTH:2 * RET_WIDTH] * (RET_DK ** -0.5)).astype(BF16)
    r_ref[:, 2 * RET_WIDTH:] = r[:, 2 * RET_WIDTH:].astype(BF16)
    o += 3 * RET_WIDTH
    g_ref[...] = jnp.dot(h, w_ref[:, o:o + RET_WIDTH], preferred_element_type=F32)


def _inproj_call(x, mod3, w_in_bf, w_vt_bf, rope_c, rope_s, tiles_per_batch):
    r, d = x.shape
    t = ROW_TILE
    nt = r // t
    row = lambda i: (i, 0)
    return pl.pallas_call(
        _inproj_kernel,
        grid=(nt,),
        in_specs=[
            pl.BlockSpec((t, d), row),
            pl.BlockSpec((None, 1, 6 * d), lambda i: (_mod_row(i, tiles_per_batch), 0, 0)),
            pl.BlockSpec((d, IN_WIDTH), lambda i: (0, 0)),
            pl.BlockSpec((DA_WIDTH, d), lambda i: (0, 0)),
            pl.BlockSpec((t, LANES), lambda i: (i % tiles_per_batch, 0)),
            pl.BlockSpec((t, LANES), lambda i: (i % tiles_per_batch, 0)),
        ],
        out_specs=[
            pl.BlockSpec((t, QK_WIDTH), row),
            pl.BlockSpec((DA_WIDTH, t), lambda i: (0, i)),
            pl.BlockSpec((t, POOL_WIDTH), row),
            pl.BlockSpec((t, 3 * RET_WIDTH), row),
            pl.BlockSpec((t, RET_WIDTH), row),
        ],
        out_shape=[
            jax.ShapeDtypeStruct((r, QK_WIDTH), BF16),
            jax.ShapeDtypeStruct((DA_WIDTH, r), BF16),
            jax.ShapeDtypeStruct((r, POOL_WIDTH), F32),
            jax.ShapeDtypeStruct((r, 3 * RET_WIDTH), BF16),
            jax.ShapeDtypeStruct((r, RET_WIDTH), F32),
        ],
        compiler_params=_cparams("arbitrary"),
        name="inproj",
    )(x, mod3, w_in_bf, w_vt_bf, rope_c, rope_s)


def _attn_kernel(lam_ref, q_ref, k_ref, vt_ref, o_ref, s_ref, *, k_chunk, seq, lambda_init):
    q = q_ref[...]
    mq = q.shape[0]
    lane = lax.broadcasted_iota(jnp.int32, q.shape, 1)
    zero = jnp.zeros_like(q)
    q2 = jnp.concatenate([jnp.where(lane < DA_DIM, q, zero), jnp.where(lane >= DA_DIM, q, zero)], axis=0)
    qt = q2.astype(F32).T.astype(BF16)

    n_chunks = (seq + CTX_LEN) // k_chunk
    last = n_chunks - 1
    is_ctx_tile = pl.program_id(2) == pl.num_programs(2) - 1
    n_iters = jnp.where(is_ctx_tile, 0, last // ATTN_UNROLL)

    def score_chunk(c, m):
        off = pl.multiple_of(c * k_chunk, k_chunk)
        s = jnp.dot(k_ref[pl.ds(off, k_chunk), :], qt, preferred_element_type=F32)
        s_ref[c] = s
        return jnp.maximum(m, jnp.max(s, axis=0, keepdims=True))

    def pass1(it, m):
        for u in range(ATTN_UNROLL):
            m = score_chunk(it * ATTN_UNROLL + u, m)
        return m

    m = lax.fori_loop(0, n_iters, pass1, jnp.full((1, 2 * mq), -jnp.inf, F32))
    m = score_chunk(last, m)

    ones_rows = jnp.where(lax.broadcasted_iota(jnp.int32, (16, k_chunk), 0) == 0, 1.0, 0.0).astype(BF16)

    def value_chunk(c, acc):
        off = pl.multiple_of(c * k_chunk, k_chunk)
        vt = jnp.concatenate([vt_ref[:, pl.ds(off, k_chunk)], ones_rows], axis=0)
        p = jnp.exp2((s_ref[c] - m).astype(BF16))
        return acc + jnp.dot(vt, p, preferred_element_type=F32)

    def pass2(it, acc):
        for u in range(ATTN_UNROLL):
            acc = value_chunk(it * ATTN_UNROLL + u, acc)
        return acc

    acc = lax.fori_loop(0, n_iters, pass2, jnp.zeros((DA_VDIM + 16, 2 * mq), F32))
    acc = value_chunk(last, acc)
    l0, l1 = acc[DA_VDIM:DA_VDIM + 1, 0:mq], acc[DA_VDIM:DA_VDIM + 1, mq:]
    a0, a1 = acc[0:DA_VDIM, 0:mq], acc[0:DA_VDIM, mq:]

    lv = lam_ref[...]
    lam = (jnp.exp(jnp.sum(lv[0:1] * lv[1:2], axis=-1, keepdims=True))
           - jnp.exp(jnp.sum(lv[2:3] * lv[3:4], axis=-1, keepdims=True)) + lambda_init)
    o = a0 / l0 - lam * (a1 / l1)
    o = o * lax.rsqrt(jnp.mean(o * o, axis=0, keepdims=True) + RMS_EPS) * (1.0 - lambda_init)
    o_ref[...] = o.T.astype(BF16)


def _attn_call(lam_vec, qk, vda, *, batch, rows_per_batch, seq, lambda_init):
    tq = ATTN_Q_TILE
    assert seq % (ATTN_K_CHUNK * ATTN_UNROLL) == 0 and rows_per_batch - seq == CTX_LEN == tq == ATTN_K_CHUNK
    nq = rows_per_batch // tq
    kern = functools.partial(_attn_kernel, k_chunk=ATTN_K_CHUNK, seq=seq, lambda_init=lambda_init)
    return pl.pallas_call(
        kern,
        grid=(batch, DA_HEADS, nq),
        in_specs=[
            pl.BlockSpec((4, DA_DIM), lambda b, h, i: (0, 0)),
            pl.BlockSpec((tq, DA_VDIM), lambda b, h, i: (b * nq + i, h)),
            pl.BlockSpec((rows_per_batch, DA_VDIM), lambda b, h, i: (b, DA_HEADS + h)),
            pl.BlockSpec((DA_VDIM, rows_per_batch), lambda b, h, i: (h, b)),
        ],
        out_specs=pl.BlockSpec((tq, DA_VDIM), lambda b, h, i: (b * nq + i, h)),
        out_shape=jax.ShapeDtypeStruct((qk.shape[0], DA_WIDTH), BF16),
        scratch_shapes=[pltpu.VMEM((rows_per_batch // ATTN_K_CHUNK, ATTN_K_CHUNK, 2 * tq), F32)],
        compiler_params=_cparams("arbitrary", "arbitrary", "arbitrary"),
        name="diff_attn",
    )(lam_vec, qk, qk, vda)


def _ret_kernel(ld_ref, f_ref, b_ref, of_ref, ob_ref, dm_ref, qd_ref, kd_ref, cd_ref, st_ref):
    c = pl.program_id(1)
    ch = RET_CHUNK
    w = RET_WIDTH
    lane_head = lax.broadcasted_iota(jnp.int32, (1, w), 1) // RET_DK

    @pl.when(c == 0)
    def _():
        st_ref[...] = jnp.zeros_like(st_ref)
        ri = lax.broadcasted_iota(jnp.int32, (ch, ch), 0)
        ci = lax.broadcasted_iota(jnp.int32, (ch, ch), 1)
        rowf = lax.broadcasted_iota(jnp.int32, (ch, w), 0).astype(F32)
        for d in range(2):
            lg_lane = jnp.zeros((1, w), F32)
            for hh in range(RET_HEADS):
                lg = -jnp.exp(jnp.full((1, 1), ld_ref[d, hh], F32))
                lg_lane = jnp.where(lane_head == hh, lg, lg_lane)
                dist = ((ri - ci) if d == 0 else (ci - ri)).astype(F32)
                dm_ref[d, hh] = jnp.where(dist >= 0, jnp.exp(dist * lg), 0.0)
            if d == 0:
                qd_ref[d] = jnp.exp((rowf + 1.0) * lg_lane)
                kd_ref[d] = jnp.exp((ch - 1.0 - rowf) * lg_lane)
            else:
                qd_ref[d] = jnp.exp((ch - rowf) * lg_lane)
                kd_ref[d] = jnp.exp(rowf * lg_lane)
            cd_ref[d] = jnp.exp(float(ch) * lg_lane)

    rblk = lax.broadcasted_iota(jnp.int32, (w, w), 0) // RET_DK
    cblk = lax.broadcasted_iota(jnp.int32, (w, w), 1) // RET_DK
    for d, (src, dst) in enumerate(((f_ref, of_ref), (b_ref, ob_ref))):
        q = src[:, 0:w]
        k = src[:, w:2 * w]
        v = src[:, 2 * w:3 * w]
        st = st_ref[d]
        o = jnp.dot((q.astype(F32) * qd_ref[d]).astype(BF16), st.astype(BF16), preferred_element_type=F32)
        for hh in range(RET_HEADS):
            in_head = lane_head == hh
            qm = jnp.where(in_head, q, jnp.zeros_like(q))
            s = lax.dot_general(qm, k, (((1,), (1,)), ((), ())), preferred_element_type=F32)
            intra = (s * dm_ref[d, hh]).astype(BF16)
            o = o + jnp.where(in_head, jnp.dot(intra, v, preferred_element_type=F32), 0.0)
        dst[...] = o
        kk_t = (k.astype(F32) * kd_ref[d]).T.astype(BF16)
        upd = jnp.dot(kk_t, v, preferred_element_type=F32)
        st_ref[d] = jnp.where(rblk == cblk, st * cd_ref[d] + upd, 0.0)


def _ret_call(log_decay, rqkv, *, batch, rows_per_batch, seq):
    ch = RET_CHUNK
    nc = rows_per_batch // ch
    n_lat = seq // ch
    n_ctx = nc - n_lat

    def fwd(b, c):
        return (b * nc + jnp.where(c < n_ctx, n_lat + c, c - n_ctx), 0)

    def bwd(b, c):
        return (b * nc + nc - 1 - c, 0)

    w = RET_WIDTH
    return pl.pallas_call(
        _ret_kernel,
        grid=(batch, nc),
        in_specs=[
            pl.BlockSpec(memory_space=pltpu.SMEM),
            pl.BlockSpec((ch, 3 * w), fwd),
            pl.BlockSpec((ch, 3 * w), bwd),
        ],
        out_specs=[pl.BlockSpec((ch, w), fwd), pl.BlockSpec((ch, w), bwd)],
        out_shape=[jax.ShapeDtypeStruct((rqkv.shape[0], w), F32)] * 2,
        scratch_shapes=[
            pltpu.VMEM((2, RET_HEADS, ch, ch), F32),
            pltpu.VMEM((2, ch, w), F32),
            pltpu.VMEM((2, ch, w), F32),
            pltpu.VMEM((2, 1, w), F32),
            pltpu.VMEM((2, w, w), F32),
        ],
        compiler_params=_cparams("arbitrary", "arbitrary"),
        name="retention",
    )(log_decay, rqkv, rqkv)


def _mixout_kernel(x_ref, da_ref, u_ref, up_ref, un_ref, of_ref, ob_ref, rg_ref, mod_ref, wo_ref, pw_ref,
                   ps_ref, lng_ref, lnb_ref, o_ref, *, tiles_per_batch, seq, alpha):
    d = D_MODEL
    t = x_ref.shape[0]
    i = pl.program_id(0)
    j = i % tiles_per_batch
    is_ctx = j == tiles_per_batch - 1
    stream_len = jnp.where(is_ctx, CTX_LEN, seq)
    p0 = jnp.where(is_ctx, 0, j * t)

    u = u_ref[...]
    prev = jnp.where(p0 > 0, up_ref[...], 0.0)
    nxt = jnp.where(p0 + t < stream_len, un_ref[...], 0.0)
    ext = jnp.concatenate([prev, u, nxt], axis=0)
    n = t + 2 * POOL_HALO
    a2 = ext + pltpu.roll(ext, 1, 0)
    a4 = pltpu.roll(a2, 1, 0) + pltpu.roll(a2, n - 1, 0)
    a8 = pltpu.roll(a4, 2, 0) + pltpu.roll(a4, n - 2, 0)
    a16 = pltpu.roll(a8, 4, 0) + pltpu.roll(a8, n - 4, 0)
    pos = p0 + lax.broadcasted_iota(jnp.int32, (t, POOL_WIDTH), 0)
    group = lax.broadcasted_iota(jnp.int32, (1, POOL_WIDTH), 1) // POOL_GROUP
    mean = jnp.zeros((t, POOL_WIDTH), F32)
    for gi, (wnd, asum) in enumerate(zip(POOL_WINDOWS, (a2, a4, a8, a16))):
        cnt = jnp.minimum(pos + wnd // 2, stream_len) - jnp.maximum(pos - wnd // 2, 0)
        mean = jnp.where(group == gi, asum[POOL_HALO:POOL_HALO + t] / cnt.astype(F32), mean)
    pool = jnp.dot((mean - u).astype(BF16), pw_ref[...], preferred_element_type=F32) * ps_ref[...]

    o = of_ref[...] + ob_ref[...]
    head = lax.broadcasted_iota(jnp.int32, (1, RET_WIDTH), 1) // RET_DK

    def head_mean(val):
        out = jnp.zeros_like(val)
        for hh in range(RET_HEADS):
            m = jnp.sum(jnp.where(head == hh, val, 0.0), axis=-1, keepdims=True) * (1.0 / RET_DK)
            out = jnp.where(head == hh, m, out)
        return out

    oc = o - head_mean(o)
    rn = oc * lax.rsqrt(head_mean(oc * oc) + LN_EPS)
    g = rg_ref[...]
    ret = rn * (g * _sigmoid(g))

    y = jnp.dot(da_ref[...], wo_ref[0:DA_WIDTH, :], preferred_element_type=F32)
    y = y + jnp.dot(pool.astype(BF16), wo_ref[DA_WIDTH:DA_WIDTH + POOL_WIDTH, :], preferred_element_type=F32)
    y = y + jnp.dot(ret.astype(BF16), wo_ref[DA_WIDTH + POOL_WIDTH:, :], preferred_element_type=F32)
    z = alpha * x_ref[...] + mod_ref[:, 2 * d:3 * d] * y
    o_ref[...] = _layer_norm_rows(z) * lng_ref[...] + lnb_ref[...]


def _mixout_call(x, da, u, o_f, o_b, rg, mod3, w_out_bf, pool_bd, pool_scale, ln_g, ln_b, *, tiles_per_batch, seq,
                 alpha):
    r, d = x.shape
    t = ROW_TILE
    nt = r // t
    hb = t // POOL_HALO
    n_halo_blocks = r // POOL_HALO
    row = lambda i: (i, 0)
    const = lambda i: (0, 0)
    kern = functools.partial(_mixout_kernel, tiles_per_batch=tiles_per_batch, seq=seq, alpha=alpha)
    return pl.pallas_call(
        kern,
        grid=(nt,),
        in_specs=[
            pl.BlockSpec((t, d), row),
            pl.BlockSpec((t, DA_WIDTH), row),
            pl.BlockSpec((t, POOL_WIDTH), row),
            pl.BlockSpec((POOL_HALO, POOL_WIDTH), lambda i: (jnp.maximum(i * hb - 1, 0), 0)),
            pl.BlockSpec((POOL_HALO, POOL_WIDTH), lambda i: (jnp.minimum((i + 1) * hb, n_halo_blocks - 1), 0)),
            pl.BlockSpec((t, RET_WIDTH), row),
            pl.BlockSpec((t, RET_WIDTH), row),
            pl.BlockSpec((t, RET_WIDTH), row),
            pl.BlockSpec((None, 1, 6 * d), lambda i: (_mod_row(i, tiles_per_batch), 0, 0)),
            pl.BlockSpec((d, d), const),
            pl.BlockSpec((POOL_WIDTH, POOL_WIDTH), const),
            pl.BlockSpec((1, POOL_WIDTH), const),
            pl.BlockSpec((1, d), const),
            pl.BlockSpec((1, d), const),
        ],
        out_specs=pl.BlockSpec((t, d), row),
        out_shape=jax.ShapeDtypeStruct((r, d), F32),
        compiler_params=_cparams("arbitrary"),
        name="mixer_out",
    )(x, da, u, u, u, o_f, o_b, rg, mod3, w_out_bf, pool_bd, pool_scale, ln_g, ln_b)


def _router_kernel(x_ref, mod_ref, wrh_ref, wrl_ref, bias_ref, wsgu_ref, wsdn_ref,
                   tokp_ref, idx_ref, gate_ref, rank_ref, cnt_ref, fsh_ref, carry_ref):
    d = D_MODEL
    t = x_ref.shape[0]
    ne = N_EXPERTS
    neg = -jnp.inf

    @pl.when(pl.program_id(0) == 0)
    def _():
        carry_ref[...] = jnp.zeros_like(carry_ref)

    tok = _layer_norm_rows(x_ref[...]) * (1.0 + mod_ref[:, 4 * d:5 * d]) + mod_ref[:, 3 * d:4 * d]
    tok_hi = tok.astype(BF16)
    tok_lo = (tok - tok_hi.astype(F32)).astype(BF16)

    tokp_ref[...] = _pack_bf16_pairs(tok)

    hs = jnp.dot(tok_hi, wsgu_ref[...], preferred_element_type=F32)
    gs, us = hs[:, 0:EXPERT_HIDDEN], hs[:, EXPERT_HIDDEN:]
    fsh_ref[...] = jnp.dot((gs * _sigmoid(gs) * us).astype(BF16), wsdn_ref[...], preferred_element_type=F32)

    nt_dims = (((1,), (1,)), ((), ()))
    logits = (lax.dot_general(wrh_ref[...], tok_hi, nt_dims, preferred_element_type=F32)
              + lax.dot_general(wrh_ref[...], tok_lo, nt_dims, preferred_element_type=F32)
              + lax.dot_general(wrl_ref[...], tok_hi, nt_dims, preferred_element_type=F32))
    scores = _sigmoid(logits)
    biased = scores + bias_ref[...]

    gidx = lax.broadcasted_iota(jnp.int32, (GROUP_SIZE, t), 0)
    blocks, gscores = [], []
    for g in range(N_GROUPS):
        blk = biased[g * GROUP_SIZE:(g + 1) * GROUP_SIZE, :]
        m1 = jnp.max(blk, axis=0, keepdims=True)
        first = jnp.min(jnp.where(blk == m1, gidx, GROUP_SIZE), axis=0, keepdims=True)
        m2 = jnp.max(jnp.where(gidx == first, neg, blk), axis=0, keepdims=True)
        blocks.append(blk)
        gscores.append(m1 + m2)

    keep = [jnp.zeros((1, t), F32) for _ in range(N_GROUPS)]
    for _ in range(TOPK_GROUPS):
        m = gscores[0]
        for gs_ in gscores[1:]:
            m = jnp.maximum(m, gs_)
        found = jnp.zeros((1, t), F32)
        for g in range(N_GROUPS):
            hit = jnp.where(gscores[g] == m, 1.0 - found, 0.0)
            found = found + hit
            keep[g] = keep[g] + hit
            gscores[g] = jnp.where(hit > 0.0, neg, gscores[g])
    masked = jnp.concatenate([jnp.where(keep[g] > 0.0, blocks[g], neg) for g in range(N_GROUPS)], axis=0)

    ei = lax.broadcasted_iota(jnp.int32, (ne, t), 0)
    cur = masked
    onehot = jnp.zeros((ne, t), F32)
    idxs, gates = [], []
    for _ in range(TOP_K):
        m = jnp.max(cur, axis=0, keepdims=True)
        ii = jnp.min(jnp.where(cur == m, ei, ne), axis=0, keepdims=True)
        sel = ei == ii
        idxs.append(ii)
        gates.append(jnp.sum(jnp.where(sel, scores, 0.0), axis=0, keepdims=True))
        onehot = jnp.where(sel, 1.0, onehot)
        cur = jnp.where(sel, neg, cur)
    gsum = gates[0]
    for gk in gates[1:]:
        gsum = gsum + gk
    for k in range(TOP_K):
        idx_ref[k:k + 1, :] = idxs[k]
        gate_ref[k:k + 1, :] = gates[k] / gsum * ROUTED_SCALE

    ti = lax.broadcasted_iota(jnp.int32, (t, t), 0)
    tj = lax.broadcasted_iota(jnp.int32, (t, t), 1)
    before = jnp.where(ti < tj, 1.0, 0.0).astype(BF16)
    prefix = jnp.dot(onehot.astype(BF16), before, preferred_element_type=F32) + carry_ref[:, 0:1]
    for k in range(TOP_K):
        rank_k = jnp.sum(jnp.where(ei == idxs[k], prefix, 0.0), axis=0, keepdims=True)
        rank_ref[k:k + 1, :] = rank_k.astype(jnp.int32)
    carry_ref[...] = carry_ref[...] + jnp.sum(onehot, axis=1, keepdims=True)
    cnt_ref[...] = carry_ref[...].astype(jnp.int32)


def _router_call(x, mod3, wr_hi, wr_lo, bias_col, ws_gu_bf, ws_dn_bf, *, tiles_per_batch):
    r, d = x.shape
    t = ROW_TILE
    nt = r // t
    row = lambda i: (i, 0)
    col = lambda i: (0, i)
    const = lambda i: (0, 0)
    return pl.pallas_call(
        _router_kernel,
        grid=(nt,),
        in_specs=[
            pl.BlockSpec((t, d), row),
            pl.BlockSpec((None, 1, 6 * d), lambda i: (_mod_row(i, tiles_per_batch), 0, 0)),
            pl.BlockSpec((N_EXPERTS, d), const),
            pl.BlockSpec((N_EXPERTS, d), const),
            pl.BlockSpec((N_EXPERTS, 1), const),
            pl.BlockSpec((d, 2 * EXPERT_HIDDEN), const),
            pl.BlockSpec((EXPERT_HIDDEN, d), const),
        ],
        out_specs=[
            pl.BlockSpec((t, PACK_W), row),
            pl.BlockSpec((TOP_K, t), col),
            pl.BlockSpec((TOP_K, t), col),
            pl.BlockSpec((TOP_K, t), col),
            pl.BlockSpec((N_EXPERTS, LANES), const),
            pl.BlockSpec((t, d), row),
        ],
        out_shape=[
            jax.ShapeDtypeStruct((r, PACK_W), jnp.int32),
            jax.ShapeDtypeStruct((TOP_K, r), jnp.int32),
            jax.ShapeDtypeStruct((TOP_K, r), F32),
            jax.ShapeDtypeStruct((TOP_K, r), jnp.int32),
            jax.ShapeDtypeStruct((N_EXPERTS, LANES), jnp.int32),
            jax.ShapeDtypeStruct((r, d), F32),
        ],
        scratch_shapes=[pltpu.VMEM((N_EXPERTS, LANES), F32)],
        compiler_params=_cparams("arbitrary"),
        name="router",
    )(x, mod3, wr_hi, wr_lo, bias_col, ws_gu_bf, ws_dn_bf)


def _dest_kernel(idx_ref, rank_ref, offs_ref, dest_ref):
    t = idx_ref.shape[1]
    ei = lax.broadcasted_iota(jnp.int32, (N_EXPERTS, t), 0)
    offs = offs_ref[...].astype(F32)
    for k in range(TOP_K):
        start = jnp.sum(jnp.where(ei == idx_ref[k:k + 1, :], offs, 0.0), axis=0, keepdims=True)
        dest_ref[k:k + 1, :] = start.astype(jnp.int32) + rank_ref[k:k + 1, :]


def _dest_call(idx, rank, offs_col):
    r = idx.shape[1]
    t = r // DEST_STEPS
    assert r % DEST_STEPS == 0 and t % LANES == 0
    col = lambda i: (0, i)
    return pl.pallas_call(
        _dest_kernel,
        grid=(r // t,),
        in_specs=[pl.BlockSpec((TOP_K, t), col), pl.BlockSpec((TOP_K, t), col),
                  pl.BlockSpec((N_EXPERTS, 1), lambda i: (0, 0))],
        out_specs=pl.BlockSpec((TOP_K, t), col),
        out_shape=jax.ShapeDtypeStruct((TOP_K, r), jnp.int32),
        compiler_params=_cparams("arbitrary"),
        name="moe_dest",
    )(idx, rank, offs_col)


def _sc_dispatch(tokp, dest_flat, pad_rows, n_sorted):
    r, width = tokp.shape
    win = SC_GATHER_WINDOW
    workers = SC_NUM_CORES * SC_NUM_SUBCORES
    token_windows = r // win
    n_pad_windows = pad_rows.shape[0] // win
    assert r % win == 0 and dest_flat.shape[0] == TOP_K * r and n_pad_windows % workers == 0
    windows_per_worker = -(-token_windows // workers)
    pads_per_worker = n_pad_windows // workers
    mesh = plsc.VectorSubcoreMesh(core_axis_name="core", subcore_axis_name="subcore", num_cores=SC_NUM_CORES,
                                  num_subcores=SC_NUM_SUBCORES)
    zero_rows = jnp.zeros((win, width), tokp.dtype)

    @functools.partial(
        pl.kernel, out_type=jax.ShapeDtypeStruct((n_sorted + SC_SPARE_ROWS, width), tokp.dtype), mesh=mesh,
        scratch_types=[pltpu.VMEM((win,), jnp.int32), pltpu.VMEM((win, width), tokp.dtype),
                       pltpu.SemaphoreType.DMA],
        name="moe_sc_dispatch")
    def dispatch_kernel(tok_hbm, dest_hbm, pad_hbm, zero_hbm, xs_hbm, idx_vmem, rows_vmem, sem):
        worker = lax.axis_index("subcore") * SC_NUM_CORES + lax.axis_index("core")

        @pl.loop(0, windows_per_worker)
        def _(j):
            window = j * workers + worker

            @pl.when(window < token_windows)
            def _():
                tok0 = window * win
                pltpu.sync_copy(tok_hbm.at[pl.ds(tok0, win)], rows_vmem)
                for k in range(TOP_K):
                    pltpu.sync_copy(dest_hbm.at[pl.ds(k * r + tok0, win)], idx_vmem)
                    pltpu.async_copy(rows_vmem, xs_hbm.at[idx_vmem], sem).wait()

        pltpu.sync_copy(zero_hbm, rows_vmem)

        @pl.loop(0, pads_per_worker)
        def _(j):
            off = (worker * pads_per_worker + j) * win
            pltpu.sync_copy(pad_hbm.at[pl.ds(off, win)], idx_vmem)
            pltpu.async_copy(rows_vmem, xs_hbm.at[idx_vmem], sem).wait()

    return dispatch_kernel(tokp, dest_flat, pad_rows, zero_rows)


def _expert_kernel(be_ref, nb_ref, ord_ref, ue_ref, nue_ref, xs_ref, wgu_hbm, wdn_hbm, ys_ref, wgu_f32, wdn_f32,
                   wgu_bf, wdn_bf, sems, *, layer):
    j = pl.program_id(0)

    def weight_copies(o):
        slot = o % 2
        e = ue_ref[o]
        return (pltpu.make_async_copy(wgu_hbm.at[layer, e], wgu_f32.at[slot], sems.at[0, slot]),
                pltpu.make_async_copy(wdn_hbm.at[layer, e], wdn_f32.at[slot], sems.at[1, slot]))

    def start_weights(o):
        @pl.when(o < nue_ref[0])
        def _():
            for cp in weight_copies(o):
                cp.start()

    @pl.when(j < nb_ref[0])
    def _():
        o = ord_ref[j]
        changed = jnp.logical_or(j == 0, be_ref[j] != be_ref[jnp.maximum(j - 1, 0)])

        @pl.when(j == 0)
        def _():
            start_weights(0)
            start_weights(1)

        @pl.when(changed)
        def _():
            for cp in weight_copies(o):
                cp.wait()
            slot = o % 2
            wgu_bf[...] = wgu_f32[slot].astype(BF16)
            wdn_bf[...] = wdn_f32[slot].astype(BF16)
            start_weights(o + 2)

        x_lo, x_hi = _unpack_bf16_pairs(xs_ref[...])
        h = (jnp.dot(x_lo.astype(BF16), wgu_bf[0:PACK_W, :], preferred_element_type=F32)
             + jnp.dot(x_hi.astype(BF16), wgu_bf[PACK_W:, :], preferred_element_type=F32))
        g, u = h[:, 0:EXPERT_HIDDEN], h[:, EXPERT_HIDDEN:]
        y = jnp.dot((g * _sigmoid(g) * u).astype(BF16), wdn_bf[...], preferred_element_type=F32)
        ys_ref[...] = _pack_bf16_pairs(y)


def _expert_call(block_expert, n_blocks_used, block_ordinal, used_expert, n_used_experts, xs, w_gu, w_dn, layer):
    n_rows = block_expert.shape[0] * EXPERT_BLOCK
    bm = EXPERT_BLOCK
    d = D_MODEL
    used_block = lambda j, be, nb, od, ue, nue: (jnp.minimum(j, nb[0] - 1), 0)
    grid_spec = pltpu.PrefetchScalarGridSpec(
        num_scalar_prefetch=5,
        grid=(n_rows // bm,),
        in_specs=[
            pl.BlockSpec((bm, PACK_W), used_block),
            pl.BlockSpec(memory_space=pl.ANY),
            pl.BlockSpec(memory_space=pl.ANY),
        ],
        out_specs=pl.BlockSpec((bm, PACK_W), used_block),
        scratch_shapes=[
            pltpu.VMEM((2, d, 2 * EXPERT_HIDDEN), F32),
            pltpu.VMEM((2, EXPERT_HIDDEN, d), F32),
            pltpu.VMEM((d, 2 * EXPERT_HIDDEN), BF16),
            pltpu.VMEM((EXPERT_HIDDEN, d), BF16),
            pltpu.SemaphoreType.DMA((2, 2)),
        ],
    )
    return pl.pallas_call(
        functools.partial(_expert_kernel, layer=layer),
        grid_spec=grid_spec,
        out_shape=jax.ShapeDtypeStruct((n_rows, PACK_W), jnp.int32),
        compiler_params=_cparams("arbitrary"),
        name="moe_experts",
    )(block_expert, n_blocks_used, block_ordinal, used_expert, n_used_experts, xs, w_gu, w_dn)


def _sc_gather_rows(table, indices):
    n = indices.shape[0]
    width = table.shape[1]
    workers = SC_NUM_CORES * SC_NUM_SUBCORES
    assert n % (SC_GATHER_WINDOW * workers) == 0
    per_worker = n // workers
    mesh = plsc.VectorSubcoreMesh(core_axis_name="core", subcore_axis_name="subcore", num_cores=SC_NUM_CORES,
                                  num_subcores=SC_NUM_SUBCORES)

    @functools.partial(
        pl.kernel, out_type=jax.ShapeDtypeStruct((n, width), table.dtype), mesh=mesh,
        scratch_types=[pltpu.VMEM((SC_GATHER_WINDOW,), jnp.int32),
                       pltpu.VMEM((SC_GATHER_WINDOW, width), table.dtype),
                       pltpu.SemaphoreType.DMA],
        name="moe_sc_gather")
    def gather_kernel(table_hbm, idx_hbm, out_hbm, idx_vmem, rows_vmem, sem):
        worker = lax.axis_index("subcore") * SC_NUM_CORES + lax.axis_index("core")
        base = worker * per_worker

        @pl.loop(0, per_worker // SC_GATHER_WINDOW)
        def _(w):
            off = base + w * SC_GATHER_WINDOW
            pltpu.sync_copy(idx_hbm.at[pl.ds(off, SC_GATHER_WINDOW)], idx_vmem)
            pltpu.async_copy(table_hbm.at[idx_vmem], rows_vmem, sem).wait()
            pltpu.sync_copy(rows_vmem, out_hbm.at[pl.ds(off, SC_GATHER_WINDOW)])

    return gather_kernel(table, indices)


def _combine_kernel(*refs, alpha):
    y_refs = refs[:TOP_K]
    x_ref, fsh_ref, gate_ref, mod_ref, lng_ref, lnb_ref, o_ref = refs[TOP_K:]
    d = D_MODEL
    t = x_ref.shape[0]
    gate_rows = gate_ref[...]
    pad = jnp.zeros((LANES - TOP_K, t), F32)
    gate_cols = jnp.concatenate([gate_rows, pad], axis=0).T
    f_lo = fsh_ref[:, 0:PACK_W]
    f_hi = fsh_ref[:, PACK_W:]
    for k in range(TOP_K):
        y_lo, y_hi = _unpack_bf16_pairs(y_refs[k][...])
        f_lo = f_lo + gate_cols[:, k:k + 1] * y_lo
        f_hi = f_hi + gate_cols[:, k:k + 1] * y_hi
    f = jnp.concatenate([f_lo, f_hi], axis=1)
    z = alpha * x_ref[...] + mod_ref[:, 5 * d:6 * d] * f
    o_ref[...] = _layer_norm_rows(z) * lng_ref[...] + lnb_ref[...]


def _combine_call(y_tok, x, fsh, gate, mod3, ln_g, ln_b, *, tiles_per_batch, alpha, drop_context):
    r, d = x.shape
    t = ROW_TILE
    nt = r // t
    if drop_context:
        per_batch = tiles_per_batch - 1
        src = lambda i: (i // per_batch) * tiles_per_batch + i % per_batch
        n_tiles = nt // tiles_per_batch * per_batch
    else:
        src = lambda i: i
        n_tiles = nt
    row = lambda i: (src(i), 0)
    col = lambda i: (0, src(i))
    const = lambda i: (0, 0)
    kern = functools.partial(_combine_kernel, alpha=alpha)
    y_specs = [pl.BlockSpec((t, PACK_W), functools.partial(lambda k, i: (k * nt + src(i), 0), k))
               for k in range(TOP_K)]
    return pl.pallas_call(
        kern,
        grid=(n_tiles,),
        in_specs=y_specs + [
            pl.BlockSpec((t, d), row),
            pl.BlockSpec((t, d), row),
            pl.BlockSpec((TOP_K, t), col),
            pl.BlockSpec((None, 1, 6 * d), lambda i: (_mod_row(src(i), tiles_per_batch), 0, 0)),
            pl.BlockSpec((1, d), const),
            pl.BlockSpec((1, d), const),
        ],
        out_specs=pl.BlockSpec((t, d), lambda i: (i, 0)),
        out_shape=jax.ShapeDtypeStruct((n_tiles * t, d), F32),
        compiler_params=_cparams("arbitrary"),
        name="moe_combine",
    )(*([y_tok] * TOP_K), x, fsh, gate, mod3, ln_g, ln_b)


def _rope_tables(seq):
    rows = seq // GRID_W
    row = jnp.repeat(jnp.arange(rows, dtype=F32), GRID_W)
    col = jnp.tile(jnp.arange(GRID_W, dtype=F32), rows)
    nf = DA_DIM // 4
    freqs = ROPE_BASE ** (-jnp.arange(nf, dtype=F32) / nf)
    cr, sr = jnp.cos(row[:, None] * freqs), jnp.sin(row[:, None] * freqs)
    cc, sc = jnp.cos(col[:, None] * freqs), jnp.sin(col[:, None] * freqs)
    c64 = jnp.concatenate([cr, cr, cc, cc], axis=1)
    s64 = jnp.concatenate([-sr, sr, -sc, sc], axis=1)
    c = jnp.concatenate([jnp.tile(c64, (1, 2)), jnp.ones((CTX_LEN, LANES), F32)], axis=0)
    s = jnp.concatenate([jnp.tile(s64, (1, 2)), jnp.zeros((CTX_LEN, LANES), F32)], axis=0)
    return c, s


def kernel(x, c, ctx, c_ctx, w_mod, b_mod, w_in, w_out, diff_lambda, pool_w, pool_scale, ret_log_decay, ln_g, ln_b,
           w_router, router_bias, w_expert_gate_up, w_expert_down, w_shared_gate_up, w_shared_down):
    batch, seq, d = x.shape
    depth = w_mod.shape[0]
    assert d == D_MODEL and ctx.shape[1] == CTX_LEN == ROW_TILE and batch == 2
    assert seq % ROW_TILE == 0 and seq % GRID_W == 0 and w_in.shape[-1] == IN_WIDTH
    rows_per_batch = seq + CTX_LEN
    tiles_per_batch = rows_per_batch // ROW_TILE
    r = batch * rows_per_batch
    alpha = (2.0 * depth) ** 0.25

    xa = jnp.concatenate([x, ctx], axis=1).reshape(r, d)
    cvec = jnp.zeros((8, d), F32).at[0:batch].set(c).at[batch].set(c_ctx)
    mod_all = _mod_call(cvec, w_mod, b_mod)
    rope_c, rope_s = _rope_tables(seq)

    n_sorted = r * TOP_K + N_EXPERTS * EXPERT_BLOCK
    n_blocks = n_sorted // EXPERT_BLOCK

    for l in range(depth):
        lambda_init = 0.8 - 0.6 * math.exp(-0.3 * l)
        mod3 = mod_all[l].reshape(8, 1, 6 * d)
        lng = ln_g[l].reshape(2, 1, d)
        lnb = ln_b[l].reshape(2, 1, d)

        w_in_bf = w_in[l].astype(BF16)
        w_vt_bf = w_in_bf[:, QK_WIDTH:QK_WIDTH + DA_WIDTH].T
        qk, vda, u, rqkv, rg = _inproj_call(xa, mod3, w_in_bf, w_vt_bf, rope_c, rope_s, tiles_per_batch)
        da = _attn_call(diff_lambda[l], qk, vda, batch=batch, rows_per_batch=rows_per_batch, seq=seq,
                        lambda_init=lambda_init)
        o_f, o_b = _ret_call(ret_log_decay[l], rqkv, batch=batch, rows_per_batch=rows_per_batch, seq=seq)
        pool_bd = jnp.zeros((POOL_WIDTH, POOL_WIDTH), F32)
        for gi in range(len(POOL_WINDOWS)):
            sl = slice(gi * POOL_GROUP, (gi + 1) * POOL_GROUP)
            pool_bd = pool_bd.at[sl, sl].set(pool_w[l, gi])
        xa = _mixout_call(xa, da, u, o_f, o_b, rg, mod3, w_out[l].astype(BF16), pool_bd.astype(BF16),
                          pool_scale[l].reshape(1, POOL_WIDTH), lng[0], lnb[0],
                          tiles_per_batch=tiles_per_batch, seq=seq, alpha=alpha)

        wr_t = w_router[l].T
        wr_hi = wr_t.astype(BF16)
        wr_lo = (wr_t - wr_hi.astype(F32)).astype(BF16)
        tokp, idx, gate, rank, cnt, fsh = _router_call(
            xa, mod3, wr_hi, wr_lo, router_bias[l].reshape(N_EXPERTS, 1),
            w_shared_gate_up[l].astype(BF16), w_shared_down[l].astype(BF16), tiles_per_batch=tiles_per_batch)
        counts = cnt[:, 0]
        padded = (counts + EXPERT_BLOCK - 1) // EXPERT_BLOCK * EXPERT_BLOCK
        pad_end = jnp.cumsum(padded)
        offs = pad_end - padded
        expert_ids = jnp.arange(N_EXPERTS, dtype=jnp.int32)
        blk_row = jnp.arange(n_blocks, dtype=jnp.int32) * EXPERT_BLOCK
        block_expert = jnp.minimum(jnp.sum(pad_end[None, :] <= blk_row[:, None], axis=1), N_EXPERTS - 1)
        n_used = pad_end[-1:] // EXPERT_BLOCK
        used = counts > 0
        ordinal = jnp.cumsum(used) - 1
        hit = used[None, :] & (ordinal[None, :] == expert_ids[:, None])
        used_expert = jnp.sum(jnp.where(hit, expert_ids[None, :], 0), axis=1)
        n_used_experts = jnp.sum(used)[None]
        block_ordinal = ordinal[block_expert]
        slot = jnp.arange(EXPERT_BLOCK, dtype=jnp.int32)[None, :]
        first_pad = (padded - EXPERT_BLOCK)[:, None] + slot
        is_pad = (first_pad >= counts[:, None]) & (padded[:, None] > 0)
        spare = n_sorted + jnp.arange(N_EXPERTS * EXPERT_BLOCK, dtype=jnp.int32).reshape(N_EXPERTS, EXPERT_BLOCK)
        pad_rows = jnp.where(is_pad, offs[:, None] + first_pad, spare).reshape(N_EXPERTS * EXPERT_BLOCK)
        i32 = lambda a: a.astype(jnp.int32)

        dest = _dest_call(idx, rank, i32(offs).reshape(N_EXPERTS, 1))
        dest_flat = dest.reshape(TOP_K * r)
        xs = _sc_dispatch(tokp, dest_flat, i32(pad_rows), n_sorted)
        ys = _expert_call(i32(block_expert), i32(n_used), i32(block_ordinal), i32(used_expert),
                          i32(n_used_experts), xs, w_expert_gate_up, w_expert_down, l)
        y_tok = _sc_gather_rows(ys, dest_flat)
        xa = _combine_call(y_tok, xa, fsh, gate, mod3, lng[1], lnb[1], tiles_per_batch=tiles_per_batch,
                           alpha=alpha, drop_context=(l == depth - 1))

    return xa.reshape(batch, seq, d)
```

```python
import functools
import math

import jax
import jax.numpy as jnp
from jax import lax
from jax.experimental import pallas as pl
from jax.experimental.pallas import tpu as pltpu
from jax.experimental.pallas import tpu_sc as plsc

F32 = jnp.float32
BF16 = jnp.bfloat16
HIGHEST = lax.Precision.HIGHEST

D_MODEL = 1024
CTX_LEN = 256
GRID_W = 64
DA_HEADS = 4
DA_DIM = 64
DA_VDIM = 2 * DA_DIM
DA_WIDTH = DA_HEADS * DA_VDIM
ROPE_BASE = 10000.0
POOL_WINDOWS = (2, 4, 8, 16)
POOL_GROUP = 64
POOL_WIDTH = len(POOL_WINDOWS) * POOL_GROUP
POOL_HALO = 8
RET_HEADS = 4
RET_DK = 64
RET_WIDTH = RET_HEADS * RET_DK
RET_CHUNK = 128
QK_WIDTH = 2 * DA_HEADS * 2 * DA_DIM
IN_WIDTH = QK_WIDTH + DA_WIDTH + POOL_WIDTH + 4 * RET_WIDTH
N_EXPERTS = 256
TOP_K = 8
N_GROUPS = 8
GROUP_SIZE = N_EXPERTS // N_GROUPS
TOPK_GROUPS = 4
EXPERT_HIDDEN = 256
ROUTED_SCALE = 2.5
LN_EPS = 1e-6
RMS_EPS = 1e-5

LANES = 128
ROW_TILE = 256
DEST_STEPS = 4
ATTN_Q_TILE = 256
ATTN_K_CHUNK = 256
ATTN_UNROLL = 16
SC_NUM_CORES = 2
SC_NUM_SUBCORES = 16
SC_GATHER_WINDOW = 128
EXPERT_BLOCK = 256
SC_SPARE_ROWS = N_EXPERTS * EXPERT_BLOCK
PACK_W = D_MODEL // 2
VMEM_LIMIT = 56 * 1024 * 1024


def _cparams(*sem):
    return pltpu.CompilerParams(dimension_semantics=sem, vmem_limit_bytes=VMEM_LIMIT)


def _sigmoid(x):
    return 1.0 / (1.0 + jnp.exp(-x))


def _layer_norm_rows(x):
    mu = jnp.mean(x, axis=-1, keepdims=True)
    xc = x - mu
    var = jnp.mean(xc * xc, axis=-1, keepdims=True)
    return xc * lax.rsqrt(var + LN_EPS)


def _pack_bf16_pairs(x):
    half = x.shape[1] // 2
    bits = pltpu.bitcast(x.astype(BF16).astype(F32), jnp.uint32)
    word = lax.shift_right_logical(bits[:, 0:half], jnp.uint32(16)) | (bits[:, half:] & jnp.uint32(0xFFFF0000))
    return pltpu.bitcast(word, jnp.int32)


def _unpack_bf16_pairs(packed):
    word = pltpu.bitcast(packed, jnp.uint32)
    lo = pltpu.bitcast(lax.shift_left(word, jnp.uint32(16)), F32)
    hi = pltpu.bitcast(word & jnp.uint32(0xFFFF0000), F32)
    return lo, hi


def _mod_row(i, tiles_per_batch):
    return jnp.where(i % tiles_per_batch == tiles_per_batch - 1, 2, i // tiles_per_batch)


def _mod_kernel(c_ref, w_ref, b_ref, o_ref):
    c = c_ref[...]
    s = c * _sigmoid(c)
    o_ref[...] = jnp.dot(s, w_ref[...], precision=HIGHEST, preferred_element_type=F32) + b_ref[...]


def _mod_call(cvec, w_mod, b_mod):
    depth, d, n = w_mod.shape
    tn = 1536
    return pl.pallas_call(
        _mod_kernel,
        grid=(depth, n // tn),
        in_specs=[
            pl.BlockSpec((8, d), lambda l, j: (0, 0)),
            pl.BlockSpec((None, d, tn), lambda l, j: (l, 0, j)),
            pl.BlockSpec((None, 1, tn), lambda l, j: (l, 0, j)),
        ],
        out_specs=pl.BlockSpec((None, 8, tn), lambda l, j: (l, 0, j)),
        out_shape=jax.ShapeDtypeStruct((depth, 8, n), F32),
        compiler_params=_cparams("arbitrary", "arbitrary"),
        name="mod",
    )(cvec, w_mod, b_mod.reshape(depth, 1, n))


def _inproj_kernel(x_ref, mod_ref, w_ref, wvt_ref, ct_ref, st_ref, qk_ref, vt_ref, u_ref, r_ref, g_ref):
    d = D_MODEL
    xn = _layer_norm_rows(x_ref[...])
    h = (xn * (1.0 + mod_ref[:, d:2 * d]) + mod_ref[:, 0:d]).astype(BF16)

    a = jnp.dot(h, w_ref[:, 0:QK_WIDTH], preferred_element_type=F32)
    lane = lax.broadcasted_iota(jnp.int32, (a.shape[0], LANES), 1)
    first_half = (lane % 32) < 16
    ct = ct_ref[...]
    st = st_ref[...]
    for s in range(QK_WIDTH // LANES):
        blk = a[:, s * LANES:(s + 1) * LANES]
        partner = jnp.where(first_half, pltpu.roll(blk, LANES - 16, 1), pltpu.roll(blk, 16, 1))
        rot = blk * ct + partner * st
        if s < QK_WIDTH // LANES // 2:
            rot = rot * (DA_DIM ** -0.5 * math.log2(math.e))
        qk_ref[:, s * LANES:(s + 1) * LANES] = rot.astype(BF16)

    vt_ref[...] = lax.dot_general(wvt_ref[...], h, (((1,), (1,)), ((), ())),
                                  preferred_element_type=F32).astype(BF16)
    o = QK_WIDTH + DA_WIDTH
    u_ref[...] = jnp.dot(h, w_ref[:, o:o + POOL_WIDTH], preferred_element_type=F32)
    o += POOL_WIDTH
    r = jnp.dot(h, w_ref[:, o:o + 3 * RET_WIDTH], preferred_element_type=F32)
    r_ref[:, 0:RET_WIDTH] = r[:, 0:RET_WIDTH].astype(BF16)
    r_ref[:, RET_WIDTH:2 * RET_WIDTH] = (r[:, RET_WIDTH:2 * RET_WIDTH] * (RET_DK ** -0.5)).astype(BF16)
    r_ref[:, 2 * RET_WIDTH:] = r[:, 2 * RET_WIDTH:].astype(BF16)
    o += 3 * RET_WIDTH
    g_ref[...] = jnp.dot(h, w_ref[:, o:o + RET_WIDTH], preferred_element_type=F32)


def _inproj_call(x, mod3, w_in_bf, w_vt_bf, rope_c, rope_s, tiles_per_batch):
    r, d = x.shape
    t = ROW_TILE
    nt = r // t
    row = lambda i: (i, 0)
    return pl.pallas_call(
        _inproj_kernel,
        grid=(nt,),
        in_specs=[
            pl.BlockSpec((t, d), row),
            pl.BlockSpec((None, 1, 6 * d), lambda i: (_mod_row(i, tiles_per_batch), 0, 0)),
            pl.BlockSpec((d, IN_WIDTH), lambda i: (0, 0)),
            pl.BlockSpec((DA_WIDTH, d), lambda i: (0, 0)),
            pl.BlockSpec((t, LANES), lambda i: (i % tiles_per_batch, 0)),
            pl.BlockSpec((t, LANES), lambda i: (i % tiles_per_batch, 0)),
        ],
        out_specs=[
            pl.BlockSpec((t, QK_WIDTH), row),
            pl.BlockSpec((DA_WIDTH, t), lambda i: (0, i)),
            pl.BlockSpec((t, POOL_WIDTH), row),
            pl.BlockSpec((t, 3 * RET_WIDTH), row),
            pl.BlockSpec((t, RET_WIDTH), row),
        ],
        out_shape=[
            jax.ShapeDtypeStruct((r, QK_WIDTH), BF16),
            jax.ShapeDtypeStruct((DA_WIDTH, r), BF16),
            jax.ShapeDtypeStruct((r, POOL_WIDTH), F32),
            jax.ShapeDtypeStruct((r, 3 * RET_WIDTH), BF16),
            jax.ShapeDtypeStruct((r, RET_WIDTH), F32),
        ],
        compiler_params=_cparams("arbitrary"),
        name="inproj",
    )(x, mod3, w_in_bf, w_vt_bf, rope_c, rope_s)


def _attn_kernel(lam_ref, q_ref, k_ref, vt_ref, o_ref, s_ref, *, k_chunk, seq, lambda_init):
    q = q_ref[...]
    mq = q.shape[0]
    lane = lax.broadcasted_iota(jnp.int32, q.shape, 1)
    zero = jnp.zeros_like(q)
    q2 = jnp.concatenate([jnp.where(lane < DA_DIM, q, zero), jnp.where(lane >= DA_DIM, q, zero)], axis=0)
    qt = q2.astype(F32).T.astype(BF16)

    n_chunks = (seq + CTX_LEN) // k_chunk
    last = n_chunks - 1
    is_ctx_tile = pl.program_id(2) == pl.num_programs(2) - 1
    n_iters = jnp.where(is_ctx_tile, 0, last // ATTN_UNROLL)

    def score_chunk(c, m):
        off = pl.multiple_of(c * k_chunk, k_chunk)
        s = jnp.dot(k_ref[pl.ds(off, k_chunk), :], qt, preferred_element_type=F32)
        s_ref[c] = s
        return jnp.maximum(m, jnp.max(s, axis=0, keepdims=True))

    def pass1(it, m):
        for u in range(ATTN_UNROLL):
            m = score_chunk(it * ATTN_UNROLL + u, m)
        return m

    m = lax.fori_loop(0, n_iters, pass1, jnp.full((1, 2 * mq), -jnp.inf, F32))
    m = score_chunk(last, m)

    ones_rows = jnp.where(lax.broadcasted_iota(jnp.int32, (16, k_chunk), 0) == 0, 1.0, 0.0).astype(BF16)

    def value_chunk(c, acc):
        off = pl.multiple_of(c * k_chunk, k_chunk)
        vt = jnp.concatenate([vt_ref[:, pl.ds(off, k_chunk)], ones_rows], axis=0)
        p = jnp.exp2((s_ref[c] - m).astype(BF16))
        return acc + jnp.dot(vt, p, preferred_element_type=F32)

    def pass2(it, acc):
        for u in range(ATTN_UNROLL):
            acc = value_chunk(it * ATTN_UNROLL + u, acc)
        return acc

    acc = lax.fori_loop(0, n_iters, pass2, jnp.zeros((DA_VDIM + 16, 2 * mq), F32))
    acc = value_chunk(last, acc)
    l0, l1 = acc[DA_VDIM:DA_VDIM + 1, 0:mq], acc[DA_VDIM:DA_VDIM + 1, mq:]
    a0, a1 = acc[0:DA_VDIM, 0:mq], acc[0:DA_VDIM, mq:]

    lv = lam_ref[...]
    lam = (jnp.exp(jnp.sum(lv[0:1] * lv[1:2], axis=-1, keepdims=True))
           - jnp.exp(jnp.sum(lv[2:3] * lv[3:4], axis=-1, keepdims=True)) + lambda_init)
    o = a0 / l0 - lam * (a1 / l1)
    o = o * lax.rsqrt(jnp.mean(o * o, axis=0, keepdims=True) + RMS_EPS) * (1.0 - lambda_init)
    o_ref[...] = o.T.astype(BF16)


def _attn_call(lam_vec, qk, vda, *, batch, rows_per_batch, seq, lambda_init):
    tq = ATTN_Q_TILE
    assert seq % (ATTN_K_CHUNK * ATTN_UNROLL) == 0 and rows_per_batch - seq == CTX_LEN == tq == ATTN_K_CHUNK
    nq = rows_per_batch // tq
    kern = functools.partial(_attn_kernel, k_chunk=ATTN_K_CHUNK, seq=seq, lambda_init=lambda_init)
    return pl.pallas_call(
        kern,
        grid=(batch, DA_HEADS, nq),
        in_specs=[
            pl.BlockSpec((4, DA_DIM), lambda b, h, i: (0, 0)),
            pl.BlockSpec((tq, DA_VDIM), lambda b, h, i: (b * nq + i, h)),
            pl.BlockSpec((rows_per_batch, DA_VDIM), lambda b, h, i: (b, DA_HEADS + h)),
            pl.BlockSpec((DA_VDIM, rows_per_batch), lambda b, h, i: (h, b)),
        ],
        out_specs=pl.BlockSpec((tq, DA_VDIM), lambda b, h, i: (b * nq + i, h)),
        out_shape=jax.ShapeDtypeStruct((qk.shape[0], DA_WIDTH), BF16),
        scratch_shapes=[pltpu.VMEM((rows_per_batch // ATTN_K_CHUNK, ATTN_K_CHUNK, 2 * tq), F32)],
        compiler_params=_cparams("arbitrary", "arbitrary", "arbitrary"),
        name="diff_attn",
    )(lam_vec, qk, qk, vda)


def _ret_kernel(ld_ref, f_ref, b_ref, of_ref, ob_ref, dm_ref, qd_ref, kd_ref, cd_ref, st_ref):
    c = pl.program_id(1)
    ch = RET_CHUNK
    w = RET_WIDTH
    lane_head = lax.broadcasted_iota(jnp.int32, (1, w), 1) // RET_DK

    @pl.when(c == 0)
    def _():
        st_ref[...] = jnp.zeros_like(st_ref)
        ri = lax.broadcasted_iota(jnp.int32, (ch, ch), 0)
        ci = lax.broadcasted_iota(jnp.int32, (ch, ch), 1)
        rowf = lax.broadcasted_iota(jnp.int32, (ch, w), 0).astype(F32)
        for d in range(2):
            lg_lane = jnp.zeros((1, w), F32)
            for hh in range(RET_HEADS):
                lg = -jnp.exp(jnp.full((1, 1), ld_ref[d, hh], F32))
                lg_lane = jnp.where(lane_head == hh, lg, lg_lane)
                dist = ((ri - ci) if d == 0 else (ci - ri)).astype(F32)
                dm_ref[d, hh] = jnp.where(dist >= 0, jnp.exp(dist * lg), 0.0)
            if d == 0:
                qd_ref[d] = jnp.exp((rowf + 1.0) * lg_lane)
                kd_ref[d] = jnp.exp((ch - 1.0 - rowf) * lg_lane)
            else:
                qd_ref[d] = jnp.exp((ch - rowf) * lg_lane)
                kd_ref[d] = jnp.exp(rowf * lg_lane)
            cd_ref[d] = jnp.exp(float(ch) * lg_lane)

    rblk = lax.broadcasted_iota(jnp.int32, (w, w), 0) // RET_DK
    cblk = lax.broadcasted_iota(jnp.int32, (w, w), 1) // RET_DK
    for d, (src, dst) in enumerate(((f_ref, of_ref), (b_ref, ob_ref))):
        q = src[:, 0:w]
        k = src[:, w:2 * w]
        v = src[:, 2 * w:3 * w]
        st = st_ref[d]
        o = jnp.dot((q.astype(F32) * qd_ref[d]).astype(BF16), st.astype(BF16), preferred_element_type=F32)
        for hh in range(RET_HEADS):
            in_head = lane_head == hh
            qm = jnp.where(in_head, q, jnp.zeros_like(q))
            s = lax.dot_general(qm, k, (((1,), (1,)), ((), ())), preferred_element_type=F32)
            intra = (s * dm_ref[d, hh]).astype(BF16)
            o = o + jnp.where(in_head, jnp.dot(intra, v, preferred_element_type=F32), 0.0)
        dst[...] = o
        kk_t = (k.astype(F32) * kd_ref[d]).T.astype(BF16)
        upd = jnp.dot(kk_t, v, preferred_element_type=F32)
        st_ref[d] = jnp.where(rblk == cblk, st * cd_ref[d] + upd, 0.0)


def _ret_call(log_decay, rqkv, *, batch, rows_per_batch, seq):
    ch = RET_CHUNK
    nc = rows_per_batch // ch
    n_lat = seq // ch
    n_ctx = nc - n_lat

    def fwd(b, c):
        return (b * nc + jnp.where(c < n_ctx, n_lat + c, c - n_ctx), 0)

    def bwd(b, c):
        return (b * nc + nc - 1 - c, 0)

    w = RET_WIDTH
    return pl.pallas_call(
        _ret_kernel,
        grid=(batch, nc),
        in_specs=[
            pl.BlockSpec(memory_space=pltpu.SMEM),
            pl.BlockSpec((ch, 3 * w), fwd),
            pl.BlockSpec((ch, 3 * w), bwd),
        ],
        out_specs=[pl.BlockSpec((ch, w), fwd), pl.BlockSpec((ch, w), bwd)],
        out_shape=[jax.ShapeDtypeStruct((rqkv.shape[0], w), F32)] * 2,
        scratch_shapes=[
            pltpu.VMEM((2, RET_HEADS, ch, ch), F32),
            pltpu.VMEM((2, ch, w), F32),
            pltpu.VMEM((2, ch, w), F32),
            pltpu.VMEM((2, 1, w), F32),
            pltpu.VMEM((2, w, w), F32),
        ],
        compiler_params=_cparams("arbitrary", "arbitrary"),
        name="retention",
    )(log_decay, rqkv, rqkv)


def _mixout_kernel(x_ref, da_ref, u_ref, up_ref, un_ref, of_ref, ob_ref, rg_ref, mod_ref, wo_ref, pw_ref,
                   ps_ref, lng_ref, lnb_ref, o_ref, *, tiles_per_batch, seq, alpha):
    d = D_MODEL
    t = x_ref.shape[0]
    i = pl.program_id(0)
    j = i % tiles_per_batch
    is_ctx = j == tiles_per_batch - 1
    stream_len = jnp.where(is_ctx, CTX_LEN, seq)
    p0 = jnp.where(is_ctx, 0, j * t)

    u = u_ref[...]
    prev = jnp.where(p0 > 0, up_ref[...], 0.0)
    nxt = jnp.where(p0 + t < stream_len, un_ref[...], 0.0)
    ext = jnp.concatenate([prev, u, nxt], axis=0)
    n = t + 2 * POOL_HALO
    a2 = ext + pltpu.roll(ext, 1, 0)
    a4 = pltpu.roll(a2, 1, 0) + pltpu.roll(a2, n - 1, 0)
    a8 = pltpu.roll(a4, 2, 0) + pltpu.roll(a4, n - 2, 0)
    a16 = pltpu.roll(a8, 4, 0) + pltpu.roll(a8, n - 4, 0)
    pos = p0 + lax.broadcasted_iota(jnp.int32, (t, POOL_WIDTH), 0)
    group = lax.broadcasted_iota(jnp.int32, (1, POOL_WIDTH), 1) // POOL_GROUP
    mean = jnp.zeros((t, POOL_WIDTH), F32)
    for gi, (wnd, asum) in enumerate(zip(POOL_WINDOWS, (a2, a4, a8, a16))):
        cnt = jnp.minimum(pos + wnd // 2, stream_len) - jnp.maximum(pos - wnd // 2, 0)
        mean = jnp.where(group == gi, asum[POOL_HALO:POOL_HALO + t] / cnt.astype(F32), mean)
    pool = jnp.dot((mean - u).astype(BF16), pw_ref[...], preferred_element_type=F32) * ps_ref[...]

    o = of_ref[...] + ob_ref[...]
    head = lax.broadcasted_iota(jnp.int32, (1, RET_WIDTH), 1) // RET_DK

    def head_mean(val):
        out = jnp.zeros_like(val)
        for hh in range(RET_HEADS):
            m = jnp.sum(jnp.where(head == hh, val, 0.0), axis=-1, keepdims=True) * (1.0 / RET_DK)
            out = jnp.where(head == hh, m, out)
        return out

    oc = o - head_mean(o)
    rn = oc * lax.rsqrt(head_mean(oc * oc) + LN_EPS)
    g = rg_ref[...]
    ret = rn * (g * _sigmoid(g))

    y = jnp.dot(da_ref[...], wo_ref[0:DA_WIDTH, :], preferred_element_type=F32)
    y = y + jnp.dot(pool.astype(BF16), wo_ref[DA_WIDTH:DA_WIDTH + POOL_WIDTH, :], preferred_element_type=F32)
    y = y + jnp.dot(ret.astype(BF16), wo_ref[DA_WIDTH + POOL_WIDTH:, :], preferred_element_type=F32)
    z = alpha * x_ref[...] + mod_ref[:, 2 * d:3 * d] * y
    o_ref[...] = _layer_norm_rows(z) * lng_ref[...] + lnb_ref[...]


def _mixout_call(x, da, u, o_f, o_b, rg, mod3, w_out_bf, pool_bd, pool_scale, ln_g, ln_b, *, tiles_per_batch, seq,
                 alpha):
    r, d = x.shape
    t = ROW_TILE
    nt = r // t
    hb = t // POOL_HALO
    n_halo_blocks = r // POOL_HALO
    row = lambda i: (i, 0)
    const = lambda i: (0, 0)
    kern = functools.partial(_mixout_kernel, tiles_per_batch=tiles_per_batch, seq=seq, alpha=alpha)
    return pl.pallas_call(
        kern,
        grid=(nt,),
        in_specs=[
            pl.BlockSpec((t, d), row),
            pl.BlockSpec((t, DA_WIDTH), row),
            pl.BlockSpec((t, POOL_WIDTH), row),
            pl.BlockSpec((POOL_HALO, POOL_WIDTH), lambda i: (jnp.maximum(i * hb - 1, 0), 0)),
            pl.BlockSpec((POOL_HALO, POOL_WIDTH), lambda i: (jnp.minimum((i + 1) * hb, n_halo_blocks - 1), 0)),
            pl.BlockSpec((t, RET_WIDTH), row),
            pl.BlockSpec((t, RET_WIDTH), row),
            pl.BlockSpec((t, RET_WIDTH), row),
            pl.BlockSpec((None, 1, 6 * d), lambda i: (_mod_row(i, tiles_per_batch), 0, 0)),
            pl.BlockSpec((d, d), const),
            pl.BlockSpec((POOL_WIDTH, POOL_WIDTH), const),
            pl.BlockSpec((1, POOL_WIDTH), const),
            pl.BlockSpec((1, d), const),
            pl.BlockSpec((1, d), const),
        ],
        out_specs=pl.BlockSpec((t, d), row),
        out_shape=jax.ShapeDtypeStruct((r, d), F32),
        compiler_params=_cparams("arbitrary"),
        name="mixer_out",
    )(x, da, u, u, u, o_f, o_b, rg, mod3, w_out_bf, pool_bd, pool_scale, ln_g, ln_b)


def _router_kernel(x_ref, mod_ref, wrh_ref, wrl_ref, bias_ref, wsgu_ref, wsdn_ref,
                   tokp_ref, idx_ref, gate_ref, rank_ref, cnt_ref, fsh_ref, carry_ref):
    d = D_MODEL
    t = x_ref.shape[0]
    ne = N_EXPERTS
    neg = -jnp.inf

    @pl.when(pl.program_id(0) == 0)
    def _():
        carry_ref[...] = jnp.zeros_like(carry_ref)

    tok = _layer_norm_rows(x_ref[...]) * (1.0 + mod_ref[:, 4 * d:5 * d]) + mod_ref[:, 3 * d:4 * d]
    tok_hi = tok.astype(BF16)
    tok_lo = (tok - tok_hi.astype(F32)).astype(BF16)

    tokp_ref[...] = _pack_bf16_pairs(tok)

    hs = jnp.dot(tok_hi, wsgu_ref[...], preferred_element_type=F32)
    gs, us = hs[:, 0:EXPERT_HIDDEN], hs[:, EXPERT_HIDDEN:]
    fsh_ref[...] = jnp.dot((gs * _sigmoid(gs) * us).astype(BF16), wsdn_ref[...], preferred_element_type=F32)

    nt_dims = (((1,), (1,)), ((), ()))
    logits = (lax.dot_general(wrh_ref[...], tok_hi, nt_dims, preferred_element_type=F32)
              + lax.dot_general(wrh_ref[...], tok_lo, nt_dims, preferred_element_type=F32)
              + lax.dot_general(wrl_ref[...], tok_hi, nt_dims, preferred_element_type=F32))
    scores = _sigmoid(logits)
    biased = scores + bias_ref[...]

    gidx = lax.broadcasted_iota(jnp.int32, (GROUP_SIZE, t), 0)
    blocks, gscores = [], []
    for g in range(N_GROUPS):
        blk = biased[g * GROUP_SIZE:(g + 1) * GROUP_SIZE, :]
        m1 = jnp.max(blk, axis=0, keepdims=True)
        first = jnp.min(jnp.where(blk == m1, gidx, GROUP_SIZE), axis=0, keepdims=True)
        m2 = jnp.max(jnp.where(gidx == first, neg, blk), axis=0, keepdims=True)
        blocks.append(blk)
        gscores.append(m1 + m2)

    keep = [jnp.zeros((1, t), F32) for _ in range(N_GROUPS)]
    for _ in range(TOPK_GROUPS):
        m = gscores[0]
        for gs_ in gscores[1:]:
            m = jnp.maximum(m, gs_)
        found = jnp.zeros((1, t), F32)
        for g in range(N_GROUPS):
            hit = jnp.where(gscores[g] == m, 1.0 - found, 0.0)
            found = found + hit
            keep[g] = keep[g] + hit
            gscores[g] = jnp.where(hit > 0.0, neg, gscores[g])
    masked = jnp.concatenate([jnp.where(keep[g] > 0.0, blocks[g], neg) for g in range(N_GROUPS)], axis=0)

    ei = lax.broadcasted_iota(jnp.int32, (ne, t), 0)
    cur = masked
    onehot = jnp.zeros((ne, t), F32)
    idxs, gates = [], []
    for _ in range(TOP_K):
        m = jnp.max(cur, axis=0, keepdims=True)
        ii = jnp.min(jnp.where(cur == m, ei, ne), axis=0, keepdims=True)
        sel = ei == ii
        idxs.append(ii)
        gates.append(jnp.sum(jnp.where(sel, scores, 0.0), axis=0, keepdims=True))
        onehot = jnp.where(sel, 1.0, onehot)
        cur = jnp.where(sel, neg, cur)
    gsum = gates[0]
    for gk in gates[1:]:
        gsum = gsum + gk
    for k in range(TOP_K):
        idx_ref[k:k + 1, :] = idxs[k]
        gate_ref[k:k + 1, :] = gates[k] / gsum * ROUTED_SCALE

    ti = lax.broadcasted_iota(jnp.int32, (t, t), 0)
    tj = lax.broadcasted_iota(jnp.int32, (t, t), 1)
    before = jnp.where(ti < tj, 1.0, 0.0).astype(BF16)
    prefix = jnp.dot(onehot.astype(BF16), before, preferred_element_type=F32) + carry_ref[:, 0:1]
    for k in range(TOP_K):
        rank_k = jnp.sum(jnp.where(ei == idxs[k], prefix, 0.0), axis=0, keepdims=True)
        rank_ref[k:k + 1, :] = rank_k.astype(jnp.int32)
    carry_ref[...] = carry_ref[...] + jnp.sum(onehot, axis=1, keepdims=True)
    cnt_ref[...] = carry_ref[...].astype(jnp.int32)


def _router_call(x, mod3, wr_hi, wr_lo, bias_col, ws_gu_bf, ws_dn_bf, *, tiles_per_batch):
    r, d = x.shape
    t = ROW_TILE
    nt = r // t
    row = lambda i: (i, 0)
    col = lambda i: (0, i)
    const = lambda i: (0, 0)
    return pl.pallas_call(
        _router_kernel,
        grid=(nt,),
        in_specs=[
            pl.BlockSpec((t, d), row),
            pl.BlockSpec((None, 1, 6 * d), lambda i: (_mod_row(i, tiles_per_batch), 0, 0)),
            pl.BlockSpec((N_EXPERTS, d), const),
            pl.BlockSpec((N_EXPERTS, d), const),
            pl.BlockSpec((N_EXPERTS, 1), const),
            pl.BlockSpec((d, 2 * EXPERT_HIDDEN), const),
            pl.BlockSpec((EXPERT_HIDDEN, d), const),
        ],
        out_specs=[
            pl.BlockSpec((t, PACK_W), row),
            pl.BlockSpec((TOP_K, t), col),
            pl.BlockSpec((TOP_K, t), col),
            pl.BlockSpec((TOP_K, t), col),
            pl.BlockSpec((N_EXPERTS, LANES), const),
            pl.BlockSpec((t, d), row),
        ],
        out_shape=[
            jax.ShapeDtypeStruct((r, PACK_W), jnp.int32),
            jax.ShapeDtypeStruct((TOP_K, r), jnp.int32),
            jax.ShapeDtypeStruct((TOP_K, r), F32),
            jax.ShapeDtypeStruct((TOP_K, r), jnp.int32),
            jax.ShapeDtypeStruct((N_EXPERTS, LANES), jnp.int32),
            jax.ShapeDtypeStruct((r, d), F32),
        ],
        scratch_shapes=[pltpu.VMEM((N_EXPERTS, LANES), F32)],
        compiler_params=_cparams("arbitrary"),
        name="router",
    )(x, mod3, wr_hi, wr_lo, bias_col, ws_gu_bf, ws_dn_bf)


def _dest_kernel(idx_ref, rank_ref, offs_ref, dest_ref):
    t = idx_ref.shape[1]
    ei = lax.broadcasted_iota(jnp.int32, (N_EXPERTS, t), 0)
    offs = offs_ref[...].astype(F32)
    for k in range(TOP_K):
        start = jnp.sum(jnp.where(ei == idx_ref[k:k + 1, :], offs, 0.0), axis=0, keepdims=True)
        dest_ref[k:k + 1, :] = start.astype(jnp.int32) + rank_ref[k:k + 1, :]


def _dest_call(idx, rank, offs_col):
    r = idx.shape[1]
    t = r // DEST_STEPS
    assert r % DEST_STEPS == 0 and t % LANES == 0
    col = lambda i: (0, i)
    return pl.pallas_call(
        _dest_kernel,
        grid=(r // t,),
        in_specs=[pl.BlockSpec((TOP_K, t), col), pl.BlockSpec((TOP_K, t), col),
                  pl.BlockSpec((N_EXPERTS, 1), lambda i: (0, 0))],
        out_specs=pl.BlockSpec((TOP_K, t), col),
        out_shape=jax.ShapeDtypeStruct((TOP_K, r), jnp.int32),
        compiler_params=_cparams("arbitrary"),
        name="moe_dest",
    )(idx, rank, offs_col)


def _sc_dispatch(tokp, dest_flat, pad_rows, n_sorted):
    r, width = tokp.shape
    win = SC_GATHER_WINDOW
    workers = SC_NUM_CORES * SC_NUM_SUBCORES
    token_windows = r // win
    n_pad_windows = pad_rows.shape[0] // win
    assert r % win == 0 and dest_flat.shape[0] == TOP_K * r and n_pad_windows % workers == 0
    windows_per_worker = -(-token_windows // workers)
    pads_per_worker = n_pad_windows // workers
    mesh = plsc.VectorSubcoreMesh(core_axis_name="core", subcore_axis_name="subcore", num_cores=SC_NUM_CORES,
                                  num_subcores=SC_NUM_SUBCORES)
    zero_rows = jnp.zeros((win, width), tokp.dtype)

    @functools.partial(
        pl.kernel, out_type=jax.ShapeDtypeStruct((n_sorted + SC_SPARE_ROWS, width), tokp.dtype), mesh=mesh,
        scratch_types=[pltpu.VMEM((win,), jnp.int32), pltpu.VMEM((win, width), tokp.dtype),
                       pltpu.SemaphoreType.DMA],
        name="moe_sc_dispatch")
    def dispatch_kernel(tok_hbm, dest_hbm, pad_hbm, zero_hbm, xs_hbm, idx_vmem, rows_vmem, sem):
        worker = lax.axis_index("subcore") * SC_NUM_CORES + lax.axis_index("core")

        @pl.loop(0, windows_per_worker)
        def _(j):
            window = j * workers + worker

            @pl.when(window < token_windows)
            def _():
                tok0 = window * win
                pltpu.sync_copy(tok_hbm.at[pl.ds(tok0, win)], rows_vmem)
                for k in range(TOP_K):
                    pltpu.sync_copy(dest_hbm.at[pl.ds(k * r + tok0, win)], idx_vmem)
                    pltpu.async_copy(rows_vmem, xs_hbm.at[idx_vmem], sem).wait()

        pltpu.sync_copy(zero_hbm, rows_vmem)

        @pl.loop(0, pads_per_worker)
        def _(j):
            off = (worker * pads_per_worker + j) * win
            pltpu.sync_copy(pad_hbm.at[pl.ds(off, win)], idx_vmem)
            pltpu.async_copy(rows_vmem, xs_hbm.at[idx_vmem], sem).wait()

    return dispatch_kernel(tokp, dest_flat, pad_rows, zero_rows)


def _expert_kernel(be_ref, nb_ref, ord_ref, ue_ref, nue_ref, xs_ref, wgu_hbm, wdn_hbm, ys_ref, wgu_f32, wdn_f32,
                   wgu_bf, wdn_bf, sems, *, layer):
    j = pl.program_id(0)

    def weight_copies(o):
        slot = o % 2
        e = ue_ref[o]
        return (pltpu.make_async_copy(wgu_hbm.at[layer, e], wgu_f32.at[slot], sems.at[0, slot]),
                pltpu.make_async_copy(wdn_hbm.at[layer, e], wdn_f32.at[slot], sems.at[1, slot]))

    def start_weights(o):
        @pl.when(o < nue_ref[0])
        def _():
            for cp in weight_copies(o):
                cp.start(priority=1)

    @pl.when(j < nb_ref[0])
    def _():
        o = ord_ref[j]
        changed = jnp.logical_or(j == 0, be_ref[j] != be_ref[jnp.maximum(j - 1, 0)])

        @pl.when(j == 0)
        def _():
            start_weights(0)
            start_weights(1)

        @pl.when(changed)
        def _():
            for cp in weight_copies(o):
                cp.wait()
            slot = o % 2
            wgu_bf[...] = wgu_f32[slot].astype(BF16)
            wdn_bf[...] = wdn_f32[slot].astype(BF16)
            start_weights(o + 2)

        x_lo, x_hi = _unpack_bf16_pairs(xs_ref[...])
        h = (jnp.dot(x_lo.astype(BF16), wgu_bf[0:PACK_W, :], preferred_element_type=F32)
             + jnp.dot(x_hi.astype(BF16), wgu_bf[PACK_W:, :], preferred_element_type=F32))
        g, u = h[:, 0:EXPERT_HIDDEN], h[:, EXPERT_HIDDEN:]
        y = jnp.dot((g * _sigmoid(g) * u).astype(BF16), wdn_bf[...], preferred_element_type=F32)
        ys_ref[...] = _pack_bf16_pairs(y)


def _expert_call(block_expert, n_blocks_used, block_ordinal, used_expert, n_used_experts, xs, w_gu, w_dn, layer):
    n_rows = block_expert.shape[0] * EXPERT_BLOCK
    bm = EXPERT_BLOCK
    d = D_MODEL
    used_block = lambda j, be, nb, od, ue, nue: (jnp.minimum(j, nb[0] - 1), 0)
    grid_spec = pltpu.PrefetchScalarGridSpec(
        num_scalar_prefetch=5,
        grid=(n_rows // bm,),
        in_specs=[
            pl.BlockSpec((bm, PACK_W), used_block),
            pl.BlockSpec(memory_space=pl.ANY),
            pl.BlockSpec(memory_space=pl.ANY),
        ],
        out_specs=pl.BlockSpec((bm, PACK_W), used_block),
        scratch_shapes=[
            pltpu.VMEM((2, d, 2 * EXPERT_HIDDEN), F32),
            pltpu.VMEM((2, EXPERT_HIDDEN, d), F32),
            pltpu.VMEM((d, 2 * EXPERT_HIDDEN), BF16),
            pltpu.VMEM((EXPERT_HIDDEN, d), BF16),
            pltpu.SemaphoreType.DMA((2, 2)),
        ],
    )
    return pl.pallas_call(
        functools.partial(_expert_kernel, layer=layer),
        grid_spec=grid_spec,
        out_shape=jax.ShapeDtypeStruct((n_rows, PACK_W), jnp.int32),
        compiler_params=_cparams("arbitrary"),
        name="moe_experts",
    )(block_expert, n_blocks_used, block_ordinal, used_expert, n_used_experts, xs, w_gu, w_dn)


def _sc_gather_rows(table, indices):
    n = indices.shape[0]
    width = table.shape[1]
    workers = SC_NUM_CORES * SC_NUM_SUBCORES
    assert n % (SC_GATHER_WINDOW * workers) == 0
    per_worker = n // workers
    mesh = plsc.VectorSubcoreMesh(core_axis_name="core", subcore_axis_name="subcore", num_cores=SC_NUM_CORES,
                                  num_subcores=SC_NUM_SUBCORES)

    @functools.partial(
        pl.kernel, out_type=jax.ShapeDtypeStruct((n, width), table.dtype), mesh=mesh,
        scratch_types=[pltpu.VMEM((SC_GATHER_WINDOW,), jnp.int32),
                       pltpu.VMEM((SC_GATHER_WINDOW, width), table.dtype),
                       pltpu.SemaphoreType.DMA],
        name="moe_sc_gather")
    def gather_kernel(table_hbm, idx_hbm, out_hbm, idx_vmem, rows_vmem, sem):
        worker = lax.axis_index("subcore") * SC_NUM_CORES + lax.axis_index("core")
        base = worker * per_worker

        @pl.loop(0, per_worker // SC_GATHER_WINDOW)
        def _(w):
            off = base + w * SC_GATHER_WINDOW
            pltpu.sync_copy(idx_hbm.at[pl.ds(off, SC_GATHER_WINDOW)], idx_vmem)
            pltpu.async_copy(table_hbm.at[idx_vmem], rows_vmem, sem).wait()
            pltpu.sync_copy(rows_vmem, out_hbm.at[pl.ds(off, SC_GATHER_WINDOW)])

    return gather_kernel(table, indices)


def _combine_kernel(*refs, alpha):
    y_refs = refs[:TOP_K]
    x_ref, fsh_ref, gate_ref, mod_ref, lng_ref, lnb_ref, o_ref = refs[TOP_K:]
    d = D_MODEL
    t = x_ref.shape[0]
    gate_rows = gate_ref[...]
    pad = jnp.zeros((LANES - TOP_K, t), F32)
    gate_cols = jnp.concatenate([gate_rows, pad], axis=0).T
    f_lo = fsh_ref[:, 0:PACK_W]
    f_hi = fsh_ref[:, PACK_W:]
    for k in range(TOP_K):
        y_lo, y_hi = _unpack_bf16_pairs(y_refs[k][...])
        f_lo = f_lo + gate_cols[:, k:k + 1] * y_lo
        f_hi = f_hi + gate_cols[:, k:k + 1] * y_hi
    f = jnp.concatenate([f_lo, f_hi], axis=1)
    z = alpha * x_ref[...] + mod_ref[:, 5 * d:6 * d] * f
    o_ref[...] = _layer_norm_rows(z) * lng_ref[...] + lnb_ref[...]


def _combine_call(y_tok, x, fsh, gate, mod3, ln_g, ln_b, *, tiles_per_batch, alpha, drop_context):
    r, d = x.shape
    t = ROW_TILE
    nt = r // t
    if drop_context:
        per_batch = tiles_per_batch - 1
        src = lambda i: (i // per_batch) * tiles_per_batch + i % per_batch
        n_tiles = nt // tiles_per_batch * per_batch
    else:
        src = lambda i: i
        n_tiles = nt
    row = lambda i: (src(i), 0)
    col = lambda i: (0, src(i))
    const = lambda i: (0, 0)
    kern = functools.partial(_combine_kernel, alpha=alpha)
    y_specs = [pl.BlockSpec((t, PACK_W), functools.partial(lambda k, i: (k * nt + src(i), 0), k))
               for k in range(TOP_K)]
    return pl.pallas_call(
        kern,
        grid=(n_tiles,),
        in_specs=y_specs + [
            pl.BlockSpec((t, d), row),
            pl.BlockSpec((t, d), row),
            pl.BlockSpec((TOP_K, t), col),
            pl.BlockSpec((None, 1, 6 * d), lambda i: (_mod_row(src(i), tiles_per_batch), 0, 0)),
            pl.BlockSpec((1, d), const),
            pl.BlockSpec((1, d), const),
        ],
        out_specs=pl.BlockSpec((t, d), lambda i: (i, 0)),
        out_shape=jax.ShapeDtypeStruct((n_tiles * t, d), F32),
        compiler_params=_cparams("arbitrary"),
        name="moe_combine",
    )(*([y_tok] * TOP_K), x, fsh, gate, mod3, ln_g, ln_b)


def _rope_tables(seq):
    rows = seq // GRID_W
    row = jnp.repeat(jnp.arange(rows, dtype=F32), GRID_W)
    col = jnp.tile(jnp.arange(GRID_W, dtype=F32), rows)
    nf = DA_DIM // 4
    freqs = ROPE_BASE ** (-jnp.arange(nf, dtype=F32) / nf)
    cr, sr = jnp.cos(row[:, None] * freqs), jnp.sin(row[:, None] * freqs)
    cc, sc = jnp.cos(col[:, None] * freqs), jnp.sin(col[:, None] * freqs)
    c64 = jnp.concatenate([cr, cr, cc, cc], axis=1)
    s64 = jnp.concatenate([-sr, sr, -sc, sc], axis=1)
    c = jnp.concatenate([jnp.tile(c64, (1, 2)), jnp.ones((CTX_LEN, LANES), F32)], axis=0)
    s = jnp.concatenate([jnp.tile(s64, (1, 2)), jnp.zeros((CTX_LEN, LANES), F32)], axis=0)
    return c, s


def kernel(x, c, ctx, c_ctx, w_mod, b_mod, w_in, w_out, diff_lambda, pool_w, pool_scale, ret_log_decay, ln_g, ln_b,
           w_router, router_bias, w_expert_gate_up, w_expert_down, w_shared_gate_up, w_shared_down):
    batch, seq, d = x.shape
    depth = w_mod.shape[0]
    assert d == D_MODEL and ctx.shape[1] == CTX_LEN == ROW_TILE and batch == 2
    assert seq % ROW_TILE == 0 and seq % GRID_W == 0 and w_in.shape[-1] == IN_WIDTH
    rows_per_batch = seq + CTX_LEN
    tiles_per_batch = rows_per_batch // ROW_TILE
    r = batch * rows_per_batch
    alpha = (2.0 * depth) ** 0.25

    xa = jnp.concatenate([x, ctx], axis=1).reshape(r, d)
    cvec = jnp.zeros((8, d), F32).at[0:batch].set(c).at[batch].set(c_ctx)
    mod_all = _mod_call(cvec, w_mod, b_mod)
    rope_c, rope_s = _rope_tables(seq)

    n_sorted = r * TOP_K + N_EXPERTS * EXPERT_BLOCK
    n_blocks = n_sorted // EXPERT_BLOCK

    for l in range(depth):
        lambda_init = 0.8 - 0.6 * math.exp(-0.3 * l)
        mod3 = mod_all[l].reshape(8, 1, 6 * d)
        lng = ln_g[l].reshape(2, 1, d)
        lnb = ln_b[l].reshape(2, 1, d)

        w_in_bf = w_in[l].astype(BF16)
        w_vt_bf = w_in_bf[:, QK_WIDTH:QK_WIDTH + DA_WIDTH].T
        qk, vda, u, rqkv, rg = _inproj_call(xa, mod3, w_in_bf, w_vt_bf, rope_c, rope_s, tiles_per_batch)
        da = _attn_call(diff_lambda[l], qk, vda, batch=batch, rows_per_batch=rows_per_batch, seq=seq,
                        lambda_init=lambda_init)
        o_f, o_b = _ret_call(ret_log_decay[l], rqkv, batch=batch, rows_per_batch=rows_per_batch, seq=seq)
        pool_bd = jnp.zeros((POOL_WIDTH, POOL_WIDTH), F32)
        for gi in range(len(POOL_WINDOWS)):
            sl = slice(gi * POOL_GROUP, (gi + 1) * POOL_GROUP)
            pool_bd = pool_bd.at[sl, sl].set(pool_w[l, gi])
        xa = _mixout_call(xa, da, u, o_f, o_b, rg, mod3, w_out[l].astype(BF16), pool_bd.astype(BF16),
                          pool_scale[l].reshape(1, POOL_WIDTH), lng[0], lnb[0],
                          tiles_per_batch=tiles_per_batch, seq=seq, alpha=alpha)

        wr_t = w_router[l].T
        wr_hi = wr_t.astype(BF16)
        wr_lo = (wr_t - wr_hi.astype(F32)).astype(BF16)
        tokp, idx, gate, rank, cnt, fsh = _router_call(
            xa, mod3, wr_hi, wr_lo, router_bias[l].reshape(N_EXPERTS, 1),
            w_shared_gate_up[l].astype(BF16), w_shared_down[l].astype(BF16), tiles_per_batch=tiles_per_batch)
        counts = cnt[:, 0]
        padded = (counts + EXPERT_BLOCK - 1) // EXPERT_BLOCK * EXPERT_BLOCK
        pad_end = jnp.cumsum(padded)
        offs = pad_end - padded
        expert_ids = jnp.arange(N_EXPERTS, dtype=jnp.int32)
        blk_row = jnp.arange(n_blocks, dtype=jnp.int32) * EXPERT_BLOCK
        block_expert = jnp.minimum(jnp.sum(pad_end[None, :] <= blk_row[:, None], axis=1), N_EXPERTS - 1)
        n_used = pad_end[-1:] // EXPERT_BLOCK
        used = counts > 0
        ordinal = jnp.cumsum(used) - 1
        hit = used[None, :] & (ordinal[None, :] == expert_ids[:, None])
        used_expert = jnp.sum(jnp.where(hit, expert_ids[None, :], 0), axis=1)
        n_used_experts = jnp.sum(used)[None]
        block_ordinal = ordinal[block_expert]
        slot = jnp.arange(EXPERT_BLOCK, dtype=jnp.int32)[None, :]
        first_pad = (padded - EXPERT_BLOCK)[:, None] + slot
        is_pad = (first_pad >= counts[:, None]) & (padded[:, None] > 0)
        spare = n_sorted + jnp.arange(N_EXPERTS * EXPERT_BLOCK, dtype=jnp.int32).reshape(N_EXPERTS, EXPERT_BLOCK)
        pad_rows = jnp.where(is_pad, offs[:, None] + first_pad, spare).reshape(N_EXPERTS * EXPERT_BLOCK)
        i32 = lambda a: a.astype(jnp.int32)

        dest = _dest_call(idx, rank, i32(offs).reshape(N_EXPERTS, 1))
        dest_flat = dest.reshape(TOP_K * r)
        xs = _sc_dispatch(tokp, dest_flat, i32(pad_rows), n_sorted)
        ys = _expert_call(i32(block_expert), i32(n_used), i32(block_ordinal), i32(used_expert),
                          i32(n_used_experts), xs, w_expert_gate_up, w_expert_down, l)
        y_tok = _sc_gather_rows(ys, dest_flat)
        xa = _combine_call(y_tok, xa, fsh, gate, mod3, lng[1], lnb[1], tiles_per_batch=tiles_per_batch,
                           alpha=alpha, drop_context=(l == depth - 1))

    return xa.reshape(batch, seq, d)
```

```python
import functools
import math

import jax
import jax.numpy as jnp
from jax import lax
from jax.experimental import pallas as pl
from jax.experimental.pallas import tpu as pltpu
from jax.experimental.pallas import tpu_sc as plsc

F32 = jnp.float32
BF16 = jnp.bfloat16
HIGHEST = lax.Precision.HIGHEST

D_MODEL = 1024
CTX_LEN = 256
GRID_W = 64
DA_HEADS = 4
DA_DIM = 64
DA_VDIM = 2 * DA_DIM
DA_WIDTH = DA_HEADS * DA_VDIM
ROPE_BASE = 10000.0
POOL_WINDOWS = (2, 4, 8, 16)
POOL_GROUP = 64
POOL_WIDTH = len(POOL_WINDOWS) * POOL_GROUP
POOL_HALO = 8
RET_HEADS = 4
RET_DK = 64
RET_WIDTH = RET_HEADS * RET_DK
RET_CHUNK = 128
QK_WIDTH = 2 * DA_HEADS * 2 * DA_DIM
IN_WIDTH = QK_WIDTH + DA_WIDTH + POOL_WIDTH + 4 * RET_WIDTH
N_EXPERTS = 256
TOP_K = 8
N_GROUPS = 8
GROUP_SIZE = N_EXPERTS // N_GROUPS
TOPK_GROUPS = 4
EXPERT_HIDDEN = 256
ROUTED_SCALE = 2.5
LN_EPS = 1e-6
RMS_EPS = 1e-5

LANES = 128
ROW_TILE = 256
DEST_STEPS = 4
ATTN_Q_TILE = 256
ATTN_K_CHUNK = 256
ATTN_UNROLL = 16
SC_NUM_CORES = 2
SC_NUM_SUBCORES = 16
SC_GATHER_WINDOW = 128
EXPERT_BLOCK = 256
EXPERT_BLOCKS_PER_STEP = 4
SC_SPARE_ROWS = N_EXPERTS * EXPERT_BLOCK
PACK_W = D_MODEL // 2
VMEM_LIMIT = 56 * 1024 * 1024


def _cparams(*sem):
    return pltpu.CompilerParams(dimension_semantics=sem, vmem_limit_bytes=VMEM_LIMIT)


def _sigmoid(x):
    return 1.0 / (1.0 + jnp.exp(-x))


def _layer_norm_rows(x):
    mu = jnp.mean(x, axis=-1, keepdims=True)
    xc = x - mu
    var = jnp.mean(xc * xc, axis=-1, keepdims=True)
    return xc * lax.rsqrt(var + LN_EPS)


def _pack_bf16_pairs(x):
    half = x.shape[1] // 2
    bits = pltpu.bitcast(x.astype(BF16).astype(F32), jnp.uint32)
    word = lax.shift_right_logical(bits[:, 0:half], jnp.uint32(16)) | (bits[:, half:] & jnp.uint32(0xFFFF0000))
    return pltpu.bitcast(word, jnp.int32)


def _unpack_bf16_pairs(packed):
    word = pltpu.bitcast(packed, jnp.uint32)
    lo = pltpu.bitcast(lax.shift_left(word, jnp.uint32(16)), F32)
    hi = pltpu.bitcast(word & jnp.uint32(0xFFFF0000), F32)
    return lo, hi


def _mod_row(i, tiles_per_batch):
    return jnp.where(i % tiles_per_batch == tiles_per_batch - 1, 2, i // tiles_per_batch)


def _mod_kernel(c_ref, w_ref, b_ref, o_ref):
    c = c_ref[...]
    s = c * _sigmoid(c)
    o_ref[...] = jnp.dot(s, w_ref[...], precision=HIGHEST, preferred_element_type=F32) + b_ref[...]


def _mod_call(cvec, w_mod, b_mod):
    depth, d, n = w_mod.shape
    tn = 1536
    return pl.pallas_call(
        _mod_kernel,
        grid=(depth, n // tn),
        in_specs=[
            pl.BlockSpec((8, d), lambda l, j: (0, 0)),
            pl.BlockSpec((None, d, tn), lambda l, j: (l, 0, j)),
            pl.BlockSpec((None, 1, tn), lambda l, j: (l, 0, j)),
        ],
        out_specs=pl.BlockSpec((None, 8, tn), lambda l, j: (l, 0, j)),
        out_shape=jax.ShapeDtypeStruct((depth, 8, n), F32),
        compiler_params=_cparams("arbitrary", "arbitrary"),
        name="mod",
    )(cvec, w_mod, b_mod.reshape(depth, 1, n))


def _inproj_kernel(x_ref, mod_ref, w_ref, wvt_ref, ct_ref, st_ref, qk_ref, vt_ref, u_ref, r_ref, g_ref):
    d = D_MODEL
    xn = _layer_norm_rows(x_ref[...])
    h = (xn * (1.0 + mod_ref[:, d:2 * d]) + mod_ref[:, 0:d]).astype(BF16)

    a = jnp.dot(h, w_ref[:, 0:QK_WIDTH], preferred_element_type=F32)
    lane = lax.broadcasted_iota(jnp.int32, (a.shape[0], LANES), 1)
    first_half = (lane % 32) < 16
    ct = ct_ref[...]
    st = st_ref[...]
    for s in range(QK_WIDTH // LANES):
        blk = a[:, s * LANES:(s + 1) * LANES]
        partner = jnp.where(first_half, pltpu.roll(blk, LANES - 16, 1), pltpu.roll(blk, 16, 1))
        rot = blk * ct + partner * st
        if s < QK_WIDTH // LANES // 2:
            rot = rot * (DA_DIM ** -0.5 * math.log2(math.e))
        qk_ref[:, s * LANES:(s + 1) * LANES] = rot.astype(BF16)

    vt_ref[...] = lax.dot_general(wvt_ref[...], h, (((1,), (1,)), ((), ())),
                                  preferred_element_type=F32).astype(BF16)
    o = QK_WIDTH + DA_WIDTH
    u_ref[...] = jnp.dot(h, w_ref[:, o:o + POOL_WIDTH], preferred_element_type=F32)
    o += POOL_WIDTH
    r = jnp.dot(h, w_ref[:, o:o + 3 * RET_WIDTH], preferred_element_type=F32)
    r_ref[:, 0:RET_WIDTH] = r[:, 0:RET_WIDTH].astype(BF16)
    r_ref[:, RET_WIDTH:2 * RET_WIDTH] = (r[:, RET_WIDTH:2 * RET_WIDTH] * (RET_DK ** -0.5)).astype(BF16)
    r_ref[:, 2 * RET_WIDTH:] = r[:, 2 * RET_WIDTH:].astype(BF16)
    o += 3 * RET_WIDTH
    g_ref[...] = jnp.dot(h, w_ref[:, o:o + RET_WIDTH], preferred_element_type=F32)


def _inproj_call(x, mod3, w_in_bf, w_vt_bf, rope_c, rope_s, tiles_per_batch):
    r, d = x.shape
    t = ROW_TILE
    nt = r // t
    row = lambda i: (i, 0)
    return pl.pallas_call(
        _inproj_kernel,
        grid=(nt,),
        in_specs=[
            pl.BlockSpec((t, d), row),
            pl.BlockSpec((None, 1, 6 * d), lambda i: (_mod_row(i, tiles_per_batch), 0, 0)),
            pl.BlockSpec((d, IN_WIDTH), lambda i: (0, 0)),
            pl.BlockSpec((DA_WIDTH, d), lambda i: (0, 0)),
            pl.BlockSpec((t, LANES), lambda i: (i % tiles_per_batch, 0)),
            pl.BlockSpec((t, LANES), lambda i: (i % tiles_per_batch, 0)),
        ],
        out_specs=[
            pl.BlockSpec((t, QK_WIDTH), row),
            pl.BlockSpec((DA_WIDTH, t), lambda i: (0, i)),
            pl.BlockSpec((t, POOL_WIDTH), row),
            pl.BlockSpec((t, 3 * RET_WIDTH), row),
            pl.BlockSpec((t, RET_WIDTH), row),
        ],
        out_shape=[
            jax.ShapeDtypeStruct((r, QK_WIDTH), BF16),
            jax.ShapeDtypeStruct((DA_WIDTH, r), BF16),
            jax.ShapeDtypeStruct((r, POOL_WIDTH), F32),
            jax.ShapeDtypeStruct((r, 3 * RET_WIDTH), BF16),
            jax.ShapeDtypeStruct((r, RET_WIDTH), F32),
        ],
        compiler_params=_cparams("arbitrary"),
        name="inproj",
    )(x, mod3, w_in_bf, w_vt_bf, rope_c, rope_s)


def _attn_kernel(lam_ref, q_ref, k_ref, vt_ref, o_ref, s_ref, *, k_chunk, seq, lambda_init):
    q = q_ref[...]
    mq = q.shape[0]
    lane = lax.broadcasted_iota(jnp.int32, q.shape, 1)
    zero = jnp.zeros_like(q)
    q2 = jnp.concatenate([jnp.where(lane < DA_DIM, q, zero), jnp.where(lane >= DA_DIM, q, zero)], axis=0)
    qt = q2.astype(F32).T.astype(BF16)

    n_chunks = (seq + CTX_LEN) // k_chunk
    last = n_chunks - 1
    is_ctx_tile = pl.program_id(2) == pl.num_programs(2) - 1
    n_iters = jnp.where(is_ctx_tile, 0, last // ATTN_UNROLL)

    def score_chunk(c, m):
        off = pl.multiple_of(c * k_chunk, k_chunk)
        s = jnp.dot(k_ref[pl.ds(off, k_chunk), :], qt, preferred_element_type=F32)
        s_ref[c] = s
        return jnp.maximum(m, jnp.max(s, axis=0, keepdims=True))

    def pass1(it, m):
        for u in range(ATTN_UNROLL):
            m = score_chunk(it * ATTN_UNROLL + u, m)
        return m

    m = lax.fori_loop(0, n_iters, pass1, jnp.full((1, 2 * mq), -jnp.inf, F32))
    m = score_chunk(last, m)

    ones_rows = jnp.where(lax.broadcasted_iota(jnp.int32, (16, k_chunk), 0) == 0, 1.0, 0.0).astype(BF16)

    def value_chunk(c, acc):
        off = pl.multiple_of(c * k_chunk, k_chunk)
        vt = jnp.concatenate([vt_ref[:, pl.ds(off, k_chunk)], ones_rows], axis=0)
        p = jnp.exp2((s_ref[c] - m).astype(BF16))
        return acc + jnp.dot(vt, p, preferred_element_type=F32)

    def pass2(it, acc):
        for u in range(ATTN_UNROLL):
            acc = value_chunk(it * ATTN_UNROLL + u, acc)
        return acc

    acc = lax.fori_loop(0, n_iters, pass2, jnp.zeros((DA_VDIM + 16, 2 * mq), F32))
    acc = value_chunk(last, acc)
    l0, l1 = acc[DA_VDIM:DA_VDIM + 1, 0:mq], acc[DA_VDIM:DA_VDIM + 1, mq:]
    a0, a1 = acc[0:DA_VDIM, 0:mq], acc[0:DA_VDIM, mq:]

    lv = lam_ref[...]
    lam = (jnp.exp(jnp.sum(lv[0:1] * lv[1:2], axis=-1, keepdims=True))
           - jnp.exp(jnp.sum(lv[2:3] * lv[3:4], axis=-1, keepdims=True)) + lambda_init)
    o = a0 / l0 - lam * (a1 / l1)
    o = o * lax.rsqrt(jnp.mean(o * o, axis=0, keepdims=True) + RMS_EPS) * (1.0 - lambda_init)
    o_ref[...] = o.T.astype(BF16)


def _attn_call(lam_vec, qk, vda, *, batch, rows_per_batch, seq, lambda_init):
    tq = ATTN_Q_TILE
    assert seq % (ATTN_K_CHUNK * ATTN_UNROLL) == 0 and rows_per_batch - seq == CTX_LEN == tq == ATTN_K_CHUNK
    nq = rows_per_batch // tq
    kern = functools.partial(_attn_kernel, k_chunk=ATTN_K_CHUNK, seq=seq, lambda_init=lambda_init)
    return pl.pallas_call(
        kern,
        grid=(batch, DA_HEADS, nq),
        in_specs=[
            pl.BlockSpec((4, DA_DIM), lambda b, h, i: (0, 0)),
            pl.BlockSpec((tq, DA_VDIM), lambda b, h, i: (b * nq + i, h)),
            pl.BlockSpec((rows_per_batch, DA_VDIM), lambda b, h, i: (b, DA_HEADS + h)),
            pl.BlockSpec((DA_VDIM, rows_per_batch), lambda b, h, i: (h, b)),
        ],
        out_specs=pl.BlockSpec((tq, DA_VDIM), lambda b, h, i: (b * nq + i, h)),
        out_shape=jax.ShapeDtypeStruct((qk.shape[0], DA_WIDTH), BF16),
        scratch_shapes=[pltpu.VMEM((rows_per_batch // ATTN_K_CHUNK, ATTN_K_CHUNK, 2 * tq), F32)],
        compiler_params=_cparams("arbitrary", "arbitrary", "arbitrary"),
        name="diff_attn",
    )(lam_vec, qk, qk, vda)


def _ret_kernel(ld_ref, f_ref, b_ref, of_ref, ob_ref, dm_ref, qd_ref, kd_ref, cd_ref, st_ref):
    c = pl.program_id(1)
    ch = RET_CHUNK
    w = RET_WIDTH
    lane_head = lax.broadcasted_iota(jnp.int32, (1, w), 1) // RET_DK

    @pl.when(c == 0)
    def _():
        st_ref[...] = jnp.zeros_like(st_ref)
        ri = lax.broadcasted_iota(jnp.int32, (ch, ch), 0)
        ci = lax.broadcasted_iota(jnp.int32, (ch, ch), 1)
        rowf = lax.broadcasted_iota(jnp.int32, (ch, w), 0).astype(F32)
        for d in range(2):
            lg_lane = jnp.zeros((1, w), F32)
            for hh in range(RET_HEADS):
                lg = -jnp.exp(jnp.full((1, 1), ld_ref[d, hh], F32))
                lg_lane = jnp.where(lane_head == hh, lg, lg_lane)
                dist = ((ri - ci) if d == 0 else (ci - ri)).astype(F32)
                dm_ref[d, hh] = jnp.where(dist >= 0, jnp.exp(dist * lg), 0.0)
            if d == 0:
                qd_ref[d] = jnp.exp((rowf + 1.0) * lg_lane)
                kd_ref[d] = jnp.exp((ch - 1.0 - rowf) * lg_lane)
            else:
                qd_ref[d] = jnp.exp((ch - rowf) * lg_lane)
                kd_ref[d] = jnp.exp(rowf * lg_lane)
            cd_ref[d] = jnp.exp(float(ch) * lg_lane)

    rblk = lax.broadcasted_iota(jnp.int32, (w, w), 0) // RET_DK
    cblk = lax.broadcasted_iota(jnp.int32, (w, w), 1) // RET_DK
    for d, (src, dst) in enumerate(((f_ref, of_ref), (b_ref, ob_ref))):
        q = src[:, 0:w]
        k = src[:, w:2 * w]
        v = src[:, 2 * w:3 * w]
        st = st_ref[d]
        o = jnp.dot((q.astype(F32) * qd_ref[d]).astype(BF16), st.astype(BF16), preferred_element_type=F32)
        for hh in range(RET_HEADS):
            in_head = lane_head == hh
            qm = jnp.where(in_head, q, jnp.zeros_like(q))
            s = lax.dot_general(qm, k, (((1,), (1,)), ((), ())), preferred_element_type=F32)
            intra = (s * dm_ref[d, hh]).astype(BF16)
            o = o + jnp.where(in_head, jnp.dot(intra, v, preferred_element_type=F32), 0.0)
        dst[...] = o
        kk_t = (k.astype(F32) * kd_ref[d]).T.astype(BF16)
        upd = jnp.dot(kk_t, v, preferred_element_type=F32)
        st_ref[d] = jnp.where(rblk == cblk, st * cd_ref[d] + upd, 0.0)


def _ret_call(log_decay, rqkv, *, batch, rows_per_batch, seq):
    ch = RET_CHUNK
    nc = rows_per_batch // ch
    n_lat = seq // ch
    n_ctx = nc - n_lat

    def fwd(b, c):
        return (b * nc + jnp.where(c < n_ctx, n_lat + c, c - n_ctx), 0)

    def bwd(b, c):
        return (b * nc + nc - 1 - c, 0)

    w = RET_WIDTH
    return pl.pallas_call(
        _ret_kernel,
        grid=(batch, nc),
        in_specs=[
            pl.BlockSpec(memory_space=pltpu.SMEM),
            pl.BlockSpec((ch, 3 * w), fwd),
            pl.BlockSpec((ch, 3 * w), bwd),
        ],
        out_specs=[pl.BlockSpec((ch, w), fwd), pl.BlockSpec((ch, w), bwd)],
        out_shape=[jax.ShapeDtypeStruct((rqkv.shape[0], w), F32)] * 2,
        scratch_shapes=[
            pltpu.VMEM((2, RET_HEADS, ch, ch), F32),
            pltpu.VMEM((2, ch, w), F32),
            pltpu.VMEM((2, ch, w), F32),
            pltpu.VMEM((2, 1, w), F32),
            pltpu.VMEM((2, w, w), F32),
        ],
        compiler_params=_cparams("arbitrary", "arbitrary"),
        name="retention",
    )(log_decay, rqkv, rqkv)


def _mixout_kernel(x_ref, da_ref, u_ref, up_ref, un_ref, of_ref, ob_ref, rg_ref, mod_ref, wo_ref, pw_ref,
                   ps_ref, lng_ref, lnb_ref, o_ref, *, tiles_per_batch, seq, alpha):
    d = D_MODEL
    t = x_ref.shape[0]
    i = pl.program_id(0)
    j = i % tiles_per_batch
    is_ctx = j == tiles_per_batch - 1
    stream_len = jnp.where(is_ctx, CTX_LEN, seq)
    p0 = jnp.where(is_ctx, 0, j * t)

    u = u_ref[...]
    prev = jnp.where(p0 > 0, up_ref[...], 0.0)
    nxt = jnp.where(p0 + t < stream_len, un_ref[...], 0.0)
    ext = jnp.concatenate([prev, u, nxt], axis=0)
    n = t + 2 * POOL_HALO
    a2 = ext + pltpu.roll(ext, 1, 0)
    a4 = pltpu.roll(a2, 1, 0) + pltpu.roll(a2, n - 1, 0)
    a8 = pltpu.roll(a4, 2, 0) + pltpu.roll(a4, n - 2, 0)
    a16 = pltpu.roll(a8, 4, 0) + pltpu.roll(a8, n - 4, 0)
    pos = p0 + lax.broadcasted_iota(jnp.int32, (t, POOL_WIDTH), 0)
    group = lax.broadcasted_iota(jnp.int32, (1, POOL_WIDTH), 1) // POOL_GROUP
    mean = jnp.zeros((t, POOL_WIDTH), F32)
    for gi, (wnd, asum) in enumerate(zip(POOL_WINDOWS, (a2, a4, a8, a16))):
        cnt = jnp.minimum(pos + wnd // 2, stream_len) - jnp.maximum(pos - wnd // 2, 0)
        mean = jnp.where(group == gi, asum[POOL_HALO:POOL_HALO + t] / cnt.astype(F32), mean)
    pool = jnp.dot((mean - u).astype(BF16), pw_ref[...], preferred_element_type=F32) * ps_ref[...]

    o = of_ref[...] + ob_ref[...]
    head = lax.broadcasted_iota(jnp.int32, (1, RET_WIDTH), 1) // RET_DK

    def head_mean(val):
        out = jnp.zeros_like(val)
        for hh in range(RET_HEADS):
            m = jnp.sum(jnp.where(head == hh, val, 0.0), axis=-1, keepdims=True) * (1.0 / RET_DK)
            out = jnp.where(head == hh, m, out)
        return out

    oc = o - head_mean(o)
    rn = oc * lax.rsqrt(head_mean(oc * oc) + LN_EPS)
    g = rg_ref[...]
    ret = rn * (g * _sigmoid(g))

    y = jnp.dot(da_ref[...], wo_ref[0:DA_WIDTH, :], preferred_element_type=F32)
    y = y + jnp.dot(pool.astype(BF16), wo_ref[DA_WIDTH:DA_WIDTH + POOL_WIDTH, :], preferred_element_type=F32)
    y = y + jnp.dot(ret.astype(BF16), wo_ref[DA_WIDTH + POOL_WIDTH:, :], preferred_element_type=F32)
    z = alpha * x_ref[...] + mod_ref[:, 2 * d:3 * d] * y
    o_ref[...] = _layer_norm_rows(z) * lng_ref[...] + lnb_ref[...]


def _mixout_call(x, da, u, o_f, o_b, rg, mod3, w_out_bf, pool_bd, pool_scale, ln_g, ln_b, *, tiles_per_batch, seq,
                 alpha):
    r, d = x.shape
    t = ROW_TILE
    nt = r // t
    hb = t // POOL_HALO
    n_halo_blocks = r // POOL_HALO
    row = lambda i: (i, 0)
    const = lambda i: (0, 0)
    kern = functools.partial(_mixout_kernel, tiles_per_batch=tiles_per_batch, seq=seq, alpha=alpha)
    return pl.pallas_call(
        kern,
        grid=(nt,),
        in_specs=[
            pl.BlockSpec((t, d), row),
            pl.BlockSpec((t, DA_WIDTH), row),
            pl.BlockSpec((t, POOL_WIDTH), row),
            pl.BlockSpec((POOL_HALO, POOL_WIDTH), lambda i: (jnp.maximum(i * hb - 1, 0), 0)),
            pl.BlockSpec((POOL_HALO, POOL_WIDTH), lambda i: (jnp.minimum((i + 1) * hb, n_halo_blocks - 1), 0)),
            pl.BlockSpec((t, RET_WIDTH), row),
            pl.BlockSpec((t, RET_WIDTH), row),
            pl.BlockSpec((t, RET_WIDTH), row),
            pl.BlockSpec((None, 1, 6 * d), lambda i: (_mod_row(i, tiles_per_batch), 0, 0)),
            pl.BlockSpec((d, d), const),
            pl.BlockSpec((POOL_WIDTH, POOL_WIDTH), const),
            pl.BlockSpec((1, POOL_WIDTH), const),
            pl.BlockSpec((1, d), const),
            pl.BlockSpec((1, d), const),
        ],
        out_specs=pl.BlockSpec((t, d), row),
        out_shape=jax.ShapeDtypeStruct((r, d), F32),
        compiler_params=_cparams("arbitrary"),
        name="mixer_out",
    )(x, da, u, u, u, o_f, o_b, rg, mod3, w_out_bf, pool_bd, pool_scale, ln_g, ln_b)


def _router_kernel(x_ref, mod_ref, wrh_ref, wrl_ref, bias_ref, wsgu_ref, wsdn_ref,
                   tokp_ref, idx_ref, gate_ref, rank_ref, cnt_ref, fsh_ref, carry_ref):
    d = D_MODEL
    t = x_ref.shape[0]
    ne = N_EXPERTS
    neg = -jnp.inf

    @pl.when(pl.program_id(0) == 0)
    def _():
        carry_ref[...] = jnp.zeros_like(carry_ref)

    tok = _layer_norm_rows(x_ref[...]) * (1.0 + mod_ref[:, 4 * d:5 * d]) + mod_ref[:, 3 * d:4 * d]
    tok_hi = tok.astype(BF16)
    tok_lo = (tok - tok_hi.astype(F32)).astype(BF16)

    tokp_ref[...] = _pack_bf16_pairs(tok)

    hs = jnp.dot(tok_hi, wsgu_ref[...], preferred_element_type=F32)
    gs, us = hs[:, 0:EXPERT_HIDDEN], hs[:, EXPERT_HIDDEN:]
    fsh_ref[...] = jnp.dot((gs * _sigmoid(gs) * us).astype(BF16), wsdn_ref[...], preferred_element_type=F32)

    nt_dims = (((1,), (1,)), ((), ()))
    logits = (lax.dot_general(wrh_ref[...], tok_hi, nt_dims, preferred_element_type=F32)
              + lax.dot_general(wrh_ref[...], tok_lo, nt_dims, preferred_element_type=F32)
              + lax.dot_general(wrl_ref[...], tok_hi, nt_dims, preferred_element_type=F32))
    scores = _sigmoid(logits)
    biased = scores + bias_ref[...]

    gidx = lax.broadcasted_iota(jnp.int32, (GROUP_SIZE, t), 0)
    blocks, gscores = [], []
    for g in range(N_GROUPS):
        blk = biased[g * GROUP_SIZE:(g + 1) * GROUP_SIZE, :]
        m1 = jnp.max(blk, axis=0, keepdims=True)
        first = jnp.min(jnp.where(blk == m1, gidx, GROUP_SIZE), axis=0, keepdims=True)
        m2 = jnp.max(jnp.where(gidx == first, neg, blk), axis=0, keepdims=True)
        blocks.append(blk)
        gscores.append(m1 + m2)

    keep = [jnp.zeros((1, t), F32) for _ in range(N_GROUPS)]
    for _ in range(TOPK_GROUPS):
        m = gscores[0]
        for gs_ in gscores[1:]:
            m = jnp.maximum(m, gs_)
        found = jnp.zeros((1, t), F32)
        for g in range(N_GROUPS):
            hit = jnp.where(gscores[g] == m, 1.0 - found, 0.0)
            found = found + hit
            keep[g] = keep[g] + hit
            gscores[g] = jnp.where(hit > 0.0, neg, gscores[g])
    masked = jnp.concatenate([jnp.where(keep[g] > 0.0, blocks[g], neg) for g in range(N_GROUPS)], axis=0)

    ei = lax.broadcasted_iota(jnp.int32, (ne, t), 0)
    cur = masked
    onehot = jnp.zeros((ne, t), F32)
    idxs, gates = [], []
    for _ in range(TOP_K):
        m = jnp.max(cur, axis=0, keepdims=True)
        ii = jnp.min(jnp.where(cur == m, ei, ne), axis=0, keepdims=True)
        sel = ei == ii
        idxs.append(ii)
        gates.append(jnp.sum(jnp.where(sel, scores, 0.0), axis=0, keepdims=True))
        onehot = jnp.where(sel, 1.0, onehot)
        cur = jnp.where(sel, neg, cur)
    gsum = gates[0]
    for gk in gates[1:]:
        gsum = gsum + gk
    for k in range(TOP_K):
        idx_ref[k:k + 1, :] = idxs[k]
        gate_ref[k:k + 1, :] = gates[k] / gsum * ROUTED_SCALE

    ti = lax.broadcasted_iota(jnp.int32, (t, t), 0)
    tj = lax.broadcasted_iota(jnp.int32, (t, t), 1)
    before = jnp.where(ti < tj, 1.0, 0.0).astype(BF16)
    prefix = jnp.dot(onehot.astype(BF16), before, preferred_element_type=F32) + carry_ref[:, 0:1]
    for k in range(TOP_K):
        rank_k = jnp.sum(jnp.where(ei == idxs[k], prefix, 0.0), axis=0, keepdims=True)
        rank_ref[k:k + 1, :] = rank_k.astype(jnp.int32)
    carry_ref[...] = carry_ref[...] + jnp.sum(onehot, axis=1, keepdims=True)
    cnt_ref[...] = carry_ref[...].astype(jnp.int32)


def _router_call(x, mod3, wr_hi, wr_lo, bias_col, ws_gu_bf, ws_dn_bf, *, tiles_per_batch):
    r, d = x.shape
    t = ROW_TILE
    nt = r // t
    row = lambda i: (i, 0)
    col = lambda i: (0, i)
    const = lambda i: (0, 0)
    return pl.pallas_call(
        _router_kernel,
        grid=(nt,),
        in_specs=[
            pl.BlockSpec((t, d), row),
            pl.BlockSpec((None, 1, 6 * d), lambda i: (_mod_row(i, tiles_per_batch), 0, 0)),
            pl.BlockSpec((N_EXPERTS, d), const),
            pl.BlockSpec((N_EXPERTS, d), const),
            pl.BlockSpec((N_EXPERTS, 1), const),
            pl.BlockSpec((d, 2 * EXPERT_HIDDEN), const),
            pl.BlockSpec((EXPERT_HIDDEN, d), const),
        ],
        out_specs=[
            pl.BlockSpec((t, PACK_W), row),
            pl.BlockSpec((TOP_K, t), col),
            pl.BlockSpec((TOP_K, t), col),
            pl.BlockSpec((TOP_K, t), col),
            pl.BlockSpec((N_EXPERTS, LANES), const),
            pl.BlockSpec((t, d), row),
        ],
        out_shape=[
            jax.ShapeDtypeStruct((r, PACK_W), jnp.int32),
            jax.ShapeDtypeStruct((TOP_K, r), jnp.int32),
            jax.ShapeDtypeStruct((TOP_K, r), F32),
            jax.ShapeDtypeStruct((TOP_K, r), jnp.int32),
            jax.ShapeDtypeStruct((N_EXPERTS, LANES), jnp.int32),
            jax.ShapeDtypeStruct((r, d), F32),
        ],
        scratch_shapes=[pltpu.VMEM((N_EXPERTS, LANES), F32)],
        compiler_params=_cparams("arbitrary"),
        name="router",
    )(x, mod3, wr_hi, wr_lo, bias_col, ws_gu_bf, ws_dn_bf)


def _dest_kernel(idx_ref, rank_ref, offs_ref, dest_ref):
    t = idx_ref.shape[1]
    ei = lax.broadcasted_iota(jnp.int32, (N_EXPERTS, t), 0)
    offs = offs_ref[...].astype(F32)
    for k in range(TOP_K):
        start = jnp.sum(jnp.where(ei == idx_ref[k:k + 1, :], offs, 0.0), axis=0, keepdims=True)
        dest_ref[k:k + 1, :] = start.astype(jnp.int32) + rank_ref[k:k + 1, :]


def _dest_call(idx, rank, offs_col):
    r = idx.shape[1]
    t = r // DEST_STEPS
    assert r % DEST_STEPS == 0 and t % LANES == 0
    col = lambda i: (0, i)
    return pl.pallas_call(
        _dest_kernel,
        grid=(r // t,),
        in_specs=[pl.BlockSpec((TOP_K, t), col), pl.BlockSpec((TOP_K, t), col),
                  pl.BlockSpec((N_EXPERTS, 1), lambda i: (0, 0))],
        out_specs=pl.BlockSpec((TOP_K, t), col),
        out_shape=jax.ShapeDtypeStruct((TOP_K, r), jnp.int32),
        compiler_params=_cparams("arbitrary"),
        name="moe_dest",
    )(idx, rank, offs_col)


def _sc_dispatch(tokp, dest_flat, pad_rows, n_sorted):
    r, width = tokp.shape
    win = SC_GATHER_WINDOW
    workers = SC_NUM_CORES * SC_NUM_SUBCORES
    token_windows = r // win
    n_pad_windows = pad_rows.shape[0] // win
    assert r % win == 0 and dest_flat.shape[0] == TOP_K * r and n_pad_windows % workers == 0
    windows_per_worker = -(-token_windows // workers)
    pads_per_worker = n_pad_windows // workers
    mesh = plsc.VectorSubcoreMesh(core_axis_name="core", subcore_axis_name="subcore", num_cores=SC_NUM_CORES,
                                  num_subcores=SC_NUM_SUBCORES)
    zero_rows = jnp.zeros((win, width), tokp.dtype)

    @functools.partial(
        pl.kernel, out_type=jax.ShapeDtypeStruct((n_sorted + SC_SPARE_ROWS, width), tokp.dtype), mesh=mesh,
        scratch_types=[pltpu.VMEM((win,), jnp.int32), pltpu.VMEM((win, width), tokp.dtype),
                       pltpu.SemaphoreType.DMA],
        name="moe_sc_dispatch")
    def dispatch_kernel(tok_hbm, dest_hbm, pad_hbm, zero_hbm, xs_hbm, idx_vmem, rows_vmem, sem):
        worker = lax.axis_index("subcore") * SC_NUM_CORES + lax.axis_index("core")

        @pl.loop(0, windows_per_worker)
        def _(j):
            window = j * workers + worker

            @pl.when(window < token_windows)
            def _():
                tok0 = window * win
                pltpu.sync_copy(tok_hbm.at[pl.ds(tok0, win)], rows_vmem)
                for k in range(TOP_K):
                    pltpu.sync_copy(dest_hbm.at[pl.ds(k * r + tok0, win)], idx_vmem)
                    pltpu.async_copy(rows_vmem, xs_hbm.at[idx_vmem], sem).wait()

        pltpu.sync_copy(zero_hbm, rows_vmem)

        @pl.loop(0, pads_per_worker)
        def _(j):
            off = (worker * pads_per_worker + j) * win
            pltpu.sync_copy(pad_hbm.at[pl.ds(off, win)], idx_vmem)
            pltpu.async_copy(rows_vmem, xs_hbm.at[idx_vmem], sem).wait()

    return dispatch_kernel(tokp, dest_flat, pad_rows, zero_rows)


def _expert_kernel(be_ref, nb_ref, ord_ref, ue_ref, nue_ref, xs_ref, wgu_hbm, wdn_hbm, ys_ref, wgu_f32, wdn_f32,
                   wgu_bf, wdn_bf, sems, *, layer):
    def weight_copies(o):
        slot = o % 2
        e = ue_ref[o]
        return (pltpu.make_async_copy(wgu_hbm.at[layer, e], wgu_f32.at[slot], sems.at[0, slot]),
                pltpu.make_async_copy(wdn_hbm.at[layer, e], wdn_f32.at[slot], sems.at[1, slot]))

    def start_weights(o):
        @pl.when(o < nue_ref[0])
        def _():
            for cp in weight_copies(o):
                cp.start()

    def one_block(sub, carry):
        j = pl.program_id(0) * EXPERT_BLOCKS_PER_STEP + sub

        @pl.when(j < nb_ref[0])
        def _():
            o = ord_ref[j]
            changed = jnp.logical_or(j == 0, be_ref[j] != be_ref[jnp.maximum(j - 1, 0)])

            @pl.when(j == 0)
            def _():
                start_weights(0)
                start_weights(1)

            @pl.when(changed)
            def _():
                for cp in weight_copies(o):
                    cp.wait()
                slot = o % 2
                wgu_bf[...] = wgu_f32[slot].astype(BF16)
                wdn_bf[...] = wdn_f32[slot].astype(BF16)
                start_weights(o + 2)

            rows = pl.ds(pl.multiple_of(sub * EXPERT_BLOCK, EXPERT_BLOCK), EXPERT_BLOCK)
            x_lo, x_hi = _unpack_bf16_pairs(xs_ref[rows, :])
            h = (jnp.dot(x_lo.astype(BF16), wgu_bf[0:PACK_W, :], preferred_element_type=F32)
                 + jnp.dot(x_hi.astype(BF16), wgu_bf[PACK_W:, :], preferred_element_type=F32))
            g, u = h[:, 0:EXPERT_HIDDEN], h[:, EXPERT_HIDDEN:]
            y = jnp.dot((g * _sigmoid(g) * u).astype(BF16), wdn_bf[...], preferred_element_type=F32)
            ys_ref[rows, :] = _pack_bf16_pairs(y)

        return carry

    lax.fori_loop(0, EXPERT_BLOCKS_PER_STEP, one_block, 0)


def _expert_call(block_expert, n_blocks_used, block_ordinal, used_expert, n_used_experts, xs, w_gu, w_dn, layer):
    n_rows = block_expert.shape[0] * EXPERT_BLOCK
    bm = EXPERT_BLOCK
    d = D_MODEL
    step_rows = bm * EXPERT_BLOCKS_PER_STEP
    assert n_rows % step_rows == 0
    used_step = lambda s, be, nb, od, ue, nue: (jnp.minimum(s, (nb[0] - 1) // EXPERT_BLOCKS_PER_STEP), 0)
    grid_spec = pltpu.PrefetchScalarGridSpec(
        num_scalar_prefetch=5,
        grid=(n_rows // step_rows,),
        in_specs=[
            pl.BlockSpec((step_rows, PACK_W), used_step),
            pl.BlockSpec(memory_space=pl.ANY),
            pl.BlockSpec(memory_space=pl.ANY),
        ],
        out_specs=pl.BlockSpec((step_rows, PACK_W), used_step),
        scratch_shapes=[
            pltpu.VMEM((2, d, 2 * EXPERT_HIDDEN), F32),
            pltpu.VMEM((2, EXPERT_HIDDEN, d), F32),
            pltpu.VMEM((d, 2 * EXPERT_HIDDEN), BF16),
            pltpu.VMEM((EXPERT_HIDDEN, d), BF16),
            pltpu.SemaphoreType.DMA((2, 2)),
        ],
    )
    return pl.pallas_call(
        functools.partial(_expert_kernel, layer=layer),
        grid_spec=grid_spec,
        out_shape=jax.ShapeDtypeStruct((n_rows, PACK_W), jnp.int32),
        compiler_params=_cparams("arbitrary"),
        name="moe_experts",
    )(block_expert, n_blocks_used, block_ordinal, used_expert, n_used_experts, xs, w_gu, w_dn)


def _sc_gather_rows(table, indices):
    n = indices.shape[0]
    width = table.shape[1]
    workers = SC_NUM_CORES * SC_NUM_SUBCORES
    assert n % (SC_GATHER_WINDOW * workers) == 0
    per_worker = n // workers
    mesh = plsc.VectorSubcoreMesh(core_axis_name="core", subcore_axis_name="subcore", num_cores=SC_NUM_CORES,
                                  num_subcores=SC_NUM_SUBCORES)

    @functools.partial(
        pl.kernel, out_type=jax.ShapeDtypeStruct((n, width), table.dtype), mesh=mesh,
        scratch_types=[pltpu.VMEM((SC_GATHER_WINDOW,), jnp.int32),
                       pltpu.VMEM((SC_GATHER_WINDOW, width), table.dtype),
                       pltpu.SemaphoreType.DMA],
        name="moe_sc_gather")
    def gather_kernel(table_hbm, idx_hbm, out_hbm, idx_vmem, rows_vmem, sem):
        worker = lax.axis_index("subcore") * SC_NUM_CORES + lax.axis_index("core")
        base = worker * per_worker

        @pl.loop(0, per_worker // SC_GATHER_WINDOW)
        def _(w):
            off = base + w * SC_GATHER_WINDOW
            pltpu.sync_copy(idx_hbm.at[pl.ds(off, SC_GATHER_WINDOW)], idx_vmem)
            pltpu.async_copy(table_hbm.at[idx_vmem], rows_vmem, sem).wait()
            pltpu.sync_copy(rows_vmem, out_hbm.at[pl.ds(off, SC_GATHER_WINDOW)])

    return gather_kernel(table, indices)


def _combine_kernel(*refs, alpha):
    y_refs = refs[:TOP_K]
    x_ref, fsh_ref, gate_ref, mod_ref, lng_ref, lnb_ref, o_ref = refs[TOP_K:]
    d = D_MODEL
    t = x_ref.shape[0]
    gate_rows = gate_ref[...]
    pad = jnp.zeros((LANES - TOP_K, t), F32)
    gate_cols = jnp.concatenate([gate_rows, pad], axis=0).T
    f_lo = fsh_ref[:, 0:PACK_W]
    f_hi = fsh_ref[:, PACK_W:]
    for k in range(TOP_K):
        y_lo, y_hi = _unpack_bf16_pairs(y_refs[k][...])
        f_lo = f_lo + gate_cols[:, k:k + 1] * y_lo
        f_hi = f_hi + gate_cols[:, k:k + 1] * y_hi
    f = jnp.concatenate([f_lo, f_hi], axis=1)
    z = alpha * x_ref[...] + mod_ref[:, 5 * d:6 * d] * f
    o_ref[...] = _layer_norm_rows(z) * lng_ref[...] + lnb_ref[...]


def _combine_call(y_tok, x, fsh, gate, mod3, ln_g, ln_b, *, tiles_per_batch, alpha, drop_context):
    r, d = x.shape
    t = ROW_TILE
    nt = r // t
    if drop_context:
        per_batch = tiles_per_batch - 1
        src = lambda i: (i // per_batch) * tiles_per_batch + i % per_batch
        n_tiles = nt // tiles_per_batch * per_batch
    else:
        src = lambda i: i
        n_tiles = nt
    row = lambda i: (src(i), 0)
    col = lambda i: (0, src(i))
    const = lambda i: (0, 0)
    kern = functools.partial(_combine_kernel, alpha=alpha)
    y_specs = [pl.BlockSpec((t, PACK_W), functools.partial(lambda k, i: (k * nt + src(i), 0), k))
               for k in range(TOP_K)]
    return pl.pallas_call(
        kern,
        grid=(n_tiles,),
        in_specs=y_specs + [
            pl.BlockSpec((t, d), row),
            pl.BlockSpec((t, d), row),
            pl.BlockSpec((TOP_K, t), col),
            pl.BlockSpec((None, 1, 6 * d), lambda i: (_mod_row(src(i), tiles_per_batch), 0, 0)),
            pl.BlockSpec((1, d), const),
            pl.BlockSpec((1, d), const),
        ],
        out_specs=pl.BlockSpec((t, d), lambda i: (i, 0)),
        out_shape=jax.ShapeDtypeStruct((n_tiles * t, d), F32),
        compiler_params=_cparams("arbitrary"),
        name="moe_combine",
    )(*([y_tok] * TOP_K), x, fsh, gate, mod3, ln_g, ln_b)


def _rope_tables(seq):
    rows = seq // GRID_W
    row = jnp.repeat(jnp.arange(rows, dtype=F32), GRID_W)
    col = jnp.tile(jnp.arange(GRID_W, dtype=F32), rows)
    nf = DA_DIM // 4
    freqs = ROPE_BASE ** (-jnp.arange(nf, dtype=F32) / nf)
    cr, sr = jnp.cos(row[:, None] * freqs), jnp.sin(row[:, None] * freqs)
    cc, sc = jnp.cos(col[:, None] * freqs), jnp.sin(col[:, None] * freqs)
    c64 = jnp.concatenate([cr, cr, cc, cc], axis=1)
    s64 = jnp.concatenate([-sr, sr, -sc, sc], axis=1)
    c = jnp.concatenate([jnp.tile(c64, (1, 2)), jnp.ones((CTX_LEN, LANES), F32)], axis=0)
    s = jnp.concatenate([jnp.tile(s64, (1, 2)), jnp.zeros((CTX_LEN, LANES), F32)], axis=0)
    return c, s


def kernel(x, c, ctx, c_ctx, w_mod, b_mod, w_in, w_out, diff_lambda, pool_w, pool_scale, ret_log_decay, ln_g, ln_b,
           w_router, router_bias, w_expert_gate_up, w_expert_down, w_shared_gate_up, w_shared_down):
    batch, seq, d = x.shape
    depth = w_mod.shape[0]
    assert d == D_MODEL and ctx.shape[1] == CTX_LEN == ROW_TILE and batch == 2
    assert seq % ROW_TILE == 0 and seq % GRID_W == 0 and w_in.shape[-1] == IN_WIDTH
    rows_per_batch = seq + CTX_LEN
    tiles_per_batch = rows_per_batch // ROW_TILE
    r = batch * rows_per_batch
    alpha = (2.0 * depth) ** 0.25

    xa = jnp.concatenate([x, ctx], axis=1).reshape(r, d)
    cvec = jnp.zeros((8, d), F32).at[0:batch].set(c).at[batch].set(c_ctx)
    mod_all = _mod_call(cvec, w_mod, b_mod)
    rope_c, rope_s = _rope_tables(seq)

    n_sorted = r * TOP_K + N_EXPERTS * EXPERT_BLOCK
    n_blocks = n_sorted // EXPERT_BLOCK

    for l in range(depth):
        lambda_init = 0.8 - 0.6 * math.exp(-0.3 * l)
        mod3 = mod_all[l].reshape(8, 1, 6 * d)
        lng = ln_g[l].reshape(2, 1, d)
        lnb = ln_b[l].reshape(2, 1, d)

        w_in_bf = w_in[l].astype(BF16)
        w_vt_bf = w_in_bf[:, QK_WIDTH:QK_WIDTH + DA_WIDTH].T
        qk, vda, u, rqkv, rg = _inproj_call(xa, mod3, w_in_bf, w_vt_bf, rope_c, rope_s, tiles_per_batch)
        da = _attn_call(diff_lambda[l], qk, vda, batch=batch, rows_per_batch=rows_per_batch, seq=seq,
                        lambda_init=lambda_init)
        o_f, o_b = _ret_call(ret_log_decay[l], rqkv, batch=batch, rows_per_batch=rows_per_batch, seq=seq)
        pool_bd = jnp.zeros((POOL_WIDTH, POOL_WIDTH), F32)
        for gi in range(len(POOL_WINDOWS)):
            sl = slice(gi * POOL_GROUP, (gi + 1) * POOL_GROUP)
            pool_bd = pool_bd.at[sl, sl].set(pool_w[l, gi])
        xa = _mixout_call(xa, da, u, o_f, o_b, rg, mod3, w_out[l].astype(BF16), pool_bd.astype(BF16),
                          pool_scale[l].reshape(1, POOL_WIDTH), lng[0], lnb[0],
                          tiles_per_batch=tiles_per_batch, seq=seq, alpha=alpha)

        wr_t = w_router[l].T
        wr_hi = wr_t.astype(BF16)
        wr_lo = (wr_t - wr_hi.astype(F32)).astype(BF16)
        tokp, idx, gate, rank, cnt, fsh = _router_call(
            xa, mod3, wr_hi, wr_lo, router_bias[l].reshape(N_EXPERTS, 1),
            w_shared_gate_up[l].astype(BF16), w_shared_down[l].astype(BF16), tiles_per_batch=tiles_per_batch)
        counts = cnt[:, 0]
        padded = (counts + EXPERT_BLOCK - 1) // EXPERT_BLOCK * EXPERT_BLOCK
        pad_end = jnp.cumsum(padded)
        offs = pad_end - padded
        expert_ids = jnp.arange(N_EXPERTS, dtype=jnp.int32)
        blk_row = jnp.arange(n_blocks, dtype=jnp.int32) * EXPERT_BLOCK
        block_expert = jnp.minimum(jnp.sum(pad_end[None, :] <= blk_row[:, None], axis=1), N_EXPERTS - 1)
        n_used = pad_end[-1:] // EXPERT_BLOCK
        used = counts > 0
        ordinal = jnp.cumsum(used) - 1
        hit = used[None, :] & (ordinal[None, :] == expert_ids[:, None])
        used_expert = jnp.sum(jnp.where(hit, expert_ids[None, :], 0), axis=1)
        n_used_experts = jnp.sum(used)[None]
        block_ordinal = ordinal[block_expert]
        slot = jnp.arange(EXPERT_BLOCK, dtype=jnp.int32)[None, :]
        first_pad = (padded - EXPERT_BLOCK)[:, None] + slot
        is_pad = (first_pad >= counts[:, None]) & (padded[:, None] > 0)
        spare = n_sorted + jnp.arange(N_EXPERTS * EXPERT_BLOCK, dtype=jnp.int32).reshape(N_EXPERTS, EXPERT_BLOCK)
        pad_rows = jnp.where(is_pad, offs[:, None] + first_pad, spare).reshape(N_EXPERTS * EXPERT_BLOCK)
        i32 = lambda a: a.astype(jnp.int32)

        dest = _dest_call(idx, rank, i32(offs).reshape(N_EXPERTS, 1))
        dest_flat = dest.reshape(TOP_K * r)
        xs = _sc_dispatch(tokp, dest_flat, i32(pad_rows), n_sorted)
        ys = _expert_call(i32(block_expert), i32(n_used), i32(block_ordinal), i32(used_expert),
                          i32(n_used_experts), xs, w_expert_gate_up, w_expert_down, l)
        y_tok = _sc_gather_rows(ys, dest_flat)
        xa = _combine_call(y_tok, xa, fsh, gate, mod3, lng[1], lnb[1], tiles_per_batch=tiles_per_batch,
                           alpha=alpha, drop_context=(l == depth - 1))

    return xa.reshape(batch, seq, d)
```

```python
import functools
import math

import jax
import jax.numpy as jnp
from jax import lax
from jax.experimental import pallas as pl
from jax.experimental.pallas import tpu as pltpu
from jax.experimental.pallas import tpu_sc as plsc

F32 = jnp.float32
BF16 = jnp.bfloat16
HIGHEST = lax.Precision.HIGHEST

D_MODEL = 1024
CTX_LEN = 256
GRID_W = 64
DA_HEADS = 4
DA_DIM = 64
DA_VDIM = 2 * DA_DIM
DA_WIDTH = DA_HEADS * DA_VDIM
ROPE_BASE = 10000.0
POOL_WINDOWS = (2, 4, 8, 16)
POOL_GROUP = 64
POOL_WIDTH = len(POOL_WINDOWS) * POOL_GROUP
POOL_HALO = 8
RET_HEADS = 4
RET_DK = 64
RET_WIDTH = RET_HEADS * RET_DK
RET_CHUNK = 128
QK_WIDTH = 2 * DA_HEADS * 2 * DA_DIM
IN_WIDTH = QK_WIDTH + DA_WIDTH + POOL_WIDTH + 4 * RET_WIDTH
N_EXPERTS = 256
TOP_K = 8
N_GROUPS = 8
GROUP_SIZE = N_EXPERTS // N_GROUPS
TOPK_GROUPS = 4
EXPERT_HIDDEN = 256
ROUTED_SCALE = 2.5
LN_EPS = 1e-6
RMS_EPS = 1e-5

LANES = 128
ROW_TILE = 256
DEST_STEPS = 4
ATTN_Q_TILE = 256
ATTN_K_CHUNK = 256
ATTN_UNROLL = 16
SC_NUM_CORES = 2
SC_NUM_SUBCORES = 16
SC_GATHER_WINDOW = 128
EXPERT_BLOCK = 256
EXPERT_BLOCKS_PER_STEP = 8
SC_SPARE_ROWS = N_EXPERTS * EXPERT_BLOCK
PACK_W = D_MODEL // 2
VMEM_LIMIT = 56 * 1024 * 1024


def _cparams(*sem):
    return pltpu.CompilerParams(dimension_semantics=sem, vmem_limit_bytes=VMEM_LIMIT)


def _sigmoid(x):
    return 1.0 / (1.0 + jnp.exp(-x))


def _layer_norm_rows(x):
    mu = jnp.mean(x, axis=-1, keepdims=True)
    xc = x - mu
    var = jnp.mean(xc * xc, axis=-1, keepdims=True)
    return xc * lax.rsqrt(var + LN_EPS)


def _pack_bf16_pairs(x):
    half = x.shape[1] // 2
    bits = pltpu.bitcast(x.astype(BF16).astype(F32), jnp.uint32)
    word = lax.shift_right_logical(bits[:, 0:half], jnp.uint32(16)) | (bits[:, half:] & jnp.uint32(0xFFFF0000))
    return pltpu.bitcast(word, jnp.int32)


def _unpack_bf16_pairs(packed):
    word = pltpu.bitcast(packed, jnp.uint32)
    lo = pltpu.bitcast(lax.shift_left(word, jnp.uint32(16)), F32)
    hi = pltpu.bitcast(word & jnp.uint32(0xFFFF0000), F32)
    return lo, hi


def _mod_row(i, tiles_per_batch):
    return jnp.where(i % tiles_per_batch == tiles_per_batch - 1, 2, i // tiles_per_batch)


def _mod_kernel(c_ref, w_ref, b_ref, o_ref):
    c = c_ref[...]
    s = c * _sigmoid(c)
    o_ref[...] = jnp.dot(s, w_ref[...], precision=HIGHEST, preferred_element_type=F32) + b_ref[...]


def _mod_call(cvec, w_mod, b_mod):
    depth, d, n = w_mod.shape
    tn = 1536
    return pl.pallas_call(
        _mod_kernel,
        grid=(depth, n // tn),
        in_specs=[
            pl.BlockSpec((8, d), lambda l, j: (0, 0)),
            pl.BlockSpec((None, d, tn), lambda l, j: (l, 0, j)),
            pl.BlockSpec((None, 1, tn), lambda l, j: (l, 0, j)),
        ],
        out_specs=pl.BlockSpec((None, 8, tn), lambda l, j: (l, 0, j)),
        out_shape=jax.ShapeDtypeStruct((depth, 8, n), F32),
        compiler_params=_cparams("arbitrary", "arbitrary"),
        name="mod",
    )(cvec, w_mod, b_mod.reshape(depth, 1, n))


def _inproj_kernel(x_ref, mod_ref, w_ref, wvt_ref, ct_ref, st_ref, qk_ref, vt_ref, u_ref, r_ref, g_ref):
    d = D_MODEL
    xn = _layer_norm_rows(x_ref[...])
    h = (xn * (1.0 + mod_ref[:, d:2 * d]) + mod_ref[:, 0:d]).astype(BF16)

    a = jnp.dot(h, w_ref[:, 0:QK_WIDTH], preferred_element_type=F32)
    lane = lax.broadcasted_iota(jnp.int32, (a.shape[0], LANES), 1)
    first_half = (lane % 32) < 16
    ct = ct_ref[...]
    st = st_ref[...]
    for s in range(QK_WIDTH // LANES):
        blk = a[:, s * LANES:(s + 1) * LANES]
        partner = jnp.where(first_half, pltpu.roll(blk, LANES - 16, 1), pltpu.roll(blk, 16, 1))
        rot = blk * ct + partner * st
        if s < QK_WIDTH // LANES // 2:
            rot = rot * (DA_DIM ** -0.5 * math.log2(math.e))
        qk_ref[:, s * LANES:(s + 1) * LANES] = rot.astype(BF16)

    vt_ref[...] = lax.dot_general(wvt_ref[...], h, (((1,), (1,)), ((), ())),
                                  preferred_element_type=F32).astype(BF16)
    o = QK_WIDTH + DA_WIDTH
    u_ref[...] = jnp.dot(h, w_ref[:, o:o + POOL_WIDTH], preferred_element_type=F32)
    o += POOL_WIDTH
    r = jnp.dot(h, w_ref[:, o:o + 3 * RET_WIDTH], preferred_element_type=F32)
    r_ref[:, 0:RET_WIDTH] = r[:, 0:RET_WIDTH].astype(BF16)
    r_ref[:, RET_WIDTH:2 * RET_WIDTH] = (r[:, RET_WIDTH:2 * RET_WIDTH] * (RET_DK ** -0.5)).astype(BF16)
    r_ref[:, 2 * RET_WIDTH:] = r[:, 2 * RET_WIDTH:].astype(BF16)
    o += 3 * RET_WIDTH
    g_ref[...] = jnp.dot(h, w_ref[:, o:o + RET_WIDTH], preferred_element_type=F32)


def _inproj_call(x, mod3, w_in_bf, w_vt_bf, rope_c, rope_s, tiles_per_batch):
    r, d = x.shape
    t = ROW_TILE
    nt = r // t
    row = lambda i: (i, 0)
    return pl.pallas_call(
        _inproj_kernel,
        grid=(nt,),
        in_specs=[
            pl.BlockSpec((t, d), row),
            pl.BlockSpec((None, 1, 6 * d), lambda i: (_mod_row(i, tiles_per_batch), 0, 0)),
            pl.BlockSpec((d, IN_WIDTH), lambda i: (0, 0)),
            pl.BlockSpec((DA_WIDTH, d), lambda i: (0, 0)),
            pl.BlockSpec((t, LANES), lambda i: (i % tiles_per_batch, 0)),
            pl.BlockSpec((t, LANES), lambda i: (i % tiles_per_batch, 0)),
        ],
        out_specs=[
            pl.BlockSpec((t, QK_WIDTH), row),
            pl.BlockSpec((DA_WIDTH, t), lambda i: (0, i)),
            pl.BlockSpec((t, POOL_WIDTH), row),
            pl.BlockSpec((t, 3 * RET_WIDTH), row),
            pl.BlockSpec((t, RET_WIDTH), row),
        ],
        out_shape=[
            jax.ShapeDtypeStruct((r, QK_WIDTH), BF16),
            jax.ShapeDtypeStruct((DA_WIDTH, r), BF16),
            jax.ShapeDtypeStruct((r, POOL_WIDTH), F32),
            jax.ShapeDtypeStruct((r, 3 * RET_WIDTH), BF16),
            jax.ShapeDtypeStruct((r, RET_WIDTH), F32),
        ],
        compiler_params=_cparams("arbitrary"),
        name="inproj",
    )(x, mod3, w_in_bf, w_vt_bf, rope_c, rope_s)


def _attn_kernel(lam_ref, q_ref, k_ref, vt_ref, o_ref, s_ref, *, k_chunk, seq, lambda_init):
    n_tiles = (seq + CTX_LEN) // ATTN_Q_TILE

    def tile(i, carry):
        rows = pl.ds(pl.multiple_of(i * ATTN_Q_TILE, ATTN_Q_TILE), ATTN_Q_TILE)
        o_ref[rows, :] = _attn_tile(lam_ref, q_ref[rows, :], k_ref, vt_ref, s_ref, i == n_tiles - 1,
                                    k_chunk=k_chunk, seq=seq, lambda_init=lambda_init)
        return carry

    lax.fori_loop(0, n_tiles, tile, 0)


def _attn_tile(lam_ref, q, k_ref, vt_ref, s_ref, is_ctx_tile, *, k_chunk, seq, lambda_init):
    mq = q.shape[0]
    lane = lax.broadcasted_iota(jnp.int32, q.shape, 1)
    zero = jnp.zeros_like(q)
    q2 = jnp.concatenate([jnp.where(lane < DA_DIM, q, zero), jnp.where(lane >= DA_DIM, q, zero)], axis=0)
    qt = q2.astype(F32).T.astype(BF16)

    n_chunks = (seq + CTX_LEN) // k_chunk
    last = n_chunks - 1
    n_iters = jnp.where(is_ctx_tile, 0, last // ATTN_UNROLL)

    def score_chunk(c, m):
        off = pl.multiple_of(c * k_chunk, k_chunk)
        s = jnp.dot(k_ref[pl.ds(off, k_chunk), :], qt, preferred_element_type=F32)
        s_ref[c] = s
        return jnp.maximum(m, jnp.max(s, axis=0, keepdims=True))

    def pass1(it, m):
        for u in range(ATTN_UNROLL):
            m = score_chunk(it * ATTN_UNROLL + u, m)
        return m

    m = lax.fori_loop(0, n_iters, pass1, jnp.full((1, 2 * mq), -jnp.inf, F32))
    m = score_chunk(last, m)

    ones_rows = jnp.where(lax.broadcasted_iota(jnp.int32, (16, k_chunk), 0) == 0, 1.0, 0.0).astype(BF16)

    def value_chunk(c, acc):
        off = pl.multiple_of(c * k_chunk, k_chunk)
        vt = jnp.concatenate([vt_ref[:, pl.ds(off, k_chunk)], ones_rows], axis=0)
        p = jnp.exp2((s_ref[c] - m).astype(BF16))
        return acc + jnp.dot(vt, p, preferred_element_type=F32)

    def pass2(it, acc):
        for u in range(ATTN_UNROLL):
            acc = value_chunk(it * ATTN_UNROLL + u, acc)
        return acc

    acc = lax.fori_loop(0, n_iters, pass2, jnp.zeros((DA_VDIM + 16, 2 * mq), F32))
    acc = value_chunk(last, acc)
    l0, l1 = acc[DA_VDIM:DA_VDIM + 1, 0:mq], acc[DA_VDIM:DA_VDIM + 1, mq:]
    a0, a1 = acc[0:DA_VDIM, 0:mq], acc[0:DA_VDIM, mq:]

    lv = lam_ref[...]
    lam = (jnp.exp(jnp.sum(lv[0:1] * lv[1:2], axis=-1, keepdims=True))
           - jnp.exp(jnp.sum(lv[2:3] * lv[3:4], axis=-1, keepdims=True)) + lambda_init)
    o = a0 / l0 - lam * (a1 / l1)
    o = o * lax.rsqrt(jnp.mean(o * o, axis=0, keepdims=True) + RMS_EPS) * (1.0 - lambda_init)
    return o.T.astype(BF16)


def _attn_call(lam_vec, qk, vda, *, batch, rows_per_batch, seq, lambda_init):
    tq = ATTN_Q_TILE
    assert seq % (ATTN_K_CHUNK * ATTN_UNROLL) == 0 and rows_per_batch - seq == CTX_LEN == tq == ATTN_K_CHUNK
    nq = rows_per_batch // tq
    kern = functools.partial(_attn_kernel, k_chunk=ATTN_K_CHUNK, seq=seq, lambda_init=lambda_init)
    return pl.pallas_call(
        kern,
        grid=(batch, DA_HEADS),
        in_specs=[
            pl.BlockSpec((4, DA_DIM), lambda b, h: (0, 0)),
            pl.BlockSpec((rows_per_batch, DA_VDIM), lambda b, h: (b, h)),
            pl.BlockSpec((rows_per_batch, DA_VDIM), lambda b, h: (b, DA_HEADS + h)),
            pl.BlockSpec((DA_VDIM, rows_per_batch), lambda b, h: (h, b)),
        ],
        out_specs=pl.BlockSpec((rows_per_batch, DA_VDIM), lambda b, h: (b, h)),
        out_shape=jax.ShapeDtypeStruct((qk.shape[0], DA_WIDTH), BF16),
        scratch_shapes=[pltpu.VMEM((rows_per_batch // ATTN_K_CHUNK, ATTN_K_CHUNK, 2 * tq), F32)],
        compiler_params=_cparams("arbitrary", "arbitrary"),
        name="diff_attn",
    )(lam_vec, qk, qk, vda)


def _ret_kernel(ld_ref, f_ref, b_ref, of_ref, ob_ref, dm_ref, qd_ref, kd_ref, cd_ref, st_ref):
    c = pl.program_id(1)
    ch = RET_CHUNK
    w = RET_WIDTH
    lane_head = lax.broadcasted_iota(jnp.int32, (1, w), 1) // RET_DK

    @pl.when(c == 0)
    def _():
        st_ref[...] = jnp.zeros_like(st_ref)
        ri = lax.broadcasted_iota(jnp.int32, (ch, ch), 0)
        ci = lax.broadcasted_iota(jnp.int32, (ch, ch), 1)
        rowf = lax.broadcasted_iota(jnp.int32, (ch, w), 0).astype(F32)
        for d in range(2):
            lg_lane = jnp.zeros((1, w), F32)
            for hh in range(RET_HEADS):
                lg = -jnp.exp(jnp.full((1, 1), ld_ref[d, hh], F32))
                lg_lane = jnp.where(lane_head == hh, lg, lg_lane)
                dist = ((ri - ci) if d == 0 else (ci - ri)).astype(F32)
                dm_ref[d, hh] = jnp.where(dist >= 0, jnp.exp(dist * lg), 0.0)
            if d == 0:
                qd_ref[d] = jnp.exp((rowf + 1.0) * lg_lane)
                kd_ref[d] = jnp.exp((ch - 1.0 - rowf) * lg_lane)
            else:
                qd_ref[d] = jnp.exp((ch - rowf) * lg_lane)
                kd_ref[d] = jnp.exp(rowf * lg_lane)
            cd_ref[d] = jnp.exp(float(ch) * lg_lane)

    rblk = lax.broadcasted_iota(jnp.int32, (w, w), 0) // RET_DK
    cblk = lax.broadcasted_iota(jnp.int32, (w, w), 1) // RET_DK
    for d, (src, dst) in enumerate(((f_ref, of_ref), (b_ref, ob_ref))):
        q = src[:, 0:w]
        k = src[:, w:2 * w]
        v = src[:, 2 * w:3 * w]
        st = st_ref[d]
        o = jnp.dot((q.astype(F32) * qd_ref[d]).astype(BF16), st.astype(BF16), preferred_element_type=F32)
        for hh in range(RET_HEADS):
            in_head = lane_head == hh
            qm = jnp.where(in_head, q, jnp.zeros_like(q))
            s = lax.dot_general(qm, k, (((1,), (1,)), ((), ())), preferred_element_type=F32)
            intra = (s * dm_ref[d, hh]).astype(BF16)
            o = o + jnp.where(in_head, jnp.dot(intra, v, preferred_element_type=F32), 0.0)
        dst[...] = o
        kk_t = (k.astype(F32) * kd_ref[d]).T.astype(BF16)
        upd = jnp.dot(kk_t, v, preferred_element_type=F32)
        st_ref[d] = jnp.where(rblk == cblk, st * cd_ref[d] + upd, 0.0)


def _ret_call(log_decay, rqkv, *, batch, rows_per_batch, seq):
    ch = RET_CHUNK
    nc = rows_per_batch // ch
    n_lat = seq // ch
    n_ctx = nc - n_lat

    def fwd(b, c):
        return (b * nc + jnp.where(c < n_ctx, n_lat + c, c - n_ctx), 0)

    def bwd(b, c):
        return (b * nc + nc - 1 - c, 0)

    w = RET_WIDTH
    return pl.pallas_call(
        _ret_kernel,
        grid=(batch, nc),
        in_specs=[
            pl.BlockSpec(memory_space=pltpu.SMEM),
            pl.BlockSpec((ch, 3 * w), fwd),
            pl.BlockSpec((ch, 3 * w), bwd),
        ],
        out_specs=[pl.BlockSpec((ch, w), fwd), pl.BlockSpec((ch, w), bwd)],
        out_shape=[jax.ShapeDtypeStruct((rqkv.shape[0], w), F32)] * 2,
        scratch_shapes=[
            pltpu.VMEM((2, RET_HEADS, ch, ch), F32),
            pltpu.VMEM((2, ch, w), F32),
            pltpu.VMEM((2, ch, w), F32),
            pltpu.VMEM((2, 1, w), F32),
            pltpu.VMEM((2, w, w), F32),
        ],
        compiler_params=_cparams("arbitrary", "arbitrary"),
        name="retention",
    )(log_decay, rqkv, rqkv)


def _mixout_kernel(x_ref, da_ref, u_ref, up_ref, un_ref, of_ref, ob_ref, rg_ref, mod_ref, wo_ref, pw_ref,
                   ps_ref, lng_ref, lnb_ref, o_ref, *, tiles_per_batch, seq, alpha):
    d = D_MODEL
    t = x_ref.shape[0]
    i = pl.program_id(0)
    j = i % tiles_per_batch
    is_ctx = j == tiles_per_batch - 1
    stream_len = jnp.where(is_ctx, CTX_LEN, seq)
    p0 = jnp.where(is_ctx, 0, j * t)

    u = u_ref[...]
    prev = jnp.where(p0 > 0, up_ref[...], 0.0)
    nxt = jnp.where(p0 + t < stream_len, un_ref[...], 0.0)
    ext = jnp.concatenate([prev, u, nxt], axis=0)
    n = t + 2 * POOL_HALO
    a2 = ext + pltpu.roll(ext, 1, 0)
    a4 = pltpu.roll(a2, 1, 0) + pltpu.roll(a2, n - 1, 0)
    a8 = pltpu.roll(a4, 2, 0) + pltpu.roll(a4, n - 2, 0)
    a16 = pltpu.roll(a8, 4, 0) + pltpu.roll(a8, n - 4, 0)
    pos = p0 + lax.broadcasted_iota(jnp.int32, (t, POOL_WIDTH), 0)
    group = lax.broadcasted_iota(jnp.int32, (1, POOL_WIDTH), 1) // POOL_GROUP
    mean = jnp.zeros((t, POOL_WIDTH), F32)
    for gi, (wnd, asum) in enumerate(zip(POOL_WINDOWS, (a2, a4, a8, a16))):
        cnt = jnp.minimum(pos + wnd // 2, stream_len) - jnp.maximum(pos - wnd // 2, 0)
        mean = jnp.where(group == gi, asum[POOL_HALO:POOL_HALO + t] / cnt.astype(F32), mean)
    pool = jnp.dot((mean - u).astype(BF16), pw_ref[...], preferred_element_type=F32) * ps_ref[...]

    o = of_ref[...] + ob_ref[...]
    head = lax.broadcasted_iota(jnp.int32, (1, RET_WIDTH), 1) // RET_DK

    def head_mean(val):
        out = jnp.zeros_like(val)
        for hh in range(RET_HEADS):
            m = jnp.sum(jnp.where(head == hh, val, 0.0), axis=-1, keepdims=True) * (1.0 / RET_DK)
            out = jnp.where(head == hh, m, out)
        return out

    oc = o - head_mean(o)
    rn = oc * lax.rsqrt(head_mean(oc * oc) + LN_EPS)
    g = rg_ref[...]
    ret = rn * (g * _sigmoid(g))

    y = jnp.dot(da_ref[...], wo_ref[0:DA_WIDTH, :], preferred_element_type=F32)
    y = y + jnp.dot(pool.astype(BF16), wo_ref[DA_WIDTH:DA_WIDTH + POOL_WIDTH, :], preferred_element_type=F32)
    y = y + jnp.dot(ret.astype(BF16), wo_ref[DA_WIDTH + POOL_WIDTH:, :], preferred_element_type=F32)
    z = alpha * x_ref[...] + mod_ref[:, 2 * d:3 * d] * y
    o_ref[...] = _layer_norm_rows(z) * lng_ref[...] + lnb_ref[...]


def _mixout_call(x, da, u, o_f, o_b, rg, mod3, w_out_bf, pool_bd, pool_scale, ln_g, ln_b, *, tiles_per_batch, seq,
                 alpha):
    r, d = x.shape
    t = ROW_TILE
    nt = r // t
    hb = t // POOL_HALO
    n_halo_blocks = r // POOL_HALO
    row = lambda i: (i, 0)
    const = lambda i: (0, 0)
    kern = functools.partial(_mixout_kernel, tiles_per_batch=tiles_per_batch, seq=seq, alpha=alpha)
    return pl.pallas_call(
        kern,
        grid=(nt,),
        in_specs=[
            pl.BlockSpec((t, d), row),
            pl.BlockSpec((t, DA_WIDTH), row),
            pl.BlockSpec((t, POOL_WIDTH), row),
            pl.BlockSpec((POOL_HALO, POOL_WIDTH), lambda i: (jnp.maximum(i * hb - 1, 0), 0)),
            pl.BlockSpec((POOL_HALO, POOL_WIDTH), lambda i: (jnp.minimum((i + 1) * hb, n_halo_blocks - 1), 0)),
            pl.BlockSpec((t, RET_WIDTH), row),
            pl.BlockSpec((t, RET_WIDTH), row),
            pl.BlockSpec((t, RET_WIDTH), row),
            pl.BlockSpec((None, 1, 6 * d), lambda i: (_mod_row(i, tiles_per_batch), 0, 0)),
            pl.BlockSpec((d, d), const),
            pl.BlockSpec((POOL_WIDTH, POOL_WIDTH), const),
            pl.BlockSpec((1, POOL_WIDTH), const),
            pl.BlockSpec((1, d), const),
            pl.BlockSpec((1, d), const),
        ],
        out_specs=pl.BlockSpec((t, d), row),
        out_shape=jax.ShapeDtypeStruct((r, d), F32),
        compiler_params=_cparams("arbitrary"),
        name="mixer_out",
    )(x, da, u, u, u, o_f, o_b, rg, mod3, w_out_bf, pool_bd, pool_scale, ln_g, ln_b)


def _router_kernel(x_ref, mod_ref, wrh_ref, wrl_ref, bias_ref, wsgu_ref, wsdn_ref,
                   tokp_ref, idx_ref, gate_ref, rank_ref, cnt_ref, fsh_ref, carry_ref):
    d = D_MODEL
    t = x_ref.shape[0]
    ne = N_EXPERTS
    neg = -jnp.inf

    @pl.when(pl.program_id(0) == 0)
    def _():
        carry_ref[...] = jnp.zeros_like(carry_ref)

    tok = _layer_norm_rows(x_ref[...]) * (1.0 + mod_ref[:, 4 * d:5 * d]) + mod_ref[:, 3 * d:4 * d]
    tok_hi = tok.astype(BF16)
    tok_lo = (tok - tok_hi.astype(F32)).astype(BF16)

    tokp_ref[...] = _pack_bf16_pairs(tok)

    hs = jnp.dot(tok_hi, wsgu_ref[...], preferred_element_type=F32)
    gs, us = hs[:, 0:EXPERT_HIDDEN], hs[:, EXPERT_HIDDEN:]
    fsh_ref[...] = jnp.dot((gs * _sigmoid(gs) * us).astype(BF16), wsdn_ref[...], preferred_element_type=F32)

    nt_dims = (((1,), (1,)), ((), ()))
    logits = (lax.dot_general(wrh_ref[...], tok_hi, nt_dims, preferred_element_type=F32)
              + lax.dot_general(wrh_ref[...], tok_lo, nt_dims, preferred_element_type=F32)
              + lax.dot_general(wrl_ref[...], tok_hi, nt_dims, preferred_element_type=F32))
    scores = _sigmoid(logits)
    biased = scores + bias_ref[...]

    gidx = lax.broadcasted_iota(jnp.int32, (GROUP_SIZE, t), 0)
    blocks, gscores = [], []
    for g in range(N_GROUPS):
        blk = biased[g * GROUP_SIZE:(g + 1) * GROUP_SIZE, :]
        m1 = jnp.max(blk, axis=0, keepdims=True)
        first = jnp.min(jnp.where(blk == m1, gidx, GROUP_SIZE), axis=0, keepdims=True)
        m2 = jnp.max(jnp.where(gidx == first, neg, blk), axis=0, keepdims=True)
        blocks.append(blk)
        gscores.append(m1 + m2)

    keep = [jnp.zeros((1, t), F32) for _ in range(N_GROUPS)]
    for _ in range(TOPK_GROUPS):
        m = gscores[0]
        for gs_ in gscores[1:]:
            m = jnp.maximum(m, gs_)
        found = jnp.zeros((1, t), F32)
        for g in range(N_GROUPS):
            hit = jnp.where(gscores[g] == m, 1.0 - found, 0.0)
            found = found + hit
            keep[g] = keep[g] + hit
            gscores[g] = jnp.where(hit > 0.0, neg, gscores[g])
    masked = jnp.concatenate([jnp.where(keep[g] > 0.0, blocks[g], neg) for g in range(N_GROUPS)], axis=0)

    ei = lax.broadcasted_iota(jnp.int32, (ne, t), 0)
    cur = masked
    onehot = jnp.zeros((ne, t), F32)
    idxs, gates = [], []
    for _ in range(TOP_K):
        m = jnp.max(cur, axis=0, keepdims=True)
        ii = jnp.min(jnp.where(cur == m, ei, ne), axis=0, keepdims=True)
        sel = ei == ii
        idxs.append(ii)
        gates.append(jnp.sum(jnp.where(sel, scores, 0.0), axis=0, keepdims=True))
        onehot = jnp.where(sel, 1.0, onehot)
        cur = jnp.where(sel, neg, cur)
    gsum = gates[0]
    for gk in gates[1:]:
        gsum = gsum + gk
    for k in range(TOP_K):
        idx_ref[k:k + 1, :] = idxs[k]
        gate_ref[k:k + 1, :] = gates[k] / gsum * ROUTED_SCALE

    ti = lax.broadcasted_iota(jnp.int32, (t, t), 0)
    tj = lax.broadcasted_iota(jnp.int32, (t, t), 1)
    before = jnp.where(ti < tj, 1.0, 0.0).astype(BF16)
    prefix = jnp.dot(onehot.astype(BF16), before, preferred_element_type=F32) + carry_ref[:, 0:1]
    for k in range(TOP_K):
        rank_k = jnp.sum(jnp.where(ei == idxs[k], prefix, 0.0), axis=0, keepdims=True)
        rank_ref[k:k + 1, :] = rank_k.astype(jnp.int32)
    carry_ref[...] = carry_ref[...] + jnp.sum(onehot, axis=1, keepdims=True)
    cnt_ref[...] = carry_ref[...].astype(jnp.int32)


def _router_call(x, mod3, wr_hi, wr_lo, bias_col, ws_gu_bf, ws_dn_bf, *, tiles_per_batch):
    r, d = x.shape
    t = ROW_TILE
    nt = r // t
    row = lambda i: (i, 0)
    col = lambda i: (0, i)
    const = lambda i: (0, 0)
    return pl.pallas_call(
        _router_kernel,
        grid=(nt,),
        in_specs=[
            pl.BlockSpec((t, d), row),
            pl.BlockSpec((None, 1, 6 * d), lambda i: (_mod_row(i, tiles_per_batch), 0, 0)),
            pl.BlockSpec((N_EXPERTS, d), const),
            pl.BlockSpec((N_EXPERTS, d), const),
            pl.BlockSpec((N_EXPERTS, 1), const),
            pl.BlockSpec((d, 2 * EXPERT_HIDDEN), const),
            pl.BlockSpec((EXPERT_HIDDEN, d), const),
        ],
        out_specs=[
            pl.BlockSpec((t, PACK_W), row),
            pl.BlockSpec((TOP_K, t), col),
            pl.BlockSpec((TOP_K, t), col),
            pl.BlockSpec((TOP_K, t), col),
            pl.BlockSpec((N_EXPERTS, LANES), const),
            pl.BlockSpec((t, d), row),
        ],
        out_shape=[
            jax.ShapeDtypeStruct((r, PACK_W), jnp.int32),
            jax.ShapeDtypeStruct((TOP_K, r), jnp.int32),
            jax.ShapeDtypeStruct((TOP_K, r), F32),
            jax.ShapeDtypeStruct((TOP_K, r), jnp.int32),
            jax.ShapeDtypeStruct((N_EXPERTS, LANES), jnp.int32),
            jax.ShapeDtypeStruct((r, d), F32),
        ],
        scratch_shapes=[pltpu.VMEM((N_EXPERTS, LANES), F32)],
        compiler_params=_cparams("arbitrary"),
        name="router",
    )(x, mod3, wr_hi, wr_lo, bias_col, ws_gu_bf, ws_dn_bf)


def _dest_kernel(idx_ref, rank_ref, offs_ref, dest_ref):
    t = idx_ref.shape[1]
    ei = lax.broadcasted_iota(jnp.int32, (N_EXPERTS, t), 0)
    offs = offs_ref[...].astype(F32)
    for k in range(TOP_K):
        start = jnp.sum(jnp.where(ei == idx_ref[k:k + 1, :], offs, 0.0), axis=0, keepdims=True)
        dest_ref[k:k + 1, :] = start.astype(jnp.int32) + rank_ref[k:k + 1, :]


def _dest_call(idx, rank, offs_col):
    r = idx.shape[1]
    t = r // DEST_STEPS
    assert r % DEST_STEPS == 0 and t % LANES == 0
    col = lambda i: (0, i)
    return pl.pallas_call(
        _dest_kernel,
        grid=(r // t,),
        in_specs=[pl.BlockSpec((TOP_K, t), col), pl.BlockSpec((TOP_K, t), col),
                  pl.BlockSpec((N_EXPERTS, 1), lambda i: (0, 0))],
        out_specs=pl.BlockSpec((TOP_K, t), col),
        out_shape=jax.ShapeDtypeStruct((TOP_K, r), jnp.int32),
        compiler_params=_cparams("arbitrary"),
        name="moe_dest",
    )(idx, rank, offs_col)


def _sc_dispatch(tokp, dest_flat, pad_rows, n_sorted):
    r, width = tokp.shape
    win = SC_GATHER_WINDOW
    workers = SC_NUM_CORES * SC_NUM_SUBCORES
    token_windows = r // win
    n_pad_windows = pad_rows.shape[0] // win
    assert r % win == 0 and dest_flat.shape[0] == TOP_K * r and n_pad_windows % workers == 0
    windows_per_worker = -(-token_windows // workers)
    pads_per_worker = n_pad_windows // workers
    mesh = plsc.VectorSubcoreMesh(core_axis_name="core", subcore_axis_name="subcore", num_cores=SC_NUM_CORES,
                                  num_subcores=SC_NUM_SUBCORES)
    zero_rows = jnp.zeros((win, width), tokp.dtype)

    @functools.partial(
        pl.kernel, out_type=jax.ShapeDtypeStruct((n_sorted + SC_SPARE_ROWS, width), tokp.dtype), mesh=mesh,
        scratch_types=[pltpu.VMEM((win,), jnp.int32), pltpu.VMEM((win, width), tokp.dtype),
                       pltpu.SemaphoreType.DMA],
        name="moe_sc_dispatch")
    def dispatch_kernel(tok_hbm, dest_hbm, pad_hbm, zero_hbm, xs_hbm, idx_vmem, rows_vmem, sem):
        worker = lax.axis_index("subcore") * SC_NUM_CORES + lax.axis_index("core")

        @pl.loop(0, windows_per_worker)
        def _(j):
            window = j * workers + worker

            @pl.when(window < token_windows)
            def _():
                tok0 = window * win
                pltpu.sync_copy(tok_hbm.at[pl.ds(tok0, win)], rows_vmem)
                for k in range(TOP_K):
                    pltpu.sync_copy(dest_hbm.at[pl.ds(k * r + tok0, win)], idx_vmem)
                    pltpu.async_copy(rows_vmem, xs_hbm.at[idx_vmem], sem).wait()

        pltpu.sync_copy(zero_hbm, rows_vmem)

        @pl.loop(0, pads_per_worker)
        def _(j):
            off = (worker * pads_per_worker + j) * win
            pltpu.sync_copy(pad_hbm.at[pl.ds(off, win)], idx_vmem)
            pltpu.async_copy(rows_vmem, xs_hbm.at[idx_vmem], sem).wait()

    return dispatch_kernel(tokp, dest_flat, pad_rows, zero_rows)


def _expert_kernel(be_ref, nb_ref, ord_ref, ue_ref, nue_ref, xs_ref, wgu_hbm, wdn_hbm, ys_ref, wgu_f32, wdn_f32,
                   wgu_bf, wdn_bf, sems, *, layer):
    def weight_copies(o):
        slot = o % 2
        e = ue_ref[o]
        return (pltpu.make_async_copy(wgu_hbm.at[layer, e], wgu_f32.at[slot], sems.at[0, slot]),
                pltpu.make_async_copy(wdn_hbm.at[layer, e], wdn_f32.at[slot], sems.at[1, slot]))

    def start_weights(o):
        @pl.when(o < nue_ref[0])
        def _():
            for cp in weight_copies(o):
                cp.start()

    def one_block(sub, carry):
        j = pl.program_id(0) * EXPERT_BLOCKS_PER_STEP + sub

        @pl.when(j < nb_ref[0])
        def _():
            o = ord_ref[j]
            changed = jnp.logical_or(j == 0, be_ref[j] != be_ref[jnp.maximum(j - 1, 0)])

            @pl.when(j == 0)
            def _():
                start_weights(0)
                start_weights(1)

            @pl.when(changed)
            def _():
                for cp in weight_copies(o):
                    cp.wait()
                slot = o % 2
                wgu_bf[...] = wgu_f32[slot].astype(BF16)
                wdn_bf[...] = wdn_f32[slot].astype(BF16)
                start_weights(o + 2)

            rows = pl.ds(pl.multiple_of(sub * EXPERT_BLOCK, EXPERT_BLOCK), EXPERT_BLOCK)
            x_lo, x_hi = _unpack_bf16_pairs(xs_ref[rows, :])
            h = (jnp.dot(x_lo.astype(BF16), wgu_bf[0:PACK_W, :], preferred_element_type=F32)
                 + jnp.dot(x_hi.astype(BF16), wgu_bf[PACK_W:, :], preferred_element_type=F32))
            g, u = h[:, 0:EXPERT_HIDDEN], h[:, EXPERT_HIDDEN:]
            y = jnp.dot((g * _sigmoid(g) * u).astype(BF16), wdn_bf[...], preferred_element_type=F32)
            ys_ref[rows, :] = _pack_bf16_pairs(y)

        return carry

    lax.fori_loop(0, EXPERT_BLOCKS_PER_STEP, one_block, 0)


def _expert_call(block_expert, n_blocks_used, block_ordinal, used_expert, n_used_experts, xs, w_gu, w_dn, layer):
    n_rows = block_expert.shape[0] * EXPERT_BLOCK
    bm = EXPERT_BLOCK
    d = D_MODEL
    step_rows = bm * EXPERT_BLOCKS_PER_STEP
    assert n_rows % step_rows == 0
    used_step = lambda s, be, nb, od, ue, nue: (jnp.minimum(s, (nb[0] - 1) // EXPERT_BLOCKS_PER_STEP), 0)
    grid_spec = pltpu.PrefetchScalarGridSpec(
        num_scalar_prefetch=5,
        grid=(n_rows // step_rows,),
        in_specs=[
            pl.BlockSpec((step_rows, PACK_W), used_step),
            pl.BlockSpec(memory_space=pl.ANY),
            pl.BlockSpec(memory_space=pl.ANY),
        ],
        out_specs=pl.BlockSpec((step_rows, PACK_W), used_step),
        scratch_shapes=[
            pltpu.VMEM((2, d, 2 * EXPERT_HIDDEN), F32),
            pltpu.VMEM((2, EXPERT_HIDDEN, d), F32),
            pltpu.VMEM((d, 2 * EXPERT_HIDDEN), BF16),
            pltpu.VMEM((EXPERT_HIDDEN, d), BF16),
            pltpu.SemaphoreType.DMA((2, 2)),
        ],
    )
    return pl.pallas_call(
        functools.partial(_expert_kernel, layer=layer),
        grid_spec=grid_spec,
        out_shape=jax.ShapeDtypeStruct((n_rows, PACK_W), jnp.int32),
        compiler_params=_cparams("arbitrary"),
        name="moe_experts",
    )(block_expert, n_blocks_used, block_ordinal, used_expert, n_used_experts, xs, w_gu, w_dn)


def _sc_gather_rows(table, indices):
    n = indices.shape[0]
    width = table.shape[1]
    workers = SC_NUM_CORES * SC_NUM_SUBCORES
    assert n % (SC_GATHER_WINDOW * workers) == 0
    per_worker = n // workers
    mesh = plsc.VectorSubcoreMesh(core_axis_name="core", subcore_axis_name="subcore", num_cores=SC_NUM_CORES,
                                  num_subcores=SC_NUM_SUBCORES)

    @functools.partial(
        pl.kernel, out_type=jax.ShapeDtypeStruct((n, width), table.dtype), mesh=mesh,
        scratch_types=[pltpu.VMEM((SC_GATHER_WINDOW,), jnp.int32),
                       pltpu.VMEM((SC_GATHER_WINDOW, width), table.dtype),
                       pltpu.SemaphoreType.DMA],
        name="moe_sc_gather")
    def gather_kernel(table_hbm, idx_hbm, out_hbm, idx_vmem, rows_vmem, sem):
        worker = lax.axis_index("subcore") * SC_NUM_CORES + lax.axis_index("core")
        base = worker * per_worker

        @pl.loop(0, per_worker // SC_GATHER_WINDOW)
        def _(w):
            off = base + w * SC_GATHER_WINDOW
            pltpu.sync_copy(idx_hbm.at[pl.ds(off, SC_GATHER_WINDOW)], idx_vmem)
            pltpu.async_copy(table_hbm.at[idx_vmem], rows_vmem, sem).wait()
            pltpu.sync_copy(rows_vmem, out_hbm.at[pl.ds(off, SC_GATHER_WINDOW)])

    return gather_kernel(table, indices)


def _combine_kernel(*refs, alpha):
    y_refs = refs[:TOP_K]
    x_ref, fsh_ref, gate_ref, mod_ref, lng_ref, lnb_ref, o_ref = refs[TOP_K:]
    d = D_MODEL
    t = x_ref.shape[0]
    gate_rows = gate_ref[...]
    pad = jnp.zeros((LANES - TOP_K, t), F32)
    gate_cols = jnp.concatenate([gate_rows, pad], axis=0).T
    f_lo = fsh_ref[:, 0:PACK_W]
    f_hi = fsh_ref[:, PACK_W:]
    for k in range(TOP_K):
        y_lo, y_hi = _unpack_bf16_pairs(y_refs[k][...])
        f_lo = f_lo + gate_cols[:, k:k + 1] * y_lo
        f_hi = f_hi + gate_cols[:, k:k + 1] * y_hi
    f = jnp.concatenate([f_lo, f_hi], axis=1)
    z = alpha * x_ref[...] + mod_ref[:, 5 * d:6 * d] * f
    o_ref[...] = _layer_norm_rows(z) * lng_ref[...] + lnb_ref[...]


def _combine_call(y_tok, x, fsh, gate, mod3, ln_g, ln_b, *, tiles_per_batch, alpha, drop_context):
    r, d = x.shape
    t = ROW_TILE
    nt = r // t
    if drop_context:
        per_batch = tiles_per_batch - 1
        src = lambda i: (i // per_batch) * tiles_per_batch + i % per_batch
        n_tiles = nt // tiles_per_batch * per_batch
    else:
        src = lambda i: i
        n_tiles = nt
    row = lambda i: (src(i), 0)
    col = lambda i: (0, src(i))
    const = lambda i: (0, 0)
    kern = functools.partial(_combine_kernel, alpha=alpha)
    y_specs = [pl.BlockSpec((t, PACK_W), functools.partial(lambda k, i: (k * nt + src(i), 0), k))
               for k in range(TOP_K)]
    return pl.pallas_call(
        kern,
        grid=(n_tiles,),
        in_specs=y_specs + [
            pl.BlockSpec((t, d), row),
            pl.BlockSpec((t, d), row),
            pl.BlockSpec((TOP_K, t), col),
            pl.BlockSpec((None, 1, 6 * d), lambda i: (_mod_row(src(i), tiles_per_batch), 0, 0)),
            pl.BlockSpec((1, d), const),
            pl.BlockSpec((1, d), const),
        ],
        out_specs=pl.BlockSpec((t, d), lambda i: (i, 0)),
        out_shape=jax.ShapeDtypeStruct((n_tiles * t, d), F32),
        compiler_params=_cparams("arbitrary"),
        name="moe_combine",
    )(*([y_tok] * TOP_K), x, fsh, gate, mod3, ln_g, ln_b)


def _rope_tables(seq):
    rows = seq // GRID_W
    row = jnp.repeat(jnp.arange(rows, dtype=F32), GRID_W)
    col = jnp.tile(jnp.arange(GRID_W, dtype=F32), rows)
    nf = DA_DIM // 4
    freqs = ROPE_BASE ** (-jnp.arange(nf, dtype=F32) / nf)
    cr, sr = jnp.cos(row[:, None] * freqs), jnp.sin(row[:, None] * freqs)
    cc, sc = jnp.cos(col[:, None] * freqs), jnp.sin(col[:, None] * freqs)
    c64 = jnp.concatenate([cr, cr, cc, cc], axis=1)
    s64 = jnp.concatenate([-sr, sr, -sc, sc], axis=1)
    c = jnp.concatenate([jnp.tile(c64, (1, 2)), jnp.ones((CTX_LEN, LANES), F32)], axis=0)
    s = jnp.concatenate([jnp.tile(s64, (1, 2)), jnp.zeros((CTX_LEN, LANES), F32)], axis=0)
    return c, s


def kernel(x, c, ctx, c_ctx, w_mod, b_mod, w_in, w_out, diff_lambda, pool_w, pool_scale, ret_log_decay, ln_g, ln_b,
           w_router, router_bias, w_expert_gate_up, w_expert_down, w_shared_gate_up, w_shared_down):
    batch, seq, d = x.shape
    depth = w_mod.shape[0]
    assert d == D_MODEL and ctx.shape[1] == CTX_LEN == ROW_TILE and batch == 2
    assert seq % ROW_TILE == 0 and seq % GRID_W == 0 and w_in.shape[-1] == IN_WIDTH
    rows_per_batch = seq + CTX_LEN
    tiles_per_batch = rows_per_batch // ROW_TILE
    r = batch * rows_per_batch
    alpha = (2.0 * depth) ** 0.25

    xa = jnp.concatenate([x, ctx], axis=1).reshape(r, d)
    cvec = jnp.zeros((8, d), F32).at[0:batch].set(c).at[batch].set(c_ctx)
    mod_all = _mod_call(cvec, w_mod, b_mod)
    rope_c, rope_s = _rope_tables(seq)

    n_sorted = r * TOP_K + N_EXPERTS * EXPERT_BLOCK
    n_blocks = n_sorted // EXPERT_BLOCK

    for l in range(depth):
        lambda_init = 0.8 - 0.6 * math.exp(-0.3 * l)
        mod3 = mod_all[l].reshape(8, 1, 6 * d)
        lng = ln_g[l].reshape(2, 1, d)
        lnb = ln_b[l].reshape(2, 1, d)

        w_in_bf = w_in[l].astype(BF16)
        w_vt_bf = w_in_bf[:, QK_WIDTH:QK_WIDTH + DA_WIDTH].T
        qk, vda, u, rqkv, rg = _inproj_call(xa, mod3, w_in_bf, w_vt_bf, rope_c, rope_s, tiles_per_batch)
        da = _attn_call(diff_lambda[l], qk, vda, batch=batch, rows_per_batch=rows_per_batch, seq=seq,
                        lambda_init=lambda_init)
        o_f, o_b = _ret_call(ret_log_decay[l], rqkv, batch=batch, rows_per_batch=rows_per_batch, seq=seq)
        pool_bd = jnp.zeros((POOL_WIDTH, POOL_WIDTH), F32)
        for gi in range(len(POOL_WINDOWS)):
            sl = slice(gi * POOL_GROUP, (gi + 1) * POOL_GROUP)
            pool_bd = pool_bd.at[sl, sl].set(pool_w[l, gi])
        xa = _mixout_call(xa, da, u, o_f, o_b, rg, mod3, w_out[l].astype(BF16), pool_bd.astype(BF16),
                          pool_scale[l].reshape(1, POOL_WIDTH), lng[0], lnb[0],
                          tiles_per_batch=tiles_per_batch, seq=seq, alpha=alpha)

        wr_t = w_router[l].T
        wr_hi = wr_t.astype(BF16)
        wr_lo = (wr_t - wr_hi.astype(F32)).astype(BF16)
        tokp, idx, gate, rank, cnt, fsh = _router_call(
            xa, mod3, wr_hi, wr_lo, router_bias[l].reshape(N_EXPERTS, 1),
            w_shared_gate_up[l].astype(BF16), w_shared_down[l].astype(BF16), tiles_per_batch=tiles_per_batch)
        counts = cnt[:, 0]
        padded = (counts + EXPERT_BLOCK - 1) // EXPERT_BLOCK * EXPERT_BLOCK
        pad_end = jnp.cumsum(padded)
        offs = pad_end - padded
        expert_ids = jnp.arange(N_EXPERTS, dtype=jnp.int32)
        blk_row = jnp.arange(n_blocks, dtype=jnp.int32) * EXPERT_BLOCK
        block_expert = jnp.minimum(jnp.sum(pad_end[None, :] <= blk_row[:, None], axis=1), N_EXPERTS - 1)
        n_used = pad_end[-1:] // EXPERT_BLOCK
        used = counts > 0
        ordinal = jnp.cumsum(used) - 1
        hit = used[None, :] & (ordinal[None, :] == expert_ids[:, None])
        used_expert = jnp.sum(jnp.where(hit, expert_ids[None, :], 0), axis=1)
        n_used_experts = jnp.sum(used)[None]
        block_ordinal = ordinal[block_expert]
        slot = jnp.arange(EXPERT_BLOCK, dtype=jnp.int32)[None, :]
        first_pad = (padded - EXPERT_BLOCK)[:, None] + slot
        is_pad = (first_pad >= counts[:, None]) & (padded[:, None] > 0)
        spare = n_sorted + jnp.arange(N_EXPERTS * EXPERT_BLOCK, dtype=jnp.int32).reshape(N_EXPERTS, EXPERT_BLOCK)
        pad_rows = jnp.where(is_pad, offs[:, None] + first_pad, spare).reshape(N_EXPERTS * EXPERT_BLOCK)
        i32 = lambda a: a.astype(jnp.int32)

        dest = _dest_call(idx, rank, i32(offs).reshape(N_EXPERTS, 1))
        dest_flat = dest.reshape(TOP_K * r)
        xs = _sc_dispatch(tokp, dest_flat, i32(pad_rows), n_sorted)
        ys = _expert_call(i32(block_expert), i32(n_used), i32(block_ordinal), i32(used_expert),
                          i32(n_used_experts), xs, w_expert_gate_up, w_expert_down, l)
        y_tok = _sc_gather_rows(ys, dest_flat)
        xa = _combine_call(y_tok, xa, fsh, gate, mod3, lng[1], lnb[1], tiles_per_batch=tiles_per_batch,
                           alpha=alpha, drop_context=(l == depth - 1))

    return xa.reshape(batch, seq, d)
```

```python
import functools
import math

import jax
import jax.numpy as jnp
from jax import lax
from jax.experimental import pallas as pl
from jax.experimental.pallas import tpu as pltpu
from jax.experimental.pallas import tpu_sc as plsc

F32 = jnp.float32
BF16 = jnp.bfloat16
HIGHEST = lax.Precision.HIGHEST

D_MODEL = 1024
CTX_LEN = 256
GRID_W = 64
DA_HEADS = 4
DA_DIM = 64
DA_VDIM = 2 * DA_DIM
DA_WIDTH = DA_HEADS * DA_VDIM
ROPE_BASE = 10000.0
POOL_WINDOWS = (2, 4, 8, 16)
POOL_GROUP = 64
POOL_WIDTH = len(POOL_WINDOWS) * POOL_GROUP
POOL_HALO = 8
RET_HEADS = 4
RET_DK = 64
RET_WIDTH = RET_HEADS * RET_DK
RET_CHUNK = 128
QK_WIDTH = 2 * DA_HEADS * 2 * DA_DIM
IN_WIDTH = QK_WIDTH + DA_WIDTH + POOL_WIDTH + 4 * RET_WIDTH
N_EXPERTS = 256
TOP_K = 8
N_GROUPS = 8
GROUP_SIZE = N_EXPERTS // N_GROUPS
TOPK_GROUPS = 4
EXPERT_HIDDEN = 256
ROUTED_SCALE = 2.5
LN_EPS = 1e-6
RMS_EPS = 1e-5

LANES = 128
ROW_TILE = 256
DEST_STEPS = 4
ATTN_Q_TILE = 256
ATTN_K_CHUNK = 256
ATTN_UNROLL = 16
SC_NUM_CORES = 2
SC_NUM_SUBCORES = 16
SC_GATHER_WINDOW = 128
EXPERT_BLOCK = 256
EXPERT_BLOCKS_PER_STEP = 4
SC_SPARE_ROWS = N_EXPERTS * EXPERT_BLOCK
PACK_W = D_MODEL // 2
VMEM_LIMIT = 56 * 1024 * 1024


def _cparams(*sem):
    return pltpu.CompilerParams(dimension_semantics=sem, vmem_limit_bytes=VMEM_LIMIT)


def _sigmoid(x):
    return 1.0 / (1.0 + jnp.exp(-x))


def _layer_norm_rows(x):
    mu = jnp.mean(x, axis=-1, keepdims=True)
    xc = x - mu
    var = jnp.mean(xc * xc, axis=-1, keepdims=True)
    return xc * lax.rsqrt(var + LN_EPS)


def _pack_bf16_pairs(x):
    half = x.shape[1] // 2
    bits = pltpu.bitcast(x.astype(BF16).astype(F32), jnp.uint32)
    word = lax.shift_right_logical(bits[:, 0:half], jnp.uint32(16)) | (bits[:, half:] & jnp.uint32(0xFFFF0000))
    return pltpu.bitcast(word, jnp.int32)


def _unpack_bf16_pairs(packed):
    word = pltpu.bitcast(packed, jnp.uint32)
    lo = pltpu.bitcast(lax.shift_left(word, jnp.uint32(16)), F32)
    hi = pltpu.bitcast(word & jnp.uint32(0xFFFF0000), F32)
    return lo, hi


def _mod_row(i, tiles_per_batch):
    return jnp.where(i % tiles_per_batch == tiles_per_batch - 1, 2, i // tiles_per_batch)


def _mod_kernel(c_ref, w_ref, b_ref, o_ref):
    c = c_ref[...]
    s = c * _sigmoid(c)
    o_ref[...] = jnp.dot(s, w_ref[...], precision=HIGHEST, preferred_element_type=F32) + b_ref[...]


def _mod_call(cvec, w_mod, b_mod):
    depth, d, n = w_mod.shape
    tn = 1536
    return pl.pallas_call(
        _mod_kernel,
        grid=(depth, n // tn),
        in_specs=[
            pl.BlockSpec((8, d), lambda l, j: (0, 0)),
            pl.BlockSpec((None, d, tn), lambda l, j: (l, 0, j)),
            pl.BlockSpec((None, 1, tn), lambda l, j: (l, 0, j)),
        ],
        out_specs=pl.BlockSpec((None, 8, tn), lambda l, j: (l, 0, j)),
        out_shape=jax.ShapeDtypeStruct((depth, 8, n), F32),
        compiler_params=_cparams("arbitrary", "arbitrary"),
        name="mod",
    )(cvec, w_mod, b_mod.reshape(depth, 1, n))


def _inproj_kernel(x_ref, mod_ref, w_ref, wvt_ref, ct_ref, st_ref, qk_ref, vt_ref, u_ref, r_ref, g_ref):
    d = D_MODEL
    xn = _layer_norm_rows(x_ref[...])
    h = (xn * (1.0 + mod_ref[:, d:2 * d]) + mod_ref[:, 0:d]).astype(BF16)

    a = jnp.dot(h, w_ref[:, 0:QK_WIDTH], preferred_element_type=F32)
    lane = lax.broadcasted_iota(jnp.int32, (a.shape[0], LANES), 1)
    first_half = (lane % 32) < 16
    ct = ct_ref[...]
    st = st_ref[...]
    for s in range(QK_WIDTH // LANES):
        blk = a[:, s * LANES:(s + 1) * LANES]
        partner = jnp.where(first_half, pltpu.roll(blk, LANES - 16, 1), pltpu.roll(blk, 16, 1))
        rot = blk * ct + partner * st
        if s < QK_WIDTH // LANES // 2:
            rot = rot * (DA_DIM ** -0.5 * math.log2(math.e))
        qk_ref[:, s * LANES:(s + 1) * LANES] = rot.astype(BF16)

    vt_ref[...] = lax.dot_general(wvt_ref[...], h, (((1,), (1,)), ((), ())),
                                  preferred_element_type=F32).astype(BF16)
    o = QK_WIDTH + DA_WIDTH
    u_ref[...] = jnp.dot(h, w_ref[:, o:o + POOL_WIDTH], preferred_element_type=F32)
    o += POOL_WIDTH
    r = jnp.dot(h, w_ref[:, o:o + 3 * RET_WIDTH], preferred_element_type=F32)
    r_ref[:, 0:RET_WIDTH] = r[:, 0:RET_WIDTH].astype(BF16)
    r_ref[:, RET_WIDTH:2 * RET_WIDTH] = (r[:, RET_WIDTH:2 * RET_WIDTH] * (RET_DK ** -0.5)).astype(BF16)
    r_ref[:, 2 * RET_WIDTH:] = r[:, 2 * RET_WIDTH:].astype(BF16)
    o += 3 * RET_WIDTH
    g_ref[...] = jnp.dot(h, w_ref[:, o:o + RET_WIDTH], preferred_element_type=F32)


def _inproj_call(x, mod3, w_in_bf, w_vt_bf, rope_c, rope_s, tiles_per_batch):
    r, d = x.shape
    t = ROW_TILE
    nt = r // t
    row = lambda i: (i, 0)
    return pl.pallas_call(
        _inproj_kernel,
        grid=(nt,),
        in_specs=[
            pl.BlockSpec((t, d), row),
            pl.BlockSpec((None, 1, 6 * d), lambda i: (_mod_row(i, tiles_per_batch), 0, 0)),
            pl.BlockSpec((d, IN_WIDTH), lambda i: (0, 0)),
            pl.BlockSpec((DA_WIDTH, d), lambda i: (0, 0)),
            pl.BlockSpec((t, LANES), lambda i: (i % tiles_per_batch, 0)),
            pl.BlockSpec((t, LANES), lambda i: (i % tiles_per_batch, 0)),
        ],
        out_specs=[
            pl.BlockSpec((t, QK_WIDTH), row),
            pl.BlockSpec((DA_WIDTH, t), lambda i: (0, i)),
            pl.BlockSpec((t, POOL_WIDTH), row),
            pl.BlockSpec((t, 3 * RET_WIDTH), row),
            pl.BlockSpec((t, RET_WIDTH), row),
        ],
        out_shape=[
            jax.ShapeDtypeStruct((r, QK_WIDTH), BF16),
            jax.ShapeDtypeStruct((DA_WIDTH, r), BF16),
            jax.ShapeDtypeStruct((r, POOL_WIDTH), F32),
            jax.ShapeDtypeStruct((r, 3 * RET_WIDTH), BF16),
            jax.ShapeDtypeStruct((r, RET_WIDTH), F32),
        ],
        compiler_params=_cparams("arbitrary"),
        name="inproj",
    )(x, mod3, w_in_bf, w_vt_bf, rope_c, rope_s)


def _attn_kernel(lam_ref, q_ref, k_ref, vt_ref, o_ref, *s_refs, k_chunk, seq, lambda_init):
    mq = ATTN_Q_TILE
    n_tiles = (seq + CTX_LEN) // mq
    n_chunks = (seq + CTX_LEN) // k_chunk
    last = n_chunks - 1
    n_iters = last // ATTN_UNROLL
    neg_inf = jnp.full((1, 2 * mq), -jnp.inf, F32)
    acc_zero = jnp.zeros((DA_VDIM + 16, 2 * mq), F32)
    ones_rows = jnp.where(lax.broadcasted_iota(jnp.int32, (16, k_chunk), 0) == 0, 1.0, 0.0).astype(BF16)

    def tile_rows(i):
        return pl.ds(pl.multiple_of(i * mq, mq), mq)

    def q_transposed(i):
        q = q_ref[tile_rows(i), :]
        lane = lax.broadcasted_iota(jnp.int32, q.shape, 1)
        zero = jnp.zeros_like(q)
        q2 = jnp.concatenate([jnp.where(lane < DA_DIM, q, zero), jnp.where(lane >= DA_DIM, q, zero)], axis=0)
        return q2.astype(F32).T.astype(BF16)

    def score_chunk(s_ref, c, qt, m):
        off = pl.multiple_of(c * k_chunk, k_chunk)
        s = jnp.dot(k_ref[pl.ds(off, k_chunk), :], qt, preferred_element_type=F32)
        s_ref[c] = s
        return jnp.maximum(m, jnp.max(s, axis=0, keepdims=True))

    def value_chunk(s_ref, c, m, acc):
        off = pl.multiple_of(c * k_chunk, k_chunk)
        vt = jnp.concatenate([vt_ref[:, pl.ds(off, k_chunk)], ones_rows], axis=0)
        p = jnp.exp2((s_ref[c] - m).astype(BF16))
        return acc + jnp.dot(vt, p, preferred_element_type=F32)

    def finish(i, acc):
        l0, l1 = acc[DA_VDIM:DA_VDIM + 1, 0:mq], acc[DA_VDIM:DA_VDIM + 1, mq:]
        a0, a1 = acc[0:DA_VDIM, 0:mq], acc[0:DA_VDIM, mq:]
        lv = lam_ref[...]
        lam = (jnp.exp(jnp.sum(lv[0:1] * lv[1:2], axis=-1, keepdims=True))
               - jnp.exp(jnp.sum(lv[2:3] * lv[3:4], axis=-1, keepdims=True)) + lambda_init)
        o = a0 / l0 - lam * (a1 / l1)
        o = o * lax.rsqrt(jnp.mean(o * o, axis=0, keepdims=True) + RMS_EPS) * (1.0 - lambda_init)
        o_ref[tile_rows(i), :] = o.T.astype(BF16)

    def scores_only(s_ref, qt):
        def body(it, m):
            for u in range(ATTN_UNROLL):
                m = score_chunk(s_ref, it * ATTN_UNROLL + u, qt, m)
            return m
        return score_chunk(s_ref, last, qt, lax.fori_loop(0, n_iters, body, neg_inf))

    def values_only(s_ref, m):
        def body(it, acc):
            for u in range(ATTN_UNROLL):
                acc = value_chunk(s_ref, it * ATTN_UNROLL + u, m, acc)
            return acc
        return value_chunk(s_ref, last, m, lax.fori_loop(0, n_iters, body, acc_zero))

    def fused_tile(i, m_prev, s_cur, s_prev):
        qt = q_transposed(i)

        def body(it, carry):
            m, acc = carry
            for u in range(ATTN_UNROLL):
                c = it * ATTN_UNROLL + u
                m = score_chunk(s_cur, c, qt, m)
                acc = value_chunk(s_prev, c, m_prev, acc)
            return m, acc

        m, acc = lax.fori_loop(0, n_iters, body, (neg_inf, acc_zero))
        m = score_chunk(s_cur, last, qt, m)
        finish(i - 1, value_chunk(s_prev, last, m_prev, acc))
        return m

    s_even, s_odd = s_refs
    ctx_tile = n_tiles - 1
    assert ctx_tile % 2 == 0 and ctx_tile >= 2
    m = scores_only(s_even, q_transposed(0))

    def tile_pair(p, m):
        m = fused_tile(2 * p + 1, m, s_odd, s_even)
        return fused_tile(2 * p + 2, m, s_even, s_odd)

    m = lax.fori_loop(0, (ctx_tile - 2) // 2, tile_pair, m)
    m = fused_tile(ctx_tile - 1, m, s_odd, s_even)
    m_ctx = score_chunk(s_even, last, q_transposed(ctx_tile), neg_inf)
    finish(ctx_tile - 1, values_only(s_odd, m))
    finish(ctx_tile, value_chunk(s_even, last, m_ctx, acc_zero))


def _attn_call(lam_vec, qk, vda, *, batch, rows_per_batch, seq, lambda_init):
    tq = ATTN_Q_TILE
    assert seq % (ATTN_K_CHUNK * ATTN_UNROLL) == 0 and rows_per_batch - seq == CTX_LEN == tq == ATTN_K_CHUNK
    nq = rows_per_batch // tq
    kern = functools.partial(_attn_kernel, k_chunk=ATTN_K_CHUNK, seq=seq, lambda_init=lambda_init)
    return pl.pallas_call(
        kern,
        grid=(batch, DA_HEADS),
        in_specs=[
            pl.BlockSpec((4, DA_DIM), lambda b, h: (0, 0)),
            pl.BlockSpec((rows_per_batch, DA_VDIM), lambda b, h: (b, h)),
            pl.BlockSpec((rows_per_batch, DA_VDIM), lambda b, h: (b, DA_HEADS + h)),
            pl.BlockSpec((DA_VDIM, rows_per_batch), lambda b, h: (h, b)),
        ],
        out_specs=pl.BlockSpec((rows_per_batch, DA_VDIM), lambda b, h: (b, h)),
        out_shape=jax.ShapeDtypeStruct((qk.shape[0], DA_WIDTH), BF16),
        scratch_shapes=[pltpu.VMEM((rows_per_batch // ATTN_K_CHUNK, ATTN_K_CHUNK, 2 * tq), F32)] * 2,
        compiler_params=_cparams("arbitrary", "arbitrary"),
        name="diff_attn",
    )(lam_vec, qk, qk, vda)


def _ret_kernel(ld_ref, f_ref, b_ref, of_ref, ob_ref, dm_ref, qd_ref, kd_ref, cd_ref, st_ref):
    c = pl.program_id(1)
    ch = RET_CHUNK
    w = RET_WIDTH
    lane_head = lax.broadcasted_iota(jnp.int32, (1, w), 1) // RET_DK

    @pl.when(c == 0)
    def _():
        st_ref[...] = jnp.zeros_like(st_ref)
        ri = lax.broadcasted_iota(jnp.int32, (ch, ch), 0)
        ci = lax.broadcasted_iota(jnp.int32, (ch, ch), 1)
        rowf = lax.broadcasted_iota(jnp.int32, (ch, w), 0).astype(F32)
        for d in range(2):
            lg_lane = jnp.zeros((1, w), F32)
            for hh in range(RET_HEADS):
                lg = -jnp.exp(jnp.full((1, 1), ld_ref[d, hh], F32))
                lg_lane = jnp.where(lane_head == hh, lg, lg_lane)
                dist = ((ri - ci) if d == 0 else (ci - ri)).astype(F32)
                dm_ref[d, hh] = jnp.where(dist >= 0, jnp.exp(dist * lg), 0.0)
            if d == 0:
                qd_ref[d] = jnp.exp((rowf + 1.0) * lg_lane)
                kd_ref[d] = jnp.exp((ch - 1.0 - rowf) * lg_lane)
            else:
                qd_ref[d] = jnp.exp((ch - rowf) * lg_lane)
                kd_ref[d] = jnp.exp(rowf * lg_lane)
            cd_ref[d] = jnp.exp(float(ch) * lg_lane)

    rblk = lax.broadcasted_iota(jnp.int32, (w, w), 0) // RET_DK
    cblk = lax.broadcasted_iota(jnp.int32, (w, w), 1) // RET_DK
    for d, (src, dst) in enumerate(((f_ref, of_ref), (b_ref, ob_ref))):
        q = src[:, 0:w]
        k = src[:, w:2 * w]
        v = src[:, 2 * w:3 * w]
        st = st_ref[d]
        o = jnp.dot((q.astype(F32) * qd_ref[d]).astype(BF16), st.astype(BF16), preferred_element_type=F32)
        for hh in range(RET_HEADS):
            in_head = lane_head == hh
            qm = jnp.where(in_head, q, jnp.zeros_like(q))
            s = lax.dot_general(qm, k, (((1,), (1,)), ((), ())), preferred_element_type=F32)
            intra = (s * dm_ref[d, hh]).astype(BF16)
            o = o + jnp.where(in_head, jnp.dot(intra, v, preferred_element_type=F32), 0.0)
        dst[...] = o
        kk_t = (k.astype(F32) * kd_ref[d]).T.astype(BF16)
        upd = jnp.dot(kk_t, v, preferred_element_type=F32)
        st_ref[d] = jnp.where(rblk == cblk, st * cd_ref[d] + upd, 0.0)


def _ret_call(log_decay, rqkv, *, batch, rows_per_batch, seq):
    ch = RET_CHUNK
    nc = rows_per_batch // ch
    n_lat = seq // ch
    n_ctx = nc - n_lat

    def fwd(b, c):
        return (b * nc + jnp.where(c < n_ctx, n_lat + c, c - n_ctx), 0)

    def bwd(b, c):
        return (b * nc + nc - 1 - c, 0)

    w = RET_WIDTH
    return pl.pallas_call(
        _ret_kernel,
        grid=(batch, nc),
        in_specs=[
            pl.BlockSpec(memory_space=pltpu.SMEM),
            pl.BlockSpec((ch, 3 * w), fwd),
            pl.BlockSpec((ch, 3 * w), bwd),
        ],
        out_specs=[pl.BlockSpec((ch, w), fwd), pl.BlockSpec((ch, w), bwd)],
        out_shape=[jax.ShapeDtypeStruct((rqkv.shape[0], w), F32)] * 2,
        scratch_shapes=[
            pltpu.VMEM((2, RET_HEADS, ch, ch), F32),
            pltpu.VMEM((2, ch, w), F32),
            pltpu.VMEM((2, ch, w), F32),
            pltpu.VMEM((2, 1, w), F32),
            pltpu.VMEM((2, w, w), F32),
        ],
        compiler_params=_cparams("arbitrary", "arbitrary"),
        name="retention",
    )(log_decay, rqkv, rqkv)


def _mixout_kernel(x_ref, da_ref, u_ref, up_ref, un_ref, of_ref, ob_ref, rg_ref, mod_ref, wo_ref, pw_ref,
                   ps_ref, lng_ref, lnb_ref, o_ref, *, tiles_per_batch, seq, alpha):
    d = D_MODEL
    t = x_ref.shape[0]
    i = pl.program_id(0)
    j = i % tiles_per_batch
    is_ctx = j == tiles_per_batch - 1
    stream_len = jnp.where(is_ctx, CTX_LEN, seq)
    p0 = jnp.where(is_ctx, 0, j * t)

    u = u_ref[...]
    prev = jnp.where(p0 > 0, up_ref[...], 0.0)
    nxt = jnp.where(p0 + t < stream_len, un_ref[...], 0.0)
    ext = jnp.concatenate([prev, u, nxt], axis=0)
    n = t + 2 * POOL_HALO
    a2 = ext + pltpu.roll(ext, 1, 0)
    a4 = pltpu.roll(a2, 1, 0) + pltpu.roll(a2, n - 1, 0)
    a8 = pltpu.roll(a4, 2, 0) + pltpu.roll(a4, n - 2, 0)
    a16 = pltpu.roll(a8, 4, 0) + pltpu.roll(a8, n - 4, 0)
    pos = p0 + lax.broadcasted_iota(jnp.int32, (t, POOL_WIDTH), 0)
    group = lax.broadcasted_iota(jnp.int32, (1, POOL_WIDTH), 1) // POOL_GROUP
    mean = jnp.zeros((t, POOL_WIDTH), F32)
    for gi, (wnd, asum) in enumerate(zip(POOL_WINDOWS, (a2, a4, a8, a16))):
        cnt = jnp.minimum(pos + wnd // 2, stream_len) - jnp.maximum(pos - wnd // 2, 0)
        mean = jnp.where(group == gi, asum[POOL_HALO:POOL_HALO + t] / cnt.astype(F32), mean)
    pool = jnp.dot((mean - u).astype(BF16), pw_ref[...], preferred_element_type=F32) * ps_ref[...]

    o = of_ref[...] + ob_ref[...]
    head = lax.broadcasted_iota(jnp.int32, (1, RET_WIDTH), 1) // RET_DK

    def head_mean(val):
        out = jnp.zeros_like(val)
        for hh in range(RET_HEADS):
            m = jnp.sum(jnp.where(head == hh, val, 0.0), axis=-1, keepdims=True) * (1.0 / RET_DK)
            out = jnp.where(head == hh, m, out)
        return out

    oc = o - head_mean(o)
    rn = oc * lax.rsqrt(head_mean(oc * oc) + LN_EPS)
    g = rg_ref[...]
    ret = rn * (g * _sigmoid(g))

    y = jnp.dot(da_ref[...], wo_ref[0:DA_WIDTH, :], preferred_element_type=F32)
    y = y + jnp.dot(pool.astype(BF16), wo_ref[DA_WIDTH:DA_WIDTH + POOL_WIDTH, :], preferred_element_type=F32)
    y = y + jnp.dot(ret.astype(BF16), wo_ref[DA_WIDTH + POOL_WIDTH:, :], preferred_element_type=F32)
    z = alpha * x_ref[...] + mod_ref[:, 2 * d:3 * d] * y
    o_ref[...] = _layer_norm_rows(z) * lng_ref[...] + lnb_ref[...]


def _mixout_call(x, da, u, o_f, o_b, rg, mod3, w_out_bf, pool_bd, pool_scale, ln_g, ln_b, *, tiles_per_batch, seq,
                 alpha):
    r, d = x.shape
    t = ROW_TILE
    nt = r // t
    hb = t // POOL_HALO
    n_halo_blocks = r // POOL_HALO
    row = lambda i: (i, 0)
    const = lambda i: (0, 0)
    kern = functools.partial(_mixout_kernel, tiles_per_batch=tiles_per_batch, seq=seq, alpha=alpha)
    return pl.pallas_call(
        kern,
        grid=(nt,),
        in_specs=[
            pl.BlockSpec((t, d), row),
            pl.BlockSpec((t, DA_WIDTH), row),
            pl.BlockSpec((t, POOL_WIDTH), row),
            pl.BlockSpec((POOL_HALO, POOL_WIDTH), lambda i: (jnp.maximum(i * hb - 1, 0), 0)),
            pl.BlockSpec((POOL_HALO, POOL_WIDTH), lambda i: (jnp.minimum((i + 1) * hb, n_halo_blocks - 1), 0)),
            pl.BlockSpec((t, RET_WIDTH), row),
            pl.BlockSpec((t, RET_WIDTH), row),
            pl.BlockSpec((t, RET_WIDTH), row),
            pl.BlockSpec((None, 1, 6 * d), lambda i: (_mod_row(i, tiles_per_batch), 0, 0)),
            pl.BlockSpec((d, d), const),
            pl.BlockSpec((POOL_WIDTH, POOL_WIDTH), const),
            pl.BlockSpec((1, POOL_WIDTH), const),
            pl.BlockSpec((1, d), const),
            pl.BlockSpec((1, d), const),
        ],
        out_specs=pl.BlockSpec((t, d), row),
        out_shape=jax.ShapeDtypeStruct((r, d), F32),
        compiler_params=_cparams("arbitrary"),
        name="mixer_out",
    )(x, da, u, u, u, o_f, o_b, rg, mod3, w_out_bf, pool_bd, pool_scale, ln_g, ln_b)


def _router_kernel(x_ref, mod_ref, wrh_ref, wrl_ref, bias_ref, wsgu_ref, wsdn_ref,
                   tokp_ref, idx_ref, gate_ref, rank_ref, cnt_ref, fsh_ref, carry_ref):
    d = D_MODEL
    t = x_ref.shape[0]
    ne = N_EXPERTS
    neg = -jnp.inf

    @pl.when(pl.program_id(0) == 0)
    def _():
        carry_ref[...] = jnp.zeros_like(carry_ref)

    tok = _layer_norm_rows(x_ref[...]) * (1.0 + mod_ref[:, 4 * d:5 * d]) + mod_ref[:, 3 * d:4 * d]
    tok_hi = tok.astype(BF16)
    tok_lo = (tok - tok_hi.astype(F32)).astype(BF16)

    tokp_ref[...] = _pack_bf16_pairs(tok)

    hs = jnp.dot(tok_hi, wsgu_ref[...], preferred_element_type=F32)
    gs, us = hs[:, 0:EXPERT_HIDDEN], hs[:, EXPERT_HIDDEN:]
    fsh_ref[...] = jnp.dot((gs * _sigmoid(gs) * us).astype(BF16), wsdn_ref[...], preferred_element_type=F32)

    nt_dims = (((1,), (1,)), ((), ()))
    logits = (lax.dot_general(wrh_ref[...], tok_hi, nt_dims, preferred_element_type=F32)
              + lax.dot_general(wrh_ref[...], tok_lo, nt_dims, preferred_element_type=F32)
              + lax.dot_general(wrl_ref[...], tok_hi, nt_dims, preferred_element_type=F32))
    scores = _sigmoid(logits)
    biased = scores + bias_ref[...]

    gidx = lax.broadcasted_iota(jnp.int32, (GROUP_SIZE, t), 0)
    blocks, gscores = [], []
    for g in range(N_GROUPS):
        blk = biased[g * GROUP_SIZE:(g + 1) * GROUP_SIZE, :]
        m1 = jnp.max(blk, axis=0, keepdims=True)
        first = jnp.min(jnp.where(blk == m1, gidx, GROUP_SIZE), axis=0, keepdims=True)
        m2 = jnp.max(jnp.where(gidx == first, neg, blk), axis=0, keepdims=True)
        blocks.append(blk)
        gscores.append(m1 + m2)

    keep = [jnp.zeros((1, t), F32) for _ in range(N_GROUPS)]
    for _ in range(TOPK_GROUPS):
        m = gscores[0]
        for gs_ in gscores[1:]:
            m = jnp.maximum(m, gs_)
        found = jnp.zeros((1, t), F32)
        for g in range(N_GROUPS):
            hit = jnp.where(gscores[g] == m, 1.0 - found, 0.0)
            found = found + hit
            keep[g] = keep[g] + hit
            gscores[g] = jnp.where(hit > 0.0, neg, gscores[g])
    masked = jnp.concatenate([jnp.where(keep[g] > 0.0, blocks[g], neg) for g in range(N_GROUPS)], axis=0)

    ei = lax.broadcasted_iota(jnp.int32, (ne, t), 0)
    cur = masked
    onehot = jnp.zeros((ne, t), F32)
    idxs, gates = [], []
    for _ in range(TOP_K):
        m = jnp.max(cur, axis=0, keepdims=True)
        ii = jnp.min(jnp.where(cur == m, ei, ne), axis=0, keepdims=True)
        sel = ei == ii
        idxs.append(ii)
        gates.append(jnp.sum(jnp.where(sel, scores, 0.0), axis=0, keepdims=True))
        onehot = jnp.where(sel, 1.0, onehot)
        cur = jnp.where(sel, neg, cur)
    gsum = gates[0]
    for gk in gates[1:]:
        gsum = gsum + gk
    for k in range(TOP_K):
        idx_ref[k:k + 1, :] = idxs[k]
        gate_ref[k:k + 1, :] = gates[k] / gsum * ROUTED_SCALE

    ti = lax.broadcasted_iota(jnp.int32, (t, t), 0)
    tj = lax.broadcasted_iota(jnp.int32, (t, t), 1)
    before = jnp.where(ti < tj, 1.0, 0.0).astype(BF16)
    prefix = jnp.dot(onehot.astype(BF16), before, preferred_element_type=F32) + carry_ref[:, 0:1]
    for k in range(TOP_K):
        rank_k = jnp.sum(jnp.where(ei == idxs[k], prefix, 0.0), axis=0, keepdims=True)
        rank_ref[k:k + 1, :] = rank_k.astype(jnp.int32)
    carry_ref[...] = carry_ref[...] + jnp.sum(onehot, axis=1, keepdims=True)
    cnt_ref[...] = carry_ref[...].astype(jnp.int32)


def _router_call(x, mod3, wr_hi, wr_lo, bias_col, ws_gu_bf, ws_dn_bf, *, tiles_per_batch):
    r, d = x.shape
    t = ROW_TILE
    nt = r // t
    row = lambda i: (i, 0)
    col = lambda i: (0, i)
    const = lambda i: (0, 0)
    return pl.pallas_call(
        _router_kernel,
        grid=(nt,),
        in_specs=[
            pl.BlockSpec((t, d), row),
            pl.BlockSpec((None, 1, 6 * d), lambda i: (_mod_row(i, tiles_per_batch), 0, 0)),
            pl.BlockSpec((N_EXPERTS, d), const),
            pl.BlockSpec((N_EXPERTS, d), const),
            pl.BlockSpec((N_EXPERTS, 1), const),
            pl.BlockSpec((d, 2 * EXPERT_HIDDEN), const),
            pl.BlockSpec((EXPERT_HIDDEN, d), const),
        ],
        out_specs=[
            pl.BlockSpec((t, PACK_W), row),
            pl.BlockSpec((TOP_K, t), col),
            pl.BlockSpec((TOP_K, t), col),
            pl.BlockSpec((TOP_K, t), col),
            pl.BlockSpec((N_EXPERTS, LANES), const),
            pl.BlockSpec((t, d), row),
        ],
        out_shape=[
            jax.ShapeDtypeStruct((r, PACK_W), jnp.int32),
            jax.ShapeDtypeStruct((TOP_K, r), jnp.int32),
            jax.ShapeDtypeStruct((TOP_K, r), F32),
            jax.ShapeDtypeStruct((TOP_K, r), jnp.int32),
            jax.ShapeDtypeStruct((N_EXPERTS, LANES), jnp.int32),
            jax.ShapeDtypeStruct((r, d), F32),
        ],
        scratch_shapes=[pltpu.VMEM((N_EXPERTS, LANES), F32)],
        compiler_params=_cparams("arbitrary"),
        name="router",
    )(x, mod3, wr_hi, wr_lo, bias_col, ws_gu_bf, ws_dn_bf)


def _dest_kernel(idx_ref, rank_ref, offs_ref, dest_ref):
    t = idx_ref.shape[1]
    ei = lax.broadcasted_iota(jnp.int32, (N_EXPERTS, t), 0)
    offs = offs_ref[...].astype(F32)
    for k in range(TOP_K):
        start = jnp.sum(jnp.where(ei == idx_ref[k:k + 1, :], offs, 0.0), axis=0, keepdims=True)
        dest_ref[k:k + 1, :] = start.astype(jnp.int32) + rank_ref[k:k + 1, :]


def _dest_call(idx, rank, offs_col):
    r = idx.shape[1]
    t = r // DEST_STEPS
    assert r % DEST_STEPS == 0 and t % LANES == 0
    col = lambda i: (0, i)
    return pl.pallas_call(
        _dest_kernel,
        grid=(r // t,),
        in_specs=[pl.BlockSpec((TOP_K, t), col), pl.BlockSpec((TOP_K, t), col),
                  pl.BlockSpec((N_EXPERTS, 1), lambda i: (0, 0))],
        out_specs=pl.BlockSpec((TOP_K, t), col),
        out_shape=jax.ShapeDtypeStruct((TOP_K, r), jnp.int32),
        compiler_params=_cparams("arbitrary"),
        name="moe_dest",
    )(idx, rank, offs_col)


def _sc_dispatch(tokp, dest_flat, pad_rows, n_sorted):
    r, width = tokp.shape
    win = SC_GATHER_WINDOW
    workers = SC_NUM_CORES * SC_NUM_SUBCORES
    token_windows = r // win
    n_pad_windows = pad_rows.shape[0] // win
    assert r % win == 0 and dest_flat.shape[0] == TOP_K * r and n_pad_windows % workers == 0
    windows_per_worker = -(-token_windows // workers)
    pads_per_worker = n_pad_windows // workers
    mesh = plsc.VectorSubcoreMesh(core_axis_name="core", subcore_axis_name="subcore", num_cores=SC_NUM_CORES,
                                  num_subcores=SC_NUM_SUBCORES)
    zero_rows = jnp.zeros((win, width), tokp.dtype)

    @functools.partial(
        pl.kernel, out_type=jax.ShapeDtypeStruct((n_sorted + SC_SPARE_ROWS, width), tokp.dtype), mesh=mesh,
        scratch_types=[pltpu.VMEM((win,), jnp.int32), pltpu.VMEM((win, width), tokp.dtype),
                       pltpu.SemaphoreType.DMA],
        name="moe_sc_dispatch")
    def dispatch_kernel(tok_hbm, dest_hbm, pad_hbm, zero_hbm, xs_hbm, idx_vmem, rows_vmem, sem):
        worker = lax.axis_index("subcore") * SC_NUM_CORES + lax.axis_index("core")

        @pl.loop(0, windows_per_worker)
        def _(j):
            window = j * workers + worker

            @pl.when(window < token_windows)
            def _():
                tok0 = window * win
                pltpu.sync_copy(tok_hbm.at[pl.ds(tok0, win)], rows_vmem)
                for k in range(TOP_K):
                    pltpu.sync_copy(dest_hbm.at[pl.ds(k * r + tok0, win)], idx_vmem)
                    pltpu.async_copy(rows_vmem, xs_hbm.at[idx_vmem], sem).wait()

        pltpu.sync_copy(zero_hbm, rows_vmem)

        @pl.loop(0, pads_per_worker)
        def _(j):
            off = (worker * pads_per_worker + j) * win
            pltpu.sync_copy(pad_hbm.at[pl.ds(off, win)], idx_vmem)
            pltpu.async_copy(rows_vmem, xs_hbm.at[idx_vmem], sem).wait()

    return dispatch_kernel(tokp, dest_flat, pad_rows, zero_rows)


def _expert_kernel(be_ref, nb_ref, ord_ref, ue_ref, nue_ref, xs_ref, wgu_hbm, wdn_hbm, ys_ref, wgu_f32, wdn_f32,
                   wgu_bf, wdn_bf, sems, *, layer):
    def weight_copies(o):
        slot = o % 2
        e = ue_ref[o]
        return (pltpu.make_async_copy(wgu_hbm.at[layer, e], wgu_f32.at[slot], sems.at[0, slot]),
                pltpu.make_async_copy(wdn_hbm.at[layer, e], wdn_f32.at[slot], sems.at[1, slot]))

    def start_weights(o):
        @pl.when(o < nue_ref[0])
        def _():
            for cp in weight_copies(o):
                cp.start()

    def one_block(sub, carry):
        j = pl.program_id(0) * EXPERT_BLOCKS_PER_STEP + sub

        @pl.when(j < nb_ref[0])
        def _():
            o = ord_ref[j]
            changed = jnp.logical_or(j == 0, be_ref[j] != be_ref[jnp.maximum(j - 1, 0)])

            @pl.when(j == 0)
            def _():
                start_weights(0)
                start_weights(1)

            @pl.when(changed)
            def _():
                for cp in weight_copies(o):
                    cp.wait()
                slot = o % 2
                wgu_bf[...] = wgu_f32[slot].astype(BF16)
                wdn_bf[...] = wdn_f32[slot].astype(BF16)
                start_weights(o + 2)

            rows = pl.ds(pl.multiple_of(sub * EXPERT_BLOCK, EXPERT_BLOCK), EXPERT_BLOCK)
            x_lo, x_hi = _unpack_bf16_pairs(xs_ref[rows, :])
            h = (jnp.dot(x_lo.astype(BF16), wgu_bf[0:PACK_W, :], preferred_element_type=F32)
                 + jnp.dot(x_hi.astype(BF16), wgu_bf[PACK_W:, :], preferred_element_type=F32))
            g, u = h[:, 0:EXPERT_HIDDEN], h[:, EXPERT_HIDDEN:]
            y = jnp.dot((g * _sigmoid(g) * u).astype(BF16), wdn_bf[...], preferred_element_type=F32)
            ys_ref[rows, :] = _pack_bf16_pairs(y)

        return carry

    lax.fori_loop(0, EXPERT_BLOCKS_PER_STEP, one_block, 0)


def _expert_call(block_expert, n_blocks_used, block_ordinal, used_expert, n_used_experts, xs, w_gu, w_dn, layer):
    n_rows = block_expert.shape[0] * EXPERT_BLOCK
    bm = EXPERT_BLOCK
    d = D_MODEL
    step_rows = bm * EXPERT_BLOCKS_PER_STEP
    assert n_rows % step_rows == 0
    used_step = lambda s, be, nb, od, ue, nue: (jnp.minimum(s, (nb[0] - 1) // EXPERT_BLOCKS_PER_STEP), 0)
    grid_spec = pltpu.PrefetchScalarGridSpec(
        num_scalar_prefetch=5,
        grid=(n_rows // step_rows,),
        in_specs=[
            pl.BlockSpec((step_rows, PACK_W), used_step),
            pl.BlockSpec(memory_space=pl.ANY),
            pl.BlockSpec(memory_space=pl.ANY),
        ],
        out_specs=pl.BlockSpec((step_rows, PACK_W), used_step),
        scratch_shapes=[
            pltpu.VMEM((2, d, 2 * EXPERT_HIDDEN), F32),
            pltpu.VMEM((2, EXPERT_HIDDEN, d), F32),
            pltpu.VMEM((d, 2 * EXPERT_HIDDEN), BF16),
            pltpu.VMEM((EXPERT_HIDDEN, d), BF16),
            pltpu.SemaphoreType.DMA((2, 2)),
        ],
    )
    return pl.pallas_call(
        functools.partial(_expert_kernel, layer=layer),
        grid_spec=grid_spec,
        out_shape=jax.ShapeDtypeStruct((n_rows, PACK_W), jnp.int32),
        compiler_params=_cparams("arbitrary"),
        name="moe_experts",
    )(block_expert, n_blocks_used, block_ordinal, used_expert, n_used_experts, xs, w_gu, w_dn)


def _sc_gather_rows(table, indices):
    n = indices.shape[0]
    width = table.shape[1]
    workers = SC_NUM_CORES * SC_NUM_SUBCORES
    assert n % (SC_GATHER_WINDOW * workers) == 0
    per_worker = n // workers
    mesh = plsc.VectorSubcoreMesh(core_axis_name="core", subcore_axis_name="subcore", num_cores=SC_NUM_CORES,
                                  num_subcores=SC_NUM_SUBCORES)

    @functools.partial(
        pl.kernel, out_type=jax.ShapeDtypeStruct((n, width), table.dtype), mesh=mesh,
        scratch_types=[pltpu.VMEM((SC_GATHER_WINDOW,), jnp.int32),
                       pltpu.VMEM((SC_GATHER_WINDOW, width), table.dtype),
                       pltpu.SemaphoreType.DMA],
        name="moe_sc_gather")
    def gather_kernel(table_hbm, idx_hbm, out_hbm, idx_vmem, rows_vmem, sem):
        worker = lax.axis_index("subcore") * SC_NUM_CORES + lax.axis_index("core")
        base = worker * per_worker

        @pl.loop(0, per_worker // SC_GATHER_WINDOW)
        def _(w):
            off = base + w * SC_GATHER_WINDOW
            pltpu.sync_copy(idx_hbm.at[pl.ds(off, SC_GATHER_WINDOW)], idx_vmem)
            pltpu.async_copy(table_hbm.at[idx_vmem], rows_vmem, sem).wait()
            pltpu.sync_copy(rows_vmem, out_hbm.at[pl.ds(off, SC_GATHER_WINDOW)])

    return gather_kernel(table, indices)


def _combine_kernel(*refs, alpha):
    y_refs = refs[:TOP_K]
    x_ref, fsh_ref, gate_ref, mod_ref, lng_ref, lnb_ref, o_ref = refs[TOP_K:]
    d = D_MODEL
    t = x_ref.shape[0]
    gate_rows = gate_ref[...]
    pad = jnp.zeros((LANES - TOP_K, t), F32)
    gate_cols = jnp.concatenate([gate_rows, pad], axis=0).T
    f_lo = fsh_ref[:, 0:PACK_W]
    f_hi = fsh_ref[:, PACK_W:]
    for k in range(TOP_K):
        y_lo, y_hi = _unpack_bf16_pairs(y_refs[k][...])
        f_lo = f_lo + gate_cols[:, k:k + 1] * y_lo
        f_hi = f_hi + gate_cols[:, k:k + 1] * y_hi
    f = jnp.concatenate([f_lo, f_hi], axis=1)
    z = alpha * x_ref[...] + mod_ref[:, 5 * d:6 * d] * f
    o_ref[...] = _layer_norm_rows(z) * lng_ref[...] + lnb_ref[...]


def _combine_call(y_tok, x, fsh, gate, mod3, ln_g, ln_b, *, tiles_per_batch, alpha, drop_context):
    r, d = x.shape
    t = ROW_TILE
    nt = r // t
    if drop_context:
        per_batch = tiles_per_batch - 1
        src = lambda i: (i // per_batch) * tiles_per_batch + i % per_batch
        n_tiles = nt // tiles_per_batch * per_batch
    else:
        src = lambda i: i
        n_tiles = nt
    row = lambda i: (src(i), 0)
    col = lambda i: (0, src(i))
    const = lambda i: (0, 0)
    kern = functools.partial(_combine_kernel, alpha=alpha)
    y_specs = [pl.BlockSpec((t, PACK_W), functools.partial(lambda k, i: (k * nt + src(i), 0), k))
               for k in range(TOP_K)]
    return pl.pallas_call(
        kern,
        grid=(n_tiles,),
        in_specs=y_specs + [
            pl.BlockSpec((t, d), row),
            pl.BlockSpec((t, d), row),
            pl.BlockSpec((TOP_K, t), col),
            pl.BlockSpec((None, 1, 6 * d), lambda i: (_mod_row(src(i), tiles_per_batch), 0, 0)),
            pl.BlockSpec((1, d), const),
            pl.BlockSpec((1, d), const),
        ],
        out_specs=pl.BlockSpec((t, d), lambda i: (i, 0)),
        out_shape=jax.ShapeDtypeStruct((n_tiles * t, d), F32),
        compiler_params=_cparams("arbitrary"),
        name="moe_combine",
    )(*([y_tok] * TOP_K), x, fsh, gate, mod3, ln_g, ln_b)


def _rope_tables(seq):
    rows = seq // GRID_W
    row = jnp.repeat(jnp.arange(rows, dtype=F32), GRID_W)
    col = jnp.tile(jnp.arange(GRID_W, dtype=F32), rows)
    nf = DA_DIM // 4
    freqs = ROPE_BASE ** (-jnp.arange(nf, dtype=F32) / nf)
    cr, sr = jnp.cos(row[:, None] * freqs), jnp.sin(row[:, None] * freqs)
    cc, sc = jnp.cos(col[:, None] * freqs), jnp.sin(col[:, None] * freqs)
    c64 = jnp.concatenate([cr, cr, cc, cc], axis=1)
    s64 = jnp.concatenate([-sr, sr, -sc, sc], axis=1)
    c = jnp.concatenate([jnp.tile(c64, (1, 2)), jnp.ones((CTX_LEN, LANES), F32)], axis=0)
    s = jnp.concatenate([jnp.tile(s64, (1, 2)), jnp.zeros((CTX_LEN, LANES), F32)], axis=0)
    return c, s


def kernel(x, c, ctx, c_ctx, w_mod, b_mod, w_in, w_out, diff_lambda, pool_w, pool_scale, ret_log_decay, ln_g, ln_b,
           w_router, router_bias, w_expert_gate_up, w_expert_down, w_shared_gate_up, w_shared_down):
    batch, seq, d = x.shape
    depth = w_mod.shape[0]
    assert d == D_MODEL and ctx.shape[1] == CTX_LEN == ROW_TILE and batch == 2
    assert seq % ROW_TILE == 0 and seq % GRID_W == 0 and w_in.shape[-1] == IN_WIDTH
    rows_per_batch = seq + CTX_LEN
    tiles_per_batch = rows_per_batch // ROW_TILE
    r = batch * rows_per_batch
    alpha = (2.0 * depth) ** 0.25

    xa = jnp.concatenate([x, ctx], axis=1).reshape(r, d)
    cvec = jnp.zeros((8, d), F32).at[0:batch].set(c).at[batch].set(c_ctx)
    mod_all = _mod_call(cvec, w_mod, b_mod)
    rope_c, rope_s = _rope_tables(seq)

    n_sorted = r * TOP_K + N_EXPERTS * EXPERT_BLOCK
    n_blocks = n_sorted // EXPERT_BLOCK

    for l in range(depth):
        lambda_init = 0.8 - 0.6 * math.exp(-0.3 * l)
        mod3 = mod_all[l].reshape(8, 1, 6 * d)
        lng = ln_g[l].reshape(2, 1, d)
        lnb = ln_b[l].reshape(2, 1, d)

        w_in_bf = w_in[l].astype(BF16)
        w_vt_bf = w_in_bf[:, QK_WIDTH:QK_WIDTH + DA_WIDTH].T
        qk, vda, u, rqkv, rg = _inproj_call(xa, mod3, w_in_bf, w_vt_bf, rope_c, rope_s, tiles_per_batch)
        da = _attn_call(diff_lambda[l], qk, vda, batch=batch, rows_per_batch=rows_per_batch, seq=seq,
                        lambda_init=lambda_init)
        o_f, o_b = _ret_call(ret_log_decay[l], rqkv, batch=batch, rows_per_batch=rows_per_batch, seq=seq)
        pool_bd = jnp.zeros((POOL_WIDTH, POOL_WIDTH), F32)
        for gi in range(len(POOL_WINDOWS)):
            sl = slice(gi * POOL_GROUP, (gi + 1) * POOL_GROUP)
            pool_bd = pool_bd.at[sl, sl].set(pool_w[l, gi])
        xa = _mixout_call(xa, da, u, o_f, o_b, rg, mod3, w_out[l].astype(BF16), pool_bd.astype(BF16),
                          pool_scale[l].reshape(1, POOL_WIDTH), lng[0], lnb[0],
                          tiles_per_batch=tiles_per_batch, seq=seq, alpha=alpha)

        wr_t = w_router[l].T
        wr_hi = wr_t.astype(BF16)
        wr_lo = (wr_t - wr_hi.astype(F32)).astype(BF16)
        tokp, idx, gate, rank, cnt, fsh = _router_call(
            xa, mod3, wr_hi, wr_lo, router_bias[l].reshape(N_EXPERTS, 1),
            w_shared_gate_up[l].astype(BF16), w_shared_down[l].astype(BF16), tiles_per_batch=tiles_per_batch)
        counts = cnt[:, 0]
        padded = (counts + EXPERT_BLOCK - 1) // EXPERT_BLOCK * EXPERT_BLOCK
        pad_end = jnp.cumsum(padded)
        offs = pad_end - padded
        expert_ids = jnp.arange(N_EXPERTS, dtype=jnp.int32)
        blk_row = jnp.arange(n_blocks, dtype=jnp.int32) * EXPERT_BLOCK
        block_expert = jnp.minimum(jnp.sum(pad_end[None, :] <= blk_row[:, None], axis=1), N_EXPERTS - 1)
        n_used = pad_end[-1:] // EXPERT_BLOCK
        used = counts > 0
        ordinal = jnp.cumsum(used) - 1
        hit = used[None, :] & (ordinal[None, :] == expert_ids[:, None])
        used_expert = jnp.sum(jnp.where(hit, expert_ids[None, :], 0), axis=1)
        n_used_experts = jnp.sum(used)[None]
        block_ordinal = ordinal[block_expert]
        slot = jnp.arange(EXPERT_BLOCK, dtype=jnp.int32)[None, :]
        first_pad = (padded - EXPERT_BLOCK)[:, None] + slot
        is_pad = (first_pad >= counts[:, None]) & (padded[:, None] > 0)
        spare = n_sorted + jnp.arange(N_EXPERTS * EXPERT_BLOCK, dtype=jnp.int32).reshape(N_EXPERTS, EXPERT_BLOCK)
        pad_rows = jnp.where(is_pad, offs[:, None] + first_pad, spare).reshape(N_EXPERTS * EXPERT_BLOCK)
        i32 = lambda a: a.astype(jnp.int32)

        dest = _dest_call(idx, rank, i32(offs).reshape(N_EXPERTS, 1))
        dest_flat = dest.reshape(TOP_K * r)
        xs = _sc_dispatch(tokp, dest_flat, i32(pad_rows), n_sorted)
        ys = _expert_call(i32(block_expert), i32(n_used), i32(block_ordinal), i32(used_expert),
                          i32(n_used_experts), xs, w_expert_gate_up, w_expert_down, l)
        y_tok = _sc_gather_rows(ys, dest_flat)
        xa = _combine_call(y_tok, xa, fsh, gate, mod3, lng[1], lnb[1], tiles_per_batch=tiles_per_batch,
                           alpha=alpha, drop_context=(l == depth - 1))

    return xa.reshape(batch, seq, d)
```

```python
import functools
import math

import jax
import jax.numpy as jnp
from jax import lax
from jax.experimental import pallas as pl
from jax.experimental.pallas import tpu as pltpu
from jax.experimental.pallas import tpu_sc as plsc

F32 = jnp.float32
BF16 = jnp.bfloat16
HIGHEST = lax.Precision.HIGHEST

D_MODEL = 1024
CTX_LEN = 256
GRID_W = 64
DA_HEADS = 4
DA_DIM = 64
DA_VDIM = 2 * DA_DIM
DA_WIDTH = DA_HEADS * DA_VDIM
ROPE_BASE = 10000.0
POOL_WINDOWS = (2, 4, 8, 16)
POOL_GROUP = 64
POOL_WIDTH = len(POOL_WINDOWS) * POOL_GROUP
POOL_HALO = 8
RET_HEADS = 4
RET_DK = 64
RET_WIDTH = RET_HEADS * RET_DK
RET_CHUNK = 128
QK_WIDTH = 2 * DA_HEADS * 2 * DA_DIM
IN_WIDTH = QK_WIDTH + DA_WIDTH + POOL_WIDTH + 4 * RET_WIDTH
N_EXPERTS = 256
TOP_K = 8
N_GROUPS = 8
GROUP_SIZE = N_EXPERTS // N_GROUPS
TOPK_GROUPS = 4
EXPERT_HIDDEN = 256
ROUTED_SCALE = 2.5
LN_EPS = 1e-6
RMS_EPS = 1e-5

LANES = 128
ROW_TILE = 256
DEST_STEPS = 4
ATTN_Q_TILE = 256
ATTN_K_CHUNK = 256
ATTN_UNROLL = 16
SC_NUM_CORES = 2
SC_NUM_SUBCORES = 16
SC_GATHER_WINDOW = 128
EXPERT_BLOCK = 256
EXPERT_BLOCKS_PER_STEP = 4
SC_SPARE_ROWS = N_EXPERTS * EXPERT_BLOCK
PACK_W = D_MODEL // 2
VMEM_LIMIT = 56 * 1024 * 1024


def _cparams(*sem):
    return pltpu.CompilerParams(dimension_semantics=sem, vmem_limit_bytes=VMEM_LIMIT)


def _sigmoid(x):
    return 1.0 / (1.0 + jnp.exp(-x))


def _layer_norm_rows(x):
    mu = jnp.mean(x, axis=-1, keepdims=True)
    xc = x - mu
    var = jnp.mean(xc * xc, axis=-1, keepdims=True)
    return xc * lax.rsqrt(var + LN_EPS)


def _pack_bf16_pairs(x):
    half = x.shape[1] // 2
    bits = pltpu.bitcast(x.astype(BF16).astype(F32), jnp.uint32)
    word = lax.shift_right_logical(bits[:, 0:half], jnp.uint32(16)) | (bits[:, half:] & jnp.uint32(0xFFFF0000))
    return pltpu.bitcast(word, jnp.int32)


def _unpack_bf16_pairs(packed):
    word = pltpu.bitcast(packed, jnp.uint32)
    lo = pltpu.bitcast(lax.shift_left(word, jnp.uint32(16)), F32)
    hi = pltpu.bitcast(word & jnp.uint32(0xFFFF0000), F32)
    return lo, hi


def _mod_row(i, tiles_per_batch):
    return jnp.where(i % tiles_per_batch == tiles_per_batch - 1, 2, i // tiles_per_batch)


def _mod_kernel(c_ref, w_ref, b_ref, o_ref):
    c = c_ref[...]
    s = c * _sigmoid(c)
    o_ref[...] = jnp.dot(s, w_ref[...], precision=HIGHEST, preferred_element_type=F32) + b_ref[...]


def _mod_call(cvec, w_mod, b_mod):
    depth, d, n = w_mod.shape
    tn = 1536
    return pl.pallas_call(
        _mod_kernel,
        grid=(depth, n // tn),
        in_specs=[
            pl.BlockSpec((8, d), lambda l, j: (0, 0)),
            pl.BlockSpec((None, d, tn), lambda l, j: (l, 0, j)),
            pl.BlockSpec((None, 1, tn), lambda l, j: (l, 0, j)),
        ],
        out_specs=pl.BlockSpec((None, 8, tn), lambda l, j: (l, 0, j)),
        out_shape=jax.ShapeDtypeStruct((depth, 8, n), F32),
        compiler_params=_cparams("arbitrary", "arbitrary"),
        name="mod",
    )(cvec, w_mod, b_mod.reshape(depth, 1, n))


def _inproj_kernel(x_ref, mod_ref, w_ref, wvt_ref, ct_ref, st_ref, qk_ref, vt_ref, u_ref, r_ref, g_ref):
    d = D_MODEL
    xn = _layer_norm_rows(x_ref[...])
    h = (xn * (1.0 + mod_ref[:, d:2 * d]) + mod_ref[:, 0:d]).astype(BF16)

    a = jnp.dot(h, w_ref[:, 0:QK_WIDTH], preferred_element_type=F32)
    lane = lax.broadcasted_iota(jnp.int32, (a.shape[0], LANES), 1)
    first_half = (lane % 32) < 16
    ct = ct_ref[...]
    st = st_ref[...]
    for s in range(QK_WIDTH // LANES):
        blk = a[:, s * LANES:(s + 1) * LANES]
        partner = jnp.where(first_half, pltpu.roll(blk, LANES - 16, 1), pltpu.roll(blk, 16, 1))
        rot = blk * ct + partner * st
        if s < QK_WIDTH // LANES // 2:
            rot = rot * (DA_DIM ** -0.5 * math.log2(math.e))
        qk_ref[:, s * LANES:(s + 1) * LANES] = rot.astype(BF16)

    vt_ref[...] = lax.dot_general(wvt_ref[...], h, (((1,), (1,)), ((), ())),
                                  preferred_element_type=F32).astype(BF16)
    o = QK_WIDTH + DA_WIDTH
    u_ref[...] = jnp.dot(h, w_ref[:, o:o + POOL_WIDTH], preferred_element_type=F32)
    o += POOL_WIDTH
    r = jnp.dot(h, w_ref[:, o:o + 3 * RET_WIDTH], preferred_element_type=F32)
    r_ref[:, 0:RET_WIDTH] = r[:, 0:RET_WIDTH].astype(BF16)
    r_ref[:, RET_WIDTH:2 * RET_WIDTH] = (r[:, RET_WIDTH:2 * RET_WIDTH] * (RET_DK ** -0.5)).astype(BF16)
    r_ref[:, 2 * RET_WIDTH:] = r[:, 2 * RET_WIDTH:].astype(BF16)
    o += 3 * RET_WIDTH
    g_ref[...] = jnp.dot(h, w_ref[:, o:o + RET_WIDTH], preferred_element_type=F32)


def _inproj_call(x, mod3, w_in_bf, w_vt_bf, rope_c, rope_s, tiles_per_batch):
    r, d = x.shape
    t = ROW_TILE
    nt = r // t
    row = lambda i: (i, 0)
    return pl.pallas_call(
        _inproj_kernel,
        grid=(nt,),
        in_specs=[
            pl.BlockSpec((t, d), row),
            pl.BlockSpec((None, 1, 6 * d), lambda i: (_mod_row(i, tiles_per_batch), 0, 0)),
            pl.BlockSpec((d, IN_WIDTH), lambda i: (0, 0)),
            pl.BlockSpec((DA_WIDTH, d), lambda i: (0, 0)),
            pl.BlockSpec((t, LANES), lambda i: (i % tiles_per_batch, 0)),
            pl.BlockSpec((t, LANES), lambda i: (i % tiles_per_batch, 0)),
        ],
        out_specs=[
            pl.BlockSpec((t, QK_WIDTH), row),
            pl.BlockSpec((DA_WIDTH, t), lambda i: (0, i)),
            pl.BlockSpec((t, POOL_WIDTH), row),
            pl.BlockSpec((t, 3 * RET_WIDTH), row),
            pl.BlockSpec((t, RET_WIDTH), row),
        ],
        out_shape=[
            jax.ShapeDtypeStruct((r, QK_WIDTH), BF16),
            jax.ShapeDtypeStruct((DA_WIDTH, r), BF16),
            jax.ShapeDtypeStruct((r, POOL_WIDTH), F32),
            jax.ShapeDtypeStruct((r, 3 * RET_WIDTH), BF16),
            jax.ShapeDtypeStruct((r, RET_WIDTH), F32),
        ],
        compiler_params=_cparams("arbitrary"),
        name="inproj",
    )(x, mod3, w_in_bf, w_vt_bf, rope_c, rope_s)


def _attn_kernel(lam_ref, q_ref, k_ref, vt_ref, o_ref, *s_refs, k_chunk, seq, lambda_init):
    mq = ATTN_Q_TILE
    n_tiles = (seq + CTX_LEN) // mq
    n_chunks = (seq + CTX_LEN) // k_chunk
    last = n_chunks - 1
    n_iters = last // ATTN_UNROLL
    neg_inf = jnp.full((1, 2 * mq), -jnp.inf, F32)
    acc_zero = jnp.zeros((DA_VDIM + 16, 2 * mq), F32)
    ones_rows = jnp.where(lax.broadcasted_iota(jnp.int32, (16, k_chunk), 0) == 0, 1.0, 0.0).astype(BF16)

    def tile_rows(i):
        return pl.ds(pl.multiple_of(i * mq, mq), mq)

    def q_transposed(i):
        q = q_ref[tile_rows(i), :]
        lane = lax.broadcasted_iota(jnp.int32, q.shape, 1)
        zero = jnp.zeros_like(q)
        q2 = jnp.concatenate([jnp.where(lane < DA_DIM, q, zero), jnp.where(lane >= DA_DIM, q, zero)], axis=0)
        return q2.astype(F32).T.astype(BF16)

    def score_chunk(s_ref, c, qt, m):
        off = pl.multiple_of(c * k_chunk, k_chunk)
        s = jnp.dot(k_ref[pl.ds(off, k_chunk), :], qt, preferred_element_type=F32)
        s_ref[c] = s
        return jnp.maximum(m, jnp.max(s, axis=0, keepdims=True))

    def value_chunk(s_ref, c, m, acc):
        off = pl.multiple_of(c * k_chunk, k_chunk)
        vt = jnp.concatenate([vt_ref[:, pl.ds(off, k_chunk)], ones_rows], axis=0)
        p = jnp.exp2((s_ref[c] - m).astype(BF16))
        return acc + jnp.dot(vt, p, preferred_element_type=F32)

    def finish(i, acc):
        l0, l1 = acc[DA_VDIM:DA_VDIM + 1, 0:mq], acc[DA_VDIM:DA_VDIM + 1, mq:]
        a0, a1 = acc[0:DA_VDIM, 0:mq], acc[0:DA_VDIM, mq:]
        lv = lam_ref[...]
        lam = (jnp.exp(jnp.sum(lv[0:1] * lv[1:2], axis=-1, keepdims=True))
               - jnp.exp(jnp.sum(lv[2:3] * lv[3:4], axis=-1, keepdims=True)) + lambda_init)
        o = a0 / l0 - lam * (a1 / l1)
        o = o * lax.rsqrt(jnp.mean(o * o, axis=0, keepdims=True) + RMS_EPS) * (1.0 - lambda_init)
        o_ref[tile_rows(i), :] = o.T.astype(BF16)

    def scores_only(s_ref, qt):
        def body(it, m):
            for u in range(ATTN_UNROLL):
                m = score_chunk(s_ref, it * ATTN_UNROLL + u, qt, m)
            return m
        return score_chunk(s_ref, last, qt, lax.fori_loop(0, n_iters, body, neg_inf))

    def values_only(s_ref, m):
        def body(it, acc):
            for u in range(ATTN_UNROLL):
                acc = value_chunk(s_ref, it * ATTN_UNROLL + u, m, acc)
            return acc
        return value_chunk(s_ref, last, m, lax.fori_loop(0, n_iters, body, acc_zero))

    def fused_tile(i, m_prev, s_cur, s_prev):
        qt = q_transposed(i)

        def body(it, carry):
            m, acc = carry
            for u in range(ATTN_UNROLL):
                c = it * ATTN_UNROLL + u
                m = score_chunk(s_cur, c, qt, m)
                acc = value_chunk(s_prev, c, m_prev, acc)
            return m, acc

        m, acc = lax.fori_loop(0, n_iters, body, (neg_inf, acc_zero))
        m = score_chunk(s_cur, last, qt, m)
        finish(i - 1, value_chunk(s_prev, last, m_prev, acc))
        return m

    s_even, s_odd = s_refs
    ctx_tile = n_tiles - 1
    assert ctx_tile % 2 == 0 and ctx_tile >= 2
    m = scores_only(s_even, q_transposed(0))

    def tile_pair(p, m):
        m = fused_tile(2 * p + 1, m, s_odd, s_even)
        return fused_tile(2 * p + 2, m, s_even, s_odd)

    m = lax.fori_loop(0, (ctx_tile - 2) // 2, tile_pair, m)
    m = fused_tile(ctx_tile - 1, m, s_odd, s_even)
    m_ctx = score_chunk(s_even, last, q_transposed(ctx_tile), neg_inf)
    finish(ctx_tile - 1, values_only(s_odd, m))
    finish(ctx_tile, value_chunk(s_even, last, m_ctx, acc_zero))


def _attn_call(lam_vec, qk, vda, *, batch, rows_per_batch, seq, lambda_init):
    tq = ATTN_Q_TILE
    assert seq % (ATTN_K_CHUNK * ATTN_UNROLL) == 0 and rows_per_batch - seq == CTX_LEN == tq == ATTN_K_CHUNK
    nq = rows_per_batch // tq
    kern = functools.partial(_attn_kernel, k_chunk=ATTN_K_CHUNK, seq=seq, lambda_init=lambda_init)
    return pl.pallas_call(
        kern,
        grid=(batch, DA_HEADS),
        in_specs=[
            pl.BlockSpec((4, DA_DIM), lambda b, h: (0, 0)),
            pl.BlockSpec((rows_per_batch, DA_VDIM), lambda b, h: (b, h)),
            pl.BlockSpec((rows_per_batch, DA_VDIM), lambda b, h: (b, DA_HEADS + h)),
            pl.BlockSpec((DA_VDIM, rows_per_batch), lambda b, h: (h, b)),
        ],
        out_specs=pl.BlockSpec((rows_per_batch, DA_VDIM), lambda b, h: (b, h)),
        out_shape=jax.ShapeDtypeStruct((qk.shape[0], DA_WIDTH), BF16),
        scratch_shapes=[pltpu.VMEM((rows_per_batch // ATTN_K_CHUNK, ATTN_K_CHUNK, 2 * tq), F32)] * 2,
        compiler_params=_cparams("arbitrary", "arbitrary"),
        name="diff_attn",
    )(lam_vec, qk, qk, vda)


def _ret_kernel(ld_ref, f_ref, b_ref, of_ref, ob_ref, dm_ref, qd_ref, kd_ref, cd_ref, st_ref):
    c = pl.program_id(1)
    ch = RET_CHUNK
    w = RET_WIDTH
    lane_head = lax.broadcasted_iota(jnp.int32, (1, w), 1) // RET_DK

    @pl.when(c == 0)
    def _():
        st_ref[...] = jnp.zeros_like(st_ref)
        ri = lax.broadcasted_iota(jnp.int32, (ch, ch), 0)
        ci = lax.broadcasted_iota(jnp.int32, (ch, ch), 1)
        rowf = lax.broadcasted_iota(jnp.int32, (ch, w), 0).astype(F32)
        for d in range(2):
            lg_lane = jnp.zeros((1, w), F32)
            for hh in range(RET_HEADS):
                lg = -jnp.exp(jnp.full((1, 1), ld_ref[d, hh], F32))
                lg_lane = jnp.where(lane_head == hh, lg, lg_lane)
                dist = ((ri - ci) if d == 0 else (ci - ri)).astype(F32)
                dm_ref[d, hh] = jnp.where(dist >= 0, jnp.exp(dist * lg), 0.0)
            if d == 0:
                qd_ref[d] = jnp.exp((rowf + 1.0) * lg_lane)
                kd_ref[d] = jnp.exp((ch - 1.0 - rowf) * lg_lane)
            else:
                qd_ref[d] = jnp.exp((ch - rowf) * lg_lane)
                kd_ref[d] = jnp.exp(rowf * lg_lane)
            cd_ref[d] = jnp.exp(float(ch) * lg_lane)

    rblk = lax.broadcasted_iota(jnp.int32, (w, w), 0) // RET_DK
    cblk = lax.broadcasted_iota(jnp.int32, (w, w), 1) // RET_DK
    for d, (src, dst) in enumerate(((f_ref, of_ref), (b_ref, ob_ref))):
        q = src[:, 0:w]
        k = src[:, w:2 * w]
        v = src[:, 2 * w:3 * w]
        st = st_ref[d]
        o = jnp.dot((q.astype(F32) * qd_ref[d]).astype(BF16), st.astype(BF16), preferred_element_type=F32)
        for hh in range(RET_HEADS):
            in_head = lane_head == hh
            qm = jnp.where(in_head, q, jnp.zeros_like(q))
            s = lax.dot_general(qm, k, (((1,), (1,)), ((), ())), preferred_element_type=F32)
            intra = (s * dm_ref[d, hh]).astype(BF16)
            o = o + jnp.where(in_head, jnp.dot(intra, v, preferred_element_type=F32), 0.0)
        dst[...] = o
        kk_t = (k.astype(F32) * kd_ref[d]).T.astype(BF16)
        upd = jnp.dot(kk_t, v, preferred_element_type=F32)
        st_ref[d] = jnp.where(rblk == cblk, st * cd_ref[d] + upd, 0.0)


def _ret_call(log_decay, rqkv, *, batch, rows_per_batch, seq):
    ch = RET_CHUNK
    nc = rows_per_batch // ch
    n_lat = seq // ch
    n_ctx = nc - n_lat

    def fwd(b, c):
        return (b * nc + jnp.where(c < n_ctx, n_lat + c, c - n_ctx), 0)

    def bwd(b, c):
        return (b * nc + nc - 1 - c, 0)

    w = RET_WIDTH
    return pl.pallas_call(
        _ret_kernel,
        grid=(batch, nc),
        in_specs=[
            pl.BlockSpec(memory_space=pltpu.SMEM),
            pl.BlockSpec((ch, 3 * w), fwd),
            pl.BlockSpec((ch, 3 * w), bwd),
        ],
        out_specs=[pl.BlockSpec((ch, w), fwd), pl.BlockSpec((ch, w), bwd)],
        out_shape=[jax.ShapeDtypeStruct((rqkv.shape[0], w), F32)] * 2,
        scratch_shapes=[
            pltpu.VMEM((2, RET_HEADS, ch, ch), F32),
            pltpu.VMEM((2, ch, w), F32),
            pltpu.VMEM((2, ch, w), F32),
            pltpu.VMEM((2, 1, w), F32),
            pltpu.VMEM((2, w, w), F32),
        ],
        compiler_params=_cparams("arbitrary", "arbitrary"),
        name="retention",
    )(log_decay, rqkv, rqkv)


def _mixout_kernel(x_ref, da_ref, u_ref, up_ref, un_ref, of_ref, ob_ref, rg_ref, mod_ref, wo_ref, pw_ref,
                   ps_ref, lng_ref, lnb_ref, o_ref, *, tiles_per_batch, seq, alpha):
    d = D_MODEL
    t = x_ref.shape[0]
    i = pl.program_id(0)
    j = i % tiles_per_batch
    is_ctx = j == tiles_per_batch - 1
    stream_len = jnp.where(is_ctx, CTX_LEN, seq)
    p0 = jnp.where(is_ctx, 0, j * t)

    u = u_ref[...]
    prev = jnp.where(p0 > 0, up_ref[...], 0.0)
    nxt = jnp.where(p0 + t < stream_len, un_ref[...], 0.0)
    ext = jnp.concatenate([prev, u, nxt], axis=0)
    n = t + 2 * POOL_HALO
    a2 = ext + pltpu.roll(ext, 1, 0)
    a4 = pltpu.roll(a2, 1, 0) + pltpu.roll(a2, n - 1, 0)
    a8 = pltpu.roll(a4, 2, 0) + pltpu.roll(a4, n - 2, 0)
    a16 = pltpu.roll(a8, 4, 0) + pltpu.roll(a8, n - 4, 0)
    pos = p0 + lax.broadcasted_iota(jnp.int32, (t, POOL_WIDTH), 0)
    group = lax.broadcasted_iota(jnp.int32, (1, POOL_WIDTH), 1) // POOL_GROUP
    mean = jnp.zeros((t, POOL_WIDTH), F32)
    for gi, (wnd, asum) in enumerate(zip(POOL_WINDOWS, (a2, a4, a8, a16))):
        cnt = jnp.minimum(pos + wnd // 2, stream_len) - jnp.maximum(pos - wnd // 2, 0)
        mean = jnp.where(group == gi, asum[POOL_HALO:POOL_HALO + t] / cnt.astype(F32), mean)
    pool = jnp.dot((mean - u).astype(BF16), pw_ref[...], preferred_element_type=F32) * ps_ref[...]

    o = of_ref[...] + ob_ref[...]
    head = lax.broadcasted_iota(jnp.int32, (1, RET_WIDTH), 1) // RET_DK

    def head_mean(val):
        out = jnp.zeros_like(val)
        for hh in range(RET_HEADS):
            m = jnp.sum(jnp.where(head == hh, val, 0.0), axis=-1, keepdims=True) * (1.0 / RET_DK)
            out = jnp.where(head == hh, m, out)
        return out

    oc = o - head_mean(o)
    rn = oc * lax.rsqrt(head_mean(oc * oc) + LN_EPS)
    g = rg_ref[...]
    ret = rn * (g * _sigmoid(g))

    y = jnp.dot(da_ref[...], wo_ref[0:DA_WIDTH, :], preferred_element_type=F32)
    y = y + jnp.dot(pool.astype(BF16), wo_ref[DA_WIDTH:DA_WIDTH + POOL_WIDTH, :], preferred_element_type=F32)
    y = y + jnp.dot(ret.astype(BF16), wo_ref[DA_WIDTH + POOL_WIDTH:, :], preferred_element_type=F32)
    z = alpha * x_ref[...] + mod_ref[:, 2 * d:3 * d] * y
    o_ref[...] = _layer_norm_rows(z) * lng_ref[...] + lnb_ref[...]


def _mixout_call(x, da, u, o_f, o_b, rg, mod3, w_out_bf, pool_bd, pool_scale, ln_g, ln_b, *, tiles_per_batch, seq,
                 alpha):
    r, d = x.shape
    t = ROW_TILE
    nt = r // t
    hb = t // POOL_HALO
    n_halo_blocks = r // POOL_HALO
    row = lambda i: (i, 0)
    const = lambda i: (0, 0)
    kern = functools.partial(_mixout_kernel, tiles_per_batch=tiles_per_batch, seq=seq, alpha=alpha)
    return pl.pallas_call(
        kern,
        grid=(nt,),
        in_specs=[
            pl.BlockSpec((t, d), row),
            pl.BlockSpec((t, DA_WIDTH), row),
            pl.BlockSpec((t, POOL_WIDTH), row),
            pl.BlockSpec((POOL_HALO, POOL_WIDTH), lambda i: (jnp.maximum(i * hb - 1, 0), 0)),
            pl.BlockSpec((POOL_HALO, POOL_WIDTH), lambda i: (jnp.minimum((i + 1) * hb, n_halo_blocks - 1), 0)),
            pl.BlockSpec((t, RET_WIDTH), row),
            pl.BlockSpec((t, RET_WIDTH), row),
            pl.BlockSpec((t, RET_WIDTH), row),
            pl.BlockSpec((None, 1, 6 * d), lambda i: (_mod_row(i, tiles_per_batch), 0, 0)),
            pl.BlockSpec((d, d), const),
            pl.BlockSpec((POOL_WIDTH, POOL_WIDTH), const),
            pl.BlockSpec((1, POOL_WIDTH), const),
            pl.BlockSpec((1, d), const),
            pl.BlockSpec((1, d), const),
        ],
        out_specs=pl.BlockSpec((t, d), row),
        out_shape=jax.ShapeDtypeStruct((r, d), F32),
        compiler_params=_cparams("arbitrary"),
        name="mixer_out",
    )(x, da, u, u, u, o_f, o_b, rg, mod3, w_out_bf, pool_bd, pool_scale, ln_g, ln_b)


def _router_kernel(x_ref, mod_ref, wrh_ref, wrl_ref, bias_ref, wsgu_ref, wsdn_ref,
                   tokp_ref, idx_ref, gate_ref, rank_ref, cnt_ref, fsh_ref, carry_ref):
    d = D_MODEL
    t = x_ref.shape[0]
    ne = N_EXPERTS
    neg = -jnp.inf

    @pl.when(pl.program_id(0) == 0)
    def _():
        carry_ref[...] = jnp.zeros_like(carry_ref)

    tok = _layer_norm_rows(x_ref[...]) * (1.0 + mod_ref[:, 4 * d:5 * d]) + mod_ref[:, 3 * d:4 * d]
    tok_hi = tok.astype(BF16)
    tok_lo = (tok - tok_hi.astype(F32)).astype(BF16)

    tokp_ref[...] = _pack_bf16_pairs(tok)

    hs = jnp.dot(tok_hi, wsgu_ref[...], preferred_element_type=F32)
    gs, us = hs[:, 0:EXPERT_HIDDEN], hs[:, EXPERT_HIDDEN:]
    fsh_ref[...] = jnp.dot((gs * _sigmoid(gs) * us).astype(BF16), wsdn_ref[...], preferred_element_type=F32)

    nt_dims = (((1,), (1,)), ((), ()))
    logits = (lax.dot_general(wrh_ref[...], tok_hi, nt_dims, preferred_element_type=F32)
              + lax.dot_general(wrh_ref[...], tok_lo, nt_dims, preferred_element_type=F32)
              + lax.dot_general(wrl_ref[...], tok_hi, nt_dims, preferred_element_type=F32))
    scores = _sigmoid(logits)
    biased = scores + bias_ref[...]

    gidx = lax.broadcasted_iota(jnp.int32, (GROUP_SIZE, t), 0)
    blocks, gscores = [], []
    for g in range(N_GROUPS):
        blk = biased[g * GROUP_SIZE:(g + 1) * GROUP_SIZE, :]
        m1 = jnp.max(blk, axis=0, keepdims=True)
        first = jnp.min(jnp.where(blk == m1, gidx, GROUP_SIZE), axis=0, keepdims=True)
        m2 = jnp.max(jnp.where(gidx == first, neg, blk), axis=0, keepdims=True)
        blocks.append(blk)
        gscores.append(m1 + m2)

    keep = [jnp.zeros((1, t), F32) for _ in range(N_GROUPS)]
    for _ in range(TOPK_GROUPS):
        m = gscores[0]
        for gs_ in gscores[1:]:
            m = jnp.maximum(m, gs_)
        found = jnp.zeros((1, t), F32)
        for g in range(N_GROUPS):
            hit = jnp.where(gscores[g] == m, 1.0 - found, 0.0)
            found = found + hit
            keep[g] = keep[g] + hit
            gscores[g] = jnp.where(hit > 0.0, neg, gscores[g])
    masked = jnp.concatenate([jnp.where(keep[g] > 0.0, blocks[g], neg) for g in range(N_GROUPS)], axis=0)

    ei = lax.broadcasted_iota(jnp.int32, (ne, t), 0)
    cur = masked
    onehot = jnp.zeros((ne, t), F32)
    idxs, gates = [], []
    for _ in range(TOP_K):
        m = jnp.max(cur, axis=0, keepdims=True)
        ii = jnp.min(jnp.where(cur == m, ei, ne), axis=0, keepdims=True)
        sel = ei == ii
        idxs.append(ii)
        gates.append(jnp.sum(jnp.where(sel, scores, 0.0), axis=0, keepdims=True))
        onehot = jnp.where(sel, 1.0, onehot)
        cur = jnp.where(sel, neg, cur)
    gsum = gates[0]
    for gk in gates[1:]:
        gsum = gsum + gk
    for k in range(TOP_K):
        idx_ref[k:k + 1, :] = idxs[k]
        gate_ref[k:k + 1, :] = gates[k] / gsum * ROUTED_SCALE

    ti = lax.broadcasted_iota(jnp.int32, (t, t), 0)
    tj = lax.broadcasted_iota(jnp.int32, (t, t), 1)
    before = jnp.where(ti < tj, 1.0, 0.0).astype(BF16)
    prefix = jnp.dot(onehot.astype(BF16), before, preferred_element_type=F32) + carry_ref[:, 0:1]
    for k in range(TOP_K):
        rank_k = jnp.sum(jnp.where(ei == idxs[k], prefix, 0.0), axis=0, keepdims=True)
        rank_ref[k:k + 1, :] = rank_k.astype(jnp.int32)
    carry_ref[...] = carry_ref[...] + jnp.sum(onehot, axis=1, keepdims=True)
    cnt_ref[...] = carry_ref[...].astype(jnp.int32)


def _router_call(x, mod3, wr_hi, wr_lo, bias_col, ws_gu_bf, ws_dn_bf, *, tiles_per_batch):
    r, d = x.shape
    t = ROW_TILE
    nt = r // t
    row = lambda i: (i, 0)
    col = lambda i: (0, i)
    const = lambda i: (0, 0)
    return pl.pallas_call(
        _router_kernel,
        grid=(nt,),
        in_specs=[
            pl.BlockSpec((t, d), row),
            pl.BlockSpec((None, 1, 6 * d), lambda i: (_mod_row(i, tiles_per_batch), 0, 0)),
            pl.BlockSpec((N_EXPERTS, d), const),
            pl.BlockSpec((N_EXPERTS, d), const),
            pl.BlockSpec((N_EXPERTS, 1), const),
            pl.BlockSpec((d, 2 * EXPERT_HIDDEN), const),
            pl.BlockSpec((EXPERT_HIDDEN, d), const),
        ],
        out_specs=[
            pl.BlockSpec((t, PACK_W), row),
            pl.BlockSpec((TOP_K, t), col),
            pl.BlockSpec((TOP_K, t), col),
            pl.BlockSpec((TOP_K, t), col),
            pl.BlockSpec((N_EXPERTS, LANES), const),
            pl.BlockSpec((t, d), row),
        ],
        out_shape=[
            jax.ShapeDtypeStruct((r, PACK_W), jnp.int32),
            jax.ShapeDtypeStruct((TOP_K, r), jnp.int32),
            jax.ShapeDtypeStruct((TOP_K, r), F32),
            jax.ShapeDtypeStruct((TOP_K, r), jnp.int32),
            jax.ShapeDtypeStruct((N_EXPERTS, LANES), jnp.int32),
            jax.ShapeDtypeStruct((r, d), F32),
        ],
        scratch_shapes=[pltpu.VMEM((N_EXPERTS, LANES), F32)],
        compiler_params=_cparams("arbitrary"),
        name="router",
    )(x, mod3, wr_hi, wr_lo, bias_col, ws_gu_bf, ws_dn_bf)


def _dest_kernel(idx_ref, rank_ref, offs_ref, dest_ref):
    t = idx_ref.shape[1]
    ei = lax.broadcasted_iota(jnp.int32, (N_EXPERTS, t), 0)
    offs = offs_ref[...].astype(F32)
    for k in range(TOP_K):
        start = jnp.sum(jnp.where(ei == idx_ref[k:k + 1, :], offs, 0.0), axis=0, keepdims=True)
        dest_ref[k:k + 1, :] = start.astype(jnp.int32) + rank_ref[k:k + 1, :]


def _dest_call(idx, rank, offs_col):
    r = idx.shape[1]
    t = r // DEST_STEPS
    assert r % DEST_STEPS == 0 and t % LANES == 0
    col = lambda i: (0, i)
    return pl.pallas_call(
        _dest_kernel,
        grid=(r // t,),
        in_specs=[pl.BlockSpec((TOP_K, t), col), pl.BlockSpec((TOP_K, t), col),
                  pl.BlockSpec((N_EXPERTS, 1), lambda i: (0, 0))],
        out_specs=pl.BlockSpec((TOP_K, t), col),
        out_shape=jax.ShapeDtypeStruct((TOP_K, r), jnp.int32),
        compiler_params=_cparams("arbitrary"),
        name="moe_dest",
    )(idx, rank, offs_col)


def _sc_dispatch(tokp, dest_flat, pad_rows, n_sorted):
    r, width = tokp.shape
    win = SC_GATHER_WINDOW
    workers = SC_NUM_CORES * SC_NUM_SUBCORES
    token_windows = r // win
    n_pad_windows = pad_rows.shape[0] // win
    assert r % win == 0 and dest_flat.shape[0] == TOP_K * r and n_pad_windows % workers == 0
    windows_per_worker = -(-token_windows // workers)
    pads_per_worker = n_pad_windows // workers
    mesh = plsc.VectorSubcoreMesh(core_axis_name="core", subcore_axis_name="subcore", num_cores=SC_NUM_CORES,
                                  num_subcores=SC_NUM_SUBCORES)
    zero_rows = jnp.zeros((win, width), tokp.dtype)

    @functools.partial(
        pl.kernel, out_type=jax.ShapeDtypeStruct((n_sorted + SC_SPARE_ROWS, width), tokp.dtype), mesh=mesh,
        scratch_types=[pltpu.VMEM((win,), jnp.int32), pltpu.VMEM((win, width), tokp.dtype),
                       pltpu.SemaphoreType.DMA],
        name="moe_sc_dispatch")
    def dispatch_kernel(tok_hbm, dest_hbm, pad_hbm, zero_hbm, xs_hbm, idx_vmem, rows_vmem, sem):
        worker = lax.axis_index("subcore") * SC_NUM_CORES + lax.axis_index("core")

        @pl.loop(0, windows_per_worker)
        def _(j):
            window = j * workers + worker

            @pl.when(window < token_windows)
            def _():
                tok0 = window * win
                pltpu.sync_copy(tok_hbm.at[pl.ds(tok0, win)], rows_vmem)
                for k in range(TOP_K):
                    pltpu.sync_copy(dest_hbm.at[pl.ds(k * r + tok0, win)], idx_vmem)
                    pltpu.async_copy(rows_vmem, xs_hbm.at[idx_vmem], sem).wait()

        pltpu.sync_copy(zero_hbm, rows_vmem)

        @pl.loop(0, pads_per_worker)
        def _(j):
            off = (worker * pads_per_worker + j) * win
            pltpu.sync_copy(pad_hbm.at[pl.ds(off, win)], idx_vmem)
            pltpu.async_copy(rows_vmem, xs_hbm.at[idx_vmem], sem).wait()

    return dispatch_kernel(tokp, dest_flat, pad_rows, zero_rows)


def _expert_kernel(be_ref, nb_ref, ord_ref, ue_ref, nue_ref, xs_ref, wgu_hbm, wdn_hbm, ys_ref, wgu_f32, wdn_f32,
                   wgu_bf, wdn_bf, sems, *, layer):
    def weight_copies(o):
        slot = o % 2
        e = ue_ref[o]
        return (pltpu.make_async_copy(wgu_hbm.at[layer, e], wgu_f32.at[slot], sems.at[0, slot]),
                pltpu.make_async_copy(wdn_hbm.at[layer, e], wdn_f32.at[slot], sems.at[1, slot]))

    def start_weights(o):
        @pl.when(o < nue_ref[0])
        def _():
            for cp in weight_copies(o):
                cp.start()

    def one_block(sub, carry):
        j = pl.program_id(0) * EXPERT_BLOCKS_PER_STEP + sub

        @pl.when(j < nb_ref[0])
        def _():
            o = ord_ref[j]
            changed = jnp.logical_or(j == 0, be_ref[j] != be_ref[jnp.maximum(j - 1, 0)])

            @pl.when(j == 0)
            def _():
                start_weights(0)
                start_weights(1)

            @pl.when(changed)
            def _():
                for cp in weight_copies(o):
                    cp.wait()
                slot = o % 2
                wgu_bf[...] = wgu_f32[slot].astype(BF16)
                wdn_bf[...] = wdn_f32[slot].astype(BF16)
                start_weights(o + 2)

            rows = pl.ds(pl.multiple_of(sub * EXPERT_BLOCK, EXPERT_BLOCK), EXPERT_BLOCK)
            x_lo, x_hi = _unpack_bf16_pairs(xs_ref[rows, :])
            h = (jnp.dot(x_lo.astype(BF16), wgu_bf[0:PACK_W, :], preferred_element_type=F32)
                 + jnp.dot(x_hi.astype(BF16), wgu_bf[PACK_W:, :], preferred_element_type=F32))
            g, u = h[:, 0:EXPERT_HIDDEN], h[:, EXPERT_HIDDEN:]
            y = jnp.dot((g * _sigmoid(g) * u).astype(BF16), wdn_bf[...], preferred_element_type=F32)
            ys_ref[rows, :] = _pack_bf16_pairs(y)

        return carry

    lax.fori_loop(0, EXPERT_BLOCKS_PER_STEP, one_block, 0)


def _expert_call(block_expert, n_blocks_used, block_ordinal, used_expert, n_used_experts, xs, w_gu, w_dn, layer):
    n_rows = block_expert.shape[0] * EXPERT_BLOCK
    bm = EXPERT_BLOCK
    d = D_MODEL
    step_rows = bm * EXPERT_BLOCKS_PER_STEP
    assert n_rows % step_rows == 0
    used_step = lambda s, be, nb, od, ue, nue: (jnp.minimum(s, (nb[0] - 1) // EXPERT_BLOCKS_PER_STEP), 0)
    grid_spec = pltpu.PrefetchScalarGridSpec(
        num_scalar_prefetch=5,
        grid=(n_rows // step_rows,),
        in_specs=[
            pl.BlockSpec((step_rows, PACK_W), used_step),
            pl.BlockSpec(memory_space=pl.ANY),
            pl.BlockSpec(memory_space=pl.ANY),
        ],
        out_specs=pl.BlockSpec((step_rows, PACK_W), used_step),
        scratch_shapes=[
            pltpu.VMEM((2, d, 2 * EXPERT_HIDDEN), F32),
            pltpu.VMEM((2, EXPERT_HIDDEN, d), F32),
            pltpu.VMEM((d, 2 * EXPERT_HIDDEN), BF16),
            pltpu.VMEM((EXPERT_HIDDEN, d), BF16),
            pltpu.SemaphoreType.DMA((2, 2)),
        ],
    )
    return pl.pallas_call(
        functools.partial(_expert_kernel, layer=layer),
        grid_spec=grid_spec,
        out_shape=jax.ShapeDtypeStruct((n_rows, PACK_W), jnp.int32),
        compiler_params=_cparams("arbitrary"),
        name="moe_experts",
    )(block_expert, n_blocks_used, block_ordinal, used_expert, n_used_experts, xs, w_gu, w_dn)


def _sc_gather_rows(table, indices):
    n = indices.shape[0]
    width = table.shape[1]
    workers = SC_NUM_CORES * SC_NUM_SUBCORES
    assert n % SC_GATHER_WINDOW == 0
    n_windows = n // SC_GATHER_WINDOW
    mesh = plsc.VectorSubcoreMesh(core_axis_name="core", subcore_axis_name="subcore", num_cores=SC_NUM_CORES,
                                  num_subcores=SC_NUM_SUBCORES)

    @functools.partial(
        pl.kernel, out_type=jax.ShapeDtypeStruct((n, width), table.dtype), mesh=mesh,
        scratch_types=[pltpu.VMEM((SC_GATHER_WINDOW,), jnp.int32),
                       pltpu.VMEM((SC_GATHER_WINDOW, width), table.dtype),
                       pltpu.SemaphoreType.DMA],
        name="moe_sc_gather")
    def gather_kernel(table_hbm, idx_hbm, out_hbm, idx_vmem, rows_vmem, sem):
        worker = lax.axis_index("subcore") * SC_NUM_CORES + lax.axis_index("core")

        @pl.loop(0, -(-n_windows // workers))
        def _(j):
            window = j * workers + worker

            @pl.when(window < n_windows)
            def _():
                off = window * SC_GATHER_WINDOW
                pltpu.sync_copy(idx_hbm.at[pl.ds(off, SC_GATHER_WINDOW)], idx_vmem)
                pltpu.async_copy(table_hbm.at[idx_vmem], rows_vmem, sem).wait()
                pltpu.sync_copy(rows_vmem, out_hbm.at[pl.ds(off, SC_GATHER_WINDOW)])

    return gather_kernel(table, indices)


def _combine_kernel(*refs, alpha):
    y_refs = refs[:TOP_K]
    x_ref, fsh_ref, gate_ref, mod_ref, lng_ref, lnb_ref = refs[TOP_K:TOP_K + 6]
    o_ref = refs[-1]
    d = D_MODEL
    t = x_ref.shape[0]
    gate_rows = gate_ref[...]
    pad = jnp.zeros((LANES - TOP_K, t), F32)
    gate_cols = jnp.concatenate([gate_rows, pad], axis=0).T
    f_lo = fsh_ref[:, 0:PACK_W]
    f_hi = fsh_ref[:, PACK_W:]
    for k in range(TOP_K):
        y_lo, y_hi = _unpack_bf16_pairs(y_refs[k][...])
        f_lo = f_lo + gate_cols[:, k:k + 1] * y_lo
        f_hi = f_hi + gate_cols[:, k:k + 1] * y_hi
    f = jnp.concatenate([f_lo, f_hi], axis=1)
    z = alpha * x_ref[...] + mod_ref[:, 5 * d:6 * d] * f
    o_ref[...] = _layer_norm_rows(z) * lng_ref[...] + lnb_ref[...]


def _combine_call(y_tok, x, fsh, gate, mod3, ln_g, ln_b, prev_out, *, batch_index, tiles_per_batch, alpha,
                  drop_context):
    r, d = x.shape
    t = ROW_TILE
    n_batches = r // t // tiles_per_batch
    n_tiles = tiles_per_batch - 1 if drop_context else tiles_per_batch
    tile0 = batch_index * tiles_per_batch
    row = lambda i: (tile0 + i, 0)
    col = lambda i: (0, tile0 + i)
    const = lambda i: (0, 0)
    kern = functools.partial(_combine_kernel, alpha=alpha)
    y_specs = [pl.BlockSpec((t, PACK_W), functools.partial(lambda k, i: (k * tiles_per_batch + i, 0), k))
               for k in range(TOP_K)]
    in_specs = y_specs + [
        pl.BlockSpec((t, d), row),
        pl.BlockSpec((t, d), row),
        pl.BlockSpec((TOP_K, t), col),
        pl.BlockSpec((None, 1, 6 * d), lambda i: (_mod_row(tile0 + i, tiles_per_batch), 0, 0)),
        pl.BlockSpec((1, d), const),
        pl.BlockSpec((1, d), const),
    ]
    args = [y_tok] * TOP_K + [x, fsh, gate, mod3, ln_g, ln_b]
    aliases = {}
    if prev_out is not None:
        in_specs.append(pl.BlockSpec(memory_space=pl.ANY))
        args.append(prev_out)
        aliases = {len(args) - 1: 0}
    return pl.pallas_call(
        kern,
        grid=(n_tiles,),
        in_specs=in_specs,
        out_specs=pl.BlockSpec((t, d), lambda i: (batch_index * n_tiles + i, 0)),
        out_shape=jax.ShapeDtypeStruct((n_batches * n_tiles * t, d), F32),
        input_output_aliases=aliases,
        compiler_params=_cparams("arbitrary"),
        name="moe_combine",
    )(*args)


def _rope_tables(seq):
    rows = seq // GRID_W
    row = jnp.repeat(jnp.arange(rows, dtype=F32), GRID_W)
    col = jnp.tile(jnp.arange(GRID_W, dtype=F32), rows)
    nf = DA_DIM // 4
    freqs = ROPE_BASE ** (-jnp.arange(nf, dtype=F32) / nf)
    cr, sr = jnp.cos(row[:, None] * freqs), jnp.sin(row[:, None] * freqs)
    cc, sc = jnp.cos(col[:, None] * freqs), jnp.sin(col[:, None] * freqs)
    c64 = jnp.concatenate([cr, cr, cc, cc], axis=1)
    s64 = jnp.concatenate([-sr, sr, -sc, sc], axis=1)
    c = jnp.concatenate([jnp.tile(c64, (1, 2)), jnp.ones((CTX_LEN, LANES), F32)], axis=0)
    s = jnp.concatenate([jnp.tile(s64, (1, 2)), jnp.zeros((CTX_LEN, LANES), F32)], axis=0)
    return c, s


def kernel(x, c, ctx, c_ctx, w_mod, b_mod, w_in, w_out, diff_lambda, pool_w, pool_scale, ret_log_decay, ln_g, ln_b,
           w_router, router_bias, w_expert_gate_up, w_expert_down, w_shared_gate_up, w_shared_down):
    batch, seq, d = x.shape
    depth = w_mod.shape[0]
    assert d == D_MODEL and ctx.shape[1] == CTX_LEN == ROW_TILE and batch == 2
    assert seq % ROW_TILE == 0 and seq % GRID_W == 0 and w_in.shape[-1] == IN_WIDTH
    rows_per_batch = seq + CTX_LEN
    tiles_per_batch = rows_per_batch // ROW_TILE
    r = batch * rows_per_batch
    alpha = (2.0 * depth) ** 0.25

    xa = jnp.concatenate([x, ctx], axis=1).reshape(r, d)
    cvec = jnp.zeros((8, d), F32).at[0:batch].set(c).at[batch].set(c_ctx)
    mod_all = _mod_call(cvec, w_mod, b_mod)
    rope_c, rope_s = _rope_tables(seq)

    n_sorted = r * TOP_K + N_EXPERTS * EXPERT_BLOCK
    n_blocks = n_sorted // EXPERT_BLOCK

    for l in range(depth):
        lambda_init = 0.8 - 0.6 * math.exp(-0.3 * l)
        mod3 = mod_all[l].reshape(8, 1, 6 * d)
        lng = ln_g[l].reshape(2, 1, d)
        lnb = ln_b[l].reshape(2, 1, d)

        w_in_bf = w_in[l].astype(BF16)
        w_vt_bf = w_in_bf[:, QK_WIDTH:QK_WIDTH + DA_WIDTH].T
        qk, vda, u, rqkv, rg = _inproj_call(xa, mod3, w_in_bf, w_vt_bf, rope_c, rope_s, tiles_per_batch)
        da = _attn_call(diff_lambda[l], qk, vda, batch=batch, rows_per_batch=rows_per_batch, seq=seq,
                        lambda_init=lambda_init)
        o_f, o_b = _ret_call(ret_log_decay[l], rqkv, batch=batch, rows_per_batch=rows_per_batch, seq=seq)
        pool_bd = jnp.zeros((POOL_WIDTH, POOL_WIDTH), F32)
        for gi in range(len(POOL_WINDOWS)):
            sl = slice(gi * POOL_GROUP, (gi + 1) * POOL_GROUP)
            pool_bd = pool_bd.at[sl, sl].set(pool_w[l, gi])
        xa = _mixout_call(xa, da, u, o_f, o_b, rg, mod3, w_out[l].astype(BF16), pool_bd.astype(BF16),
                          pool_scale[l].reshape(1, POOL_WIDTH), lng[0], lnb[0],
                          tiles_per_batch=tiles_per_batch, seq=seq, alpha=alpha)

        wr_t = w_router[l].T
        wr_hi = wr_t.astype(BF16)
        wr_lo = (wr_t - wr_hi.astype(F32)).astype(BF16)
        tokp, idx, gate, rank, cnt, fsh = _router_call(
            xa, mod3, wr_hi, wr_lo, router_bias[l].reshape(N_EXPERTS, 1),
            w_shared_gate_up[l].astype(BF16), w_shared_down[l].astype(BF16), tiles_per_batch=tiles_per_batch)
        counts = cnt[:, 0]
        padded = (counts + EXPERT_BLOCK - 1) // EXPERT_BLOCK * EXPERT_BLOCK
        pad_end = jnp.cumsum(padded)
        offs = pad_end - padded
        expert_ids = jnp.arange(N_EXPERTS, dtype=jnp.int32)
        blk_row = jnp.arange(n_blocks, dtype=jnp.int32) * EXPERT_BLOCK
        block_expert = jnp.minimum(jnp.sum(pad_end[None, :] <= blk_row[:, None], axis=1), N_EXPERTS - 1)
        n_used = pad_end[-1:] // EXPERT_BLOCK
        used = counts > 0
        ordinal = jnp.cumsum(used) - 1
        hit = used[None, :] & (ordinal[None, :] == expert_ids[:, None])
        used_expert = jnp.sum(jnp.where(hit, expert_ids[None, :], 0), axis=1)
        n_used_experts = jnp.sum(used)[None]
        block_ordinal = ordinal[block_expert]
        slot = jnp.arange(EXPERT_BLOCK, dtype=jnp.int32)[None, :]
        first_pad = (padded - EXPERT_BLOCK)[:, None] + slot
        is_pad = (first_pad >= counts[:, None]) & (padded[:, None] > 0)
        spare = n_sorted + jnp.arange(N_EXPERTS * EXPERT_BLOCK, dtype=jnp.int32).reshape(N_EXPERTS, EXPERT_BLOCK)
        pad_rows = jnp.where(is_pad, offs[:, None] + first_pad, spare).reshape(N_EXPERTS * EXPERT_BLOCK)
        i32 = lambda a: a.astype(jnp.int32)

        dest = _dest_call(idx, rank, i32(offs).reshape(N_EXPERTS, 1))
        dest_flat = dest.reshape(TOP_K * r)
        xs = _sc_dispatch(tokp, dest_flat, i32(pad_rows), n_sorted)
        ys = _expert_call(i32(block_expert), i32(n_used), i32(block_ordinal), i32(used_expert),
                          i32(n_used_experts), xs, w_expert_gate_up, w_expert_down, l)
        x_new = None
        for b in range(batch):
            dest_b = dest[:, b * rows_per_batch:(b + 1) * rows_per_batch].reshape(TOP_K * rows_per_batch)
            y_tok = _sc_gather_rows(ys, dest_b)
            x_new = _combine_call(y_tok, xa, fsh, gate, mod3, lng[1], lnb[1], x_new, batch_index=b,
                                  tiles_per_batch=tiles_per_batch, alpha=alpha, drop_context=(l == depth - 1))
        xa = x_new

    return xa.reshape(batch, seq, d)
```

```python
import functools
import math

import jax
import jax.numpy as jnp
from jax import lax
from jax.experimental import pallas as pl
from jax.experimental.pallas import tpu as pltpu
from jax.experimental.pallas import tpu_sc as plsc

F32 = jnp.float32
BF16 = jnp.bfloat16
HIGHEST = lax.Precision.HIGHEST

D_MODEL = 1024
CTX_LEN = 256
GRID_W = 64
DA_HEADS = 4
DA_DIM = 64
DA_VDIM = 2 * DA_DIM
DA_WIDTH = DA_HEADS * DA_VDIM
ROPE_BASE = 10000.0
POOL_WINDOWS = (2, 4, 8, 16)
POOL_GROUP = 64
POOL_WIDTH = len(POOL_WINDOWS) * POOL_GROUP
POOL_HALO = 8
RET_HEADS = 4
RET_DK = 64
RET_WIDTH = RET_HEADS * RET_DK
RET_CHUNK = 128
QK_WIDTH = 2 * DA_HEADS * 2 * DA_DIM
IN_WIDTH = QK_WIDTH + DA_WIDTH + POOL_WIDTH + 4 * RET_WIDTH
N_EXPERTS = 256
TOP_K = 8
N_GROUPS = 8
GROUP_SIZE = N_EXPERTS // N_GROUPS
TOPK_GROUPS = 4
EXPERT_HIDDEN = 256
ROUTED_SCALE = 2.5
LN_EPS = 1e-6
RMS_EPS = 1e-5

LANES = 128
ROW_TILE = 256
MOD_COL_TILE = 1536
DEST_STEPS = 4
ATTN_Q_TILE = 256
ATTN_K_CHUNK = 256
ATTN_UNROLL = 16
SC_NUM_CORES = 2
SC_NUM_SUBCORES = 16
SC_GATHER_WINDOW = 128
EXPERT_BLOCK = 256
EXPERT_BLOCKS_PER_STEP = 4
SC_SPARE_ROWS = N_EXPERTS * EXPERT_BLOCK
PACK_W = D_MODEL // 2
VMEM_LIMIT = 56 * 1024 * 1024


def _cparams(*sem):
    return pltpu.CompilerParams(dimension_semantics=sem, vmem_limit_bytes=VMEM_LIMIT)


def _sigmoid(x):
    return 1.0 / (1.0 + jnp.exp(-x))


def _layer_norm_rows(x):
    mu = jnp.mean(x, axis=-1, keepdims=True)
    xc = x - mu
    var = jnp.mean(xc * xc, axis=-1, keepdims=True)
    return xc * lax.rsqrt(var + LN_EPS)


def _pack_bf16_pairs(x):
    half = x.shape[1] // 2
    bits = pltpu.bitcast(x.astype(BF16).astype(F32), jnp.uint32)
    word = lax.shift_right_logical(bits[:, 0:half], jnp.uint32(16)) | (bits[:, half:] & jnp.uint32(0xFFFF0000))
    return pltpu.bitcast(word, jnp.int32)


def _unpack_bf16_pairs(packed):
    word = pltpu.bitcast(packed, jnp.uint32)
    lo = pltpu.bitcast(lax.shift_left(word, jnp.uint32(16)), F32)
    hi = pltpu.bitcast(word & jnp.uint32(0xFFFF0000), F32)
    return lo, hi


def _mod_row(i, tiles_per_batch):
    return jnp.where(i % tiles_per_batch == tiles_per_batch - 1, 2, i // tiles_per_batch)


def _mod_kernel(c_ref, w_ref, b_ref, o_ref):
    c = c_ref[...]
    s = c * _sigmoid(c)
    o_ref[...] = jnp.dot(s, w_ref[...], precision=HIGHEST, preferred_element_type=F32) + b_ref[...]


def _mod_call(cvec, w_mod, b_mod):
    depth, d, n = w_mod.shape
    tn = MOD_COL_TILE
    assert n % tn == 0
    return pl.pallas_call(
        _mod_kernel,
        grid=(depth, n // tn),
        in_specs=[
            pl.BlockSpec((8, d), lambda l, j: (0, 0)),
            pl.BlockSpec((None, d, tn), lambda l, j: (l, 0, j)),
            pl.BlockSpec((None, 1, tn), lambda l, j: (l, 0, j)),
        ],
        out_specs=pl.BlockSpec((None, 8, tn), lambda l, j: (l, 0, j)),
        out_shape=jax.ShapeDtypeStruct((depth, 8, n), F32),
        compiler_params=_cparams("arbitrary", "arbitrary"),
        name="mod",
    )(cvec, w_mod, b_mod.reshape(depth, 1, n))


def _inproj_kernel(x_ref, mod_ref, w_ref, wvt_ref, ct_ref, st_ref, qk_ref, vt_ref, u_ref, r_ref, g_ref):
    d = D_MODEL
    xn = _layer_norm_rows(x_ref[...])
    h = (xn * (1.0 + mod_ref[:, d:2 * d]) + mod_ref[:, 0:d]).astype(BF16)

    a = jnp.dot(h, w_ref[:, 0:QK_WIDTH], preferred_element_type=F32)
    lane = lax.broadcasted_iota(jnp.int32, (a.shape[0], LANES), 1)
    first_half = (lane % 32) < 16
    ct = ct_ref[...]
    st = st_ref[...]
    for s in range(QK_WIDTH // LANES):
        blk = a[:, s * LANES:(s + 1) * LANES]
        partner = jnp.where(first_half, pltpu.roll(blk, LANES - 16, 1), pltpu.roll(blk, 16, 1))
        rot = blk * ct + partner * st
        if s < QK_WIDTH // LANES // 2:
            rot = rot * (DA_DIM ** -0.5 * math.log2(math.e))
        qk_ref[:, s * LANES:(s + 1) * LANES] = rot.astype(BF16)

    vt_ref[...] = lax.dot_general(wvt_ref[...], h, (((1,), (1,)), ((), ())),
                                  preferred_element_type=F32).astype(BF16)
    o = QK_WIDTH + DA_WIDTH
    u_ref[...] = jnp.dot(h, w_ref[:, o:o + POOL_WIDTH], preferred_element_type=F32)
    o += POOL_WIDTH
    r = jnp.dot(h, w_ref[:, o:o + 3 * RET_WIDTH], preferred_element_type=F32)
    r_ref[:, 0:RET_WIDTH] = r[:, 0:RET_WIDTH].astype(BF16)
    r_ref[:, RET_WIDTH:2 * RET_WIDTH] = (r[:, RET_WIDTH:2 * RET_WIDTH] * (RET_DK ** -0.5)).astype(BF16)
    r_ref[:, 2 * RET_WIDTH:] = r[:, 2 * RET_WIDTH:].astype(BF16)
    o += 3 * RET_WIDTH
    g_ref[...] = jnp.dot(h, w_ref[:, o:o + RET_WIDTH], preferred_element_type=F32)


def _inproj_call(x, mod3, w_in_bf, w_vt_bf, rope_c, rope_s, tiles_per_batch):
    r, d = x.shape
    t = ROW_TILE
    nt = r // t
    row = lambda i: (i, 0)
    return pl.pallas_call(
        _inproj_kernel,
        grid=(nt,),
        in_specs=[
            pl.BlockSpec((t, d), row),
            pl.BlockSpec((None, 1, 6 * d), lambda i: (_mod_row(i, tiles_per_batch), 0, 0)),
            pl.BlockSpec((d, IN_WIDTH), lambda i: (0, 0)),
            pl.BlockSpec((DA_WIDTH, d), lambda i: (0, 0)),
            pl.BlockSpec((t, LANES), lambda i: (i % tiles_per_batch, 0)),
            pl.BlockSpec((t, LANES), lambda i: (i % tiles_per_batch, 0)),
        ],
        out_specs=[
            pl.BlockSpec((t, QK_WIDTH), row),
            pl.BlockSpec((DA_WIDTH, t), lambda i: (0, i)),
            pl.BlockSpec((t, POOL_WIDTH), row),
            pl.BlockSpec((t, 3 * RET_WIDTH), row),
            pl.BlockSpec((t, RET_WIDTH), row),
        ],
        out_shape=[
            jax.ShapeDtypeStruct((r, QK_WIDTH), BF16),
            jax.ShapeDtypeStruct((DA_WIDTH, r), BF16),
            jax.ShapeDtypeStruct((r, POOL_WIDTH), F32),
            jax.ShapeDtypeStruct((r, 3 * RET_WIDTH), BF16),
            jax.ShapeDtypeStruct((r, RET_WIDTH), F32),
        ],
        compiler_params=_cparams("arbitrary"),
        name="inproj",
    )(x, mod3, w_in_bf, w_vt_bf, rope_c, rope_s)


def _attn_kernel(lam_ref, q_ref, k_ref, vt_ref, o_ref, *s_refs, k_chunk, seq, lambda_init):
    mq = ATTN_Q_TILE
    n_tiles = (seq + CTX_LEN) // mq
    n_chunks = (seq + CTX_LEN) // k_chunk
    last = n_chunks - 1
    n_iters = last // ATTN_UNROLL
    neg_inf = jnp.full((1, 2 * mq), -jnp.inf, F32)
    acc_zero = jnp.zeros((DA_VDIM + 16, 2 * mq), F32)
    ones_rows = jnp.where(lax.broadcasted_iota(jnp.int32, (16, k_chunk), 0) == 0, 1.0, 0.0).astype(BF16)

    def tile_rows(i):
        return pl.ds(pl.multiple_of(i * mq, mq), mq)

    def q_transposed(i):
        q = q_ref[tile_rows(i), :]
        lane = lax.broadcasted_iota(jnp.int32, q.shape, 1)
        zero = jnp.zeros_like(q)
        q2 = jnp.concatenate([jnp.where(lane < DA_DIM, q, zero), jnp.where(lane >= DA_DIM, q, zero)], axis=0)
        return q2.astype(F32).T.astype(BF16)

    def score_chunk(s_ref, c, qt, m):
        off = pl.multiple_of(c * k_chunk, k_chunk)
        s = jnp.dot(k_ref[pl.ds(off, k_chunk), :], qt, preferred_element_type=F32)
        s_ref[c] = s
        return jnp.maximum(m, jnp.max(s, axis=0, keepdims=True))

    def score_chunk_pair(s_ref, c, qt, m):
        off = pl.multiple_of(c * k_chunk, 2 * k_chunk)
        s = jnp.dot(k_ref[pl.ds(off, 2 * k_chunk), :], qt, preferred_element_type=F32)
        s_ref[c] = s[0:k_chunk]
        s_ref[c + 1] = s[k_chunk:]
        return jnp.maximum(m, jnp.max(s, axis=0, keepdims=True))

    def value_chunk(s_ref, c, m, acc):
        off = pl.multiple_of(c * k_chunk, k_chunk)
        vt = jnp.concatenate([vt_ref[:, pl.ds(off, k_chunk)], ones_rows], axis=0)
        p = jnp.exp2((s_ref[c] - m).astype(BF16))
        return acc + jnp.dot(vt, p, preferred_element_type=F32)

    def finish(i, acc):
        l0, l1 = acc[DA_VDIM:DA_VDIM + 1, 0:mq], acc[DA_VDIM:DA_VDIM + 1, mq:]
        a0, a1 = acc[0:DA_VDIM, 0:mq], acc[0:DA_VDIM, mq:]
        lv = lam_ref[...]
        lam = (jnp.exp(jnp.sum(lv[0:1] * lv[1:2], axis=-1, keepdims=True))
               - jnp.exp(jnp.sum(lv[2:3] * lv[3:4], axis=-1, keepdims=True)) + lambda_init)
        o = a0 / l0 - lam * (a1 / l1)
        o = o * lax.rsqrt(jnp.mean(o * o, axis=0, keepdims=True) + RMS_EPS) * (1.0 - lambda_init)
        o_ref[tile_rows(i), :] = o.T.astype(BF16)

    def scores_only(s_ref, qt):
        def body(it, m):
            for u in range(ATTN_UNROLL):
                m = score_chunk(s_ref, it * ATTN_UNROLL + u, qt, m)
            return m
        return score_chunk(s_ref, last, qt, lax.fori_loop(0, n_iters, body, neg_inf))

    def values_only(s_ref, m):
        def body(it, acc):
            for u in range(ATTN_UNROLL):
                acc = value_chunk(s_ref, it * ATTN_UNROLL + u, m, acc)
            return acc
        return value_chunk(s_ref, last, m, lax.fori_loop(0, n_iters, body, acc_zero))

    def fused_tile(i, m_prev, s_cur, s_prev):
        qt = q_transposed(i)

        def body(it, carry):
            m, acc = carry
            for u in range(0, ATTN_UNROLL, 2):
                c = it * ATTN_UNROLL + u
                m = score_chunk_pair(s_cur, c, qt, m)
                acc = value_chunk(s_prev, c, m_prev, acc)
                acc = value_chunk(s_prev, c + 1, m_prev, acc)
            return m, acc

        m, acc = lax.fori_loop(0, n_iters, body, (neg_inf, acc_zero))
        m = score_chunk(s_cur, last, qt, m)
        finish(i - 1, value_chunk(s_prev, last, m_prev, acc))
        return m

    s_even, s_odd = s_refs
    ctx_tile = n_tiles - 1
    assert ctx_tile % 2 == 0 and ctx_tile >= 2
    m = scores_only(s_even, q_transposed(0))

    def tile_pair(p, m):
        m = fused_tile(2 * p + 1, m, s_odd, s_even)
        return fused_tile(2 * p + 2, m, s_even, s_odd)

    m = lax.fori_loop(0, (ctx_tile - 2) // 2, tile_pair, m)
    m = fused_tile(ctx_tile - 1, m, s_odd, s_even)
    m_ctx = score_chunk(s_even, last, q_transposed(ctx_tile), neg_inf)
    finish(ctx_tile - 1, values_only(s_odd, m))
    finish(ctx_tile, value_chunk(s_even, last, m_ctx, acc_zero))


def _attn_call(lam_vec, qk, vda, *, batch, rows_per_batch, seq, lambda_init):
    tq = ATTN_Q_TILE
    assert seq % (ATTN_K_CHUNK * ATTN_UNROLL) == 0 and rows_per_batch - seq == CTX_LEN == tq == ATTN_K_CHUNK
    nq = rows_per_batch // tq
    kern = functools.partial(_attn_kernel, k_chunk=ATTN_K_CHUNK, seq=seq, lambda_init=lambda_init)
    return pl.pallas_call(
        kern,
        grid=(batch, DA_HEADS),
        in_specs=[
            pl.BlockSpec((4, DA_DIM), lambda b, h: (0, 0)),
            pl.BlockSpec((rows_per_batch, DA_VDIM), lambda b, h: (b, h)),
            pl.BlockSpec((rows_per_batch, DA_VDIM), lambda b, h: (b, DA_HEADS + h)),
            pl.BlockSpec((DA_VDIM, rows_per_batch), lambda b, h: (h, b)),
        ],
        out_specs=pl.BlockSpec((rows_per_batch, DA_VDIM), lambda b, h: (b, h)),
        out_shape=jax.ShapeDtypeStruct((qk.shape[0], DA_WIDTH), BF16),
        scratch_shapes=[pltpu.VMEM((rows_per_batch // ATTN_K_CHUNK, ATTN_K_CHUNK, 2 * tq), F32)] * 2,
        compiler_params=_cparams("arbitrary", "arbitrary"),
        name="diff_attn",
    )(lam_vec, qk, qk, vda)


def _ret_kernel(ld_ref, f_ref, b_ref, of_ref, ob_ref, dm_ref, qd_ref, kd_ref, cd_ref, st_ref):
    c = pl.program_id(1)
    ch = RET_CHUNK
    w = RET_WIDTH
    lane_head = lax.broadcasted_iota(jnp.int32, (1, w), 1) // RET_DK

    @pl.when(c == 0)
    def _():
        st_ref[...] = jnp.zeros_like(st_ref)
        ri = lax.broadcasted_iota(jnp.int32, (ch, ch), 0)
        ci = lax.broadcasted_iota(jnp.int32, (ch, ch), 1)
        rowf = lax.broadcasted_iota(jnp.int32, (ch, w), 0).astype(F32)
        for d in range(2):
            lg_lane = jnp.zeros((1, w), F32)
            for hh in range(RET_HEADS):
                lg = -jnp.exp(jnp.full((1, 1), ld_ref[d, hh], F32))
                lg_lane = jnp.where(lane_head == hh, lg, lg_lane)
                dist = ((ri - ci) if d == 0 else (ci - ri)).astype(F32)
                dm_ref[d, hh] = jnp.where(dist >= 0, jnp.exp(dist * lg), 0.0)
            if d == 0:
                qd_ref[d] = jnp.exp((rowf + 1.0) * lg_lane)
                kd_ref[d] = jnp.exp((ch - 1.0 - rowf) * lg_lane)
            else:
                qd_ref[d] = jnp.exp((ch - rowf) * lg_lane)
                kd_ref[d] = jnp.exp(rowf * lg_lane)
            cd_ref[d] = jnp.exp(float(ch) * lg_lane)

    rblk = lax.broadcasted_iota(jnp.int32, (w, w), 0) // RET_DK
    cblk = lax.broadcasted_iota(jnp.int32, (w, w), 1) // RET_DK
    for d, (src, dst) in enumerate(((f_ref, of_ref), (b_ref, ob_ref))):
        q = src[:, 0:w]
        k = src[:, w:2 * w]
        v = src[:, 2 * w:3 * w]
        st = st_ref[d]
        o = jnp.dot((q.astype(F32) * qd_ref[d]).astype(BF16), st.astype(BF16), preferred_element_type=F32)
        for hh in range(RET_HEADS):
            in_head = lane_head == hh
            qm = jnp.where(in_head, q, jnp.zeros_like(q))
            s = lax.dot_general(qm, k, (((1,), (1,)), ((), ())), preferred_element_type=F32)
            intra = (s * dm_ref[d, hh]).astype(BF16)
            o = o + jnp.where(in_head, jnp.dot(intra, v, preferred_element_type=F32), 0.0)
        dst[...] = o
        kk_t = (k.astype(F32) * kd_ref[d]).T.astype(BF16)
        upd = jnp.dot(kk_t, v, preferred_element_type=F32)
        st_ref[d] = jnp.where(rblk == cblk, st * cd_ref[d] + upd, 0.0)


def _ret_call(log_decay, rqkv, *, batch, rows_per_batch, seq):
    ch = RET_CHUNK
    nc = rows_per_batch // ch
    n_lat = seq // ch
    n_ctx = nc - n_lat

    def fwd(b, c):
        return (b * nc + jnp.where(c < n_ctx, n_lat + c, c - n_ctx), 0)

    def bwd(b, c):
        return (b * nc + nc - 1 - c, 0)

    w = RET_WIDTH
    return pl.pallas_call(
        _ret_kernel,
        grid=(batch, nc),
        in_specs=[
            pl.BlockSpec(memory_space=pltpu.SMEM),
            pl.BlockSpec((ch, 3 * w), fwd),
            pl.BlockSpec((ch, 3 * w), bwd),
        ],
        out_specs=[pl.BlockSpec((ch, w), fwd), pl.BlockSpec((ch, w), bwd)],
        out_shape=[jax.ShapeDtypeStruct((rqkv.shape[0], w), F32)] * 2,
        scratch_shapes=[
            pltpu.VMEM((2, RET_HEADS, ch, ch), F32),
            pltpu.VMEM((2, ch, w), F32),
            pltpu.VMEM((2, ch, w), F32),
            pltpu.VMEM((2, 1, w), F32),
            pltpu.VMEM((2, w, w), F32),
        ],
        compiler_params=_cparams("arbitrary", "arbitrary"),
        name="retention",
    )(log_decay, rqkv, rqkv)


def _mixout_kernel(x_ref, da_ref, u_ref, up_ref, un_ref, of_ref, ob_ref, rg_ref, mod_ref, wo_ref, pw_ref,
                   ps_ref, lng_ref, lnb_ref, o_ref, *, tiles_per_batch, seq, alpha):
    d = D_MODEL
    t = x_ref.shape[0]
    i = pl.program_id(0)
    j = i % tiles_per_batch
    is_ctx = j == tiles_per_batch - 1
    stream_len = jnp.where(is_ctx, CTX_LEN, seq)
    p0 = jnp.where(is_ctx, 0, j * t)

    u = u_ref[...]
    prev = jnp.where(p0 > 0, up_ref[...], 0.0)
    nxt = jnp.where(p0 + t < stream_len, un_ref[...], 0.0)
    ext = jnp.concatenate([prev, u, nxt], axis=0)
    n = t + 2 * POOL_HALO
    a2 = ext + pltpu.roll(ext, 1, 0)
    a4 = pltpu.roll(a2, 1, 0) + pltpu.roll(a2, n - 1, 0)
    a8 = pltpu.roll(a4, 2, 0) + pltpu.roll(a4, n - 2, 0)
    a16 = pltpu.roll(a8, 4, 0) + pltpu.roll(a8, n - 4, 0)
    pos = p0 + lax.broadcasted_iota(jnp.int32, (t, POOL_WIDTH), 0)
    group = lax.broadcasted_iota(jnp.int32, (1, POOL_WIDTH), 1) // POOL_GROUP
    mean = jnp.zeros((t, POOL_WIDTH), F32)
    for gi, (wnd, asum) in enumerate(zip(POOL_WINDOWS, (a2, a4, a8, a16))):
        cnt = jnp.minimum(pos + wnd // 2, stream_len) - jnp.maximum(pos - wnd // 2, 0)
        mean = jnp.where(group == gi, asum[POOL_HALO:POOL_HALO + t] / cnt.astype(F32), mean)
    pool = jnp.dot((mean - u).astype(BF16), pw_ref[...], preferred_element_type=F32) * ps_ref[...]

    o = of_ref[...] + ob_ref[...]
    head = lax.broadcasted_iota(jnp.int32, (1, RET_WIDTH), 1) // RET_DK

    def head_mean(val):
        out = jnp.zeros_like(val)
        for hh in range(RET_HEADS):
            m = jnp.sum(jnp.where(head == hh, val, 0.0), axis=-1, keepdims=True) * (1.0 / RET_DK)
            out = jnp.where(head == hh, m, out)
        return out

    oc = o - head_mean(o)
    rn = oc * lax.rsqrt(head_mean(oc * oc) + LN_EPS)
    g = rg_ref[...]
    ret = rn * (g * _sigmoid(g))

    y = jnp.dot(da_ref[...], wo_ref[0:DA_WIDTH, :], preferred_element_type=F32)
    y = y + jnp.dot(pool.astype(BF16), wo_ref[DA_WIDTH:DA_WIDTH + POOL_WIDTH, :], preferred_element_type=F32)
    y = y + jnp.dot(ret.astype(BF16), wo_ref[DA_WIDTH + POOL_WIDTH:, :], preferred_element_type=F32)
    z = alpha * x_ref[...] + mod_ref[:, 2 * d:3 * d] * y
    o_ref[...] = _layer_norm_rows(z) * lng_ref[...] + lnb_ref[...]


def _mixout_call(x, da, u, o_f, o_b, rg, mod3, w_out_bf, pool_bd, pool_scale, ln_g, ln_b, *, tiles_per_batch, seq,
                 alpha):
    r, d = x.shape
    t = ROW_TILE
    nt = r // t
    hb = t // POOL_HALO
    n_halo_blocks = r // POOL_HALO
    row = lambda i: (i, 0)
    const = lambda i: (0, 0)
    kern = functools.partial(_mixout_kernel, tiles_per_batch=tiles_per_batch, seq=seq, alpha=alpha)
    return pl.pallas_call(
        kern,
        grid=(nt,),
        in_specs=[
            pl.BlockSpec((t, d), row),
            pl.BlockSpec((t, DA_WIDTH), row),
            pl.BlockSpec((t, POOL_WIDTH), row),
            pl.BlockSpec((POOL_HALO, POOL_WIDTH), lambda i: (jnp.maximum(i * hb - 1, 0), 0)),
            pl.BlockSpec((POOL_HALO, POOL_WIDTH), lambda i: (jnp.minimum((i + 1) * hb, n_halo_blocks - 1), 0)),
            pl.BlockSpec((t, RET_WIDTH), row),
            pl.BlockSpec((t, RET_WIDTH), row),
            pl.BlockSpec((t, RET_WIDTH), row),
            pl.BlockSpec((None, 1, 6 * d), lambda i: (_mod_row(i, tiles_per_batch), 0, 0)),
            pl.BlockSpec((d, d), const),
            pl.BlockSpec((POOL_WIDTH, POOL_WIDTH), const),
            pl.BlockSpec((1, POOL_WIDTH), const),
            pl.BlockSpec((1, d), const),
            pl.BlockSpec((1, d), const),
        ],
        out_specs=pl.BlockSpec((t, d), row),
        out_shape=jax.ShapeDtypeStruct((r, d), F32),
        compiler_params=_cparams("arbitrary"),
        name="mixer_out",
    )(x, da, u, u, u, o_f, o_b, rg, mod3, w_out_bf, pool_bd, pool_scale, ln_g, ln_b)


def _router_kernel(x_ref, mod_ref, wrh_ref, wrl_ref, bias_ref, wsgu_ref, wsdn_ref,
                   tokp_ref, idx_ref, gate_ref, rank_ref, cnt_ref, fsh_ref, carry_ref):
    d = D_MODEL
    t = x_ref.shape[0]
    ne = N_EXPERTS
    neg = -jnp.inf

    @pl.when(pl.program_id(0) == 0)
    def _():
        carry_ref[...] = jnp.zeros_like(carry_ref)

    tok = _layer_norm_rows(x_ref[...]) * (1.0 + mod_ref[:, 4 * d:5 * d]) + mod_ref[:, 3 * d:4 * d]
    tok_hi = tok.astype(BF16)
    tok_lo = (tok - tok_hi.astype(F32)).astype(BF16)

    tokp_ref[...] = _pack_bf16_pairs(tok)

    hs = jnp.dot(tok_hi, wsgu_ref[...], preferred_element_type=F32)
    gs, us = hs[:, 0:EXPERT_HIDDEN], hs[:, EXPERT_HIDDEN:]
    fsh_ref[...] = jnp.dot((gs * _sigmoid(gs) * us).astype(BF16), wsdn_ref[...], preferred_element_type=F32)

    nt_dims = (((1,), (1,)), ((), ()))
    logits = (lax.dot_general(wrh_ref[...], tok_hi, nt_dims, preferred_element_type=F32)
              + lax.dot_general(wrh_ref[...], tok_lo, nt_dims, preferred_element_type=F32)
              + lax.dot_general(wrl_ref[...], tok_hi, nt_dims, preferred_element_type=F32))
    scores = _sigmoid(logits)
    biased = scores + bias_ref[...]

    gidx = lax.broadcasted_iota(jnp.int32, (GROUP_SIZE, t), 0)
    blocks, gscores = [], []
    for g in range(N_GROUPS):
        blk = biased[g * GROUP_SIZE:(g + 1) * GROUP_SIZE, :]
        m1 = jnp.max(blk, axis=0, keepdims=True)
        first = jnp.min(jnp.where(blk == m1, gidx, GROUP_SIZE), axis=0, keepdims=True)
        m2 = jnp.max(jnp.where(gidx == first, neg, blk), axis=0, keepdims=True)
        blocks.append(blk)
        gscores.append(m1 + m2)

    keep = [jnp.zeros((1, t), F32) for _ in range(N_GROUPS)]
    for _ in range(TOPK_GROUPS):
        m = gscores[0]
        for gs_ in gscores[1:]:
            m = jnp.maximum(m, gs_)
        found = jnp.zeros((1, t), F32)
        for g in range(N_GROUPS):
            hit = jnp.where(gscores[g] == m, 1.0 - found, 0.0)
            found = found + hit
            keep[g] = keep[g] + hit
            gscores[g] = jnp.where(hit > 0.0, neg, gscores[g])
    masked = jnp.concatenate([jnp.where(keep[g] > 0.0, blocks[g], neg) for g in range(N_GROUPS)], axis=0)

    ei = lax.broadcasted_iota(jnp.int32, (ne, t), 0)
    cur = masked
    onehot = jnp.zeros((ne, t), F32)
    idxs, gates = [], []
    for _ in range(TOP_K):
        m = jnp.max(cur, axis=0, keepdims=True)
        ii = jnp.min(jnp.where(cur == m, ei, ne), axis=0, keepdims=True)
        sel = ei == ii
        idxs.append(ii)
        gates.append(jnp.sum(jnp.where(sel, scores, 0.0), axis=0, keepdims=True))
        onehot = jnp.where(sel, 1.0, onehot)
        cur = jnp.where(sel, neg, cur)
    gsum = gates[0]
    for gk in gates[1:]:
        gsum = gsum + gk
    for k in range(TOP_K):
        idx_ref[k:k + 1, :] = idxs[k]
        gate_ref[k:k + 1, :] = gates[k] / gsum * ROUTED_SCALE

    ti = lax.broadcasted_iota(jnp.int32, (t, t), 0)
    tj = lax.broadcasted_iota(jnp.int32, (t, t), 1)
    before = jnp.where(ti < tj, 1.0, 0.0).astype(BF16)
    prefix = jnp.dot(onehot.astype(BF16), before, preferred_element_type=F32) + carry_ref[:, 0:1]
    for k in range(TOP_K):
        rank_k = jnp.sum(jnp.where(ei == idxs[k], prefix, 0.0), axis=0, keepdims=True)
        rank_ref[k:k + 1, :] = rank_k.astype(jnp.int32)
    carry_ref[...] = carry_ref[...] + jnp.sum(onehot, axis=1, keepdims=True)
    cnt_ref[...] = carry_ref[...].astype(jnp.int32)


def _router_call(x, mod3, wr_hi, wr_lo, bias_col, ws_gu_bf, ws_dn_bf, *, tiles_per_batch):
    r, d = x.shape
    t = ROW_TILE
    nt = r // t
    row = lambda i: (i, 0)
    col = lambda i: (0, i)
    const = lambda i: (0, 0)
    return pl.pallas_call(
        _router_kernel,
        grid=(nt,),
        in_specs=[
            pl.BlockSpec((t, d), row),
            pl.BlockSpec((None, 1, 6 * d), lambda i: (_mod_row(i, tiles_per_batch), 0, 0)),
            pl.BlockSpec((N_EXPERTS, d), const),
            pl.BlockSpec((N_EXPERTS, d), const),
            pl.BlockSpec((N_EXPERTS, 1), const),
            pl.BlockSpec((d, 2 * EXPERT_HIDDEN), const),
            pl.BlockSpec((EXPERT_HIDDEN, d), const),
        ],
        out_specs=[
            pl.BlockSpec((t, PACK_W), row),
            pl.BlockSpec((TOP_K, t), col),
            pl.BlockSpec((TOP_K, t), col),
            pl.BlockSpec((TOP_K, t), col),
            pl.BlockSpec((N_EXPERTS, LANES), const),
            pl.BlockSpec((t, d), row),
        ],
        out_shape=[
            jax.ShapeDtypeStruct((r, PACK_W), jnp.int32),
            jax.ShapeDtypeStruct((TOP_K, r), jnp.int32),
            jax.ShapeDtypeStruct((TOP_K, r), F32),
            jax.ShapeDtypeStruct((TOP_K, r), jnp.int32),
            jax.ShapeDtypeStruct((N_EXPERTS, LANES), jnp.int32),
            jax.ShapeDtypeStruct((r, d), F32),
        ],
        scratch_shapes=[pltpu.VMEM((N_EXPERTS, LANES), F32)],
        compiler_params=_cparams("arbitrary"),
        name="router",
    )(x, mod3, wr_hi, wr_lo, bias_col, ws_gu_bf, ws_dn_bf)


def _dest_kernel(idx_ref, rank_ref, offs_ref, dest_ref):
    t = idx_ref.shape[1]
    ei = lax.broadcasted_iota(jnp.int32, (N_EXPERTS, t), 0)
    offs = offs_ref[...].astype(F32)
    for k in range(TOP_K):
        start = jnp.sum(jnp.where(ei == idx_ref[k:k + 1, :], offs, 0.0), axis=0, keepdims=True)
        dest_ref[k:k + 1, :] = start.astype(jnp.int32) + rank_ref[k:k + 1, :]


def _dest_call(idx, rank, offs_col):
    r = idx.shape[1]
    t = r // DEST_STEPS
    assert r % DEST_STEPS == 0 and t % LANES == 0
    col = lambda i: (0, i)
    return pl.pallas_call(
        _dest_kernel,
        grid=(r // t,),
        in_specs=[pl.BlockSpec((TOP_K, t), col), pl.BlockSpec((TOP_K, t), col),
                  pl.BlockSpec((N_EXPERTS, 1), lambda i: (0, 0))],
        out_specs=pl.BlockSpec((TOP_K, t), col),
        out_shape=jax.ShapeDtypeStruct((TOP_K, r), jnp.int32),
        compiler_params=_cparams("arbitrary"),
        name="moe_dest",
    )(idx, rank, offs_col)


def _sc_dispatch(tokp, dest_flat, pad_rows, n_sorted):
    r, width = tokp.shape
    win = SC_GATHER_WINDOW
    workers = SC_NUM_CORES * SC_NUM_SUBCORES
    token_windows = r // win
    n_pad_windows = pad_rows.shape[0] // win
    assert r % win == 0 and dest_flat.shape[0] == TOP_K * r and n_pad_windows % workers == 0
    windows_per_worker = -(-token_windows // workers)
    pads_per_worker = n_pad_windows // workers
    mesh = plsc.VectorSubcoreMesh(core_axis_name="core", subcore_axis_name="subcore", num_cores=SC_NUM_CORES,
                                  num_subcores=SC_NUM_SUBCORES)
    zero_rows = jnp.zeros((win, width), tokp.dtype)

    @functools.partial(
        pl.kernel, out_type=jax.ShapeDtypeStruct((n_sorted + SC_SPARE_ROWS, width), tokp.dtype), mesh=mesh,
        scratch_types=[pltpu.VMEM((win,), jnp.int32), pltpu.VMEM((win, width), tokp.dtype),
                       pltpu.SemaphoreType.DMA],
        name="moe_sc_dispatch")
    def dispatch_kernel(tok_hbm, dest_hbm, pad_hbm, zero_hbm, xs_hbm, idx_vmem, rows_vmem, sem):
        worker = lax.axis_index("subcore") * SC_NUM_CORES + lax.axis_index("core")

        @pl.loop(0, windows_per_worker)
        def _(j):
            window = j * workers + worker

            @pl.when(window < token_windows)
            def _():
                tok0 = window * win
                pltpu.sync_copy(tok_hbm.at[pl.ds(tok0, win)], rows_vmem)
                for k in range(TOP_K):
                    pltpu.sync_copy(dest_hbm.at[pl.ds(k * r + tok0, win)], idx_vmem)
                    pltpu.async_copy(rows_vmem, xs_hbm.at[idx_vmem], sem).wait()

        pltpu.sync_copy(zero_hbm, rows_vmem)

        @pl.loop(0, pads_per_worker)
        def _(j):
            off = (worker * pads_per_worker + j) * win
            pltpu.sync_copy(pad_hbm.at[pl.ds(off, win)], idx_vmem)
            pltpu.async_copy(rows_vmem, xs_hbm.at[idx_vmem], sem).wait()

    return dispatch_kernel(tokp, dest_flat, pad_rows, zero_rows)


def _expert_kernel(be_ref, nb_ref, ord_ref, ue_ref, nue_ref, xs_ref, wgu_hbm, wdn_hbm, ys_ref, wgu_f32, wdn_f32,
                   wgu_bf, wdn_bf, sems, *, layer):
    def weight_copies(o):
        slot = o % 2
        e = ue_ref[o]
        return (pltpu.make_async_copy(wgu_hbm.at[layer, e], wgu_f32.at[slot], sems.at[0, slot]),
                pltpu.make_async_copy(wdn_hbm.at[layer, e], wdn_f32.at[slot], sems.at[1, slot]))

    def start_weights(o):
        @pl.when(o < nue_ref[0])
        def _():
            for cp in weight_copies(o):
                cp.start()

    def one_block(sub, carry):
        j = pl.program_id(0) * EXPERT_BLOCKS_PER_STEP + sub

        @pl.when(j < nb_ref[0])
        def _():
            o = ord_ref[j]
            changed = jnp.logical_or(j == 0, be_ref[j] != be_ref[jnp.maximum(j - 1, 0)])

            @pl.when(j == 0)
            def _():
                start_weights(0)
                start_weights(1)

            @pl.when(changed)
            def _():
                for cp in weight_copies(o):
                    cp.wait()
                slot = o % 2
                wgu_bf[...] = wgu_f32[slot].astype(BF16)
                wdn_bf[...] = wdn_f32[slot].astype(BF16)
                start_weights(o + 2)

            rows = pl.ds(pl.multiple_of(sub * EXPERT_BLOCK, EXPERT_BLOCK), EXPERT_BLOCK)
            x_lo, x_hi = _unpack_bf16_pairs(xs_ref[rows, :])
            h = (jnp.dot(x_lo.astype(BF16), wgu_bf[0:PACK_W, :], preferred_element_type=F32)
                 + jnp.dot(x_hi.astype(BF16), wgu_bf[PACK_W:, :], preferred_element_type=F32))
            g, u = h[:, 0:EXPERT_HIDDEN], h[:, EXPERT_HIDDEN:]
            y = jnp.dot((g * _sigmoid(g) * u).astype(BF16), wdn_bf[...], preferred_element_type=F32)
            ys_ref[rows, :] = _pack_bf16_pairs(y)

        return carry

    lax.fori_loop(0, EXPERT_BLOCKS_PER_STEP, one_block, 0)


def _expert_call(block_expert, n_blocks_used, block_ordinal, used_expert, n_used_experts, xs, w_gu, w_dn, layer):
    n_rows = block_expert.shape[0] * EXPERT_BLOCK
    bm = EXPERT_BLOCK
    d = D_MODEL
    step_rows = bm * EXPERT_BLOCKS_PER_STEP
    assert n_rows % step_rows == 0
    used_step = lambda s, be, nb, od, ue, nue: (jnp.minimum(s, (nb[0] - 1) // EXPERT_BLOCKS_PER_STEP), 0)
    grid_spec = pltpu.PrefetchScalarGridSpec(
        num_scalar_prefetch=5,
        grid=(n_rows // step_rows,),
        in_specs=[
            pl.BlockSpec((step_rows, PACK_W), used_step),
            pl.BlockSpec(memory_space=pl.ANY),
            pl.BlockSpec(memory_space=pl.ANY),
        ],
        out_specs=pl.BlockSpec((step_rows, PACK_W), used_step),
        scratch_shapes=[
            pltpu.VMEM((2, d, 2 * EXPERT_HIDDEN), F32),
            pltpu.VMEM((2, EXPERT_HIDDEN, d), F32),
            pltpu.VMEM((d, 2 * EXPERT_HIDDEN), BF16),
            pltpu.VMEM((EXPERT_HIDDEN, d), BF16),
            pltpu.SemaphoreType.DMA((2, 2)),
        ],
    )
    return pl.pallas_call(
        functools.partial(_expert_kernel, layer=layer),
        grid_spec=grid_spec,
        out_shape=jax.ShapeDtypeStruct((n_rows, PACK_W), jnp.int32),
        compiler_params=_cparams("arbitrary"),
        name="moe_experts",
    )(block_expert, n_blocks_used, block_ordinal, used_expert, n_used_experts, xs, w_gu, w_dn)


def _sc_gather_rows(table, indices):
    n = indices.shape[0]
    width = table.shape[1]
    workers = SC_NUM_CORES * SC_NUM_SUBCORES
    assert n % SC_GATHER_WINDOW == 0
    n_windows = n // SC_GATHER_WINDOW
    mesh = plsc.VectorSubcoreMesh(core_axis_name="core", subcore_axis_name="subcore", num_cores=SC_NUM_CORES,
                                  num_subcores=SC_NUM_SUBCORES)

    @functools.partial(
        pl.kernel, out_type=jax.ShapeDtypeStruct((n, width), table.dtype), mesh=mesh,
        scratch_types=[pltpu.VMEM((SC_GATHER_WINDOW,), jnp.int32),
                       pltpu.VMEM((SC_GATHER_WINDOW, width), table.dtype),
                       pltpu.SemaphoreType.DMA],
        name="moe_sc_gather")
    def gather_kernel(table_hbm, idx_hbm, out_hbm, idx_vmem, rows_vmem, sem):
        worker = lax.axis_index("subcore") * SC_NUM_CORES + lax.axis_index("core")

        @pl.loop(0, -(-n_windows // workers))
        def _(j):
            window = j * workers + worker

            @pl.when(window < n_windows)
            def _():
                off = window * SC_GATHER_WINDOW
                pltpu.sync_copy(idx_hbm.at[pl.ds(off, SC_GATHER_WINDOW)], idx_vmem)
                pltpu.async_copy(table_hbm.at[idx_vmem], rows_vmem, sem).wait()
                pltpu.sync_copy(rows_vmem, out_hbm.at[pl.ds(off, SC_GATHER_WINDOW)])

    return gather_kernel(table, indices)


def _combine_kernel(*refs, alpha):
    y_refs = refs[:TOP_K]
    x_ref, fsh_ref, gate_ref, mod_ref, lng_ref, lnb_ref = refs[TOP_K:TOP_K + 6]
    o_ref = refs[-1]
    d = D_MODEL
    t = x_ref.shape[0]
    gate_rows = gate_ref[...]
    pad = jnp.zeros((LANES - TOP_K, t), F32)
    gate_cols = jnp.concatenate([gate_rows, pad], axis=0).T
    f_lo = fsh_ref[:, 0:PACK_W]
    f_hi = fsh_ref[:, PACK_W:]
    for k in range(TOP_K):
        y_lo, y_hi = _unpack_bf16_pairs(y_refs[k][...])
        f_lo = f_lo + gate_cols[:, k:k + 1] * y_lo
        f_hi = f_hi + gate_cols[:, k:k + 1] * y_hi
    f = jnp.concatenate([f_lo, f_hi], axis=1)
    z = alpha * x_ref[...] + mod_ref[:, 5 * d:6 * d] * f
    o_ref[...] = _layer_norm_rows(z) * lng_ref[...] + lnb_ref[...]


def _combine_call(y_tok, x, fsh, gate, mod3, ln_g, ln_b, prev_out, *, batch_index, tiles_per_batch, alpha,
                  drop_context):
    r, d = x.shape
    t = ROW_TILE
    n_batches = r // t // tiles_per_batch
    n_tiles = tiles_per_batch - 1 if drop_context else tiles_per_batch
    tile0 = batch_index * tiles_per_batch
    row = lambda i: (tile0 + i, 0)
    col = lambda i: (0, tile0 + i)
    const = lambda i: (0, 0)
    kern = functools.partial(_combine_kernel, alpha=alpha)
    y_specs = [pl.BlockSpec((t, PACK_W), functools.partial(lambda k, i: (k * tiles_per_batch + i, 0), k))
               for k in range(TOP_K)]
    in_specs = y_specs + [
        pl.BlockSpec((t, d), row),
        pl.BlockSpec((t, d), row),
        pl.BlockSpec((TOP_K, t), col),
        pl.BlockSpec((None, 1, 6 * d), lambda i: (_mod_row(tile0 + i, tiles_per_batch), 0, 0)),
        pl.BlockSpec((1, d), const),
        pl.BlockSpec((1, d), const),
    ]
    args = [y_tok] * TOP_K + [x, fsh, gate, mod3, ln_g, ln_b]
    aliases = {}
    if prev_out is not None:
        in_specs.append(pl.BlockSpec(memory_space=pl.ANY))
        args.append(prev_out)
        aliases = {len(args) - 1: 0}
    return pl.pallas_call(
        kern,
        grid=(n_tiles,),
        in_specs=in_specs,
        out_specs=pl.BlockSpec((t, d), lambda i: (batch_index * n_tiles + i, 0)),
        out_shape=jax.ShapeDtypeStruct((n_batches * n_tiles * t, d), F32),
        input_output_aliases=aliases,
        compiler_params=_cparams("arbitrary"),
        name="moe_combine",
    )(*args)


def _rope_tables(seq):
    rows = seq // GRID_W
    row = jnp.repeat(jnp.arange(rows, dtype=F32), GRID_W)
    col = jnp.tile(jnp.arange(GRID_W, dtype=F32), rows)
    nf = DA_DIM // 4
    freqs = ROPE_BASE ** (-jnp.arange(nf, dtype=F32) / nf)
    cr, sr = jnp.cos(row[:, None] * freqs), jnp.sin(row[:, None] * freqs)
    cc, sc = jnp.cos(col[:, None] * freqs), jnp.sin(col[:, None] * freqs)
    c64 = jnp.concatenate([cr, cr, cc, cc], axis=1)
    s64 = jnp.concatenate([-sr, sr, -sc, sc], axis=1)
    c = jnp.concatenate([jnp.tile(c64, (1, 2)), jnp.ones((CTX_LEN, LANES), F32)], axis=0)
    s = jnp.concatenate([jnp.tile(s64, (1, 2)), jnp.zeros((CTX_LEN, LANES), F32)], axis=0)
    return c, s


def kernel(x, c, ctx, c_ctx, w_mod, b_mod, w_in, w_out, diff_lambda, pool_w, pool_scale, ret_log_decay, ln_g, ln_b,
           w_router, router_bias, w_expert_gate_up, w_expert_down, w_shared_gate_up, w_shared_down):
    batch, seq, d = x.shape
    depth = w_mod.shape[0]
    assert d == D_MODEL and ctx.shape[1] == CTX_LEN == ROW_TILE and batch == 2
    assert seq % ROW_TILE == 0 and seq % GRID_W == 0 and w_in.shape[-1] == IN_WIDTH
    rows_per_batch = seq + CTX_LEN
    tiles_per_batch = rows_per_batch // ROW_TILE
    r = batch * rows_per_batch
    alpha = (2.0 * depth) ** 0.25

    xa = jnp.concatenate([x, ctx], axis=1).reshape(r, d)
    cvec = jnp.zeros((8, d), F32).at[0:batch].set(c).at[batch].set(c_ctx)
    mod_all = _mod_call(cvec, w_mod, b_mod)
    rope_c, rope_s = _rope_tables(seq)

    n_sorted = r * TOP_K + N_EXPERTS * EXPERT_BLOCK
    n_blocks = n_sorted // EXPERT_BLOCK

    for l in range(depth):
        lambda_init = 0.8 - 0.6 * math.exp(-0.3 * l)
        mod3 = mod_all[l].reshape(8, 1, 6 * d)
        lng = ln_g[l].reshape(2, 1, d)
        lnb = ln_b[l].reshape(2, 1, d)

        w_in_bf = w_in[l].astype(BF16)
        w_vt_bf = w_in_bf[:, QK_WIDTH:QK_WIDTH + DA_WIDTH].T
        qk, vda, u, rqkv, rg = _inproj_call(xa, mod3, w_in_bf, w_vt_bf, rope_c, rope_s, tiles_per_batch)
        da = _attn_call(diff_lambda[l], qk, vda, batch=batch, rows_per_batch=rows_per_batch, seq=seq,
                        lambda_init=lambda_init)
        o_f, o_b = _ret_call(ret_log_decay[l], rqkv, batch=batch, rows_per_batch=rows_per_batch, seq=seq)
        pool_bd = jnp.zeros((POOL_WIDTH, POOL_WIDTH), F32)
        for gi in range(len(POOL_WINDOWS)):
            sl = slice(gi * POOL_GROUP, (gi + 1) * POOL_GROUP)
            pool_bd = pool_bd.at[sl, sl].set(pool_w[l, gi])
        xa = _mixout_call(xa, da, u, o_f, o_b, rg, mod3, w_out[l].astype(BF16), pool_bd.astype(BF16),
                          pool_scale[l].reshape(1, POOL_WIDTH), lng[0], lnb[0],
                          tiles_per_batch=tiles_per_batch, seq=seq, alpha=alpha)

        wr_t = w_router[l].T
        wr_hi = wr_t.astype(BF16)
        wr_lo = (wr_t - wr_hi.astype(F32)).astype(BF16)
        tokp, idx, gate, rank, cnt, fsh = _router_call(
            xa, mod3, wr_hi, wr_lo, router_bias[l].reshape(N_EXPERTS, 1),
            w_shared_gate_up[l].astype(BF16), w_shared_down[l].astype(BF16), tiles_per_batch=tiles_per_batch)
        counts = cnt[:, 0]
        padded = (counts + EXPERT_BLOCK - 1) // EXPERT_BLOCK * EXPERT_BLOCK
        pad_end = jnp.cumsum(padded)
        offs = pad_end - padded
        expert_ids = jnp.arange(N_EXPERTS, dtype=jnp.int32)
        blk_row = jnp.arange(n_blocks, dtype=jnp.int32) * EXPERT_BLOCK
        block_expert = jnp.minimum(jnp.sum(pad_end[None, :] <= blk_row[:, None], axis=1), N_EXPERTS - 1)
        n_used = pad_end[-1:] // EXPERT_BLOCK
        used = counts > 0
        ordinal = jnp.cumsum(used) - 1
        hit = used[None, :] & (ordinal[None, :] == expert_ids[:, None])
        used_expert = jnp.sum(jnp.where(hit, expert_ids[None, :], 0), axis=1)
        n_used_experts = jnp.sum(used)[None]
        block_ordinal = ordinal[block_expert]
        slot = jnp.arange(EXPERT_BLOCK, dtype=jnp.int32)[None, :]
        first_pad = (padded - EXPERT_BLOCK)[:, None] + slot
        is_pad = (first_pad >= counts[:, None]) & (padded[:, None] > 0)
        spare = n_sorted + jnp.arange(N_EXPERTS * EXPERT_BLOCK, dtype=jnp.int32).reshape(N_EXPERTS, EXPERT_BLOCK)
        pad_rows = jnp.where(is_pad, offs[:, None] + first_pad, spare).reshape(N_EXPERTS * EXPERT_BLOCK)
        i32 = lambda a: a.astype(jnp.int32)

        dest = _dest_call(idx, rank, i32(offs).reshape(N_EXPERTS, 1))
        dest_flat = dest.reshape(TOP_K * r)
        xs = _sc_dispatch(tokp, dest_flat, i32(pad_rows), n_sorted)
        ys = _expert_call(i32(block_expert), i32(n_used), i32(block_ordinal), i32(used_expert),
                          i32(n_used_experts), xs, w_expert_gate_up, w_expert_down, l)
        x_new = None
        for b in range(batch):
            dest_b = dest[:, b * rows_per_batch:(b + 1) * rows_per_batch].reshape(TOP_K * rows_per_batch)
            y_tok = _sc_gather_rows(ys, dest_b)
            x_new = _combine_call(y_tok, xa, fsh, gate, mod3, lng[1], lnb[1], x_new, batch_index=b,
                                  tiles_per_batch=tiles_per_batch, alpha=alpha, drop_context=(l == depth - 1))
        xa = x_new

    return xa.reshape(batch, seq, d)
```

```python
import functools
import math

import jax
import jax.numpy as jnp
from jax import lax
from jax.experimental import pallas as pl
from jax.experimental.pallas import tpu as pltpu
from jax.experimental.pallas import tpu_sc as plsc

F32 = jnp.float32
BF16 = jnp.bfloat16
HIGHEST = lax.Precision.HIGHEST

D_MODEL = 1024
CTX_LEN = 256
GRID_W = 64
DA_HEADS = 4
DA_DIM = 64
DA_VDIM = 2 * DA_DIM
DA_WIDTH = DA_HEADS * DA_VDIM
ROPE_BASE = 10000.0
POOL_WINDOWS = (2, 4, 8, 16)
POOL_GROUP = 64
POOL_WIDTH = len(POOL_WINDOWS) * POOL_GROUP
POOL_HALO = 8
RET_HEADS = 4
RET_DK = 64
RET_WIDTH = RET_HEADS * RET_DK
RET_CHUNK = 128
QK_WIDTH = 2 * DA_HEADS * 2 * DA_DIM
IN_WIDTH = QK_WIDTH + DA_WIDTH + POOL_WIDTH + 4 * RET_WIDTH
N_EXPERTS = 256
TOP_K = 8
N_GROUPS = 8
GROUP_SIZE = N_EXPERTS // N_GROUPS
TOPK_GROUPS = 4
EXPERT_HIDDEN = 256
ROUTED_SCALE = 2.5
LN_EPS = 1e-6
RMS_EPS = 1e-5

LANES = 128
ROW_TILE = 256
MOD_COL_TILE = 1536
DEST_STEPS = 4
ATTN_Q_TILE = 256
ATTN_K_CHUNK = 256
ATTN_SCORE_GROUP = 4
ATTN_UNROLL = 16
SC_NUM_CORES = 2
SC_NUM_SUBCORES = 16
SC_GATHER_WINDOW = 128
EXPERT_BLOCK = 256
EXPERT_BLOCKS_PER_STEP = 4
SC_SPARE_ROWS = N_EXPERTS * EXPERT_BLOCK
PACK_W = D_MODEL // 2
VMEM_LIMIT = 56 * 1024 * 1024


def _cparams(*sem):
    return pltpu.CompilerParams(dimension_semantics=sem, vmem_limit_bytes=VMEM_LIMIT)


def _sigmoid(x):
    return 1.0 / (1.0 + jnp.exp(-x))


def _layer_norm_rows(x):
    mu = jnp.mean(x, axis=-1, keepdims=True)
    xc = x - mu
    var = jnp.mean(xc * xc, axis=-1, keepdims=True)
    return xc * lax.rsqrt(var + LN_EPS)


def _pack_bf16_pairs(x):
    half = x.shape[1] // 2
    bits = pltpu.bitcast(x.astype(BF16).astype(F32), jnp.uint32)
    word = lax.shift_right_logical(bits[:, 0:half], jnp.uint32(16)) | (bits[:, half:] & jnp.uint32(0xFFFF0000))
    return pltpu.bitcast(word, jnp.int32)


def _unpack_bf16_pairs(packed):
    word = pltpu.bitcast(packed, jnp.uint32)
    lo = pltpu.bitcast(lax.shift_left(word, jnp.uint32(16)), F32)
    hi = pltpu.bitcast(word & jnp.uint32(0xFFFF0000), F32)
    return lo, hi


def _mod_row(i, tiles_per_batch):
    return jnp.where(i % tiles_per_batch == tiles_per_batch - 1, 2, i // tiles_per_batch)


def _mod_kernel(c_ref, w_ref, b_ref, o_ref):
    c = c_ref[...]
    s = c * _sigmoid(c)
    o_ref[...] = jnp.dot(s, w_ref[...], precision=HIGHEST, preferred_element_type=F32) + b_ref[...]


def _mod_call(cvec, w_mod, b_mod):
    depth, d, n = w_mod.shape
    tn = MOD_COL_TILE
    assert n % tn == 0
    return pl.pallas_call(
        _mod_kernel,
        grid=(depth, n // tn),
        in_specs=[
            pl.BlockSpec((8, d), lambda l, j: (0, 0)),
            pl.BlockSpec((None, d, tn), lambda l, j: (l, 0, j)),
            pl.BlockSpec((None, 1, tn), lambda l, j: (l, 0, j)),
        ],
        out_specs=pl.BlockSpec((None, 8, tn), lambda l, j: (l, 0, j)),
        out_shape=jax.ShapeDtypeStruct((depth, 8, n), F32),
        compiler_params=_cparams("arbitrary", "arbitrary"),
        name="mod",
    )(cvec, w_mod, b_mod.reshape(depth, 1, n))


def _inproj_kernel(x_ref, mod_ref, w_ref, wvt_ref, ct_ref, st_ref, qk_ref, vt_ref, u_ref, r_ref, g_ref):
    d = D_MODEL
    xn = _layer_norm_rows(x_ref[...])
    h = (xn * (1.0 + mod_ref[:, d:2 * d]) + mod_ref[:, 0:d]).astype(BF16)

    a = jnp.dot(h, w_ref[:, 0:QK_WIDTH], preferred_element_type=F32)
    lane = lax.broadcasted_iota(jnp.int32, (a.shape[0], LANES), 1)
    first_half = (lane % 32) < 16
    ct = ct_ref[...]
    st = st_ref[...]
    for s in range(QK_WIDTH // LANES):
        blk = a[:, s * LANES:(s + 1) * LANES]
        partner = jnp.where(first_half, pltpu.roll(blk, LANES - 16, 1), pltpu.roll(blk, 16, 1))
        rot = blk * ct + partner * st
        if s < QK_WIDTH // LANES // 2:
            rot = rot * (DA_DIM ** -0.5 * math.log2(math.e))
        qk_ref[:, s * LANES:(s + 1) * LANES] = rot.astype(BF16)

    vt_ref[...] = lax.dot_general(wvt_ref[...], h, (((1,), (1,)), ((), ())),
                                  preferred_element_type=F32).astype(BF16)
    o = QK_WIDTH + DA_WIDTH
    u_ref[...] = jnp.dot(h, w_ref[:, o:o + POOL_WIDTH], preferred_element_type=F32)
    o += POOL_WIDTH
    r = jnp.dot(h, w_ref[:, o:o + 3 * RET_WIDTH], preferred_element_type=F32)
    r_ref[:, 0:RET_WIDTH] = r[:, 0:RET_WIDTH].astype(BF16)
    r_ref[:, RET_WIDTH:2 * RET_WIDTH] = (r[:, RET_WIDTH:2 * RET_WIDTH] * (RET_DK ** -0.5)).astype(BF16)
    r_ref[:, 2 * RET_WIDTH:] = r[:, 2 * RET_WIDTH:].astype(BF16)
    o += 3 * RET_WIDTH
    g_ref[...] = jnp.dot(h, w_ref[:, o:o + RET_WIDTH], preferred_element_type=F32)


def _inproj_call(x, mod3, w_in_bf, w_vt_bf, rope_c, rope_s, tiles_per_batch):
    r, d = x.shape
    t = ROW_TILE
    nt = r // t
    row = lambda i: (i, 0)
    return pl.pallas_call(
        _inproj_kernel,
        grid=(nt,),
        in_specs=[
            pl.BlockSpec((t, d), row),
            pl.BlockSpec((None, 1, 6 * d), lambda i: (_mod_row(i, tiles_per_batch), 0, 0)),
            pl.BlockSpec((d, IN_WIDTH), lambda i: (0, 0)),
            pl.BlockSpec((DA_WIDTH, d), lambda i: (0, 0)),
            pl.BlockSpec((t, LANES), lambda i: (i % tiles_per_batch, 0)),
            pl.BlockSpec((t, LANES), lambda i: (i % tiles_per_batch, 0)),
        ],
        out_specs=[
            pl.BlockSpec((t, QK_WIDTH), row),
            pl.BlockSpec((DA_WIDTH, t), lambda i: (0, i)),
            pl.BlockSpec((t, POOL_WIDTH), row),
            pl.BlockSpec((t, 3 * RET_WIDTH), row),
            pl.BlockSpec((t, RET_WIDTH), row),
        ],
        out_shape=[
            jax.ShapeDtypeStruct((r, QK_WIDTH), BF16),
            jax.ShapeDtypeStruct((DA_WIDTH, r), BF16),
            jax.ShapeDtypeStruct((r, POOL_WIDTH), F32),
            jax.ShapeDtypeStruct((r, 3 * RET_WIDTH), BF16),
            jax.ShapeDtypeStruct((r, RET_WIDTH), F32),
        ],
        compiler_params=_cparams("arbitrary"),
        name="inproj",
    )(x, mod3, w_in_bf, w_vt_bf, rope_c, rope_s)


def _attn_kernel(lam_ref, q_ref, k_ref, vt_ref, o_ref, *s_refs, k_chunk, seq, lambda_init):
    mq = ATTN_Q_TILE
    n_tiles = (seq + CTX_LEN) // mq
    n_chunks = (seq + CTX_LEN) // k_chunk
    last = n_chunks - 1
    n_iters = last // ATTN_UNROLL
    neg_inf = jnp.full((1, 2 * mq), -jnp.inf, F32)
    acc_zero = jnp.zeros((DA_VDIM + 16, 2 * mq), F32)
    ones_rows = jnp.where(lax.broadcasted_iota(jnp.int32, (16, k_chunk), 0) == 0, 1.0, 0.0).astype(BF16)

    def tile_rows(i):
        return pl.ds(pl.multiple_of(i * mq, mq), mq)

    def q_transposed(i):
        q = q_ref[tile_rows(i), :]
        lane = lax.broadcasted_iota(jnp.int32, q.shape, 1)
        zero = jnp.zeros_like(q)
        q2 = jnp.concatenate([jnp.where(lane < DA_DIM, q, zero), jnp.where(lane >= DA_DIM, q, zero)], axis=0)
        return q2.astype(F32).T.astype(BF16)

    def score_chunk(s_ref, c, qt, m):
        off = pl.multiple_of(c * k_chunk, k_chunk)
        s = jnp.dot(k_ref[pl.ds(off, k_chunk), :], qt, preferred_element_type=F32)
        s_ref[c] = s
        return jnp.maximum(m, jnp.max(s, axis=0, keepdims=True))

    def score_chunk_group(s_ref, c, qt, m):
        g = ATTN_SCORE_GROUP
        off = pl.multiple_of(c * k_chunk, g * k_chunk)
        s = jnp.dot(k_ref[pl.ds(off, g * k_chunk), :], qt, preferred_element_type=F32)
        for j in range(g):
            s_ref[c + j] = s[j * k_chunk:(j + 1) * k_chunk]
        return jnp.maximum(m, jnp.max(s, axis=0, keepdims=True))

    def value_chunk(s_ref, c, m, acc):
        off = pl.multiple_of(c * k_chunk, k_chunk)
        vt = jnp.concatenate([vt_ref[:, pl.ds(off, k_chunk)], ones_rows], axis=0)
        p = jnp.exp2((s_ref[c] - m).astype(BF16))
        return acc + jnp.dot(vt, p, preferred_element_type=F32)

    def finish(i, acc):
        l0, l1 = acc[DA_VDIM:DA_VDIM + 1, 0:mq], acc[DA_VDIM:DA_VDIM + 1, mq:]
        a0, a1 = acc[0:DA_VDIM, 0:mq], acc[0:DA_VDIM, mq:]
        lv = lam_ref[...]
        lam = (jnp.exp(jnp.sum(lv[0:1] * lv[1:2], axis=-1, keepdims=True))
               - jnp.exp(jnp.sum(lv[2:3] * lv[3:4], axis=-1, keepdims=True)) + lambda_init)
        o = a0 / l0 - lam * (a1 / l1)
        o = o * lax.rsqrt(jnp.mean(o * o, axis=0, keepdims=True) + RMS_EPS) * (1.0 - lambda_init)
        o_ref[tile_rows(i), :] = o.T.astype(BF16)

    def scores_only(s_ref, qt):
        def body(it, m):
            for u in range(ATTN_UNROLL):
                m = score_chunk(s_ref, it * ATTN_UNROLL + u, qt, m)
            return m
        return score_chunk(s_ref, last, qt, lax.fori_loop(0, n_iters, body, neg_inf))

    def values_only(s_ref, m):
        def body(it, acc):
            for u in range(ATTN_UNROLL):
                acc = value_chunk(s_ref, it * ATTN_UNROLL + u, m, acc)
            return acc
        return value_chunk(s_ref, last, m, lax.fori_loop(0, n_iters, body, acc_zero))

    def fused_tile(i, m_prev, s_cur, s_prev):
        qt = q_transposed(i)

        def body(it, carry):
            m, acc = carry
            for u in range(0, ATTN_UNROLL, ATTN_SCORE_GROUP):
                c = it * ATTN_UNROLL + u
                m = score_chunk_group(s_cur, c, qt, m)
                for j in range(ATTN_SCORE_GROUP):
                    acc = value_chunk(s_prev, c + j, m_prev, acc)
            return m, acc

        m, acc = lax.fori_loop(0, n_iters, body, (neg_inf, acc_zero))
        m = score_chunk(s_cur, last, qt, m)
        finish(i - 1, value_chunk(s_prev, last, m_prev, acc))
        return m

    s_even, s_odd = s_refs
    ctx_tile = n_tiles - 1
    assert ctx_tile % 2 == 0 and ctx_tile >= 2
    m = scores_only(s_even, q_transposed(0))

    def tile_pair(p, m):
        m = fused_tile(2 * p + 1, m, s_odd, s_even)
        return fused_tile(2 * p + 2, m, s_even, s_odd)

    m = lax.fori_loop(0, (ctx_tile - 2) // 2, tile_pair, m)
    m = fused_tile(ctx_tile - 1, m, s_odd, s_even)
    m_ctx = score_chunk(s_even, last, q_transposed(ctx_tile), neg_inf)
    finish(ctx_tile - 1, values_only(s_odd, m))
    finish(ctx_tile, value_chunk(s_even, last, m_ctx, acc_zero))


def _attn_call(lam_vec, qk, vda, *, batch, rows_per_batch, seq, lambda_init):
    tq = ATTN_Q_TILE
    assert seq % (ATTN_K_CHUNK * ATTN_UNROLL) == 0 and rows_per_batch - seq == CTX_LEN == tq == ATTN_K_CHUNK
    nq = rows_per_batch // tq
    kern = functools.partial(_attn_kernel, k_chunk=ATTN_K_CHUNK, seq=seq, lambda_init=lambda_init)
    return pl.pallas_call(
        kern,
        grid=(batch, DA_HEADS),
        in_specs=[
            pl.BlockSpec((4, DA_DIM), lambda b, h: (0, 0)),
            pl.BlockSpec((rows_per_batch, DA_VDIM), lambda b, h: (b, h)),
            pl.BlockSpec((rows_per_batch, DA_VDIM), lambda b, h: (b, DA_HEADS + h)),
            pl.BlockSpec((DA_VDIM, rows_per_batch), lambda b, h: (h, b)),
        ],
        out_specs=pl.BlockSpec((rows_per_batch, DA_VDIM), lambda b, h: (b, h)),
        out_shape=jax.ShapeDtypeStruct((qk.shape[0], DA_WIDTH), BF16),
        scratch_shapes=[pltpu.VMEM((rows_per_batch // ATTN_K_CHUNK, ATTN_K_CHUNK, 2 * tq), F32)] * 2,
        compiler_params=_cparams("arbitrary", "arbitrary"),
        name="diff_attn",
    )(lam_vec, qk, qk, vda)


def _ret_kernel(ld_ref, f_ref, b_ref, of_ref, ob_ref, dm_ref, qd_ref, kd_ref, cd_ref, st_ref):
    c = pl.program_id(1)
    ch = RET_CHUNK
    w = RET_WIDTH
    lane_head = lax.broadcasted_iota(jnp.int32, (1, w), 1) // RET_DK

    @pl.when(c == 0)
    def _():
        st_ref[...] = jnp.zeros_like(st_ref)
        ri = lax.broadcasted_iota(jnp.int32, (ch, ch), 0)
        ci = lax.broadcasted_iota(jnp.int32, (ch, ch), 1)
        rowf = lax.broadcasted_iota(jnp.int32, (ch, w), 0).astype(F32)
        for d in range(2):
            lg_lane = jnp.zeros((1, w), F32)
            for hh in range(RET_HEADS):
                lg = -jnp.exp(jnp.full((1, 1), ld_ref[d, hh], F32))
                lg_lane = jnp.where(lane_head == hh, lg, lg_lane)
                dist = ((ri - ci) if d == 0 else (ci - ri)).astype(F32)
                dm_ref[d, hh] = jnp.where(dist >= 0, jnp.exp(dist * lg), 0.0)
            if d == 0:
                qd_ref[d] = jnp.exp((rowf + 1.0) * lg_lane)
                kd_ref[d] = jnp.exp((ch - 1.0 - rowf) * lg_lane)
            else:
                qd_ref[d] = jnp.exp((ch - rowf) * lg_lane)
                kd_ref[d] = jnp.exp(rowf * lg_lane)
            cd_ref[d] = jnp.exp(float(ch) * lg_lane)

    rblk = lax.broadcasted_iota(jnp.int32, (w, w), 0) // RET_DK
    cblk = lax.broadcasted_iota(jnp.int32, (w, w), 1) // RET_DK
    for d, (src, dst) in enumerate(((f_ref, of_ref), (b_ref, ob_ref))):
        q = src[:, 0:w]
        k = src[:, w:2 * w]
        v = src[:, 2 * w:3 * w]
        st = st_ref[d]
        o = jnp.dot((q.astype(F32) * qd_ref[d]).astype(BF16), st.astype(BF16), preferred_element_type=F32)
        for hh in range(RET_HEADS):
            in_head = lane_head == hh
            qm = jnp.where(in_head, q, jnp.zeros_like(q))
            s = lax.dot_general(qm, k, (((1,), (1,)), ((), ())), preferred_element_type=F32)
            intra = (s * dm_ref[d, hh]).astype(BF16)
            o = o + jnp.where(in_head, jnp.dot(intra, v, preferred_element_type=F32), 0.0)
        dst[...] = o
        kk_t = (k.astype(F32) * kd_ref[d]).T.astype(BF16)
        upd = jnp.dot(kk_t, v, preferred_element_type=F32)
        st_ref[d] = jnp.where(rblk == cblk, st * cd_ref[d] + upd, 0.0)


def _ret_call(log_decay, rqkv, *, batch, rows_per_batch, seq):
    ch = RET_CHUNK
    nc = rows_per_batch // ch
    n_lat = seq // ch
    n_ctx = nc - n_lat

    def fwd(b, c):
        return (b * nc + jnp.where(c < n_ctx, n_lat + c, c - n_ctx), 0)

    def bwd(b, c):
        return (b * nc + nc - 1 - c, 0)

    w = RET_WIDTH
    return pl.pallas_call(
        _ret_kernel,
        grid=(batch, nc),
        in_specs=[
            pl.BlockSpec(memory_space=pltpu.SMEM),
            pl.BlockSpec((ch, 3 * w), fwd),
            pl.BlockSpec((ch, 3 * w), bwd),
        ],
        out_specs=[pl.BlockSpec((ch, w), fwd), pl.BlockSpec((ch, w), bwd)],
        out_shape=[jax.ShapeDtypeStruct((rqkv.shape[0], w), F32)] * 2,
        scratch_shapes=[
            pltpu.VMEM((2, RET_HEADS, ch, ch), F32),
            pltpu.VMEM((2, ch, w), F32),
            pltpu.VMEM((2, ch, w), F32),
            pltpu.VMEM((2, 1, w), F32),
            pltpu.VMEM((2, w, w), F32),
        ],
        compiler_params=_cparams("arbitrary", "arbitrary"),
        name="retention",
    )(log_decay, rqkv, rqkv)


def _mixout_kernel(x_ref, da_ref, u_ref, up_ref, un_ref, of_ref, ob_ref, rg_ref, mod_ref, wo_ref, pw_ref,
                   ps_ref, lng_ref, lnb_ref, o_ref, *, tiles_per_batch, seq, alpha):
    d = D_MODEL
    t = x_ref.shape[0]
    i = pl.program_id(0)
    j = i % tiles_per_batch
    is_ctx = j == tiles_per_batch - 1
    stream_len = jnp.where(is_ctx, CTX_LEN, seq)
    p0 = jnp.where(is_ctx, 0, j * t)

    u = u_ref[...]
    prev = jnp.where(p0 > 0, up_ref[...], 0.0)
    nxt = jnp.where(p0 + t < stream_len, un_ref[...], 0.0)
    ext = jnp.concatenate([prev, u, nxt], axis=0)
    n = t + 2 * POOL_HALO
    a2 = ext + pltpu.roll(ext, 1, 0)
    a4 = pltpu.roll(a2, 1, 0) + pltpu.roll(a2, n - 1, 0)
    a8 = pltpu.roll(a4, 2, 0) + pltpu.roll(a4, n - 2, 0)
    a16 = pltpu.roll(a8, 4, 0) + pltpu.roll(a8, n - 4, 0)
    pos = p0 + lax.broadcasted_iota(jnp.int32, (t, POOL_WIDTH), 0)
    group = lax.broadcasted_iota(jnp.int32, (1, POOL_WIDTH), 1) // POOL_GROUP
    mean = jnp.zeros((t, POOL_WIDTH), F32)
    for gi, (wnd, asum) in enumerate(zip(POOL_WINDOWS, (a2, a4, a8, a16))):
        cnt = jnp.minimum(pos + wnd // 2, stream_len) - jnp.maximum(pos - wnd // 2, 0)
        mean = jnp.where(group == gi, asum[POOL_HALO:POOL_HALO + t] / cnt.astype(F32), mean)
    pool = jnp.dot((mean - u).astype(BF16), pw_ref[...], preferred_element_type=F32) * ps_ref[...]

    o = of_ref[...] + ob_ref[...]
    head = lax.broadcasted_iota(jnp.int32, (1, RET_WIDTH), 1) // RET_DK

    def head_mean(val):
        out = jnp.zeros_like(val)
        for hh in range(RET_HEADS):
            m = jnp.sum(jnp.where(head == hh, val, 0.0), axis=-1, keepdims=True) * (1.0 / RET_DK)
            out = jnp.where(head == hh, m, out)
        return out

    oc = o - head_mean(o)
    rn = oc * lax.rsqrt(head_mean(oc * oc) + LN_EPS)
    g = rg_ref[...]
    ret = rn * (g * _sigmoid(g))

    y = jnp.dot(da_ref[...], wo_ref[0:DA_WIDTH, :], preferred_element_type=F32)
    y = y + jnp.dot(pool.astype(BF16), wo_ref[DA_WIDTH:DA_WIDTH + POOL_WIDTH, :], preferred_element_type=F32)
    y = y + jnp.dot(ret.astype(BF16), wo_ref[DA_WIDTH + POOL_WIDTH:, :], preferred_element_type=F32)
    z = alpha * x_ref[...] + mod_ref[:, 2 * d:3 * d] * y
    o_ref[...] = _layer_norm_rows(z) * lng_ref[...] + lnb_ref[...]


def _mixout_call(x, da, u, o_f, o_b, rg, mod3, w_out_bf, pool_bd, pool_scale, ln_g, ln_b, *, tiles_per_batch, seq,
                 alpha):
    r, d = x.shape
    t = ROW_TILE
    nt = r // t
    hb = t // POOL_HALO
    n_halo_blocks = r // POOL_HALO
    row = lambda i: (i, 0)
    const = lambda i: (0, 0)
    kern = functools.partial(_mixout_kernel, tiles_per_batch=tiles_per_batch, seq=seq, alpha=alpha)
    return pl.pallas_call(
        kern,
        grid=(nt,),
        in_specs=[
            pl.BlockSpec((t, d), row),
            pl.BlockSpec((t, DA_WIDTH), row),
            pl.BlockSpec((t, POOL_WIDTH), row),
            pl.BlockSpec((POOL_HALO, POOL_WIDTH), lambda i: (jnp.maximum(i * hb - 1, 0), 0)),
            pl.BlockSpec((POOL_HALO, POOL_WIDTH), lambda i: (jnp.minimum((i + 1) * hb, n_halo_blocks - 1), 0)),
            pl.BlockSpec((t, RET_WIDTH), row),
            pl.BlockSpec((t, RET_WIDTH), row),
            pl.BlockSpec((t, RET_WIDTH), row),
            pl.BlockSpec((None, 1, 6 * d), lambda i: (_mod_row(i, tiles_per_batch), 0, 0)),
            pl.BlockSpec((d, d), const),
            pl.BlockSpec((POOL_WIDTH, POOL_WIDTH), const),
            pl.BlockSpec((1, POOL_WIDTH), const),
            pl.BlockSpec((1, d), const),
            pl.BlockSpec((1, d), const),
        ],
        out_specs=pl.BlockSpec((t, d), row),
        out_shape=jax.ShapeDtypeStruct((r, d), F32),
        compiler_params=_cparams("arbitrary"),
        name="mixer_out",
    )(x, da, u, u, u, o_f, o_b, rg, mod3, w_out_bf, pool_bd, pool_scale, ln_g, ln_b)


def _router_kernel(x_ref, mod_ref, wrh_ref, wrl_ref, bias_ref, wsgu_ref, wsdn_ref,
                   tokp_ref, idx_ref, gate_ref, rank_ref, cnt_ref, fsh_ref, carry_ref):
    d = D_MODEL
    t = x_ref.shape[0]
    ne = N_EXPERTS
    neg = -jnp.inf

    @pl.when(pl.program_id(0) == 0)
    def _():
        carry_ref[...] = jnp.zeros_like(carry_ref)

    tok = _layer_norm_rows(x_ref[...]) * (1.0 + mod_ref[:, 4 * d:5 * d]) + mod_ref[:, 3 * d:4 * d]
    tok_hi = tok.astype(BF16)
    tok_lo = (tok - tok_hi.astype(F32)).astype(BF16)

    tokp_ref[...] = _pack_bf16_pairs(tok)

    hs = jnp.dot(tok_hi, wsgu_ref[...], preferred_element_type=F32)
    gs, us = hs[:, 0:EXPERT_HIDDEN], hs[:, EXPERT_HIDDEN:]
    fsh_ref[...] = jnp.dot((gs * _sigmoid(gs) * us).astype(BF16), wsdn_ref[...], preferred_element_type=F32)

    nt_dims = (((1,), (1,)), ((), ()))
    logits = (lax.dot_general(wrh_ref[...], tok_hi, nt_dims, preferred_element_type=F32)
              + lax.dot_general(wrh_ref[...], tok_lo, nt_dims, preferred_element_type=F32)
              + lax.dot_general(wrl_ref[...], tok_hi, nt_dims, preferred_element_type=F32))
    scores = _sigmoid(logits)
    biased = scores + bias_ref[...]

    gidx = lax.broadcasted_iota(jnp.int32, (GROUP_SIZE, t), 0)
    blocks, gscores = [], []
    for g in range(N_GROUPS):
        blk = biased[g * GROUP_SIZE:(g + 1) * GROUP_SIZE, :]
        m1 = jnp.max(blk, axis=0, keepdims=True)
        first = jnp.min(jnp.where(blk == m1, gidx, GROUP_SIZE), axis=0, keepdims=True)
        m2 = jnp.max(jnp.where(gidx == first, neg, blk), axis=0, keepdims=True)
        blocks.append(blk)
        gscores.append(m1 + m2)

    keep = [jnp.zeros((1, t), F32) for _ in range(N_GROUPS)]
    for _ in range(TOPK_GROUPS):
        m = gscores[0]
        for gs_ in gscores[1:]:
            m = jnp.maximum(m, gs_)
        found = jnp.zeros((1, t), F32)
        for g in range(N_GROUPS):
            hit = jnp.where(gscores[g] == m, 1.0 - found, 0.0)
            found = found + hit
            keep[g] = keep[g] + hit
            gscores[g] = jnp.where(hit > 0.0, neg, gscores[g])
    masked = jnp.concatenate([jnp.where(keep[g] > 0.0, blocks[g], neg) for g in range(N_GROUPS)], axis=0)

    ei = lax.broadcasted_iota(jnp.int32, (ne, t), 0)
    cur = masked
    onehot = jnp.zeros((ne, t), F32)
    idxs, gates = [], []
    for _ in range(TOP_K):
        m = jnp.max(cur, axis=0, keepdims=True)
        ii = jnp.min(jnp.where(cur == m, ei, ne), axis=0, keepdims=True)
        sel = ei == ii
        idxs.append(ii)
        gates.append(jnp.sum(jnp.where(sel, scores, 0.0), axis=0, keepdims=True))
        onehot = jnp.where(sel, 1.0, onehot)
        cur = jnp.where(sel, neg, cur)
    gsum = gates[0]
    for gk in gates[1:]:
        gsum = gsum + gk
    for k in range(TOP_K):
        idx_ref[k:k + 1, :] = idxs[k]
        gate_ref[k:k + 1, :] = gates[k] / gsum * ROUTED_SCALE

    ti = lax.broadcasted_iota(jnp.int32, (t, t), 0)
    tj = lax.broadcasted_iota(jnp.int32, (t, t), 1)
    before = jnp.where(ti < tj, 1.0, 0.0).astype(BF16)
    prefix = jnp.dot(onehot.astype(BF16), before, preferred_element_type=F32) + carry_ref[:, 0:1]
    for k in range(TOP_K):
        rank_k = jnp.sum(jnp.where(ei == idxs[k], prefix, 0.0), axis=0, keepdims=True)
        rank_ref[k:k + 1, :] = rank_k.astype(jnp.int32)
    carry_ref[...] = carry_ref[...] + jnp.sum(onehot, axis=1, keepdims=True)
    cnt_ref[...] = carry_ref[...].astype(jnp.int32)


def _router_call(x, mod3, wr_hi, wr_lo, bias_col, ws_gu_bf, ws_dn_bf, *, tiles_per_batch):
    r, d = x.shape
    t = ROW_TILE
    nt = r // t
    row = lambda i: (i, 0)
    col = lambda i: (0, i)
    const = lambda i: (0, 0)
    return pl.pallas_call(
        _router_kernel,
        grid=(nt,),
        in_specs=[
            pl.BlockSpec((t, d), row),
            pl.BlockSpec((None, 1, 6 * d), lambda i: (_mod_row(i, tiles_per_batch), 0, 0)),
            pl.BlockSpec((N_EXPERTS, d), const),
            pl.BlockSpec((N_EXPERTS, d), const),
            pl.BlockSpec((N_EXPERTS, 1), const),
            pl.BlockSpec((d, 2 * EXPERT_HIDDEN), const),
            pl.BlockSpec((EXPERT_HIDDEN, d), const),
        ],
        out_specs=[
            pl.BlockSpec((t, PACK_W), row),
            pl.BlockSpec((TOP_K, t), col),
            pl.BlockSpec((TOP_K, t), col),
            pl.BlockSpec((TOP_K, t), col),
            pl.BlockSpec((N_EXPERTS, LANES), const),
            pl.BlockSpec((t, d), row),
        ],
        out_shape=[
            jax.ShapeDtypeStruct((r, PACK_W), jnp.int32),
            jax.ShapeDtypeStruct((TOP_K, r), jnp.int32),
            jax.ShapeDtypeStruct((TOP_K, r), F32),
            jax.ShapeDtypeStruct((TOP_K, r), jnp.int32),
            jax.ShapeDtypeStruct((N_EXPERTS, LANES), jnp.int32),
            jax.ShapeDtypeStruct((r, d), F32),
        ],
        scratch_shapes=[pltpu.VMEM((N_EXPERTS, LANES), F32)],
        compiler_params=_cparams("arbitrary"),
        name="router",
    )(x, mod3, wr_hi, wr_lo, bias_col, ws_gu_bf, ws_dn_bf)


def _dest_kernel(idx_ref, rank_ref, offs_ref, dest_ref):
    t = idx_ref.shape[1]
    ei = lax.broadcasted_iota(jnp.int32, (N_EXPERTS, t), 0)
    offs = offs_ref[...].astype(F32)
    for k in range(TOP_K):
        start = jnp.sum(jnp.where(ei == idx_ref[k:k + 1, :], offs, 0.0), axis=0, keepdims=True)
        dest_ref[k:k + 1, :] = start.astype(jnp.int32) + rank_ref[k:k + 1, :]


def _dest_call(idx, rank, offs_col):
    r = idx.shape[1]
    t = r // DEST_STEPS
    assert r % DEST_STEPS == 0 and t % LANES == 0
    col = lambda i: (0, i)
    return pl.pallas_call(
        _dest_kernel,
        grid=(r // t,),
        in_specs=[pl.BlockSpec((TOP_K, t), col), pl.BlockSpec((TOP_K, t), col),
                  pl.BlockSpec((N_EXPERTS, 1), lambda i: (0, 0))],
        out_specs=pl.BlockSpec((TOP_K, t), col),
        out_shape=jax.ShapeDtypeStruct((TOP_K, r), jnp.int32),
        compiler_params=_cparams("arbitrary"),
        name="moe_dest",
    )(idx, rank, offs_col)


def _sc_dispatch(tokp, dest_flat, pad_rows, n_sorted):
    r, width = tokp.shape
    win = SC_GATHER_WINDOW
    workers = SC_NUM_CORES * SC_NUM_SUBCORES
    token_windows = r // win
    n_pad_windows = pad_rows.shape[0] // win
    assert r % win == 0 and dest_flat.shape[0] == TOP_K * r and n_pad_windows % workers == 0
    windows_per_worker = -(-token_windows // workers)
    pads_per_worker = n_pad_windows // workers
    mesh = plsc.VectorSubcoreMesh(core_axis_name="core", subcore_axis_name="subcore", num_cores=SC_NUM_CORES,
                                  num_subcores=SC_NUM_SUBCORES)
    zero_rows = jnp.zeros((win, width), tokp.dtype)

    @functools.partial(
        pl.kernel, out_type=jax.ShapeDtypeStruct((n_sorted + SC_SPARE_ROWS, width), tokp.dtype), mesh=mesh,
        scratch_types=[pltpu.VMEM((win,), jnp.int32), pltpu.VMEM((win, width), tokp.dtype),
                       pltpu.SemaphoreType.DMA],
        name="moe_sc_dispatch")
    def dispatch_kernel(tok_hbm, dest_hbm, pad_hbm, zero_hbm, xs_hbm, idx_vmem, rows_vmem, sem):
        worker = lax.axis_index("subcore") * SC_NUM_CORES + lax.axis_index("core")

        @pl.loop(0, windows_per_worker)
        def _(j):
            window = j * workers + worker

            @pl.when(window < token_windows)
            def _():
                tok0 = window * win
                pltpu.sync_copy(tok_hbm.at[pl.ds(tok0, win)], rows_vmem)
                for k in range(TOP_K):
                    pltpu.sync_copy(dest_hbm.at[pl.ds(k * r + tok0, win)], idx_vmem)
                    pltpu.async_copy(rows_vmem, xs_hbm.at[idx_vmem], sem).wait()

        pltpu.sync_copy(zero_hbm, rows_vmem)

        @pl.loop(0, pads_per_worker)
        def _(j):
            off = (worker * pads_per_worker + j) * win
            pltpu.sync_copy(pad_hbm.at[pl.ds(off, win)], idx_vmem)
            pltpu.async_copy(rows_vmem, xs_hbm.at[idx_vmem], sem).wait()

    return dispatch_kernel(tokp, dest_flat, pad_rows, zero_rows)


def _expert_kernel(be_ref, nb_ref, ord_ref, ue_ref, nue_ref, xs_ref, wgu_hbm, wdn_hbm, ys_ref, wgu_f32, wdn_f32,
                   wgu_bf, wdn_bf, sems, *, layer):
    def weight_copies(o):
        slot = o % 2
        e = ue_ref[o]
        return (pltpu.make_async_copy(wgu_hbm.at[layer, e], wgu_f32.at[slot], sems.at[0, slot]),
                pltpu.make_async_copy(wdn_hbm.at[layer, e], wdn_f32.at[slot], sems.at[1, slot]))

    def start_weights(o):
        @pl.when(o < nue_ref[0])
        def _():
            for cp in weight_copies(o):
                cp.start()

    def one_block(sub, carry):
        j = pl.program_id(0) * EXPERT_BLOCKS_PER_STEP + sub

        @pl.when(j < nb_ref[0])
        def _():
            o = ord_ref[j]
            changed = jnp.logical_or(j == 0, be_ref[j] != be_ref[jnp.maximum(j - 1, 0)])

            @pl.when(j == 0)
            def _():
                start_weights(0)
                start_weights(1)

            @pl.when(changed)
            def _():
                for cp in weight_copies(o):
                    cp.wait()
                slot = o % 2
                wgu_bf[...] = wgu_f32[slot].astype(BF16)
                wdn_bf[...] = wdn_f32[slot].astype(BF16)
                start_weights(o + 2)

            rows = pl.ds(pl.multiple_of(sub * EXPERT_BLOCK, EXPERT_BLOCK), EXPERT_BLOCK)
            x_lo, x_hi = _unpack_bf16_pairs(xs_ref[rows, :])
            h = (jnp.dot(x_lo.astype(BF16), wgu_bf[0:PACK_W, :], preferred_element_type=F32)
                 + jnp.dot(x_hi.astype(BF16), wgu_bf[PACK_W:, :], preferred_element_type=F32))
            g, u = h[:, 0:EXPERT_HIDDEN], h[:, EXPERT_HIDDEN:]
            y = jnp.dot((g * _sigmoid(g) * u).astype(BF16), wdn_bf[...], preferred_element_type=F32)
            ys_ref[rows, :] = _pack_bf16_pairs(y)

        return carry

    lax.fori_loop(0, EXPERT_BLOCKS_PER_STEP, one_block, 0)


def _expert_call(block_expert, n_blocks_used, block_ordinal, used_expert, n_used_experts, xs, w_gu, w_dn, layer):
    n_rows = block_expert.shape[0] * EXPERT_BLOCK
    bm = EXPERT_BLOCK
    d = D_MODEL
    step_rows = bm * EXPERT_BLOCKS_PER_STEP
    assert n_rows % step_rows == 0
    used_step = lambda s, be, nb, od, ue, nue: (jnp.minimum(s, (nb[0] - 1) // EXPERT_BLOCKS_PER_STEP), 0)
    grid_spec = pltpu.PrefetchScalarGridSpec(
        num_scalar_prefetch=5,
        grid=(n_rows // step_rows,),
        in_specs=[
            pl.BlockSpec((step_rows, PACK_W), used_step),
            pl.BlockSpec(memory_space=pl.ANY),
            pl.BlockSpec(memory_space=pl.ANY),
        ],
        out_specs=pl.BlockSpec((step_rows, PACK_W), used_step),
        scratch_shapes=[
            pltpu.VMEM((2, d, 2 * EXPERT_HIDDEN), F32),
            pltpu.VMEM((2, EXPERT_HIDDEN, d), F32),
            pltpu.VMEM((d, 2 * EXPERT_HIDDEN), BF16),
            pltpu.VMEM((EXPERT_HIDDEN, d), BF16),
            pltpu.SemaphoreType.DMA((2, 2)),
        ],
    )
    return pl.pallas_call(
        functools.partial(_expert_kernel, layer=layer),
        grid_spec=grid_spec,
        out_shape=jax.ShapeDtypeStruct((n_rows, PACK_W), jnp.int32),
        compiler_params=_cparams("arbitrary"),
        name="moe_experts",
    )(block_expert, n_blocks_used, block_ordinal, used_expert, n_used_experts, xs, w_gu, w_dn)


def _sc_gather_rows(table, indices):
    n = indices.shape[0]
    width = table.shape[1]
    workers = SC_NUM_CORES * SC_NUM_SUBCORES
    assert n % SC_GATHER_WINDOW == 0
    n_windows = n // SC_GATHER_WINDOW
    mesh = plsc.VectorSubcoreMesh(core_axis_name="core", subcore_axis_name="subcore", num_cores=SC_NUM_CORES,
                                  num_subcores=SC_NUM_SUBCORES)

    @functools.partial(
        pl.kernel, out_type=jax.ShapeDtypeStruct((n, width), table.dtype), mesh=mesh,
        scratch_types=[pltpu.VMEM((SC_GATHER_WINDOW,), jnp.int32),
                       pltpu.VMEM((SC_GATHER_WINDOW, width), table.dtype),
                       pltpu.SemaphoreType.DMA],
        name="moe_sc_gather")
    def gather_kernel(table_hbm, idx_hbm, out_hbm, idx_vmem, rows_vmem, sem):
        worker = lax.axis_index("subcore") * SC_NUM_CORES + lax.axis_index("core")

        @pl.loop(0, -(-n_windows // workers))
        def _(j):
            window = j * workers + worker

            @pl.when(window < n_windows)
            def _():
                off = window * SC_GATHER_WINDOW
                pltpu.sync_copy(idx_hbm.at[pl.ds(off, SC_GATHER_WINDOW)], idx_vmem)
                pltpu.async_copy(table_hbm.at[idx_vmem], rows_vmem, sem).wait()
                pltpu.sync_copy(rows_vmem, out_hbm.at[pl.ds(off, SC_GATHER_WINDOW)])

    return gather_kernel(table, indices)


def _combine_kernel(*refs, alpha):
    y_refs = refs[:TOP_K]
    x_ref, fsh_ref, gate_ref, mod_ref, lng_ref, lnb_ref = refs[TOP_K:TOP_K + 6]
    o_ref = refs[-1]
    d = D_MODEL
    t = x_ref.shape[0]
    gate_rows = gate_ref[...]
    pad = jnp.zeros((LANES - TOP_K, t), F32)
    gate_cols = jnp.concatenate([gate_rows, pad], axis=0).T
    f_lo = fsh_ref[:, 0:PACK_W]
    f_hi = fsh_ref[:, PACK_W:]
    for k in range(TOP_K):
        y_lo, y_hi = _unpack_bf16_pairs(y_refs[k][...])
        f_lo = f_lo + gate_cols[:, k:k + 1] * y_lo
        f_hi = f_hi + gate_cols[:, k:k + 1] * y_hi
    f = jnp.concatenate([f_lo, f_hi], axis=1)
    z = alpha * x_ref[...] + mod_ref[:, 5 * d:6 * d] * f
    o_ref[...] = _layer_norm_rows(z) * lng_ref[...] + lnb_ref[...]


def _combine_call(y_tok, x, fsh, gate, mod3, ln_g, ln_b, prev_out, *, batch_index, tiles_per_batch, alpha,
                  drop_context):
    r, d = x.shape
    t = ROW_TILE
    n_batches = r // t // tiles_per_batch
    n_tiles = tiles_per_batch - 1 if drop_context else tiles_per_batch
    tile0 = batch_index * tiles_per_batch
    row = lambda i: (tile0 + i, 0)
    col = lambda i: (0, tile0 + i)
    const = lambda i: (0, 0)
    kern = functools.partial(_combine_kernel, alpha=alpha)
    y_specs = [pl.BlockSpec((t, PACK_W), functools.partial(lambda k, i: (k * tiles_per_batch + i, 0), k))
               for k in range(TOP_K)]
    in_specs = y_specs + [
        pl.BlockSpec((t, d), row),
        pl.BlockSpec((t, d), row),
        pl.BlockSpec((TOP_K, t), col),
        pl.BlockSpec((None, 1, 6 * d), lambda i: (_mod_row(tile0 + i, tiles_per_batch), 0, 0)),
        pl.BlockSpec((1, d), const),
        pl.BlockSpec((1, d), const),
    ]
    args = [y_tok] * TOP_K + [x, fsh, gate, mod3, ln_g, ln_b]
    aliases = {}
    if prev_out is not None:
        in_specs.append(pl.BlockSpec(memory_space=pl.ANY))
        args.append(prev_out)
        aliases = {len(args) - 1: 0}
    return pl.pallas_call(
        kern,
        grid=(n_tiles,),
        in_specs=in_specs,
        out_specs=pl.BlockSpec((t, d), lambda i: (batch_index * n_tiles + i, 0)),
        out_shape=jax.ShapeDtypeStruct((n_batches * n_tiles * t, d), F32),
        input_output_aliases=aliases,
        compiler_params=_cparams("arbitrary"),
        name="moe_combine",
    )(*args)


def _rope_tables(seq):
    rows = seq // GRID_W
    row = jnp.repeat(jnp.arange(rows, dtype=F32), GRID_W)
    col = jnp.tile(jnp.arange(GRID_W, dtype=F32), rows)
    nf = DA_DIM // 4
    freqs = ROPE_BASE ** (-jnp.arange(nf, dtype=F32) / nf)
    cr, sr = jnp.cos(row[:, None] * freqs), jnp.sin(row[:, None] * freqs)
    cc, sc = jnp.cos(col[:, None] * freqs), jnp.sin(col[:, None] * freqs)
    c64 = jnp.concatenate([cr, cr, cc, cc], axis=1)
    s64 = jnp.concatenate([-sr, sr, -sc, sc], axis=1)
    c = jnp.concatenate([jnp.tile(c64, (1, 2)), jnp.ones((CTX_LEN, LANES), F32)], axis=0)
    s = jnp.concatenate([jnp.tile(s64, (1, 2)), jnp.zeros((CTX_LEN, LANES), F32)], axis=0)
    return c, s


def kernel(x, c, ctx, c_ctx, w_mod, b_mod, w_in, w_out, diff_lambda, pool_w, pool_scale, ret_log_decay, ln_g, ln_b,
           w_router, router_bias, w_expert_gate_up, w_expert_down, w_shared_gate_up, w_shared_down):
    batch, seq, d = x.shape
    depth = w_mod.shape[0]
    assert d == D_MODEL and ctx.shape[1] == CTX_LEN == ROW_TILE and batch == 2
    assert seq % ROW_TILE == 0 and seq % GRID_W == 0 and w_in.shape[-1] == IN_WIDTH
    rows_per_batch = seq + CTX_LEN
    tiles_per_batch = rows_per_batch // ROW_TILE
    r = batch * rows_per_batch
    alpha = (2.0 * depth) ** 0.25

    xa = jnp.concatenate([x, ctx], axis=1).reshape(r, d)
    cvec = jnp.zeros((8, d), F32).at[0:batch].set(c).at[batch].set(c_ctx)
    mod_all = _mod_call(cvec, w_mod, b_mod)
    rope_c, rope_s = _rope_tables(seq)

    n_sorted = r * TOP_K + N_EXPERTS * EXPERT_BLOCK
    n_blocks = n_sorted // EXPERT_BLOCK

    for l in range(depth):
        lambda_init = 0.8 - 0.6 * math.exp(-0.3 * l)
        mod3 = mod_all[l].reshape(8, 1, 6 * d)
        lng = ln_g[l].reshape(2, 1, d)
        lnb = ln_b[l].reshape(2, 1, d)

        w_in_bf = w_in[l].astype(BF16)
        w_vt_bf = w_in_bf[:, QK_WIDTH:QK_WIDTH + DA_WIDTH].T
        qk, vda, u, rqkv, rg = _inproj_call(xa, mod3, w_in_bf, w_vt_bf, rope_c, rope_s, tiles_per_batch)
        da = _attn_call(diff_lambda[l], qk, vda, batch=batch, rows_per_batch=rows_per_batch, seq=seq,
                        lambda_init=lambda_init)
        o_f, o_b = _ret_call(ret_log_decay[l], rqkv, batch=batch, rows_per_batch=rows_per_batch, seq=seq)
        pool_bd = jnp.zeros((POOL_WIDTH, POOL_WIDTH), F32)
        for gi in range(len(POOL_WINDOWS)):
            sl = slice(gi * POOL_GROUP, (gi + 1) * POOL_GROUP)
            pool_bd = pool_bd.at[sl, sl].set(pool_w[l, gi])
        xa = _mixout_call(xa, da, u, o_f, o_b, rg, mod3, w_out[l].astype(BF16), pool_bd.astype(BF16),
                          pool_scale[l].reshape(1, POOL_WIDTH), lng[0], lnb[0],
                          tiles_per_batch=tiles_per_batch, seq=seq, alpha=alpha)

        wr_t = w_router[l].T
        wr_hi = wr_t.astype(BF16)
        wr_lo = (wr_t - wr_hi.astype(F32)).astype(BF16)
        tokp, idx, gate, rank, cnt, fsh = _router_call(
            xa, mod3, wr_hi, wr_lo, router_bias[l].reshape(N_EXPERTS, 1),
            w_shared_gate_up[l].astype(BF16), w_shared_down[l].astype(BF16), tiles_per_batch=tiles_per_batch)
        counts = cnt[:, 0]
        padded = (counts + EXPERT_BLOCK - 1) // EXPERT_BLOCK * EXPERT_BLOCK
        pad_end = jnp.cumsum(padded)
        offs = pad_end - padded
        expert_ids = jnp.arange(N_EXPERTS, dtype=jnp.int32)
        blk_row = jnp.arange(n_blocks, dtype=jnp.int32) * EXPERT_BLOCK
        block_expert = jnp.minimum(jnp.sum(pad_end[None, :] <= blk_row[:, None], axis=1), N_EXPERTS - 1)
        n_used = pad_end[-1:] // EXPERT_BLOCK
        used = counts > 0
        ordinal = jnp.cumsum(used) - 1
        hit = used[None, :] & (ordinal[None, :] == expert_ids[:, None])
        used_expert = jnp.sum(jnp.where(hit, expert_ids[None, :], 0), axis=1)
        n_used_experts = jnp.sum(used)[None]
        block_ordinal = ordinal[block_expert]
        slot = jnp.arange(EXPERT_BLOCK, dtype=jnp.int32)[None, :]
        first_pad = (padded - EXPERT_BLOCK)[:, None] + slot
        is_pad = (first_pad >= counts[:, None]) & (padded[:, None] > 0)
        spare = n_sorted + jnp.arange(N_EXPERTS * EXPERT_BLOCK, dtype=jnp.int32).reshape(N_EXPERTS, EXPERT_BLOCK)
        pad_rows = jnp.where(is_pad, offs[:, None] + first_pad, spare).reshape(N_EXPERTS * EXPERT_BLOCK)
        i32 = lambda a: a.astype(jnp.int32)

        dest = _dest_call(idx, rank, i32(offs).reshape(N_EXPERTS, 1))
        dest_flat = dest.reshape(TOP_K * r)
        xs = _sc_dispatch(tokp, dest_flat, i32(pad_rows), n_sorted)
        ys = _expert_call(i32(block_expert), i32(n_used), i32(block_ordinal), i32(used_expert),
                          i32(n_used_experts), xs, w_expert_gate_up, w_expert_down, l)
        x_new = None
        for b in range(batch):
            dest_b = dest[:, b * rows_per_batch:(b + 1) * rows_per_batch].reshape(TOP_K * rows_per_batch)
            y_tok = _sc_gather_rows(ys, dest_b)
            x_new = _combine_call(y_tok, xa, fsh, gate, mod3, lng[1], lnb[1], x_new, batch_index=b,
                                  tiles_per_batch=tiles_per_batch, alpha=alpha, drop_context=(l == depth - 1))
        xa = x_new

    return xa.reshape(batch, seq, d)
```

```python
import functools
import math

import jax
import jax.numpy as jnp
from jax import lax
from jax.experimental import pallas as pl
from jax.experimental.pallas import tpu as pltpu
from jax.experimental.pallas import tpu_sc as plsc

F32 = jnp.float32
BF16 = jnp.bfloat16
HIGHEST = lax.Precision.HIGHEST

D_MODEL = 1024
CTX_LEN = 256
GRID_W = 64
DA_HEADS = 4
DA_DIM = 64
DA_VDIM = 2 * DA_DIM
DA_WIDTH = DA_HEADS * DA_VDIM
ROPE_BASE = 10000.0
POOL_WINDOWS = (2, 4, 8, 16)
POOL_GROUP = 64
POOL_WIDTH = len(POOL_WINDOWS) * POOL_GROUP
POOL_HALO = 8
RET_HEADS = 4
RET_DK = 64
RET_WIDTH = RET_HEADS * RET_DK
RET_CHUNK = 128
QK_WIDTH = 2 * DA_HEADS * 2 * DA_DIM
IN_WIDTH = QK_WIDTH + DA_WIDTH + POOL_WIDTH + 4 * RET_WIDTH
N_EXPERTS = 256
TOP_K = 8
N_GROUPS = 8
GROUP_SIZE = N_EXPERTS // N_GROUPS
TOPK_GROUPS = 4
EXPERT_HIDDEN = 256
ROUTED_SCALE = 2.5
LN_EPS = 1e-6
RMS_EPS = 1e-5

LANES = 128
ROW_TILE = 256
MOD_COL_TILE = 1536
DEST_STEPS = 4
ATTN_Q_TILE = 256
ATTN_K_CHUNK = 256
ATTN_SCORE_GROUP = 2
ATTN_UNROLL = 16
SC_NUM_CORES = 2
SC_NUM_SUBCORES = 16
SC_GATHER_WINDOW = 128
EXPERT_BLOCK = 256
EXPERT_BLOCKS_PER_STEP = 4
SC_SPARE_ROWS = N_EXPERTS * EXPERT_BLOCK
PACK_W = D_MODEL // 2
VMEM_LIMIT = 56 * 1024 * 1024


def _cparams(*sem):
    return pltpu.CompilerParams(dimension_semantics=sem, vmem_limit_bytes=VMEM_LIMIT)


def _sigmoid(x):
    return 1.0 / (1.0 + jnp.exp(-x))


def _layer_norm_rows(x):
    mu = jnp.mean(x, axis=-1, keepdims=True)
    xc = x - mu
    var = jnp.mean(xc * xc, axis=-1, keepdims=True)
    return xc * lax.rsqrt(var + LN_EPS)


def _pack_bf16_pairs(x):
    half = x.shape[1] // 2
    bits = pltpu.bitcast(x.astype(BF16).astype(F32), jnp.uint32)
    word = lax.shift_right_logical(bits[:, 0:half], jnp.uint32(16)) | (bits[:, half:] & jnp.uint32(0xFFFF0000))
    return pltpu.bitcast(word, jnp.int32)


def _unpack_bf16_pairs(packed):
    word = pltpu.bitcast(packed, jnp.uint32)
    lo = pltpu.bitcast(lax.shift_left(word, jnp.uint32(16)), F32)
    hi = pltpu.bitcast(word & jnp.uint32(0xFFFF0000), F32)
    return lo, hi


def _mod_row(i, tiles_per_batch):
    return jnp.where(i % tiles_per_batch == tiles_per_batch - 1, 2, i // tiles_per_batch)


def _mod_kernel(c_ref, w_ref, b_ref, o_ref):
    c = c_ref[...]
    s = c * _sigmoid(c)
    o_ref[...] = jnp.dot(s, w_ref[...], precision=HIGHEST, preferred_element_type=F32) + b_ref[...]


def _mod_call(cvec, w_mod, b_mod):
    depth, d, n = w_mod.shape
    tn = MOD_COL_TILE
    assert n % tn == 0
    return pl.pallas_call(
        _mod_kernel,
        grid=(depth, n // tn),
        in_specs=[
            pl.BlockSpec((8, d), lambda l, j: (0, 0)),
            pl.BlockSpec((None, d, tn), lambda l, j: (l, 0, j)),
            pl.BlockSpec((None, 1, tn), lambda l, j: (l, 0, j)),
        ],
        out_specs=pl.BlockSpec((None, 8, tn), lambda l, j: (l, 0, j)),
        out_shape=jax.ShapeDtypeStruct((depth, 8, n), F32),
        compiler_params=_cparams("arbitrary", "arbitrary"),
        name="mod",
    )(cvec, w_mod, b_mod.reshape(depth, 1, n))


def _inproj_kernel(x_ref, mod_ref, w_ref, wvt_ref, ct_ref, st_ref, qk_ref, vt_ref, u_ref, r_ref, g_ref):
    d = D_MODEL
    xn = _layer_norm_rows(x_ref[...])
    h = (xn * (1.0 + mod_ref[:, d:2 * d]) + mod_ref[:, 0:d]).astype(BF16)

    a = jnp.dot(h, w_ref[:, 0:QK_WIDTH], preferred_element_type=F32)
    lane = lax.broadcasted_iota(jnp.int32, (a.shape[0], LANES), 1)
    first_half = (lane % 32) < 16
    ct = ct_ref[...]
    st = st_ref[...]
    for s in range(QK_WIDTH // LANES):
        blk = a[:, s * LANES:(s + 1) * LANES]
        partner = jnp.where(first_half, pltpu.roll(blk, LANES - 16, 1), pltpu.roll(blk, 16, 1))
        rot = blk * ct + partner * st
        if s < QK_WIDTH // LANES // 2:
            rot = rot * (DA_DIM ** -0.5 * math.log2(math.e))
        qk_ref[:, s * LANES:(s + 1) * LANES] = rot.astype(BF16)

    vt_ref[...] = lax.dot_general(wvt_ref[...], h, (((1,), (1,)), ((), ())),
                                  preferred_element_type=F32).astype(BF16)
    o = QK_WIDTH + DA_WIDTH
    u_ref[...] = jnp.dot(h, w_ref[:, o:o + POOL_WIDTH], preferred_element_type=F32)
    o += POOL_WIDTH
    r = jnp.dot(h, w_ref[:, o:o + 3 * RET_WIDTH], preferred_element_type=F32)
    r_ref[:, 0:RET_WIDTH] = r[:, 0:RET_WIDTH].astype(BF16)
    r_ref[:, RET_WIDTH:2 * RET_WIDTH] = (r[:, RET_WIDTH:2 * RET_WIDTH] * (RET_DK ** -0.5)).astype(BF16)
    r_ref[:, 2 * RET_WIDTH:] = r[:, 2 * RET_WIDTH:].astype(BF16)
    o += 3 * RET_WIDTH
    g_ref[...] = jnp.dot(h, w_ref[:, o:o + RET_WIDTH], preferred_element_type=F32)


def _inproj_call(x, mod3, w_in_bf, w_vt_bf, rope_c, rope_s, tiles_per_batch):
    r, d = x.shape
    t = ROW_TILE
    nt = r // t
    row = lambda i: (i, 0)
    return pl.pallas_call(
        _inproj_kernel,
        grid=(nt,),
        in_specs=[
            pl.BlockSpec((t, d), row),
            pl.BlockSpec((None, 1, 6 * d), lambda i: (_mod_row(i, tiles_per_batch), 0, 0)),
            pl.BlockSpec((d, IN_WIDTH), lambda i: (0, 0)),
            pl.BlockSpec((DA_WIDTH, d), lambda i: (0, 0)),
            pl.BlockSpec((t, LANES), lambda i: (i % tiles_per_batch, 0)),
            pl.BlockSpec((t, LANES), lambda i: (i % tiles_per_batch, 0)),
        ],
        out_specs=[
            pl.BlockSpec((t, QK_WIDTH), row),
            pl.BlockSpec((DA_WIDTH, t), lambda i: (0, i)),
            pl.BlockSpec((t, POOL_WIDTH), row),
            pl.BlockSpec((t, 3 * RET_WIDTH), row),
            pl.BlockSpec((t, RET_WIDTH), row),
        ],
        out_shape=[
            jax.ShapeDtypeStruct((r, QK_WIDTH), BF16),
            jax.ShapeDtypeStruct((DA_WIDTH, r), BF16),
            jax.ShapeDtypeStruct((r, POOL_WIDTH), F32),
            jax.ShapeDtypeStruct((r, 3 * RET_WIDTH), BF16),
            jax.ShapeDtypeStruct((r, RET_WIDTH), F32),
        ],
        compiler_params=_cparams("arbitrary"),
        name="inproj",
    )(x, mod3, w_in_bf, w_vt_bf, rope_c, rope_s)


def _attn_kernel(lam_ref, q_ref, k_ref, vt_ref, o_ref, *s_refs, k_chunk, seq, lambda_init):
    mq = ATTN_Q_TILE
    n_tiles = (seq + CTX_LEN) // mq
    n_chunks = (seq + CTX_LEN) // k_chunk
    last = n_chunks - 1
    n_iters = last // ATTN_UNROLL
    neg_inf = jnp.full((1, 2 * mq), -jnp.inf, F32)
    acc_zero = jnp.zeros((DA_VDIM + 16, 2 * mq), F32)
    ones_rows = jnp.where(lax.broadcasted_iota(jnp.int32, (16, k_chunk), 0) == 0, 1.0, 0.0).astype(BF16)

    def tile_rows(i):
        return pl.ds(pl.multiple_of(i * mq, mq), mq)

    def q_transposed(i):
        q = q_ref[tile_rows(i), :]
        lane = lax.broadcasted_iota(jnp.int32, q.shape, 1)
        zero = jnp.zeros_like(q)
        q2 = jnp.concatenate([jnp.where(lane < DA_DIM, q, zero), jnp.where(lane >= DA_DIM, q, zero)], axis=0)
        return q2.astype(F32).T.astype(BF16)

    def score_chunk(s_ref, c, qt, m):
        off = pl.multiple_of(c * k_chunk, k_chunk)
        s = jnp.dot(k_ref[pl.ds(off, k_chunk), :], qt, preferred_element_type=F32)
        s_ref[c] = s
        return jnp.maximum(m, jnp.max(s, axis=0, keepdims=True))

    def score_chunk_group(s_ref, c, qt, m):
        g = ATTN_SCORE_GROUP
        off = pl.multiple_of(c * k_chunk, g * k_chunk)
        s = jnp.dot(k_ref[pl.ds(off, g * k_chunk), :], qt, preferred_element_type=F32)
        for j in range(g):
            s_ref[c + j] = s[j * k_chunk:(j + 1) * k_chunk]
        return jnp.maximum(m, jnp.max(s, axis=0, keepdims=True))

    def value_chunk(s_ref, c, m, acc):
        off = pl.multiple_of(c * k_chunk, k_chunk)
        vt = jnp.concatenate([vt_ref[:, pl.ds(off, k_chunk)], ones_rows], axis=0)
        p = jnp.exp2((s_ref[c] - m).astype(BF16))
        return acc + jnp.dot(vt, p, preferred_element_type=F32)

    def value_chunk_group(s_ref, c, m, acc):
        g = ATTN_SCORE_GROUP
        off = pl.multiple_of(c * k_chunk, g * k_chunk)
        vt = jnp.concatenate([vt_ref[:, pl.ds(off, g * k_chunk)], jnp.concatenate([ones_rows] * g, axis=1)], axis=0)
        p = jnp.concatenate([jnp.exp2((s_ref[c + j] - m).astype(BF16)) for j in range(g)], axis=0)
        return acc + jnp.dot(vt, p, preferred_element_type=F32)

    def finish(i, acc):
        l0, l1 = acc[DA_VDIM:DA_VDIM + 1, 0:mq], acc[DA_VDIM:DA_VDIM + 1, mq:]
        a0, a1 = acc[0:DA_VDIM, 0:mq], acc[0:DA_VDIM, mq:]
        lv = lam_ref[...]
        lam = (jnp.exp(jnp.sum(lv[0:1] * lv[1:2], axis=-1, keepdims=True))
               - jnp.exp(jnp.sum(lv[2:3] * lv[3:4], axis=-1, keepdims=True)) + lambda_init)
        o = a0 / l0 - lam * (a1 / l1)
        o = o * lax.rsqrt(jnp.mean(o * o, axis=0, keepdims=True) + RMS_EPS) * (1.0 - lambda_init)
        o_ref[tile_rows(i), :] = o.T.astype(BF16)

    def scores_only(s_ref, qt):
        def body(it, m):
            for u in range(ATTN_UNROLL):
                m = score_chunk(s_ref, it * ATTN_UNROLL + u, qt, m)
            return m
        return score_chunk(s_ref, last, qt, lax.fori_loop(0, n_iters, body, neg_inf))

    def values_only(s_ref, m):
        def body(it, acc):
            for u in range(ATTN_UNROLL):
                acc = value_chunk(s_ref, it * ATTN_UNROLL + u, m, acc)
            return acc
        return value_chunk(s_ref, last, m, lax.fori_loop(0, n_iters, body, acc_zero))

    def fused_tile(i, m_prev, s_cur, s_prev):
        qt = q_transposed(i)

        def body(it, carry):
            m, acc = carry
            for u in range(0, ATTN_UNROLL, ATTN_SCORE_GROUP):
                c = it * ATTN_UNROLL + u
                m = score_chunk_group(s_cur, c, qt, m)
                acc = value_chunk_group(s_prev, c, m_prev, acc)
            return m, acc

        m, acc = lax.fori_loop(0, n_iters, body, (neg_inf, acc_zero))
        m = score_chunk(s_cur, last, qt, m)
        finish(i - 1, value_chunk(s_prev, last, m_prev, acc))
        return m

    s_even, s_odd = s_refs
    ctx_tile = n_tiles - 1
    assert ctx_tile % 2 == 0 and ctx_tile >= 2
    m = scores_only(s_even, q_transposed(0))

    def tile_pair(p, m):
        m = fused_tile(2 * p + 1, m, s_odd, s_even)
        return fused_tile(2 * p + 2, m, s_even, s_odd)

    m = lax.fori_loop(0, (ctx_tile - 2) // 2, tile_pair, m)
    m = fused_tile(ctx_tile - 1, m, s_odd, s_even)
    m_ctx = score_chunk(s_even, last, q_transposed(ctx_tile), neg_inf)
    finish(ctx_tile - 1, values_only(s_odd, m))
    finish(ctx_tile, value_chunk(s_even, last, m_ctx, acc_zero))


def _attn_call(lam_vec, qk, vda, *, batch, rows_per_batch, seq, lambda_init):
    tq = ATTN_Q_TILE
    assert seq % (ATTN_K_CHUNK * ATTN_UNROLL) == 0 and rows_per_batch - seq == CTX_LEN == tq == ATTN_K_CHUNK
    nq = rows_per_batch // tq
    kern = functools.partial(_attn_kernel, k_chunk=ATTN_K_CHUNK, seq=seq, lambda_init=lambda_init)
    return pl.pallas_call(
        kern,
        grid=(batch, DA_HEADS),
        in_specs=[
            pl.BlockSpec((4, DA_DIM), lambda b, h: (0, 0)),
            pl.BlockSpec((rows_per_batch, DA_VDIM), lambda b, h: (b, h)),
            pl.BlockSpec((rows_per_batch, DA_VDIM), lambda b, h: (b, DA_HEADS + h)),
            pl.BlockSpec((DA_VDIM, rows_per_batch), lambda b, h: (h, b)),
        ],
        out_specs=pl.BlockSpec((rows_per_batch, DA_VDIM), lambda b, h: (b, h)),
        out_shape=jax.ShapeDtypeStruct((qk.shape[0], DA_WIDTH), BF16),
        scratch_shapes=[pltpu.VMEM((rows_per_batch // ATTN_K_CHUNK, ATTN_K_CHUNK, 2 * tq), F32)] * 2,
        compiler_params=_cparams("arbitrary", "arbitrary"),
        name="diff_attn",
    )(lam_vec, qk, qk, vda)


def _ret_kernel(ld_ref, f_ref, b_ref, of_ref, ob_ref, dm_ref, qd_ref, kd_ref, cd_ref, st_ref):
    c = pl.program_id(1)
    ch = RET_CHUNK
    w = RET_WIDTH
    lane_head = lax.broadcasted_iota(jnp.int32, (1, w), 1) // RET_DK

    @pl.when(c == 0)
    def _():
        st_ref[...] = jnp.zeros_like(st_ref)
        ri = lax.broadcasted_iota(jnp.int32, (ch, ch), 0)
        ci = lax.broadcasted_iota(jnp.int32, (ch, ch), 1)
        rowf = lax.broadcasted_iota(jnp.int32, (ch, w), 0).astype(F32)
        for d in range(2):
            lg_lane = jnp.zeros((1, w), F32)
            for hh in range(RET_HEADS):
                lg = -jnp.exp(jnp.full((1, 1), ld_ref[d, hh], F32))
                lg_lane = jnp.where(lane_head == hh, lg, lg_lane)
                dist = ((ri - ci) if d == 0 else (ci - ri)).astype(F32)
                dm_ref[d, hh] = jnp.where(dist >= 0, jnp.exp(dist * lg), 0.0)
            if d == 0:
                qd_ref[d] = jnp.exp((rowf + 1.0) * lg_lane)
                kd_ref[d] = jnp.exp((ch - 1.0 - rowf) * lg_lane)
            else:
                qd_ref[d] = jnp.exp((ch - rowf) * lg_lane)
                kd_ref[d] = jnp.exp(rowf * lg_lane)
            cd_ref[d] = jnp.exp(float(ch) * lg_lane)

    rblk = lax.broadcasted_iota(jnp.int32, (w, w), 0) // RET_DK
    cblk = lax.broadcasted_iota(jnp.int32, (w, w), 1) // RET_DK
    for d, (src, dst) in enumerate(((f_ref, of_ref), (b_ref, ob_ref))):
        q = src[:, 0:w]
        k = src[:, w:2 * w]
        v = src[:, 2 * w:3 * w]
        st = st_ref[d]
        o = jnp.dot((q.astype(F32) * qd_ref[d]).astype(BF16), st.astype(BF16), preferred_element_type=F32)
        for hh in range(RET_HEADS):
            in_head = lane_head == hh
            qm = jnp.where(in_head, q, jnp.zeros_like(q))
            s = lax.dot_general(qm, k, (((1,), (1,)), ((), ())), preferred_element_type=F32)
            intra = (s * dm_ref[d, hh]).astype(BF16)
            o = o + jnp.where(in_head, jnp.dot(intra, v, preferred_element_type=F32), 0.0)
        dst[...] = o
        kk_t = (k.astype(F32) * kd_ref[d]).T.astype(BF16)
        upd = jnp.dot(kk_t, v, preferred_element_type=F32)
        st_ref[d] = jnp.where(rblk == cblk, st * cd_ref[d] + upd, 0.0)


def _ret_call(log_decay, rqkv, *, batch, rows_per_batch, seq):
    ch = RET_CHUNK
    nc = rows_per_batch // ch
    n_lat = seq // ch
    n_ctx = nc - n_lat

    def fwd(b, c):
        return (b * nc + jnp.where(c < n_ctx, n_lat + c, c - n_ctx), 0)

    def bwd(b, c):
        return (b * nc + nc - 1 - c, 0)

    w = RET_WIDTH
    return pl.pallas_call(
        _ret_kernel,
        grid=(batch, nc),
        in_specs=[
            pl.BlockSpec(memory_space=pltpu.SMEM),
            pl.BlockSpec((ch, 3 * w), fwd),
            pl.BlockSpec((ch, 3 * w), bwd),
        ],
        out_specs=[pl.BlockSpec((ch, w), fwd), pl.BlockSpec((ch, w), bwd)],
        out_shape=[jax.ShapeDtypeStruct((rqkv.shape[0], w), F32)] * 2,
        scratch_shapes=[
            pltpu.VMEM((2, RET_HEADS, ch, ch), F32),
            pltpu.VMEM((2, ch, w), F32),
            pltpu.VMEM((2, ch, w), F32),
            pltpu.VMEM((2, 1, w), F32),
            pltpu.VMEM((2, w, w), F32),
        ],
        compiler_params=_cparams("arbitrary", "arbitrary"),
        name="retention",
    )(log_decay, rqkv, rqkv)


def _mixout_kernel(x_ref, da_ref, u_ref, up_ref, un_ref, of_ref, ob_ref, rg_ref, mod_ref, wo_ref, pw_ref,
                   ps_ref, lng_ref, lnb_ref, o_ref, *, tiles_per_batch, seq, alpha):
    d = D_MODEL
    t = x_ref.shape[0]
    i = pl.program_id(0)
    j = i % tiles_per_batch
    is_ctx = j == tiles_per_batch - 1
    stream_len = jnp.where(is_ctx, CTX_LEN, seq)
    p0 = jnp.where(is_ctx, 0, j * t)

    u = u_ref[...]
    prev = jnp.where(p0 > 0, up_ref[...], 0.0)
    nxt = jnp.where(p0 + t < stream_len, un_ref[...], 0.0)
    ext = jnp.concatenate([prev, u, nxt], axis=0)
    n = t + 2 * POOL_HALO
    a2 = ext + pltpu.roll(ext, 1, 0)
    a4 = pltpu.roll(a2, 1, 0) + pltpu.roll(a2, n - 1, 0)
    a8 = pltpu.roll(a4, 2, 0) + pltpu.roll(a4, n - 2, 0)
    a16 = pltpu.roll(a8, 4, 0) + pltpu.roll(a8, n - 4, 0)
    pos = p0 + lax.broadcasted_iota(jnp.int32, (t, POOL_WIDTH), 0)
    group = lax.broadcasted_iota(jnp.int32, (1, POOL_WIDTH), 1) // POOL_GROUP
    mean = jnp.zeros((t, POOL_WIDTH), F32)
    for gi, (wnd, asum) in enumerate(zip(POOL_WINDOWS, (a2, a4, a8, a16))):
        cnt = jnp.minimum(pos + wnd // 2, stream_len) - jnp.maximum(pos - wnd // 2, 0)
        mean = jnp.where(group == gi, asum[POOL_HALO:POOL_HALO + t] / cnt.astype(F32), mean)
    pool = jnp.dot((mean - u).astype(BF16), pw_ref[...], preferred_element_type=F32) * ps_ref[...]

    o = of_ref[...] + ob_ref[...]
    head = lax.broadcasted_iota(jnp.int32, (1, RET_WIDTH), 1) // RET_DK

    def head_mean(val):
        out = jnp.zeros_like(val)
        for hh in range(RET_HEADS):
            m = jnp.sum(jnp.where(head == hh, val, 0.0), axis=-1, keepdims=True) * (1.0 / RET_DK)
            out = jnp.where(head == hh, m, out)
        return out

    oc = o - head_mean(o)
    rn = oc * lax.rsqrt(head_mean(oc * oc) + LN_EPS)
    g = rg_ref[...]
    ret = rn * (g * _sigmoid(g))

    y = jnp.dot(da_ref[...], wo_ref[0:DA_WIDTH, :], preferred_element_type=F32)
    y = y + jnp.dot(pool.astype(BF16), wo_ref[DA_WIDTH:DA_WIDTH + POOL_WIDTH, :], preferred_element_type=F32)
    y = y + jnp.dot(ret.astype(BF16), wo_ref[DA_WIDTH + POOL_WIDTH:, :], preferred_element_type=F32)
    z = alpha * x_ref[...] + mod_ref[:, 2 * d:3 * d] * y
    o_ref[...] = _layer_norm_rows(z) * lng_ref[...] + lnb_ref[...]


def _mixout_call(x, da, u, o_f, o_b, rg, mod3, w_out_bf, pool_bd, pool_scale, ln_g, ln_b, *, tiles_per_batch, seq,
                 alpha):
    r, d = x.shape
    t = ROW_TILE
    nt = r // t
    hb = t // POOL_HALO
    n_halo_blocks = r // POOL_HALO
    row = lambda i: (i, 0)
    const = lambda i: (0, 0)
    kern = functools.partial(_mixout_kernel, tiles_per_batch=tiles_per_batch, seq=seq, alpha=alpha)
    return pl.pallas_call(
        kern,
        grid=(nt,),
        in_specs=[
            pl.BlockSpec((t, d), row),
            pl.BlockSpec((t, DA_WIDTH), row),
            pl.BlockSpec((t, POOL_WIDTH), row),
            pl.BlockSpec((POOL_HALO, POOL_WIDTH), lambda i: (jnp.maximum(i * hb - 1, 0), 0)),
            pl.BlockSpec((POOL_HALO, POOL_WIDTH), lambda i: (jnp.minimum((i + 1) * hb, n_halo_blocks - 1), 0)),
            pl.BlockSpec((t, RET_WIDTH), row),
            pl.BlockSpec((t, RET_WIDTH), row),
            pl.BlockSpec((t, RET_WIDTH), row),
            pl.BlockSpec((None, 1, 6 * d), lambda i: (_mod_row(i, tiles_per_batch), 0, 0)),
            pl.BlockSpec((d, d), const),
            pl.BlockSpec((POOL_WIDTH, POOL_WIDTH), const),
            pl.BlockSpec((1, POOL_WIDTH), const),
            pl.BlockSpec((1, d), const),
            pl.BlockSpec((1, d), const),
        ],
        out_specs=pl.BlockSpec((t, d), row),
        out_shape=jax.ShapeDtypeStruct((r, d), F32),
        compiler_params=_cparams("arbitrary"),
        name="mixer_out",
    )(x, da, u, u, u, o_f, o_b, rg, mod3, w_out_bf, pool_bd, pool_scale, ln_g, ln_b)


def _router_kernel(x_ref, mod_ref, wrh_ref, wrl_ref, bias_ref, wsgu_ref, wsdn_ref,
                   tokp_ref, idx_ref, gate_ref, rank_ref, cnt_ref, fsh_ref, carry_ref):
    d = D_MODEL
    t = x_ref.shape[0]
    ne = N_EXPERTS
    neg = -jnp.inf

    @pl.when(pl.program_id(0) == 0)
    def _():
        carry_ref[...] = jnp.zeros_like(carry_ref)

    tok = _layer_norm_rows(x_ref[...]) * (1.0 + mod_ref[:, 4 * d:5 * d]) + mod_ref[:, 3 * d:4 * d]
    tok_hi = tok.astype(BF16)
    tok_lo = (tok - tok_hi.astype(F32)).astype(BF16)

    tokp_ref[...] = _pack_bf16_pairs(tok)

    hs = jnp.dot(tok_hi, wsgu_ref[...], preferred_element_type=F32)
    gs, us = hs[:, 0:EXPERT_HIDDEN], hs[:, EXPERT_HIDDEN:]
    fsh_ref[...] = jnp.dot((gs * _sigmoid(gs) * us).astype(BF16), wsdn_ref[...], preferred_element_type=F32)

    nt_dims = (((1,), (1,)), ((), ()))
    logits = (lax.dot_general(wrh_ref[...], tok_hi, nt_dims, preferred_element_type=F32)
              + lax.dot_general(wrh_ref[...], tok_lo, nt_dims, preferred_element_type=F32)
              + lax.dot_general(wrl_ref[...], tok_hi, nt_dims, preferred_element_type=F32))
    scores = _sigmoid(logits)
    biased = scores + bias_ref[...]

    gidx = lax.broadcasted_iota(jnp.int32, (GROUP_SIZE, t), 0)
    blocks, gscores = [], []
    for g in range(N_GROUPS):
        blk = biased[g * GROUP_SIZE:(g + 1) * GROUP_SIZE, :]
        m1 = jnp.max(blk, axis=0, keepdims=True)
        first = jnp.min(jnp.where(blk == m1, gidx, GROUP_SIZE), axis=0, keepdims=True)
        m2 = jnp.max(jnp.where(gidx == first, neg, blk), axis=0, keepdims=True)
        blocks.append(blk)
        gscores.append(m1 + m2)

    keep = [jnp.zeros((1, t), F32) for _ in range(N_GROUPS)]
    for _ in range(TOPK_GROUPS):
        m = gscores[0]
        for gs_ in gscores[1:]:
            m = jnp.maximum(m, gs_)
        found = jnp.zeros((1, t), F32)
        for g in range(N_GROUPS):
            hit = jnp.where(gscores[g] == m, 1.0 - found, 0.0)
            found = found + hit
            keep[g] = keep[g] + hit
            gscores[g] = jnp.where(hit > 0.0, neg, gscores[g])
    masked = jnp.concatenate([jnp.where(keep[g] > 0.0, blocks[g], neg) for g in range(N_GROUPS)], axis=0)

    ei = lax.broadcasted_iota(jnp.int32, (ne, t), 0)
    cur = masked
    onehot = jnp.zeros((ne, t), F32)
    idxs, gates = [], []
    for _ in range(TOP_K):
        m = jnp.max(cur, axis=0, keepdims=True)
        ii = jnp.min(jnp.where(cur == m, ei, ne), axis=0, keepdims=True)
        sel = ei == ii
        idxs.append(ii)
        gates.append(jnp.sum(jnp.where(sel, scores, 0.0), axis=0, keepdims=True))
        onehot = jnp.where(sel, 1.0, onehot)
        cur = jnp.where(sel, neg, cur)
    gsum = gates[0]
    for gk in gates[1:]:
        gsum = gsum + gk
    for k in range(TOP_K):
        idx_ref[k:k + 1, :] = idxs[k]
        gate_ref[k:k + 1, :] = gates[k] / gsum * ROUTED_SCALE

    ti = lax.broadcasted_iota(jnp.int32, (t, t), 0)
    tj = lax.broadcasted_iota(jnp.int32, (t, t), 1)
    before = jnp.where(ti < tj, 1.0, 0.0).astype(BF16)
    prefix = jnp.dot(onehot.astype(BF16), before, preferred_element_type=F32) + carry_ref[:, 0:1]
    for k in range(TOP_K):
        rank_k = jnp.sum(jnp.where(ei == idxs[k], prefix, 0.0), axis=0, keepdims=True)
        rank_ref[k:k + 1, :] = rank_k.astype(jnp.int32)
    carry_ref[...] = carry_ref[...] + jnp.sum(onehot, axis=1, keepdims=True)
    cnt_ref[...] = carry_ref[...].astype(jnp.int32)


def _router_call(x, mod3, wr_hi, wr_lo, bias_col, ws_gu_bf, ws_dn_bf, *, tiles_per_batch):
    r, d = x.shape
    t = ROW_TILE
    nt = r // t
    row = lambda i: (i, 0)
    col = lambda i: (0, i)
    const = lambda i: (0, 0)
    return pl.pallas_call(
        _router_kernel,
        grid=(nt,),
        in_specs=[
            pl.BlockSpec((t, d), row),
            pl.BlockSpec((None, 1, 6 * d), lambda i: (_mod_row(i, tiles_per_batch), 0, 0)),
            pl.BlockSpec((N_EXPERTS, d), const),
            pl.BlockSpec((N_EXPERTS, d), const),
            pl.BlockSpec((N_EXPERTS, 1), const),
            pl.BlockSpec((d, 2 * EXPERT_HIDDEN), const),
            pl.BlockSpec((EXPERT_HIDDEN, d), const),
        ],
        out_specs=[
            pl.BlockSpec((t, PACK_W), row),
            pl.BlockSpec((TOP_K, t), col),
            pl.BlockSpec((TOP_K, t), col),
            pl.BlockSpec((TOP_K, t), col),
            pl.BlockSpec((N_EXPERTS, LANES), const),
            pl.BlockSpec((t, d), row),
        ],
        out_shape=[
            jax.ShapeDtypeStruct((r, PACK_W), jnp.int32),
            jax.ShapeDtypeStruct((TOP_K, r), jnp.int32),
            jax.ShapeDtypeStruct((TOP_K, r), F32),
            jax.ShapeDtypeStruct((TOP_K, r), jnp.int32),
            jax.ShapeDtypeStruct((N_EXPERTS, LANES), jnp.int32),
            jax.ShapeDtypeStruct((r, d), F32),
        ],
        scratch_shapes=[pltpu.VMEM((N_EXPERTS, LANES), F32)],
        compiler_params=_cparams("arbitrary"),
        name="router",
    )(x, mod3, wr_hi, wr_lo, bias_col, ws_gu_bf, ws_dn_bf)


def _dest_kernel(idx_ref, rank_ref, offs_ref, dest_ref):
    t = idx_ref.shape[1]
    ei = lax.broadcasted_iota(jnp.int32, (N_EXPERTS, t), 0)
    offs = offs_ref[...].astype(F32)
    for k in range(TOP_K):
        start = jnp.sum(jnp.where(ei == idx_ref[k:k + 1, :], offs, 0.0), axis=0, keepdims=True)
        dest_ref[k:k + 1, :] = start.astype(jnp.int32) + rank_ref[k:k + 1, :]


def _dest_call(idx, rank, offs_col):
    r = idx.shape[1]
    t = r // DEST_STEPS
    assert r % DEST_STEPS == 0 and t % LANES == 0
    col = lambda i: (0, i)
    return pl.pallas_call(
        _dest_kernel,
        grid=(r // t,),
        in_specs=[pl.BlockSpec((TOP_K, t), col), pl.BlockSpec((TOP_K, t), col),
                  pl.BlockSpec((N_EXPERTS, 1), lambda i: (0, 0))],
        out_specs=pl.BlockSpec((TOP_K, t), col),
        out_shape=jax.ShapeDtypeStruct((TOP_K, r), jnp.int32),
        compiler_params=_cparams("arbitrary"),
        name="moe_dest",
    )(idx, rank, offs_col)


def _sc_dispatch(tokp, dest_flat, pad_rows, n_sorted):
    r, width = tokp.shape
    win = SC_GATHER_WINDOW
    workers = SC_NUM_CORES * SC_NUM_SUBCORES
    token_windows = r // win
    n_pad_windows = pad_rows.shape[0] // win
    assert r % win == 0 and dest_flat.shape[0] == TOP_K * r and n_pad_windows % workers == 0
    windows_per_worker = -(-token_windows // workers)
    pads_per_worker = n_pad_windows // workers
    mesh = plsc.VectorSubcoreMesh(core_axis_name="core", subcore_axis_name="subcore", num_cores=SC_NUM_CORES,
                                  num_subcores=SC_NUM_SUBCORES)
    zero_rows = jnp.zeros((win, width), tokp.dtype)

    @functools.partial(
        pl.kernel, out_type=jax.ShapeDtypeStruct((n_sorted + SC_SPARE_ROWS, width), tokp.dtype), mesh=mesh,
        scratch_types=[pltpu.VMEM((win,), jnp.int32), pltpu.VMEM((win, width), tokp.dtype),
                       pltpu.SemaphoreType.DMA],
        name="moe_sc_dispatch")
    def dispatch_kernel(tok_hbm, dest_hbm, pad_hbm, zero_hbm, xs_hbm, idx_vmem, rows_vmem, sem):
        worker = lax.axis_index("subcore") * SC_NUM_CORES + lax.axis_index("core")

        @pl.loop(0, windows_per_worker)
        def _(j):
            window = j * workers + worker

            @pl.when(window < token_windows)
            def _():
                tok0 = window * win
                pltpu.sync_copy(tok_hbm.at[pl.ds(tok0, win)], rows_vmem)
                for k in range(TOP_K):
                    pltpu.sync_copy(dest_hbm.at[pl.ds(k * r + tok0, win)], idx_vmem)
                    pltpu.async_copy(rows_vmem, xs_hbm.at[idx_vmem], sem).wait()

        pltpu.sync_copy(zero_hbm, rows_vmem)

        @pl.loop(0, pads_per_worker)
        def _(j):
            off = (worker * pads_per_worker + j) * win
            pltpu.sync_copy(pad_hbm.at[pl.ds(off, win)], idx_vmem)
            pltpu.async_copy(rows_vmem, xs_hbm.at[idx_vmem], sem).wait()

    return dispatch_kernel(tokp, dest_flat, pad_rows, zero_rows)


def _expert_kernel(be_ref, nb_ref, ord_ref, ue_ref, nue_ref, xs_ref, wgu_hbm, wdn_hbm, ys_ref, wgu_f32, wdn_f32,
                   wgu_bf, wdn_bf, sems, *, layer):
    def weight_copies(o):
        slot = o % 2
        e = ue_ref[o]
        return (pltpu.make_async_copy(wgu_hbm.at[layer, e], wgu_f32.at[slot], sems.at[0, slot]),
                pltpu.make_async_copy(wdn_hbm.at[layer, e], wdn_f32.at[slot], sems.at[1, slot]))

    def start_weights(o):
        @pl.when(o < nue_ref[0])
        def _():
            for cp in weight_copies(o):
                cp.start()

    def one_block(sub, carry):
        j = pl.program_id(0) * EXPERT_BLOCKS_PER_STEP + sub

        @pl.when(j < nb_ref[0])
        def _():
            o = ord_ref[j]
            changed = jnp.logical_or(j == 0, be_ref[j] != be_ref[jnp.maximum(j - 1, 0)])

            @pl.when(j == 0)
            def _():
                start_weights(0)
                start_weights(1)

            @pl.when(changed)
            def _():
                for cp in weight_copies(o):
                    cp.wait()
                slot = o % 2
                wgu_bf[...] = wgu_f32[slot].astype(BF16)
                wdn_bf[...] = wdn_f32[slot].astype(BF16)
                start_weights(o + 2)

            rows = pl.ds(pl.multiple_of(sub * EXPERT_BLOCK, EXPERT_BLOCK), EXPERT_BLOCK)
            x_lo, x_hi = _unpack_bf16_pairs(xs_ref[rows, :])
            h = (jnp.dot(x_lo.astype(BF16), wgu_bf[0:PACK_W, :], preferred_element_type=F32)
                 + jnp.dot(x_hi.astype(BF16), wgu_bf[PACK_W:, :], preferred_element_type=F32))
            g, u = h[:, 0:EXPERT_HIDDEN], h[:, EXPERT_HIDDEN:]
            y = jnp.dot((g * _sigmoid(g) * u).astype(BF16), wdn_bf[...], preferred_element_type=F32)
            ys_ref[rows, :] = _pack_bf16_pairs(y)

        return carry

    lax.fori_loop(0, EXPERT_BLOCKS_PER_STEP, one_block, 0)


def _expert_call(block_expert, n_blocks_used, block_ordinal, used_expert, n_used_experts, xs, w_gu, w_dn, layer):
    n_rows = block_expert.shape[0] * EXPERT_BLOCK
    bm = EXPERT_BLOCK
    d = D_MODEL
    step_rows = bm * EXPERT_BLOCKS_PER_STEP
    assert n_rows % step_rows == 0
    used_step = lambda s, be, nb, od, ue, nue: (jnp.minimum(s, (nb[0] - 1) // EXPERT_BLOCKS_PER_STEP), 0)
    grid_spec = pltpu.PrefetchScalarGridSpec(
        num_scalar_prefetch=5,
        grid=(n_rows // step_rows,),
        in_specs=[
            pl.BlockSpec((step_rows, PACK_W), used_step),
            pl.BlockSpec(memory_space=pl.ANY),
            pl.BlockSpec(memory_space=pl.ANY),
        ],
        out_specs=pl.BlockSpec((step_rows, PACK_W), used_step),
        scratch_shapes=[
            pltpu.VMEM((2, d, 2 * EXPERT_HIDDEN), F32),
            pltpu.VMEM((2, EXPERT_HIDDEN, d), F32),
            pltpu.VMEM((d, 2 * EXPERT_HIDDEN), BF16),
            pltpu.VMEM((EXPERT_HIDDEN, d), BF16),
            pltpu.SemaphoreType.DMA((2, 2)),
        ],
    )
    return pl.pallas_call(
        functools.partial(_expert_kernel, layer=layer),
        grid_spec=grid_spec,
        out_shape=jax.ShapeDtypeStruct((n_rows, PACK_W), jnp.int32),
        compiler_params=_cparams("arbitrary"),
        name="moe_experts",
    )(block_expert, n_blocks_used, block_ordinal, used_expert, n_used_experts, xs, w_gu, w_dn)


def _sc_gather_rows(table, indices):
    n = indices.shape[0]
    width = table.shape[1]
    workers = SC_NUM_CORES * SC_NUM_SUBCORES
    assert n % SC_GATHER_WINDOW == 0
    n_windows = n // SC_GATHER_WINDOW
    mesh = plsc.VectorSubcoreMesh(core_axis_name="core", subcore_axis_name="subcore", num_cores=SC_NUM_CORES,
                                  num_subcores=SC_NUM_SUBCORES)

    @functools.partial(
        pl.kernel, out_type=jax.ShapeDtypeStruct((n, width), table.dtype), mesh=mesh,
        scratch_types=[pltpu.VMEM((SC_GATHER_WINDOW,), jnp.int32),
                       pltpu.VMEM((SC_GATHER_WINDOW, width), table.dtype),
                       pltpu.SemaphoreType.DMA],
        name="moe_sc_gather")
    def gather_kernel(table_hbm, idx_hbm, out_hbm, idx_vmem, rows_vmem, sem):
        worker = lax.axis_index("subcore") * SC_NUM_CORES + lax.axis_index("core")

        @pl.loop(0, -(-n_windows // workers))
        def _(j):
            window = j * workers + worker

            @pl.when(window < n_windows)
            def _():
                off = window * SC_GATHER_WINDOW
                pltpu.sync_copy(idx_hbm.at[pl.ds(off, SC_GATHER_WINDOW)], idx_vmem)
                pltpu.async_copy(table_hbm.at[idx_vmem], rows_vmem, sem).wait()
                pltpu.sync_copy(rows_vmem, out_hbm.at[pl.ds(off, SC_GATHER_WINDOW)])

    return gather_kernel(table, indices)


def _combine_kernel(*refs, alpha):
    y_refs = refs[:TOP_K]
    x_ref, fsh_ref, gate_ref, mod_ref, lng_ref, lnb_ref = refs[TOP_K:TOP_K + 6]
    o_ref = refs[-1]
    d = D_MODEL
    t = x_ref.shape[0]
    gate_rows = gate_ref[...]
    pad = jnp.zeros((LANES - TOP_K, t), F32)
    gate_cols = jnp.concatenate([gate_rows, pad], axis=0).T
    f_lo = fsh_ref[:, 0:PACK_W]
    f_hi = fsh_ref[:, PACK_W:]
    for k in range(TOP_K):
        y_lo, y_hi = _unpack_bf16_pairs(y_refs[k][...])
        f_lo = f_lo + gate_cols[:, k:k + 1] * y_lo
        f_hi = f_hi + gate_cols[:, k:k + 1] * y_hi
    f = jnp.concatenate([f_lo, f_hi], axis=1)
    z = alpha * x_ref[...] + mod_ref[:, 5 * d:6 * d] * f
    o_ref[...] = _layer_norm_rows(z) * lng_ref[...] + lnb_ref[...]


def _combine_call(y_tok, x, fsh, gate, mod3, ln_g, ln_b, prev_out, *, batch_index, tiles_per_batch, alpha,
                  drop_context):
    r, d = x.shape
    t = ROW_TILE
    n_batches = r // t // tiles_per_batch
    n_tiles = tiles_per_batch - 1 if drop_context else tiles_per_batch
    tile0 = batch_index * tiles_per_batch
    row = lambda i: (tile0 + i, 0)
    col = lambda i: (0, tile0 + i)
    const = lambda i: (0, 0)
    kern = functools.partial(_combine_kernel, alpha=alpha)
    y_specs = [pl.BlockSpec((t, PACK_W), functools.partial(lambda k, i: (k * tiles_per_batch + i, 0), k))
               for k in range(TOP_K)]
    in_specs = y_specs + [
        pl.BlockSpec((t, d), row),
        pl.BlockSpec((t, d), row),
        pl.BlockSpec((TOP_K, t), col),
        pl.BlockSpec((None, 1, 6 * d), lambda i: (_mod_row(tile0 + i, tiles_per_batch), 0, 0)),
        pl.BlockSpec((1, d), const),
        pl.BlockSpec((1, d), const),
    ]
    args = [y_tok] * TOP_K + [x, fsh, gate, mod3, ln_g, ln_b]
    aliases = {}
    if prev_out is not None:
        in_specs.append(pl.BlockSpec(memory_space=pl.ANY))
        args.append(prev_out)
        aliases = {len(args) - 1: 0}
    return pl.pallas_call(
        kern,
        grid=(n_tiles,),
        in_specs=in_specs,
        out_specs=pl.BlockSpec((t, d), lambda i: (batch_index * n_tiles + i, 0)),
        out_shape=jax.ShapeDtypeStruct((n_batches * n_tiles * t, d), F32),
        input_output_aliases=aliases,
        compiler_params=_cparams("arbitrary"),
        name="moe_combine",
    )(*args)


def _rope_tables(seq):
    rows = seq // GRID_W
    row = jnp.repeat(jnp.arange(rows, dtype=F32), GRID_W)
    col = jnp.tile(jnp.arange(GRID_W, dtype=F32), rows)
    nf = DA_DIM // 4
    freqs = ROPE_BASE ** (-jnp.arange(nf, dtype=F32) / nf)
    cr, sr = jnp.cos(row[:, None] * freqs), jnp.sin(row[:, None] * freqs)
    cc, sc = jnp.cos(col[:, None] * freqs), jnp.sin(col[:, None] * freqs)
    c64 = jnp.concatenate([cr, cr, cc, cc], axis=1)
    s64 = jnp.concatenate([-sr, sr, -sc, sc], axis=1)
    c = jnp.concatenate([jnp.tile(c64, (1, 2)), jnp.ones((CTX_LEN, LANES), F32)], axis=0)
    s = jnp.concatenate([jnp.tile(s64, (1, 2)), jnp.zeros((CTX_LEN, LANES), F32)], axis=0)
    return c, s


def kernel(x, c, ctx, c_ctx, w_mod, b_mod, w_in, w_out, diff_lambda, pool_w, pool_scale, ret_log_decay, ln_g, ln_b,
           w_router, router_bias, w_expert_gate_up, w_expert_down, w_shared_gate_up, w_shared_down):
    batch, seq, d = x.shape
    depth = w_mod.shape[0]
    assert d == D_MODEL and ctx.shape[1] == CTX_LEN == ROW_TILE and batch == 2
    assert seq % ROW_TILE == 0 and seq % GRID_W == 0 and w_in.shape[-1] == IN_WIDTH
    rows_per_batch = seq + CTX_LEN
    tiles_per_batch = rows_per_batch // ROW_TILE
    r = batch * rows_per_batch
    alpha = (2.0 * depth) ** 0.25

    xa = jnp.concatenate([x, ctx], axis=1).reshape(r, d)
    cvec = jnp.zeros((8, d), F32).at[0:batch].set(c).at[batch].set(c_ctx)
    mod_all = _mod_call(cvec, w_mod, b_mod)
    rope_c, rope_s = _rope_tables(seq)

    n_sorted = r * TOP_K + N_EXPERTS * EXPERT_BLOCK
    n_blocks = n_sorted // EXPERT_BLOCK

    for l in range(depth):
        lambda_init = 0.8 - 0.6 * math.exp(-0.3 * l)
        mod3 = mod_all[l].reshape(8, 1, 6 * d)
        lng = ln_g[l].reshape(2, 1, d)
        lnb = ln_b[l].reshape(2, 1, d)

        w_in_bf = w_in[l].astype(BF16)
        w_vt_bf = w_in_bf[:, QK_WIDTH:QK_WIDTH + DA_WIDTH].T
        qk, vda, u, rqkv, rg = _inproj_call(xa, mod3, w_in_bf, w_vt_bf, rope_c, rope_s, tiles_per_batch)
        da = _attn_call(diff_lambda[l], qk, vda, batch=batch, rows_per_batch=rows_per_batch, seq=seq,
                        lambda_init=lambda_init)
        o_f, o_b = _ret_call(ret_log_decay[l], rqkv, batch=batch, rows_per_batch=rows_per_batch, seq=seq)
        pool_bd = jnp.zeros((POOL_WIDTH, POOL_WIDTH), F32)
        for gi in range(len(POOL_WINDOWS)):
            sl = slice(gi * POOL_GROUP, (gi + 1) * POOL_GROUP)
            pool_bd = pool_bd.at[sl, sl].set(pool_w[l, gi])
        xa = _mixout_call(xa, da, u, o_f, o_b, rg, mod3, w_out[l].astype(BF16), pool_bd.astype(BF16),
                          pool_scale[l].reshape(1, POOL_WIDTH), lng[0], lnb[0],
                          tiles_per_batch=tiles_per_batch, seq=seq, alpha=alpha)

        wr_t = w_router[l].T
        wr_hi = wr_t.astype(BF16)
        wr_lo = (wr_t - wr_hi.astype(F32)).astype(BF16)
        tokp, idx, gate, rank, cnt, fsh = _router_call(
            xa, mod3, wr_hi, wr_lo, router_bias[l].reshape(N_EXPERTS, 1),
            w_shared_gate_up[l].astype(BF16), w_shared_down[l].astype(BF16), tiles_per_batch=tiles_per_batch)
        counts = cnt[:, 0]
        padded = (counts + EXPERT_BLOCK - 1) // EXPERT_BLOCK * EXPERT_BLOCK
        pad_end = jnp.cumsum(padded)
        offs = pad_end - padded
        expert_ids = jnp.arange(N_EXPERTS, dtype=jnp.int32)
        blk_row = jnp.arange(n_blocks, dtype=jnp.int32) * EXPERT_BLOCK
        block_expert = jnp.minimum(jnp.sum(pad_end[None, :] <= blk_row[:, None], axis=1), N_EXPERTS - 1)
        n_used = pad_end[-1:] // EXPERT_BLOCK
        used = counts > 0
        ordinal = jnp.cumsum(used) - 1
        hit = used[None, :] & (ordinal[None, :] == expert_ids[:, None])
        used_expert = jnp.sum(jnp.where(hit, expert_ids[None, :], 0), axis=1)
        n_used_experts = jnp.sum(used)[None]
        block_ordinal = ordinal[block_expert]
        slot = jnp.arange(EXPERT_BLOCK, dtype=jnp.int32)[None, :]
        first_pad = (padded - EXPERT_BLOCK)[:, None] + slot
        is_pad = (first_pad >= counts[:, None]) & (padded[:, None] > 0)
        spare = n_sorted + jnp.arange(N_EXPERTS * EXPERT_BLOCK, dtype=jnp.int32).reshape(N_EXPERTS, EXPERT_BLOCK)
        pad_rows = jnp.where(is_pad, offs[:, None] + first_pad, spare).reshape(N_EXPERTS * EXPERT_BLOCK)
        i32 = lambda a: a.astype(jnp.int32)

        dest = _dest_call(idx, rank, i32(offs).reshape(N_EXPERTS, 1))
        dest_flat = dest.reshape(TOP_K * r)
        xs = _sc_dispatch(tokp, dest_flat, i32(pad_rows), n_sorted)
        ys = _expert_call(i32(block_expert), i32(n_used), i32(block_ordinal), i32(used_expert),
                          i32(n_used_experts), xs, w_expert_gate_up, w_expert_down, l)
        x_new = None
        for b in range(batch):
            dest_b = dest[:, b * rows_per_batch:(b + 1) * rows_per_batch].reshape(TOP_K * rows_per_batch)
            y_tok = _sc_gather_rows(ys, dest_b)
            x_new = _combine_call(y_tok, xa, fsh, gate, mod3, lng[1], lnb[1], x_new, batch_index=b,
                                  tiles_per_batch=tiles_per_batch, alpha=alpha, drop_context=(l == depth - 1))
        xa = x_new

    return xa.reshape(batch, seq, d)
```

```python
import functools
import math

import jax
import jax.numpy as jnp
from jax import lax
from jax.experimental import pallas as pl
from jax.experimental.pallas import tpu as pltpu
from jax.experimental.pallas import tpu_sc as plsc

F32 = jnp.float32
BF16 = jnp.bfloat16
HIGHEST = lax.Precision.HIGHEST

D_MODEL = 1024
CTX_LEN = 256
GRID_W = 64
DA_HEADS = 4
DA_DIM = 64
DA_VDIM = 2 * DA_DIM
DA_WIDTH = DA_HEADS * DA_VDIM
ROPE_BASE = 10000.0
POOL_WINDOWS = (2, 4, 8, 16)
POOL_GROUP = 64
POOL_WIDTH = len(POOL_WINDOWS) * POOL_GROUP
POOL_HALO = 8
RET_HEADS = 4
RET_DK = 64
RET_WIDTH = RET_HEADS * RET_DK
RET_CHUNK = 128
QK_WIDTH = 2 * DA_HEADS * 2 * DA_DIM
IN_WIDTH = QK_WIDTH + DA_WIDTH + POOL_WIDTH + 4 * RET_WIDTH
N_EXPERTS = 256
TOP_K = 8
N_GROUPS = 8
GROUP_SIZE = N_EXPERTS // N_GROUPS
TOPK_GROUPS = 4
EXPERT_HIDDEN = 256
ROUTED_SCALE = 2.5
LN_EPS = 1e-6
RMS_EPS = 1e-5

LANES = 128
ROW_TILE = 256
MOD_COL_TILE = 1536
DEST_STEPS = 4
ATTN_Q_TILE = 256
ATTN_K_CHUNK = 256
ATTN_UNROLL = 16
SC_NUM_CORES = 2
SC_NUM_SUBCORES = 16
SC_GATHER_WINDOW = 128
EXPERT_BLOCK = 256
EXPERT_BLOCKS_PER_STEP = 4
SC_SPARE_ROWS = N_EXPERTS * EXPERT_BLOCK
PACK_W = D_MODEL // 2
VMEM_LIMIT = 56 * 1024 * 1024


def _cparams(*sem):
    return pltpu.CompilerParams(dimension_semantics=sem, vmem_limit_bytes=VMEM_LIMIT)


def _sigmoid(x):
    return 1.0 / (1.0 + jnp.exp(-x))


def _layer_norm_rows(x):
    mu = jnp.mean(x, axis=-1, keepdims=True)
    xc = x - mu
    var = jnp.mean(xc * xc, axis=-1, keepdims=True)
    return xc * lax.rsqrt(var + LN_EPS)


def _pack_bf16_pairs(x):
    half = x.shape[1] // 2
    bits = pltpu.bitcast(x.astype(BF16).astype(F32), jnp.uint32)
    word = lax.shift_right_logical(bits[:, 0:half], jnp.uint32(16)) | (bits[:, half:] & jnp.uint32(0xFFFF0000))
    return pltpu.bitcast(word, jnp.int32)


def _unpack_bf16_pairs(packed):
    word = pltpu.bitcast(packed, jnp.uint32)
    lo = pltpu.bitcast(lax.shift_left(word, jnp.uint32(16)), F32)
    hi = pltpu.bitcast(word & jnp.uint32(0xFFFF0000), F32)
    return lo, hi


def _mod_row(i, tiles_per_batch):
    return jnp.where(i % tiles_per_batch == tiles_per_batch - 1, 2, i // tiles_per_batch)


def _mod_kernel(c_ref, w_ref, b_ref, o_ref):
    c = c_ref[...]
    s = c * _sigmoid(c)
    o_ref[...] = jnp.dot(s, w_ref[...], precision=HIGHEST, preferred_element_type=F32) + b_ref[...]


def _mod_call(cvec, w_mod, b_mod):
    depth, d, n = w_mod.shape
    tn = MOD_COL_TILE
    assert n % tn == 0
    return pl.pallas_call(
        _mod_kernel,
        grid=(depth, n // tn),
        in_specs=[
            pl.BlockSpec((8, d), lambda l, j: (0, 0)),
            pl.BlockSpec((None, d, tn), lambda l, j: (l, 0, j)),
            pl.BlockSpec((None, 1, tn), lambda l, j: (l, 0, j)),
        ],
        out_specs=pl.BlockSpec((None, 8, tn), lambda l, j: (l, 0, j)),
        out_shape=jax.ShapeDtypeStruct((depth, 8, n), F32),
        compiler_params=_cparams("arbitrary", "arbitrary"),
        name="mod",
    )(cvec, w_mod, b_mod.reshape(depth, 1, n))


def _inproj_kernel(x_ref, mod_ref, w_ref, wvt_ref, ct_ref, st_ref, qk_ref, vt_ref, u_ref, r_ref, g_ref):
    d = D_MODEL
    xn = _layer_norm_rows(x_ref[...])
    h = (xn * (1.0 + mod_ref[:, d:2 * d]) + mod_ref[:, 0:d]).astype(BF16)

    a = jnp.dot(h, w_ref[:, 0:QK_WIDTH], preferred_element_type=F32)
    lane = lax.broadcasted_iota(jnp.int32, (a.shape[0], LANES), 1)
    first_half = (lane % 32) < 16
    ct = ct_ref[...]
    st = st_ref[...]
    for s in range(QK_WIDTH // LANES):
        blk = a[:, s * LANES:(s + 1) * LANES]
        partner = jnp.where(first_half, pltpu.roll(blk, LANES - 16, 1), pltpu.roll(blk, 16, 1))
        rot = blk * ct + partner * st
        if s < QK_WIDTH // LANES // 2:
            rot = rot * (DA_DIM ** -0.5 * math.log2(math.e))
        qk_ref[:, s * LANES:(s + 1) * LANES] = rot.astype(BF16)

    vt_ref[...] = lax.dot_general(wvt_ref[...], h, (((1,), (1,)), ((), ())),
                                  preferred_element_type=F32).astype(BF16)
    o = QK_WIDTH + DA_WIDTH
    u_ref[...] = jnp.dot(h, w_ref[:, o:o + POOL_WIDTH], preferred_element_type=F32)
    o += POOL_WIDTH
    r = jnp.dot(h, w_ref[:, o:o + 3 * RET_WIDTH], preferred_element_type=F32)
    r_ref[:, 0:RET_WIDTH] = r[:, 0:RET_WIDTH].astype(BF16)
    r_ref[:, RET_WIDTH:2 * RET_WIDTH] = (r[:, RET_WIDTH:2 * RET_WIDTH] * (RET_DK ** -0.5)).astype(BF16)
    r_ref[:, 2 * RET_WIDTH:] = r[:, 2 * RET_WIDTH:].astype(BF16)
    o += 3 * RET_WIDTH
    g_ref[...] = jnp.dot(h, w_ref[:, o:o + RET_WIDTH], preferred_element_type=F32)


def _inproj_call(x, mod3, w_in_bf, w_vt_bf, rope_c, rope_s, tiles_per_batch):
    r, d = x.shape
    t = ROW_TILE
    nt = r // t
    row = lambda i: (i, 0)
    return pl.pallas_call(
        _inproj_kernel,
        grid=(nt,),
        in_specs=[
            pl.BlockSpec((t, d), row),
            pl.BlockSpec((None, 1, 6 * d), lambda i: (_mod_row(i, tiles_per_batch), 0, 0)),
            pl.BlockSpec((d, IN_WIDTH), lambda i: (0, 0)),
            pl.BlockSpec((DA_WIDTH, d), lambda i: (0, 0)),
            pl.BlockSpec((t, LANES), lambda i: (i % tiles_per_batch, 0)),
            pl.BlockSpec((t, LANES), lambda i: (i % tiles_per_batch, 0)),
        ],
        out_specs=[
            pl.BlockSpec((t, QK_WIDTH), row),
            pl.BlockSpec((DA_WIDTH, t), lambda i: (0, i)),
            pl.BlockSpec((t, POOL_WIDTH), row),
            pl.BlockSpec((t, 3 * RET_WIDTH), row),
            pl.BlockSpec((t, RET_WIDTH), row),
        ],
        out_shape=[
            jax.ShapeDtypeStruct((r, QK_WIDTH), BF16),
            jax.ShapeDtypeStruct((DA_WIDTH, r), BF16),
            jax.ShapeDtypeStruct((r, POOL_WIDTH), F32),
            jax.ShapeDtypeStruct((r, 3 * RET_WIDTH), BF16),
            jax.ShapeDtypeStruct((r, RET_WIDTH), F32),
        ],
        compiler_params=_cparams("arbitrary"),
        name="inproj",
    )(x, mod3, w_in_bf, w_vt_bf, rope_c, rope_s)


def _attn_kernel(lam_ref, q_ref, k_ref, vt_ref, o_ref, *s_refs, k_chunk, seq, lambda_init):
    mq = ATTN_Q_TILE
    n_tiles = (seq + CTX_LEN) // mq
    n_chunks = (seq + CTX_LEN) // k_chunk
    last = n_chunks - 1
    n_iters = last // ATTN_UNROLL
    neg_inf = jnp.full((1, 2 * mq), -jnp.inf, F32)
    acc_zero = jnp.zeros((DA_VDIM + 16, 2 * mq), F32)
    ones_rows = jnp.where(lax.broadcasted_iota(jnp.int32, (16, k_chunk), 0) == 0, 1.0, 0.0).astype(BF16)

    def tile_rows(i):
        return pl.ds(pl.multiple_of(i * mq, mq), mq)

    def q_transposed(i):
        q = q_ref[tile_rows(i), :]
        lane = lax.broadcasted_iota(jnp.int32, q.shape, 1)
        zero = jnp.zeros_like(q)
        q2 = jnp.concatenate([jnp.where(lane < DA_DIM, q, zero), jnp.where(lane >= DA_DIM, q, zero)], axis=0)
        return q2.astype(F32).T.astype(BF16)

    def score_chunk(s_ref, c, qt, m):
        off = pl.multiple_of(c * k_chunk, k_chunk)
        s = jnp.dot(k_ref[pl.ds(off, k_chunk), :], qt, preferred_element_type=F32)
        s_ref[c] = s
        return jnp.maximum(m, jnp.max(s, axis=0, keepdims=True))

    def score_chunk_pair(s_ref, c, qt, m):
        off = pl.multiple_of(c * k_chunk, 2 * k_chunk)
        s = jnp.dot(k_ref[pl.ds(off, 2 * k_chunk), :], qt, preferred_element_type=F32)
        s_ref[c] = s[0:k_chunk]
        s_ref[c + 1] = s[k_chunk:]
        return jnp.maximum(m, jnp.max(s, axis=0, keepdims=True))

    def value_chunk(s_ref, c, m, acc):
        off = pl.multiple_of(c * k_chunk, k_chunk)
        vt = jnp.concatenate([vt_ref[:, pl.ds(off, k_chunk)], ones_rows], axis=0)
        p = jnp.exp2((s_ref[c] - m).astype(BF16))
        return acc + jnp.dot(vt, p, preferred_element_type=F32)

    def finish(i, acc):
        l0, l1 = acc[DA_VDIM:DA_VDIM + 1, 0:mq], acc[DA_VDIM:DA_VDIM + 1, mq:]
        a0, a1 = acc[0:DA_VDIM, 0:mq], acc[0:DA_VDIM, mq:]
        lv = lam_ref[...]
        lam = (jnp.exp(jnp.sum(lv[0:1] * lv[1:2], axis=-1, keepdims=True))
               - jnp.exp(jnp.sum(lv[2:3] * lv[3:4], axis=-1, keepdims=True)) + lambda_init)
        o = a0 / l0 - lam * (a1 / l1)
        o = o * lax.rsqrt(jnp.mean(o * o, axis=0, keepdims=True) + RMS_EPS) * (1.0 - lambda_init)
        o_ref[tile_rows(i), :] = o.T.astype(BF16)

    def scores_only(s_ref, qt):
        def body(it, m):
            for u in range(ATTN_UNROLL):
                m = score_chunk(s_ref, it * ATTN_UNROLL + u, qt, m)
            return m
        return score_chunk(s_ref, last, qt, lax.fori_loop(0, n_iters, body, neg_inf))

    def values_only(s_ref, m):
        def body(it, acc):
            for u in range(ATTN_UNROLL):
                acc = value_chunk(s_ref, it * ATTN_UNROLL + u, m, acc)
            return acc
        return value_chunk(s_ref, last, m, lax.fori_loop(0, n_iters, body, acc_zero))

    def fused_tile(i, m_prev, s_cur, s_prev):
        qt = q_transposed(i)

        def body(it, carry):
            m, acc = carry
            for u in range(0, ATTN_UNROLL, 2):
                c = it * ATTN_UNROLL + u
                m = score_chunk_pair(s_cur, c, qt, m)
                acc = value_chunk(s_prev, c, m_prev, acc)
                acc = value_chunk(s_prev, c + 1, m_prev, acc)
            return m, acc

        m, acc = lax.fori_loop(0, n_iters, body, (neg_inf, acc_zero))
        m = score_chunk(s_cur, last, qt, m)
        finish(i - 1, value_chunk(s_prev, last, m_prev, acc))
        return m

    s_even, s_odd = s_refs
    ctx_tile = n_tiles - 1
    assert ctx_tile % 2 == 0 and ctx_tile >= 2
    m = scores_only(s_even, q_transposed(0))

    def tile_pair(p, m):
        m = fused_tile(2 * p + 1, m, s_odd, s_even)
        return fused_tile(2 * p + 2, m, s_even, s_odd)

    m = lax.fori_loop(0, (ctx_tile - 2) // 2, tile_pair, m)
    m = fused_tile(ctx_tile - 1, m, s_odd, s_even)
    m_ctx = score_chunk(s_even, last, q_transposed(ctx_tile), neg_inf)
    finish(ctx_tile - 1, values_only(s_odd, m))
    finish(ctx_tile, value_chunk(s_even, last, m_ctx, acc_zero))


def _attn_call(lam_vec, qk, vda, *, batch, rows_per_batch, seq, lambda_init):
    tq = ATTN_Q_TILE
    assert seq % (ATTN_K_CHUNK * ATTN_UNROLL) == 0 and rows_per_batch - seq == CTX_LEN == tq == ATTN_K_CHUNK
    nq = rows_per_batch // tq
    kern = functools.partial(_attn_kernel, k_chunk=ATTN_K_CHUNK, seq=seq, lambda_init=lambda_init)
    return pl.pallas_call(
        kern,
        grid=(batch, DA_HEADS),
        in_specs=[
            pl.BlockSpec((4, DA_DIM), lambda b, h: (0, 0)),
            pl.BlockSpec((rows_per_batch, DA_VDIM), lambda b, h: (b, h)),
            pl.BlockSpec((rows_per_batch, DA_VDIM), lambda b, h: (b, DA_HEADS + h)),
            pl.BlockSpec((DA_VDIM, rows_per_batch), lambda b, h: (h, b)),
        ],
        out_specs=pl.BlockSpec((rows_per_batch, DA_VDIM), lambda b, h: (b, h)),
        out_shape=jax.ShapeDtypeStruct((qk.shape[0], DA_WIDTH), BF16),
        scratch_shapes=[pltpu.VMEM((rows_per_batch // ATTN_K_CHUNK, ATTN_K_CHUNK, 2 * tq), F32)] * 2,
        compiler_params=_cparams("arbitrary", "arbitrary"),
        name="diff_attn",
    )(lam_vec, qk, qk, vda)


def _ret_kernel(ld_ref, f_ref, b_ref, of_ref, ob_ref, dm_ref, qd_ref, kd_ref, cd_ref, st_ref):
    c = pl.program_id(1)
    ch = RET_CHUNK
    w = RET_WIDTH
    lane_head = lax.broadcasted_iota(jnp.int32, (1, w), 1) // RET_DK

    @pl.when(c == 0)
    def _():
        st_ref[...] = jnp.zeros_like(st_ref)
        ri = lax.broadcasted_iota(jnp.int32, (ch, ch), 0)
        ci = lax.broadcasted_iota(jnp.int32, (ch, ch), 1)
        rowf = lax.broadcasted_iota(jnp.int32, (ch, w), 0).astype(F32)
        for d in range(2):
            lg_lane = jnp.zeros((1, w), F32)
            for hh in range(RET_HEADS):
                lg = -jnp.exp(jnp.full((1, 1), ld_ref[d, hh], F32))
                lg_lane = jnp.where(lane_head == hh, lg, lg_lane)
                dist = ((ri - ci) if d == 0 else (ci - ri)).astype(F32)
                dm_ref[d, hh] = jnp.where(dist >= 0, jnp.exp(dist * lg), 0.0)
            if d == 0:
                qd_ref[d] = jnp.exp((rowf + 1.0) * lg_lane)
                kd_ref[d] = jnp.exp((ch - 1.0 - rowf) * lg_lane)
            else:
                qd_ref[d] = jnp.exp((ch - rowf) * lg_lane)
                kd_ref[d] = jnp.exp(rowf * lg_lane)
            cd_ref[d] = jnp.exp(float(ch) * lg_lane)

    rblk = lax.broadcasted_iota(jnp.int32, (w, w), 0) // RET_DK
    cblk = lax.broadcasted_iota(jnp.int32, (w, w), 1) // RET_DK
    for d, (src, dst) in enumerate(((f_ref, of_ref), (b_ref, ob_ref))):
        q = src[:, 0:w]
        k = src[:, w:2 * w]
        v = src[:, 2 * w:3 * w]
        st = st_ref[d]
        o = jnp.dot((q.astype(F32) * qd_ref[d]).astype(BF16), st.astype(BF16), preferred_element_type=F32)
        for hh in range(RET_HEADS):
            in_head = lane_head == hh
            qm = jnp.where(in_head, q, jnp.zeros_like(q))
            s = lax.dot_general(qm, k, (((1,), (1,)), ((), ())), preferred_element_type=F32)
            intra = (s * dm_ref[d, hh]).astype(BF16)
            o = o + jnp.where(in_head, jnp.dot(intra, v, preferred_element_type=F32), 0.0)
        dst[...] = o
        kk_t = (k.astype(F32) * kd_ref[d]).T.astype(BF16)
        upd = jnp.dot(kk_t, v, preferred_element_type=F32)
        st_ref[d] = jnp.where(rblk == cblk, st * cd_ref[d] + upd, 0.0)


def _ret_call(log_decay, rqkv, *, batch, rows_per_batch, seq):
    ch = RET_CHUNK
    nc = rows_per_batch // ch
    n_lat = seq // ch
    n_ctx = nc - n_lat

    def fwd(b, c):
        return (b * nc + jnp.where(c < n_ctx, n_lat + c, c - n_ctx), 0)

    def bwd(b, c):
        return (b * nc + nc - 1 - c, 0)

    w = RET_WIDTH
    return pl.pallas_call(
        _ret_kernel,
        grid=(batch, nc),
        in_specs=[
            pl.BlockSpec(memory_space=pltpu.SMEM),
            pl.BlockSpec((ch, 3 * w), fwd),
            pl.BlockSpec((ch, 3 * w), bwd),
        ],
        out_specs=[pl.BlockSpec((ch, w), fwd), pl.BlockSpec((ch, w), bwd)],
        out_shape=[jax.ShapeDtypeStruct((rqkv.shape[0], w), F32)] * 2,
        scratch_shapes=[
            pltpu.VMEM((2, RET_HEADS, ch, ch), F32),
            pltpu.VMEM((2, ch, w), F32),
            pltpu.VMEM((2, ch, w), F32),
            pltpu.VMEM((2, 1, w), F32),
            pltpu.VMEM((2, w, w), F32),
        ],
        compiler_params=_cparams("arbitrary", "arbitrary"),
        name="retention",
    )(log_decay, rqkv, rqkv)


def _mixout_kernel(x_ref, da_ref, u_ref, up_ref, un_ref, of_ref, ob_ref, rg_ref, mod_ref, wo_ref, pw_ref,
                   ps_ref, lng_ref, lnb_ref, o_ref, *, tiles_per_batch, seq, alpha):
    d = D_MODEL
    t = x_ref.shape[0]
    i = pl.program_id(0)
    j = i % tiles_per_batch
    is_ctx = j == tiles_per_batch - 1
    stream_len = jnp.where(is_ctx, CTX_LEN, seq)
    p0 = jnp.where(is_ctx, 0, j * t)

    u = u_ref[...]
    prev = jnp.where(p0 > 0, up_ref[...], 0.0)
    nxt = jnp.where(p0 + t < stream_len, un_ref[...], 0.0)
    ext = jnp.concatenate([prev, u, nxt], axis=0)
    n = t + 2 * POOL_HALO
    a2 = ext + pltpu.roll(ext, 1, 0)
    a4 = pltpu.roll(a2, 1, 0) + pltpu.roll(a2, n - 1, 0)
    a8 = pltpu.roll(a4, 2, 0) + pltpu.roll(a4, n - 2, 0)
    a16 = pltpu.roll(a8, 4, 0) + pltpu.roll(a8, n - 4, 0)
    pos = p0 + lax.broadcasted_iota(jnp.int32, (t, POOL_WIDTH), 0)
    group = lax.broadcasted_iota(jnp.int32, (1, POOL_WIDTH), 1) // POOL_GROUP
    mean = jnp.zeros((t, POOL_WIDTH), F32)
    for gi, (wnd, asum) in enumerate(zip(POOL_WINDOWS, (a2, a4, a8, a16))):
        cnt = jnp.minimum(pos + wnd // 2, stream_len) - jnp.maximum(pos - wnd // 2, 0)
        mean = jnp.where(group == gi, asum[POOL_HALO:POOL_HALO + t] / cnt.astype(F32), mean)
    pool = jnp.dot((mean - u).astype(BF16), pw_ref[...], preferred_element_type=F32) * ps_ref[...]

    o = of_ref[...] + ob_ref[...]
    head = lax.broadcasted_iota(jnp.int32, (1, RET_WIDTH), 1) // RET_DK

    def head_mean(val):
        out = jnp.zeros_like(val)
        for hh in range(RET_HEADS):
            m = jnp.sum(jnp.where(head == hh, val, 0.0), axis=-1, keepdims=True) * (1.0 / RET_DK)
            out = jnp.where(head == hh, m, out)
        return out

    oc = o - head_mean(o)
    rn = oc * lax.rsqrt(head_mean(oc * oc) + LN_EPS)
    g = rg_ref[...]
    ret = rn * (g * _sigmoid(g))

    y = jnp.dot(da_ref[...], wo_ref[0:DA_WIDTH, :], preferred_element_type=F32)
    y = y + jnp.dot(pool.astype(BF16), wo_ref[DA_WIDTH:DA_WIDTH + POOL_WIDTH, :], preferred_element_type=F32)
    y = y + jnp.dot(ret.astype(BF16), wo_ref[DA_WIDTH + POOL_WIDTH:, :], preferred_element_type=F32)
    z = alpha * x_ref[...] + mod_ref[:, 2 * d:3 * d] * y
    o_ref[...] = _layer_norm_rows(z) * lng_ref[...] + lnb_ref[...]


def _mixout_call(x, da, u, o_f, o_b, rg, mod3, w_out_bf, pool_bd, pool_scale, ln_g, ln_b, *, tiles_per_batch, seq,
                 alpha):
    r, d = x.shape
    t = ROW_TILE
    nt = r // t
    hb = t // POOL_HALO
    n_halo_blocks = r // POOL_HALO
    row = lambda i: (i, 0)
    const = lambda i: (0, 0)
    kern = functools.partial(_mixout_kernel, tiles_per_batch=tiles_per_batch, seq=seq, alpha=alpha)
    return pl.pallas_call(
        kern,
        grid=(nt,),
        in_specs=[
            pl.BlockSpec((t, d), row),
            pl.BlockSpec((t, DA_WIDTH), row),
            pl.BlockSpec((t, POOL_WIDTH), row),
            pl.BlockSpec((POOL_HALO, POOL_WIDTH), lambda i: (jnp.maximum(i * hb - 1, 0), 0)),
            pl.BlockSpec((POOL_HALO, POOL_WIDTH), lambda i: (jnp.minimum((i + 1) * hb, n_halo_blocks - 1), 0)),
            pl.BlockSpec((t, RET_WIDTH), row),
            pl.BlockSpec((t, RET_WIDTH), row),
            pl.BlockSpec((t, RET_WIDTH), row),
            pl.BlockSpec((None, 1, 6 * d), lambda i: (_mod_row(i, tiles_per_batch), 0, 0)),
            pl.BlockSpec((d, d), const),
            pl.BlockSpec((POOL_WIDTH, POOL_WIDTH), const),
            pl.BlockSpec((1, POOL_WIDTH), const),
            pl.BlockSpec((1, d), const),
            pl.BlockSpec((1, d), const),
        ],
        out_specs=pl.BlockSpec((t, d), row),
        out_shape=jax.ShapeDtypeStruct((r, d), F32),
        compiler_params=_cparams("arbitrary"),
        name="mixer_out",
    )(x, da, u, u, u, o_f, o_b, rg, mod3, w_out_bf, pool_bd, pool_scale, ln_g, ln_b)


def _router_kernel(x_ref, mod_ref, wrh_ref, wrl_ref, bias_ref, wsgu_ref, wsdn_ref,
                   tokp_ref, idx_ref, gate_ref, rank_ref, cnt_ref, fsh_ref, carry_ref):
    d = D_MODEL
    t = x_ref.shape[0]
    ne = N_EXPERTS
    neg = -jnp.inf

    @pl.when(pl.program_id(0) == 0)
    def _():
        carry_ref[...] = jnp.zeros_like(carry_ref)

    tok = _layer_norm_rows(x_ref[...]) * (1.0 + mod_ref[:, 4 * d:5 * d]) + mod_ref[:, 3 * d:4 * d]
    tok_hi = tok.astype(BF16)
    tok_lo = (tok - tok_hi.astype(F32)).astype(BF16)

    tokp_ref[...] = _pack_bf16_pairs(tok)

    hs = jnp.dot(tok_hi, wsgu_ref[...], preferred_element_type=F32)
    gs, us = hs[:, 0:EXPERT_HIDDEN], hs[:, EXPERT_HIDDEN:]
    fsh_ref[...] = jnp.dot((gs * _sigmoid(gs) * us).astype(BF16), wsdn_ref[...], preferred_element_type=F32)

    nt_dims = (((1,), (1,)), ((), ()))
    logits = (lax.dot_general(wrh_ref[...], tok_hi, nt_dims, preferred_element_type=F32)
              + lax.dot_general(wrh_ref[...], tok_lo, nt_dims, preferred_element_type=F32)
              + lax.dot_general(wrl_ref[...], tok_hi, nt_dims, preferred_element_type=F32))
    scores = _sigmoid(logits)
    biased = scores + bias_ref[...]

    gidx = lax.broadcasted_iota(jnp.int32, (GROUP_SIZE, t), 0)
    blocks, gscores = [], []
    for g in range(N_GROUPS):
        blk = biased[g * GROUP_SIZE:(g + 1) * GROUP_SIZE, :]
        m1 = jnp.max(blk, axis=0, keepdims=True)
        first = jnp.min(jnp.where(blk == m1, gidx, GROUP_SIZE), axis=0, keepdims=True)
        m2 = jnp.max(jnp.where(gidx == first, neg, blk), axis=0, keepdims=True)
        blocks.append(blk)
        gscores.append(m1 + m2)

    keep = [jnp.zeros((1, t), F32) for _ in range(N_GROUPS)]
    for _ in range(TOPK_GROUPS):
        m = gscores[0]
        for gs_ in gscores[1:]:
            m = jnp.maximum(m, gs_)
        found = jnp.zeros((1, t), F32)
        for g in range(N_GROUPS):
            hit = jnp.where(gscores[g] == m, 1.0 - found, 0.0)
            found = found + hit
            keep[g] = keep[g] + hit
            gscores[g] = jnp.where(hit > 0.0, neg, gscores[g])
    masked = jnp.concatenate([jnp.where(keep[g] > 0.0, blocks[g], neg) for g in range(N_GROUPS)], axis=0)

    ei = lax.broadcasted_iota(jnp.int32, (ne, t), 0)
    cur = masked
    onehot = jnp.zeros((ne, t), F32)
    idxs, gates = [], []
    for _ in range(TOP_K):
        m = jnp.max(cur, axis=0, keepdims=True)
        ii = jnp.min(jnp.where(cur == m, ei, ne), axis=0, keepdims=True)
        sel = ei == ii
        idxs.append(ii)
        gates.append(jnp.sum(jnp.where(sel, scores, 0.0), axis=0, keepdims=True))
        onehot = jnp.where(sel, 1.0, onehot)
        cur = jnp.where(sel, neg, cur)
    gsum = gates[0]
    for gk in gates[1:]:
        gsum = gsum + gk
    for k in range(TOP_K):
        idx_ref[k:k + 1, :] = idxs[k]
        gate_ref[k:k + 1, :] = gates[k] / gsum * ROUTED_SCALE

    ti = lax.broadcasted_iota(jnp.int32, (t, t), 0)
    tj = lax.broadcasted_iota(jnp.int32, (t, t), 1)
    before = jnp.where(ti < tj, 1.0, 0.0).astype(BF16)
    prefix = jnp.dot(onehot.astype(BF16), before, preferred_element_type=F32) + carry_ref[:, 0:1]
    for k in range(TOP_K):
        rank_k = jnp.sum(jnp.where(ei == idxs[k], prefix, 0.0), axis=0, keepdims=True)
        rank_ref[k:k + 1, :] = rank_k.astype(jnp.int32)
    carry_ref[...] = carry_ref[...] + jnp.sum(onehot, axis=1, keepdims=True)
    cnt_ref[...] = carry_ref[...].astype(jnp.int32)


def _router_call(x, mod3, wr_hi, wr_lo, bias_col, ws_gu_bf, ws_dn_bf, *, tiles_per_batch):
    r, d = x.shape
    t = ROW_TILE
    nt = r // t
    row = lambda i: (i, 0)
    col = lambda i: (0, i)
    const = lambda i: (0, 0)
    return pl.pallas_call(
        _router_kernel,
        grid=(nt,),
        in_specs=[
            pl.BlockSpec((t, d), row),
            pl.BlockSpec((None, 1, 6 * d), lambda i: (_mod_row(i, tiles_per_batch), 0, 0)),
            pl.BlockSpec((N_EXPERTS, d), const),
            pl.BlockSpec((N_EXPERTS, d), const),
            pl.BlockSpec((N_EXPERTS, 1), const),
            pl.BlockSpec((d, 2 * EXPERT_HIDDEN), const),
            pl.BlockSpec((EXPERT_HIDDEN, d), const),
        ],
        out_specs=[
            pl.BlockSpec((t, PACK_W), row),
            pl.BlockSpec((TOP_K, t), col),
            pl.BlockSpec((TOP_K, t), col),
            pl.BlockSpec((TOP_K, t), col),
            pl.BlockSpec((N_EXPERTS, LANES), const),
            pl.BlockSpec((t, d), row),
        ],
        out_shape=[
            jax.ShapeDtypeStruct((r, PACK_W), jnp.int32),
            jax.ShapeDtypeStruct((TOP_K, r), jnp.int32),
            jax.ShapeDtypeStruct((TOP_K, r), F32),
            jax.ShapeDtypeStruct((TOP_K, r), jnp.int32),
            jax.ShapeDtypeStruct((N_EXPERTS, LANES), jnp.int32),
            jax.ShapeDtypeStruct((r, d), F32),
        ],
        scratch_shapes=[pltpu.VMEM((N_EXPERTS, LANES), F32)],
        compiler_params=_cparams("arbitrary"),
        name="router",
    )(x, mod3, wr_hi, wr_lo, bias_col, ws_gu_bf, ws_dn_bf)


def _dest_kernel(idx_ref, rank_ref, offs_ref, dest_ref):
    t = idx_ref.shape[1]
    ei = lax.broadcasted_iota(jnp.int32, (N_EXPERTS, t), 0)
    offs = offs_ref[...].astype(F32)
    for k in range(TOP_K):
        start = jnp.sum(jnp.where(ei == idx_ref[k:k + 1, :], offs, 0.0), axis=0, keepdims=True)
        dest_ref[k:k + 1, :] = start.astype(jnp.int32) + rank_ref[k:k + 1, :]


def _dest_call(idx, rank, offs_col):
    r = idx.shape[1]
    t = r // DEST_STEPS
    assert r % DEST_STEPS == 0 and t % LANES == 0
    col = lambda i: (0, i)
    return pl.pallas_call(
        _dest_kernel,
        grid=(r // t,),
        in_specs=[pl.BlockSpec((TOP_K, t), col), pl.BlockSpec((TOP_K, t), col),
                  pl.BlockSpec((N_EXPERTS, 1), lambda i: (0, 0))],
        out_specs=pl.BlockSpec((TOP_K, t), col),
        out_shape=jax.ShapeDtypeStruct((TOP_K, r), jnp.int32),
        compiler_params=_cparams("arbitrary"),
        name="moe_dest",
    )(idx, rank, offs_col)


def _sc_dispatch(tokp, dest_flat, pad_rows, n_sorted):
    r, width = tokp.shape
    win = SC_GATHER_WINDOW
    workers = SC_NUM_CORES * SC_NUM_SUBCORES
    token_windows = r // win
    n_pad_windows = pad_rows.shape[0] // win
    assert r % win == 0 and dest_flat.shape[0] == TOP_K * r and n_pad_windows % workers == 0
    windows_per_worker = -(-token_windows // workers)
    pads_per_worker = n_pad_windows // workers
    mesh = plsc.VectorSubcoreMesh(core_axis_name="core", subcore_axis_name="subcore", num_cores=SC_NUM_CORES,
                                  num_subcores=SC_NUM_SUBCORES)
    zero_rows = jnp.zeros((win, width), tokp.dtype)

    @functools.partial(
        pl.kernel, out_type=jax.ShapeDtypeStruct((n_sorted + SC_SPARE_ROWS, width), tokp.dtype), mesh=mesh,
        scratch_types=[pltpu.VMEM((win,), jnp.int32), pltpu.VMEM((win, width), tokp.dtype),
                       pltpu.SemaphoreType.DMA],
        name="moe_sc_dispatch")
    def dispatch_kernel(tok_hbm, dest_hbm, pad_hbm, zero_hbm, xs_hbm, idx_vmem, rows_vmem, sem):
        worker = lax.axis_index("subcore") * SC_NUM_CORES + lax.axis_index("core")

        @pl.loop(0, windows_per_worker)
        def _(j):
            window = j * workers + worker

            @pl.when(window < token_windows)
            def _():
                tok0 = window * win
                pltpu.sync_copy(tok_hbm.at[pl.ds(tok0, win)], rows_vmem)
                for k in range(TOP_K):
                    pltpu.sync_copy(dest_hbm.at[pl.ds(k * r + tok0, win)], idx_vmem)
                    pltpu.async_copy(rows_vmem, xs_hbm.at[idx_vmem], sem).wait()

        pltpu.sync_copy(zero_hbm, rows_vmem)

        @pl.loop(0, pads_per_worker)
        def _(j):
            off = (worker * pads_per_worker + j) * win
            pltpu.sync_copy(pad_hbm.at[pl.ds(off, win)], idx_vmem)
            pltpu.async_copy(rows_vmem, xs_hbm.at[idx_vmem], sem).wait()

    return dispatch_kernel(tokp, dest_flat, pad_rows, zero_rows)


def _expert_kernel(be_ref, nb_ref, ord_ref, ue_ref, nue_ref, xs_ref, wgu_hbm, wdn_hbm, ys_ref, wgu_f32, wdn_f32,
                   wgu_bf, wdn_bf, sems, *, layer):
    def weight_copies(o):
        slot = o % 2
        e = ue_ref[o]
        return (pltpu.make_async_copy(wgu_hbm.at[layer, e], wgu_f32.at[slot], sems.at[0, slot]),
                pltpu.make_async_copy(wdn_hbm.at[layer, e], wdn_f32.at[slot], sems.at[1, slot]))

    def start_weights(o):
        @pl.when(o < nue_ref[0])
        def _():
            for cp in weight_copies(o):
                cp.start()

    def prepare_expert(j):
        o = ord_ref[j]
        changed = jnp.logical_or(j == 0, be_ref[j] != be_ref[jnp.maximum(j - 1, 0)])

        @pl.when(j == 0)
        def _():
            start_weights(0)
            start_weights(1)

        @pl.when(changed)
        def _():
            for cp in weight_copies(o):
                cp.wait()
            slot = o % 2
            wgu_bf[...] = wgu_f32[slot].astype(BF16)
            wdn_bf[...] = wdn_f32[slot].astype(BF16)
            start_weights(o + 2)

    def compute(sub, n_blocks):
        n_rows = n_blocks * EXPERT_BLOCK
        rows = pl.ds(pl.multiple_of(sub * EXPERT_BLOCK, EXPERT_BLOCK), n_rows)
        x_lo, x_hi = _unpack_bf16_pairs(xs_ref[rows, :])
        h = (jnp.dot(x_lo.astype(BF16), wgu_bf[0:PACK_W, :], preferred_element_type=F32)
             + jnp.dot(x_hi.astype(BF16), wgu_bf[PACK_W:, :], preferred_element_type=F32))
        g, u = h[:, 0:EXPERT_HIDDEN], h[:, EXPERT_HIDDEN:]
        y = jnp.dot((g * _sigmoid(g) * u).astype(BF16), wdn_bf[...], preferred_element_type=F32)
        ys_ref[rows, :] = _pack_bf16_pairs(y)

    def block_pair(pair, carry):
        nb = nb_ref[0]
        sub0 = 2 * pair
        j0 = pl.program_id(0) * EXPERT_BLOCKS_PER_STEP + sub0
        j1 = j0 + 1
        same = jnp.logical_and(j1 < nb, be_ref[j1] == be_ref[j0])

        @pl.when(j0 < nb)
        def _():
            prepare_expert(j0)

        @pl.when(jnp.logical_and(j0 < nb, same))
        def _():
            compute(sub0, 2)

        @pl.when(jnp.logical_and(j0 < nb, jnp.logical_not(same)))
        def _():
            compute(sub0, 1)

        @pl.when(jnp.logical_and(j1 < nb, jnp.logical_not(same)))
        def _():
            prepare_expert(j1)
            compute(sub0 + 1, 1)

        return carry

    lax.fori_loop(0, EXPERT_BLOCKS_PER_STEP // 2, block_pair, 0)


def _expert_call(block_expert, n_blocks_used, block_ordinal, used_expert, n_used_experts, xs, w_gu, w_dn, layer):
    n_rows = block_expert.shape[0] * EXPERT_BLOCK
    bm = EXPERT_BLOCK
    d = D_MODEL
    step_rows = bm * EXPERT_BLOCKS_PER_STEP
    assert n_rows % step_rows == 0
    used_step = lambda s, be, nb, od, ue, nue: (jnp.minimum(s, (nb[0] - 1) // EXPERT_BLOCKS_PER_STEP), 0)
    grid_spec = pltpu.PrefetchScalarGridSpec(
        num_scalar_prefetch=5,
        grid=(n_rows // step_rows,),
        in_specs=[
            pl.BlockSpec((step_rows, PACK_W), used_step),
            pl.BlockSpec(memory_space=pl.ANY),
            pl.BlockSpec(memory_space=pl.ANY),
        ],
        out_specs=pl.BlockSpec((step_rows, PACK_W), used_step),
        scratch_shapes=[
            pltpu.VMEM((2, d, 2 * EXPERT_HIDDEN), F32),
            pltpu.VMEM((2, EXPERT_HIDDEN, d), F32),
            pltpu.VMEM((d, 2 * EXPERT_HIDDEN), BF16),
            pltpu.VMEM((EXPERT_HIDDEN, d), BF16),
            pltpu.SemaphoreType.DMA((2, 2)),
        ],
    )
    return pl.pallas_call(
        functools.partial(_expert_kernel, layer=layer),
        grid_spec=grid_spec,
        out_shape=jax.ShapeDtypeStruct((n_rows, PACK_W), jnp.int32),
        compiler_params=_cparams("arbitrary"),
        name="moe_experts",
    )(block_expert, n_blocks_used, block_ordinal, used_expert, n_used_experts, xs, w_gu, w_dn)


def _sc_gather_rows(table, indices):
    n = indices.shape[0]
    width = table.shape[1]
    workers = SC_NUM_CORES * SC_NUM_SUBCORES
    assert n % SC_GATHER_WINDOW == 0
    n_windows = n // SC_GATHER_WINDOW
    mesh = plsc.VectorSubcoreMesh(core_axis_name="core", subcore_axis_name="subcore", num_cores=SC_NUM_CORES,
                                  num_subcores=SC_NUM_SUBCORES)

    @functools.partial(
        pl.kernel, out_type=jax.ShapeDtypeStruct((n, width), table.dtype), mesh=mesh,
        scratch_types=[pltpu.VMEM((SC_GATHER_WINDOW,), jnp.int32),
                       pltpu.VMEM((SC_GATHER_WINDOW, width), table.dtype),
                       pltpu.SemaphoreType.DMA],
        name="moe_sc_gather")
    def gather_kernel(table_hbm, idx_hbm, out_hbm, idx_vmem, rows_vmem, sem):
        worker = lax.axis_index("subcore") * SC_NUM_CORES + lax.axis_index("core")

        @pl.loop(0, -(-n_windows // workers))
        def _(j):
            window = j * workers + worker

            @pl.when(window < n_windows)
            def _():
                off = window * SC_GATHER_WINDOW
                pltpu.sync_copy(idx_hbm.at[pl.ds(off, SC_GATHER_WINDOW)], idx_vmem)
                pltpu.async_copy(table_hbm.at[idx_vmem], rows_vmem, sem).wait()
                pltpu.sync_copy(rows_vmem, out_hbm.at[pl.ds(off, SC_GATHER_WINDOW)])

    return gather_kernel(table, indices)


def _combine_kernel(*refs, alpha):
    y_refs = refs[:TOP_K]
    x_ref, fsh_ref, gate_ref, mod_ref, lng_ref, lnb_ref = refs[TOP_K:TOP_K + 6]
    o_ref = refs[-1]
    d = D_MODEL
    t = x_ref.shape[0]
    gate_rows = gate_ref[...]
    pad = jnp.zeros((LANES - TOP_K, t), F32)
    gate_cols = jnp.concatenate([gate_rows, pad], axis=0).T
    f_lo = fsh_ref[:, 0:PACK_W]
    f_hi = fsh_ref[:, PACK_W:]
    for k in range(TOP_K):
        y_lo, y_hi = _unpack_bf16_pairs(y_refs[k][...])
        f_lo = f_lo + gate_cols[:, k:k + 1] * y_lo
        f_hi = f_hi + gate_cols[:, k:k + 1] * y_hi
    f = jnp.concatenate([f_lo, f_hi], axis=1)
    z = alpha * x_ref[...] + mod_ref[:, 5 * d:6 * d] * f
    o_ref[...] = _layer_norm_rows(z) * lng_ref[...] + lnb_ref[...]


def _combine_call(y_tok, x, fsh, gate, mod3, ln_g, ln_b, prev_out, *, batch_index, tiles_per_batch, alpha,
                  drop_context):
    r, d = x.shape
    t = ROW_TILE
    n_batches = r // t // tiles_per_batch
    n_tiles = tiles_per_batch - 1 if drop_context else tiles_per_batch
    tile0 = batch_index * tiles_per_batch
    row = lambda i: (tile0 + i, 0)
    col = lambda i: (0, tile0 + i)
    const = lambda i: (0, 0)
    kern = functools.partial(_combine_kernel, alpha=alpha)
    y_specs = [pl.BlockSpec((t, PACK_W), functools.partial(lambda k, i: (k * tiles_per_batch + i, 0), k))
               for k in range(TOP_K)]
    in_specs = y_specs + [
        pl.BlockSpec((t, d), row),
        pl.BlockSpec((t, d), row),
        pl.BlockSpec((TOP_K, t), col),
        pl.BlockSpec((None, 1, 6 * d), lambda i: (_mod_row(tile0 + i, tiles_per_batch), 0, 0)),
        pl.BlockSpec((1, d), const),
        pl.BlockSpec((1, d), const),
    ]
    args = [y_tok] * TOP_K + [x, fsh, gate, mod3, ln_g, ln_b]
    aliases = {}
    if prev_out is not None:
        in_specs.append(pl.BlockSpec(memory_space=pl.ANY))
        args.append(prev_out)
        aliases = {len(args) - 1: 0}
    return pl.pallas_call(
        kern,
        grid=(n_tiles,),
        in_specs=in_specs,
        out_specs=pl.BlockSpec((t, d), lambda i: (batch_index * n_tiles + i, 0)),
        out_shape=jax.ShapeDtypeStruct((n_batches * n_tiles * t, d), F32),
        input_output_aliases=aliases,
        compiler_params=_cparams("arbitrary"),
        name="moe_combine",
    )(*args)


def _rope_tables(seq):
    rows = seq // GRID_W
    row = jnp.repeat(jnp.arange(rows, dtype=F32), GRID_W)
    col = jnp.tile(jnp.arange(GRID_W, dtype=F32), rows)
    nf = DA_DIM // 4
    freqs = ROPE_BASE ** (-jnp.arange(nf, dtype=F32) / nf)
    cr, sr = jnp.cos(row[:, None] * freqs), jnp.sin(row[:, None] * freqs)
    cc, sc = jnp.cos(col[:, None] * freqs), jnp.sin(col[:, None] * freqs)
    c64 = jnp.concatenate([cr, cr, cc, cc], axis=1)
    s64 = jnp.concatenate([-sr, sr, -sc, sc], axis=1)
    c = jnp.concatenate([jnp.tile(c64, (1, 2)), jnp.ones((CTX_LEN, LANES), F32)], axis=0)
    s = jnp.concatenate([jnp.tile(s64, (1, 2)), jnp.zeros((CTX_LEN, LANES), F32)], axis=0)
    return c, s


def kernel(x, c, ctx, c_ctx, w_mod, b_mod, w_in, w_out, diff_lambda, pool_w, pool_scale, ret_log_decay, ln_g, ln_b,
           w_router, router_bias, w_expert_gate_up, w_expert_down, w_shared_gate_up, w_shared_down):
    batch, seq, d = x.shape
    depth = w_mod.shape[0]
    assert d == D_MODEL and ctx.shape[1] == CTX_LEN == ROW_TILE and batch == 2
    assert seq % ROW_TILE == 0 and seq % GRID_W == 0 and w_in.shape[-1] == IN_WIDTH
    rows_per_batch = seq + CTX_LEN
    tiles_per_batch = rows_per_batch // ROW_TILE
    r = batch * rows_per_batch
    alpha = (2.0 * depth) ** 0.25

    xa = jnp.concatenate([x, ctx], axis=1).reshape(r, d)
    cvec = jnp.zeros((8, d), F32).at[0:batch].set(c).at[batch].set(c_ctx)
    mod_all = _mod_call(cvec, w_mod, b_mod)
    rope_c, rope_s = _rope_tables(seq)

    n_sorted = r * TOP_K + N_EXPERTS * EXPERT_BLOCK
    n_blocks = n_sorted // EXPERT_BLOCK

    for l in range(depth):
        lambda_init = 0.8 - 0.6 * math.exp(-0.3 * l)
        mod3 = mod_all[l].reshape(8, 1, 6 * d)
        lng = ln_g[l].reshape(2, 1, d)
        lnb = ln_b[l].reshape(2, 1, d)

        w_in_bf = w_in[l].astype(BF16)
        w_vt_bf = w_in_bf[:, QK_WIDTH:QK_WIDTH + DA_WIDTH].T
        qk, vda, u, rqkv, rg = _inproj_call(xa, mod3, w_in_bf, w_vt_bf, rope_c, rope_s, tiles_per_batch)
        da = _attn_call(diff_lambda[l], qk, vda, batch=batch, rows_per_batch=rows_per_batch, seq=seq,
                        lambda_init=lambda_init)
        o_f, o_b = _ret_call(ret_log_decay[l], rqkv, batch=batch, rows_per_batch=rows_per_batch, seq=seq)
        pool_bd = jnp.zeros((POOL_WIDTH, POOL_WIDTH), F32)
        for gi in range(len(POOL_WINDOWS)):
            sl = slice(gi * POOL_GROUP, (gi + 1) * POOL_GROUP)
            pool_bd = pool_bd.at[sl, sl].set(pool_w[l, gi])
        xa = _mixout_call(xa, da, u, o_f, o_b, rg, mod3, w_out[l].astype(BF16), pool_bd.astype(BF16),
                          pool_scale[l].reshape(1, POOL_WIDTH), lng[0], lnb[0],
                          tiles_per_batch=tiles_per_batch, seq=seq, alpha=alpha)

        wr_t = w_router[l].T
        wr_hi = wr_t.astype(BF16)
        wr_lo = (wr_t - wr_hi.astype(F32)).astype(BF16)
        tokp, idx, gate, rank, cnt, fsh = _router_call(
            xa, mod3, wr_hi, wr_lo, router_bias[l].reshape(N_EXPERTS, 1),
            w_shared_gate_up[l].astype(BF16), w_shared_down[l].astype(BF16), tiles_per_batch=tiles_per_batch)
        counts = cnt[:, 0]
        padded = (counts + EXPERT_BLOCK - 1) // EXPERT_BLOCK * EXPERT_BLOCK
        pad_end = jnp.cumsum(padded)
        offs = pad_end - padded
        expert_ids = jnp.arange(N_EXPERTS, dtype=jnp.int32)
        blk_row = jnp.arange(n_blocks, dtype=jnp.int32) * EXPERT_BLOCK
        block_expert = jnp.minimum(jnp.sum(pad_end[None, :] <= blk_row[:, None], axis=1), N_EXPERTS - 1)
        n_used = pad_end[-1:] // EXPERT_BLOCK
        used = counts > 0
        ordinal = jnp.cumsum(used) - 1
        hit = used[None, :] & (ordinal[None, :] == expert_ids[:, None])
        used_expert = jnp.sum(jnp.where(hit, expert_ids[None, :], 0), axis=1)
        n_used_experts = jnp.sum(used)[None]
        block_ordinal = ordinal[block_expert]
        slot = jnp.arange(EXPERT_BLOCK, dtype=jnp.int32)[None, :]
        first_pad = (padded - EXPERT_BLOCK)[:, None] + slot
        is_pad = (first_pad >= counts[:, None]) & (padded[:, None] > 0)
        spare = n_sorted + jnp.arange(N_EXPERTS * EXPERT_BLOCK, dtype=jnp.int32).reshape(N_EXPERTS, EXPERT_BLOCK)
        pad_rows = jnp.where(is_pad, offs[:, None] + first_pad, spare).reshape(N_EXPERTS * EXPERT_BLOCK)
        i32 = lambda a: a.astype(jnp.int32)

        dest = _dest_call(idx, rank, i32(offs).reshape(N_EXPERTS, 1))
        dest_flat = dest.reshape(TOP_K * r)
        xs = _sc_dispatch(tokp, dest_flat, i32(pad_rows), n_sorted)
        ys = _expert_call(i32(block_expert), i32(n_used), i32(block_ordinal), i32(used_expert),
                          i32(n_used_experts), xs, w_expert_gate_up, w_expert_down, l)
        x_new = None
        for b in range(batch):
            dest_b = dest[:, b * rows_per_batch:(b + 1) * rows_per_batch].reshape(TOP_K * rows_per_batch)
            y_tok = _sc_gather_rows(ys, dest_b)
            x_new = _combine_call(y_tok, xa, fsh, gate, mod3, lng[1], lnb[1], x_new, batch_index=b,
                                  tiles_per_batch=tiles_per_batch, alpha=alpha, drop_context=(l == depth - 1))
        xa = x_new

    return xa.reshape(batch, seq, d)
```

```python
import functools
import math

import jax
import jax.numpy as jnp
from jax import lax
from jax.experimental import pallas as pl
from jax.experimental.pallas import tpu as pltpu
from jax.experimental.pallas import tpu_sc as plsc

F32 = jnp.float32
BF16 = jnp.bfloat16
HIGHEST = lax.Precision.HIGHEST

D_MODEL = 1024
CTX_LEN = 256
GRID_W = 64
DA_HEADS = 4
DA_DIM = 64
DA_VDIM = 2 * DA_DIM
DA_WIDTH = DA_HEADS * DA_VDIM
ROPE_BASE = 10000.0
POOL_WINDOWS = (2, 4, 8, 16)
POOL_GROUP = 64
POOL_WIDTH = len(POOL_WINDOWS) * POOL_GROUP
POOL_HALO = 8
RET_HEADS = 4
RET_DK = 64
RET_WIDTH = RET_HEADS * RET_DK
RET_CHUNK = 128
QK_WIDTH = 2 * DA_HEADS * 2 * DA_DIM
IN_WIDTH = QK_WIDTH + DA_WIDTH + POOL_WIDTH + 4 * RET_WIDTH
N_EXPERTS = 256
TOP_K = 8
N_GROUPS = 8
GROUP_SIZE = N_EXPERTS // N_GROUPS
TOPK_GROUPS = 4
EXPERT_HIDDEN = 256
ROUTED_SCALE = 2.5
LN_EPS = 1e-6
RMS_EPS = 1e-5

LANES = 128
ROW_TILE = 256
MOD_COL_TILE = 1536
DEST_STEPS = 4
ATTN_Q_TILE = 256
ATTN_K_CHUNK = 256
ATTN_UNROLL = 16
SC_NUM_CORES = 2
SC_NUM_SUBCORES = 16
SC_GATHER_WINDOW = 128
EXPERT_BLOCK = 256
EXPERT_BLOCKS_PER_STEP = 4
SC_SPARE_ROWS = N_EXPERTS * EXPERT_BLOCK
PACK_W = D_MODEL // 2
VMEM_LIMIT = 56 * 1024 * 1024


def _cparams(*sem):
    return pltpu.CompilerParams(dimension_semantics=sem, vmem_limit_bytes=VMEM_LIMIT)


def _sigmoid(x):
    return 1.0 / (1.0 + jnp.exp(-x))


def _layer_norm_rows(x):
    mu = jnp.mean(x, axis=-1, keepdims=True)
    xc = x - mu
    var = jnp.mean(xc * xc, axis=-1, keepdims=True)
    return xc * lax.rsqrt(var + LN_EPS)


def _pack_bf16_pairs(x):
    half = x.shape[1] // 2
    bits = pltpu.bitcast(x.astype(BF16).astype(F32), jnp.uint32)
    word = lax.shift_right_logical(bits[:, 0:half], jnp.uint32(16)) | (bits[:, half:] & jnp.uint32(0xFFFF0000))
    return pltpu.bitcast(word, jnp.int32)


def _unpack_bf16_pairs(packed):
    word = pltpu.bitcast(packed, jnp.uint32)
    lo = pltpu.bitcast(lax.shift_left(word, jnp.uint32(16)), F32)
    hi = pltpu.bitcast(word & jnp.uint32(0xFFFF0000), F32)
    return lo, hi


def _mod_row(i, tiles_per_batch):
    return jnp.where(i % tiles_per_batch == tiles_per_batch - 1, 2, i // tiles_per_batch)


def _mod_kernel(c_ref, w_ref, b_ref, o_ref):
    c = c_ref[...]
    s = c * _sigmoid(c)
    o_ref[...] = jnp.dot(s, w_ref[...], precision=HIGHEST, preferred_element_type=F32) + b_ref[...]


def _mod_call(cvec, w_mod, b_mod):
    depth, d, n = w_mod.shape
    tn = MOD_COL_TILE
    assert n % tn == 0
    return pl.pallas_call(
        _mod_kernel,
        grid=(depth, n // tn),
        in_specs=[
            pl.BlockSpec((8, d), lambda l, j: (0, 0)),
            pl.BlockSpec((None, d, tn), lambda l, j: (l, 0, j)),
            pl.BlockSpec((None, 1, tn), lambda l, j: (l, 0, j)),
        ],
        out_specs=pl.BlockSpec((None, 8, tn), lambda l, j: (l, 0, j)),
        out_shape=jax.ShapeDtypeStruct((depth, 8, n), F32),
        compiler_params=_cparams("arbitrary", "arbitrary"),
        name="mod",
    )(cvec, w_mod, b_mod.reshape(depth, 1, n))


def _inproj_kernel(x_ref, mod_a, mod_b, w_ref, wvt_ref, ct_a, ct_b, st_a, st_b, qk_ref, vt_ref, u_ref, r_ref,
                   g_ref):
    d = D_MODEL
    t = ROW_TILE
    halves = []
    for half, mod_ref in enumerate((mod_a, mod_b)):
        xn = _layer_norm_rows(x_ref[half * t:(half + 1) * t, :])
        halves.append((xn * (1.0 + mod_ref[:, d:2 * d]) + mod_ref[:, 0:d]).astype(BF16))
    h = jnp.concatenate(halves, axis=0)

    a = jnp.dot(h, w_ref[:, 0:QK_WIDTH], preferred_element_type=F32)
    lane = lax.broadcasted_iota(jnp.int32, (a.shape[0], LANES), 1)
    first_half = (lane % 32) < 16
    ct = jnp.concatenate([ct_a[...], ct_b[...]], axis=0)
    st = jnp.concatenate([st_a[...], st_b[...]], axis=0)
    for s in range(QK_WIDTH // LANES):
        blk = a[:, s * LANES:(s + 1) * LANES]
        partner = jnp.where(first_half, pltpu.roll(blk, LANES - 16, 1), pltpu.roll(blk, 16, 1))
        rot = blk * ct + partner * st
        if s < QK_WIDTH // LANES // 2:
            rot = rot * (DA_DIM ** -0.5 * math.log2(math.e))
        qk_ref[:, s * LANES:(s + 1) * LANES] = rot.astype(BF16)

    vt_ref[...] = lax.dot_general(wvt_ref[...], h, (((1,), (1,)), ((), ())),
                                  preferred_element_type=F32).astype(BF16)
    o = QK_WIDTH + DA_WIDTH
    u_ref[...] = jnp.dot(h, w_ref[:, o:o + POOL_WIDTH], preferred_element_type=F32)
    o += POOL_WIDTH
    r = jnp.dot(h, w_ref[:, o:o + 3 * RET_WIDTH], preferred_element_type=F32)
    r_ref[:, 0:RET_WIDTH] = r[:, 0:RET_WIDTH].astype(BF16)
    r_ref[:, RET_WIDTH:2 * RET_WIDTH] = (r[:, RET_WIDTH:2 * RET_WIDTH] * (RET_DK ** -0.5)).astype(BF16)
    r_ref[:, 2 * RET_WIDTH:] = r[:, 2 * RET_WIDTH:].astype(BF16)
    o += 3 * RET_WIDTH
    g_ref[...] = jnp.dot(h, w_ref[:, o:o + RET_WIDTH], preferred_element_type=F32)


def _inproj_call(x, mod3, w_in_bf, w_vt_bf, rope_c, rope_s, tiles_per_batch):
    r, d = x.shape
    t = ROW_TILE
    nt = r // t
    assert nt % 2 == 0
    row = lambda i: (i, 0)
    mod_spec = lambda half: pl.BlockSpec((None, 1, 6 * d),
                                         lambda i: (_mod_row(2 * i + half, tiles_per_batch), 0, 0))
    rope_spec = lambda half: pl.BlockSpec((t, LANES), lambda i: ((2 * i + half) % tiles_per_batch, 0))
    return pl.pallas_call(
        _inproj_kernel,
        grid=(nt // 2,),
        in_specs=[
            pl.BlockSpec((2 * t, d), row),
            mod_spec(0),
            mod_spec(1),
            pl.BlockSpec((d, IN_WIDTH), lambda i: (0, 0)),
            pl.BlockSpec((DA_WIDTH, d), lambda i: (0, 0)),
            rope_spec(0),
            rope_spec(1),
            rope_spec(0),
            rope_spec(1),
        ],
        out_specs=[
            pl.BlockSpec((2 * t, QK_WIDTH), row),
            pl.BlockSpec((DA_WIDTH, 2 * t), lambda i: (0, i)),
            pl.BlockSpec((2 * t, POOL_WIDTH), row),
            pl.BlockSpec((2 * t, 3 * RET_WIDTH), row),
            pl.BlockSpec((2 * t, RET_WIDTH), row),
        ],
        out_shape=[
            jax.ShapeDtypeStruct((r, QK_WIDTH), BF16),
            jax.ShapeDtypeStruct((DA_WIDTH, r), BF16),
            jax.ShapeDtypeStruct((r, POOL_WIDTH), F32),
            jax.ShapeDtypeStruct((r, 3 * RET_WIDTH), BF16),
            jax.ShapeDtypeStruct((r, RET_WIDTH), F32),
        ],
        compiler_params=_cparams("arbitrary"),
        name="inproj",
    )(x, mod3, mod3, w_in_bf, w_vt_bf, rope_c, rope_c, rope_s, rope_s)


def _attn_kernel(lam_ref, q_ref, k_ref, vt_ref, o_ref, *s_refs, k_chunk, seq, lambda_init):
    mq = ATTN_Q_TILE
    n_tiles = (seq + CTX_LEN) // mq
    n_chunks = (seq + CTX_LEN) // k_chunk
    last = n_chunks - 1
    n_iters = last // ATTN_UNROLL
    neg_inf = jnp.full((1, 2 * mq), -jnp.inf, F32)
    acc_zero = jnp.zeros((DA_VDIM + 16, 2 * mq), F32)
    ones_rows = jnp.where(lax.broadcasted_iota(jnp.int32, (16, k_chunk), 0) == 0, 1.0, 0.0).astype(BF16)

    def tile_rows(i):
        return pl.ds(pl.multiple_of(i * mq, mq), mq)

    def q_transposed(i):
        q = q_ref[tile_rows(i), :]
        lane = lax.broadcasted_iota(jnp.int32, q.shape, 1)
        zero = jnp.zeros_like(q)
        q2 = jnp.concatenate([jnp.where(lane < DA_DIM, q, zero), jnp.where(lane >= DA_DIM, q, zero)], axis=0)
        return q2.astype(F32).T.astype(BF16)

    def score_chunk(s_ref, c, qt, m):
        off = pl.multiple_of(c * k_chunk, k_chunk)
        s = jnp.dot(k_ref[pl.ds(off, k_chunk), :], qt, preferred_element_type=F32)
        s_ref[c] = s
        return jnp.maximum(m, jnp.max(s, axis=0, keepdims=True))

    def score_chunk_pair(s_ref, c, qt, m):
        off = pl.multiple_of(c * k_chunk, 2 * k_chunk)
        s = jnp.dot(k_ref[pl.ds(off, 2 * k_chunk), :], qt, preferred_element_type=F32)
        s_ref[c] = s[0:k_chunk]
        s_ref[c + 1] = s[k_chunk:]
        return jnp.maximum(m, jnp.max(s, axis=0, keepdims=True))

    def value_chunk(s_ref, c, m, acc):
        off = pl.multiple_of(c * k_chunk, k_chunk)
        vt = jnp.concatenate([vt_ref[:, pl.ds(off, k_chunk)], ones_rows], axis=0)
        p = jnp.exp2((s_ref[c] - m).astype(BF16))
        return acc + jnp.dot(vt, p, preferred_element_type=F32)

    def finish(i, acc):
        l0, l1 = acc[DA_VDIM:DA_VDIM + 1, 0:mq], acc[DA_VDIM:DA_VDIM + 1, mq:]
        a0, a1 = acc[0:DA_VDIM, 0:mq], acc[0:DA_VDIM, mq:]
        lv = lam_ref[...]
        lam = (jnp.exp(jnp.sum(lv[0:1] * lv[1:2], axis=-1, keepdims=True))
               - jnp.exp(jnp.sum(lv[2:3] * lv[3:4], axis=-1, keepdims=True)) + lambda_init)
        o = a0 / l0 - lam * (a1 / l1)
        o = o * lax.rsqrt(jnp.mean(o * o, axis=0, keepdims=True) + RMS_EPS) * (1.0 - lambda_init)
        o_ref[tile_rows(i), :] = o.T.astype(BF16)

    def scores_only(s_ref, qt):
        def body(it, m):
            for u in range(ATTN_UNROLL):
                m = score_chunk(s_ref, it * ATTN_UNROLL + u, qt, m)
            return m
        return score_chunk(s_ref, last, qt, lax.fori_loop(0, n_iters, body, neg_inf))

    def values_only(s_ref, m):
        def body(it, acc):
            for u in range(ATTN_UNROLL):
                acc = value_chunk(s_ref, it * ATTN_UNROLL + u, m, acc)
            return acc
        return value_chunk(s_ref, last, m, lax.fori_loop(0, n_iters, body, acc_zero))

    def fused_tile(i, m_prev, s_cur, s_prev):
        qt = q_transposed(i)

        def body(it, carry):
            m, acc = carry
            for u in range(0, ATTN_UNROLL, 2):
                c = it * ATTN_UNROLL + u
                m = score_chunk_pair(s_cur, c, qt, m)
                acc = value_chunk(s_prev, c, m_prev, acc)
                acc = value_chunk(s_prev, c + 1, m_prev, acc)
            return m, acc

        m, acc = lax.fori_loop(0, n_iters, body, (neg_inf, acc_zero))
        m = score_chunk(s_cur, last, qt, m)
        finish(i - 1, value_chunk(s_prev, last, m_prev, acc))
        return m

    s_even, s_odd = s_refs
    ctx_tile = n_tiles - 1
    assert ctx_tile % 2 == 0 and ctx_tile >= 2
    m = scores_only(s_even, q_transposed(0))

    def tile_pair(p, m):
        m = fused_tile(2 * p + 1, m, s_odd, s_even)
        return fused_tile(2 * p + 2, m, s_even, s_odd)

    m = lax.fori_loop(0, (ctx_tile - 2) // 2, tile_pair, m)
    m = fused_tile(ctx_tile - 1, m, s_odd, s_even)
    m_ctx = score_chunk(s_even, last, q_transposed(ctx_tile), neg_inf)
    finish(ctx_tile - 1, values_only(s_odd, m))
    finish(ctx_tile, value_chunk(s_even, last, m_ctx, acc_zero))


def _attn_call(lam_vec, qk, vda, *, batch, rows_per_batch, seq, lambda_init):
    tq = ATTN_Q_TILE
    assert seq % (ATTN_K_CHUNK * ATTN_UNROLL) == 0 and rows_per_batch - seq == CTX_LEN == tq == ATTN_K_CHUNK
    nq = rows_per_batch // tq
    kern = functools.partial(_attn_kernel, k_chunk=ATTN_K_CHUNK, seq=seq, lambda_init=lambda_init)
    return pl.pallas_call(
        kern,
        grid=(batch, DA_HEADS),
        in_specs=[
            pl.BlockSpec((4, DA_DIM), lambda b, h: (0, 0)),
            pl.BlockSpec((rows_per_batch, DA_VDIM), lambda b, h: (b, h)),
            pl.BlockSpec((rows_per_batch, DA_VDIM), lambda b, h: (b, DA_HEADS + h)),
            pl.BlockSpec((DA_VDIM, rows_per_batch), lambda b, h: (h, b)),
        ],
        out_specs=pl.BlockSpec((rows_per_batch, DA_VDIM), lambda b, h: (b, h)),
        out_shape=jax.ShapeDtypeStruct((qk.shape[0], DA_WIDTH), BF16),
        scratch_shapes=[pltpu.VMEM((rows_per_batch // ATTN_K_CHUNK, ATTN_K_CHUNK, 2 * tq), F32)] * 2,
        compiler_params=_cparams("arbitrary", "arbitrary"),
        name="diff_attn",
    )(lam_vec, qk, qk, vda)


def _ret_kernel(ld_ref, f_ref, b_ref, of_ref, ob_ref, dm_ref, qd_ref, kd_ref, cd_ref, st_ref):
    c = pl.program_id(1)
    ch = RET_CHUNK
    w = RET_WIDTH
    lane_head = lax.broadcasted_iota(jnp.int32, (1, w), 1) // RET_DK

    @pl.when(c == 0)
    def _():
        st_ref[...] = jnp.zeros_like(st_ref)
        ri = lax.broadcasted_iota(jnp.int32, (ch, ch), 0)
        ci = lax.broadcasted_iota(jnp.int32, (ch, ch), 1)
        rowf = lax.broadcasted_iota(jnp.int32, (ch, w), 0).astype(F32)
        for d in range(2):
            lg_lane = jnp.zeros((1, w), F32)
            for hh in range(RET_HEADS):
                lg = -jnp.exp(jnp.full((1, 1), ld_ref[d, hh], F32))
                lg_lane = jnp.where(lane_head == hh, lg, lg_lane)
                dist = ((ri - ci) if d == 0 else (ci - ri)).astype(F32)
                dm_ref[d, hh] = jnp.where(dist >= 0, jnp.exp(dist * lg), 0.0)
            if d == 0:
                qd_ref[d] = jnp.exp((rowf + 1.0) * lg_lane)
                kd_ref[d] = jnp.exp((ch - 1.0 - rowf) * lg_lane)
            else:
                qd_ref[d] = jnp.exp((ch - rowf) * lg_lane)
                kd_ref[d] = jnp.exp(rowf * lg_lane)
            cd_ref[d] = jnp.exp(float(ch) * lg_lane)

    rblk = lax.broadcasted_iota(jnp.int32, (w, w), 0) // RET_DK
    cblk = lax.broadcasted_iota(jnp.int32, (w, w), 1) // RET_DK
    for d, (src, dst) in enumerate(((f_ref, of_ref), (b_ref, ob_ref))):
        q = src[:, 0:w]
        k = src[:, w:2 * w]
        v = src[:, 2 * w:3 * w]
        st = st_ref[d]
        o = jnp.dot((q.astype(F32) * qd_ref[d]).astype(BF16), st.astype(BF16), preferred_element_type=F32)
        for hh in range(RET_HEADS):
            in_head = lane_head == hh
            qm = jnp.where(in_head, q, jnp.zeros_like(q))
            s = lax.dot_general(qm, k, (((1,), (1,)), ((), ())), preferred_element_type=F32)
            intra = (s * dm_ref[d, hh]).astype(BF16)
            o = o + jnp.where(in_head, jnp.dot(intra, v, preferred_element_type=F32), 0.0)
        dst[...] = o
        kk_t = (k.astype(F32) * kd_ref[d]).T.astype(BF16)
        upd = jnp.dot(kk_t, v, preferred_element_type=F32)
        st_ref[d] = jnp.where(rblk == cblk, st * cd_ref[d] + upd, 0.0)


def _ret_call(log_decay, rqkv, *, batch, rows_per_batch, seq):
    ch = RET_CHUNK
    nc = rows_per_batch // ch
    n_lat = seq // ch
    n_ctx = nc - n_lat

    def fwd(b, c):
        return (b * nc + jnp.where(c < n_ctx, n_lat + c, c - n_ctx), 0)

    def bwd(b, c):
        return (b * nc + nc - 1 - c, 0)

    w = RET_WIDTH
    return pl.pallas_call(
        _ret_kernel,
        grid=(batch, nc),
        in_specs=[
            pl.BlockSpec(memory_space=pltpu.SMEM),
            pl.BlockSpec((ch, 3 * w), fwd),
            pl.BlockSpec((ch, 3 * w), bwd),
        ],
        out_specs=[pl.BlockSpec((ch, w), fwd), pl.BlockSpec((ch, w), bwd)],
        out_shape=[jax.ShapeDtypeStruct((rqkv.shape[0], w), F32)] * 2,
        scratch_shapes=[
            pltpu.VMEM((2, RET_HEADS, ch, ch), F32),
            pltpu.VMEM((2, ch, w), F32),
            pltpu.VMEM((2, ch, w), F32),
            pltpu.VMEM((2, 1, w), F32),
            pltpu.VMEM((2, w, w), F32),
        ],
        compiler_params=_cparams("arbitrary", "arbitrary"),
        name="retention",
    )(log_decay, rqkv, rqkv)


def _mixout_kernel(x_ref, da_ref, u_ref, up_ref, un_ref, of_ref, ob_ref, rg_ref, mod_ref, wo_ref, pw_ref,
                   ps_ref, lng_ref, lnb_ref, o_ref, *, tiles_per_batch, seq, alpha):
    d = D_MODEL
    t = x_ref.shape[0]
    i = pl.program_id(0)
    j = i % tiles_per_batch
    is_ctx = j == tiles_per_batch - 1
    stream_len = jnp.where(is_ctx, CTX_LEN, seq)
    p0 = jnp.where(is_ctx, 0, j * t)

    u = u_ref[...]
    prev = jnp.where(p0 > 0, up_ref[...], 0.0)
    nxt = jnp.where(p0 + t < stream_len, un_ref[...], 0.0)
    ext = jnp.concatenate([prev, u, nxt], axis=0)
    n = t + 2 * POOL_HALO
    a2 = ext + pltpu.roll(ext, 1, 0)
    a4 = pltpu.roll(a2, 1, 0) + pltpu.roll(a2, n - 1, 0)
    a8 = pltpu.roll(a4, 2, 0) + pltpu.roll(a4, n - 2, 0)
    a16 = pltpu.roll(a8, 4, 0) + pltpu.roll(a8, n - 4, 0)
    pos = p0 + lax.broadcasted_iota(jnp.int32, (t, POOL_WIDTH), 0)
    group = lax.broadcasted_iota(jnp.int32, (1, POOL_WIDTH), 1) // POOL_GROUP
    mean = jnp.zeros((t, POOL_WIDTH), F32)
    for gi, (wnd, asum) in enumerate(zip(POOL_WINDOWS, (a2, a4, a8, a16))):
        cnt = jnp.minimum(pos + wnd // 2, stream_len) - jnp.maximum(pos - wnd // 2, 0)
        mean = jnp.where(group == gi, asum[POOL_HALO:POOL_HALO + t] / cnt.astype(F32), mean)
    pool = jnp.dot((mean - u).astype(BF16), pw_ref[...], preferred_element_type=F32) * ps_ref[...]

    o = of_ref[...] + ob_ref[...]
    head = lax.broadcasted_iota(jnp.int32, (1, RET_WIDTH), 1) // RET_DK

    def head_mean(val):
        out = jnp.zeros_like(val)
        for hh in range(RET_HEADS):
            m = jnp.sum(jnp.where(head == hh, val, 0.0), axis=-1, keepdims=True) * (1.0 / RET_DK)
            out = jnp.where(head == hh, m, out)
        return out

    oc = o - head_mean(o)
    rn = oc * lax.rsqrt(head_mean(oc * oc) + LN_EPS)
    g = rg_ref[...]
    ret = rn * (g * _sigmoid(g))

    y = jnp.dot(da_ref[...], wo_ref[0:DA_WIDTH, :], preferred_element_type=F32)
    y = y + jnp.dot(pool.astype(BF16), wo_ref[DA_WIDTH:DA_WIDTH + POOL_WIDTH, :], preferred_element_type=F32)
    y = y + jnp.dot(ret.astype(BF16), wo_ref[DA_WIDTH + POOL_WIDTH:, :], preferred_element_type=F32)
    z = alpha * x_ref[...] + mod_ref[:, 2 * d:3 * d] * y
    o_ref[...] = _layer_norm_rows(z) * lng_ref[...] + lnb_ref[...]


def _mixout_call(x, da, u, o_f, o_b, rg, mod3, w_out_bf, pool_bd, pool_scale, ln_g, ln_b, *, tiles_per_batch, seq,
                 alpha):
    r, d = x.shape
    t = ROW_TILE
    nt = r // t
    hb = t // POOL_HALO
    n_halo_blocks = r // POOL_HALO
    row = lambda i: (i, 0)
    const = lambda i: (0, 0)
    kern = functools.partial(_mixout_kernel, tiles_per_batch=tiles_per_batch, seq=seq, alpha=alpha)
    return pl.pallas_call(
        kern,
        grid=(nt,),
        in_specs=[
            pl.BlockSpec((t, d), row),
            pl.BlockSpec((t, DA_WIDTH), row),
            pl.BlockSpec((t, POOL_WIDTH), row),
            pl.BlockSpec((POOL_HALO, POOL_WIDTH), lambda i: (jnp.maximum(i * hb - 1, 0), 0)),
            pl.BlockSpec((POOL_HALO, POOL_WIDTH), lambda i: (jnp.minimum((i + 1) * hb, n_halo_blocks - 1), 0)),
            pl.BlockSpec((t, RET_WIDTH), row),
            pl.BlockSpec((t, RET_WIDTH), row),
            pl.BlockSpec((t, RET_WIDTH), row),
            pl.BlockSpec((None, 1, 6 * d), lambda i: (_mod_row(i, tiles_per_batch), 0, 0)),
            pl.BlockSpec((d, d), const),
            pl.BlockSpec((POOL_WIDTH, POOL_WIDTH), const),
            pl.BlockSpec((1, POOL_WIDTH), const),
            pl.BlockSpec((1, d), const),
            pl.BlockSpec((1, d), const),
        ],
        out_specs=pl.BlockSpec((t, d), row),
        out_shape=jax.ShapeDtypeStruct((r, d), F32),
        compiler_params=_cparams("arbitrary"),
        name="mixer_out",
    )(x, da, u, u, u, o_f, o_b, rg, mod3, w_out_bf, pool_bd, pool_scale, ln_g, ln_b)


def _router_kernel(x_ref, mod_ref, wrh_ref, wrl_ref, bias_ref, wsgu_ref, wsdn_ref,
                   tokp_ref, idx_ref, gate_ref, rank_ref, cnt_ref, fsh_ref, carry_ref):
    d = D_MODEL
    t = x_ref.shape[0]
    ne = N_EXPERTS
    neg = -jnp.inf

    @pl.when(pl.program_id(0) == 0)
    def _():
        carry_ref[...] = jnp.zeros_like(carry_ref)

    tok = _layer_norm_rows(x_ref[...]) * (1.0 + mod_ref[:, 4 * d:5 * d]) + mod_ref[:, 3 * d:4 * d]
    tok_hi = tok.astype(BF16)
    tok_lo = (tok - tok_hi.astype(F32)).astype(BF16)

    tokp_ref[...] = _pack_bf16_pairs(tok)

    hs = jnp.dot(tok_hi, wsgu_ref[...], preferred_element_type=F32)
    gs, us = hs[:, 0:EXPERT_HIDDEN], hs[:, EXPERT_HIDDEN:]
    fsh_ref[...] = jnp.dot((gs * _sigmoid(gs) * us).astype(BF16), wsdn_ref[...], preferred_element_type=F32)

    nt_dims = (((1,), (1,)), ((), ()))
    logits = (lax.dot_general(wrh_ref[...], tok_hi, nt_dims, preferred_element_type=F32)
              + lax.dot_general(wrh_ref[...], tok_lo, nt_dims, preferred_element_type=F32)
              + lax.dot_general(wrl_ref[...], tok_hi, nt_dims, preferred_element_type=F32))
    scores = _sigmoid(logits)
    biased = scores + bias_ref[...]

    gidx = lax.broadcasted_iota(jnp.int32, (GROUP_SIZE, t), 0)
    blocks, gscores = [], []
    for g in range(N_GROUPS):
        blk = biased[g * GROUP_SIZE:(g + 1) * GROUP_SIZE, :]
        m1 = jnp.max(blk, axis=0, keepdims=True)
        first = jnp.min(jnp.where(blk == m1, gidx, GROUP_SIZE), axis=0, keepdims=True)
        m2 = jnp.max(jnp.where(gidx == first, neg, blk), axis=0, keepdims=True)
        blocks.append(blk)
        gscores.append(m1 + m2)

    keep = [jnp.zeros((1, t), F32) for _ in range(N_GROUPS)]
    for _ in range(TOPK_GROUPS):
        m = gscores[0]
        for gs_ in gscores[1:]:
            m = jnp.maximum(m, gs_)
        found = jnp.zeros((1, t), F32)
        for g in range(N_GROUPS):
            hit = jnp.where(gscores[g] == m, 1.0 - found, 0.0)
            found = found + hit
            keep[g] = keep[g] + hit
            gscores[g] = jnp.where(hit > 0.0, neg, gscores[g])
    masked = jnp.concatenate([jnp.where(keep[g] > 0.0, blocks[g], neg) for g in range(N_GROUPS)], axis=0)

    ei = lax.broadcasted_iota(jnp.int32, (ne, t), 0)
    cur = masked
    onehot = jnp.zeros((ne, t), F32)
    idxs, gates = [], []
    for _ in range(TOP_K):
        m = jnp.max(cur, axis=0, keepdims=True)
        ii = jnp.min(jnp.where(cur == m, ei, ne), axis=0, keepdims=True)
        sel = ei == ii
        idxs.append(ii)
        gates.append(jnp.sum(jnp.where(sel, scores, 0.0), axis=0, keepdims=True))
        onehot = jnp.where(sel, 1.0, onehot)
        cur = jnp.where(sel, neg, cur)
    gsum = gates[0]
    for gk in gates[1:]:
        gsum = gsum + gk
    for k in range(TOP_K):
        idx_ref[k:k + 1, :] = idxs[k]
        gate_ref[k:k + 1, :] = gates[k] / gsum * ROUTED_SCALE

    ti = lax.broadcasted_iota(jnp.int32, (t, t), 0)
    tj = lax.broadcasted_iota(jnp.int32, (t, t), 1)
    before = jnp.where(ti < tj, 1.0, 0.0).astype(BF16)
    prefix = jnp.dot(onehot.astype(BF16), before, preferred_element_type=F32) + carry_ref[:, 0:1]
    for k in range(TOP_K):
        rank_k = jnp.sum(jnp.where(ei == idxs[k], prefix, 0.0), axis=0, keepdims=True)
        rank_ref[k:k + 1, :] = rank_k.astype(jnp.int32)
    carry_ref[...] = carry_ref[...] + jnp.sum(onehot, axis=1, keepdims=True)
    cnt_ref[...] = carry_ref[...].astype(jnp.int32)


def _router_call(x, mod3, wr_hi, wr_lo, bias_col, ws_gu_bf, ws_dn_bf, *, tiles_per_batch):
    r, d = x.shape
    t = ROW_TILE
    nt = r // t
    row = lambda i: (i, 0)
    col = lambda i: (0, i)
    const = lambda i: (0, 0)
    return pl.pallas_call(
        _router_kernel,
        grid=(nt,),
        in_specs=[
            pl.BlockSpec((t, d), row),
            pl.BlockSpec((None, 1, 6 * d), lambda i: (_mod_row(i, tiles_per_batch), 0, 0)),
            pl.BlockSpec((N_EXPERTS, d), const),
            pl.BlockSpec((N_EXPERTS, d), const),
            pl.BlockSpec((N_EXPERTS, 1), const),
            pl.BlockSpec((d, 2 * EXPERT_HIDDEN), const),
            pl.BlockSpec((EXPERT_HIDDEN, d), const),
        ],
        out_specs=[
            pl.BlockSpec((t, PACK_W), row),
            pl.BlockSpec((TOP_K, t), col),
            pl.BlockSpec((TOP_K, t), col),
            pl.BlockSpec((TOP_K, t), col),
            pl.BlockSpec((N_EXPERTS, LANES), const),
            pl.BlockSpec((t, d), row),
        ],
        out_shape=[
            jax.ShapeDtypeStruct((r, PACK_W), jnp.int32),
            jax.ShapeDtypeStruct((TOP_K, r), jnp.int32),
            jax.ShapeDtypeStruct((TOP_K, r), F32),
            jax.ShapeDtypeStruct((TOP_K, r), jnp.int32),
            jax.ShapeDtypeStruct((N_EXPERTS, LANES), jnp.int32),
            jax.ShapeDtypeStruct((r, d), F32),
        ],
        scratch_shapes=[pltpu.VMEM((N_EXPERTS, LANES), F32)],
        compiler_params=_cparams("arbitrary"),
        name="router",
    )(x, mod3, wr_hi, wr_lo, bias_col, ws_gu_bf, ws_dn_bf)


def _dest_kernel(idx_ref, rank_ref, offs_ref, dest_ref):
    t = idx_ref.shape[1]
    ei = lax.broadcasted_iota(jnp.int32, (N_EXPERTS, t), 0)
    offs = offs_ref[...].astype(F32)
    for k in range(TOP_K):
        start = jnp.sum(jnp.where(ei == idx_ref[k:k + 1, :], offs, 0.0), axis=0, keepdims=True)
        dest_ref[k:k + 1, :] = start.astype(jnp.int32) + rank_ref[k:k + 1, :]


def _dest_call(idx, rank, offs_col):
    r = idx.shape[1]
    t = r // DEST_STEPS
    assert r % DEST_STEPS == 0 and t % LANES == 0
    col = lambda i: (0, i)
    return pl.pallas_call(
        _dest_kernel,
        grid=(r // t,),
        in_specs=[pl.BlockSpec((TOP_K, t), col), pl.BlockSpec((TOP_K, t), col),
                  pl.BlockSpec((N_EXPERTS, 1), lambda i: (0, 0))],
        out_specs=pl.BlockSpec((TOP_K, t), col),
        out_shape=jax.ShapeDtypeStruct((TOP_K, r), jnp.int32),
        compiler_params=_cparams("arbitrary"),
        name="moe_dest",
    )(idx, rank, offs_col)


def _sc_dispatch(tokp, dest_flat, pad_rows, n_sorted):
    r, width = tokp.shape
    win = SC_GATHER_WINDOW
    workers = SC_NUM_CORES * SC_NUM_SUBCORES
    token_windows = r // win
    n_pad_windows = pad_rows.shape[0] // win
    assert r % win == 0 and dest_flat.shape[0] == TOP_K * r and n_pad_windows % workers == 0
    windows_per_worker = -(-token_windows // workers)
    pads_per_worker = n_pad_windows // workers
    mesh = plsc.VectorSubcoreMesh(core_axis_name="core", subcore_axis_name="subcore", num_cores=SC_NUM_CORES,
                                  num_subcores=SC_NUM_SUBCORES)
    zero_rows = jnp.zeros((win, width), tokp.dtype)

    @functools.partial(
        pl.kernel, out_type=jax.ShapeDtypeStruct((n_sorted + SC_SPARE_ROWS, width), tokp.dtype), mesh=mesh,
        scratch_types=[pltpu.VMEM((win,), jnp.int32), pltpu.VMEM((win, width), tokp.dtype),
                       pltpu.SemaphoreType.DMA],
        name="moe_sc_dispatch")
    def dispatch_kernel(tok_hbm, dest_hbm, pad_hbm, zero_hbm, xs_hbm, idx_vmem, rows_vmem, sem):
        worker = lax.axis_index("subcore") * SC_NUM_CORES + lax.axis_index("core")

        @pl.loop(0, windows_per_worker)
        def _(j):
            window = j * workers + worker

            @pl.when(window < token_windows)
            def _():
                tok0 = window * win
                pltpu.sync_copy(tok_hbm.at[pl.ds(tok0, win)], rows_vmem)
                for k in range(TOP_K):
                    pltpu.sync_copy(dest_hbm.at[pl.ds(k * r + tok0, win)], idx_vmem)
                    pltpu.async_copy(rows_vmem, xs_hbm.at[idx_vmem], sem).wait()

        pltpu.sync_copy(zero_hbm, rows_vmem)

        @pl.loop(0, pads_per_worker)
        def _(j):
            off = (worker * pads_per_worker + j) * win
            pltpu.sync_copy(pad_hbm.at[pl.ds(off, win)], idx_vmem)
            pltpu.async_copy(rows_vmem, xs_hbm.at[idx_vmem], sem).wait()

    return dispatch_kernel(tokp, dest_flat, pad_rows, zero_rows)


def _expert_kernel(be_ref, nb_ref, ord_ref, ue_ref, nue_ref, xs_ref, wgu_hbm, wdn_hbm, ys_ref, wgu_f32, wdn_f32,
                   wgu_bf, wdn_bf, sems, *, layer):
    def weight_copies(o):
        slot = o % 2
        e = ue_ref[o]
        return (pltpu.make_async_copy(wgu_hbm.at[layer, e], wgu_f32.at[slot], sems.at[0, slot]),
                pltpu.make_async_copy(wdn_hbm.at[layer, e], wdn_f32.at[slot], sems.at[1, slot]))

    def start_weights(o):
        @pl.when(o < nue_ref[0])
        def _():
            for cp in weight_copies(o):
                cp.start()

    def prepare_expert(j):
        o = ord_ref[j]
        changed = jnp.logical_or(j == 0, be_ref[j] != be_ref[jnp.maximum(j - 1, 0)])

        @pl.when(j == 0)
        def _():
            start_weights(0)
            start_weights(1)

        @pl.when(changed)
        def _():
            for cp in weight_copies(o):
                cp.wait()
            slot = o % 2
            wgu_bf[...] = wgu_f32[slot].astype(BF16)
            wdn_bf[...] = wdn_f32[slot].astype(BF16)
            start_weights(o + 2)

    def compute(sub, n_blocks):
        n_rows = n_blocks * EXPERT_BLOCK
        rows = pl.ds(pl.multiple_of(sub * EXPERT_BLOCK, EXPERT_BLOCK), n_rows)
        x_lo, x_hi = _unpack_bf16_pairs(xs_ref[rows, :])
        h = (jnp.dot(x_lo.astype(BF16), wgu_bf[0:PACK_W, :], preferred_element_type=F32)
             + jnp.dot(x_hi.astype(BF16), wgu_bf[PACK_W:, :], preferred_element_type=F32))
        g, u = h[:, 0:EXPERT_HIDDEN], h[:, EXPERT_HIDDEN:]
        y = jnp.dot((g * _sigmoid(g) * u).astype(BF16), wdn_bf[...], preferred_element_type=F32)
        ys_ref[rows, :] = _pack_bf16_pairs(y)

    def block_pair(pair, carry):
        nb = nb_ref[0]
        sub0 = 2 * pair
        j0 = pl.program_id(0) * EXPERT_BLOCKS_PER_STEP + sub0
        j1 = j0 + 1
        same = jnp.logical_and(j1 < nb, be_ref[j1] == be_ref[j0])

        @pl.when(j0 < nb)
        def _():
            prepare_expert(j0)

        @pl.when(jnp.logical_and(j0 < nb, same))
        def _():
            compute(sub0, 2)

        @pl.when(jnp.logical_and(j0 < nb, jnp.logical_not(same)))
        def _():
            compute(sub0, 1)

        @pl.when(jnp.logical_and(j1 < nb, jnp.logical_not(same)))
        def _():
            prepare_expert(j1)
            compute(sub0 + 1, 1)

        return carry

    lax.fori_loop(0, EXPERT_BLOCKS_PER_STEP // 2, block_pair, 0)


def _expert_call(block_expert, n_blocks_used, block_ordinal, used_expert, n_used_experts, xs, w_gu, w_dn, layer):
    n_rows = block_expert.shape[0] * EXPERT_BLOCK
    bm = EXPERT_BLOCK
    d = D_MODEL
    step_rows = bm * EXPERT_BLOCKS_PER_STEP
    assert n_rows % step_rows == 0
    used_step = lambda s, be, nb, od, ue, nue: (jnp.minimum(s, (nb[0] - 1) // EXPERT_BLOCKS_PER_STEP), 0)
    grid_spec = pltpu.PrefetchScalarGridSpec(
        num_scalar_prefetch=5,
        grid=(n_rows // step_rows,),
        in_specs=[
            pl.BlockSpec((step_rows, PACK_W), used_step),
            pl.BlockSpec(memory_space=pl.ANY),
            pl.BlockSpec(memory_space=pl.ANY),
        ],
        out_specs=pl.BlockSpec((step_rows, PACK_W), used_step),
        scratch_shapes=[
            pltpu.VMEM((2, d, 2 * EXPERT_HIDDEN), F32),
            pltpu.VMEM((2, EXPERT_HIDDEN, d), F32),
            pltpu.VMEM((d, 2 * EXPERT_HIDDEN), BF16),
            pltpu.VMEM((EXPERT_HIDDEN, d), BF16),
            pltpu.SemaphoreType.DMA((2, 2)),
        ],
    )
    return pl.pallas_call(
        functools.partial(_expert_kernel, layer=layer),
        grid_spec=grid_spec,
        out_shape=jax.ShapeDtypeStruct((n_rows, PACK_W), jnp.int32),
        compiler_params=_cparams("arbitrary"),
        name="moe_experts",
    )(block_expert, n_blocks_used, block_ordinal, used_expert, n_used_experts, xs, w_gu, w_dn)


def _sc_gather_rows(table, indices):
    n = indices.shape[0]
    width = table.shape[1]
    workers = SC_NUM_CORES * SC_NUM_SUBCORES
    assert n % SC_GATHER_WINDOW == 0
    n_windows = n // SC_GATHER_WINDOW
    mesh = plsc.VectorSubcoreMesh(core_axis_name="core", subcore_axis_name="subcore", num_cores=SC_NUM_CORES,
                                  num_subcores=SC_NUM_SUBCORES)

    @functools.partial(
        pl.kernel, out_type=jax.ShapeDtypeStruct((n, width), table.dtype), mesh=mesh,
        scratch_types=[pltpu.VMEM((SC_GATHER_WINDOW,), jnp.int32),
                       pltpu.VMEM((SC_GATHER_WINDOW, width), table.dtype),
                       pltpu.SemaphoreType.DMA],
        name="moe_sc_gather")
    def gather_kernel(table_hbm, idx_hbm, out_hbm, idx_vmem, rows_vmem, sem):
        worker = lax.axis_index("subcore") * SC_NUM_CORES + lax.axis_index("core")

        @pl.loop(0, -(-n_windows // workers))
        def _(j):
            window = j * workers + worker

            @pl.when(window < n_windows)
            def _():
                off = window * SC_GATHER_WINDOW
                pltpu.sync_copy(idx_hbm.at[pl.ds(off, SC_GATHER_WINDOW)], idx_vmem)
                pltpu.async_copy(table_hbm.at[idx_vmem], rows_vmem, sem).wait()
                pltpu.sync_copy(rows_vmem, out_hbm.at[pl.ds(off, SC_GATHER_WINDOW)])

    return gather_kernel(table, indices)


def _combine_kernel(*refs, alpha):
    y_refs = refs[:TOP_K]
    x_ref, fsh_ref, gate_ref, mod_ref, lng_ref, lnb_ref = refs[TOP_K:TOP_K + 6]
    o_ref = refs[-1]
    d = D_MODEL
    t = x_ref.shape[0]
    gate_rows = gate_ref[...]
    pad = jnp.zeros((LANES - TOP_K, t), F32)
    gate_cols = jnp.concatenate([gate_rows, pad], axis=0).T
    f_lo = fsh_ref[:, 0:PACK_W]
    f_hi = fsh_ref[:, PACK_W:]
    for k in range(TOP_K):
        y_lo, y_hi = _unpack_bf16_pairs(y_refs[k][...])
        f_lo = f_lo + gate_cols[:, k:k + 1] * y_lo
        f_hi = f_hi + gate_cols[:, k:k + 1] * y_hi
    f = jnp.concatenate([f_lo, f_hi], axis=1)
    z = alpha * x_ref[...] + mod_ref[:, 5 * d:6 * d] * f
    o_ref[...] = _layer_norm_rows(z) * lng_ref[...] + lnb_ref[...]


def _combine_call(y_tok, x, fsh, gate, mod3, ln_g, ln_b, prev_out, *, batch_index, tiles_per_batch, alpha,
                  drop_context):
    r, d = x.shape
    t = ROW_TILE
    n_batches = r // t // tiles_per_batch
    n_tiles = tiles_per_batch - 1 if drop_context else tiles_per_batch
    tile0 = batch_index * tiles_per_batch
    row = lambda i: (tile0 + i, 0)
    col = lambda i: (0, tile0 + i)
    const = lambda i: (0, 0)
    kern = functools.partial(_combine_kernel, alpha=alpha)
    y_specs = [pl.BlockSpec((t, PACK_W), functools.partial(lambda k, i: (k * tiles_per_batch + i, 0), k))
               for k in range(TOP_K)]
    in_specs = y_specs + [
        pl.BlockSpec((t, d), row),
        pl.BlockSpec((t, d), row),
        pl.BlockSpec((TOP_K, t), col),
        pl.BlockSpec((None, 1, 6 * d), lambda i: (_mod_row(tile0 + i, tiles_per_batch), 0, 0)),
        pl.BlockSpec((1, d), const),
        pl.BlockSpec((1, d), const),
    ]
    args = [y_tok] * TOP_K + [x, fsh, gate, mod3, ln_g, ln_b]
    aliases = {}
    if prev_out is not None:
        in_specs.append(pl.BlockSpec(memory_space=pl.ANY))
        args.append(prev_out)
        aliases = {len(args) - 1: 0}
    return pl.pallas_call(
        kern,
        grid=(n_tiles,),
        in_specs=in_specs,
        out_specs=pl.BlockSpec((t, d), lambda i: (batch_index * n_tiles + i, 0)),
        out_shape=jax.ShapeDtypeStruct((n_batches * n_tiles * t, d), F32),
        input_output_aliases=aliases,
        compiler_params=_cparams("arbitrary"),
        name="moe_combine",
    )(*args)


def _rope_tables(seq):
    rows = seq // GRID_W
    row = jnp.repeat(jnp.arange(rows, dtype=F32), GRID_W)
    col = jnp.tile(jnp.arange(GRID_W, dtype=F32), rows)
    nf = DA_DIM // 4
    freqs = ROPE_BASE ** (-jnp.arange(nf, dtype=F32) / nf)
    cr, sr = jnp.cos(row[:, None] * freqs), jnp.sin(row[:, None] * freqs)
    cc, sc = jnp.cos(col[:, None] * freqs), jnp.sin(col[:, None] * freqs)
    c64 = jnp.concatenate([cr, cr, cc, cc], axis=1)
    s64 = jnp.concatenate([-sr, sr, -sc, sc], axis=1)
    c = jnp.concatenate([jnp.tile(c64, (1, 2)), jnp.ones((CTX_LEN, LANES), F32)], axis=0)
    s = jnp.concatenate([jnp.tile(s64, (1, 2)), jnp.zeros((CTX_LEN, LANES), F32)], axis=0)
    return c, s


def kernel(x, c, ctx, c_ctx, w_mod, b_mod, w_in, w_out, diff_lambda, pool_w, pool_scale, ret_log_decay, ln_g, ln_b,
           w_router, router_bias, w_expert_gate_up, w_expert_down, w_shared_gate_up, w_shared_down):
    batch, seq, d = x.shape
    depth = w_mod.shape[0]
    assert d == D_MODEL and ctx.shape[1] == CTX_LEN == ROW_TILE and batch == 2
    assert seq % ROW_TILE == 0 and seq % GRID_W == 0 and w_in.shape[-1] == IN_WIDTH
    rows_per_batch = seq + CTX_LEN
    tiles_per_batch = rows_per_batch // ROW_TILE
    r = batch * rows_per_batch
    alpha = (2.0 * depth) ** 0.25

    xa = jnp.concatenate([x, ctx], axis=1).reshape(r, d)
    cvec = jnp.zeros((8, d), F32).at[0:batch].set(c).at[batch].set(c_ctx)
    mod_all = _mod_call(cvec, w_mod, b_mod)
    rope_c, rope_s = _rope_tables(seq)

    n_sorted = r * TOP_K + N_EXPERTS * EXPERT_BLOCK
    n_blocks = n_sorted // EXPERT_BLOCK

    for l in range(depth):
        lambda_init = 0.8 - 0.6 * math.exp(-0.3 * l)
        mod3 = mod_all[l].reshape(8, 1, 6 * d)
        lng = ln_g[l].reshape(2, 1, d)
        lnb = ln_b[l].reshape(2, 1, d)

        w_in_bf = w_in[l].astype(BF16)
        w_vt_bf = w_in_bf[:, QK_WIDTH:QK_WIDTH + DA_WIDTH].T
        qk, vda, u, rqkv, rg = _inproj_call(xa, mod3, w_in_bf, w_vt_bf, rope_c, rope_s, tiles_per_batch)
        da = _attn_call(diff_lambda[l], qk, vda, batch=batch, rows_per_batch=rows_per_batch, seq=seq,
                        lambda_init=lambda_init)
        o_f, o_b = _ret_call(ret_log_decay[l], rqkv, batch=batch, rows_per_batch=rows_per_batch, seq=seq)
        pool_bd = jnp.zeros((POOL_WIDTH, POOL_WIDTH), F32)
        for gi in range(len(POOL_WINDOWS)):
            sl = slice(gi * POOL_GROUP, (gi + 1) * POOL_GROUP)
            pool_bd = pool_bd.at[sl, sl].set(pool_w[l, gi])
        xa = _mixout_call(xa, da, u, o_f, o_b, rg, mod3, w_out[l].astype(BF16), pool_bd.astype(BF16),
                          pool_scale[l].reshape(1, POOL_WIDTH), lng[0], lnb[0],
                          tiles_per_batch=tiles_per_batch, seq=seq, alpha=alpha)

        wr_t = w_router[l].T
        wr_hi = wr_t.astype(BF16)
        wr_lo = (wr_t - wr_hi.astype(F32)).astype(BF16)
        tokp, idx, gate, rank, cnt, fsh = _router_call(
            xa, mod3, wr_hi, wr_lo, router_bias[l].reshape(N_EXPERTS, 1),
            w_shared_gate_up[l].astype(BF16), w_shared_down[l].astype(BF16), tiles_per_batch=tiles_per_batch)
        counts = cnt[:, 0]
        padded = (counts + EXPERT_BLOCK - 1) // EXPERT_BLOCK * EXPERT_BLOCK
        pad_end = jnp.cumsum(padded)
        offs = pad_end - padded
        expert_ids = jnp.arange(N_EXPERTS, dtype=jnp.int32)
        blk_row = jnp.arange(n_blocks, dtype=jnp.int32) * EXPERT_BLOCK
        block_expert = jnp.minimum(jnp.sum(pad_end[None, :] <= blk_row[:, None], axis=1), N_EXPERTS - 1)
        n_used = pad_end[-1:] // EXPERT_BLOCK
        used = counts > 0
        ordinal = jnp.cumsum(used) - 1
        hit = used[None, :] & (ordinal[None, :] == expert_ids[:, None])
        used_expert = jnp.sum(jnp.where(hit, expert_ids[None, :], 0), axis=1)
        n_used_experts = jnp.sum(used)[None]
        block_ordinal = ordinal[block_expert]
        slot = jnp.arange(EXPERT_BLOCK, dtype=jnp.int32)[None, :]
        first_pad = (padded - EXPERT_BLOCK)[:, None] + slot
        is_pad = (first_pad >= counts[:, None]) & (padded[:, None] > 0)
        spare = n_sorted + jnp.arange(N_EXPERTS * EXPERT_BLOCK, dtype=jnp.int32).reshape(N_EXPERTS, EXPERT_BLOCK)
        pad_rows = jnp.where(is_pad, offs[:, None] + first_pad, spare).reshape(N_EXPERTS * EXPERT_BLOCK)
        i32 = lambda a: a.astype(jnp.int32)

        dest = _dest_call(idx, rank, i32(offs).reshape(N_EXPERTS, 1))
        dest_flat = dest.reshape(TOP_K * r)
        xs = _sc_dispatch(tokp, dest_flat, i32(pad_rows), n_sorted)
        ys = _expert_call(i32(block_expert), i32(n_used), i32(block_ordinal), i32(used_expert),
                          i32(n_used_experts), xs, w_expert_gate_up, w_expert_down, l)
        x_new = None
        for b in range(batch):
            dest_b = dest[:, b * rows_per_batch:(b + 1) * rows_per_batch].reshape(TOP_K * rows_per_batch)
            y_tok = _sc_gather_rows(ys, dest_b)
            x_new = _combine_call(y_tok, xa, fsh, gate, mod3, lng[1], lnb[1], x_new, batch_index=b,
                                  tiles_per_batch=tiles_per_batch, alpha=alpha, drop_context=(l == depth - 1))
        xa = x_new

    return xa.reshape(batch, seq, d)
```
